```python
import math
import jax, jax.numpy as jnp
from jax import lax
import numpy as np

D_MODEL = 1024
BATCH = 8
SEQ = 4096
DEPTH = 1

MEM_LEN = 256
MLA_HEADS = 8
MLA_NOPE = 64
MLA_ROPE = 32
MLA_V = 64
MLA_Q_LORA = 256
MLA_KV_LORA = 128
MLA_WIDTH = MLA_HEADS * MLA_V
SB_HEADS = 8
SB_HEAD_DIM = 64
SB_WIDTH = SB_HEADS * SB_HEAD_DIM
MEM_HEADS = 4
MEM_HEAD_DIM = 128
MEM_WIDTH = MEM_HEADS * MEM_HEAD_DIM
N_BRANCHES = 3

BLOCK_Q = 128
ROPE_BASE = 10000.0
RMS_EPS = 1e-6
LN_EPS = 1e-5
DEEPNORM_ALPHA = (2.0 * DEPTH) ** 0.25
DEEPNORM_BETA = (8.0 * DEPTH) ** -0.25

IN_SIZES = [
    MLA_Q_LORA, MLA_KV_LORA, MLA_ROPE, MLA_WIDTH,
    SB_WIDTH, SB_WIDTH, SB_WIDTH, SB_WIDTH,
    MEM_WIDTH, MEM_WIDTH,
]
IN_WIDTH = int(sum(IN_SIZES))
IN_OFFSETS = [int(o) for o in np.cumsum(IN_SIZES)[:-1]]

kernel_name = "hybrid_mla_stickbreaking_memxattn_deepnorm"


def _rms_norm(x, g):
    x32 = x.astype(jnp.float32)
    y = x32 * lax.rsqrt(jnp.mean(x32 * x32, axis=-1, keepdims=True) + RMS_EPS)
    return (y * g.astype(jnp.float32)).astype(x.dtype)


def _layer_norm(x, g, b):
    x32 = x.astype(jnp.float32)
    mu = jnp.mean(x32, axis=-1, keepdims=True)
    xc = x32 - mu
    var = jnp.mean(xc * xc, axis=-1, keepdims=True)
    y = xc * lax.rsqrt(var + LN_EPS) * g.astype(jnp.float32) + b.astype(jnp.float32)
    return y.astype(x.dtype)


def _rope(x, pos):
    half = x.shape[-1] // 2
    freqs = ROPE_BASE ** (-jnp.arange(half, dtype=jnp.float32) / half)
    ang = pos.astype(jnp.float32)[:, None] * freqs[None, :]
    cos = jnp.cos(ang)[None, :, None, :]
    sin = jnp.sin(ang)[None, :, None, :]
    x32 = x.astype(jnp.float32)
    x1, x2 = x32[..., :half], x32[..., half:]
    out = jnp.concatenate([x1 * cos - x2 * sin, x1 * sin + x2 * cos], axis=-1)
    return out.astype(x.dtype)


def _sweep_query_blocks(q, k, v, weights_fn):
    seq = q.shape[1]
    outs = []
    for start in range(0, seq, BLOCK_Q):
        end = start + BLOCK_Q
        scores = jnp.einsum('bqhd,bkhd->bhqk', q[:, start:end], k[:, :end]).astype(jnp.float32)
        q_pos = (start + jnp.arange(BLOCK_Q))[:, None]
        k_pos = jnp.arange(end)[None, :]
        w = weights_fn(scores, q_pos, k_pos)
        outs.append(jnp.einsum('bhqk,bkhd->bqhd', w.astype(v.dtype), v[:, :end]))
    return jnp.concatenate(outs, axis=1)


def _softmax_causal_weights(scale):
    def fn(scores, q_pos, k_pos):
        s = jnp.where(k_pos <= q_pos, scores * scale, jnp.finfo(jnp.float32).min)
        return jax.nn.softmax(s, axis=-1)
    return fn


def _stick_breaking_weights(scale):
    def fn(scores, q_pos, k_pos):
        z = scores * scale
        strict = k_pos < q_pos
        log_beta = jax.nn.log_sigmoid(z)
        log_keep = jnp.where(strict, jax.nn.log_sigmoid(-z), 0.0)
        after = lax.cumsum(log_keep, axis=log_keep.ndim - 1, reverse=True) - log_keep
        return jnp.where(strict, jnp.exp(log_beta + after), 0.0)
    return fn


def _hybrid_layer(x, mem, w_in, w_mem_kv, q_a_gain, w_q_b, kv_a_gain, w_kv_b,
                  w_branch_mla, w_branch_sb, w_branch_mem, w_merge_gate, b_merge_gate,
                  w_out, ln_gain, ln_bias):
    b, s, _ = x.shape
    pos = jnp.arange(s, dtype=jnp.int32)
    proj = x @ w_in
    (c_q, c_kv, k_rope, gate_a, q_b, k_b, v_b, gate_b, q_m, gate_m) = jnp.split(proj, IN_OFFSETS, axis=-1)

    q_a = (_rms_norm(c_q, q_a_gain) @ w_q_b).reshape(b, s, MLA_HEADS, MLA_NOPE + MLA_ROPE)
    kv_a = (_rms_norm(c_kv, kv_a_gain) @ w_kv_b).reshape(b, s, MLA_HEADS, MLA_NOPE + MLA_V)
    q_nope, q_pe = q_a[..., :MLA_NOPE], q_a[..., MLA_NOPE:]
    k_nope, v_a = kv_a[..., :MLA_NOPE], kv_a[..., MLA_NOPE:]
    k_pe = _rope(k_rope.reshape(b, s, 1, MLA_ROPE), pos)
    q_full = jnp.concatenate([q_nope, _rope(q_pe, pos)], axis=-1)
    k_full = jnp.concatenate([k_nope, jnp.broadcast_to(k_pe, (b, s, MLA_HEADS, MLA_ROPE))], axis=-1)
    o_a = _sweep_query_blocks(q_full, k_full, v_a,
                              _softmax_causal_weights(1.0 / math.sqrt(MLA_NOPE + MLA_ROPE)))
    y_a = (o_a.reshape(b, s, MLA_WIDTH) * jax.nn.silu(gate_a)) @ w_branch_mla

    q_sb = q_b.reshape(b, s, SB_HEADS, SB_HEAD_DIM)
    k_sb = k_b.reshape(b, s, SB_HEADS, SB_HEAD_DIM)
    v_sb = v_b.reshape(b, s, SB_HEADS, SB_HEAD_DIM)
    o_b = _sweep_query_blocks(q_sb, k_sb, v_sb, _stick_breaking_weights(1.0 / math.sqrt(SB_HEAD_DIM)))
    y_b = (o_b.reshape(b, s, SB_WIDTH) * jax.nn.silu(gate_b)) @ w_branch_sb

    mem_kv = (mem @ w_mem_kv).reshape(b, mem.shape[1], 2, MEM_HEADS, MEM_HEAD_DIM)
    k_m, v_m = mem_kv[:, :, 0], mem_kv[:, :, 1]
    q_mh = q_m.reshape(b, s, MEM_HEADS, MEM_HEAD_DIM)
    sc = jnp.einsum('bshd,bmhd->bhsm', q_mh, k_m).astype(jnp.float32) / math.sqrt(MEM_HEAD_DIM)
    p_m = jax.nn.softmax(sc, axis=-1).astype(v_m.dtype)
    o_m = jnp.einsum('bhsm,bmhd->bshd', p_m, v_m).reshape(b, s, MEM_WIDTH)
    y_m = (o_m * jax.nn.silu(gate_m)) @ w_branch_mem

    g = jax.nn.sigmoid(x @ w_merge_gate + b_merge_gate)
    g_a, g_b, g_m = jnp.split(g, N_BRANCHES, axis=-1)
    merged = g_a * y_a + g_b * y_b + g_m * y_m
    out = merged @ w_out

    return _layer_norm(DEEPNORM_ALPHA * x + out, ln_gain, ln_bias)


def _fwd_setup_inputs(seed: int = 0) -> dict:
    key = jax.random.key(seed)
    ks = jax.random.split(key, 18)
    f32 = jnp.float32

    def nrm(k, shape, fan_in, gain=1.0):
        return jax.random.normal(k, shape, f32) * (gain * fan_in ** -0.5)

    L = DEPTH
    return {
        "x": jax.random.normal(ks[0], (BATCH, SEQ, D_MODEL), f32),
        "mem": jax.random.normal(ks[1], (BATCH, MEM_LEN, D_MODEL), f32),
        "w_in": nrm(ks[2], (L, D_MODEL, IN_WIDTH), D_MODEL),
        "w_mem_kv": nrm(ks[3], (L, D_MODEL, 2 * MEM_WIDTH), D_MODEL),
        "q_a_gain": 1.0 + 0.01 * jax.random.normal(ks[4], (L, MLA_Q_LORA), f32),
        "w_q_b": nrm(ks[5], (L, MLA_Q_LORA, MLA_HEADS * (MLA_NOPE + MLA_ROPE)), MLA_Q_LORA),
        "kv_a_gain": 1.0 + 0.01 * jax.random.normal(ks[6], (L, MLA_KV_LORA), f32),
        "w_kv_b": nrm(ks[7], (L, MLA_KV_LORA, MLA_HEADS * (MLA_NOPE + MLA_V)), MLA_KV_LORA),
        "w_branch_mla": nrm(ks[8], (L, MLA_WIDTH, D_MODEL), MLA_WIDTH, DEEPNORM_BETA),
        "w_branch_sb": nrm(ks[9], (L, SB_WIDTH, D_MODEL), SB_WIDTH, DEEPNORM_BETA),
        "w_branch_mem": nrm(ks[10], (L, MEM_WIDTH, D_MODEL), MEM_WIDTH, DEEPNORM_BETA),
        "w_merge_gate": nrm(ks[11], (L, D_MODEL, N_BRANCHES * D_MODEL), D_MODEL),
        "b_merge_gate": 0.01 * jax.random.normal(ks[12], (L, N_BRANCHES * D_MODEL), f32),
        "w_out": nrm(ks[13], (L, D_MODEL, D_MODEL), D_MODEL, DEEPNORM_BETA),
        "ln_gain": 1.0 + 0.01 * jax.random.normal(ks[14], (L, D_MODEL), f32),
        "ln_bias": 0.01 * jax.random.normal(ks[15], (L, D_MODEL), f32),
    }


def _fwd_reference(x, mem, w_in, w_mem_kv, q_a_gain, w_q_b, kv_a_gain, w_kv_b,
              w_branch_mla, w_branch_sb, w_branch_mem, w_merge_gate, b_merge_gate,
              w_out, ln_gain, ln_bias):
    h = x
    for l in range(DEPTH):
        h = _hybrid_layer(h, mem, w_in[l], w_mem_kv[l], q_a_gain[l], w_q_b[l], kv_a_gain[l], w_kv_b[l],
                          w_branch_mla[l], w_branch_sb[l], w_branch_mem[l], w_merge_gate[l],
                          b_merge_gate[l], w_out[l], ln_gain[l], ln_bias[l])
    return h


import jax as _jax
import jax.numpy as _jnp

TWIN_FORMAT = 'train_step'
FWD_PARAMS = ['x', 'mem', 'w_in', 'w_mem_kv', 'q_a_gain', 'w_q_b', 'kv_a_gain', 'w_kv_b', 'w_branch_mla', 'w_branch_sb', 'w_branch_mem', 'w_merge_gate', 'b_merge_gate', 'w_out', 'ln_gain', 'ln_bias']
TWIN_WEIGHTS = ['w_in', 'w_mem_kv', 'q_a_gain', 'w_q_b', 'kv_a_gain', 'w_kv_b', 'w_branch_mla', 'w_branch_sb', 'w_branch_mem', 'w_merge_gate', 'b_merge_gate', 'w_out', 'ln_gain', 'ln_bias']
TWIN_DIFF_INPUT = 'x'
TWIN_INPUTS = ['x', 'mem', 'w_in', 'w_mem_kv', 'q_a_gain', 'w_q_b', 'kv_a_gain', 'w_kv_b', 'w_branch_mla', 'w_branch_sb', 'w_branch_mem', 'w_merge_gate', 'b_merge_gate', 'w_out', 'ln_gain', 'ln_bias', 'loss_target', 'm_w_in', 'm_w_mem_kv', 'm_q_a_gain', 'm_w_q_b', 'm_kv_a_gain', 'm_w_kv_b', 'm_w_branch_mla', 'm_w_branch_sb', 'm_w_branch_mem', 'm_w_merge_gate', 'm_b_merge_gate', 'm_w_out', 'm_ln_gain', 'm_ln_bias', 'v_w_in', 'v_w_mem_kv', 'v_q_a_gain', 'v_w_q_b', 'v_kv_a_gain', 'v_w_kv_b', 'v_w_branch_mla', 'v_w_branch_sb', 'v_w_branch_mem', 'v_w_merge_gate', 'v_b_merge_gate', 'v_w_out', 'v_ln_gain', 'v_ln_bias']
TWIN_OUTPUTS = ['loss', 'grad_x', 'grad_w_in', 'grad_w_mem_kv', 'grad_q_a_gain', 'grad_w_q_b', 'grad_kv_a_gain', 'grad_w_kv_b', 'grad_w_branch_mla', 'grad_w_branch_sb', 'grad_w_branch_mem', 'grad_w_merge_gate', 'grad_b_merge_gate', 'grad_w_out', 'grad_ln_gain', 'grad_ln_bias', 'delta_w_in', 'delta_w_mem_kv', 'delta_q_a_gain', 'delta_w_q_b', 'delta_kv_a_gain', 'delta_w_kv_b', 'delta_w_branch_mla', 'delta_w_branch_sb', 'delta_w_branch_mem', 'delta_w_merge_gate', 'delta_b_merge_gate', 'delta_w_out', 'delta_ln_gain', 'delta_ln_bias', 'new_m_w_in', 'new_m_w_mem_kv', 'new_m_q_a_gain', 'new_m_w_q_b', 'new_m_kv_a_gain', 'new_m_w_kv_b', 'new_m_w_branch_mla', 'new_m_w_branch_sb', 'new_m_w_branch_mem', 'new_m_w_merge_gate', 'new_m_b_merge_gate', 'new_m_w_out', 'new_m_ln_gain', 'new_m_ln_bias', 'new_v_w_in', 'new_v_w_mem_kv', 'new_v_q_a_gain', 'new_v_w_q_b', 'new_v_kv_a_gain', 'new_v_w_kv_b', 'new_v_w_branch_mla', 'new_v_w_branch_sb', 'new_v_w_branch_mem', 'new_v_w_merge_gate', 'new_v_b_merge_gate', 'new_v_w_out', 'new_v_ln_gain', 'new_v_ln_bias']
TWIN_LEAF_KINDS = {'loss': 'loss', 'grad_x': 'grad_x', 'grad_w_in': 'grad_w', 'grad_w_mem_kv': 'grad_w', 'grad_q_a_gain': 'grad_w', 'grad_w_q_b': 'grad_w', 'grad_kv_a_gain': 'grad_w', 'grad_w_kv_b': 'grad_w', 'grad_w_branch_mla': 'grad_w', 'grad_w_branch_sb': 'grad_w', 'grad_w_branch_mem': 'grad_w', 'grad_w_merge_gate': 'grad_w', 'grad_b_merge_gate': 'grad_w', 'grad_w_out': 'grad_w', 'grad_ln_gain': 'grad_w', 'grad_ln_bias': 'grad_w', 'delta_w_in': 'delta_w', 'delta_w_mem_kv': 'delta_w', 'delta_q_a_gain': 'delta_w', 'delta_w_q_b': 'delta_w', 'delta_kv_a_gain': 'delta_w', 'delta_w_kv_b': 'delta_w', 'delta_w_branch_mla': 'delta_w', 'delta_w_branch_sb': 'delta_w', 'delta_w_branch_mem': 'delta_w', 'delta_w_merge_gate': 'delta_w', 'delta_b_merge_gate': 'delta_w', 'delta_w_out': 'delta_w', 'delta_ln_gain': 'delta_w', 'delta_ln_bias': 'delta_w', 'new_m_w_in': 'new_m', 'new_m_w_mem_kv': 'new_m', 'new_m_q_a_gain': 'new_m', 'new_m_w_q_b': 'new_m', 'new_m_kv_a_gain': 'new_m', 'new_m_w_kv_b': 'new_m', 'new_m_w_branch_mla': 'new_m', 'new_m_w_branch_sb': 'new_m', 'new_m_w_branch_mem': 'new_m', 'new_m_w_merge_gate': 'new_m', 'new_m_b_merge_gate': 'new_m', 'new_m_w_out': 'new_m', 'new_m_ln_gain': 'new_m', 'new_m_ln_bias': 'new_m', 'new_v_w_in': 'new_v', 'new_v_w_mem_kv': 'new_v', 'new_v_q_a_gain': 'new_v', 'new_v_w_q_b': 'new_v', 'new_v_kv_a_gain': 'new_v', 'new_v_w_kv_b': 'new_v', 'new_v_w_branch_mla': 'new_v', 'new_v_w_branch_sb': 'new_v', 'new_v_w_branch_mem': 'new_v', 'new_v_w_merge_gate': 'new_v', 'new_v_b_merge_gate': 'new_v', 'new_v_w_out': 'new_v', 'new_v_ln_gain': 'new_v', 'new_v_ln_bias': 'new_v'}


def _forward(args):
    return _fwd_reference(*[args[k] for k in FWD_PARAMS])


def _output_shape():
    out = _jax.eval_shape(lambda: _forward(_fwd_setup_inputs(0)))
    return out.shape, out.dtype

N_MICROBATCH = 1
ADAM_LR = 0.001
ADAM_B1 = 0.9
ADAM_B2 = 0.999
ADAM_EPS = 1e-08
ADAM_WD = 0.01
ADAM_STEP = 10
PER_EXAMPLE_BATCH_AXIS = {'x': 0, 'mem': 0, 'loss_target': 0}
SHARED_INPUTS = []
_WEIGHT_DTYPES = {'w_in': _jnp.float32, 'w_mem_kv': _jnp.float32, 'q_a_gain': _jnp.float32, 'w_q_b': _jnp.float32, 'kv_a_gain': _jnp.float32, 'w_kv_b': _jnp.float32, 'w_branch_mla': _jnp.float32, 'w_branch_sb': _jnp.float32, 'w_branch_mem': _jnp.float32, 'w_merge_gate': _jnp.float32, 'b_merge_gate': _jnp.float32, 'w_out': _jnp.float32, 'ln_gain': _jnp.float32, 'ln_bias': _jnp.float32}
MOMENT_SCALE = {'w_in': 9.511103e-03, 'w_mem_kv': 2.532110e-03, 'q_a_gain': 6.411306e-03, 'w_q_b': 3.675956e-03, 'kv_a_gain': 1.291807e-02, 'w_kv_b': 4.627098e-03, 'w_branch_mla': 6.445538e-03, 'w_branch_sb': 1.843224e-02, 'w_branch_mem': 3.015538e-03, 'w_merge_gate': 2.647646e-03, 'b_merge_gate': 2.707909e-03, 'w_out': 1.970843e-02, 'ln_gain': 3.197588e+01, 'ln_bias': 3.020128e-01}


def _to_microbatches(a, axis):
    t = _jnp.moveaxis(a, axis, 0)
    t = t.reshape((N_MICROBATCH, t.shape[0] // N_MICROBATCH) + t.shape[1:])
    return _jnp.moveaxis(t, 1, axis + 1)


def setup_inputs(seed: int = 0) -> dict:
    inp = _fwd_setup_inputs(seed)
    key = _jax.random.fold_in(_jax.random.key(seed), 7919)
    shape, _ = _output_shape()
    out = dict(inp)
    out["loss_target"] = _jax.random.normal(_jax.random.fold_in(key, 0), shape, _jnp.float32)
    for i, name in enumerate(TWIN_WEIGHTS):
        w = inp[name].astype(_jnp.float32)
        if MOMENT_SCALE is None:
            s = _jnp.sqrt(_jnp.mean(_jnp.square(w)) + 1e-30)
        else:
            s = MOMENT_SCALE[name]
        km, kv = _jax.random.split(_jax.random.fold_in(key, i + 1))
        out[name] = w
        out["m_" + name] = s * _jax.random.normal(km, w.shape, _jnp.float32)
        out["v_" + name] = (s * s) * _jax.random.uniform(kv, w.shape, _jnp.float32, 0.5, 1.5)
    if N_MICROBATCH > 1:
        for name, axis in PER_EXAMPLE_BATCH_AXIS.items():
            out[name] = _to_microbatches(out[name], axis)
    return {'x': out['x'], 'mem': out['mem'], 'w_in': out['w_in'], 'w_mem_kv': out['w_mem_kv'], 'q_a_gain': out['q_a_gain'], 'w_q_b': out['w_q_b'], 'kv_a_gain': out['kv_a_gain'], 'w_kv_b': out['w_kv_b'], 'w_branch_mla': out['w_branch_mla'], 'w_branch_sb': out['w_branch_sb'], 'w_branch_mem': out['w_branch_mem'], 'w_merge_gate': out['w_merge_gate'], 'b_merge_gate': out['b_merge_gate'], 'w_out': out['w_out'], 'ln_gain': out['ln_gain'], 'ln_bias': out['ln_bias'], 'loss_target': out['loss_target'], 'm_w_in': out['m_w_in'], 'm_w_mem_kv': out['m_w_mem_kv'], 'm_q_a_gain': out['m_q_a_gain'], 'm_w_q_b': out['m_w_q_b'], 'm_kv_a_gain': out['m_kv_a_gain'], 'm_w_kv_b': out['m_w_kv_b'], 'm_w_branch_mla': out['m_w_branch_mla'], 'm_w_branch_sb': out['m_w_branch_sb'], 'm_w_branch_mem': out['m_w_branch_mem'], 'm_w_merge_gate': out['m_w_merge_gate'], 'm_b_merge_gate': out['m_b_merge_gate'], 'm_w_out': out['m_w_out'], 'm_ln_gain': out['m_ln_gain'], 'm_ln_bias': out['m_ln_bias'], 'v_w_in': out['v_w_in'], 'v_w_mem_kv': out['v_w_mem_kv'], 'v_q_a_gain': out['v_q_a_gain'], 'v_w_q_b': out['v_w_q_b'], 'v_kv_a_gain': out['v_kv_a_gain'], 'v_w_kv_b': out['v_w_kv_b'], 'v_w_branch_mla': out['v_w_branch_mla'], 'v_w_branch_sb': out['v_w_branch_sb'], 'v_w_branch_mem': out['v_w_branch_mem'], 'v_w_merge_gate': out['v_w_merge_gate'], 'v_b_merge_gate': out['v_b_merge_gate'], 'v_w_out': out['v_w_out'], 'v_ln_gain': out['v_ln_gain'], 'v_ln_bias': out['v_ln_bias']}


def _loss(weights, diff, rest, loss_target):
    with _jax.named_scope("forward"):
        args = {**rest, TWIN_DIFF_INPUT: diff, **{k: w.astype(_WEIGHT_DTYPES[k]) for k, w in weights.items()}}
        y = _forward(args)
    with _jax.named_scope("loss_head"):
        err = _jnp.square(y.astype(_jnp.float32) - loss_target)
        return 0.5 * _jnp.sum(_jnp.mean(err, axis=-1)) if err.ndim else 0.5 * err


def _adamw(w, g, m, v):
    m = ADAM_B1 * m + (1.0 - ADAM_B1) * g
    v = ADAM_B2 * v + (1.0 - ADAM_B2) * _jnp.square(g)
    m_hat = m / (1.0 - ADAM_B1 ** ADAM_STEP)
    v_hat = v / (1.0 - ADAM_B2 ** ADAM_STEP)
    delta = -ADAM_LR * (m_hat / (_jnp.sqrt(v_hat) + ADAM_EPS) + ADAM_WD * w)
    return delta, m, v


def reference(x, mem, w_in, w_mem_kv, q_a_gain, w_q_b, kv_a_gain, w_kv_b, w_branch_mla, w_branch_sb, w_branch_mem, w_merge_gate, b_merge_gate, w_out, ln_gain, ln_bias, loss_target, m_w_in, m_w_mem_kv, m_q_a_gain, m_w_q_b, m_kv_a_gain, m_w_kv_b, m_w_branch_mla, m_w_branch_sb, m_w_branch_mem, m_w_merge_gate, m_b_merge_gate, m_w_out, m_ln_gain, m_ln_bias, v_w_in, v_w_mem_kv, v_q_a_gain, v_w_q_b, v_kv_a_gain, v_w_kv_b, v_w_branch_mla, v_w_branch_sb, v_w_branch_mem, v_w_merge_gate, v_b_merge_gate, v_w_out, v_ln_gain, v_ln_bias):
    given = dict(x=x, mem=mem, w_in=w_in, w_mem_kv=w_mem_kv, q_a_gain=q_a_gain, w_q_b=w_q_b, kv_a_gain=kv_a_gain, w_kv_b=w_kv_b, w_branch_mla=w_branch_mla, w_branch_sb=w_branch_sb, w_branch_mem=w_branch_mem, w_merge_gate=w_merge_gate, b_merge_gate=b_merge_gate, w_out=w_out, ln_gain=ln_gain, ln_bias=ln_bias, loss_target=loss_target, m_w_in=m_w_in, m_w_mem_kv=m_w_mem_kv, m_q_a_gain=m_q_a_gain, m_w_q_b=m_w_q_b, m_kv_a_gain=m_kv_a_gain, m_w_kv_b=m_w_kv_b, m_w_branch_mla=m_w_branch_mla, m_w_branch_sb=m_w_branch_sb, m_w_branch_mem=m_w_branch_mem, m_w_merge_gate=m_w_merge_gate, m_b_merge_gate=m_b_merge_gate, m_w_out=m_w_out, m_ln_gain=m_ln_gain, m_ln_bias=m_ln_bias, v_w_in=v_w_in, v_w_mem_kv=v_w_mem_kv, v_q_a_gain=v_q_a_gain, v_w_q_b=v_w_q_b, v_kv_a_gain=v_kv_a_gain, v_w_kv_b=v_w_kv_b, v_w_branch_mla=v_w_branch_mla, v_w_branch_sb=v_w_branch_sb, v_w_branch_mem=v_w_branch_mem, v_w_merge_gate=v_w_merge_gate, v_b_merge_gate=v_b_merge_gate, v_w_out=v_w_out, v_ln_gain=v_ln_gain, v_ln_bias=v_ln_bias)
    weights = {n: given[n] for n in TWIN_WEIGHTS}
    shared = {n: given[n] for n in SHARED_INPUTS}
    per_example = {n: given[n] for n in ['x', 'mem']}
    grad_fn = _jax.value_and_grad(_loss, argnums=(0, 1))

    def one_microbatch(ex, loss_target):
        ex = dict(ex)
        diff = ex.pop(TWIN_DIFF_INPUT)
        return grad_fn(weights, diff, {**shared, **ex}, loss_target)

    if N_MICROBATCH == 1:
        loss, (grad_w, grad_x) = one_microbatch(per_example, given["loss_target"])
    else:
        def body(carry, xs):
            loss_sum, grad_sum = carry
            l_k, (gw_k, gx_k) = one_microbatch(xs[0], xs[1])
            with _jax.named_scope("update"):
                return (loss_sum + l_k, _jax.tree.map(_jnp.add, grad_sum, gw_k)), gx_k

        init = (_jnp.zeros((), _jnp.float32), _jax.tree.map(_jnp.zeros_like, weights))
        (loss, grad_w), grad_x = _jax.lax.scan(body, init, (per_example, given["loss_target"]))
    with _jax.named_scope("update"):
        delta_w, new_m, new_v = {}, {}, {}
        for n in TWIN_WEIGHTS:
            delta_w[n], new_m[n], new_v[n] = _adamw(weights[n], grad_w[n], given["m_" + n], given["v_" + n])
    return (loss, grad_x, *[grad_w[n] for n in TWIN_WEIGHTS], *[delta_w[n] for n in TWIN_WEIGHTS],
            *[new_m[n] for n in TWIN_WEIGHTS], *[new_v[n] for n in TWIN_WEIGHTS])
```

```python
import functools
import math

import jax
import jax.numpy as jnp
from jax import lax
from jax.experimental import pallas as pl
from jax.experimental.pallas import tpu as pltpu

F32, BF16 = jnp.float32, jnp.bfloat16

N_DEV = 8
D_MODEL = 1024
MLA_HEADS, MLA_NOPE, MLA_ROPE, MLA_V = 8, 64, 32, 64
MLA_Q_LORA, MLA_KV_LORA = 256, 128
SB_HEADS, SB_HEAD_DIM = 8, 64
MEM_HEADS, MEM_HEAD_DIM = 4, 128
BRANCH_WIDTH = 512
ROPE_BASE = 10000.0
RMS_EPS = 1e-6
LN_EPS = 1e-5
DEEPNORM_ALPHA = 2.0 ** 0.25
ADAM_LR, ADAM_B1, ADAM_B2, ADAM_EPS, ADAM_WD, ADAM_STEP = 0.001, 0.9, 0.999, 1e-08, 0.01, 10

PROJ_WIDTH = 4096
COL_CQ, COL_CKV, COL_KROPE, COL_GATE_A = 0, 256, 384, 512
COL_QB, COL_KB, COL_VB, COL_GATE_B, COL_QM, COL_GATE_M = 1024, 1536, 2048, 2560, 3072, 3584
IN_REAL = 416

VMEM_LIMIT_BYTES = 56 * 1024 * 1024
NEG_BIG = -1e30

SHARDED = (
    ("w_in", (1024, 4000), 1), ("w_mem_kv", (1024, 1024), 0), ("w_q_b", (256, 768), 1), ("w_kv_b", (128, 1024), 1),
    ("w_branch_mla", (512, 1024), 1), ("w_branch_sb", (512, 1024), 1), ("w_branch_mem", (512, 1024), 1),
    ("w_merge_gate", (1024, 3072), 1), ("w_out", (1024, 1024), 0),
)
SMALL = (("q_a_gain", 256), ("kv_a_gain", 128), ("b_merge_gate", 3072), ("ln_gain", 1024), ("ln_bias", 1024))
PACK_COLS = 1024
PACK_ROWS = 1408
PACK_TILE = 128
SMALL_ROWS, SMALL_LANES = 48, 128
LOSS_INDEX = 5504


def _cparams(*sem):
    return pltpu.CompilerParams(dimension_semantics=sem or None, vmem_limit_bytes=VMEM_LIMIT_BYTES)


_DIMS = {"nn": (((1,), (0,)), ((), ())), "nt": (((1,), (1,)), ((), ())), "tn": (((0,), (0,)), ((), ()))}


def _tile(dim, want):
    if dim <= want:
        return dim
    t = want - want % 128
    while dim % t:
        t -= 128
    assert t > 0, (dim, want)
    return t


def _mm(a, b, dims, *, name, out_dtype=F32, add=None, add_scale=1.0, tm=512, tn=512, tk=512):
    if dims == "nn":
        (m, k), (k2, n) = a.shape, b.shape
    elif dims == "nt":
        (m, k), (n, k2) = a.shape, b.shape
    else:
        (k, m), (k2, n) = a.shape, b.shape
    assert k == k2, (a.shape, b.shape, dims)
    tm, tn, tk = _tile(m, tm), _tile(n, tn), _tile(k, tk)
    nk = k // tk
    a_spec = pl.BlockSpec((tk, tm), lambda i, j, kk: (kk, i)) if dims == "tn" else pl.BlockSpec((tm, tk), lambda i, j, kk: (i, kk))
    b_spec = pl.BlockSpec((tn, tk), lambda i, j, kk: (j, kk)) if dims == "nt" else pl.BlockSpec((tk, tn), lambda i, j, kk: (kk, j))
    o_spec = pl.BlockSpec((tm, tn), lambda i, j, kk: (i, j))

    def body(*refs):
        if add is None:
            a_ref, b_ref, o_ref, acc = refs
        else:
            a_ref, b_ref, add_ref, o_ref, acc = refs
        kk = pl.program_id(2)

        @pl.when(kk == 0)
        def _():
            acc[...] = jnp.zeros_like(acc)

        acc[...] += lax.dot_general(a_ref[...].astype(BF16), b_ref[...].astype(BF16), _DIMS[dims], preferred_element_type=F32)

        @pl.when(kk == nk - 1)
        def _():
            r = acc[...]
            if add is not None:
                r = r + add_scale * add_ref[...]
            o_ref[...] = r.astype(out_dtype)

    return pl.pallas_call(
        body, name=name, grid=(m // tm, n // tn, nk),
        in_specs=[a_spec, b_spec] + ([o_spec] if add is not None else []), out_specs=o_spec,
        out_shape=jax.ShapeDtypeStruct((m, n), out_dtype), scratch_shapes=[pltpu.VMEM((tm, tn), F32)],
        compiler_params=_cparams("parallel", "parallel", "arbitrary"),
    )(*((a, b) if add is None else (a, b, add)))


def _rowwise(fn, ins, outs, *, name, rows, tr=256):
    n_in = len(ins)
    in_specs, args = [], []
    for it in ins:
        arr, w, off = it if isinstance(it, tuple) else (it, it.shape[-1], 0)
        assert off % w == 0
        cb = off // w
        if arr.ndim == 3:
            in_specs.append(pl.BlockSpec((arr.shape[0], tr, w), lambda i, cb=cb: (0, i, cb)))
        elif arr.shape[0] == 1:
            in_specs.append(pl.BlockSpec((1, w), lambda i, cb=cb: (0, cb)))
        else:
            in_specs.append(pl.BlockSpec((tr, w), lambda i, cb=cb: (i, cb)))
        args.append(arr)
    out_shape, out_specs, is_sum = [], [], []
    for kind, d in outs:
        if kind == "sum":
            out_shape.append(jax.ShapeDtypeStruct((1, d), F32))
            out_specs.append(pl.BlockSpec((1, d), lambda i: (0, 0)))
            is_sum.append(True)
        else:
            out_shape.append(jax.ShapeDtypeStruct((rows, kind), d))
            out_specs.append(pl.BlockSpec((tr, kind), lambda i: (i, 0)))
            is_sum.append(False)

    def body(*refs):
        res = fn(*[r[...] for r in refs[:n_in]])
        for r, val, s in zip(refs[n_in:], res, is_sum, strict=True):
            if s:
                @pl.when(pl.program_id(0) == 0)
                def _(r=r):
                    r[...] = jnp.zeros_like(r)

                r[...] += val
            else:
                r[...] = val.astype(r.dtype)

    return pl.pallas_call(
        body, name=name, grid=(rows // tr,), in_specs=in_specs, out_specs=out_specs, out_shape=out_shape,
        compiler_params=_cparams("arbitrary"),
    )(*args)


def _colsum(v):
    return jnp.sum(v, axis=0, keepdims=True)


def _sigmoid(v):
    return 1.0 / (1.0 + jnp.exp(-v))


def _swap_halves(v, first_lane):
    lane = lax.broadcasted_iota(jnp.int32, v.shape, 1)
    return jnp.where(lane < first_lane + 16, pltpu.roll(v, 112, axis=1), pltpu.roll(v, 16, axis=1))


def _dot(a, b, dims):
    return lax.dot_general(a, b, _DIMS[dims], preferred_element_type=F32)


def _strict_lower(n):
    return lax.broadcasted_iota(jnp.int32, (n, n), 1) < lax.broadcasted_iota(jnp.int32, (n, n), 0)


def _softmax_fwd(q, k, v, *, scale, causal, name, blk):
    h, s, dk = q.shape
    sk, dv = k.shape[1], v.shape[2]
    n_kb = sk // blk

    def body(q_ref, k_ref, v_ref, o_ref, lse_ref):
        i = pl.program_id(1)
        qb = q_ref[0]

        def tile(kb, carry, masked):
            m, l, acc = carry
            off = pl.multiple_of(kb * blk, blk)
            kt, vt = k_ref[0, pl.ds(off, blk), :], v_ref[0, pl.ds(off, blk), :]
            sc = _dot(qb, kt, "nt") * scale
            if masked:
                lower = _strict_lower(blk)
                diag = lax.broadcasted_iota(jnp.int32, (blk, blk), 1) == lax.broadcasted_iota(jnp.int32, (blk, blk), 0)
                sc = jnp.where(lower | diag, sc, NEG_BIG)
            m_new = jnp.maximum(m, jnp.max(sc, axis=1, keepdims=True))
            p = jnp.exp(sc - m_new)
            alpha = jnp.exp(m - m_new)
            l = alpha * l + jnp.sum(p, axis=1, keepdims=True)
            acc = alpha * acc + _dot(p.astype(BF16), vt, "nn")
            return m_new, l, acc

        init = (jnp.full((blk, 1), NEG_BIG, F32), jnp.zeros((blk, 1), F32), jnp.zeros((blk, dv), F32))
        if causal:
            carry = lax.fori_loop(0, i, lambda kb, c: tile(kb, c, False), init)
            m, l, acc = tile(i, carry, True)
        else:
            m, l, acc = lax.fori_loop(0, n_kb, lambda kb, c: tile(kb, c, False), init)
        o_ref[0] = acc / l
        lse_ref[0] = m + jnp.log(l)

    return pl.pallas_call(
        body, name=name, grid=(h, s // blk),
        in_specs=[pl.BlockSpec((1, blk, dk), lambda hh, i: (hh, i, 0)), pl.BlockSpec((1, sk, dk), lambda hh, i: (hh, 0, 0)),
                  pl.BlockSpec((1, sk, dv), lambda hh, i: (hh, 0, 0))],
        out_specs=[pl.BlockSpec((1, blk, dv), lambda hh, i: (hh, i, 0)), pl.BlockSpec((1, blk, 1), lambda hh, i: (hh, i, 0))],
        out_shape=[jax.ShapeDtypeStruct((h, s, dv), F32), jax.ShapeDtypeStruct((h, s, 1), F32)],
        compiler_params=_cparams("parallel", "arbitrary"),
    )(q, k, v)


def _softmax_bwd(q, k, v, o, do, lse, *, scale, causal, name, blk):
    h, s, dk = q.shape
    sk, dv = k.shape[1], v.shape[2]
    n_kb = sk // blk

    def body(q_ref, k_ref, v_ref, o_ref, do_ref, lse_ref, dq_ref, dk_ref, dv_ref):
        i = pl.program_id(1)

        @pl.when(i == 0)
        def _():
            dk_ref[...] = jnp.zeros_like(dk_ref)
            dv_ref[...] = jnp.zeros_like(dv_ref)

        qb = q_ref[0]
        dob = do_ref[0]
        delta = jnp.sum(dob * o_ref[0], axis=1, keepdims=True)
        dob = dob.astype(BF16)
        lse_b = lse_ref[0]

        def tile(kb, dq, masked):
            off = pl.multiple_of(kb * blk, blk)
            kt, vt = k_ref[0, pl.ds(off, blk), :], v_ref[0, pl.ds(off, blk), :]
            p = jnp.exp(_dot(qb, kt, "nt") * scale - lse_b)
            if masked:
                lower = _strict_lower(blk)
                diag = lax.broadcasted_iota(jnp.int32, (blk, blk), 1) == lax.broadcasted_iota(jnp.int32, (blk, blk), 0)
                p = jnp.where(lower | diag, p, 0.0)
            ds = (p * (_dot(dob, vt, "nt") - delta) * scale).astype(BF16)
            dk_ref[0, pl.ds(off, blk), :] += _dot(ds, qb, "tn")
            dv_ref[0, pl.ds(off, blk), :] += _dot(p.astype(BF16), dob, "tn")
            return dq + _dot(ds, kt, "nn")

        dq0 = jnp.zeros((blk, dk), F32)
        if causal:
            dq = lax.fori_loop(0, i, lambda kb, c: tile(kb, c, False), dq0)
            dq = tile(i, dq, True)
        else:
            dq = lax.fori_loop(0, n_kb, lambda kb, c: tile(kb, c, False), dq0)
        dq_ref[0] = dq

    q_spec = lambda w: pl.BlockSpec((1, blk, w), lambda hh, i: (hh, i, 0))
    kv_spec = lambda w: pl.BlockSpec((1, sk, w), lambda hh, i: (hh, 0, 0))
    return pl.pallas_call(
        body, name=name, grid=(h, s // blk),
        in_specs=[q_spec(dk), kv_spec(dk), kv_spec(dv), q_spec(dv), q_spec(dv), q_spec(1)],
        out_specs=[q_spec(dk), kv_spec(dk), kv_spec(dv)],
        out_shape=[jax.ShapeDtypeStruct((h, s, dk), F32), jax.ShapeDtypeStruct((h, sk, dk), F32), jax.ShapeDtypeStruct((h, sk, dv), F32)],
        compiler_params=_cparams("arbitrary", "arbitrary"),
    )(q, k, v, o, do, lse)


def _log_sigmoid_pair(z):
    sp = jnp.log(1.0 + jnp.exp(-jnp.abs(z)))
    return jnp.minimum(z, 0.0) - sp, jnp.minimum(-z, 0.0) - sp


def _tri_sum(v, tri):
    hi = v.astype(BF16)
    lo = (v - hi.astype(F32)).astype(BF16)
    return _dot(hi, tri, "nn") + _dot(lo, tri, "nn")


def _tri_after(n):
    return (lax.broadcasted_iota(jnp.int32, (n, n), 0) > lax.broadcasted_iota(jnp.int32, (n, n), 1)).astype(BF16)


def _tri_before(n):
    return (lax.broadcasted_iota(jnp.int32, (n, n), 0) < lax.broadcasted_iota(jnp.int32, (n, n), 1)).astype(BF16)


def _sb_weights(qb, kt, scale, run, tri, diag):
    z = _dot(qb, kt, "nt") * scale
    log_beta, log_keep = _log_sigmoid_pair(z)
    if diag:
        strict = _strict_lower(z.shape[0])
        log_keep = jnp.where(strict, log_keep, 0.0)
    a = jnp.exp(log_beta + _tri_sum(log_keep, tri) + run)
    if diag:
        a = jnp.where(strict, a, 0.0)
    return a, log_beta, log_keep


def _sb_fwd(q, k, v, *, scale, name, blk):
    h, s, d = q.shape

    def body(q_ref, k_ref, v_ref, o_ref):
        i = pl.program_id(1)
        qb = q_ref[0]
        tri = _tri_after(blk)

        def tile(kb, carry, diag):
            run, acc = carry
            off = pl.multiple_of(kb * blk, blk)
            a, _, log_keep = _sb_weights(qb, k_ref[0, pl.ds(off, blk), :], scale, run, tri, diag)
            acc = acc + _dot(a.astype(BF16), v_ref[0, pl.ds(off, blk), :], "nn")
            return run + jnp.sum(log_keep, axis=1, keepdims=True), acc

        carry = tile(i, (jnp.zeros((blk, 1), F32), jnp.zeros((blk, d), F32)), True)
        _, acc = lax.fori_loop(0, i, lambda n, c: tile(i - 1 - n, c, False), carry)
        o_ref[0] = acc

    qs = pl.BlockSpec((1, blk, d), lambda hh, i: (hh, i, 0))
    kvs = pl.BlockSpec((1, s, d), lambda hh, i: (hh, 0, 0))
    return pl.pallas_call(
        body, name=name, grid=(h, s // blk), in_specs=[qs, kvs, kvs], out_specs=qs,
        out_shape=jax.ShapeDtypeStruct((h, s, d), F32), compiler_params=_cparams("parallel", "arbitrary"),
    )(q, k, v)


def _sb_bwd(q, k, v, do, *, scale, name, blk):
    h, s, d = q.shape
    n_blk = s // blk

    def body(q_ref, k_ref, v_ref, do_ref, dq_ref, dk_ref, dv_ref, g_s, beta_s):
        i = pl.program_id(1)

        @pl.when(i == 0)
        def _():
            dk_ref[...] = jnp.zeros_like(dk_ref)
            dv_ref[...] = jnp.zeros_like(dv_ref)

        qb = q_ref[0]
        dob = do_ref[0].astype(BF16)
        tri_after, tri_before = _tri_after(blk), _tri_before(blk)

        def sweep1(kb, run, diag):
            off = pl.multiple_of(kb * blk, blk)
            vt = v_ref[0, pl.ds(off, blk), :]
            a, log_beta, log_keep = _sb_weights(qb, k_ref[0, pl.ds(off, blk), :], scale, run, tri_after, diag)
            g_s[kb] = a * _dot(dob, vt, "nt")
            beta_s[kb] = jnp.exp(log_beta)
            dv_ref[0, pl.ds(off, blk), :] += _dot(a.astype(BF16), dob, "tn")
            return run + jnp.sum(log_keep, axis=1, keepdims=True)

        run = sweep1(i, jnp.zeros((blk, 1), F32), True)
        lax.fori_loop(0, i, lambda n, c: sweep1(i - 1 - n, c, False), run)

        def sweep2(kb, carry, diag):
            before, dq = carry
            off = pl.multiple_of(kb * blk, blk)
            g, beta = g_s[kb], beta_s[kb]
            dz = (g * (1.0 - beta) - beta * (before + _tri_sum(g, tri_before))) * scale
            if diag:
                dz = jnp.where(_strict_lower(blk), dz, 0.0)
            dz = dz.astype(BF16)
            dk_ref[0, pl.ds(off, blk), :] += _dot(dz, qb, "tn")
            return before + jnp.sum(g, axis=1, keepdims=True), dq + _dot(dz, k_ref[0, pl.ds(off, blk), :], "nn")

        carry = lax.fori_loop(0, i, lambda kb, c: sweep2(kb, c, False), (jnp.zeros((blk, 1), F32), jnp.zeros((blk, d), F32)))
        _, dq = sweep2(i, carry, True)
        dq_ref[0] = dq

    qs = pl.BlockSpec((1, blk, d), lambda hh, i: (hh, i, 0))
    kvs = pl.BlockSpec((1, s, d), lambda hh, i: (hh, 0, 0))
    return pl.pallas_call(
        body, name=name, grid=(h, n_blk), in_specs=[qs, kvs, kvs, qs], out_specs=[qs, kvs, kvs],
        out_shape=[jax.ShapeDtypeStruct((h, s, d), F32)] * 3,
        scratch_shapes=[pltpu.VMEM((n_blk, blk, blk), F32), pltpu.VMEM((n_blk, blk, blk), F32)],
        compiler_params=_cparams("arbitrary", "arbitrary"),
    )(q, k, v, do)


def _heads(a, n_heads):
    s = a.shape[0]
    return a.reshape(s, n_heads, -1).transpose(1, 0, 2)


def _tokens(a):
    return a.transpose(1, 0, 2).reshape(a.shape[1], -1)


def _rope_tables(s):
    half = MLA_ROPE // 2
    freqs = ROPE_BASE ** (-jnp.arange(half, dtype=F32) / half)
    ang = jnp.arange(s, dtype=F32)[:, None] * freqs[None, :]
    cos, sin = jnp.cos(ang), jnp.sin(ang)
    pair = lambda a, b, lead: jnp.concatenate([jnp.zeros((s, lead), F32), a, b, jnp.zeros((s, 128 - MLA_ROPE - lead), F32)], axis=1)
    return dict(
        cos_q=jnp.tile(cos, (1, MLA_HEADS)), sin_q=jnp.tile(sin, (1, MLA_HEADS)),
        cos_k0=pair(cos, cos, 0), sin_k0=pair(-sin, sin, 0),
        cos_k64=pair(cos, cos, MLA_NOPE), sin_k64=pair(-sin, sin, MLA_NOPE),
    )


def _local_step(x, mem, target, w):
    s = x.shape[0]
    blk = min(256, s)
    rope = _rope_tables(s)
    xb = x.astype(BF16)
    inv_d = 1.0 / D_MODEL

    proj = _mm(xb, w["w_in"], "nn", name="proj")
    pre = _mm(xb, w["w_merge_gate"], "nn", name="merge_pre")

    def mla_prep(c_q, c_kv, k_rope, g_q, g_kv, cos_k, sin_k):
        n_q = c_q * lax.rsqrt(jnp.mean(c_q * c_q, axis=1, keepdims=True) + RMS_EPS) * g_q
        n_kv = c_kv * lax.rsqrt(jnp.mean(c_kv * c_kv, axis=1, keepdims=True) + RMS_EPS) * g_kv
        return n_q, n_kv, k_rope * cos_k + _swap_halves(k_rope, 0) * sin_k

    n_q, n_kv, k_pe = _rowwise(
        mla_prep, [(proj, 256, COL_CQ), (proj, 128, COL_CKV), (proj, 128, COL_KROPE), w["q_a_gain"], w["kv_a_gain"],
                   rope["cos_k0"], rope["sin_k0"]],
        [(256, BF16), (128, BF16), (128, BF16)], name="mla_prep", rows=s)
    q_a = _mm(n_q, w["w_q_b"], "nn", name="q_up")
    kv_a = _mm(n_kv, w["w_kv_b"], "nn", name="kv_up", out_dtype=BF16)

    def rope_q(pe1, pe2, cos, sin):
        return pe1 * cos - pe2 * sin, pe1 * sin + pe2 * cos

    r1, r2 = _rowwise(rope_q, [(q_a, 128, 512), (q_a, 128, 640), rope["cos_q"], rope["sin_q"]],
                      [(128, BF16), (128, BF16)], name="rope_q", rows=s)
    pad_h = jnp.zeros((s, MLA_HEADS, 128 - MLA_NOPE - MLA_ROPE), BF16)
    q_full = jnp.concatenate([q_a[:, :512].astype(BF16).reshape(s, MLA_HEADS, MLA_NOPE), r1.reshape(s, MLA_HEADS, 16),
                              r2.reshape(s, MLA_HEADS, 16), pad_h], axis=2).transpose(1, 0, 2)
    k_full = jnp.concatenate([kv_a[:, :512].reshape(s, MLA_HEADS, MLA_NOPE),
                              jnp.broadcast_to(k_pe[:, None, :MLA_ROPE], (s, MLA_HEADS, MLA_ROPE)), pad_h], axis=2).transpose(1, 0, 2)
    v_a = _heads(kv_a[:, 512:], MLA_HEADS)
    scale_a = 1.0 / math.sqrt(MLA_NOPE + MLA_ROPE)
    o_a, lse_a = _softmax_fwd(q_full, k_full, v_a, scale=scale_a, causal=True, name="mla_fwd", blk=blk)

    q_sb = _heads(proj[:, COL_QB:COL_QB + 512].astype(BF16), SB_HEADS)
    k_sb = _heads(proj[:, COL_KB:COL_KB + 512].astype(BF16), SB_HEADS)
    v_sb = _heads(proj[:, COL_VB:COL_VB + 512].astype(BF16), SB_HEADS)
    scale_b = 1.0 / math.sqrt(SB_HEAD_DIM)
    o_b = _sb_fwd(q_sb, k_sb, v_sb, scale=scale_b, name="sb_fwd", blk=blk)

    memb = mem.astype(BF16)
    n_mem = mem.shape[0]
    mem_kv = _mm(memb, w["w_mem_kv"], "nn", name="mem_kv", out_dtype=BF16)
    k_m, v_m = _heads(mem_kv[:, :512], MEM_HEADS), _heads(mem_kv[:, 512:], MEM_HEADS)
    q_m = _heads(proj[:, COL_QM:COL_QM + 512].astype(BF16), MEM_HEADS)
    scale_m = 1.0 / math.sqrt(MEM_HEAD_DIM)
    o_m, lse_m = _softmax_fwd(q_m, k_m, v_m, scale=scale_m, causal=False, name="mem_fwd", blk=min(blk, n_mem))

    o_tok = {"mla": _tokens(o_a), "sb": _tokens(o_b), "mem": _tokens(o_m)}
    gate_col = {"mla": COL_GATE_A, "sb": COL_GATE_B, "mem": COL_GATE_M}

    def gated(o, gate):
        return (o * gate * _sigmoid(gate),)

    u, y = {}, {}
    for br in ("mla", "sb", "mem"):
        (u[br],) = _rowwise(gated, [o_tok[br], (proj, 512, gate_col[br])], [(512, BF16)], name=f"gated_{br}", rows=s)
        y[br] = _mm(u[br], w[f"w_branch_{br}"], "nn", name=f"branch_{br}")

    def merge(pa, pb, pm, ba, bb, bm, ya, yb, ym):
        return (_sigmoid(pa + ba) * ya + _sigmoid(pb + bb) * yb + _sigmoid(pm + bm) * ym,)

    bias = w["b_merge_gate"]
    gate_ins = [(pre, 1024, 0), (pre, 1024, 1024), (pre, 1024, 2048), (bias, 1024, 0), (bias, 1024, 1024), (bias, 1024, 2048)]
    (merged,) = _rowwise(merge, gate_ins + [y["mla"], y["sb"], y["mem"]], [(1024, BF16)], name="merge", rows=s)
    out = _mm(merged, w["w_out"], "nn", name="out_proj")

    def norm_loss(xv, ov, tv, gain, bias_ln):
        z = DEEPNORM_ALPHA * xv + ov
        zc = z - jnp.mean(z, axis=1, keepdims=True)
        rstd = lax.rsqrt(jnp.mean(zc * zc, axis=1, keepdims=True) + LN_EPS)
        xhat = zc * rstd
        err = xhat * gain + bias_ln - tv
        loss = 0.5 * jnp.sum(jnp.mean(err * err, axis=1, keepdims=True), axis=0, keepdims=True)
        dy = err * inv_d
        dxhat = dy * gain
        dz = rstd * (dxhat - jnp.mean(dxhat, axis=1, keepdims=True) - xhat * jnp.mean(dxhat * xhat, axis=1, keepdims=True))
        return dz, dz, _colsum(dy * xhat), _colsum(dy), jnp.broadcast_to(loss, (1, 128))

    dz, dzb, g_ln_gain, g_ln_bias, loss = _rowwise(
        norm_loss, [x, out, target, w["ln_gain"], w["ln_bias"]],
        [(1024, F32), (1024, BF16), ("sum", 1024), ("sum", 1024), ("sum", 128)], name="norm_loss", rows=s)

    grads = {"ln_gain": g_ln_gain, "ln_bias": g_ln_bias}
    dmerged = _mm(dzb, w["w_out"], "nt", name="d_merged")
    grads["w_out"] = _mm(merged, dzb, "tn", name="g_w_out")

    def merge_bwd(dm, pa, pb, pm, ba, bb, bm, ya, yb, ym):
        res, dpre = [], []
        for p, b, yv in ((pa, ba, ya), (pb, bb, yb), (pm, bm, ym)):
            g = _sigmoid(p + b)
            dpre.append(dm * yv * g * (1.0 - g))
            res.append(dm * g)
        dpre = jnp.concatenate(dpre, axis=1)
        return dpre, _colsum(dpre), *res

    dpre, grads["b_merge_gate"], dy_a, dy_b, dy_m = _rowwise(
        merge_bwd, [dmerged] + gate_ins + [y["mla"], y["sb"], y["mem"]],
        [(3072, BF16), ("sum", 3072), (1024, BF16), (1024, BF16), (1024, BF16)], name="merge_bwd", rows=s, tr=128)
    grads["w_merge_gate"] = _mm(xb, dpre, "tn", name="g_w_merge")
    dx = _mm(dpre, w["w_merge_gate"], "nt", name="dx_merge", add=dz, add_scale=DEEPNORM_ALPHA)

    def gated_bwd(du, o, gate):
        sg = _sigmoid(gate)
        return du * gate * sg, du * o * sg * (1.0 + gate * (1.0 - sg))

    d_o, d_gate = {}, {}
    for br, dy in (("mla", dy_a), ("sb", dy_b), ("mem", dy_m)):
        grads[f"w_branch_{br}"] = _mm(u[br], dy, "tn", name=f"g_w_branch_{br}")
        du = _mm(dy, w[f"w_branch_{br}"], "nt", name=f"d_u_{br}")
        d_o[br], d_gate[br] = _rowwise(gated_bwd, [du, o_tok[br], (proj, 512, gate_col[br])], [(512, F32), (512, BF16)],
                                       name=f"gated_bwd_{br}", rows=s)

    dq_m, dk_m, dv_m = _softmax_bwd(q_m, k_m, v_m, o_m, _heads(d_o["mem"], MEM_HEADS), lse_m, scale=scale_m, causal=False,
                                    name="mem_bwd", blk=min(blk, n_mem))
    d_mem_kv = jnp.concatenate([_tokens(dk_m), _tokens(dv_m)], axis=1).astype(BF16)
    grads["w_mem_kv"] = _mm(memb, d_mem_kv, "tn", name="g_w_mem_kv")

    dq_sb, dk_sb, dv_sb = _sb_bwd(q_sb, k_sb, v_sb, _heads(d_o["sb"], SB_HEADS), scale=scale_b, name="sb_bwd", blk=blk)

    dq_full, dk_full, dv_a = _softmax_bwd(q_full, k_full, v_a, o_a, _heads(d_o["mla"], MLA_HEADS), lse_a, scale=scale_a,
                                          causal=True, name="mla_bwd", blk=blk)
    dq_tok = dq_full.transpose(1, 0, 2)
    dr1 = dq_tok[:, :, MLA_NOPE:MLA_NOPE + 16].reshape(s, 128)
    dr2 = dq_tok[:, :, MLA_NOPE + 16:MLA_NOPE + 32].reshape(s, 128)

    def rope_q_bwd(d1, d2, cos, sin):
        return (jnp.concatenate([d1 * cos + d2 * sin, d2 * cos - d1 * sin], axis=1),)

    (dpe,) = _rowwise(rope_q_bwd, [dr1, dr2, rope["cos_q"], rope["sin_q"]], [(256, BF16)], name="rope_q_bwd", rows=s)
    dq_a = jnp.concatenate([dq_tok[:, :, :MLA_NOPE].reshape(s, 512).astype(BF16), dpe], axis=1)
    grads["w_q_b"] = _mm(n_q, dq_a, "tn", name="g_w_q_b")
    dn_q = _mm(dq_a, w["w_q_b"], "nt", name="d_n_q")

    def rope_k_bwd(dk_heads, cos_k, sin_k):
        g = jnp.sum(dk_heads, axis=0)
        return (g * cos_k + _swap_halves(g * sin_k, MLA_NOPE),)

    (dk_rope,) = _rowwise(rope_k_bwd, [dk_full, rope["cos_k64"], rope["sin_k64"]], [(128, BF16)], name="rope_k_bwd", rows=s)
    dkv_a = jnp.concatenate([dk_full.transpose(1, 0, 2)[:, :, :MLA_NOPE].reshape(s, 512), _tokens(dv_a)], axis=1).astype(BF16)
    grads["w_kv_b"] = _mm(n_kv, dkv_a, "tn", name="g_w_kv_b")
    dn_kv = _mm(dkv_a, w["w_kv_b"], "nt", name="d_n_kv")

    def rms_bwd(c_q, c_kv, dq, dkv, g_q, g_kv):
        res = []
        for c, dn, g in ((c_q, dq, g_q), (c_kv, dkv, g_kv)):
            r = lax.rsqrt(jnp.mean(c * c, axis=1, keepdims=True) + RMS_EPS)
            t = dn * g
            res += [r * t - c * (r * r * r) * jnp.mean(c * t, axis=1, keepdims=True), _colsum(dn * c * r)]
        return res

    dc_q, grads["q_a_gain"], dc_kv, grads["kv_a_gain"] = _rowwise(
        rms_bwd, [(proj, 256, COL_CQ), (proj, 128, COL_CKV), dn_q, dn_kv, w["q_a_gain"], w["kv_a_gain"]],
        [(256, BF16), ("sum", 256), (128, BF16), ("sum", 128)], name="rms_bwd", rows=s)

    dk_rope_pad = jnp.concatenate([dk_rope[:, MLA_NOPE:MLA_NOPE + MLA_ROPE], jnp.zeros((s, 128 - MLA_ROPE), BF16)], axis=1)
    dproj = jnp.concatenate(
        [dc_q, dc_kv, dk_rope_pad, d_gate["mla"], _tokens(dq_sb).astype(BF16), _tokens(dk_sb).astype(BF16),
         _tokens(dv_sb).astype(BF16), d_gate["sb"], _tokens(dq_m).astype(BF16), d_gate["mem"]], axis=1)
    grads["w_in"] = _mm(xb, dproj, "tn", name="g_w_in")
    grad_x = _mm(dproj, w["w_in"], "nt", name="grad_x", add=dx)
    return loss, grad_x, grads


def _shard_shape(shape, axis):
    return tuple(d // N_DEV if a == axis else d for a, d in enumerate(shape))


def _pack(shards, dtype):
    flat = [a.reshape(-1).astype(dtype) for a in shards]
    n = sum(a.size for a in flat)
    flat.append(jnp.zeros((PACK_ROWS * PACK_COLS - n,), dtype))
    return jnp.concatenate(flat).reshape(PACK_ROWS, PACK_COLS)


def _unpack_shards(packed):
    flat, res, off = packed.reshape(-1), [], 0
    for _, shape, axis in SHARDED:
        shp = _shard_shape(shape, axis)
        n = shp[0] * shp[1]
        res.append(flat[off:off + n].reshape(shp))
        off += n
    return res


def _unpack_full(gathered):
    flat, res, off = gathered.reshape(N_DEV, -1), {}, 0
    for name, shape, axis in SHARDED:
        shp = _shard_shape(shape, axis)
        n = shp[0] * shp[1]
        blocks = flat[:, off:off + n].reshape(N_DEV, *shp)
        res[name] = blocks.reshape(shape) if axis == 0 else blocks.transpose(1, 0, 2).reshape(shape)
        off += n
    return res


def _pack_blocks(full):
    parts = []
    for name, shape, axis in SHARDED:
        shp = _shard_shape(shape, axis)
        g = full[name]
        g = g.reshape(N_DEV, -1) if axis == 0 else g.reshape(shape[0], N_DEV, shp[1]).transpose(1, 0, 2).reshape(N_DEV, -1)
        parts.append(g.astype(BF16))
    n = sum(p.shape[1] for p in parts)
    parts.append(jnp.zeros((N_DEV, PACK_ROWS * PACK_COLS - n), BF16))
    return jnp.concatenate(parts, axis=1).reshape(N_DEV, PACK_ROWS, PACK_COLS)


def _to_kernel_layout(full):
    w = dict(full)
    w_in = full["w_in"]
    w["w_in"] = jnp.concatenate([w_in[:, :IN_REAL], jnp.zeros((D_MODEL, 512 - IN_REAL), w_in.dtype), w_in[:, IN_REAL:]], axis=1)
    q = full["w_q_b"].reshape(MLA_Q_LORA, MLA_HEADS, MLA_NOPE + MLA_ROPE)
    w["w_q_b"] = jnp.concatenate([q[:, :, :MLA_NOPE].reshape(MLA_Q_LORA, -1), q[:, :, MLA_NOPE:MLA_NOPE + 16].reshape(MLA_Q_LORA, -1),
                                  q[:, :, MLA_NOPE + 16:].reshape(MLA_Q_LORA, -1)], axis=1)
    kv = full["w_kv_b"].reshape(MLA_KV_LORA, MLA_HEADS, MLA_NOPE + MLA_V)
    w["w_kv_b"] = jnp.concatenate([kv[:, :, :MLA_NOPE].reshape(MLA_KV_LORA, -1), kv[:, :, MLA_NOPE:].reshape(MLA_KV_LORA, -1)], axis=1)
    return w


def _from_kernel_layout(g):
    r = dict(g)
    g_in = g["w_in"]
    r["w_in"] = jnp.concatenate([g_in[:, :IN_REAL], g_in[:, 512:]], axis=1)
    q = g["w_q_b"]
    r["w_q_b"] = jnp.concatenate([q[:, :512].reshape(MLA_Q_LORA, MLA_HEADS, MLA_NOPE), q[:, 512:640].reshape(MLA_Q_LORA, MLA_HEADS, 16),
                                  q[:, 640:].reshape(MLA_Q_LORA, MLA_HEADS, 16)], axis=2).reshape(MLA_Q_LORA, -1)
    kv = g["w_kv_b"]
    r["w_kv_b"] = jnp.concatenate([kv[:, :512].reshape(MLA_KV_LORA, MLA_HEADS, MLA_NOPE), kv[:, 512:].reshape(MLA_KV_LORA, MLA_HEADS, MLA_V)],
                                  axis=2).reshape(MLA_KV_LORA, -1)
    return r


def _pack_small(vectors, loss=None):
    flat = [v.reshape(-1) for v in vectors]
    flat.append(jnp.zeros((SMALL_ROWS * SMALL_LANES - LOSS_INDEX,), F32) if loss is None else
                jnp.concatenate([loss.reshape(-1)[:1], jnp.zeros((SMALL_ROWS * SMALL_LANES - LOSS_INDEX - 1,), F32)]))
    return jnp.concatenate(flat).reshape(SMALL_ROWS, SMALL_LANES)


def _unpack_small(packed):
    flat, res, off = packed.reshape(-1), [], 0
    for _, n in SMALL:
        res.append(flat[off:off + n].reshape(1, n))
        off += n
    return res


def _exchange(big, small, *, name):
    gather = big.ndim == 2
    blk_shape = big.shape if gather else big.shape[1:]
    n_sem = 7 * (1 if small is None else 2)

    def body(*refs):
        if small is None:
            big_ref, rbig_ref, send_sems, recv_sems, local_sems = refs
        else:
            big_ref, small_ref, rbig_ref, rsmall_ref, send_sems, recv_sems, local_sems = refs
        x, y, c = lax.axis_index("x"), lax.axis_index("y"), lax.axis_index("c")
        me = 4 * x + 2 * y + c
        copies = []
        for kk in range(1, N_DEV):
            px, py, pc = (x + (kk >> 2)) % 2, (y + ((kk >> 1) & 1)) % 2, (c + (kk & 1)) % 2
            peer = 4 * px + 2 * py + pc
            copies.append(pltpu.make_async_remote_copy(
                src_ref=big_ref if gather else big_ref.at[peer], dst_ref=rbig_ref.at[me],
                send_sem=send_sems.at[kk - 1], recv_sem=recv_sems.at[kk - 1],
                device_id=(px, py, pc), device_id_type=pl.DeviceIdType.MESH))
            if small is not None:
                copies.append(pltpu.make_async_remote_copy(
                    src_ref=small_ref, dst_ref=rsmall_ref.at[me], send_sem=send_sems.at[6 + kk], recv_sem=recv_sems.at[6 + kk],
                    device_id=(px, py, pc), device_id_type=pl.DeviceIdType.MESH))
        local = [pltpu.make_async_copy(big_ref if gather else big_ref.at[me], rbig_ref.at[me], local_sems.at[0])]
        if small is not None:
            local.append(pltpu.make_async_copy(small_ref, rsmall_ref.at[me], local_sems.at[1]))
        for cp in copies + local:
            cp.start()
        for cp in copies + local:
            cp.wait()

    hbm = pl.BlockSpec(memory_space=pl.ANY)
    out_shape = [jax.ShapeDtypeStruct((N_DEV, *blk_shape), big.dtype)]
    if small is not None:
        out_shape.append(jax.ShapeDtypeStruct((N_DEV, *small.shape), small.dtype))
    res = pl.pallas_call(
        body, name=name, in_specs=[hbm] * (1 if small is None else 2), out_specs=[hbm] * len(out_shape), out_shape=out_shape,
        scratch_shapes=[pltpu.SemaphoreType.DMA((n_sem,)), pltpu.SemaphoreType.DMA((n_sem,)), pltpu.SemaphoreType.DMA((2,))],
        compiler_params=pltpu.CompilerParams(has_side_effects=True),
    )(*((big,) if small is None else (big, small)))
    return res[0] if small is None else res


def _adamw(contrib, w, m, v, *, name, tile):
    rows, cols = w.shape

    def body(c_ref, w_ref, m_ref, v_ref, g_ref, d_ref, nm_ref, nv_ref):
        g = c_ref[0].astype(F32)
        for s in range(1, N_DEV):
            g = g + c_ref[s].astype(F32)
        m_new = ADAM_B1 * m_ref[...] + (1.0 - ADAM_B1) * g
        v_new = ADAM_B2 * v_ref[...] + (1.0 - ADAM_B2) * (g * g)
        m_hat = m_new / (1.0 - ADAM_B1 ** ADAM_STEP)
        v_hat = v_new / (1.0 - ADAM_B2 ** ADAM_STEP)
        g_ref[...] = g
        d_ref[...] = -ADAM_LR * (m_hat / (jnp.sqrt(v_hat) + ADAM_EPS) + ADAM_WD * w_ref[...])
        nm_ref[...] = m_new
        nv_ref[...] = v_new

    spec = pl.BlockSpec((tile, cols), lambda i: (i, 0))
    return pl.pallas_call(
        body, name=name, grid=(rows // tile,),
        in_specs=[pl.BlockSpec((N_DEV, tile, cols), lambda i: (0, i, 0)), spec, spec, spec], out_specs=[spec] * 4,
        out_shape=[jax.ShapeDtypeStruct((rows, cols), F32)] * 4, compiler_params=_cparams("parallel"),
    )(contrib, w, m, v)


def kernel(x, mem, w_in, w_mem_kv, q_a_gain, w_q_b, kv_a_gain, w_kv_b, w_branch_mla, w_branch_sb, w_branch_mem, w_merge_gate, b_merge_gate, w_out, ln_gain, ln_bias, loss_target, m_w_in, m_w_mem_kv, m_q_a_gain, m_w_q_b, m_kv_a_gain, m_w_kv_b, m_w_branch_mla, m_w_branch_sb, m_w_branch_mem, m_w_merge_gate, m_b_merge_gate, m_w_out, m_ln_gain, m_ln_bias, v_w_in, v_w_mem_kv, v_q_a_gain, v_w_q_b, v_kv_a_gain, v_w_kv_b, v_w_branch_mla, v_w_branch_sb, v_w_branch_mem, v_w_merge_gate, v_b_merge_gate, v_w_out, v_ln_gain, v_ln_bias):
    given = dict(locals())
    sharded_names = [n for n, _, _ in SHARDED]
    small_names = [n for n, _ in SMALL]
    shards = lambda prefix: [given[prefix + n][0] for n in sharded_names]
    smalls = lambda prefix: [given[prefix + n] for n in small_names]

    gathered = _exchange(_pack(shards(""), BF16), None, name="gather_weights")
    w = _to_kernel_layout(_unpack_full(gathered))
    for n in small_names:
        w[n] = given[n]
    loss, grad_x, grads = _local_step(x[0], mem[0], loss_target[0], w)

    blocks = _pack_blocks(_from_kernel_layout({n: grads[n] for n in sharded_names}))
    contrib, contrib_small = _exchange(blocks, _pack_small([grads[n] for n in small_names], loss), name="exchange_grads")

    big = _adamw(contrib, _pack(shards(""), F32), _pack(shards("m_"), F32), _pack(shards("v_"), F32), name="adamw", tile=PACK_TILE)
    sml = _adamw(contrib_small, _pack_small(smalls("")), _pack_small(smalls("m_")), _pack_small(smalls("v_")), name="adamw_small",
                 tile=SMALL_ROWS)
    per_kind = []
    for packed_big, packed_small in zip(big, sml, strict=True):
        by_name = dict(zip(sharded_names, [a[None] for a in _unpack_shards(packed_big)], strict=True))
        by_name.update(zip(small_names, _unpack_small(packed_small), strict=True))
        per_kind.append(by_name)
    order = ["w_in", "w_mem_kv", "q_a_gain", "w_q_b", "kv_a_gain", "w_kv_b", "w_branch_mla", "w_branch_sb", "w_branch_mem",
             "w_merge_gate", "b_merge_gate", "w_out", "ln_gain", "ln_bias"]
    loss_out = sml[0].reshape(-1)[LOSS_INDEX]
    return (loss_out, grad_x[None], *[kind[n] for kind in per_kind for n in order])
```

```python
import math

import jax
import jax.numpy as jnp
from jax import lax
from jax.experimental import pallas as pl
from jax.experimental.pallas import tpu as pltpu

F32, BF16 = jnp.float32, jnp.bfloat16

N_DEV = 8
D_MODEL = 1024
MLA_HEADS, MLA_NOPE, MLA_ROPE, MLA_V = 8, 64, 32, 64
MLA_Q_LORA, MLA_KV_LORA = 256, 128
SB_HEAD_DIM = 64
MEM_HEAD_DIM = 128
ROPE_BASE = 10000.0
RMS_EPS = 1e-6
LN_EPS = 1e-5
DEEPNORM_ALPHA = 2.0 ** 0.25
ADAM_LR, ADAM_B1, ADAM_B2, ADAM_EPS, ADAM_WD, ADAM_STEP = 0.001, 0.9, 0.999, 1e-08, 0.01, 10
LOG2E, LN2 = math.log2(math.e), math.log(2.0)

LANES = 128
GROUPS = 4
PROJ_WIDTH = 4096
COL_CQ, COL_CKV, COL_KROPE, COL_GATE_A = 0, 256, 384, 512
COL_QB, COL_KB, COL_VB, COL_GATE_B, COL_QM, COL_GATE_M = 1024, 1536, 2048, 2560, 3072, 3584
IN_REAL = 416

VMEM_LIMIT_BYTES = 56 * 1024 * 1024
NEG_BIG = -1e30
Q_BLOCK = 256
KEY_CHUNK = 512

SHARDED = (
    ("w_in", (1024, 4000), 1), ("w_mem_kv", (1024, 1024), 0), ("w_q_b", (256, 768), 1), ("w_kv_b", (128, 1024), 1),
    ("w_branch_mla", (512, 1024), 1), ("w_branch_sb", (512, 1024), 1), ("w_branch_mem", (512, 1024), 1),
    ("w_merge_gate", (1024, 3072), 1), ("w_out", (1024, 1024), 0),
)
SMALL = (("q_a_gain", 256), ("kv_a_gain", 128), ("b_merge_gate", 3072), ("ln_gain", 1024), ("ln_bias", 1024))
PACK_COLS = 1024
PACK_ROWS = 1408
PACK_TILE = 128
SMALL_ROWS, SMALL_LANES = 48, 128
LOSS_INDEX = 5504


def _cparams(*sem):
    return pltpu.CompilerParams(dimension_semantics=sem or None, vmem_limit_bytes=VMEM_LIMIT_BYTES)


_DIMS = {"nn": (((1,), (0,)), ((), ())), "nt": (((1,), (1,)), ((), ())), "tn": (((0,), (0,)), ((), ()))}


def _dot(a, b, dims):
    return lax.dot_general(a, b, _DIMS[dims], preferred_element_type=F32)


def _tile(dim, want):
    if dim <= want:
        return dim
    t = want - want % LANES
    while dim % t:
        t -= LANES
    assert t > 0, (dim, want)
    return t


def _mm(a, b, dims, *, name, out_dtype=F32, add=None, add_scale=1.0, tm=512, tn=512, tk=512):
    if dims == "nn":
        (m, k), (k2, n) = a.shape, b.shape
    elif dims == "nt":
        (m, k), (n, k2) = a.shape, b.shape
    else:
        (k, m), (k2, n) = a.shape, b.shape
    assert k == k2, (a.shape, b.shape, dims)
    tm, tn, tk = _tile(m, tm), _tile(n, tn), _tile(k, tk)
    nk = k // tk
    a_spec = pl.BlockSpec((tk, tm), lambda i, j, kk: (kk, i)) if dims == "tn" else pl.BlockSpec((tm, tk), lambda i, j, kk: (i, kk))
    b_spec = pl.BlockSpec((tn, tk), lambda i, j, kk: (j, kk)) if dims == "nt" else pl.BlockSpec((tk, tn), lambda i, j, kk: (kk, j))
    o_spec = pl.BlockSpec((tm, tn), lambda i, j, kk: (i, j))

    def body(*refs):
        if add is None:
            a_ref, b_ref, o_ref, acc = refs
        else:
            a_ref, b_ref, add_ref, o_ref, acc = refs
        kk = pl.program_id(2)

        @pl.when(kk == 0)
        def _():
            acc[...] = jnp.zeros_like(acc)

        acc[...] += _dot(a_ref[...].astype(BF16), b_ref[...].astype(BF16), dims)

        @pl.when(kk == nk - 1)
        def _():
            r = acc[...]
            if add is not None:
                r = r + add_scale * add_ref[...]
            o_ref[...] = r.astype(out_dtype)

    return pl.pallas_call(
        body, name=name, grid=(m // tm, n // tn, nk),
        in_specs=[a_spec, b_spec] + ([o_spec] if add is not None else []), out_specs=o_spec,
        out_shape=jax.ShapeDtypeStruct((m, n), out_dtype), scratch_shapes=[pltpu.VMEM((tm, tn), F32)],
        compiler_params=_cparams("parallel", "parallel", "arbitrary"),
    )(*((a, b) if add is None else (a, b, add)))


def _rowwise(fn, ins, outs, *, name, rows, tr=256):
    n_in = len(ins)
    in_specs, args = [], []
    for it in ins:
        arr, w, off = it if isinstance(it, tuple) else (it, it.shape[-1], 0)
        assert off % w == 0
        cb = off // w
        if arr.shape[0] == 1:
            in_specs.append(pl.BlockSpec((1, w), lambda i, cb=cb: (0, cb)))
        else:
            in_specs.append(pl.BlockSpec((tr, w), lambda i, cb=cb: (i, cb)))
        args.append(arr)
    out_shape, out_specs, is_sum = [], [], []
    for kind, d in outs:
        if kind == "sum":
            out_shape.append(jax.ShapeDtypeStruct((1, d), F32))
            out_specs.append(pl.BlockSpec((1, d), lambda i: (0, 0)))
            is_sum.append(True)
        else:
            out_shape.append(jax.ShapeDtypeStruct((rows, kind), d))
            out_specs.append(pl.BlockSpec((tr, kind), lambda i: (i, 0)))
            is_sum.append(False)

    def body(*refs):
        res = fn(*[r[...] for r in refs[:n_in]])
        for r, val, s in zip(refs[n_in:], res, is_sum, strict=True):
            if s:
                @pl.when(pl.program_id(0) == 0)
                def _(r=r):
                    r[...] = jnp.zeros_like(r)

                r[...] += val
            else:
                r[...] = val.astype(r.dtype)

    return pl.pallas_call(
        body, name=name, grid=(rows // tr,), in_specs=in_specs, out_specs=out_specs, out_shape=out_shape,
        compiler_params=_cparams("arbitrary"),
    )(*args)


def _colsum(v):
    return jnp.sum(v, axis=0, keepdims=True)


def _sigmoid(v):
    return 1.0 / (1.0 + jnp.exp(-v))


def _lane_groups(v):
    return [v[:, g * LANES:(g + 1) * LANES] for g in range(v.shape[1] // LANES)]


def _swap_halves(v, first_lane):
    lane = lax.broadcasted_iota(jnp.int32, v.shape, 1)
    return jnp.where(lane < first_lane + 16, pltpu.roll(v, 112, axis=1), pltpu.roll(v, 16, axis=1))


def _lane_sum(acc, v):
    for part in _lane_groups(v):
        acc = acc + part
    return acc


def _low_half(shape):
    return lax.broadcasted_iota(jnp.int32, shape, 1) < LANES // 2


def _select_heads(per_head, pick):
    if len(per_head) == 1:
        return pick(per_head[0], 0)
    return jnp.where(_low_half(per_head[0].shape), pick(per_head[0], 0), pick(per_head[1], 1))


def _attn_specs(s, sk, hp, bq, q0, k0, v0):
    wq = hp * LANES
    assert q0 % wq == 0 and k0 % wq == 0 and v0 % LANES == 0
    qb0, kb0, vb0 = q0 // wq, k0 // wq, v0 // LANES
    q_spec = pl.BlockSpec((bq, wq), lambda g, i: (i, qb0 + g))
    k_spec = pl.BlockSpec((sk, wq), lambda g, i: (0, kb0 + g))
    v_spec = pl.BlockSpec((sk, LANES), lambda g, i: (0, vb0 + g))
    row_out = lambda w: pl.BlockSpec((bq, w), lambda g, i: (i, g))
    key_out = lambda w: pl.BlockSpec((sk, w), lambda g, i: (0, g))
    return q_spec, k_spec, v_spec, row_out, key_out


def _chunks(i, bq, ch, sk, causal):
    return ((i + 1) * bq - 1) // ch if causal else jnp.int32(sk // ch - 1)


def _positions(i, c, bq, ch):
    return (i * bq + lax.broadcasted_iota(jnp.int32, (bq, ch), 0), c * ch + lax.broadcasted_iota(jnp.int32, (bq, ch), 1))


def _softmax_fwd(q, k, v, *, hp, causal, name, q0=0, k0=0, v0=0):
    s, sk = q.shape[0], k.shape[0]
    bq, ch = min(Q_BLOCK, s), min(KEY_CHUNK, sk)
    q_spec, k_spec, v_spec, row_out, _ = _attn_specs(s, sk, hp, bq, q0, k0, v0)

    def body(q_ref, k_ref, v_ref, o_ref, lse_ref, s_scr):
        i = pl.program_id(1)
        qs = _lane_groups(q_ref[...])
        last = _chunks(i, bq, ch, sk, causal)

        def scores(c, ms, masked):
            off = pl.multiple_of(c * ch, ch)
            out = []
            for j in range(hp):
                sc = _dot(qs[j], k_ref[pl.ds(off, ch), j * LANES:(j + 1) * LANES], "nt")
                if masked:
                    qpos, kpos = _positions(i, c, bq, ch)
                    sc = jnp.where(kpos <= qpos, sc, NEG_BIG)
                s_scr[j, c] = sc
                m = ms[j]
                for part in _lane_groups(sc):
                    m = jnp.maximum(m, part)
                out.append(m)
            return tuple(out)

        ms = lax.fori_loop(0, last, lambda c, m: scores(c, m, False), tuple(jnp.full((bq, LANES), NEG_BIG, F32) for _ in range(hp)))
        ms = scores(last, ms, causal)
        row_max = [jnp.max(m, axis=1, keepdims=True) for m in ms]

        def weigh(c, carry):
            off = pl.multiple_of(c * ch, ch)
            vt = v_ref[pl.ds(off, ch), :]
            out = []
            for j in range(hp):
                l, acc = carry[j]
                p = jnp.exp2(s_scr[j, c] - row_max[j])
                out.append((_lane_sum(l, p), acc + _dot(p.astype(BF16), vt, "nn")))
            return tuple(out)

        zero = jnp.zeros((bq, LANES), F32)
        res = lax.fori_loop(0, last + 1, weigh, tuple((zero, zero) for _ in range(hp)))
        row_sum = [jnp.sum(l, axis=1, keepdims=True) for l, _ in res]
        o_ref[...] = _select_heads([acc for _, acc in res], lambda acc, j: acc / row_sum[j])
        lse_ref[...] = _select_heads([jnp.broadcast_to(row_max[j] + jnp.log2(row_sum[j]), (bq, LANES)) for j in range(hp)], lambda a, j: a)

    return pl.pallas_call(
        body, name=name, grid=(GROUPS, s // bq), in_specs=[q_spec, k_spec, v_spec], out_specs=[row_out(LANES), row_out(LANES)],
        out_shape=[jax.ShapeDtypeStruct((s, GROUPS * LANES), F32)] * 2,
        scratch_shapes=[pltpu.VMEM((hp, sk // ch, bq, ch), F32)], compiler_params=_cparams("parallel", "arbitrary"),
    )(q, k, v)


def _head_cotangent(do, j, hp):
    if hp == 1:
        return do
    return jnp.where(_low_half(do.shape) == (j == 0), do, 0.0)


def _softmax_bwd(q, k, v, o, do, lse, *, hp, causal, dq_scale, name, q0=0, k0=0, v0=0):
    s, sk = q.shape[0], k.shape[0]
    bq, ch = min(Q_BLOCK, s), min(KEY_CHUNK, sk)
    wq = hp * LANES
    q_spec, k_spec, v_spec, row_out, key_out = _attn_specs(s, sk, hp, bq, q0, k0, v0)

    def body(q_ref, k_ref, v_ref, o_ref, do_ref, lse_ref, dq_ref, dk_ref, dv_ref):
        i = pl.program_id(1)

        @pl.when(i == 0)
        def _():
            dk_ref[...] = jnp.zeros_like(dk_ref)
            dv_ref[...] = jnp.zeros_like(dv_ref)

        qs = _lane_groups(q_ref[...])
        do_all, o_all, lse_all = do_ref[...], o_ref[...], lse_ref[...]
        dos, deltas, lses = [], [], []
        for j in range(hp):
            d = _head_cotangent(do_all, j, hp)
            deltas.append(jnp.sum(d * o_all, axis=1, keepdims=True))
            dos.append(d.astype(BF16))
            lses.append(lse_all[:, j * (LANES // hp):j * (LANES // hp) + 1])
        last = _chunks(i, bq, ch, sk, causal)

        def chunk(c, dqs, masked):
            off = pl.multiple_of(c * ch, ch)
            vt = v_ref[pl.ds(off, ch), :]
            out, dks, dv = [], [], None
            for j in range(hp):
                kt = k_ref[pl.ds(off, ch), j * LANES:(j + 1) * LANES]
                p = jnp.exp2(_dot(qs[j], kt, "nt") - lses[j])
                if masked:
                    qpos, kpos = _positions(i, c, bq, ch)
                    p = jnp.where(kpos <= qpos, p, 0.0)
                ds = (p * (_dot(dos[j], vt, "nt") - deltas[j]) * LN2).astype(BF16)
                out.append(dqs[j] + _dot(ds, kt, "nn"))
                dks.append(_dot(ds, qs[j], "tn"))
                dvj = _dot(p.astype(BF16), dos[j], "tn")
                dv = dvj if dv is None else dv + dvj
            dk_ref[pl.ds(off, ch), :] += dks[0] if hp == 1 else jnp.concatenate(dks, axis=1)
            dv_ref[pl.ds(off, ch), :] += dv
            return tuple(out)

        dqs = lax.fori_loop(0, last, lambda c, d: chunk(c, d, False), tuple(jnp.zeros((bq, LANES), F32) for _ in range(hp)))
        dqs = chunk(last, dqs, causal)
        dq_ref[...] = (dqs[0] if hp == 1 else jnp.concatenate(dqs, axis=1)) * dq_scale

    return pl.pallas_call(
        body, name=name, grid=(GROUPS, s // bq),
        in_specs=[q_spec, k_spec, v_spec, row_out(LANES), row_out(LANES), row_out(LANES)],
        out_specs=[row_out(wq), key_out(wq), key_out(LANES)],
        out_shape=[jax.ShapeDtypeStruct((s, GROUPS * wq), F32), jax.ShapeDtypeStruct((sk, GROUPS * wq), F32),
                   jax.ShapeDtypeStruct((sk, GROUPS * LANES), F32)],
        compiler_params=_cparams("arbitrary", "arbitrary"),
    )(q, k, v, o, do, lse)


def _log_sigmoid_pair(z):
    sp = jnp.log(1.0 + jnp.exp(-jnp.abs(z)))
    return jnp.minimum(z, 0.0) - sp, jnp.minimum(-z, 0.0) - sp


def _tri_sum(v, tri):
    hi = v.astype(BF16)
    lo = (v - hi.astype(F32)).astype(BF16)
    return _dot(hi, tri, "nn") + _dot(lo, tri, "nn")


def _tri(n, after):
    rows, cols = lax.broadcasted_iota(jnp.int32, (n, n), 0), lax.broadcasted_iota(jnp.int32, (n, n), 1)
    return (rows > cols if after else rows < cols).astype(BF16)


def _running_sums(v, start, tri, backwards):
    n = tri.shape[0]
    blocks = [v[:, t * n:(t + 1) * n] for t in range(v.shape[1] // n)]
    order = range(len(blocks) - 1, -1, -1) if backwards else range(len(blocks))
    parts, run = [None] * len(blocks), start
    for t in order:
        parts[t] = _tri_sum(blocks[t], tri) + run
        run = run + jnp.sum(blocks[t], axis=1, keepdims=True)
    return (parts[0] if len(parts) == 1 else jnp.concatenate(parts, axis=1)), run


def _sb_weights(qm, kt, run, tri, strict):
    log_beta, log_keep = _log_sigmoid_pair(_dot(qm, kt, "nt"))
    if strict is not None:
        log_keep = jnp.where(strict, log_keep, 0.0)
    behind, run = _running_sums(log_keep, run, tri, True)
    a = jnp.exp(log_beta + behind)
    if strict is not None:
        a = jnp.where(strict, a, 0.0)
    return a, log_beta, run


def _sb_queries(q_all):
    low = _low_half(q_all.shape)
    zero = jnp.zeros_like(q_all)
    return [jnp.where(low, q_all, zero), jnp.where(low, zero, q_all)]


def _sb_fwd(qkv, *, q0, k0, v0, name):
    s = qkv.shape[0]
    bq, ch = min(Q_BLOCK, s), min(KEY_CHUNK, s)
    q_spec, k_spec, v_spec, row_out, _ = _attn_specs(s, s, 1, bq, q0, k0, v0)

    def body(q_ref, k_ref, v_ref, o_ref):
        i = pl.program_id(1)
        qms = _sb_queries(q_ref[...])
        tri = _tri(bq, True)
        last = _chunks(i, bq, ch, s, True)

        def chunk(c, carry, masked):
            off = pl.multiple_of(c * ch, ch)
            kt, vt = k_ref[pl.ds(off, ch), :], v_ref[pl.ds(off, ch), :]
            strict = None
            if masked:
                qpos, kpos = _positions(i, c, bq, ch)
                strict = kpos < qpos
            out = []
            for j in range(2):
                run, acc = carry[j]
                a, _, run = _sb_weights(qms[j], kt, run, tri, strict)
                out.append((run, acc + _dot(a.astype(BF16), vt, "nn")))
            return tuple(out)

        carry = chunk(last, tuple((jnp.zeros((bq, 1), F32), jnp.zeros((bq, LANES), F32)) for _ in range(2)), True)
        res = lax.fori_loop(0, last, lambda n, c: chunk(last - 1 - n, c, False), carry)
        o_ref[...] = _select_heads([acc for _, acc in res], lambda acc, j: acc)

    return pl.pallas_call(
        body, name=name, grid=(GROUPS, s // bq), in_specs=[q_spec, k_spec, v_spec], out_specs=row_out(LANES),
        out_shape=jax.ShapeDtypeStruct((s, GROUPS * LANES), F32), compiler_params=_cparams("parallel", "arbitrary"),
    )(qkv, qkv, qkv)


def _sb_bwd(qkv, do, *, q0, k0, v0, dq_scale, name):
    s = qkv.shape[0]
    bq, ch = min(Q_BLOCK, s), min(KEY_CHUNK, s)
    q_spec, k_spec, v_spec, row_out, key_out = _attn_specs(s, s, 1, bq, q0, k0, v0)

    def body(q_ref, k_ref, v_ref, do_ref, dq_ref, dk_ref, dv_ref, g_s, beta_s):
        i = pl.program_id(1)

        @pl.when(i == 0)
        def _():
            dk_ref[...] = jnp.zeros_like(dk_ref)
            dv_ref[...] = jnp.zeros_like(dv_ref)

        qms = _sb_queries(q_ref[...])
        do_all = do_ref[...]
        dos = [_head_cotangent(do_all, j, 2).astype(BF16) for j in range(2)]
        tri_after, tri_before = _tri(bq, True), _tri(bq, False)
        last = _chunks(i, bq, ch, s, True)

        def strict_mask(c):
            qpos, kpos = _positions(i, c, bq, ch)
            return kpos < qpos

        def sweep1(c, runs, masked):
            off = pl.multiple_of(c * ch, ch)
            kt, vt = k_ref[pl.ds(off, ch), :], v_ref[pl.ds(off, ch), :]
            strict = strict_mask(c) if masked else None
            out, dv = [], None
            for j in range(2):
                a, log_beta, run = _sb_weights(qms[j], kt, runs[j], tri_after, strict)
                g_s[j, c] = a * _dot(dos[j], vt, "nt")
                beta_s[j, c] = jnp.exp(log_beta)
                dvj = _dot(a.astype(BF16), dos[j], "tn")
                dv = dvj if dv is None else dv + dvj
                out.append(run)
            dv_ref[pl.ds(off, ch), :] += dv
            return tuple(out)

        runs = sweep1(last, tuple(jnp.zeros((bq, 1), F32) for _ in range(2)), True)
        lax.fori_loop(0, last, lambda n, r: sweep1(last - 1 - n, r, False), runs)

        def sweep2(c, carry, masked):
            off = pl.multiple_of(c * ch, ch)
            kt = k_ref[pl.ds(off, ch), :]
            out, dk = [], None
            for j in range(2):
                before, dq = carry[j]
                g, beta = g_s[j, c], beta_s[j, c]
                in_front, before = _running_sums(g, before, tri_before, False)
                dz = g * (1.0 - beta) - beta * in_front
                if masked:
                    dz = jnp.where(strict_mask(c), dz, 0.0)
                dz = dz.astype(BF16)
                dkj = _dot(dz, qms[j], "tn")
                dk = dkj if dk is None else dk + dkj
                out.append((before, dq + _dot(dz, kt, "nn")))
            dk_ref[pl.ds(off, ch), :] += dk
            return tuple(out)

        carry = lax.fori_loop(0, last, lambda c, cr: sweep2(c, cr, False),
                              tuple((jnp.zeros((bq, 1), F32), jnp.zeros((bq, LANES), F32)) for _ in range(2)))
        res = sweep2(last, carry, True)
        dq_ref[...] = _select_heads([dq for _, dq in res], lambda dq, j: dq) * dq_scale

    n_ch = s // ch
    return pl.pallas_call(
        body, name=name, grid=(GROUPS, s // bq), in_specs=[q_spec, k_spec, v_spec, row_out(LANES)],
        out_specs=[row_out(LANES), key_out(LANES), key_out(LANES)],
        out_shape=[jax.ShapeDtypeStruct((s, GROUPS * LANES), F32)] * 3,
        scratch_shapes=[pltpu.VMEM((2, n_ch, bq, ch), F32), pltpu.VMEM((2, n_ch, bq, ch), F32)],
        compiler_params=_cparams("arbitrary", "arbitrary"),
    )(qkv, qkv, qkv, do)


def _rope_tables(s):
    half = MLA_ROPE // 2
    freqs = ROPE_BASE ** (-jnp.arange(half, dtype=F32) / half)
    ang = jnp.arange(s, dtype=F32)[:, None] * freqs[None, :]
    cos, sin = jnp.cos(ang), jnp.sin(ang)
    tail = jnp.zeros((s, LANES - MLA_NOPE - MLA_ROPE), F32)
    lead = lambda fill: jnp.full((s, MLA_NOPE), fill, F32)
    return dict(
        cos_k0=jnp.concatenate([cos, cos, lead(0.0), tail], axis=1), sin_k0=jnp.concatenate([-sin, sin, lead(0.0), tail], axis=1),
        cos_k64=jnp.concatenate([lead(0.0), cos, cos, tail], axis=1), sin_k64=jnp.concatenate([lead(0.0), -sin, sin, tail], axis=1),
        cos_q=jnp.concatenate([lead(1.0), cos, cos, tail], axis=1),
        sin_k64_t=jnp.concatenate([lead(0.0), sin, -sin, tail], axis=1),
    )


def _local_step(x, mem, target, w):
    s = x.shape[0]
    rope = _rope_tables(s)
    xb = x.astype(BF16)
    inv_d = 1.0 / D_MODEL
    scale_a = LOG2E / math.sqrt(MLA_NOPE + MLA_ROPE)
    scale_b = 1.0 / math.sqrt(SB_HEAD_DIM)
    scale_m = LOG2E / math.sqrt(MEM_HEAD_DIM)

    proj = _mm(xb, w["w_in"], "nn", name="proj")
    pre = _mm(xb, w["w_merge_gate"], "nn", name="merge_pre")

    def rms_pair(c_q, c_kv, g_q, g_kv):
        return (c_q * lax.rsqrt(jnp.mean(c_q * c_q, axis=1, keepdims=True) + RMS_EPS) * g_q,
                c_kv * lax.rsqrt(jnp.mean(c_kv * c_kv, axis=1, keepdims=True) + RMS_EPS) * g_kv)

    n_q, n_kv = _rowwise(rms_pair, [(proj, 256, COL_CQ), (proj, 128, COL_CKV), w["q_a_gain"], w["kv_a_gain"]],
                         [(256, BF16), (128, BF16)], name="rms_pair", rows=s)
    q_a = _mm(n_q, w["w_q_b"], "nn", name="q_up")
    kv_a = _mm(n_kv, w["w_kv_b"], "nn", name="kv_up", out_dtype=BF16)

    def rope_q(qa, cos, sin):
        return (jnp.concatenate([(g * cos + _swap_halves(g, MLA_NOPE) * sin) * scale_a for g in _lane_groups(qa)], axis=1),)

    (q_mla,) = _rowwise(rope_q, [q_a, rope["cos_q"], rope["sin_k64"]], [(1024, BF16)], name="rope_q", rows=s)

    def rope_k(k_nope, k_rope, cos, sin):
        k_pe = pltpu.roll(k_rope * cos + _swap_halves(k_rope, 0) * sin, MLA_NOPE, axis=1).astype(BF16)
        return (jnp.concatenate([g + k_pe for g in _lane_groups(k_nope)], axis=1),)

    (k_mla,) = _rowwise(rope_k, [(kv_a, 1024, 0), (proj, 128, COL_KROPE), rope["cos_k0"], rope["sin_k0"]],
                        [(1024, BF16)], name="rope_k", rows=s)
    o_a, lse_a = _softmax_fwd(q_mla, k_mla, kv_a, hp=2, causal=True, name="mla_fwd", v0=1024)

    def cast_qkv(q_sb, k_sb, v_sb, q_m):
        return (jnp.concatenate([(q_sb * scale_b).astype(BF16), k_sb.astype(BF16), v_sb.astype(BF16), (q_m * scale_m).astype(BF16)], axis=1),)

    (qkv,) = _rowwise(cast_qkv, [(proj, 512, COL_QB), (proj, 512, COL_KB), (proj, 512, COL_VB), (proj, 512, COL_QM)],
                      [(2048, BF16)], name="cast_qkv", rows=s)
    o_b = _sb_fwd(qkv, q0=0, k0=512, v0=1024, name="sb_fwd")

    memb = mem.astype(BF16)
    mem_kv = _mm(memb, w["w_mem_kv"], "nn", name="mem_kv", out_dtype=BF16)
    o_m, lse_m = _softmax_fwd(qkv, mem_kv, mem_kv, hp=1, causal=False, name="mem_fwd", q0=1536, v0=512)

    o_br = {"mla": o_a, "sb": o_b, "mem": o_m}
    gate_col = {"mla": COL_GATE_A, "sb": COL_GATE_B, "mem": COL_GATE_M}

    def gated(o, gate):
        return (o * gate * _sigmoid(gate),)

    u, y = {}, {}
    for br in ("mla", "sb", "mem"):
        (u[br],) = _rowwise(gated, [o_br[br], (proj, 512, gate_col[br])], [(512, BF16)], name=f"gated_{br}", rows=s)
        y[br] = _mm(u[br], w[f"w_branch_{br}"], "nn", name=f"branch_{br}")

    def merge(pa, pb, pm, ba, bb, bm, ya, yb, ym):
        return (_sigmoid(pa + ba) * ya + _sigmoid(pb + bb) * yb + _sigmoid(pm + bm) * ym,)

    bias = w["b_merge_gate"]
    gate_ins = [(pre, 1024, 0), (pre, 1024, 1024), (pre, 1024, 2048), (bias, 1024, 0), (bias, 1024, 1024), (bias, 1024, 2048)]
    (merged,) = _rowwise(merge, gate_ins + [y["mla"], y["sb"], y["mem"]], [(1024, BF16)], name="merge", rows=s)
    out = _mm(merged, w["w_out"], "nn", name="out_proj")

    def norm_loss(xv, ov, tv, gain, bias_ln):
        z = DEEPNORM_ALPHA * xv + ov
        zc = z - jnp.mean(z, axis=1, keepdims=True)
        rstd = lax.rsqrt(jnp.mean(zc * zc, axis=1, keepdims=True) + LN_EPS)
        xhat = zc * rstd
        err = xhat * gain + bias_ln - tv
        loss = 0.5 * jnp.sum(jnp.mean(err * err, axis=1, keepdims=True), axis=0, keepdims=True)
        dy = err * inv_d
        dxhat = dy * gain
        dz = rstd * (dxhat - jnp.mean(dxhat, axis=1, keepdims=True) - xhat * jnp.mean(dxhat * xhat, axis=1, keepdims=True))
        return dz, dz, _colsum(dy * xhat), _colsum(dy), jnp.broadcast_to(loss, (1, LANES))

    dz, dzb, g_ln_gain, g_ln_bias, loss = _rowwise(
        norm_loss, [x, out, target, w["ln_gain"], w["ln_bias"]],
        [(1024, F32), (1024, BF16), ("sum", 1024), ("sum", 1024), ("sum", LANES)], name="norm_loss", rows=s)

    grads = {"ln_gain": g_ln_gain, "ln_bias": g_ln_bias}
    dmerged = _mm(dzb, w["w_out"], "nt", name="d_merged")
    grads["w_out"] = _mm(merged, dzb, "tn", name="g_w_out")

    def merge_bwd(dm, pa, pb, pm, ba, bb, bm, ya, yb, ym):
        res, dpre = [], []
        for p, b, yv in ((pa, ba, ya), (pb, bb, yb), (pm, bm, ym)):
            g = _sigmoid(p + b)
            dpre.append(dm * yv * g * (1.0 - g))
            res.append(dm * g)
        dpre = jnp.concatenate(dpre, axis=1)
        return dpre, _colsum(dpre), *res

    dpre, grads["b_merge_gate"], dy_a, dy_b, dy_m = _rowwise(
        merge_bwd, [dmerged] + gate_ins + [y["mla"], y["sb"], y["mem"]],
        [(3072, BF16), ("sum", 3072), (1024, BF16), (1024, BF16), (1024, BF16)], name="merge_bwd", rows=s, tr=128)
    grads["w_merge_gate"] = _mm(xb, dpre, "tn", name="g_w_merge")
    dx = _mm(dpre, w["w_merge_gate"], "nt", name="dx_merge", add=dz, add_scale=DEEPNORM_ALPHA)

    def gated_bwd(du, o, gate):
        sg = _sigmoid(gate)
        return du * gate * sg, du * o * sg * (1.0 + gate * (1.0 - sg))

    d_o, d_gate = {}, {}
    for br, dy in (("mla", dy_a), ("sb", dy_b), ("mem", dy_m)):
        grads[f"w_branch_{br}"] = _mm(u[br], dy, "tn", name=f"g_w_branch_{br}")
        du = _mm(dy, w[f"w_branch_{br}"], "nt", name=f"d_u_{br}")
        d_o[br], d_gate[br] = _rowwise(gated_bwd, [du, o_br[br], (proj, 512, gate_col[br])], [(512, F32), (512, BF16)],
                                       name=f"gated_bwd_{br}", rows=s)

    dq_m, dk_m, dv_m = _softmax_bwd(qkv, mem_kv, mem_kv, o_m, d_o["mem"], lse_m, hp=1, causal=False, dq_scale=scale_m,
                                    name="mem_bwd", q0=1536, v0=512)
    grads["w_mem_kv"] = _mm(memb, jnp.concatenate([dk_m, dv_m], axis=1), "tn", name="g_w_mem_kv")

    dq_sb, dk_sb, dv_sb = _sb_bwd(qkv, d_o["sb"], q0=0, k0=512, v0=1024, dq_scale=scale_b, name="sb_bwd")

    dq_mla, dk_mla, dv_a = _softmax_bwd(q_mla, k_mla, kv_a, o_a, d_o["mla"], lse_a, hp=2, causal=True, dq_scale=scale_a,
                                        name="mla_bwd", v0=1024)

    def rope_q_bwd(dq, cos, sin):
        return (jnp.concatenate([g * cos + _swap_halves(g, MLA_NOPE) * sin for g in _lane_groups(dq)], axis=1),)

    (dq_a,) = _rowwise(rope_q_bwd, [dq_mla, rope["cos_q"], rope["sin_k64_t"]], [(1024, BF16)], name="rope_q_bwd", rows=s)
    grads["w_q_b"] = _mm(n_q, dq_a, "tn", name="g_w_q_b")
    dn_q = _mm(dq_a, w["w_q_b"], "nt", name="d_n_q")

    def rope_k_bwd(dk, cos, sin):
        groups = _lane_groups(dk)
        g = groups[0]
        for other in groups[1:]:
            g = g + other
        d_rope = pltpu.roll(g * cos + _swap_halves(g, MLA_NOPE) * sin, MLA_NOPE, axis=1)
        nope = _low_half(g.shape)
        return d_rope, jnp.concatenate([jnp.where(nope, grp, 0.0) for grp in groups], axis=1)

    dk_rope, dk_nope = _rowwise(rope_k_bwd, [dk_mla, rope["cos_k64"], rope["sin_k64_t"]], [(128, BF16), (1024, BF16)],
                                name="rope_k_bwd", rows=s)
    grads["w_kv_b"] = jnp.concatenate([_mm(n_kv, dk_nope, "tn", name="g_w_kv_b_k"), _mm(n_kv, dv_a, "tn", name="g_w_kv_b_v")], axis=1)
    dn_kv = _mm(dk_nope, w["w_kv_b"][:, :1024], "nt", name="d_n_kv_k")
    dn_kv = _mm(dv_a, w["w_kv_b"][:, 1024:], "nt", name="d_n_kv_v", add=dn_kv)

    def rms_bwd(c_q, c_kv, dq, dkv, g_q, g_kv):
        res = []
        for c, dn, g in ((c_q, dq, g_q), (c_kv, dkv, g_kv)):
            r = lax.rsqrt(jnp.mean(c * c, axis=1, keepdims=True) + RMS_EPS)
            t = dn * g
            res += [r * t - c * (r * r * r) * jnp.mean(c * t, axis=1, keepdims=True), _colsum(dn * c * r)]
        return res

    dc_q, grads["q_a_gain"], dc_kv, grads["kv_a_gain"] = _rowwise(
        rms_bwd, [(proj, 256, COL_CQ), (proj, 128, COL_CKV), dn_q, dn_kv, w["q_a_gain"], w["kv_a_gain"]],
        [(256, BF16), ("sum", 256), (128, BF16), ("sum", 128)], name="rms_bwd", rows=s)

    dproj = jnp.concatenate(
        [dc_q, dc_kv, dk_rope, d_gate["mla"], dq_sb.astype(BF16), dk_sb.astype(BF16), dv_sb.astype(BF16), d_gate["sb"],
         dq_m.astype(BF16), d_gate["mem"]], axis=1)
    grads["w_in"] = _mm(xb, dproj, "tn", name="g_w_in")
    grad_x = _mm(dproj, w["w_in"], "nt", name="grad_x", add=dx)
    return loss, grad_x, grads


def _shard_shape(shape, axis):
    return tuple(d // N_DEV if a == axis else d for a, d in enumerate(shape))


def _pack(shards, dtype):
    flat = [a.reshape(-1).astype(dtype) for a in shards]
    n = sum(a.size for a in flat)
    flat.append(jnp.zeros((PACK_ROWS * PACK_COLS - n,), dtype))
    return jnp.concatenate(flat).reshape(PACK_ROWS, PACK_COLS)


def _unpack_shards(packed):
    flat, res, off = packed.reshape(-1), [], 0
    for _, shape, axis in SHARDED:
        shp = _shard_shape(shape, axis)
        n = shp[0] * shp[1]
        res.append(flat[off:off + n].reshape(shp))
        off += n
    return res


def _unpack_full(gathered):
    flat, res, off = gathered.reshape(N_DEV, -1), {}, 0
    for name, shape, axis in SHARDED:
        shp = _shard_shape(shape, axis)
        n = shp[0] * shp[1]
        blocks = flat[:, off:off + n].reshape(N_DEV, *shp)
        res[name] = blocks.reshape(shape) if axis == 0 else blocks.transpose(1, 0, 2).reshape(shape)
        off += n
    return res


def _pack_blocks(full):
    parts = []
    for name, shape, axis in SHARDED:
        shp = _shard_shape(shape, axis)
        g = full[name]
        g = g.reshape(N_DEV, -1) if axis == 0 else g.reshape(shape[0], N_DEV, shp[1]).transpose(1, 0, 2).reshape(N_DEV, -1)
        parts.append(g.astype(BF16))
    n = sum(p.shape[1] for p in parts)
    parts.append(jnp.zeros((N_DEV, PACK_ROWS * PACK_COLS - n), BF16))
    return jnp.concatenate(parts, axis=1).reshape(N_DEV, PACK_ROWS, PACK_COLS)


def _pad_heads(a, used):
    rows = a.shape[0]
    a = a.reshape(rows, MLA_HEADS, used)
    return jnp.concatenate([a, jnp.zeros((rows, MLA_HEADS, LANES - used), a.dtype)], axis=2).reshape(rows, MLA_HEADS * LANES)


def _to_kernel_layout(full):
    w = dict(full)
    w_in = full["w_in"]
    w["w_in"] = jnp.concatenate([w_in[:, :IN_REAL], jnp.zeros((D_MODEL, 512 - IN_REAL), w_in.dtype), w_in[:, IN_REAL:]], axis=1)
    w["w_q_b"] = _pad_heads(full["w_q_b"], MLA_NOPE + MLA_ROPE)
    kv = full["w_kv_b"].reshape(MLA_KV_LORA, MLA_HEADS, MLA_NOPE + MLA_V)
    w["w_kv_b"] = jnp.concatenate([_pad_heads(kv[:, :, :MLA_NOPE].reshape(MLA_KV_LORA, -1), MLA_NOPE),
                                   kv[:, :, MLA_NOPE:].reshape(MLA_KV_LORA, -1)], axis=1)
    return w


def _from_kernel_layout(g):
    r = dict(g)
    g_in = g["w_in"]
    r["w_in"] = jnp.concatenate([g_in[:, :IN_REAL], g_in[:, 512:]], axis=1)
    r["w_q_b"] = g["w_q_b"].reshape(MLA_Q_LORA, MLA_HEADS, LANES)[:, :, :MLA_NOPE + MLA_ROPE].reshape(MLA_Q_LORA, -1)
    kv = g["w_kv_b"]
    r["w_kv_b"] = jnp.concatenate([kv[:, :1024].reshape(MLA_KV_LORA, MLA_HEADS, LANES)[:, :, :MLA_NOPE],
                                   kv[:, 1024:].reshape(MLA_KV_LORA, MLA_HEADS, MLA_V)], axis=2).reshape(MLA_KV_LORA, -1)
    return r


def _pack_small(vectors, loss=None):
    flat = [v.reshape(-1) for v in vectors]
    flat.append(jnp.zeros((SMALL_ROWS * SMALL_LANES - LOSS_INDEX,), F32) if loss is None else
                jnp.concatenate([loss.reshape(-1)[:1], jnp.zeros((SMALL_ROWS * SMALL_LANES - LOSS_INDEX - 1,), F32)]))
    return jnp.concatenate(flat).reshape(SMALL_ROWS, SMALL_LANES)


def _unpack_small(packed):
    flat, res, off = packed.reshape(-1), [], 0
    for _, n in SMALL:
        res.append(flat[off:off + n].reshape(1, n))
        off += n
    return res


def _exchange(big, small, *, name):
    gather = big.ndim == 2
    blk_shape = big.shape if gather else big.shape[1:]
    n_sem = 7 * (1 if small is None else 2)

    def body(*refs):
        if small is None:
            big_ref, rbig_ref, send_sems, recv_sems, local_sems = refs
        else:
            big_ref, small_ref, rbig_ref, rsmall_ref, send_sems, recv_sems, local_sems = refs
        x, y, c = lax.axis_index("x"), lax.axis_index("y"), lax.axis_index("c")
        me = 4 * x + 2 * y + c
        copies = []
        for kk in range(1, N_DEV):
            px, py, pc = (x + (kk >> 2)) % 2, (y + ((kk >> 1) & 1)) % 2, (c + (kk & 1)) % 2
            peer = 4 * px + 2 * py + pc
            copies.append(pltpu.make_async_remote_copy(
                src_ref=big_ref if gather else big_ref.at[peer], dst_ref=rbig_ref.at[me],
                send_sem=send_sems.at[kk - 1], recv_sem=recv_sems.at[kk - 1],
                device_id=(px, py, pc), device_id_type=pl.DeviceIdType.MESH))
            if small is not None:
                copies.append(pltpu.make_async_remote_copy(
                    src_ref=small_ref, dst_ref=rsmall_ref.at[me], send_sem=send_sems.at[6 + kk], recv_sem=recv_sems.at[6 + kk],
                    device_id=(px, py, pc), device_id_type=pl.DeviceIdType.MESH))
        local = [pltpu.make_async_copy(big_ref if gather else big_ref.at[me], rbig_ref.at[me], local_sems.at[0])]
        if small is not None:
            local.append(pltpu.make_async_copy(small_ref, rsmall_ref.at[me], local_sems.at[1]))
        for cp in copies + local:
            cp.start()
        for cp in copies + local:
            cp.wait()

    hbm = pl.BlockSpec(memory_space=pl.ANY)
    out_shape = [jax.ShapeDtypeStruct((N_DEV, *blk_shape), big.dtype)]
    if small is not None:
        out_shape.append(jax.ShapeDtypeStruct((N_DEV, *small.shape), small.dtype))
    res = pl.pallas_call(
        body, name=name, in_specs=[hbm] * (1 if small is None else 2), out_specs=[hbm] * len(out_shape), out_shape=out_shape,
        scratch_shapes=[pltpu.SemaphoreType.DMA((n_sem,)), pltpu.SemaphoreType.DMA((n_sem,)), pltpu.SemaphoreType.DMA((2,))],
        compiler_params=pltpu.CompilerParams(has_side_effects=True),
    )(*((big,) if small is None else (big, small)))
    return res[0] if small is None else res


def _adamw(contrib, w, m, v, *, name, tile):
    rows, cols = w.shape

    def body(c_ref, w_ref, m_ref, v_ref, g_ref, d_ref, nm_ref, nv_ref):
        g = c_ref[0].astype(F32)
        for s in range(1, N_DEV):
            g = g + c_ref[s].astype(F32)
        m_new = ADAM_B1 * m_ref[...] + (1.0 - ADAM_B1) * g
        v_new = ADAM_B2 * v_ref[...] + (1.0 - ADAM_B2) * (g * g)
        m_hat = m_new / (1.0 - ADAM_B1 ** ADAM_STEP)
        v_hat = v_new / (1.0 - ADAM_B2 ** ADAM_STEP)
        g_ref[...] = g
        d_ref[...] = -ADAM_LR * (m_hat / (jnp.sqrt(v_hat) + ADAM_EPS) + ADAM_WD * w_ref[...])
        nm_ref[...] = m_new
        nv_ref[...] = v_new

    spec = pl.BlockSpec((tile, cols), lambda i: (i, 0))
    return pl.pallas_call(
        body, name=name, grid=(rows // tile,),
        in_specs=[pl.BlockSpec((N_DEV, tile, cols), lambda i: (0, i, 0)), spec, spec, spec], out_specs=[spec] * 4,
        out_shape=[jax.ShapeDtypeStruct((rows, cols), F32)] * 4, compiler_params=_cparams("parallel"),
    )(contrib, w, m, v)


def kernel(x, mem, w_in, w_mem_kv, q_a_gain, w_q_b, kv_a_gain, w_kv_b, w_branch_mla, w_branch_sb, w_branch_mem, w_merge_gate, b_merge_gate, w_out, ln_gain, ln_bias, loss_target, m_w_in, m_w_mem_kv, m_q_a_gain, m_w_q_b, m_kv_a_gain, m_w_kv_b, m_w_branch_mla, m_w_branch_sb, m_w_branch_mem, m_w_merge_gate, m_b_merge_gate, m_w_out, m_ln_gain, m_ln_bias, v_w_in, v_w_mem_kv, v_q_a_gain, v_w_q_b, v_kv_a_gain, v_w_kv_b, v_w_branch_mla, v_w_branch_sb, v_w_branch_mem, v_w_merge_gate, v_b_merge_gate, v_w_out, v_ln_gain, v_ln_bias):
    given = dict(locals())
    sharded_names = [n for n, _, _ in SHARDED]
    small_names = [n for n, _ in SMALL]
    shards = lambda prefix: [given[prefix + n][0] for n in sharded_names]
    smalls = lambda prefix: [given[prefix + n] for n in small_names]

    gathered = _exchange(_pack(shards(""), BF16), None, name="gather_weights")
    w = _to_kernel_layout(_unpack_full(gathered))
    for n in small_names:
        w[n] = given[n]
    loss, grad_x, grads = _local_step(x[0], mem[0], loss_target[0], w)

    blocks = _pack_blocks(_from_kernel_layout({n: grads[n] for n in sharded_names}))
    contrib, contrib_small = _exchange(blocks, _pack_small([grads[n] for n in small_names], loss), name="exchange_grads")

    big = _adamw(contrib, _pack(shards(""), F32), _pack(shards("m_"), F32), _pack(shards("v_"), F32), name="adamw", tile=PACK_TILE)
    sml = _adamw(contrib_small, _pack_small(smalls("")), _pack_small(smalls("m_")), _pack_small(smalls("v_")), name="adamw_small",
                 tile=SMALL_ROWS)
    per_kind = []
    for packed_big, packed_small in zip(big, sml, strict=True):
        by_name = dict(zip(sharded_names, [a[None] for a in _unpack_shards(packed_big)], strict=True))
        by_name.update(zip(small_names, _unpack_small(packed_small), strict=True))
        per_kind.append(by_name)
    order = ["w_in", "w_mem_kv", "q_a_gain", "w_q_b", "kv_a_gain", "w_kv_b", "w_branch_mla", "w_branch_sb", "w_branch_mem",
             "w_merge_gate", "b_merge_gate", "w_out", "ln_gain", "ln_bias"]
    loss_out = sml[0].reshape(-1)[LOSS_INDEX]
    return (loss_out, grad_x[None], *[kind[n] for kind in per_kind for n in order])
```

```python
import math

import jax
import jax.numpy as jnp
from jax import lax
from jax.experimental import pallas as pl
from jax.experimental.pallas import tpu as pltpu

F32, BF16 = jnp.float32, jnp.bfloat16

N_DEV = 8
D_MODEL = 1024
MLA_HEADS, MLA_NOPE, MLA_ROPE, MLA_V = 8, 64, 32, 64
MLA_Q_LORA, MLA_KV_LORA = 256, 128
SB_HEAD_DIM = 64
MEM_HEAD_DIM = 128
ROPE_BASE = 10000.0
RMS_EPS = 1e-6
LN_EPS = 1e-5
DEEPNORM_ALPHA = 2.0 ** 0.25
ADAM_LR, ADAM_B1, ADAM_B2, ADAM_EPS, ADAM_WD, ADAM_STEP = 0.001, 0.9, 0.999, 1e-08, 0.01, 10
LOG2E, LN2 = math.log2(math.e), math.log(2.0)

LANES = 128
GROUPS = 4
PROJ_WIDTH = 4096
COL_CQ, COL_CKV, COL_KROPE, COL_GATE_A = 0, 256, 384, 512
COL_QB, COL_KB, COL_VB, COL_GATE_B, COL_QM, COL_GATE_M = 1024, 1536, 2048, 2560, 3072, 3584
IN_REAL = 416

VMEM_LIMIT_BYTES = 56 * 1024 * 1024
NEG_BIG = -1e30
Q_BLOCK = 256
KEY_CHUNK = 512

SHARDED = {
    "w_in": ((1024, 4000), 1), "w_mem_kv": ((1024, 1024), 0), "w_q_b": ((256, 768), 1), "w_kv_b": ((128, 1024), 1),
    "w_branch_mla": ((512, 1024), 1), "w_branch_sb": ((512, 1024), 1), "w_branch_mem": ((512, 1024), 1),
    "w_merge_gate": ((1024, 3072), 1), "w_out": ((1024, 1024), 0),
}
GATHER_GROUPS = (("w_in",), ("w_merge_gate",), ("w_q_b", "w_kv_b", "w_mem_kv", "w_branch_mla", "w_branch_sb", "w_branch_mem", "w_out"))
GRAD_GROUPS = (("w_out", "w_merge_gate", "w_branch_mla", "w_branch_sb", "w_branch_mem"), ("w_mem_kv", "w_q_b", "w_kv_b"), ("w_in",))
SMALL = (("q_a_gain", 256), ("kv_a_gain", 128), ("b_merge_gate", 3072), ("ln_gain", 1024), ("ln_bias", 1024))
SMALL_ROWS, SMALL_LANES = 48, 128
ADAM_ROWS = 256
LOSS_INDEX = 5504


def _cparams(*sem):
    return pltpu.CompilerParams(dimension_semantics=sem or None, vmem_limit_bytes=VMEM_LIMIT_BYTES)


_DIMS = {"nn": (((1,), (0,)), ((), ())), "nt": (((1,), (1,)), ((), ())), "tn": (((0,), (0,)), ((), ()))}


def _dot(a, b, dims):
    return lax.dot_general(a, b, _DIMS[dims], preferred_element_type=F32)


def _tile(dim, want):
    if dim <= want:
        return dim
    t = want - want % LANES
    while dim % t:
        t -= LANES
    assert t > 0, (dim, want)
    return t


def _mm(a, b, dims, *, name, out_dtype=F32, add=None, add_scale=1.0, tm=1024, tn=1024, tk=1024):
    if dims == "nn":
        (m, k), (k2, n) = a.shape, b.shape
    elif dims == "nt":
        (m, k), (n, k2) = a.shape, b.shape
    else:
        (k, m), (k2, n) = a.shape, b.shape
    assert k == k2, (a.shape, b.shape, dims)
    tm, tn, tk = _tile(m, tm), _tile(n, tn), _tile(k, tk)
    nk = k // tk
    a_spec = pl.BlockSpec((tk, tm), lambda i, j, kk: (kk, i)) if dims == "tn" else pl.BlockSpec((tm, tk), lambda i, j, kk: (i, kk))
    b_spec = pl.BlockSpec((tn, tk), lambda i, j, kk: (j, kk)) if dims == "nt" else pl.BlockSpec((tk, tn), lambda i, j, kk: (kk, j))
    o_spec = pl.BlockSpec((tm, tn), lambda i, j, kk: (i, j))

    def body(*refs):
        a_ref, b_ref = refs[:2]
        add_ref = refs[2] if add is not None else None
        o_ref = refs[3 if add is not None else 2]
        part = _dot(a_ref[...].astype(BF16), b_ref[...].astype(BF16), dims)

        def finish(r):
            if add is not None:
                r = r + add_scale * add_ref[...]
            o_ref[...] = r.astype(out_dtype)

        if nk == 1:
            finish(part)
            return
        acc = refs[-1]
        kk = pl.program_id(2)

        @pl.when(kk == 0)
        def _():
            acc[...] = part

        @pl.when(kk > 0)
        def _():
            acc[...] += part

        @pl.when(kk == nk - 1)
        def _():
            finish(acc[...])

    return pl.pallas_call(
        body, name=name, grid=(m // tm, n // tn, nk),
        in_specs=[a_spec, b_spec] + ([o_spec] if add is not None else []), out_specs=o_spec,
        out_shape=jax.ShapeDtypeStruct((m, n), out_dtype), scratch_shapes=[pltpu.VMEM((tm, tn), F32)] if nk > 1 else [],
        compiler_params=_cparams("parallel", "parallel", "arbitrary"),
    )(*((a, b) if add is None else (a, b, add)))


def _rowwise(fn, ins, outs, *, name, rows, tr=256):
    n_in = len(ins)
    in_specs, args = [], []
    for it in ins:
        arr, w, off = it if isinstance(it, tuple) else (it, it.shape[-1], 0)
        assert off % w == 0
        cb = off // w
        if arr.shape[0] == 1:
            in_specs.append(pl.BlockSpec((1, w), lambda i, cb=cb: (0, cb)))
        else:
            in_specs.append(pl.BlockSpec((tr, w), lambda i, cb=cb: (i, cb)))
        args.append(arr)
    out_shape, out_specs, is_sum = [], [], []
    for kind, d in outs:
        if kind == "sum":
            out_shape.append(jax.ShapeDtypeStruct((1, d), F32))
            out_specs.append(pl.BlockSpec((1, d), lambda i: (0, 0)))
            is_sum.append(True)
        else:
            out_shape.append(jax.ShapeDtypeStruct((rows, kind), d))
            out_specs.append(pl.BlockSpec((tr, kind), lambda i: (i, 0)))
            is_sum.append(False)

    def body(*refs):
        res = fn(*[r[...] for r in refs[:n_in]])
        for r, val, s in zip(refs[n_in:], res, is_sum, strict=True):
            if s:
                @pl.when(pl.program_id(0) == 0)
                def _(r=r):
                    r[...] = jnp.zeros_like(r)

                r[...] += val
            else:
                r[...] = val.astype(r.dtype)

    return pl.pallas_call(
        body, name=name, grid=(rows // tr,), in_specs=in_specs, out_specs=out_specs, out_shape=out_shape,
        compiler_params=_cparams("arbitrary"),
    )(*args)


def _colsum(v):
    return jnp.sum(v, axis=0, keepdims=True)


def _sigmoid(v):
    return 1.0 / (1.0 + jnp.exp(-v))


def _lane_groups(v):
    return [v[:, g * LANES:(g + 1) * LANES] for g in range(v.shape[1] // LANES)]


def _swap_halves(v, first_lane):
    lane = lax.broadcasted_iota(jnp.int32, v.shape, 1)
    return jnp.where(lane < first_lane + 16, pltpu.roll(v, 112, axis=1), pltpu.roll(v, 16, axis=1))


def _lane_sum(acc, v):
    for part in _lane_groups(v):
        acc = acc + part
    return acc


def _low_half(shape):
    return lax.broadcasted_iota(jnp.int32, shape, 1) < LANES // 2


def _select_heads(per_head, pick):
    if len(per_head) == 1:
        return pick(per_head[0], 0)
    return jnp.where(_low_half(per_head[0].shape), pick(per_head[0], 0), pick(per_head[1], 1))


def _attn_specs(s, sk, hp, bq, q0, k0, v0):
    wq = hp * LANES
    assert q0 % wq == 0 and k0 % wq == 0 and v0 % LANES == 0
    qb0, kb0, vb0 = q0 // wq, k0 // wq, v0 // LANES
    q_spec = pl.BlockSpec((bq, wq), lambda g, i: (i, qb0 + g))
    k_spec = pl.BlockSpec((sk, wq), lambda g, i: (0, kb0 + g))
    v_spec = pl.BlockSpec((sk, LANES), lambda g, i: (0, vb0 + g))
    row_out = lambda w: pl.BlockSpec((bq, w), lambda g, i: (i, g))
    key_out = lambda w: pl.BlockSpec((sk, w), lambda g, i: (0, g))
    return q_spec, k_spec, v_spec, row_out, key_out


def _chunks(i, bq, ch, sk, causal):
    return ((i + 1) * bq - 1) // ch if causal else jnp.int32(sk // ch - 1)


def _positions(i, c, bq, ch):
    return (i * bq + lax.broadcasted_iota(jnp.int32, (bq, ch), 0), c * ch + lax.broadcasted_iota(jnp.int32, (bq, ch), 1))


def _softmax_fwd(q, k, v, *, hp, causal, name, q0=0, k0=0, v0=0):
    s, sk = q.shape[0], k.shape[0]
    bq, ch = min(Q_BLOCK, s), min(KEY_CHUNK, sk)
    q_spec, k_spec, v_spec, row_out, _ = _attn_specs(s, sk, hp, bq, q0, k0, v0)

    def body(q_ref, k_ref, v_ref, o_ref, lse_ref, s_scr):
        i = pl.program_id(1)
        qs = _lane_groups(q_ref[...])
        last = _chunks(i, bq, ch, sk, causal)

        def scores(c, ms, masked):
            off = pl.multiple_of(c * ch, ch)
            out = []
            for j in range(hp):
                sc = _dot(qs[j], k_ref[pl.ds(off, ch), j * LANES:(j + 1) * LANES], "nt")
                if masked:
                    qpos, kpos = _positions(i, c, bq, ch)
                    sc = jnp.where(kpos <= qpos, sc, NEG_BIG)
                s_scr[j, c] = sc
                m = ms[j]
                for part in _lane_groups(sc):
                    m = jnp.maximum(m, part)
                out.append(m)
            return tuple(out)

        ms = lax.fori_loop(0, last, lambda c, m: scores(c, m, False), tuple(jnp.full((bq, LANES), NEG_BIG, F32) for _ in range(hp)))
        ms = scores(last, ms, causal)
        row_max = [jnp.max(m, axis=1, keepdims=True) for m in ms]

        def weigh(c, carry):
            off = pl.multiple_of(c * ch, ch)
            vt = v_ref[pl.ds(off, ch), :]
            out = []
            for j in range(hp):
                l, acc = carry[j]
                p = jnp.exp2(s_scr[j, c] - row_max[j])
                out.append((_lane_sum(l, p), acc + _dot(p.astype(BF16), vt, "nn")))
            return tuple(out)

        zero = jnp.zeros((bq, LANES), F32)
        res = lax.fori_loop(0, last + 1, weigh, tuple((zero, zero) for _ in range(hp)))
        row_sum = [jnp.sum(l, axis=1, keepdims=True) for l, _ in res]
        o_ref[...] = _select_heads([acc for _, acc in res], lambda acc, j: acc / row_sum[j])
        lse_ref[...] = _select_heads([jnp.broadcast_to(row_max[j] + jnp.log2(row_sum[j]), (bq, LANES)) for j in range(hp)], lambda a, j: a)

    return pl.pallas_call(
        body, name=name, grid=(GROUPS, s // bq), in_specs=[q_spec, k_spec, v_spec], out_specs=[row_out(LANES), row_out(LANES)],
        out_shape=[jax.ShapeDtypeStruct((s, GROUPS * LANES), F32)] * 2,
        scratch_shapes=[pltpu.VMEM((hp, sk // ch, bq, ch), F32)], compiler_params=_cparams("parallel", "arbitrary"),
    )(q, k, v)


def _head_cotangent(do, j, hp):
    if hp == 1:
        return do
    return jnp.where(_low_half(do.shape) == (j == 0), do, 0.0)


def _softmax_bwd(q, k, v, o, do, lse, *, hp, causal, dq_scale, name, q0=0, k0=0, v0=0):
    s, sk = q.shape[0], k.shape[0]
    bq, ch = min(Q_BLOCK, s), min(KEY_CHUNK, sk)
    wq = hp * LANES
    q_spec, k_spec, v_spec, row_out, key_out = _attn_specs(s, sk, hp, bq, q0, k0, v0)

    def body(q_ref, k_ref, v_ref, o_ref, do_ref, lse_ref, dq_ref, dk_ref, dv_ref):
        i = pl.program_id(1)

        @pl.when(i == 0)
        def _():
            dk_ref[...] = jnp.zeros_like(dk_ref)
            dv_ref[...] = jnp.zeros_like(dv_ref)

        qs = _lane_groups(q_ref[...])
        do_all, o_all, lse_all = do_ref[...], o_ref[...], lse_ref[...]
        dos, deltas, lses = [], [], []
        for j in range(hp):
            d = _head_cotangent(do_all, j, hp)
            deltas.append(jnp.sum(d * o_all, axis=1, keepdims=True))
            dos.append(d.astype(BF16))
            lses.append(lse_all[:, j * (LANES // hp):j * (LANES // hp) + 1])
        last = _chunks(i, bq, ch, sk, causal)

        def chunk(c, dqs, masked):
            off = pl.multiple_of(c * ch, ch)
            vt = v_ref[pl.ds(off, ch), :]
            out, dks, dv = [], [], None
            for j in range(hp):
                kt = k_ref[pl.ds(off, ch), j * LANES:(j + 1) * LANES]
                p = jnp.exp2(_dot(qs[j], kt, "nt") - lses[j])
                if masked:
                    qpos, kpos = _positions(i, c, bq, ch)
                    p = jnp.where(kpos <= qpos, p, 0.0)
                ds = (p * (_dot(dos[j], vt, "nt") - deltas[j]) * LN2).astype(BF16)
                out.append(dqs[j] + _dot(ds, kt, "nn"))
                dks.append(_dot(ds, qs[j], "tn"))
                dvj = _dot(p.astype(BF16), dos[j], "tn")
                dv = dvj if dv is None else dv + dvj
            dk_ref[pl.ds(off, ch), :] += dks[0] if hp == 1 else jnp.concatenate(dks, axis=1)
            dv_ref[pl.ds(off, ch), :] += dv
            return tuple(out)

        dqs = lax.fori_loop(0, last, lambda c, d: chunk(c, d, False), tuple(jnp.zeros((bq, LANES), F32) for _ in range(hp)))
        dqs = chunk(last, dqs, causal)
        dq_ref[...] = (dqs[0] if hp == 1 else jnp.concatenate(dqs, axis=1)) * dq_scale

    return pl.pallas_call(
        body, name=name, grid=(GROUPS, s // bq),
        in_specs=[q_spec, k_spec, v_spec, row_out(LANES), row_out(LANES), row_out(LANES)],
        out_specs=[row_out(wq), key_out(wq), key_out(LANES)],
        out_shape=[jax.ShapeDtypeStruct((s, GROUPS * wq), F32), jax.ShapeDtypeStruct((sk, GROUPS * wq), F32),
                   jax.ShapeDtypeStruct((sk, GROUPS * LANES), F32)],
        compiler_params=_cparams("arbitrary", "arbitrary"),
    )(q, k, v, o, do, lse)


def _log_sigmoid_pair(z):
    sp = jnp.log(1.0 + jnp.exp(-jnp.abs(z)))
    return jnp.minimum(z, 0.0) - sp, jnp.minimum(-z, 0.0) - sp


def _tri_sum(v, tri):
    hi = v.astype(BF16)
    lo = (v - hi.astype(F32)).astype(BF16)
    return _dot(hi, tri, "nn") + _dot(lo, tri, "nn")


def _tri(n, after):
    rows, cols = lax.broadcasted_iota(jnp.int32, (n, n), 0), lax.broadcasted_iota(jnp.int32, (n, n), 1)
    return (rows > cols if after else rows < cols).astype(BF16)


def _running_sums(v, start, tri, backwards):
    n = tri.shape[0]
    blocks = [v[:, t * n:(t + 1) * n] for t in range(v.shape[1] // n)]
    order = range(len(blocks) - 1, -1, -1) if backwards else range(len(blocks))
    parts, run = [None] * len(blocks), start
    for t in order:
        parts[t] = _tri_sum(blocks[t], tri) + run
        run = run + jnp.sum(blocks[t], axis=1, keepdims=True)
    return (parts[0] if len(parts) == 1 else jnp.concatenate(parts, axis=1)), run


def _sb_weights(qm, kt, run, tri, strict):
    log_beta, log_keep = _log_sigmoid_pair(_dot(qm, kt, "nt"))
    if strict is not None:
        log_keep = jnp.where(strict, log_keep, 0.0)
    behind, run = _running_sums(log_keep, run, tri, True)
    a = jnp.exp(log_beta + behind)
    if strict is not None:
        a = jnp.where(strict, a, 0.0)
    return a, log_beta, run


def _sb_queries(q_all):
    low = _low_half(q_all.shape)
    zero = jnp.zeros_like(q_all)
    return [jnp.where(low, q_all, zero), jnp.where(low, zero, q_all)]


def _sb_fwd(qkv, *, q0, k0, v0, name):
    s = qkv.shape[0]
    bq, ch = min(Q_BLOCK, s), min(KEY_CHUNK, s)
    q_spec, k_spec, v_spec, row_out, _ = _attn_specs(s, s, 1, bq, q0, k0, v0)

    def body(q_ref, k_ref, v_ref, o_ref):
        i = pl.program_id(1)
        qms = _sb_queries(q_ref[...])
        tri = _tri(bq, True)
        last = _chunks(i, bq, ch, s, True)

        def chunk(c, carry, masked):
            off = pl.multiple_of(c * ch, ch)
            kt, vt = k_ref[pl.ds(off, ch), :], v_ref[pl.ds(off, ch), :]
            strict = None
            if masked:
                qpos, kpos = _positions(i, c, bq, ch)
                strict = kpos < qpos
            out = []
            for j in range(2):
                run, acc = carry[j]
                a, _, run = _sb_weights(qms[j], kt, run, tri, strict)
                out.append((run, acc + _dot(a.astype(BF16), vt, "nn")))
            return tuple(out)

        carry = chunk(last, tuple((jnp.zeros((bq, 1), F32), jnp.zeros((bq, LANES), F32)) for _ in range(2)), True)
        res = lax.fori_loop(0, last, lambda n, c: chunk(last - 1 - n, c, False), carry)
        o_ref[...] = _select_heads([acc for _, acc in res], lambda acc, j: acc)

    return pl.pallas_call(
        body, name=name, grid=(GROUPS, s // bq), in_specs=[q_spec, k_spec, v_spec], out_specs=row_out(LANES),
        out_shape=jax.ShapeDtypeStruct((s, GROUPS * LANES), F32), compiler_params=_cparams("parallel", "arbitrary"),
    )(qkv, qkv, qkv)


def _sb_bwd(qkv, do, *, q0, k0, v0, dq_scale, name):
    s = qkv.shape[0]
    bq, ch = min(Q_BLOCK, s), min(KEY_CHUNK, s)
    q_spec, k_spec, v_spec, row_out, key_out = _attn_specs(s, s, 1, bq, q0, k0, v0)

    def body(q_ref, k_ref, v_ref, do_ref, dq_ref, dk_ref, dv_ref, g_s, beta_s):
        i = pl.program_id(1)

        @pl.when(i == 0)
        def _():
            dk_ref[...] = jnp.zeros_like(dk_ref)
            dv_ref[...] = jnp.zeros_like(dv_ref)

        qms = _sb_queries(q_ref[...])
        do_all = do_ref[...]
        dos = [_head_cotangent(do_all, j, 2).astype(BF16) for j in range(2)]
        tri_after, tri_before = _tri(bq, True), _tri(bq, False)
        last = _chunks(i, bq, ch, s, True)

        def strict_mask(c):
            qpos, kpos = _positions(i, c, bq, ch)
            return kpos < qpos

        def sweep1(c, runs, masked):
            off = pl.multiple_of(c * ch, ch)
            kt, vt = k_ref[pl.ds(off, ch), :], v_ref[pl.ds(off, ch), :]
            strict = strict_mask(c) if masked else None
            out, dv = [], None
            for j in range(2):
                a, log_beta, run = _sb_weights(qms[j], kt, runs[j], tri_after, strict)
                g_s[j, c] = a * _dot(dos[j], vt, "nt")
                beta_s[j, c] = jnp.exp(log_beta)
                dvj = _dot(a.astype(BF16), dos[j], "tn")
                dv = dvj if dv is None else dv + dvj
                out.append(run)
            dv_ref[pl.ds(off, ch), :] += dv
            return tuple(out)

        runs = sweep1(last, tuple(jnp.zeros((bq, 1), F32) for _ in range(2)), True)
        lax.fori_loop(0, last, lambda n, r: sweep1(last - 1 - n, r, False), runs)

        def sweep2(c, carry, masked):
            off = pl.multiple_of(c * ch, ch)
            kt = k_ref[pl.ds(off, ch), :]
            out, dk = [], None
            for j in range(2):
                before, dq = carry[j]
                g, beta = g_s[j, c], beta_s[j, c]
                in_front, before = _running_sums(g, before, tri_before, False)
                dz = g * (1.0 - beta) - beta * in_front
                if masked:
                    dz = jnp.where(strict_mask(c), dz, 0.0)
                dz = dz.astype(BF16)
                dkj = _dot(dz, qms[j], "tn")
                dk = dkj if dk is None else dk + dkj
                out.append((before, dq + _dot(dz, kt, "nn")))
            dk_ref[pl.ds(off, ch), :] += dk
            return tuple(out)

        carry = lax.fori_loop(0, last, lambda c, cr: sweep2(c, cr, False),
                              tuple((jnp.zeros((bq, 1), F32), jnp.zeros((bq, LANES), F32)) for _ in range(2)))
        res = sweep2(last, carry, True)
        dq_ref[...] = _select_heads([dq for _, dq in res], lambda dq, j: dq) * dq_scale

    n_ch = s // ch
    return pl.pallas_call(
        body, name=name, grid=(GROUPS, s // bq), in_specs=[q_spec, k_spec, v_spec, row_out(LANES)],
        out_specs=[row_out(LANES), key_out(LANES), key_out(LANES)],
        out_shape=[jax.ShapeDtypeStruct((s, GROUPS * LANES), F32)] * 3,
        scratch_shapes=[pltpu.VMEM((2, n_ch, bq, ch), F32), pltpu.VMEM((2, n_ch, bq, ch), F32)],
        compiler_params=_cparams("arbitrary", "arbitrary"),
    )(qkv, qkv, qkv, do)


def _rope_tables(s):
    half = MLA_ROPE // 2
    freqs = ROPE_BASE ** (-jnp.arange(half, dtype=F32) / half)
    ang = jnp.arange(s, dtype=F32)[:, None] * freqs[None, :]
    cos, sin = jnp.cos(ang), jnp.sin(ang)
    tail = jnp.zeros((s, LANES - MLA_NOPE - MLA_ROPE), F32)
    lead = lambda fill: jnp.full((s, MLA_NOPE), fill, F32)
    return dict(
        cos_k0=jnp.concatenate([cos, cos, lead(0.0), tail], axis=1), sin_k0=jnp.concatenate([-sin, sin, lead(0.0), tail], axis=1),
        cos_k64=jnp.concatenate([lead(0.0), cos, cos, tail], axis=1), sin_k64=jnp.concatenate([lead(0.0), -sin, sin, tail], axis=1),
        cos_q=jnp.concatenate([lead(1.0), cos, cos, tail], axis=1),
        sin_k64_t=jnp.concatenate([lead(0.0), sin, -sin, tail], axis=1),
    )


def _local_step(x, mem, target, w, emit=lambda grads: None):
    s = x.shape[0]
    rope = _rope_tables(s)
    xb = x.astype(BF16)
    inv_d = 1.0 / D_MODEL
    scale_a = LOG2E / math.sqrt(MLA_NOPE + MLA_ROPE)
    scale_b = 1.0 / math.sqrt(SB_HEAD_DIM)
    scale_m = LOG2E / math.sqrt(MEM_HEAD_DIM)
    arrive_after = getattr(w, "arrive_after", lambda value: None)

    arrive_after(xb)
    proj = _mm(xb, w["w_in"], "nn", name="proj")
    arrive_after(proj)
    pre = _mm(xb, w["w_merge_gate"], "nn", name="merge_pre")
    arrive_after(pre)

    def rms_pair(c_q, c_kv, g_q, g_kv):
        return (c_q * lax.rsqrt(jnp.mean(c_q * c_q, axis=1, keepdims=True) + RMS_EPS) * g_q,
                c_kv * lax.rsqrt(jnp.mean(c_kv * c_kv, axis=1, keepdims=True) + RMS_EPS) * g_kv)

    n_q, n_kv = _rowwise(rms_pair, [(proj, 256, COL_CQ), (proj, 128, COL_CKV), w["q_a_gain"], w["kv_a_gain"]],
                         [(256, BF16), (128, BF16)], name="rms_pair", rows=s)
    q_a = _mm(n_q, w["w_q_b"], "nn", name="q_up")
    kv_a = _mm(n_kv, w["w_kv_b"], "nn", name="kv_up", out_dtype=BF16)

    def rope_q(qa, cos, sin):
        return (jnp.concatenate([(g * cos + _swap_halves(g, MLA_NOPE) * sin) * scale_a for g in _lane_groups(qa)], axis=1),)

    (q_mla,) = _rowwise(rope_q, [q_a, rope["cos_q"], rope["sin_k64"]], [(1024, BF16)], name="rope_q", rows=s)

    def rope_k(k_nope, k_rope, cos, sin):
        k_pe = pltpu.roll(k_rope * cos + _swap_halves(k_rope, 0) * sin, MLA_NOPE, axis=1).astype(BF16)
        return (jnp.concatenate([g + k_pe for g in _lane_groups(k_nope)], axis=1),)

    (k_mla,) = _rowwise(rope_k, [(kv_a, 1024, 0), (proj, 128, COL_KROPE), rope["cos_k0"], rope["sin_k0"]],
                        [(1024, BF16)], name="rope_k", rows=s)
    o_a, lse_a = _softmax_fwd(q_mla, k_mla, kv_a, hp=2, causal=True, name="mla_fwd", v0=1024)

    def cast_qkv(q_sb, k_sb, v_sb, q_m):
        return (jnp.concatenate([(q_sb * scale_b).astype(BF16), k_sb.astype(BF16), v_sb.astype(BF16), (q_m * scale_m).astype(BF16)], axis=1),)

    (qkv,) = _rowwise(cast_qkv, [(proj, 512, COL_QB), (proj, 512, COL_KB), (proj, 512, COL_VB), (proj, 512, COL_QM)],
                      [(2048, BF16)], name="cast_qkv", rows=s)
    o_b = _sb_fwd(qkv, q0=0, k0=512, v0=1024, name="sb_fwd")

    memb = mem.astype(BF16)
    mem_kv = _mm(memb, w["w_mem_kv"], "nn", name="mem_kv", out_dtype=BF16)
    o_m, lse_m = _softmax_fwd(qkv, mem_kv, mem_kv, hp=1, causal=False, name="mem_fwd", q0=1536, v0=512)

    o_br = {"mla": o_a, "sb": o_b, "mem": o_m}
    gate_col = {"mla": COL_GATE_A, "sb": COL_GATE_B, "mem": COL_GATE_M}

    def gated(o, gate):
        return (o * gate * _sigmoid(gate),)

    u, y = {}, {}
    for br in ("mla", "sb", "mem"):
        (u[br],) = _rowwise(gated, [o_br[br], (proj, 512, gate_col[br])], [(512, BF16)], name=f"gated_{br}", rows=s)
        y[br] = _mm(u[br], w[f"w_branch_{br}"], "nn", name=f"branch_{br}")

    def merge(pa, pb, pm, ba, bb, bm, ya, yb, ym):
        return (_sigmoid(pa + ba) * ya + _sigmoid(pb + bb) * yb + _sigmoid(pm + bm) * ym,)

    bias = w["b_merge_gate"]
    gate_ins = [(pre, 1024, 0), (pre, 1024, 1024), (pre, 1024, 2048), (bias, 1024, 0), (bias, 1024, 1024), (bias, 1024, 2048)]
    (merged,) = _rowwise(merge, gate_ins + [y["mla"], y["sb"], y["mem"]], [(1024, BF16)], name="merge", rows=s)
    out = _mm(merged, w["w_out"], "nn", name="out_proj")

    def norm_loss(xv, ov, tv, gain, bias_ln):
        z = DEEPNORM_ALPHA * xv + ov
        zc = z - jnp.mean(z, axis=1, keepdims=True)
        rstd = lax.rsqrt(jnp.mean(zc * zc, axis=1, keepdims=True) + LN_EPS)
        xhat = zc * rstd
        err = xhat * gain + bias_ln - tv
        loss = 0.5 * jnp.sum(jnp.mean(err * err, axis=1, keepdims=True), axis=0, keepdims=True)
        dy = err * inv_d
        dxhat = dy * gain
        dz = rstd * (dxhat - jnp.mean(dxhat, axis=1, keepdims=True) - xhat * jnp.mean(dxhat * xhat, axis=1, keepdims=True))
        return dz, dz, _colsum(dy * xhat), _colsum(dy), jnp.broadcast_to(loss, (1, LANES))

    dz, dzb, g_ln_gain, g_ln_bias, loss = _rowwise(
        norm_loss, [x, out, target, w["ln_gain"], w["ln_bias"]],
        [(1024, F32), (1024, BF16), ("sum", 1024), ("sum", 1024), ("sum", LANES)], name="norm_loss", rows=s)

    grads = {"ln_gain": g_ln_gain, "ln_bias": g_ln_bias}
    dmerged = _mm(dzb, w["w_out"], "nt", name="d_merged")
    grads["w_out"] = _mm(merged, dzb, "tn", name="g_w_out")

    def merge_bwd(dm, pa, pb, pm, ba, bb, bm, ya, yb, ym):
        res, dpre = [], []
        for p, b, yv in ((pa, ba, ya), (pb, bb, yb), (pm, bm, ym)):
            g = _sigmoid(p + b)
            dpre.append(dm * yv * g * (1.0 - g))
            res.append(dm * g)
        dpre = jnp.concatenate(dpre, axis=1)
        return dpre, _colsum(dpre), *res

    dpre, grads["b_merge_gate"], dy_a, dy_b, dy_m = _rowwise(
        merge_bwd, [dmerged] + gate_ins + [y["mla"], y["sb"], y["mem"]],
        [(3072, BF16), ("sum", 3072), (1024, BF16), (1024, BF16), (1024, BF16)], name="merge_bwd", rows=s, tr=128)
    grads["w_merge_gate"] = _mm(xb, dpre, "tn", name="g_w_merge")
    dx = _mm(dpre, w["w_merge_gate"], "nt", name="dx_merge", add=dz, add_scale=DEEPNORM_ALPHA)

    def gated_bwd(du, o, gate):
        sg = _sigmoid(gate)
        return du * gate * sg, du * o * sg * (1.0 + gate * (1.0 - sg))

    d_o, d_gate = {}, {}
    for br, dy in (("mla", dy_a), ("sb", dy_b), ("mem", dy_m)):
        grads[f"w_branch_{br}"] = _mm(u[br], dy, "tn", name=f"g_w_branch_{br}")
        du = _mm(dy, w[f"w_branch_{br}"], "nt", name=f"d_u_{br}")
        d_o[br], d_gate[br] = _rowwise(gated_bwd, [du, o_br[br], (proj, 512, gate_col[br])], [(512, F32), (512, BF16)],
                                       name=f"gated_bwd_{br}", rows=s)
    emit({n: grads[n] for n in ("w_out", "w_merge_gate", "w_branch_mla", "w_branch_sb", "w_branch_mem")})

    dq_m, dk_m, dv_m = _softmax_bwd(qkv, mem_kv, mem_kv, o_m, d_o["mem"], lse_m, hp=1, causal=False, dq_scale=scale_m,
                                    name="mem_bwd", q0=1536, v0=512)
    grads["w_mem_kv"] = _mm(memb, jnp.concatenate([dk_m, dv_m], axis=1), "tn", name="g_w_mem_kv")

    dq_sb, dk_sb, dv_sb = _sb_bwd(qkv, d_o["sb"], q0=0, k0=512, v0=1024, dq_scale=scale_b, name="sb_bwd")

    dq_mla, dk_mla, dv_a = _softmax_bwd(q_mla, k_mla, kv_a, o_a, d_o["mla"], lse_a, hp=2, causal=True, dq_scale=scale_a,
                                        name="mla_bwd", v0=1024)

    def rope_q_bwd(dq, cos, sin):
        return (jnp.concatenate([g * cos + _swap_halves(g, MLA_NOPE) * sin for g in _lane_groups(dq)], axis=1),)

    (dq_a,) = _rowwise(rope_q_bwd, [dq_mla, rope["cos_q"], rope["sin_k64_t"]], [(1024, BF16)], name="rope_q_bwd", rows=s)
    grads["w_q_b"] = _mm(n_q, dq_a, "tn", name="g_w_q_b")
    dn_q = _mm(dq_a, w["w_q_b"], "nt", name="d_n_q")

    def rope_k_bwd(dk, cos, sin):
        groups = _lane_groups(dk)
        g = groups[0]
        for other in groups[1:]:
            g = g + other
        d_rope = pltpu.roll(g * cos + _swap_halves(g, MLA_NOPE) * sin, MLA_NOPE, axis=1)
        nope = _low_half(g.shape)
        return d_rope, jnp.concatenate([jnp.where(nope, grp, 0.0) for grp in groups], axis=1)

    dk_rope, dk_nope = _rowwise(rope_k_bwd, [dk_mla, rope["cos_k64"], rope["sin_k64_t"]], [(128, BF16), (1024, BF16)],
                                name="rope_k_bwd", rows=s)
    grads["w_kv_b"] = jnp.concatenate([_mm(n_kv, dk_nope, "tn", name="g_w_kv_b_k"), _mm(n_kv, dv_a, "tn", name="g_w_kv_b_v")], axis=1)
    dn_kv = _mm(dk_nope, w["w_kv_b"][:, :1024], "nt", name="d_n_kv_k")
    dn_kv = _mm(dv_a, w["w_kv_b"][:, 1024:], "nt", name="d_n_kv_v", add=dn_kv)

    def rms_bwd(c_q, c_kv, dq, dkv, g_q, g_kv):
        res = []
        for c, dn, g in ((c_q, dq, g_q), (c_kv, dkv, g_kv)):
            r = lax.rsqrt(jnp.mean(c * c, axis=1, keepdims=True) + RMS_EPS)
            t = dn * g
            res += [r * t - c * (r * r * r) * jnp.mean(c * t, axis=1, keepdims=True), _colsum(dn * c * r)]
        return res

    dc_q, grads["q_a_gain"], dc_kv, grads["kv_a_gain"] = _rowwise(
        rms_bwd, [(proj, 256, COL_CQ), (proj, 128, COL_CKV), dn_q, dn_kv, w["q_a_gain"], w["kv_a_gain"]],
        [(256, BF16), ("sum", 256), (128, BF16), ("sum", 128)], name="rms_bwd", rows=s)

    emit({n: grads[n] for n in ("w_mem_kv", "w_q_b", "w_kv_b")})

    dproj = jnp.concatenate(
        [dc_q, dc_kv, dk_rope, d_gate["mla"], dq_sb.astype(BF16), dk_sb.astype(BF16), dv_sb.astype(BF16), d_gate["sb"],
         dq_m.astype(BF16), d_gate["mem"]], axis=1)
    grads["w_in"] = _mm(xb, dproj, "tn", name="g_w_in")
    emit({"w_in": grads["w_in"]})
    grad_x = _mm(dproj, w["w_in"], "nt", name="grad_x", add=dx)
    return loss, grad_x, grads


def _shard_shape(shape, axis):
    return tuple(d // N_DEV if a == axis else d for a, d in enumerate(shape))


def _from_blocks(blocks, name):
    shape, axis = SHARDED[name]
    return blocks.reshape(shape) if axis == 0 else blocks.transpose(1, 0, 2).reshape(shape)


def _to_blocks(full, name):
    shape, axis = SHARDED[name]
    shp = _shard_shape(shape, axis)
    return full.reshape(N_DEV, *shp) if axis == 0 else full.reshape(shape[0], N_DEV, shp[1]).transpose(1, 0, 2)


def _pad_heads(a, used):
    rows = a.shape[0]
    a = a.reshape(rows, MLA_HEADS, used)
    return jnp.concatenate([a, jnp.zeros((rows, MLA_HEADS, LANES - used), a.dtype)], axis=2).reshape(rows, MLA_HEADS * LANES)


def _to_kernel_layout(name, full):
    if name == "w_in":
        return jnp.concatenate([full[:, :IN_REAL], jnp.zeros((D_MODEL, 512 - IN_REAL), full.dtype), full[:, IN_REAL:]], axis=1)
    if name == "w_q_b":
        return _pad_heads(full, MLA_NOPE + MLA_ROPE)
    if name == "w_kv_b":
        kv = full.reshape(MLA_KV_LORA, MLA_HEADS, MLA_NOPE + MLA_V)
        return jnp.concatenate([_pad_heads(kv[:, :, :MLA_NOPE].reshape(MLA_KV_LORA, -1), MLA_NOPE),
                                kv[:, :, MLA_NOPE:].reshape(MLA_KV_LORA, -1)], axis=1)
    return full


def _from_kernel_layout(name, g):
    if name == "w_in":
        return jnp.concatenate([g[:, :IN_REAL], g[:, 512:]], axis=1)
    if name == "w_q_b":
        return g.reshape(MLA_Q_LORA, MLA_HEADS, LANES)[:, :, :MLA_NOPE + MLA_ROPE].reshape(MLA_Q_LORA, -1)
    if name == "w_kv_b":
        return jnp.concatenate([g[:, :1024].reshape(MLA_KV_LORA, MLA_HEADS, LANES)[:, :, :MLA_NOPE],
                                g[:, 1024:].reshape(MLA_KV_LORA, MLA_HEADS, MLA_V)], axis=2).reshape(MLA_KV_LORA, -1)
    return g


def _pack_small(vectors, loss=None):
    flat = [v.reshape(-1) for v in vectors]
    flat.append(jnp.zeros((SMALL_ROWS * SMALL_LANES - LOSS_INDEX,), F32) if loss is None else
                jnp.concatenate([loss.reshape(-1)[:1], jnp.zeros((SMALL_ROWS * SMALL_LANES - LOSS_INDEX - 1,), F32)]))
    return jnp.concatenate(flat).reshape(SMALL_ROWS, SMALL_LANES)


def _unpack_small(packed):
    flat, res, off = packed.reshape(-1), [], 0
    for _, n in SMALL:
        res.append(flat[off:off + n].reshape(1, n))
        off += n
    return res


def _me_and_peers():
    x, y, c = lax.axis_index("x"), lax.axis_index("y"), lax.axis_index("c")
    peers = []
    for kk in range(1, N_DEV):
        px, py, pc = (x + (kk >> 2)) % 2, (y + ((kk >> 1) & 1)) % 2, (c + (kk & 1)) % 2
        peers.append(((px, py, pc), 4 * px + 2 * py + pc))
    return 4 * x + 2 * y + c, peers


def _share_small(small, *, name):
    def body(small_ref, all_ref, send_sems, recv_sems, local_sem):
        me, peers = _me_and_peers()
        copies = [pltpu.make_async_remote_copy(src_ref=small_ref, dst_ref=all_ref.at[me], send_sem=send_sems.at[kk], recv_sem=recv_sems.at[kk],
                                               device_id=pos, device_id_type=pl.DeviceIdType.MESH) for kk, (pos, _) in enumerate(peers)]
        copies.append(pltpu.make_async_copy(small_ref, all_ref.at[me], local_sem))
        for cp in copies:
            cp.start()
        for cp in copies:
            cp.wait()

    hbm = pl.BlockSpec(memory_space=pl.ANY)
    return pl.pallas_call(
        body, name=name, in_specs=[hbm], out_specs=hbm, out_shape=jax.ShapeDtypeStruct((N_DEV, *small.shape), small.dtype),
        scratch_shapes=[pltpu.SemaphoreType.DMA((N_DEV - 1,)), pltpu.SemaphoreType.DMA((N_DEV - 1,)), pltpu.SemaphoreType.DMA],
        compiler_params=pltpu.CompilerParams(has_side_effects=True),
    )(small)


_HBM = pl.BlockSpec(memory_space=pltpu.HBM)
_SEM = pl.BlockSpec(memory_space=pltpu.SEMAPHORE)


def _exchange_copies(srcs, zones, send_sems, recv_sems, gather):
    me, peers = _me_and_peers()
    return [pltpu.make_async_remote_copy(
        src_ref=srcs[t] if gather else srcs[t].at[peer], dst_ref=zones[t].at[me], send_sem=send_sems.at[7 * t + kk],
        recv_sem=recv_sems.at[7 * t + kk], device_id=pos, device_id_type=pl.DeviceIdType.MESH)
        for t in range(len(srcs)) for kk, (pos, peer) in enumerate(peers)]


def _exchange_start(tensors, *, gather, name):
    n = len(tensors)
    zones = [lax.empty((N_DEV, *(t.shape if gather else t.shape[1:])), t.dtype) for t in tensors]

    def body(*refs):
        for cp in _exchange_copies(refs[:n], refs[n:2 * n], refs[2 * n], refs[2 * n + 1], gather):
            cp.start()
        refs[-1][...] = jnp.zeros_like(refs[-1])

    buffers = [pltpu.HBM(a.shape, a.dtype) for a in tensors + zones]
    res = pl.pallas_call(
        body, name=name, in_specs=[_HBM] * (2 * n),
        out_shape=(pltpu.SemaphoreType.DMA((7 * n,)), pltpu.SemaphoreType.DMA((7 * n,)), *buffers, jax.ShapeDtypeStruct((8, LANES), F32)),
        out_specs=(_SEM, _SEM, *[_HBM] * (2 * n), pl.BlockSpec(memory_space=pltpu.VMEM)),
        input_output_aliases={i: 2 + i for i in range(2 * n)},
        compiler_params=pltpu.CompilerParams(has_side_effects=pltpu.SideEffectType.DATAFLOW_SIDE_EFFECTING),
    )(*[pltpu.with_memory_space_constraint(a, pltpu.HBM) for a in tensors + zones])
    return dict(sems=res[:2], buffers=res[2:2 + 2 * n], gather=gather)


def _exchange_wait(started, after, *, name):
    n = len(started["buffers"]) // 2

    def body(*refs):
        for cp in _exchange_copies(refs[:n], refs[n:2 * n], refs[2 * n], refs[2 * n + 1], started["gather"]):
            cp.wait_send()
            cp.wait_recv()

    res = pl.pallas_call(
        body, name=name, in_specs=[_HBM] * (2 * n) + [_SEM, _SEM, pl.BlockSpec(memory_space=pl.ANY)],
        out_shape=tuple(pltpu.HBM(a.shape, a.dtype) for a in started["buffers"]), out_specs=tuple([_HBM] * (2 * n)),
        input_output_aliases={i: i for i in range(2 * n)},
        compiler_params=pltpu.CompilerParams(has_side_effects=pltpu.SideEffectType.DATAFLOW_SIDE_EFFECTING),
    )(*started["buffers"], *started["sems"], after)
    return res[:n], res[n:]


def _adamw(contrib, w, m, v, *, name):
    rows, cols = w.shape
    tile = min(rows, ADAM_ROWS)

    def body(c_ref, w_ref, m_ref, v_ref, g_ref, d_ref, nm_ref, nv_ref):
        g = c_ref[0].astype(F32)
        for s in range(1, N_DEV):
            g = g + c_ref[s].astype(F32)
        m_new = ADAM_B1 * m_ref[...] + (1.0 - ADAM_B1) * g
        v_new = ADAM_B2 * v_ref[...] + (1.0 - ADAM_B2) * (g * g)
        m_hat = m_new / (1.0 - ADAM_B1 ** ADAM_STEP)
        v_hat = v_new / (1.0 - ADAM_B2 ** ADAM_STEP)
        g_ref[...] = g
        d_ref[...] = -ADAM_LR * (m_hat / (jnp.sqrt(v_hat) + ADAM_EPS) + ADAM_WD * w_ref[...])
        nm_ref[...] = m_new
        nv_ref[...] = v_new

    spec = pl.BlockSpec((tile, cols), lambda i: (i, 0))
    return pl.pallas_call(
        body, name=name, grid=(rows // tile,),
        in_specs=[pl.BlockSpec((N_DEV, tile, cols), lambda i: (0, i, 0)), spec, spec, spec], out_specs=[spec] * 4,
        out_shape=[jax.ShapeDtypeStruct((rows, cols), F32)] * 4, compiler_params=_cparams("parallel"),
    )(contrib, w, m, v)


class _Weights:
    def __init__(self, gathers, vectors, me):
        self.gathers, self.ready, self.me, self.after = gathers, dict(vectors), me, None

    def arrive_after(self, value):
        self.after = value

    def __getitem__(self, name):
        if name not in self.ready:
            gi = next(i for i, group in enumerate(GATHER_GROUPS) if name in group)
            shards, zones = _exchange_wait(self.gathers[gi], self.after, name=f"gather_wait_{gi}")
            for n, shard, zone in zip(GATHER_GROUPS[gi], shards, zones, strict=True):
                blocks = lax.dynamic_update_slice_in_dim(zone, shard[None], self.me, 0)
                self.ready[n] = _to_kernel_layout(n, _from_blocks(blocks, n))
        return self.ready[name]


def kernel(x, mem, w_in, w_mem_kv, q_a_gain, w_q_b, kv_a_gain, w_kv_b, w_branch_mla, w_branch_sb, w_branch_mem, w_merge_gate, b_merge_gate, w_out, ln_gain, ln_bias, loss_target, m_w_in, m_w_mem_kv, m_q_a_gain, m_w_q_b, m_kv_a_gain, m_w_kv_b, m_w_branch_mla, m_w_branch_sb, m_w_branch_mem, m_w_merge_gate, m_b_merge_gate, m_w_out, m_ln_gain, m_ln_bias, v_w_in, v_w_mem_kv, v_q_a_gain, v_w_q_b, v_kv_a_gain, v_w_kv_b, v_w_branch_mla, v_w_branch_sb, v_w_branch_mem, v_w_merge_gate, v_b_merge_gate, v_w_out, v_ln_gain, v_ln_bias):
    given = dict(locals())
    small_names = [n for n, _ in SMALL]
    smalls = lambda prefix: [given[prefix + n] for n in small_names]
    me = 4 * lax.axis_index("x") + 2 * lax.axis_index("y") + lax.axis_index("c")

    gathers = [_exchange_start([given[n][0].astype(BF16) for n in group], gather=True, name=f"gather_start_{gi}")
               for gi, group in enumerate(GATHER_GROUPS)]
    w = _Weights(gathers, {n: given[n] for n in small_names}, me)
    exchanges = []

    def emit(grads):
        blocks = [_to_blocks(_from_kernel_layout(n, g), n).astype(BF16) for n, g in grads.items()]
        exchanges.append((tuple(grads), _exchange_start(blocks, gather=False, name=f"grads_start_{len(exchanges)}")))

    loss, grad_x, grads = _local_step(x[0], mem[0], loss_target[0], w, emit)

    contrib_small = _share_small(_pack_small([grads[n] for n in small_names], loss), name="share_small")
    sml = _adamw(contrib_small, _pack_small(smalls("")), _pack_small(smalls("m_")), _pack_small(smalls("v_")), name="adamw_small")
    results = [dict(zip(small_names, _unpack_small(packed), strict=True)) for packed in sml]
    for gi, (names, started) in enumerate(exchanges):
        sent, zones = _exchange_wait(started, grad_x, name=f"grads_wait_{gi}")
        for n, blocks, zone in zip(names, sent, zones, strict=True):
            own = lax.dynamic_index_in_dim(blocks, me, 0, keepdims=True)
            contrib = lax.dynamic_update_slice_in_dim(zone, own, me, 0)
            for kind, res in zip(results, _adamw(contrib, given[n][0], given["m_" + n][0], given["v_" + n][0], name=f"adamw_{n}"), strict=True):
                kind[n] = res[None]
    order = ["w_in", "w_mem_kv", "q_a_gain", "w_q_b", "kv_a_gain", "w_kv_b", "w_branch_mla", "w_branch_sb", "w_branch_mem",
             "w_merge_gate", "b_merge_gate", "w_out", "ln_gain", "ln_bias"]
    loss_out = sml[0].reshape(-1)[LOSS_INDEX]
    return (loss_out, grad_x[None], *[kind[n] for kind in results for n in order])
```

```python
import math

import jax
import jax.numpy as jnp
from jax import lax
from jax.experimental import pallas as pl
from jax.experimental.pallas import tpu as pltpu

F32, BF16 = jnp.float32, jnp.bfloat16

N_DEV = 8
D_MODEL = 1024
MLA_HEADS, MLA_NOPE, MLA_ROPE, MLA_V = 8, 64, 32, 64
MLA_Q_LORA, MLA_KV_LORA = 256, 128
SB_HEAD_DIM = 64
MEM_HEAD_DIM = 128
ROPE_BASE = 10000.0
RMS_EPS = 1e-6
LN_EPS = 1e-5
DEEPNORM_ALPHA = 2.0 ** 0.25
ADAM_LR, ADAM_B1, ADAM_B2, ADAM_EPS, ADAM_WD, ADAM_STEP = 0.001, 0.9, 0.999, 1e-08, 0.01, 10
LOG2E, LN2 = math.log2(math.e), math.log(2.0)

LANES = 128
GROUPS = 4
PROJ_WIDTH = 4096
COL_CQ, COL_CKV, COL_KROPE, COL_GATE_A = 0, 256, 384, 512
COL_QB, COL_KB, COL_VB, COL_GATE_B, COL_QM, COL_GATE_M = 1024, 1536, 2048, 2560, 3072, 3584
IN_REAL = 416

VMEM_LIMIT_BYTES = 56 * 1024 * 1024
NEG_BIG = -1e30
Q_BLOCK = 512
SB_BWD_Q_BLOCK = 256
TRI_BLOCK = 256
KEY_CHUNK = 512

SHARDED = {
    "w_in": ((1024, 4000), 1), "w_mem_kv": ((1024, 1024), 0), "w_q_b": ((256, 768), 1), "w_kv_b": ((128, 1024), 1),
    "w_branch_mla": ((512, 1024), 1), "w_branch_sb": ((512, 1024), 1), "w_branch_mem": ((512, 1024), 1),
    "w_merge_gate": ((1024, 3072), 1), "w_out": ((1024, 1024), 0),
}
GATHER_GROUPS = (("w_in",), ("w_merge_gate",), ("w_q_b", "w_kv_b", "w_mem_kv", "w_branch_mla", "w_branch_sb", "w_branch_mem", "w_out"))
GRAD_GROUPS = (("w_out", "w_merge_gate", "w_branch_mla", "w_branch_sb", "w_branch_mem"), ("w_mem_kv", "w_q_b", "w_kv_b"), ("w_in",))
SMALL = (("q_a_gain", 256), ("kv_a_gain", 128), ("b_merge_gate", 3072), ("ln_gain", 1024), ("ln_bias", 1024))
SMALL_ROWS, SMALL_LANES = 48, 128
ADAM_ROWS = 256
LOSS_INDEX = 5504


def _cparams(*sem):
    return pltpu.CompilerParams(dimension_semantics=sem or None, vmem_limit_bytes=VMEM_LIMIT_BYTES)


_DIMS = {"nn": (((1,), (0,)), ((), ())), "nt": (((1,), (1,)), ((), ())), "tn": (((0,), (0,)), ((), ()))}


def _dot(a, b, dims):
    return lax.dot_general(a, b, _DIMS[dims], preferred_element_type=F32)


def _tile(dim, want):
    if dim <= want:
        return dim
    t = want - want % LANES
    while dim % t:
        t -= LANES
    assert t > 0, (dim, want)
    return t


_ANY = pl.BlockSpec(memory_space=pl.ANY)


def _mm(a, b, dims, *, name, out_dtype=F32, add=None, add_scale=1.0, behind=None, tm=1024, tn=1024, tk=1024):
    if dims == "nn":
        (m, k), (k2, n) = a.shape, b.shape
    elif dims == "nt":
        (m, k), (n, k2) = a.shape, b.shape
    else:
        (k, m), (k2, n) = a.shape, b.shape
    assert k == k2, (a.shape, b.shape, dims)
    tm, tn, tk = _tile(m, tm), _tile(n, tn), _tile(k, tk)
    nk = k // tk
    a_spec = pl.BlockSpec((tk, tm), lambda i, j, kk: (kk, i)) if dims == "tn" else pl.BlockSpec((tm, tk), lambda i, j, kk: (i, kk))
    b_spec = pl.BlockSpec((tn, tk), lambda i, j, kk: (j, kk)) if dims == "nt" else pl.BlockSpec((tk, tn), lambda i, j, kk: (kk, j))
    o_spec = pl.BlockSpec((tm, tn), lambda i, j, kk: (i, j))

    def body(*refs):
        a_ref, b_ref = refs[:2]
        add_ref = refs[2] if add is not None else None
        o_ref = refs[2 + (add is not None) + (behind is not None)]
        part = _dot(a_ref[...].astype(BF16), b_ref[...].astype(BF16), dims)

        def finish(r):
            if add is not None:
                r = r + add_scale * add_ref[...]
            o_ref[...] = r.astype(out_dtype)

        if nk == 1:
            finish(part)
            return
        acc = refs[-1]
        kk = pl.program_id(2)

        @pl.when(kk == 0)
        def _():
            acc[...] = part

        @pl.when(kk > 0)
        def _():
            acc[...] += part

        @pl.when(kk == nk - 1)
        def _():
            finish(acc[...])

    return pl.pallas_call(
        body, name=name, grid=(m // tm, n // tn, nk),
        in_specs=[a_spec, b_spec] + ([o_spec] if add is not None else []) + ([_ANY] if behind is not None else []), out_specs=o_spec,
        out_shape=jax.ShapeDtypeStruct((m, n), out_dtype), scratch_shapes=[pltpu.VMEM((tm, tn), F32)] if nk > 1 else [],
        compiler_params=_cparams("parallel", "parallel", "arbitrary"),
    )(*[v for v in (a, b, add, behind) if v is not None])


def _rowwise(fn, ins, outs, *, name, rows, tr=256):
    n_in = len(ins)
    in_specs, args = [], []
    for it in ins:
        arr, w, off = it if isinstance(it, tuple) else (it, it.shape[-1], 0)
        assert off % w == 0
        cb = off // w
        if arr.shape[0] == 1:
            in_specs.append(pl.BlockSpec((1, w), lambda i, cb=cb: (0, cb)))
        else:
            in_specs.append(pl.BlockSpec((tr, w), lambda i, cb=cb: (i, cb)))
        args.append(arr)
    out_shape, out_specs, is_sum = [], [], []
    for kind, d in outs:
        if kind == "sum":
            out_shape.append(jax.ShapeDtypeStruct((1, d), F32))
            out_specs.append(pl.BlockSpec((1, d), lambda i: (0, 0)))
            is_sum.append(True)
        else:
            out_shape.append(jax.ShapeDtypeStruct((rows, kind), d))
            out_specs.append(pl.BlockSpec((tr, kind), lambda i: (i, 0)))
            is_sum.append(False)

    def body(*refs):
        res = fn(*[r[...] for r in refs[:n_in]])
        for r, val, s in zip(refs[n_in:], res, is_sum, strict=True):
            if s:
                @pl.when(pl.program_id(0) == 0)
                def _(r=r):
                    r[...] = jnp.zeros_like(r)

                r[...] += val
            else:
                r[...] = val.astype(r.dtype)

    return pl.pallas_call(
        body, name=name, grid=(rows // tr,), in_specs=in_specs, out_specs=out_specs, out_shape=out_shape,
        compiler_params=_cparams("arbitrary"),
    )(*args)


def _colsum(v):
    return jnp.sum(v, axis=0, keepdims=True)


def _sigmoid(v):
    return 1.0 / (1.0 + jnp.exp(-v))


def _lane_groups(v):
    return [v[:, g * LANES:(g + 1) * LANES] for g in range(v.shape[1] // LANES)]


def _swap_halves(v, first_lane):
    lane = lax.broadcasted_iota(jnp.int32, v.shape, 1)
    return jnp.where(lane < first_lane + 16, pltpu.roll(v, 112, axis=1), pltpu.roll(v, 16, axis=1))


def _lane_sum(acc, v):
    for part in _lane_groups(v):
        acc = acc + part
    return acc


def _low_half(shape):
    return lax.broadcasted_iota(jnp.int32, shape, 1) < LANES // 2


def _select_heads(per_head, pick):
    if len(per_head) == 1:
        return pick(per_head[0], 0)
    return jnp.where(_low_half(per_head[0].shape), pick(per_head[0], 0), pick(per_head[1], 1))


def _attn_specs(s, sk, hp, bq, q0, k0, v0):
    wq = hp * LANES
    assert q0 % wq == 0 and k0 % wq == 0 and v0 % LANES == 0
    qb0, kb0, vb0 = q0 // wq, k0 // wq, v0 // LANES
    q_spec = pl.BlockSpec((bq, wq), lambda g, i: (i, qb0 + g))
    k_spec = pl.BlockSpec((sk, wq), lambda g, i: (0, kb0 + g))
    v_spec = pl.BlockSpec((sk, LANES), lambda g, i: (0, vb0 + g))
    row_out = lambda w: pl.BlockSpec((bq, w), lambda g, i: (i, g))
    key_out = lambda w: pl.BlockSpec((sk, w), lambda g, i: (0, g))
    return q_spec, k_spec, v_spec, row_out, key_out


def _chunks(i, bq, ch, sk, causal):
    return ((i + 1) * bq - 1) // ch if causal else jnp.int32(sk // ch - 1)


def _positions(i, c, bq, ch):
    return (i * bq + lax.broadcasted_iota(jnp.int32, (bq, ch), 0), c * ch + lax.broadcasted_iota(jnp.int32, (bq, ch), 1))


def _softmax_fwd(q, k, v, *, hp, causal, name, q0=0, k0=0, v0=0):
    s, sk = q.shape[0], k.shape[0]
    bq, ch = min(Q_BLOCK, s), min(KEY_CHUNK, sk)
    q_spec, k_spec, v_spec, row_out, _ = _attn_specs(s, sk, hp, bq, q0, k0, v0)

    def body(q_ref, k_ref, v_ref, o_ref, lse_ref, s_scr):
        i = pl.program_id(1)
        qs = _lane_groups(q_ref[...])
        last = _chunks(i, bq, ch, sk, causal)

        def scores(c, ms, masked):
            off = pl.multiple_of(c * ch, ch)
            out = []
            for j in range(hp):
                sc = _dot(qs[j], k_ref[pl.ds(off, ch), j * LANES:(j + 1) * LANES], "nt")
                if masked:
                    qpos, kpos = _positions(i, c, bq, ch)
                    sc = jnp.where(kpos <= qpos, sc, NEG_BIG)
                s_scr[j, c] = sc
                m = ms[j]
                for part in _lane_groups(sc):
                    m = jnp.maximum(m, part)
                out.append(m)
            return tuple(out)

        ms = lax.fori_loop(0, last, lambda c, m: scores(c, m, False), tuple(jnp.full((bq, LANES), NEG_BIG, F32) for _ in range(hp)))
        ms = scores(last, ms, causal)
        row_max = [jnp.max(m, axis=1, keepdims=True) for m in ms]

        def weigh(c, carry):
            off = pl.multiple_of(c * ch, ch)
            vt = v_ref[pl.ds(off, ch), :]
            out = []
            for j in range(hp):
                l, acc = carry[j]
                p = jnp.exp2(s_scr[j, c] - row_max[j])
                out.append((_lane_sum(l, p), acc + _dot(p.astype(BF16), vt, "nn")))
            return tuple(out)

        zero = jnp.zeros((bq, LANES), F32)
        res = lax.fori_loop(0, last + 1, weigh, tuple((zero, zero) for _ in range(hp)))
        row_sum = [jnp.sum(l, axis=1, keepdims=True) for l, _ in res]
        o_ref[...] = _select_heads([acc for _, acc in res], lambda acc, j: acc / row_sum[j])
        lse_ref[...] = _select_heads([jnp.broadcast_to(row_max[j] + jnp.log2(row_sum[j]), (bq, LANES)) for j in range(hp)], lambda a, j: a)

    return pl.pallas_call(
        body, name=name, grid=(GROUPS, s // bq), in_specs=[q_spec, k_spec, v_spec], out_specs=[row_out(LANES), row_out(LANES)],
        out_shape=[jax.ShapeDtypeStruct((s, GROUPS * LANES), F32)] * 2,
        scratch_shapes=[pltpu.VMEM((hp, sk // ch, bq, ch), F32)], compiler_params=_cparams("parallel", "arbitrary"),
    )(q, k, v)


def _head_cotangent(do, j, hp):
    if hp == 1:
        return do
    return jnp.where(_low_half(do.shape) == (j == 0), do, 0.0)


def _softmax_bwd(q, k, v, o, do, lse, behind, *, hp, causal, dq_scale, name, q0=0, k0=0, v0=0):
    s, sk = q.shape[0], k.shape[0]
    bq, ch = min(Q_BLOCK, s), min(KEY_CHUNK, sk)
    wq = hp * LANES
    q_spec, k_spec, v_spec, row_out, key_out = _attn_specs(s, sk, hp, bq, q0, k0, v0)

    def body(q_ref, k_ref, v_ref, o_ref, do_ref, lse_ref, _, dq_ref, dk_ref, dv_ref):
        i = pl.program_id(1)

        @pl.when(i == 0)
        def _():
            dk_ref[...] = jnp.zeros_like(dk_ref)
            dv_ref[...] = jnp.zeros_like(dv_ref)

        qs = _lane_groups(q_ref[...])
        do_all, o_all, lse_all = do_ref[...], o_ref[...], lse_ref[...]
        dos, deltas, lses = [], [], []
        for j in range(hp):
            d = _head_cotangent(do_all, j, hp)
            deltas.append(jnp.sum(d * o_all, axis=1, keepdims=True))
            dos.append(d.astype(BF16))
            lses.append(lse_all[:, j * (LANES // hp):j * (LANES // hp) + 1])
        last = _chunks(i, bq, ch, sk, causal)

        def chunk(c, dqs, masked):
            off = pl.multiple_of(c * ch, ch)
            vt = v_ref[pl.ds(off, ch), :]
            out, dks, dv = [], [], None
            for j in range(hp):
                kt = k_ref[pl.ds(off, ch), j * LANES:(j + 1) * LANES]
                p = jnp.exp2(_dot(qs[j], kt, "nt") - lses[j])
                if masked:
                    qpos, kpos = _positions(i, c, bq, ch)
                    p = jnp.where(kpos <= qpos, p, 0.0)
                ds = (p * (_dot(dos[j], vt, "nt") - deltas[j]) * LN2).astype(BF16)
                out.append(dqs[j] + _dot(ds, kt, "nn"))
                dks.append(_dot(ds, qs[j], "tn"))
                dvj = _dot(p.astype(BF16), dos[j], "tn")
                dv = dvj if dv is None else dv + dvj
            dk_ref[pl.ds(off, ch), :] += dks[0] if hp == 1 else jnp.concatenate(dks, axis=1)
            dv_ref[pl.ds(off, ch), :] += dv
            return tuple(out)

        dqs = lax.fori_loop(0, last, lambda c, d: chunk(c, d, False), tuple(jnp.zeros((bq, LANES), F32) for _ in range(hp)))
        dqs = chunk(last, dqs, causal)
        dq_ref[...] = (dqs[0] if hp == 1 else jnp.concatenate(dqs, axis=1)) * dq_scale

    return pl.pallas_call(
        body, name=name, grid=(GROUPS, s // bq),
        in_specs=[q_spec, k_spec, v_spec, row_out(LANES), row_out(LANES), row_out(LANES), _ANY],
        out_specs=[row_out(wq), key_out(wq), key_out(LANES)],
        out_shape=[jax.ShapeDtypeStruct((s, GROUPS * wq), F32), jax.ShapeDtypeStruct((sk, GROUPS * wq), F32),
                   jax.ShapeDtypeStruct((sk, GROUPS * LANES), F32)],
        compiler_params=_cparams("arbitrary", "arbitrary"),
    )(q, k, v, o, do, lse, behind)


def _log2_sigmoid_pair(z2):
    minus_abs = lax.bitcast_convert_type(lax.bitcast_convert_type(z2, jnp.uint32) | jnp.uint32(0x80000000), F32)
    log_beta = jnp.minimum(z2, 0.0) - jnp.log2(1.0 + jnp.exp2(minus_abs))
    return log_beta, log_beta - z2


def _tri_sum(v, tri):
    hi = v.astype(BF16)
    lo = (v - hi.astype(F32)).astype(BF16)
    return _dot(hi, tri, "nn") + _dot(lo, tri, "nn")


def _tri(n, after):
    rows, cols = lax.broadcasted_iota(jnp.int32, (n, n), 0), lax.broadcasted_iota(jnp.int32, (n, n), 1)
    return (rows > cols if after else rows < cols).astype(BF16)


def _running_sums(v, start, tri, backwards):
    n = tri.shape[0]
    blocks = [v[:, t * n:(t + 1) * n] for t in range(v.shape[1] // n)]
    order = range(len(blocks) - 1, -1, -1) if backwards else range(len(blocks))
    parts, run = [None] * len(blocks), start
    for t in order:
        parts[t] = _tri_sum(blocks[t], tri) + run
        run = run + jnp.sum(blocks[t], axis=1, keepdims=True)
    return (parts[0] if len(parts) == 1 else jnp.concatenate(parts, axis=1)), run


def _sb_weights(qm, kt, run, tri, strict):
    log_beta, log_keep = _log2_sigmoid_pair(_dot(qm, kt, "nt"))
    if strict is not None:
        log_keep = jnp.where(strict, log_keep, 0.0)
    behind, run = _running_sums(log_keep, run, tri, True)
    a = jnp.exp2(log_beta + behind)
    if strict is not None:
        a = jnp.where(strict, a, 0.0)
    return a, log_beta, run


def _sb_queries(q_all):
    low = _low_half(q_all.shape)
    zero = jnp.zeros_like(q_all)
    return [jnp.where(low, q_all, zero), jnp.where(low, zero, q_all)]


def _sb_fwd(qkv, *, q0, k0, v0, name):
    s = qkv.shape[0]
    bq, ch = min(Q_BLOCK, s), min(KEY_CHUNK, s)
    q_spec, k_spec, v_spec, row_out, _ = _attn_specs(s, s, 1, bq, q0, k0, v0)

    def body(q_ref, k_ref, v_ref, o_ref):
        i = pl.program_id(1)
        qms = _sb_queries(q_ref[...])
        tri = _tri(min(TRI_BLOCK, ch), True)
        last = _chunks(i, bq, ch, s, True)

        def chunk(c, carry, masked):
            off = pl.multiple_of(c * ch, ch)
            kt, vt = k_ref[pl.ds(off, ch), :], v_ref[pl.ds(off, ch), :]
            strict = None
            if masked:
                qpos, kpos = _positions(i, c, bq, ch)
                strict = kpos < qpos
            out = []
            for j in range(2):
                run, acc = carry[j]
                a, _, run = _sb_weights(qms[j], kt, run, tri, strict)
                out.append((run, acc + _dot(a.astype(BF16), vt, "nn")))
            return tuple(out)

        carry = chunk(last, tuple((jnp.zeros((bq, 1), F32), jnp.zeros((bq, LANES), F32)) for _ in range(2)), True)
        res = lax.fori_loop(0, last, lambda n, c: chunk(last - 1 - n, c, False), carry)
        o_ref[...] = _select_heads([acc for _, acc in res], lambda acc, j: acc)

    return pl.pallas_call(
        body, name=name, grid=(GROUPS, s // bq), in_specs=[q_spec, k_spec, v_spec], out_specs=row_out(LANES),
        out_shape=jax.ShapeDtypeStruct((s, GROUPS * LANES), F32), compiler_params=_cparams("parallel", "arbitrary"),
    )(qkv, qkv, qkv)


def _sb_bwd(qkv, do, behind, *, q0, k0, v0, dq_scale, name):
    s = qkv.shape[0]
    bq, ch = min(SB_BWD_Q_BLOCK, s), min(KEY_CHUNK, s)
    q_spec, k_spec, v_spec, row_out, key_out = _attn_specs(s, s, 1, bq, q0, k0, v0)

    def body(q_ref, k_ref, v_ref, do_ref, _, dq_ref, dk_ref, dv_ref, g_s, beta_s):
        i = pl.program_id(1)

        @pl.when(i == 0)
        def _():
            dk_ref[...] = jnp.zeros_like(dk_ref)
            dv_ref[...] = jnp.zeros_like(dv_ref)

        qms = _sb_queries(q_ref[...])
        do_all = do_ref[...]
        dos = [_head_cotangent(do_all, j, 2).astype(BF16) for j in range(2)]
        tri_after, tri_before = _tri(min(TRI_BLOCK, ch), True), _tri(min(TRI_BLOCK, ch), False)
        last = _chunks(i, bq, ch, s, True)

        def strict_mask(c):
            qpos, kpos = _positions(i, c, bq, ch)
            return kpos < qpos

        def sweep1(c, runs, masked):
            off = pl.multiple_of(c * ch, ch)
            kt, vt = k_ref[pl.ds(off, ch), :], v_ref[pl.ds(off, ch), :]
            strict = strict_mask(c) if masked else None
            out, dv = [], None
            for j in range(2):
                a, log_beta, run = _sb_weights(qms[j], kt, runs[j], tri_after, strict)
                g_s[j, c] = a * _dot(dos[j], vt, "nt")
                beta_s[j, c] = jnp.exp2(log_beta)
                dvj = _dot(a.astype(BF16), dos[j], "tn")
                dv = dvj if dv is None else dv + dvj
                out.append(run)
            dv_ref[pl.ds(off, ch), :] += dv
            return tuple(out)

        runs = sweep1(last, tuple(jnp.zeros((bq, 1), F32) for _ in range(2)), True)
        lax.fori_loop(0, last, lambda n, r: sweep1(last - 1 - n, r, False), runs)

        def sweep2(c, carry, masked):
            off = pl.multiple_of(c * ch, ch)
            kt = k_ref[pl.ds(off, ch), :]
            out, dk = [], None
            for j in range(2):
                before, dq = carry[j]
                g, beta = g_s[j, c], beta_s[j, c]
                in_front, before = _running_sums(g, before, tri_before, False)
                dz = (g * (1.0 - beta) - beta * in_front) * LN2
                if masked:
                    dz = jnp.where(strict_mask(c), dz, 0.0)
                dz = dz.astype(BF16)
                dkj = _dot(dz, qms[j], "tn")
                dk = dkj if dk is None else dk + dkj
                out.append((before, dq + _dot(dz, kt, "nn")))
            dk_ref[pl.ds(off, ch), :] += dk
            return tuple(out)

        carry = lax.fori_loop(0, last, lambda c, cr: sweep2(c, cr, False),
                              tuple((jnp.zeros((bq, 1), F32), jnp.zeros((bq, LANES), F32)) for _ in range(2)))
        res = sweep2(last, carry, True)
        dq_ref[...] = _select_heads([dq for _, dq in res], lambda dq, j: dq) * dq_scale

    n_ch = s // ch
    return pl.pallas_call(
        body, name=name, grid=(GROUPS, s // bq), in_specs=[q_spec, k_spec, v_spec, row_out(LANES), _ANY],
        out_specs=[row_out(LANES), key_out(LANES), key_out(LANES)],
        out_shape=[jax.ShapeDtypeStruct((s, GROUPS * LANES), F32)] * 3,
        scratch_shapes=[pltpu.VMEM((2, n_ch, bq, ch), F32), pltpu.VMEM((2, n_ch, bq, ch), F32)],
        compiler_params=_cparams("arbitrary", "arbitrary"),
    )(qkv, qkv, qkv, do, behind)


def _rope_tables(s):
    half = MLA_ROPE // 2
    freqs = ROPE_BASE ** (-jnp.arange(half, dtype=F32) / half)
    ang = jnp.arange(s, dtype=F32)[:, None] * freqs[None, :]
    cos, sin = jnp.cos(ang), jnp.sin(ang)
    tail = jnp.zeros((s, LANES - MLA_NOPE - MLA_ROPE), F32)
    lead = lambda fill: jnp.full((s, MLA_NOPE), fill, F32)
    return dict(
        cos_k0=jnp.concatenate([cos, cos, lead(0.0), tail], axis=1), sin_k0=jnp.concatenate([-sin, sin, lead(0.0), tail], axis=1),
        cos_k64=jnp.concatenate([lead(0.0), cos, cos, tail], axis=1), sin_k64=jnp.concatenate([lead(0.0), -sin, sin, tail], axis=1),
        cos_q=jnp.concatenate([lead(1.0), cos, cos, tail], axis=1),
        sin_k64_t=jnp.concatenate([lead(0.0), sin, -sin, tail], axis=1),
    )


def _local_step(x, mem, target, w, emit=lambda grads: jnp.zeros((8, LANES), F32)):
    s = x.shape[0]
    rope = _rope_tables(s)
    xb = x.astype(BF16)
    inv_d = 1.0 / D_MODEL
    scale_a = LOG2E / math.sqrt(MLA_NOPE + MLA_ROPE)
    scale_b = LOG2E / math.sqrt(SB_HEAD_DIM)
    scale_m = LOG2E / math.sqrt(MEM_HEAD_DIM)
    arrive_after = getattr(w, "arrive_after", lambda value: None)

    arrive_after(xb)
    proj = _mm(xb, w["w_in"], "nn", name="proj")
    arrive_after(proj)
    pre = _mm(xb, w["w_merge_gate"], "nn", name="merge_pre")
    arrive_after(pre)

    def rms_pair(c_q, c_kv, g_q, g_kv):
        return (c_q * lax.rsqrt(jnp.mean(c_q * c_q, axis=1, keepdims=True) + RMS_EPS) * g_q,
                c_kv * lax.rsqrt(jnp.mean(c_kv * c_kv, axis=1, keepdims=True) + RMS_EPS) * g_kv)

    n_q, n_kv = _rowwise(rms_pair, [(proj, 256, COL_CQ), (proj, 128, COL_CKV), w["q_a_gain"], w["kv_a_gain"]],
                         [(256, BF16), (128, BF16)], name="rms_pair", rows=s)
    q_a = _mm(n_q, w["w_q_b"], "nn", name="q_up")
    kv_a = _mm(n_kv, w["w_kv_b"], "nn", name="kv_up", out_dtype=BF16)

    def rope_q(qa, cos, sin):
        return (jnp.concatenate([(g * cos + _swap_halves(g, MLA_NOPE) * sin) * scale_a for g in _lane_groups(qa)], axis=1),)

    (q_mla,) = _rowwise(rope_q, [q_a, rope["cos_q"], rope["sin_k64"]], [(1024, BF16)], name="rope_q", rows=s)

    def rope_k(k_nope, k_rope, cos, sin):
        k_pe = pltpu.roll(k_rope * cos + _swap_halves(k_rope, 0) * sin, MLA_NOPE, axis=1).astype(BF16)
        return (jnp.concatenate([g + k_pe for g in _lane_groups(k_nope)], axis=1),)

    (k_mla,) = _rowwise(rope_k, [(kv_a, 1024, 0), (proj, 128, COL_KROPE), rope["cos_k0"], rope["sin_k0"]],
                        [(1024, BF16)], name="rope_k", rows=s)
    o_a, lse_a = _softmax_fwd(q_mla, k_mla, kv_a, hp=2, causal=True, name="mla_fwd", v0=1024)

    def cast_qkv(q_sb, k_sb, v_sb, q_m):
        return (jnp.concatenate([(q_sb * scale_b).astype(BF16), k_sb.astype(BF16), v_sb.astype(BF16), (q_m * scale_m).astype(BF16)], axis=1),)

    (qkv,) = _rowwise(cast_qkv, [(proj, 512, COL_QB), (proj, 512, COL_KB), (proj, 512, COL_VB), (proj, 512, COL_QM)],
                      [(2048, BF16)], name="cast_qkv", rows=s)
    o_b = _sb_fwd(qkv, q0=0, k0=512, v0=1024, name="sb_fwd")

    memb = mem.astype(BF16)
    mem_kv = _mm(memb, w["w_mem_kv"], "nn", name="mem_kv", out_dtype=BF16)
    o_m, lse_m = _softmax_fwd(qkv, mem_kv, mem_kv, hp=1, causal=False, name="mem_fwd", q0=1536, v0=512)

    o_br = {"mla": o_a, "sb": o_b, "mem": o_m}
    gate_col = {"mla": COL_GATE_A, "sb": COL_GATE_B, "mem": COL_GATE_M}

    def gated(o, gate):
        return (o * gate * _sigmoid(gate),)

    u, y = {}, {}
    for br in ("mla", "sb", "mem"):
        (u[br],) = _rowwise(gated, [o_br[br], (proj, 512, gate_col[br])], [(512, BF16)], name=f"gated_{br}", rows=s)
        y[br] = _mm(u[br], w[f"w_branch_{br}"], "nn", name=f"branch_{br}")

    def merge(pa, pb, pm, ba, bb, bm, ya, yb, ym):
        return (_sigmoid(pa + ba) * ya + _sigmoid(pb + bb) * yb + _sigmoid(pm + bm) * ym,)

    bias = w["b_merge_gate"]
    gate_ins = [(pre, 1024, 0), (pre, 1024, 1024), (pre, 1024, 2048), (bias, 1024, 0), (bias, 1024, 1024), (bias, 1024, 2048)]
    (merged,) = _rowwise(merge, gate_ins + [y["mla"], y["sb"], y["mem"]], [(1024, BF16)], name="merge", rows=s)
    out = _mm(merged, w["w_out"], "nn", name="out_proj")

    def norm_loss(xv, ov, tv, gain, bias_ln):
        z = DEEPNORM_ALPHA * xv + ov
        zc = z - jnp.mean(z, axis=1, keepdims=True)
        rstd = lax.rsqrt(jnp.mean(zc * zc, axis=1, keepdims=True) + LN_EPS)
        xhat = zc * rstd
        err = xhat * gain + bias_ln - tv
        loss = 0.5 * jnp.sum(jnp.mean(err * err, axis=1, keepdims=True), axis=0, keepdims=True)
        dy = err * inv_d
        dxhat = dy * gain
        dz = rstd * (dxhat - jnp.mean(dxhat, axis=1, keepdims=True) - xhat * jnp.mean(dxhat * xhat, axis=1, keepdims=True))
        return dz, dz, _colsum(dy * xhat), _colsum(dy), jnp.broadcast_to(loss, (1, LANES))

    dz, dzb, g_ln_gain, g_ln_bias, loss = _rowwise(
        norm_loss, [x, out, target, w["ln_gain"], w["ln_bias"]],
        [(1024, F32), (1024, BF16), ("sum", 1024), ("sum", 1024), ("sum", LANES)], name="norm_loss", rows=s)

    grads = {"ln_gain": g_ln_gain, "ln_bias": g_ln_bias}
    dmerged = _mm(dzb, w["w_out"], "nt", name="d_merged")
    grads["w_out"] = _mm(merged, dzb, "tn", name="g_w_out")

    def merge_bwd(dm, pa, pb, pm, ba, bb, bm, ya, yb, ym):
        res, dpre = [], []
        for p, b, yv in ((pa, ba, ya), (pb, bb, yb), (pm, bm, ym)):
            g = _sigmoid(p + b)
            dpre.append(dm * yv * g * (1.0 - g))
            res.append(dm * g)
        dpre = jnp.concatenate(dpre, axis=1)
        return dpre, _colsum(dpre), *res

    dpre, grads["b_merge_gate"], dy_a, dy_b, dy_m = _rowwise(
        merge_bwd, [dmerged] + gate_ins + [y["mla"], y["sb"], y["mem"]],
        [(3072, BF16), ("sum", 3072), (1024, BF16), (1024, BF16), (1024, BF16)], name="merge_bwd", rows=s, tr=128)
    grads["w_merge_gate"] = _mm(xb, dpre, "tn", name="g_w_merge")
    dx = _mm(dpre, w["w_merge_gate"], "nt", name="dx_merge", add=dz, add_scale=DEEPNORM_ALPHA)

    def gated_bwd(du, o, gate):
        sg = _sigmoid(gate)
        return du * gate * sg, du * o * sg * (1.0 + gate * (1.0 - sg))

    d_o, d_gate = {}, {}
    for br, dy in (("mla", dy_a), ("sb", dy_b), ("mem", dy_m)):
        grads[f"w_branch_{br}"] = _mm(u[br], dy, "tn", name=f"g_w_branch_{br}")
        du = _mm(dy, w[f"w_branch_{br}"], "nt", name=f"d_u_{br}")
        d_o[br], d_gate[br] = _rowwise(gated_bwd, [du, o_br[br], (proj, 512, gate_col[br])], [(512, F32), (512, BF16)],
                                       name=f"gated_bwd_{br}", rows=s)
    sent = emit({n: grads[n] for n in ("w_out", "w_merge_gate", "w_branch_mla", "w_branch_sb", "w_branch_mem")})

    dq_m, dk_m, dv_m = _softmax_bwd(qkv, mem_kv, mem_kv, o_m, d_o["mem"], lse_m, sent, hp=1, causal=False, dq_scale=scale_m,
                                    name="mem_bwd", q0=1536, v0=512)
    grads["w_mem_kv"] = _mm(memb, jnp.concatenate([dk_m, dv_m], axis=1), "tn", name="g_w_mem_kv")

    dq_sb, dk_sb, dv_sb = _sb_bwd(qkv, d_o["sb"], sent, q0=0, k0=512, v0=1024, dq_scale=scale_b, name="sb_bwd")

    dq_mla, dk_mla, dv_a = _softmax_bwd(q_mla, k_mla, kv_a, o_a, d_o["mla"], lse_a, sent, hp=2, causal=True, dq_scale=scale_a,
                                        name="mla_bwd", v0=1024)

    def rope_q_bwd(dq, cos, sin):
        return (jnp.concatenate([g * cos + _swap_halves(g, MLA_NOPE) * sin for g in _lane_groups(dq)], axis=1),)

    (dq_a,) = _rowwise(rope_q_bwd, [dq_mla, rope["cos_q"], rope["sin_k64_t"]], [(1024, BF16)], name="rope_q_bwd", rows=s)
    grads["w_q_b"] = _mm(n_q, dq_a, "tn", name="g_w_q_b")
    dn_q = _mm(dq_a, w["w_q_b"], "nt", name="d_n_q")

    def rope_k_bwd(dk, cos, sin):
        groups = _lane_groups(dk)
        g = groups[0]
        for other in groups[1:]:
            g = g + other
        d_rope = pltpu.roll(g * cos + _swap_halves(g, MLA_NOPE) * sin, MLA_NOPE, axis=1)
        nope = _low_half(g.shape)
        return d_rope, jnp.concatenate([jnp.where(nope, grp, 0.0) for grp in groups], axis=1)

    dk_rope, dk_nope = _rowwise(rope_k_bwd, [dk_mla, rope["cos_k64"], rope["sin_k64_t"]], [(128, BF16), (1024, BF16)],
                                name="rope_k_bwd", rows=s)
    grads["w_kv_b"] = jnp.concatenate([_mm(n_kv, dk_nope, "tn", name="g_w_kv_b_k"), _mm(n_kv, dv_a, "tn", name="g_w_kv_b_v")], axis=1)
    dn_kv = _mm(dk_nope, w["w_kv_b"][:, :1024], "nt", name="d_n_kv_k")
    dn_kv = _mm(dv_a, w["w_kv_b"][:, 1024:], "nt", name="d_n_kv_v", add=dn_kv)

    def rms_bwd(c_q, c_kv, dq, dkv, g_q, g_kv):
        res = []
        for c, dn, g in ((c_q, dq, g_q), (c_kv, dkv, g_kv)):
            r = lax.rsqrt(jnp.mean(c * c, axis=1, keepdims=True) + RMS_EPS)
            t = dn * g
            res += [r * t - c * (r * r * r) * jnp.mean(c * t, axis=1, keepdims=True), _colsum(dn * c * r)]
        return res

    dc_q, grads["q_a_gain"], dc_kv, grads["kv_a_gain"] = _rowwise(
        rms_bwd, [(proj, 256, COL_CQ), (proj, 128, COL_CKV), dn_q, dn_kv, w["q_a_gain"], w["kv_a_gain"]],
        [(256, BF16), ("sum", 256), (128, BF16), ("sum", 128)], name="rms_bwd", rows=s)

    sent = emit({n: grads[n] for n in ("w_mem_kv", "w_q_b", "w_kv_b")})

    dproj = jnp.concatenate(
        [dc_q, dc_kv, dk_rope, d_gate["mla"], dq_sb.astype(BF16), dk_sb.astype(BF16), dv_sb.astype(BF16), d_gate["sb"],
         dq_m.astype(BF16), d_gate["mem"]], axis=1)
    grads["w_in"] = _mm(xb, dproj, "tn", name="g_w_in", behind=sent)
    sent = emit({"w_in": grads["w_in"]})
    grad_x = _mm(dproj, w["w_in"], "nt", name="grad_x", add=dx, behind=sent)
    return loss, grad_x, grads


def _shard_shape(shape, axis):
    return tuple(d // N_DEV if a == axis else d for a, d in enumerate(shape))


def _from_blocks(blocks, name):
    shape, axis = SHARDED[name]
    return blocks.reshape(shape) if axis == 0 else blocks.transpose(1, 0, 2).reshape(shape)


def _to_blocks(full, name):
    shape, axis = SHARDED[name]
    shp = _shard_shape(shape, axis)
    return full.reshape(N_DEV, *shp) if axis == 0 else full.reshape(shape[0], N_DEV, shp[1]).transpose(1, 0, 2)


def _pad_heads(a, used):
    rows = a.shape[0]
    a = a.reshape(rows, MLA_HEADS, used)
    return jnp.concatenate([a, jnp.zeros((rows, MLA_HEADS, LANES - used), a.dtype)], axis=2).reshape(rows, MLA_HEADS * LANES)


def _to_kernel_layout(name, full):
    if name == "w_in":
        return jnp.concatenate([full[:, :IN_REAL], jnp.zeros((D_MODEL, 512 - IN_REAL), full.dtype), full[:, IN_REAL:]], axis=1)
    if name == "w_q_b":
        return _pad_heads(full, MLA_NOPE + MLA_ROPE)
    if name == "w_kv_b":
        kv = full.reshape(MLA_KV_LORA, MLA_HEADS, MLA_NOPE + MLA_V)
        return jnp.concatenate([_pad_heads(kv[:, :, :MLA_NOPE].reshape(MLA_KV_LORA, -1), MLA_NOPE),
                                kv[:, :, MLA_NOPE:].reshape(MLA_KV_LORA, -1)], axis=1)
    return full


def _from_kernel_layout(name, g):
    if name == "w_in":
        return jnp.concatenate([g[:, :IN_REAL], g[:, 512:]], axis=1)
    if name == "w_q_b":
        return g.reshape(MLA_Q_LORA, MLA_HEADS, LANES)[:, :, :MLA_NOPE + MLA_ROPE].reshape(MLA_Q_LORA, -1)
    if name == "w_kv_b":
        return jnp.concatenate([g[:, :1024].reshape(MLA_KV_LORA, MLA_HEADS, LANES)[:, :, :MLA_NOPE],
                                g[:, 1024:].reshape(MLA_KV_LORA, MLA_HEADS, MLA_V)], axis=2).reshape(MLA_KV_LORA, -1)
    return g


def _pack_small(vectors, loss=None):
    flat = [v.reshape(-1) for v in vectors]
    flat.append(jnp.zeros((SMALL_ROWS * SMALL_LANES - LOSS_INDEX,), F32) if loss is None else
                jnp.concatenate([loss.reshape(-1)[:1], jnp.zeros((SMALL_ROWS * SMALL_LANES - LOSS_INDEX - 1,), F32)]))
    return jnp.concatenate(flat).reshape(SMALL_ROWS, SMALL_LANES)


def _unpack_small(packed):
    flat, res, off = packed.reshape(-1), [], 0
    for _, n in SMALL:
        res.append(flat[off:off + n].reshape(1, n))
        off += n
    return res


def _me_and_peers():
    x, y, c = lax.axis_index("x"), lax.axis_index("y"), lax.axis_index("c")
    peers = []
    for kk in range(1, N_DEV):
        px, py, pc = (x + (kk >> 2)) % 2, (y + ((kk >> 1) & 1)) % 2, (c + (kk & 1)) % 2
        peers.append(((px, py, pc), 4 * px + 2 * py + pc))
    return 4 * x + 2 * y + c, peers


def _share_small(small, *, name):
    def body(small_ref, all_ref, send_sems, recv_sems, local_sem):
        me, peers = _me_and_peers()
        copies = [pltpu.make_async_remote_copy(src_ref=small_ref, dst_ref=all_ref.at[me], send_sem=send_sems.at[kk], recv_sem=recv_sems.at[kk],
                                               device_id=pos, device_id_type=pl.DeviceIdType.MESH) for kk, (pos, _) in enumerate(peers)]
        copies.append(pltpu.make_async_copy(small_ref, all_ref.at[me], local_sem))
        for cp in copies:
            cp.start()
        for cp in copies:
            cp.wait()

    hbm = pl.BlockSpec(memory_space=pl.ANY)
    return pl.pallas_call(
        body, name=name, in_specs=[hbm], out_specs=hbm, out_shape=jax.ShapeDtypeStruct((N_DEV, *small.shape), small.dtype),
        scratch_shapes=[pltpu.SemaphoreType.DMA((N_DEV - 1,)), pltpu.SemaphoreType.DMA((N_DEV - 1,)), pltpu.SemaphoreType.DMA],
        compiler_params=pltpu.CompilerParams(has_side_effects=True),
    )(small)


_HBM = pl.BlockSpec(memory_space=pltpu.HBM)
_SEM = pl.BlockSpec(memory_space=pltpu.SEMAPHORE)


def _exchange_copies(srcs, zones, send_sems, recv_sems, gather):
    me, peers = _me_and_peers()
    return [pltpu.make_async_remote_copy(
        src_ref=srcs[t] if gather else srcs[t].at[peer], dst_ref=zones[t].at[me], send_sem=send_sems.at[7 * t + kk],
        recv_sem=recv_sems.at[7 * t + kk], device_id=pos, device_id_type=pl.DeviceIdType.MESH)
        for t in range(len(srcs)) for kk, (pos, peer) in enumerate(peers)]


def _exchange_start(tensors, *, gather, name):
    n = len(tensors)
    zones = [lax.empty((N_DEV, *(t.shape if gather else t.shape[1:])), t.dtype) for t in tensors]

    def body(*refs):
        for cp in _exchange_copies(refs[:n], refs[n:2 * n], refs[2 * n], refs[2 * n + 1], gather):
            cp.start()
        refs[-1][...] = jnp.zeros_like(refs[-1])

    buffers = [pltpu.HBM(a.shape, a.dtype) for a in tensors + zones]
    res = pl.pallas_call(
        body, name=name, in_specs=[_HBM] * (2 * n),
        out_shape=(pltpu.SemaphoreType.DMA((7 * n,)), pltpu.SemaphoreType.DMA((7 * n,)), *buffers, jax.ShapeDtypeStruct((8, LANES), F32)),
        out_specs=(_SEM, _SEM, *[_HBM] * (2 * n), pl.BlockSpec(memory_space=pltpu.VMEM)),
        input_output_aliases={i: 2 + i for i in range(2 * n)},
        compiler_params=pltpu.CompilerParams(has_side_effects=pltpu.SideEffectType.DATAFLOW_SIDE_EFFECTING),
    )(*[pltpu.with_memory_space_constraint(a, pltpu.HBM) for a in tensors + zones])
    return dict(sems=res[:2], buffers=res[2:2 + 2 * n], gather=gather, started=res[-1])


def _exchange_wait(started, after, *, name):
    n = len(started["buffers"]) // 2

    def body(*refs):
        for cp in _exchange_copies(refs[:n], refs[n:2 * n], refs[2 * n], refs[2 * n + 1], started["gather"]):
            cp.wait_send()
            cp.wait_recv()

    res = pl.pallas_call(
        body, name=name, in_specs=[_HBM] * (2 * n) + [_SEM, _SEM, pl.BlockSpec(memory_space=pl.ANY)],
        out_shape=tuple(pltpu.HBM(a.shape, a.dtype) for a in started["buffers"]), out_specs=tuple([_HBM] * (2 * n)),
        input_output_aliases={i: i for i in range(2 * n)},
        compiler_params=pltpu.CompilerParams(has_side_effects=pltpu.SideEffectType.DATAFLOW_SIDE_EFFECTING),
    )(*started["buffers"], *started["sems"], after)
    return res[:n], res[n:]


def _adamw(contrib, w, m, v, *, name):
    rows, cols = w.shape
    tile = min(rows, ADAM_ROWS)

    def body(c_ref, w_ref, m_ref, v_ref, g_ref, d_ref, nm_ref, nv_ref):
        g = c_ref[0].astype(F32)
        for s in range(1, N_DEV):
            g = g + c_ref[s].astype(F32)
        m_new = ADAM_B1 * m_ref[...] + (1.0 - ADAM_B1) * g
        v_new = ADAM_B2 * v_ref[...] + (1.0 - ADAM_B2) * (g * g)
        m_hat = m_new / (1.0 - ADAM_B1 ** ADAM_STEP)
        v_hat = v_new / (1.0 - ADAM_B2 ** ADAM_STEP)
        g_ref[...] = g
        d_ref[...] = -ADAM_LR * (m_hat / (jnp.sqrt(v_hat) + ADAM_EPS) + ADAM_WD * w_ref[...])
        nm_ref[...] = m_new
        nv_ref[...] = v_new

    spec = pl.BlockSpec((tile, cols), lambda i: (i, 0))
    return pl.pallas_call(
        body, name=name, grid=(rows // tile,),
        in_specs=[pl.BlockSpec((N_DEV, tile, cols), lambda i: (0, i, 0)), spec, spec, spec], out_specs=[spec] * 4,
        out_shape=[jax.ShapeDtypeStruct((rows, cols), F32)] * 4, compiler_params=_cparams("parallel"),
    )(contrib, w, m, v)


class _Weights:
    def __init__(self, gathers, vectors, me):
        self.gathers, self.ready, self.me, self.after = gathers, dict(vectors), me, None

    def arrive_after(self, value):
        self.after = value

    def __getitem__(self, name):
        if name not in self.ready:
            gi = next(i for i, group in enumerate(GATHER_GROUPS) if name in group)
            shards, zones = _exchange_wait(self.gathers[gi], self.after, name=f"gather_wait_{gi}")
            for n, shard, zone in zip(GATHER_GROUPS[gi], shards, zones, strict=True):
                blocks = lax.dynamic_update_slice_in_dim(zone, shard[None], self.me, 0)
                self.ready[n] = _to_kernel_layout(n, _from_blocks(blocks, n))
        return self.ready[name]


def kernel(x, mem, w_in, w_mem_kv, q_a_gain, w_q_b, kv_a_gain, w_kv_b, w_branch_mla, w_branch_sb, w_branch_mem, w_merge_gate, b_merge_gate, w_out, ln_gain, ln_bias, loss_target, m_w_in, m_w_mem_kv, m_q_a_gain, m_w_q_b, m_kv_a_gain, m_w_kv_b, m_w_branch_mla, m_w_branch_sb, m_w_branch_mem, m_w_merge_gate, m_b_merge_gate, m_w_out, m_ln_gain, m_ln_bias, v_w_in, v_w_mem_kv, v_q_a_gain, v_w_q_b, v_kv_a_gain, v_w_kv_b, v_w_branch_mla, v_w_branch_sb, v_w_branch_mem, v_w_merge_gate, v_b_merge_gate, v_w_out, v_ln_gain, v_ln_bias):
    given = dict(locals())
    small_names = [n for n, _ in SMALL]
    smalls = lambda prefix: [given[prefix + n] for n in small_names]
    me = 4 * lax.axis_index("x") + 2 * lax.axis_index("y") + lax.axis_index("c")

    gathers = [_exchange_start([given[n][0].astype(BF16) for n in group], gather=True, name=f"gather_start_{gi}")
               for gi, group in enumerate(GATHER_GROUPS)]
    w = _Weights(gathers, {n: given[n] for n in small_names}, me)
    exchanges = []

    def emit(grads):
        blocks = [_to_blocks(_from_kernel_layout(n, g), n).astype(BF16) for n, g in grads.items()]
        exchanges.append((tuple(grads), _exchange_start(blocks, gather=False, name=f"grads_start_{len(exchanges)}")))
        return exchanges[-1][1]["started"]

    loss, grad_x, grads = _local_step(x[0], mem[0], loss_target[0], w, emit)

    contrib_small = _share_small(_pack_small([grads[n] for n in small_names], loss), name="share_small")
    sml = _adamw(contrib_small, _pack_small(smalls("")), _pack_small(smalls("m_")), _pack_small(smalls("v_")), name="adamw_small")
    results = [dict(zip(small_names, _unpack_small(packed), strict=True)) for packed in sml]
    for gi, (names, started) in enumerate(exchanges):
        sent, zones = _exchange_wait(started, grad_x, name=f"grads_wait_{gi}")
        for n, blocks, zone in zip(names, sent, zones, strict=True):
            own = lax.dynamic_index_in_dim(blocks, me, 0, keepdims=True)
            contrib = lax.dynamic_update_slice_in_dim(zone, own, me, 0)
            for kind, res in zip(results, _adamw(contrib, given[n][0], given["m_" + n][0], given["v_" + n][0], name=f"adamw_{n}"), strict=True):
                kind[n] = res[None]
    order = ["w_in", "w_mem_kv", "q_a_gain", "w_q_b", "kv_a_gain", "w_kv_b", "w_branch_mla", "w_branch_sb", "w_branch_mem",
             "w_merge_gate", "b_merge_gate", "w_out", "ln_gain", "ln_bias"]
    loss_out = sml[0].reshape(-1)[LOSS_INDEX]
    return (loss_out, grad_x[None], *[kind[n] for kind in results for n in order])
```

```python
import math

import jax
import jax.numpy as jnp
from jax import lax
from jax.experimental import pallas as pl
from jax.experimental.pallas import tpu as pltpu

F32, BF16 = jnp.float32, jnp.bfloat16

N_DEV = 8
D_MODEL = 1024
MLA_HEADS, MLA_NOPE, MLA_ROPE, MLA_V = 8, 64, 32, 64
MLA_Q_LORA, MLA_KV_LORA = 256, 128
SB_HEAD_DIM = 64
MEM_HEAD_DIM = 128
ROPE_BASE = 10000.0
RMS_EPS = 1e-6
LN_EPS = 1e-5
DEEPNORM_ALPHA = 2.0 ** 0.25
ADAM_LR, ADAM_B1, ADAM_B2, ADAM_EPS, ADAM_WD, ADAM_STEP = 0.001, 0.9, 0.999, 1e-08, 0.01, 10
LOG2E, LN2 = math.log2(math.e), math.log(2.0)

LANES = 128
GROUPS = 4
PROJ_WIDTH = 4096
COL_CQ, COL_CKV, COL_KROPE, COL_GATE_A, COL_GATE_B, COL_GATE_M = 0, 256, 384, 512, 1024, 1536
QKV_FIRST, QKV_WIDTH = 2048, 2048
COL_QB, COL_KB, COL_VB, COL_QM = 0, 512, 1024, 1536
IN_PIECES = ((0, 416), None, (416, 512), (2464, 512), (3488, 512), (928, 512), (1440, 512), (1952, 512), (2976, 512))
IN_PAD = 96

VMEM_LIMIT_BYTES = 56 * 1024 * 1024
NEG_BIG = -1e30
Q_BLOCK = 512
SB_BWD_Q_BLOCK = 512
TRI_BLOCK = 256
TILE_ROWS = 64
KEY_CHUNK = 512

SHARDED = {
    "w_in": ((1024, 4000), 1), "w_mem_kv": ((1024, 1024), 0), "w_q_b": ((256, 768), 1), "w_kv_b": ((128, 1024), 1),
    "w_branch_mla": ((512, 1024), 1), "w_branch_sb": ((512, 1024), 1), "w_branch_mem": ((512, 1024), 1),
    "w_merge_gate": ((1024, 3072), 1), "w_out": ((1024, 1024), 0),
}
GATHER_GROUPS = (("w_in",), ("w_merge_gate",), ("w_q_b", "w_kv_b", "w_mem_kv", "w_branch_mla", "w_branch_sb", "w_branch_mem", "w_out"))
GRAD_GROUPS = (("w_out", "w_merge_gate", "w_branch_mla", "w_branch_sb", "w_branch_mem"), ("w_mem_kv", "w_q_b", "w_kv_b"), ("w_in",))
SMALL = (("q_a_gain", 256), ("kv_a_gain", 128), ("b_merge_gate", 3072), ("ln_gain", 1024), ("ln_bias", 1024))
SMALL_ROWS, SMALL_LANES = 48, 128
ADAM_ROWS = 256
LOSS_INDEX = 5504


def _cparams(*sem):
    return pltpu.CompilerParams(dimension_semantics=sem or None, vmem_limit_bytes=VMEM_LIMIT_BYTES)


_DIMS = {"nn": (((1,), (0,)), ((), ())), "nt": (((1,), (1,)), ((), ())), "tn": (((0,), (0,)), ((), ()))}


def _dot(a, b, dims):
    return lax.dot_general(a, b, _DIMS[dims], preferred_element_type=F32)


def _tile(dim, want):
    if dim <= want:
        return dim
    t = want - want % LANES
    while dim % t:
        t -= LANES
    assert t > 0, (dim, want)
    return t


_ANY = pl.BlockSpec(memory_space=pl.ANY)


def _mm(a, b, dims, *, name, out_dtype=F32, add=None, add_scale=1.0, col_scale=None, b_cols=None, behind=None,
        tm=1024, tn=1024, tk=1024):
    if dims == "nn":
        (m, k), (k2, n) = a.shape, b.shape
    elif dims == "nt":
        (m, k), (n, k2) = a.shape, b.shape
    else:
        (k, m), (k2, n) = a.shape, b.shape
    assert k == k2, (a.shape, b.shape, dims)
    b_first = 0
    if b_cols is not None:
        assert dims == "nn"
        b_first, n = b_cols
    tm, tn, tk = _tile(m, tm), _tile(n, tn), _tile(k, tk)
    assert b_first % tn == 0
    jb = b_first // tn
    nk = k // tk
    a_spec = pl.BlockSpec((tk, tm), lambda i, j, kk: (kk, i)) if dims == "tn" else pl.BlockSpec((tm, tk), lambda i, j, kk: (i, kk))
    b_spec = pl.BlockSpec((tn, tk), lambda i, j, kk: (j, kk)) if dims == "nt" else pl.BlockSpec((tk, tn), lambda i, j, kk: (kk, jb + j))
    o_spec = pl.BlockSpec((tm, tn), lambda i, j, kk: (i, j))
    optional = [(add, o_spec), (col_scale, pl.BlockSpec((1, tn), lambda i, j, kk: (0, j))), (behind, _ANY)]
    present = [(v, spec) for v, spec in optional if v is not None]

    def body(*refs):
        a_ref, b_ref = refs[:2]
        extra = iter(refs[2:2 + len(present)])
        add_ref = next(extra) if add is not None else None
        scale_ref = next(extra) if col_scale is not None else None
        o_ref = refs[2 + len(present)]
        part = _dot(a_ref[...].astype(BF16), b_ref[...].astype(BF16), dims)

        def finish(r):
            if add is not None:
                r = r + add_scale * add_ref[...]
            if col_scale is not None:
                r = r * scale_ref[...]
            o_ref[...] = r.astype(out_dtype)

        if nk == 1:
            finish(part)
            return
        acc = refs[-1]
        kk = pl.program_id(2)

        @pl.when(kk == 0)
        def _():
            acc[...] = part

        @pl.when(kk > 0)
        def _():
            acc[...] += part

        @pl.when(kk == nk - 1)
        def _():
            finish(acc[...])

    return pl.pallas_call(
        body, name=name, grid=(m // tm, n // tn, nk),
        in_specs=[a_spec, b_spec] + [spec for _, spec in present], out_specs=o_spec,
        out_shape=jax.ShapeDtypeStruct((m, n), out_dtype), scratch_shapes=[pltpu.VMEM((tm, tn), F32)] if nk > 1 else [],
        compiler_params=_cparams("parallel", "parallel", "arbitrary"),
    )(a, b, *[v for v, _ in present])


def _rowwise(fn, ins, outs, *, name, rows, tr=256):
    n_in = len(ins)
    in_specs, args = [], []
    for it in ins:
        arr, w, off = it if isinstance(it, tuple) else (it, it.shape[-1], 0)
        assert off % w == 0
        cb = off // w
        if arr.shape[0] == 1:
            in_specs.append(pl.BlockSpec((1, w), lambda i, cb=cb: (0, cb)))
        else:
            in_specs.append(pl.BlockSpec((tr, w), lambda i, cb=cb: (i, cb)))
        args.append(arr)
    out_shape, out_specs, is_sum = [], [], []
    for kind, d in outs:
        if kind == "sum":
            out_shape.append(jax.ShapeDtypeStruct((1, d), F32))
            out_specs.append(pl.BlockSpec((1, d), lambda i: (0, 0)))
            is_sum.append(True)
        else:
            out_shape.append(jax.ShapeDtypeStruct((rows, kind), d))
            out_specs.append(pl.BlockSpec((tr, kind), lambda i: (i, 0)))
            is_sum.append(False)

    def body(*refs):
        res = fn(*[r[...] for r in refs[:n_in]])
        for r, val, s in zip(refs[n_in:], res, is_sum, strict=True):
            if s:
                @pl.when(pl.program_id(0) == 0)
                def _(r=r):
                    r[...] = jnp.zeros_like(r)

                r[...] += val
            else:
                r[...] = val.astype(r.dtype)

    return pl.pallas_call(
        body, name=name, grid=(rows // tr,), in_specs=in_specs, out_specs=out_specs, out_shape=out_shape,
        compiler_params=_cparams("arbitrary"),
    )(*args)


def _colsum(v):
    return jnp.sum(v, axis=0, keepdims=True)


def _sigmoid(v):
    return 1.0 / (1.0 + jnp.exp(-v))


def _lane_groups(v):
    return [v[:, g * LANES:(g + 1) * LANES] for g in range(v.shape[1] // LANES)]


def _swap_halves(v, first_lane):
    lane = lax.broadcasted_iota(jnp.int32, v.shape, 1)
    return jnp.where(lane < first_lane + 16, pltpu.roll(v, 112, axis=1), pltpu.roll(v, 16, axis=1))


def _lane_sum(acc, v):
    for part in _lane_groups(v):
        acc = acc + part
    return acc


def _low_half(shape):
    return lax.broadcasted_iota(jnp.int32, shape, 1) < LANES // 2


def _select_heads(per_head, pick):
    if len(per_head) == 1:
        return pick(per_head[0], 0)
    return jnp.where(_low_half(per_head[0].shape), pick(per_head[0], 0), pick(per_head[1], 1))


def _attn_specs(s, sk, hp, bq, q0, k0, v0):
    wq = hp * LANES
    assert q0 % wq == 0 and k0 % wq == 0 and v0 % LANES == 0
    qb0, kb0, vb0 = q0 // wq, k0 // wq, v0 // LANES
    q_spec = pl.BlockSpec((bq, wq), lambda g, i: (i, qb0 + g))
    k_spec = pl.BlockSpec((sk, wq), lambda g, i: (0, kb0 + g))
    v_spec = pl.BlockSpec((sk, LANES), lambda g, i: (0, vb0 + g))
    row_out = lambda w: pl.BlockSpec((bq, w), lambda g, i: (i, g))
    key_out = lambda w: pl.BlockSpec((sk, w), lambda g, i: (0, g))
    return q_spec, k_spec, v_spec, row_out, key_out


def _chunks(i, bq, ch, sk, causal):
    return ((i + 1) * bq - 1) // ch if causal else jnp.int32(sk // ch - 1)


def _positions(i, c, bq, ch):
    return (i * bq + lax.broadcasted_iota(jnp.int32, (bq, ch), 0), c * ch + lax.broadcasted_iota(jnp.int32, (bq, ch), 1))


def _softmax_fwd(q, k, v, *, hp, causal, name, q0=0, k0=0, v0=0):
    s, sk = q.shape[0], k.shape[0]
    bq, ch = min(Q_BLOCK, s), min(KEY_CHUNK, sk)
    q_spec, k_spec, v_spec, row_out, _ = _attn_specs(s, sk, hp, bq, q0, k0, v0)

    def body(q_ref, k_ref, v_ref, o_ref, lse_ref, s_scr):
        i = pl.program_id(1)
        qs = _lane_groups(q_ref[...])
        last = _chunks(i, bq, ch, sk, causal)

        def scores(c, ms, masked):
            off = pl.multiple_of(c * ch, ch)
            out = []
            for j in range(hp):
                sc = _dot(qs[j], k_ref[pl.ds(off, ch), j * LANES:(j + 1) * LANES], "nt")
                if masked:
                    qpos, kpos = _positions(i, c, bq, ch)
                    sc = jnp.where(kpos <= qpos, sc, NEG_BIG)
                s_scr[j, c] = sc
                m = ms[j]
                for part in _lane_groups(sc):
                    m = jnp.maximum(m, part)
                out.append(m)
            return tuple(out)

        ms = lax.fori_loop(0, last, lambda c, m: scores(c, m, False), tuple(jnp.full((bq, LANES), NEG_BIG, F32) for _ in range(hp)))
        ms = scores(last, ms, causal)
        row_max = [jnp.max(m, axis=1, keepdims=True) for m in ms]

        def weigh(c, carry):
            off = pl.multiple_of(c * ch, ch)
            vt = v_ref[pl.ds(off, ch), :]
            out = []
            for j in range(hp):
                l, acc = carry[j]
                p = jnp.exp2(s_scr[j, c] - row_max[j])
                out.append((_lane_sum(l, p), acc + _dot(p.astype(BF16), vt, "nn")))
            return tuple(out)

        zero = jnp.zeros((bq, LANES), F32)
        res = lax.fori_loop(0, last + 1, weigh, tuple((zero, zero) for _ in range(hp)))
        row_sum = [jnp.sum(l, axis=1, keepdims=True) for l, _ in res]
        o_ref[...] = _select_heads([acc for _, acc in res], lambda acc, j: acc / row_sum[j])
        lse_ref[...] = _select_heads([jnp.broadcast_to(row_max[j] + jnp.log2(row_sum[j]), (bq, LANES)) for j in range(hp)], lambda a, j: a)

    return pl.pallas_call(
        body, name=name, grid=(GROUPS, s // bq), in_specs=[q_spec, k_spec, v_spec], out_specs=[row_out(LANES), row_out(LANES)],
        out_shape=[jax.ShapeDtypeStruct((s, GROUPS * LANES), F32)] * 2,
        scratch_shapes=[pltpu.VMEM((hp, sk // ch, bq, ch), F32)], compiler_params=_cparams("parallel", "arbitrary"),
    )(q, k, v)


def _head_cotangent(do, j, hp):
    if hp == 1:
        return do
    return jnp.where(_low_half(do.shape) == (j == 0), do, 0.0)


def _softmax_bwd(q, k, v, o, do, lse, behind, *, hp, causal, dq_scale, name, q0=0, k0=0, v0=0):
    s, sk = q.shape[0], k.shape[0]
    bq, ch = min(Q_BLOCK, s), min(KEY_CHUNK, sk)
    wq = hp * LANES
    q_spec, k_spec, v_spec, row_out, key_out = _attn_specs(s, sk, hp, bq, q0, k0, v0)

    def body(q_ref, k_ref, v_ref, o_ref, do_ref, lse_ref, _, dq_ref, dk_ref, dv_ref):
        i = pl.program_id(1)

        @pl.when(i == 0)
        def _():
            dk_ref[...] = jnp.zeros_like(dk_ref)
            dv_ref[...] = jnp.zeros_like(dv_ref)

        qs = _lane_groups(q_ref[...])
        do_all, o_all, lse_all = do_ref[...], o_ref[...], lse_ref[...]
        dos, deltas, lses = [], [], []
        for j in range(hp):
            d = _head_cotangent(do_all, j, hp)
            deltas.append(jnp.sum(d * o_all, axis=1, keepdims=True))
            dos.append(d.astype(BF16))
            lses.append(lse_all[:, j * (LANES // hp):j * (LANES // hp) + 1])
        last = _chunks(i, bq, ch, sk, causal)

        def chunk(c, dqs, masked):
            off = pl.multiple_of(c * ch, ch)
            vt = v_ref[pl.ds(off, ch), :]
            out, dks, dv = [], [], None
            for j in range(hp):
                kt = k_ref[pl.ds(off, ch), j * LANES:(j + 1) * LANES]
                p = jnp.exp2(_dot(qs[j], kt, "nt") - lses[j])
                if masked:
                    qpos, kpos = _positions(i, c, bq, ch)
                    p = jnp.where(kpos <= qpos, p, 0.0)
                ds = (p * (_dot(dos[j], vt, "nt") - deltas[j]) * LN2).astype(BF16)
                out.append(dqs[j] + _dot(ds, kt, "nn"))
                dks.append(_dot(ds, qs[j], "tn"))
                dvj = _dot(p.astype(BF16), dos[j], "tn")
                dv = dvj if dv is None else dv + dvj
            dk_ref[pl.ds(off, ch), :] += dks[0] if hp == 1 else jnp.concatenate(dks, axis=1)
            dv_ref[pl.ds(off, ch), :] += dv
            return tuple(out)

        dqs = lax.fori_loop(0, last, lambda c, d: chunk(c, d, False), tuple(jnp.zeros((bq, LANES), F32) for _ in range(hp)))
        dqs = chunk(last, dqs, causal)
        dq_ref[...] = (dqs[0] if hp == 1 else jnp.concatenate(dqs, axis=1)) * dq_scale

    return pl.pallas_call(
        body, name=name, grid=(GROUPS, s // bq),
        in_specs=[q_spec, k_spec, v_spec, row_out(LANES), row_out(LANES), row_out(LANES), _ANY],
        out_specs=[row_out(wq), key_out(wq), key_out(LANES)],
        out_shape=[jax.ShapeDtypeStruct((s, GROUPS * wq), F32), jax.ShapeDtypeStruct((sk, GROUPS * wq), F32),
                   jax.ShapeDtypeStruct((sk, GROUPS * LANES), F32)],
        compiler_params=_cparams("arbitrary", "arbitrary"),
    )(q, k, v, o, do, lse, behind)


def _log2_sigmoid_pair(z2):
    minus_abs = lax.bitcast_convert_type(lax.bitcast_convert_type(z2, jnp.uint32) | jnp.uint32(0x80000000), F32)
    log_beta = jnp.minimum(z2, 0.0) - jnp.log2(1.0 + jnp.exp2(minus_abs))
    return log_beta, log_beta - z2


def _tilewise(fn, *arrays):
    rows, cols = arrays[0].shape
    step = min(TILE_ROWS, rows)
    grid = [[fn(*[None if a is None else a[r:r + step, c:c + LANES] for a in arrays]) for c in range(0, cols, LANES)]
            for r in range(0, rows, step)]
    return [jnp.concatenate([jnp.concatenate([cell[k] for cell in row], axis=1) for row in grid], axis=0)
            for k in range(len(grid[0][0]))]


def _split(v):
    hi = v.astype(BF16)
    return hi, (v - hi.astype(F32)).astype(BF16)


def _tri(n, after):
    rows, cols = lax.broadcasted_iota(jnp.int32, (n, n), 0), lax.broadcasted_iota(jnp.int32, (n, n), 1)
    return (rows > cols if after else rows < cols).astype(BF16)


def _running_sums(v, terms, start, tri, backwards):
    n = tri.shape[0]
    n_blocks = v.shape[1] // n
    order = range(n_blocks - 1, -1, -1) if backwards else range(n_blocks)
    stacked = tri if len(terms) == 1 else jnp.concatenate([tri] * len(terms), axis=0)
    parts, run = [None] * n_blocks, start
    for t in order:
        cols = slice(t * n, (t + 1) * n)
        lhs = terms[0][:, cols] if len(terms) == 1 else jnp.concatenate([term[:, cols] for term in terms], axis=1)
        parts[t] = _dot(lhs, stacked, "nn") + run
        run = run + jnp.sum(v[:, cols], axis=1, keepdims=True)
    return (parts[0] if n_blocks == 1 else jnp.concatenate(parts, axis=1)), run


def _sb_weights(qm, kt, run, tri, strict):
    def logs(z2, keep):
        log_beta, log_keep = _log2_sigmoid_pair(z2)
        if keep is not None:
            log_keep = jnp.where(keep, log_keep, 0.0)
        return log_beta, log_keep, *_split(log_keep)

    log_beta, log_keep, hi, lo = _tilewise(logs, _dot(qm, kt, "nt"), strict)
    behind, run = _running_sums(log_keep, (hi, lo), run, tri, True)

    def weigh(log_beta, behind, keep):
        a = jnp.exp2(log_beta + behind)
        return (a if keep is None else jnp.where(keep, a, 0.0),)

    (a,) = _tilewise(weigh, log_beta, behind, strict)
    return a, log_beta, run


def _sb_queries(q_all):
    low = _low_half(q_all.shape)
    zero = jnp.zeros_like(q_all)
    return [jnp.where(low, q_all, zero), jnp.where(low, zero, q_all)]


def _sb_fwd(qkv, *, q0, k0, v0, name):
    s = qkv.shape[0]
    bq, ch = min(Q_BLOCK, s), min(KEY_CHUNK, s)
    q_spec, k_spec, v_spec, row_out, _ = _attn_specs(s, s, 1, bq, q0, k0, v0)

    def body(q_ref, k_ref, v_ref, o_ref):
        i = pl.program_id(1)
        qms = _sb_queries(q_ref[...])
        tri = _tri(min(TRI_BLOCK, ch), True)
        last = _chunks(i, bq, ch, s, True)

        def chunk(c, carry, masked):
            off = pl.multiple_of(c * ch, ch)
            kt, vt = k_ref[pl.ds(off, ch), :], v_ref[pl.ds(off, ch), :]
            strict = None
            if masked:
                qpos, kpos = _positions(i, c, bq, ch)
                strict = kpos < qpos
            out = []
            for j in range(2):
                run, acc = carry[j]
                a, _, run = _sb_weights(qms[j], kt, run, tri, strict)
                out.append((run, acc + _dot(a.astype(BF16), vt, "nn")))
            return tuple(out)

        carry = chunk(last, tuple((jnp.zeros((bq, 1), F32), jnp.zeros((bq, LANES), F32)) for _ in range(2)), True)
        res = lax.fori_loop(0, last, lambda n, c: chunk(last - 1 - n, c, False), carry)
        o_ref[...] = _select_heads([acc for _, acc in res], lambda acc, j: acc)

    return pl.pallas_call(
        body, name=name, grid=(GROUPS, s // bq), in_specs=[q_spec, k_spec, v_spec], out_specs=row_out(LANES),
        out_shape=jax.ShapeDtypeStruct((s, GROUPS * LANES), F32), compiler_params=_cparams("parallel", "arbitrary"),
    )(qkv, qkv, qkv)


def _sb_bwd(qkv, do, behind, *, q0, k0, v0, dq_scale, name):
    s = qkv.shape[0]
    bq, ch = min(SB_BWD_Q_BLOCK, s), min(KEY_CHUNK, s)
    q_spec, k_spec, v_spec, row_out, key_out = _attn_specs(s, s, 1, bq, q0, k0, v0)

    def body(q_ref, k_ref, v_ref, do_ref, _, dq_ref, dk_ref, dv_ref, g_s, beta_s):
        i = pl.program_id(1)

        @pl.when(i == 0)
        def _():
            dk_ref[...] = jnp.zeros_like(dk_ref)
            dv_ref[...] = jnp.zeros_like(dv_ref)

        qms = _sb_queries(q_ref[...])
        do_all = do_ref[...]
        dos = [_head_cotangent(do_all, j, 2).astype(BF16) for j in range(2)]
        dos_ln2 = [(_head_cotangent(do_all, j, 2) * LN2).astype(BF16) for j in range(2)]
        tri_after, tri_before = _tri(min(TRI_BLOCK, ch), True), _tri(min(TRI_BLOCK, ch), False)
        last = _chunks(i, bq, ch, s, True)

        def strict_mask(c):
            qpos, kpos = _positions(i, c, bq, ch)
            return kpos < qpos

        def sweep1(c, runs, masked):
            off = pl.multiple_of(c * ch, ch)
            kt, vt = k_ref[pl.ds(off, ch), :], v_ref[pl.ds(off, ch), :]
            strict = strict_mask(c) if masked else None
            out, dv = [], None
            for j in range(2):
                a, log_beta, run = _sb_weights(qms[j], kt, runs[j], tri_after, strict)
                g_s[j, c] = (a * _dot(dos_ln2[j], vt, "nt")).astype(BF16)
                beta_s[j, c] = jnp.exp2(log_beta).astype(BF16)
                dvj = _dot(a.astype(BF16), dos[j], "tn")
                dv = dvj if dv is None else dv + dvj
                out.append(run)
            dv_ref[pl.ds(off, ch), :] += dv
            return tuple(out)

        runs = sweep1(last, tuple(jnp.zeros((bq, 1), F32) for _ in range(2)), True)
        lax.fori_loop(0, last, lambda n, r: sweep1(last - 1 - n, r, False), runs)

        def sweep2(c, carry, masked):
            off = pl.multiple_of(c * ch, ch)
            kt = k_ref[pl.ds(off, ch), :]
            out, dk = [], None
            for j in range(2):
                before, dq = carry[j]
                g16, beta = g_s[j, c], beta_s[j, c].astype(F32)
                g = g16.astype(F32)
                in_front, before = _running_sums(g, (g16,), before, tri_before, False)
                dz = g * (1.0 - beta) - beta * in_front
                if masked:
                    dz = jnp.where(strict_mask(c), dz, 0.0)
                dz = dz.astype(BF16)
                dkj = _dot(dz, qms[j], "tn")
                dk = dkj if dk is None else dk + dkj
                out.append((before, dq + _dot(dz, kt, "nn")))
            dk_ref[pl.ds(off, ch), :] += dk
            return tuple(out)

        carry = lax.fori_loop(0, last, lambda c, cr: sweep2(c, cr, False),
                              tuple((jnp.zeros((bq, 1), F32), jnp.zeros((bq, LANES), F32)) for _ in range(2)))
        res = sweep2(last, carry, True)
        dq_ref[...] = _select_heads([dq for _, dq in res], lambda dq, j: dq) * dq_scale

    n_ch = s // ch
    return pl.pallas_call(
        body, name=name, grid=(GROUPS, s // bq), in_specs=[q_spec, k_spec, v_spec, row_out(LANES), _ANY],
        out_specs=[row_out(LANES), key_out(LANES), key_out(LANES)],
        out_shape=[jax.ShapeDtypeStruct((s, GROUPS * LANES), F32)] * 3,
        scratch_shapes=[pltpu.VMEM((2, n_ch, bq, ch), BF16)] * 2,
        compiler_params=_cparams("arbitrary", "arbitrary"),
    )(qkv, qkv, qkv, do, behind)


def _rope_tables(s):
    half = MLA_ROPE // 2
    freqs = ROPE_BASE ** (-jnp.arange(half, dtype=F32) / half)
    ang = jnp.arange(s, dtype=F32)[:, None] * freqs[None, :]
    cos, sin = jnp.cos(ang), jnp.sin(ang)
    tail = jnp.zeros((s, LANES - MLA_NOPE - MLA_ROPE), F32)
    lead = lambda fill: jnp.full((s, MLA_NOPE), fill, F32)
    return dict(
        cos_k0=jnp.concatenate([cos, cos, lead(0.0), tail], axis=1), sin_k0=jnp.concatenate([-sin, sin, lead(0.0), tail], axis=1),
        cos_k64=jnp.concatenate([lead(0.0), cos, cos, tail], axis=1), sin_k64=jnp.concatenate([lead(0.0), -sin, sin, tail], axis=1),
        cos_q=jnp.concatenate([lead(1.0), cos, cos, tail], axis=1),
        sin_k64_t=jnp.concatenate([lead(0.0), sin, -sin, tail], axis=1),
    )


def _local_step(x, mem, target, w, emit=lambda grads: jnp.zeros((8, LANES), F32)):
    s = x.shape[0]
    rope = _rope_tables(s)
    xb = x.astype(BF16)
    inv_d = 1.0 / D_MODEL
    scale_a = LOG2E / math.sqrt(MLA_NOPE + MLA_ROPE)
    scale_b = LOG2E / math.sqrt(SB_HEAD_DIM)
    scale_m = LOG2E / math.sqrt(MEM_HEAD_DIM)
    arrive_after = getattr(w, "arrive_after", lambda value: None)

    arrive_after(xb)
    proj = _mm(xb, w["w_in"], "nn", name="proj", b_cols=(0, QKV_FIRST))
    one = jnp.ones((1, 512), F32)
    qkv = _mm(xb, w["w_in"], "nn", name="proj_qkv", b_cols=(QKV_FIRST, QKV_WIDTH), out_dtype=BF16,
              col_scale=jnp.concatenate([one * scale_b, one, one, one * scale_m], axis=1))
    arrive_after(qkv)
    pre = _mm(xb, w["w_merge_gate"], "nn", name="merge_pre")
    arrive_after(pre)

    def rms_pair(c_q, c_kv, g_q, g_kv):
        return (c_q * lax.rsqrt(jnp.mean(c_q * c_q, axis=1, keepdims=True) + RMS_EPS) * g_q,
                c_kv * lax.rsqrt(jnp.mean(c_kv * c_kv, axis=1, keepdims=True) + RMS_EPS) * g_kv)

    n_q, n_kv = _rowwise(rms_pair, [(proj, 256, COL_CQ), (proj, 128, COL_CKV), w["q_a_gain"], w["kv_a_gain"]],
                         [(256, BF16), (128, BF16)], name="rms_pair", rows=s)
    q_a = _mm(n_q, w["w_q_b"], "nn", name="q_up")
    kv_a = _mm(n_kv, w["w_kv_b"], "nn", name="kv_up", out_dtype=BF16)

    def rope_q(qa, cos, sin):
        return (jnp.concatenate([(g * cos + _swap_halves(g, MLA_NOPE) * sin) * scale_a for g in _lane_groups(qa)], axis=1),)

    (q_mla,) = _rowwise(rope_q, [q_a, rope["cos_q"], rope["sin_k64"]], [(1024, BF16)], name="rope_q", rows=s)

    def rope_k(k_nope, k_rope, cos, sin):
        k_pe = pltpu.roll(k_rope * cos + _swap_halves(k_rope, 0) * sin, MLA_NOPE, axis=1).astype(BF16)
        return (jnp.concatenate([g + k_pe for g in _lane_groups(k_nope)], axis=1),)

    (k_mla,) = _rowwise(rope_k, [(kv_a, 1024, 0), (proj, 128, COL_KROPE), rope["cos_k0"], rope["sin_k0"]],
                        [(1024, BF16)], name="rope_k", rows=s)
    o_a, lse_a = _softmax_fwd(q_mla, k_mla, kv_a, hp=2, causal=True, name="mla_fwd", v0=1024)

    o_b = _sb_fwd(qkv, q0=COL_QB, k0=COL_KB, v0=COL_VB, name="sb_fwd")

    memb = mem.astype(BF16)
    mem_kv = _mm(memb, w["w_mem_kv"], "nn", name="mem_kv", out_dtype=BF16)
    o_m, lse_m = _softmax_fwd(qkv, mem_kv, mem_kv, hp=1, causal=False, name="mem_fwd", q0=1536, v0=512)

    o_br = {"mla": o_a, "sb": o_b, "mem": o_m}
    gate_col = {"mla": COL_GATE_A, "sb": COL_GATE_B, "mem": COL_GATE_M}

    def gated(o, gate):
        return (o * gate * _sigmoid(gate),)

    u, y = {}, {}
    for br in ("mla", "sb", "mem"):
        (u[br],) = _rowwise(gated, [o_br[br], (proj, 512, gate_col[br])], [(512, BF16)], name=f"gated_{br}", rows=s)
        y[br] = _mm(u[br], w[f"w_branch_{br}"], "nn", name=f"branch_{br}")

    def merge(pa, pb, pm, ba, bb, bm, ya, yb, ym):
        return (_sigmoid(pa + ba) * ya + _sigmoid(pb + bb) * yb + _sigmoid(pm + bm) * ym,)

    bias = w["b_merge_gate"]
    gate_ins = [(pre, 1024, 0), (pre, 1024, 1024), (pre, 1024, 2048), (bias, 1024, 0), (bias, 1024, 1024), (bias, 1024, 2048)]
    (merged,) = _rowwise(merge, gate_ins + [y["mla"], y["sb"], y["mem"]], [(1024, BF16)], name="merge", rows=s)
    out = _mm(merged, w["w_out"], "nn", name="out_proj")

    def norm_loss(xv, ov, tv, gain, bias_ln):
        z = DEEPNORM_ALPHA * xv + ov
        zc = z - jnp.mean(z, axis=1, keepdims=True)
        rstd = lax.rsqrt(jnp.mean(zc * zc, axis=1, keepdims=True) + LN_EPS)
        xhat = zc * rstd
        err = xhat * gain + bias_ln - tv
        loss = 0.5 * jnp.sum(jnp.mean(err * err, axis=1, keepdims=True), axis=0, keepdims=True)
        dy = err * inv_d
        dxhat = dy * gain
        dz = rstd * (dxhat - jnp.mean(dxhat, axis=1, keepdims=True) - xhat * jnp.mean(dxhat * xhat, axis=1, keepdims=True))
        return dz, dz, _colsum(dy * xhat), _colsum(dy), jnp.broadcast_to(loss, (1, LANES))

    dz, dzb, g_ln_gain, g_ln_bias, loss = _rowwise(
        norm_loss, [x, out, target, w["ln_gain"], w["ln_bias"]],
        [(1024, F32), (1024, BF16), ("sum", 1024), ("sum", 1024), ("sum", LANES)], name="norm_loss", rows=s)

    grads = {"ln_gain": g_ln_gain, "ln_bias": g_ln_bias}
    dmerged = _mm(dzb, w["w_out"], "nt", name="d_merged")
    grads["w_out"] = _mm(merged, dzb, "tn", name="g_w_out")

    def merge_bwd(dm, pa, pb, pm, ba, bb, bm, ya, yb, ym):
        res, dpre = [], []
        for p, b, yv in ((pa, ba, ya), (pb, bb, yb), (pm, bm, ym)):
            g = _sigmoid(p + b)
            dpre.append(dm * yv * g * (1.0 - g))
            res.append(dm * g)
        dpre = jnp.concatenate(dpre, axis=1)
        return dpre, _colsum(dpre), *res

    dpre, grads["b_merge_gate"], dy_a, dy_b, dy_m = _rowwise(
        merge_bwd, [dmerged] + gate_ins + [y["mla"], y["sb"], y["mem"]],
        [(3072, BF16), ("sum", 3072), (1024, BF16), (1024, BF16), (1024, BF16)], name="merge_bwd", rows=s, tr=128)
    grads["w_merge_gate"] = _mm(xb, dpre, "tn", name="g_w_merge")
    dx = _mm(dpre, w["w_merge_gate"], "nt", name="dx_merge", add=dz, add_scale=DEEPNORM_ALPHA)

    def gated_bwd(du, o, gate):
        sg = _sigmoid(gate)
        return du * gate * sg, du * o * sg * (1.0 + gate * (1.0 - sg))

    d_o, d_gate = {}, {}
    for br, dy in (("mla", dy_a), ("sb", dy_b), ("mem", dy_m)):
        grads[f"w_branch_{br}"] = _mm(u[br], dy, "tn", name=f"g_w_branch_{br}")
        du = _mm(dy, w[f"w_branch_{br}"], "nt", name=f"d_u_{br}")
        d_o[br], d_gate[br] = _rowwise(gated_bwd, [du, o_br[br], (proj, 512, gate_col[br])], [(512, F32), (512, BF16)],
                                       name=f"gated_bwd_{br}", rows=s)
    sent = emit({n: grads[n] for n in ("w_out", "w_merge_gate", "w_branch_mla", "w_branch_sb", "w_branch_mem")})

    dq_m, dk_m, dv_m = _softmax_bwd(qkv, mem_kv, mem_kv, o_m, d_o["mem"], lse_m, sent, hp=1, causal=False, dq_scale=scale_m,
                                    name="mem_bwd", q0=1536, v0=512)
    grads["w_mem_kv"] = _mm(memb, jnp.concatenate([dk_m, dv_m], axis=1), "tn", name="g_w_mem_kv")

    dq_sb, dk_sb, dv_sb = _sb_bwd(qkv, d_o["sb"], sent, q0=0, k0=512, v0=1024, dq_scale=scale_b, name="sb_bwd")

    dq_mla, dk_mla, dv_a = _softmax_bwd(q_mla, k_mla, kv_a, o_a, d_o["mla"], lse_a, sent, hp=2, causal=True, dq_scale=scale_a,
                                        name="mla_bwd", v0=1024)

    def rope_q_bwd(dq, cos, sin):
        return (jnp.concatenate([g * cos + _swap_halves(g, MLA_NOPE) * sin for g in _lane_groups(dq)], axis=1),)

    (dq_a,) = _rowwise(rope_q_bwd, [dq_mla, rope["cos_q"], rope["sin_k64_t"]], [(1024, BF16)], name="rope_q_bwd", rows=s)
    grads["w_q_b"] = _mm(n_q, dq_a, "tn", name="g_w_q_b")
    dn_q = _mm(dq_a, w["w_q_b"], "nt", name="d_n_q")

    def rope_k_bwd(dk, cos, sin):
        groups = _lane_groups(dk)
        g = groups[0]
        for other in groups[1:]:
            g = g + other
        d_rope = pltpu.roll(g * cos + _swap_halves(g, MLA_NOPE) * sin, MLA_NOPE, axis=1)
        nope = _low_half(g.shape)
        return d_rope, jnp.concatenate([jnp.where(nope, grp, 0.0) for grp in groups], axis=1)

    dk_rope, dk_nope = _rowwise(rope_k_bwd, [dk_mla, rope["cos_k64"], rope["sin_k64_t"]], [(128, BF16), (1024, BF16)],
                                name="rope_k_bwd", rows=s)
    grads["w_kv_b"] = jnp.concatenate([_mm(n_kv, dk_nope, "tn", name="g_w_kv_b_k"), _mm(n_kv, dv_a, "tn", name="g_w_kv_b_v")], axis=1)
    dn_kv = _mm(dk_nope, w["w_kv_b"][:, :1024], "nt", name="d_n_kv_k")
    dn_kv = _mm(dv_a, w["w_kv_b"][:, 1024:], "nt", name="d_n_kv_v", add=dn_kv)

    def rms_bwd(c_q, c_kv, dq, dkv, g_q, g_kv):
        res = []
        for c, dn, g in ((c_q, dq, g_q), (c_kv, dkv, g_kv)):
            r = lax.rsqrt(jnp.mean(c * c, axis=1, keepdims=True) + RMS_EPS)
            t = dn * g
            res += [r * t - c * (r * r * r) * jnp.mean(c * t, axis=1, keepdims=True), _colsum(dn * c * r)]
        return res

    dc_q, grads["q_a_gain"], dc_kv, grads["kv_a_gain"] = _rowwise(
        rms_bwd, [(proj, 256, COL_CQ), (proj, 128, COL_CKV), dn_q, dn_kv, w["q_a_gain"], w["kv_a_gain"]],
        [(256, BF16), ("sum", 256), (128, BF16), ("sum", 128)], name="rms_bwd", rows=s)

    sent = emit({n: grads[n] for n in ("w_mem_kv", "w_q_b", "w_kv_b")})

    dproj = jnp.concatenate(
        [dc_q, dc_kv, dk_rope, d_gate["mla"], d_gate["sb"], d_gate["mem"], dq_sb.astype(BF16), dk_sb.astype(BF16),
         dv_sb.astype(BF16), dq_m.astype(BF16)], axis=1)
    grads["w_in"] = _mm(xb, dproj, "tn", name="g_w_in", behind=sent)
    sent = emit({"w_in": grads["w_in"]})
    grad_x = _mm(dproj, w["w_in"], "nt", name="grad_x", add=dx, behind=sent)
    return loss, grad_x, grads


def _shard_shape(shape, axis):
    return tuple(d // N_DEV if a == axis else d for a, d in enumerate(shape))


def _from_blocks(blocks, name):
    shape, axis = SHARDED[name]
    return blocks.reshape(shape) if axis == 0 else blocks.transpose(1, 0, 2).reshape(shape)


def _to_blocks(full, name):
    shape, axis = SHARDED[name]
    shp = _shard_shape(shape, axis)
    return full.reshape(N_DEV, *shp) if axis == 0 else full.reshape(shape[0], N_DEV, shp[1]).transpose(1, 0, 2)


def _pad_heads(a, used):
    rows = a.shape[0]
    a = a.reshape(rows, MLA_HEADS, used)
    return jnp.concatenate([a, jnp.zeros((rows, MLA_HEADS, LANES - used), a.dtype)], axis=2).reshape(rows, MLA_HEADS * LANES)


def _to_kernel_layout(name, full):
    if name == "w_in":
        return jnp.concatenate([jnp.zeros((D_MODEL, IN_PAD), full.dtype) if piece is None else full[:, piece[0]:piece[0] + piece[1]]
                                for piece in IN_PIECES], axis=1)
    if name == "w_q_b":
        return _pad_heads(full, MLA_NOPE + MLA_ROPE)
    if name == "w_kv_b":
        kv = full.reshape(MLA_KV_LORA, MLA_HEADS, MLA_NOPE + MLA_V)
        return jnp.concatenate([_pad_heads(kv[:, :, :MLA_NOPE].reshape(MLA_KV_LORA, -1), MLA_NOPE),
                                kv[:, :, MLA_NOPE:].reshape(MLA_KV_LORA, -1)], axis=1)
    return full


def _from_kernel_layout(name, g):
    if name == "w_in":
        placed, at = [], 0
        for piece in IN_PIECES:
            if piece is not None:
                placed.append((piece[0], g[:, at:at + piece[1]]))
            at += IN_PAD if piece is None else piece[1]
        return jnp.concatenate([cols for _, cols in sorted(placed, key=lambda item: item[0])], axis=1)
    if name == "w_q_b":
        return g.reshape(MLA_Q_LORA, MLA_HEADS, LANES)[:, :, :MLA_NOPE + MLA_ROPE].reshape(MLA_Q_LORA, -1)
    if name == "w_kv_b":
        return jnp.concatenate([g[:, :1024].reshape(MLA_KV_LORA, MLA_HEADS, LANES)[:, :, :MLA_NOPE],
                                g[:, 1024:].reshape(MLA_KV_LORA, MLA_HEADS, MLA_V)], axis=2).reshape(MLA_KV_LORA, -1)
    return g


def _pack_small(vectors, loss=None):
    flat = [v.reshape(-1) for v in vectors]
    flat.append(jnp.zeros((SMALL_ROWS * SMALL_LANES - LOSS_INDEX,), F32) if loss is None else
                jnp.concatenate([loss.reshape(-1)[:1], jnp.zeros((SMALL_ROWS * SMALL_LANES - LOSS_INDEX - 1,), F32)]))
    return jnp.concatenate(flat).reshape(SMALL_ROWS, SMALL_LANES)


def _unpack_small(packed):
    flat, res, off = packed.reshape(-1), [], 0
    for _, n in SMALL:
        res.append(flat[off:off + n].reshape(1, n))
        off += n
    return res


def _me_and_peers():
    x, y, c = lax.axis_index("x"), lax.axis_index("y"), lax.axis_index("c")
    peers = []
    for kk in range(1, N_DEV):
        px, py, pc = (x + (kk >> 2)) % 2, (y + ((kk >> 1) & 1)) % 2, (c + (kk & 1)) % 2
        peers.append(((px, py, pc), 4 * px + 2 * py + pc))
    return 4 * x + 2 * y + c, peers


def _share_small(small, *, name):
    def body(small_ref, all_ref, send_sems, recv_sems, local_sem):
        me, peers = _me_and_peers()
        copies = [pltpu.make_async_remote_copy(src_ref=small_ref, dst_ref=all_ref.at[me], send_sem=send_sems.at[kk], recv_sem=recv_sems.at[kk],
                                               device_id=pos, device_id_type=pl.DeviceIdType.MESH) for kk, (pos, _) in enumerate(peers)]
        copies.append(pltpu.make_async_copy(small_ref, all_ref.at[me], local_sem))
        for cp in copies:
            cp.start()
        for cp in copies:
            cp.wait()

    hbm = pl.BlockSpec(memory_space=pl.ANY)
    return pl.pallas_call(
        body, name=name, in_specs=[hbm], out_specs=hbm, out_shape=jax.ShapeDtypeStruct((N_DEV, *small.shape), small.dtype),
        scratch_shapes=[pltpu.SemaphoreType.DMA((N_DEV - 1,)), pltpu.SemaphoreType.DMA((N_DEV - 1,)), pltpu.SemaphoreType.DMA],
        compiler_params=pltpu.CompilerParams(has_side_effects=True),
    )(small)


_HBM = pl.BlockSpec(memory_space=pltpu.HBM)
_SEM = pl.BlockSpec(memory_space=pltpu.SEMAPHORE)


def _exchange_copies(srcs, zones, send_sems, recv_sems, gather):
    me, peers = _me_and_peers()
    return [pltpu.make_async_remote_copy(
        src_ref=srcs[t] if gather else srcs[t].at[peer], dst_ref=zones[t].at[me], send_sem=send_sems.at[7 * t + kk],
        recv_sem=recv_sems.at[7 * t + kk], device_id=pos, device_id_type=pl.DeviceIdType.MESH)
        for t in range(len(srcs)) for kk, (pos, peer) in enumerate(peers)]


def _exchange_start(tensors, *, gather, name):
    n = len(tensors)
    zones = [lax.empty((N_DEV, *(t.shape if gather else t.shape[1:])), t.dtype) for t in tensors]

    def body(*refs):
        for cp in _exchange_copies(refs[:n], refs[n:2 * n], refs[2 * n], refs[2 * n + 1], gather):
            cp.start()
        refs[-1][...] = jnp.zeros_like(refs[-1])

    buffers = [pltpu.HBM(a.shape, a.dtype) for a in tensors + zones]
    res = pl.pallas_call(
        body, name=name, in_specs=[_HBM] * (2 * n),
        out_shape=(pltpu.SemaphoreType.DMA((7 * n,)), pltpu.SemaphoreType.DMA((7 * n,)), *buffers, jax.ShapeDtypeStruct((8, LANES), F32)),
        out_specs=(_SEM, _SEM, *[_HBM] * (2 * n), pl.BlockSpec(memory_space=pltpu.VMEM)),
        input_output_aliases={i: 2 + i for i in range(2 * n)},
        compiler_params=pltpu.CompilerParams(has_side_effects=pltpu.SideEffectType.DATAFLOW_SIDE_EFFECTING),
    )(*[pltpu.with_memory_space_constraint(a, pltpu.HBM) for a in tensors + zones])
    return dict(sems=res[:2], buffers=res[2:2 + 2 * n], gather=gather, started=res[-1])


def _exchange_wait(started, after, *, name):
    n = len(started["buffers"]) // 2

    def body(*refs):
        for cp in _exchange_copies(refs[:n], refs[n:2 * n], refs[2 * n], refs[2 * n + 1], started["gather"]):
            cp.wait_send()
            cp.wait_recv()

    res = pl.pallas_call(
        body, name=name, in_specs=[_HBM] * (2 * n) + [_SEM, _SEM, pl.BlockSpec(memory_space=pl.ANY)],
        out_shape=tuple(pltpu.HBM(a.shape, a.dtype) for a in started["buffers"]), out_specs=tuple([_HBM] * (2 * n)),
        input_output_aliases={i: i for i in range(2 * n)},
        compiler_params=pltpu.CompilerParams(has_side_effects=pltpu.SideEffectType.DATAFLOW_SIDE_EFFECTING),
    )(*started["buffers"], *started["sems"], after)
    return res[:n], res[n:]


def _adamw(contrib, w, m, v, *, name):
    rows, cols = w.shape
    tile = min(rows, ADAM_ROWS)

    def body(c_ref, w_ref, m_ref, v_ref, g_ref, d_ref, nm_ref, nv_ref):
        g = c_ref[0].astype(F32)
        for s in range(1, N_DEV):
            g = g + c_ref[s].astype(F32)
        m_new = ADAM_B1 * m_ref[...] + (1.0 - ADAM_B1) * g
        v_new = ADAM_B2 * v_ref[...] + (1.0 - ADAM_B2) * (g * g)
        m_hat = m_new / (1.0 - ADAM_B1 ** ADAM_STEP)
        v_hat = v_new / (1.0 - ADAM_B2 ** ADAM_STEP)
        g_ref[...] = g
        d_ref[...] = -ADAM_LR * (m_hat / (jnp.sqrt(v_hat) + ADAM_EPS) + ADAM_WD * w_ref[...])
        nm_ref[...] = m_new
        nv_ref[...] = v_new

    spec = pl.BlockSpec((tile, cols), lambda i: (i, 0))
    return pl.pallas_call(
        body, name=name, grid=(rows // tile,),
        in_specs=[pl.BlockSpec((N_DEV, tile, cols), lambda i: (0, i, 0)), spec, spec, spec], out_specs=[spec] * 4,
        out_shape=[jax.ShapeDtypeStruct((rows, cols), F32)] * 4, compiler_params=_cparams("parallel"),
    )(contrib, w, m, v)


class _Weights:
    def __init__(self, gathers, vectors, me):
        self.gathers, self.ready, self.me, self.after = gathers, dict(vectors), me, None

    def arrive_after(self, value):
        self.after = value

    def __getitem__(self, name):
        if name not in self.ready:
            gi = next(i for i, group in enumerate(GATHER_GROUPS) if name in group)
            shards, zones = _exchange_wait(self.gathers[gi], self.after, name=f"gather_wait_{gi}")
            for n, shard, zone in zip(GATHER_GROUPS[gi], shards, zones, strict=True):
                blocks = lax.dynamic_update_slice_in_dim(zone, shard[None], self.me, 0)
                self.ready[n] = _to_kernel_layout(n, _from_blocks(blocks, n))
        return self.ready[name]


def kernel(x, mem, w_in, w_mem_kv, q_a_gain, w_q_b, kv_a_gain, w_kv_b, w_branch_mla, w_branch_sb, w_branch_mem, w_merge_gate, b_merge_gate, w_out, ln_gain, ln_bias, loss_target, m_w_in, m_w_mem_kv, m_q_a_gain, m_w_q_b, m_kv_a_gain, m_w_kv_b, m_w_branch_mla, m_w_branch_sb, m_w_branch_mem, m_w_merge_gate, m_b_merge_gate, m_w_out, m_ln_gain, m_ln_bias, v_w_in, v_w_mem_kv, v_q_a_gain, v_w_q_b, v_kv_a_gain, v_w_kv_b, v_w_branch_mla, v_w_branch_sb, v_w_branch_mem, v_w_merge_gate, v_b_merge_gate, v_w_out, v_ln_gain, v_ln_bias):
    given = dict(locals())
    small_names = [n for n, _ in SMALL]
    smalls = lambda prefix: [given[prefix + n] for n in small_names]
    me = 4 * lax.axis_index("x") + 2 * lax.axis_index("y") + lax.axis_index("c")

    gathers = [_exchange_start([given[n][0].astype(BF16) for n in group], gather=True, name=f"gather_start_{gi}")
               for gi, group in enumerate(GATHER_GROUPS)]
    w = _Weights(gathers, {n: given[n] for n in small_names}, me)
    exchanges = []

    def emit(grads):
        blocks = [_to_blocks(_from_kernel_layout(n, g), n).astype(BF16) for n, g in grads.items()]
        exchanges.append((tuple(grads), _exchange_start(blocks, gather=False, name=f"grads_start_{len(exchanges)}")))
        return exchanges[-1][1]["started"]

    loss, grad_x, grads = _local_step(x[0], mem[0], loss_target[0], w, emit)

    contrib_small = _share_small(_pack_small([grads[n] for n in small_names], loss), name="share_small")
    sml = _adamw(contrib_small, _pack_small(smalls("")), _pack_small(smalls("m_")), _pack_small(smalls("v_")), name="adamw_small")
    results = [dict(zip(small_names, _unpack_small(packed), strict=True)) for packed in sml]
    for gi, (names, started) in enumerate(exchanges):
        sent, zones = _exchange_wait(started, grad_x, name=f"grads_wait_{gi}")
        for n, blocks, zone in zip(names, sent, zones, strict=True):
            own = lax.dynamic_index_in_dim(blocks, me, 0, keepdims=True)
            contrib = lax.dynamic_update_slice_in_dim(zone, own, me, 0)
            for kind, res in zip(results, _adamw(contrib, given[n][0], given["m_" + n][0], given["v_" + n][0], name=f"adamw_{n}"), strict=True):
                kind[n] = res[None]
    order = ["w_in", "w_mem_kv", "q_a_gain", "w_q_b", "kv_a_gain", "w_kv_b", "w_branch_mla", "w_branch_sb", "w_branch_mem",
             "w_merge_gate", "b_merge_gate", "w_out", "ln_gain", "ln_bias"]
    loss_out = sml[0].reshape(-1)[LOSS_INDEX]
    return (loss_out, grad_x[None], *[kind[n] for kind in results for n in order])
```

```python
import math

import jax
import jax.numpy as jnp
from jax import lax
from jax.experimental import pallas as pl
from jax.experimental.pallas import tpu as pltpu

F32, BF16 = jnp.float32, jnp.bfloat16

N_DEV = 8
D_MODEL = 1024
MLA_HEADS, MLA_NOPE, MLA_ROPE, MLA_V = 8, 64, 32, 64
MLA_Q_LORA, MLA_KV_LORA = 256, 128
SB_HEAD_DIM = 64
MEM_HEAD_DIM = 128
ROPE_BASE = 10000.0
RMS_EPS = 1e-6
LN_EPS = 1e-5
DEEPNORM_ALPHA = 2.0 ** 0.25
ADAM_LR, ADAM_B1, ADAM_B2, ADAM_EPS, ADAM_WD, ADAM_STEP = 0.001, 0.9, 0.999, 1e-08, 0.01, 10
LOG2E, LN2 = math.log2(math.e), math.log(2.0)

LANES = 128
GROUPS = 4
PROJ_WIDTH = 4096
COL_CQ, COL_CKV, COL_KROPE, COL_GATE_A, COL_GATE_B, COL_GATE_M = 0, 256, 384, 512, 1024, 1536
QKV_FIRST, QKV_WIDTH = 2048, 2048
COL_QB, COL_KB, COL_VB, COL_QM = 0, 512, 1024, 1536
IN_PIECES = ((0, 416), None, (416, 512), (2464, 512), (3488, 512), (928, 512), (1440, 512), (1952, 512), (2976, 512))
IN_PAD = 96

VMEM_LIMIT_BYTES = 56 * 1024 * 1024
NEG_BIG = -1e30
Q_BLOCK = 512
SB_BWD_Q_BLOCK = 512
TRI_BLOCK = 256
TILE_ROWS = 64
KEY_CHUNK = 512

SHARDED = {
    "w_in": ((1024, 4000), 1), "w_mem_kv": ((1024, 1024), 0), "w_q_b": ((256, 768), 1), "w_kv_b": ((128, 1024), 1),
    "w_branch_mla": ((512, 1024), 1), "w_branch_sb": ((512, 1024), 1), "w_branch_mem": ((512, 1024), 1),
    "w_merge_gate": ((1024, 3072), 1), "w_out": ((1024, 1024), 0),
}
GATHER_GROUPS = (("w_in",), ("w_merge_gate",), ("w_q_b", "w_kv_b", "w_mem_kv", "w_branch_mla", "w_branch_sb", "w_branch_mem", "w_out"))
GRAD_GROUPS = (("w_out", "w_merge_gate", "w_branch_mla", "w_branch_sb", "w_branch_mem"), ("w_mem_kv", "w_q_b", "w_kv_b"), ("w_in",))
SMALL = (("q_a_gain", 256), ("kv_a_gain", 128), ("b_merge_gate", 3072), ("ln_gain", 1024), ("ln_bias", 1024))
SMALL_ROWS, SMALL_LANES = 48, 128
ADAM_ROWS = 256
LOSS_INDEX = 5504


def _cparams(*sem):
    return pltpu.CompilerParams(dimension_semantics=sem or None, vmem_limit_bytes=VMEM_LIMIT_BYTES)


_DIMS = {"nn": (((1,), (0,)), ((), ())), "nt": (((1,), (1,)), ((), ())), "tn": (((0,), (0,)), ((), ()))}


def _dot(a, b, dims):
    return lax.dot_general(a, b, _DIMS[dims], preferred_element_type=F32)


def _tile(dim, want):
    if dim <= want:
        return dim
    t = want - want % LANES
    while dim % t:
        t -= LANES
    assert t > 0, (dim, want)
    return t


_ANY = pl.BlockSpec(memory_space=pl.ANY)


def _mm(a, b, dims, *, name, out_dtype=F32, add=None, add_scale=1.0, col_scale=None, b_cols=None, behind=None,
        tm=1024, tn=1024, tk=1024):
    if dims == "nn":
        (m, k), (k2, n) = a.shape, b.shape
    elif dims == "nt":
        (m, k), (n, k2) = a.shape, b.shape
    else:
        (k, m), (k2, n) = a.shape, b.shape
    assert k == k2, (a.shape, b.shape, dims)
    b_first = 0
    if b_cols is not None:
        assert dims == "nn"
        b_first, n = b_cols
    tm, tn, tk = _tile(m, tm), _tile(n, tn), _tile(k, tk)
    assert b_first % tn == 0
    jb = b_first // tn
    nk = k // tk
    a_spec = pl.BlockSpec((tk, tm), lambda i, j, kk: (kk, i)) if dims == "tn" else pl.BlockSpec((tm, tk), lambda i, j, kk: (i, kk))
    b_spec = pl.BlockSpec((tn, tk), lambda i, j, kk: (j, kk)) if dims == "nt" else pl.BlockSpec((tk, tn), lambda i, j, kk: (kk, jb + j))
    o_spec = pl.BlockSpec((tm, tn), lambda i, j, kk: (i, j))
    behind = [] if behind is None else behind if isinstance(behind, (list, tuple)) else [behind]
    optional = [(add, o_spec), (col_scale, pl.BlockSpec((1, tn), lambda i, j, kk: (0, j))), *[(v, _ANY) for v in behind]]
    present = [(v, spec) for v, spec in optional if v is not None]

    def body(*refs):
        a_ref, b_ref = refs[:2]
        extra = iter(refs[2:2 + len(present)])
        add_ref = next(extra) if add is not None else None
        scale_ref = next(extra) if col_scale is not None else None
        o_ref = refs[2 + len(present)]
        part = _dot(a_ref[...].astype(BF16), b_ref[...].astype(BF16), dims)

        def finish(r):
            if add is not None:
                r = r + add_scale * add_ref[...]
            if col_scale is not None:
                r = r * scale_ref[...]
            o_ref[...] = r.astype(out_dtype)

        if nk == 1:
            finish(part)
            return
        acc = refs[-1]
        kk = pl.program_id(2)

        @pl.when(kk == 0)
        def _():
            acc[...] = part

        @pl.when(kk > 0)
        def _():
            acc[...] += part

        @pl.when(kk == nk - 1)
        def _():
            finish(acc[...])

    return pl.pallas_call(
        body, name=name, grid=(m // tm, n // tn, nk),
        in_specs=[a_spec, b_spec] + [spec for _, spec in present], out_specs=o_spec,
        out_shape=jax.ShapeDtypeStruct((m, n), out_dtype), scratch_shapes=[pltpu.VMEM((tm, tn), F32)] if nk > 1 else [],
        compiler_params=_cparams("parallel", "parallel", "arbitrary"),
    )(a, b, *[v for v, _ in present])


def _rowwise(fn, ins, outs, *, name, rows, tr=256):
    n_in = len(ins)
    in_specs, args = [], []
    for it in ins:
        arr, w, off = it if isinstance(it, tuple) else (it, it.shape[-1], 0)
        assert off % w == 0
        cb = off // w
        if arr.shape[0] == 1:
            in_specs.append(pl.BlockSpec((1, w), lambda i, cb=cb: (0, cb)))
        else:
            in_specs.append(pl.BlockSpec((tr, w), lambda i, cb=cb: (i, cb)))
        args.append(arr)
    out_shape, out_specs, is_sum = [], [], []
    for kind, d in outs:
        if kind == "sum":
            out_shape.append(jax.ShapeDtypeStruct((1, d), F32))
            out_specs.append(pl.BlockSpec((1, d), lambda i: (0, 0)))
            is_sum.append(True)
        else:
            out_shape.append(jax.ShapeDtypeStruct((rows, kind), d))
            out_specs.append(pl.BlockSpec((tr, kind), lambda i: (i, 0)))
            is_sum.append(False)

    def body(*refs):
        res = fn(*[r[...] for r in refs[:n_in]])
        for r, val, s in zip(refs[n_in:], res, is_sum, strict=True):
            if s:
                @pl.when(pl.program_id(0) == 0)
                def _(r=r):
                    r[...] = jnp.zeros_like(r)

                r[...] += val
            else:
                r[...] = val.astype(r.dtype)

    return pl.pallas_call(
        body, name=name, grid=(rows // tr,), in_specs=in_specs, out_specs=out_specs, out_shape=out_shape,
        compiler_params=_cparams("arbitrary"),
    )(*args)


def _colsum(v):
    return jnp.sum(v, axis=0, keepdims=True)


def _sigmoid(v):
    return 1.0 / (1.0 + jnp.exp(-v))


def _lane_groups(v):
    return [v[:, g * LANES:(g + 1) * LANES] for g in range(v.shape[1] // LANES)]


def _swap_halves(v, first_lane):
    lane = lax.broadcasted_iota(jnp.int32, v.shape, 1)
    return jnp.where(lane < first_lane + 16, pltpu.roll(v, 112, axis=1), pltpu.roll(v, 16, axis=1))


def _lane_sum(acc, v):
    for part in _lane_groups(v):
        acc = acc + part
    return acc


def _low_half(shape):
    return lax.broadcasted_iota(jnp.int32, shape, 1) < LANES // 2


def _select_heads(per_head, pick):
    if len(per_head) == 1:
        return pick(per_head[0], 0)
    return jnp.where(_low_half(per_head[0].shape), pick(per_head[0], 0), pick(per_head[1], 1))


def _attn_specs(s, sk, hp, bq, q0, k0, v0):
    wq = hp * LANES
    assert q0 % wq == 0 and k0 % wq == 0 and v0 % LANES == 0
    qb0, kb0, vb0 = q0 // wq, k0 // wq, v0 // LANES
    q_spec = pl.BlockSpec((bq, wq), lambda g, i: (i, qb0 + g))
    k_spec = pl.BlockSpec((sk, wq), lambda g, i: (0, kb0 + g))
    v_spec = pl.BlockSpec((sk, LANES), lambda g, i: (0, vb0 + g))
    row_out = lambda w: pl.BlockSpec((bq, w), lambda g, i: (i, g))
    key_out = lambda w: pl.BlockSpec((sk, w), lambda g, i: (0, g))
    return q_spec, k_spec, v_spec, row_out, key_out


def _chunks(i, bq, ch, sk, causal):
    return ((i + 1) * bq - 1) // ch if causal else jnp.int32(sk // ch - 1)


def _positions(i, c, bq, ch):
    return (i * bq + lax.broadcasted_iota(jnp.int32, (bq, ch), 0), c * ch + lax.broadcasted_iota(jnp.int32, (bq, ch), 1))


def _softmax_fwd(q, k, v, *, hp, causal, name, q0=0, k0=0, v0=0):
    s, sk = q.shape[0], k.shape[0]
    bq, ch = min(Q_BLOCK, s), min(KEY_CHUNK, sk)
    q_spec, k_spec, v_spec, row_out, _ = _attn_specs(s, sk, hp, bq, q0, k0, v0)

    def body(q_ref, k_ref, v_ref, o_ref, lse_ref, s_scr):
        i = pl.program_id(1)
        qs = _lane_groups(q_ref[...])
        last = _chunks(i, bq, ch, sk, causal)

        def scores(c, ms, masked):
            off = pl.multiple_of(c * ch, ch)
            out = []
            for j in range(hp):
                sc = _dot(qs[j], k_ref[pl.ds(off, ch), j * LANES:(j + 1) * LANES], "nt")
                if masked:
                    qpos, kpos = _positions(i, c, bq, ch)
                    sc = jnp.where(kpos <= qpos, sc, NEG_BIG)
                s_scr[j, c] = sc
                m = ms[j]
                for part in _lane_groups(sc):
                    m = jnp.maximum(m, part)
                out.append(m)
            return tuple(out)

        ms = lax.fori_loop(0, last, lambda c, m: scores(c, m, False), tuple(jnp.full((bq, LANES), NEG_BIG, F32) for _ in range(hp)))
        ms = scores(last, ms, causal)
        row_max = [jnp.max(m, axis=1, keepdims=True) for m in ms]

        def weigh(c, carry):
            off = pl.multiple_of(c * ch, ch)
            vt = v_ref[pl.ds(off, ch), :]
            out = []
            for j in range(hp):
                l, acc = carry[j]
                p = jnp.exp2(s_scr[j, c] - row_max[j])
                out.append((_lane_sum(l, p), acc + _dot(p.astype(BF16), vt, "nn")))
            return tuple(out)

        zero = jnp.zeros((bq, LANES), F32)
        res = lax.fori_loop(0, last + 1, weigh, tuple((zero, zero) for _ in range(hp)))
        row_sum = [jnp.sum(l, axis=1, keepdims=True) for l, _ in res]
        o_ref[...] = _select_heads([acc for _, acc in res], lambda acc, j: acc / row_sum[j])
        lse_ref[...] = _select_heads([jnp.broadcast_to(row_max[j] + jnp.log2(row_sum[j]), (bq, LANES)) for j in range(hp)], lambda a, j: a)

    return pl.pallas_call(
        body, name=name, grid=(GROUPS, s // bq), in_specs=[q_spec, k_spec, v_spec], out_specs=[row_out(LANES), row_out(LANES)],
        out_shape=[jax.ShapeDtypeStruct((s, GROUPS * LANES), F32)] * 2,
        scratch_shapes=[pltpu.VMEM((hp, sk // ch, bq, ch), F32)], compiler_params=_cparams("parallel", "arbitrary"),
    )(q, k, v)


def _head_cotangent(do, j, hp):
    if hp == 1:
        return do
    return jnp.where(_low_half(do.shape) == (j == 0), do, 0.0)


def _softmax_bwd(q, k, v, o, do, lse, behind, *, hp, causal, dq_scale, name, q0=0, k0=0, v0=0):
    s, sk = q.shape[0], k.shape[0]
    bq, ch = min(Q_BLOCK, s), min(KEY_CHUNK, sk)
    wq = hp * LANES
    q_spec, k_spec, v_spec, row_out, key_out = _attn_specs(s, sk, hp, bq, q0, k0, v0)

    def body(q_ref, k_ref, v_ref, o_ref, do_ref, lse_ref, _, dq_ref, dk_ref, dv_ref):
        i = pl.program_id(1)

        @pl.when(i == 0)
        def _():
            dk_ref[...] = jnp.zeros_like(dk_ref)
            dv_ref[...] = jnp.zeros_like(dv_ref)

        qs = _lane_groups(q_ref[...])
        do_all, o_all, lse_all = do_ref[...], o_ref[...], lse_ref[...]
        dos, deltas, lses = [], [], []
        for j in range(hp):
            d = _head_cotangent(do_all, j, hp)
            deltas.append(jnp.sum(d * o_all, axis=1, keepdims=True))
            dos.append(d.astype(BF16))
            lses.append(lse_all[:, j * (LANES // hp):j * (LANES // hp) + 1])
        last = _chunks(i, bq, ch, sk, causal)

        def chunk(c, dqs, masked):
            off = pl.multiple_of(c * ch, ch)
            vt = v_ref[pl.ds(off, ch), :]
            out, dks, dv = [], [], None
            for j in range(hp):
                kt = k_ref[pl.ds(off, ch), j * LANES:(j + 1) * LANES]
                p = jnp.exp2(_dot(qs[j], kt, "nt") - lses[j])
                if masked:
                    qpos, kpos = _positions(i, c, bq, ch)
                    p = jnp.where(kpos <= qpos, p, 0.0)
                ds = (p * (_dot(dos[j], vt, "nt") - deltas[j]) * LN2).astype(BF16)
                out.append(dqs[j] + _dot(ds, kt, "nn"))
                dks.append(_dot(ds, qs[j], "tn"))
                dvj = _dot(p.astype(BF16), dos[j], "tn")
                dv = dvj if dv is None else dv + dvj
            dk_ref[pl.ds(off, ch), :] += dks[0] if hp == 1 else jnp.concatenate(dks, axis=1)
            dv_ref[pl.ds(off, ch), :] += dv
            return tuple(out)

        dqs = lax.fori_loop(0, last, lambda c, d: chunk(c, d, False), tuple(jnp.zeros((bq, LANES), F32) for _ in range(hp)))
        dqs = chunk(last, dqs, causal)
        dq_ref[...] = (dqs[0] if hp == 1 else jnp.concatenate(dqs, axis=1)) * dq_scale

    return pl.pallas_call(
        body, name=name, grid=(GROUPS, s // bq),
        in_specs=[q_spec, k_spec, v_spec, row_out(LANES), row_out(LANES), row_out(LANES), _ANY],
        out_specs=[row_out(wq), key_out(wq), key_out(LANES)],
        out_shape=[jax.ShapeDtypeStruct((s, GROUPS * wq), F32), jax.ShapeDtypeStruct((sk, GROUPS * wq), F32),
                   jax.ShapeDtypeStruct((sk, GROUPS * LANES), F32)],
        compiler_params=_cparams("arbitrary", "arbitrary"),
    )(q, k, v, o, do, lse, behind)


def _log2_sigmoid_pair(z2):
    minus_abs = lax.bitcast_convert_type(lax.bitcast_convert_type(z2, jnp.uint32) | jnp.uint32(0x80000000), F32)
    log_beta = jnp.minimum(z2, 0.0) - jnp.log2(1.0 + jnp.exp2(minus_abs))
    return log_beta, log_beta - z2


def _tilewise(fn, *arrays):
    rows, cols = arrays[0].shape
    step = min(TILE_ROWS, rows)
    grid = [[fn(*[None if a is None else a[r:r + step, c:c + LANES] for a in arrays]) for c in range(0, cols, LANES)]
            for r in range(0, rows, step)]
    return [jnp.concatenate([jnp.concatenate([cell[k] for cell in row], axis=1) for row in grid], axis=0)
            for k in range(len(grid[0][0]))]


def _split(v):
    hi = v.astype(BF16)
    return hi, (v - hi.astype(F32)).astype(BF16)


def _tri(n, after):
    rows, cols = lax.broadcasted_iota(jnp.int32, (n, n), 0), lax.broadcasted_iota(jnp.int32, (n, n), 1)
    return (rows > cols if after else rows < cols).astype(BF16)


def _running_sums(v, terms, start, tri, backwards):
    n = tri.shape[0]
    n_blocks = v.shape[1] // n
    order = range(n_blocks - 1, -1, -1) if backwards else range(n_blocks)
    stacked = tri if len(terms) == 1 else jnp.concatenate([tri] * len(terms), axis=0)
    parts, run = [None] * n_blocks, start
    for t in order:
        cols = slice(t * n, (t + 1) * n)
        lhs = terms[0][:, cols] if len(terms) == 1 else jnp.concatenate([term[:, cols] for term in terms], axis=1)
        parts[t] = _dot(lhs, stacked, "nn") + run
        run = run + jnp.sum(v[:, cols], axis=1, keepdims=True)
    return (parts[0] if n_blocks == 1 else jnp.concatenate(parts, axis=1)), run


def _sb_weights(qm, kt, run, tri, strict):
    def logs(z2, keep):
        log_beta, log_keep = _log2_sigmoid_pair(z2)
        if keep is not None:
            log_keep = jnp.where(keep, log_keep, 0.0)
        return log_beta, log_keep, *_split(log_keep)

    log_beta, log_keep, hi, lo = _tilewise(logs, _dot(qm, kt, "nt"), strict)
    behind, run = _running_sums(log_keep, (hi, lo), run, tri, True)

    def weigh(log_beta, behind, keep):
        a = jnp.exp2(log_beta + behind)
        return (a if keep is None else jnp.where(keep, a, 0.0),)

    (a,) = _tilewise(weigh, log_beta, behind, strict)
    return a, log_beta, run


def _sb_queries(q_all):
    low = _low_half(q_all.shape)
    zero = jnp.zeros_like(q_all)
    return [jnp.where(low, q_all, zero), jnp.where(low, zero, q_all)]


def _sb_fwd(qkv, *, q0, k0, v0, name):
    s = qkv.shape[0]
    bq, ch = min(Q_BLOCK, s), min(KEY_CHUNK, s)
    q_spec, k_spec, v_spec, row_out, _ = _attn_specs(s, s, 1, bq, q0, k0, v0)

    def body(q_ref, k_ref, v_ref, o_ref):
        i = pl.program_id(1)
        qms = _sb_queries(q_ref[...])
        tri = _tri(min(TRI_BLOCK, ch), True)
        last = _chunks(i, bq, ch, s, True)

        def chunk(c, carry, masked):
            off = pl.multiple_of(c * ch, ch)
            kt, vt = k_ref[pl.ds(off, ch), :], v_ref[pl.ds(off, ch), :]
            strict = None
            if masked:
                qpos, kpos = _positions(i, c, bq, ch)
                strict = kpos < qpos
            out = []
            for j in range(2):
                run, acc = carry[j]
                a, _, run = _sb_weights(qms[j], kt, run, tri, strict)
                out.append((run, acc + _dot(a.astype(BF16), vt, "nn")))
            return tuple(out)

        carry = chunk(last, tuple((jnp.zeros((bq, 1), F32), jnp.zeros((bq, LANES), F32)) for _ in range(2)), True)
        res = lax.fori_loop(0, last, lambda n, c: chunk(last - 1 - n, c, False), carry)
        o_ref[...] = _select_heads([acc for _, acc in res], lambda acc, j: acc)

    return pl.pallas_call(
        body, name=name, grid=(GROUPS, s // bq), in_specs=[q_spec, k_spec, v_spec], out_specs=row_out(LANES),
        out_shape=jax.ShapeDtypeStruct((s, GROUPS * LANES), F32), compiler_params=_cparams("parallel", "arbitrary"),
    )(qkv, qkv, qkv)


def _sb_bwd(qkv, do, behind, *, q0, k0, v0, dq_scale, name):
    s = qkv.shape[0]
    bq, ch = min(SB_BWD_Q_BLOCK, s), min(KEY_CHUNK, s)
    q_spec, k_spec, v_spec, row_out, key_out = _attn_specs(s, s, 1, bq, q0, k0, v0)

    def body(q_ref, k_ref, v_ref, do_ref, _, dq_ref, dk_ref, dv_ref, g_s, beta_s):
        i = pl.program_id(1)

        @pl.when(i == 0)
        def _():
            dk_ref[...] = jnp.zeros_like(dk_ref)
            dv_ref[...] = jnp.zeros_like(dv_ref)

        qms = _sb_queries(q_ref[...])
        do_all = do_ref[...]
        dos = [_head_cotangent(do_all, j, 2).astype(BF16) for j in range(2)]
        dos_ln2 = [(_head_cotangent(do_all, j, 2) * LN2).astype(BF16) for j in range(2)]
        tri_after, tri_before = _tri(min(TRI_BLOCK, ch), True), _tri(min(TRI_BLOCK, ch), False)
        last = _chunks(i, bq, ch, s, True)

        def strict_mask(c):
            qpos, kpos = _positions(i, c, bq, ch)
            return kpos < qpos

        def sweep1(c, runs, masked):
            off = pl.multiple_of(c * ch, ch)
            kt, vt = k_ref[pl.ds(off, ch), :], v_ref[pl.ds(off, ch), :]
            strict = strict_mask(c) if masked else None
            out, dv = [], None
            for j in range(2):
                a, log_beta, run = _sb_weights(qms[j], kt, runs[j], tri_after, strict)
                g_s[j, c] = (a * _dot(dos_ln2[j], vt, "nt")).astype(BF16)
                beta_s[j, c] = jnp.exp2(log_beta).astype(BF16)
                dvj = _dot(a.astype(BF16), dos[j], "tn")
                dv = dvj if dv is None else dv + dvj
                out.append(run)
            dv_ref[pl.ds(off, ch), :] += dv
            return tuple(out)

        runs = sweep1(last, tuple(jnp.zeros((bq, 1), F32) for _ in range(2)), True)
        lax.fori_loop(0, last, lambda n, r: sweep1(last - 1 - n, r, False), runs)

        def sweep2(c, carry, masked):
            off = pl.multiple_of(c * ch, ch)
            kt = k_ref[pl.ds(off, ch), :]
            out, dk = [], None
            for j in range(2):
                before, dq = carry[j]
                g16, beta = g_s[j, c], beta_s[j, c].astype(F32)
                g = g16.astype(F32)
                in_front, before = _running_sums(g, (g16,), before, tri_before, False)
                dz = g * (1.0 - beta) - beta * in_front
                if masked:
                    dz = jnp.where(strict_mask(c), dz, 0.0)
                dz = dz.astype(BF16)
                dkj = _dot(dz, qms[j], "tn")
                dk = dkj if dk is None else dk + dkj
                out.append((before, dq + _dot(dz, kt, "nn")))
            dk_ref[pl.ds(off, ch), :] += dk
            return tuple(out)

        carry = lax.fori_loop(0, last, lambda c, cr: sweep2(c, cr, False),
                              tuple((jnp.zeros((bq, 1), F32), jnp.zeros((bq, LANES), F32)) for _ in range(2)))
        res = sweep2(last, carry, True)
        dq_ref[...] = _select_heads([dq for _, dq in res], lambda dq, j: dq) * dq_scale

    n_ch = s // ch
    return pl.pallas_call(
        body, name=name, grid=(GROUPS, s // bq), in_specs=[q_spec, k_spec, v_spec, row_out(LANES), _ANY],
        out_specs=[row_out(LANES), key_out(LANES), key_out(LANES)],
        out_shape=[jax.ShapeDtypeStruct((s, GROUPS * LANES), F32)] * 3,
        scratch_shapes=[pltpu.VMEM((2, n_ch, bq, ch), BF16)] * 2,
        compiler_params=_cparams("arbitrary", "arbitrary"),
    )(qkv, qkv, qkv, do, behind)


def _rope_tables(s):
    half = MLA_ROPE // 2
    freqs = ROPE_BASE ** (-jnp.arange(half, dtype=F32) / half)
    ang = jnp.arange(s, dtype=F32)[:, None] * freqs[None, :]
    cos, sin = jnp.cos(ang), jnp.sin(ang)
    tail = jnp.zeros((s, LANES - MLA_NOPE - MLA_ROPE), F32)
    lead = lambda fill: jnp.full((s, MLA_NOPE), fill, F32)
    return dict(
        cos_k0=jnp.concatenate([cos, cos, lead(0.0), tail], axis=1), sin_k0=jnp.concatenate([-sin, sin, lead(0.0), tail], axis=1),
        cos_k64=jnp.concatenate([lead(0.0), cos, cos, tail], axis=1), sin_k64=jnp.concatenate([lead(0.0), -sin, sin, tail], axis=1),
        cos_q=jnp.concatenate([lead(1.0), cos, cos, tail], axis=1),
        sin_k64_t=jnp.concatenate([lead(0.0), sin, -sin, tail], axis=1),
    )


def _local_step(x, mem, target, w, emit=lambda grads: [jnp.zeros((8, LANES), F32)]):
    s = x.shape[0]
    rope = _rope_tables(s)
    xb = x.astype(BF16)
    inv_d = 1.0 / D_MODEL
    scale_a = LOG2E / math.sqrt(MLA_NOPE + MLA_ROPE)
    scale_b = LOG2E / math.sqrt(SB_HEAD_DIM)
    scale_m = LOG2E / math.sqrt(MEM_HEAD_DIM)
    arrive_after = getattr(w, "arrive_after", lambda *values: None)
    memb = mem.astype(BF16)

    arrive_after(xb, memb, *rope.values())
    proj = _mm(xb, w["w_in"], "nn", name="proj", b_cols=(0, QKV_FIRST))
    one = jnp.ones((1, 512), F32)
    qkv = _mm(xb, w["w_in"], "nn", name="proj_qkv", b_cols=(QKV_FIRST, QKV_WIDTH), out_dtype=BF16,
              col_scale=jnp.concatenate([one * scale_b, one, one, one * scale_m], axis=1))
    arrive_after(qkv)
    pre = _mm(xb, w["w_merge_gate"], "nn", name="merge_pre", out_dtype=BF16)
    arrive_after(pre)

    def rms_pair(c_q, c_kv, g_q, g_kv):
        return (c_q * lax.rsqrt(jnp.mean(c_q * c_q, axis=1, keepdims=True) + RMS_EPS) * g_q,
                c_kv * lax.rsqrt(jnp.mean(c_kv * c_kv, axis=1, keepdims=True) + RMS_EPS) * g_kv)

    n_q, n_kv = _rowwise(rms_pair, [(proj, 256, COL_CQ), (proj, 128, COL_CKV), w["q_a_gain"], w["kv_a_gain"]],
                         [(256, BF16), (128, BF16)], name="rms_pair", rows=s)
    q_a = _mm(n_q, w["w_q_b"], "nn", name="q_up")
    kv_a = _mm(n_kv, w["w_kv_b"], "nn", name="kv_up", out_dtype=BF16)

    def rope_q(qa, cos, sin):
        return (jnp.concatenate([(g * cos + _swap_halves(g, MLA_NOPE) * sin) * scale_a for g in _lane_groups(qa)], axis=1),)

    (q_mla,) = _rowwise(rope_q, [q_a, rope["cos_q"], rope["sin_k64"]], [(1024, BF16)], name="rope_q", rows=s)

    def rope_k(k_nope, k_rope, cos, sin):
        k_pe = pltpu.roll(k_rope * cos + _swap_halves(k_rope, 0) * sin, MLA_NOPE, axis=1).astype(BF16)
        return (jnp.concatenate([g + k_pe for g in _lane_groups(k_nope)], axis=1),)

    (k_mla,) = _rowwise(rope_k, [(kv_a, 1024, 0), (proj, 128, COL_KROPE), rope["cos_k0"], rope["sin_k0"]],
                        [(1024, BF16)], name="rope_k", rows=s)
    o_a, lse_a = _softmax_fwd(q_mla, k_mla, kv_a, hp=2, causal=True, name="mla_fwd", v0=1024)

    o_b = _sb_fwd(qkv, q0=COL_QB, k0=COL_KB, v0=COL_VB, name="sb_fwd")

    mem_kv =_mm(memb, w["w_mem_kv"], "nn", name="mem_kv", out_dtype=BF16)
    o_m, lse_m = _softmax_fwd(qkv, mem_kv, mem_kv, hp=1, causal=False, name="mem_fwd", q0=1536, v0=512)

    o_br = {"mla": o_a, "sb": o_b, "mem": o_m}
    gate_col = {"mla": COL_GATE_A, "sb": COL_GATE_B, "mem": COL_GATE_M}

    def gated(o, gate):
        return (o * gate * _sigmoid(gate),)

    u, y = {}, {}
    for br in ("mla", "sb", "mem"):
        (u[br],) = _rowwise(gated, [o_br[br], (proj, 512, gate_col[br])], [(512, BF16)], name=f"gated_{br}", rows=s)
        y[br] = _mm(u[br], w[f"w_branch_{br}"], "nn", name=f"branch_{br}", out_dtype=BF16)

    def merge(pa, pb, pm, ba, bb, bm, ya, yb, ym):
        return (sum(_sigmoid(p.astype(F32) + b) * yv.astype(F32) for p, b, yv in ((pa, ba, ya), (pb, bb, yb), (pm, bm, ym))),)

    bias = w["b_merge_gate"]
    gate_ins = [(pre, 1024, 0), (pre, 1024, 1024), (pre, 1024, 2048), (bias, 1024, 0), (bias, 1024, 1024), (bias, 1024, 2048)]
    (merged,) = _rowwise(merge, gate_ins + [y["mla"], y["sb"], y["mem"]], [(1024, BF16)], name="merge", rows=s)
    out = _mm(merged, w["w_out"], "nn", name="out_proj")

    def norm_loss(xv, ov, tv, gain, bias_ln):
        z = DEEPNORM_ALPHA * xv + ov
        zc = z - jnp.mean(z, axis=1, keepdims=True)
        rstd = lax.rsqrt(jnp.mean(zc * zc, axis=1, keepdims=True) + LN_EPS)
        xhat = zc * rstd
        err = xhat * gain + bias_ln - tv
        loss = 0.5 * jnp.sum(jnp.mean(err * err, axis=1, keepdims=True), axis=0, keepdims=True)
        dy = err * inv_d
        dxhat = dy * gain
        dz = rstd * (dxhat - jnp.mean(dxhat, axis=1, keepdims=True) - xhat * jnp.mean(dxhat * xhat, axis=1, keepdims=True))
        return dz, dz, _colsum(dy * xhat), _colsum(dy), jnp.broadcast_to(loss, (1, LANES))

    dz, dzb, g_ln_gain, g_ln_bias, loss = _rowwise(
        norm_loss, [x, out, target, w["ln_gain"], w["ln_bias"]],
        [(1024, F32), (1024, BF16), ("sum", 1024), ("sum", 1024), ("sum", LANES)], name="norm_loss", rows=s)

    grads = {"ln_gain": g_ln_gain, "ln_bias": g_ln_bias}
    dmerged = _mm(dzb, w["w_out"], "nt", name="d_merged")
    grads["w_out"] = _mm(merged, dzb, "tn", name="g_w_out")

    def merge_bwd(dm, pa, pb, pm, ba, bb, bm, ya, yb, ym):
        res, dpre = [], []
        for p, b, yv in ((pa, ba, ya), (pb, bb, yb), (pm, bm, ym)):
            g = _sigmoid(p.astype(F32) + b)
            dpre.append(dm * yv.astype(F32) * g * (1.0 - g))
            res.append(dm * g)
        dpre = jnp.concatenate(dpre, axis=1)
        return dpre, _colsum(dpre), *res

    dpre, grads["b_merge_gate"], dy_a, dy_b, dy_m = _rowwise(
        merge_bwd, [dmerged] + gate_ins + [y["mla"], y["sb"], y["mem"]],
        [(3072, BF16), ("sum", 3072), (1024, BF16), (1024, BF16), (1024, BF16)], name="merge_bwd", rows=s, tr=128)
    grads["w_merge_gate"] = _mm(xb, dpre, "tn", name="g_w_merge")
    dx = _mm(dpre, w["w_merge_gate"], "nt", name="dx_merge", add=dz, add_scale=DEEPNORM_ALPHA)

    def gated_bwd(du, o, gate):
        sg = _sigmoid(gate)
        return du * gate * sg, du * o * sg * (1.0 + gate * (1.0 - sg))

    d_o, d_gate = {}, {}
    for br, dy in (("mla", dy_a), ("sb", dy_b), ("mem", dy_m)):
        grads[f"w_branch_{br}"] = _mm(u[br], dy, "tn", name=f"g_w_branch_{br}")
        du = _mm(dy, w[f"w_branch_{br}"], "nt", name=f"d_u_{br}")
        d_o[br], d_gate[br] = _rowwise(gated_bwd, [du, o_br[br], (proj, 512, gate_col[br])], [(512, F32), (512, BF16)],
                                       name=f"gated_bwd_{br}", rows=s)
    (sent,) = emit({n: grads[n] for n in ("w_out", "w_merge_gate", "w_branch_mla", "w_branch_sb", "w_branch_mem")})

    dq_m, dk_m, dv_m = _softmax_bwd(qkv, mem_kv, mem_kv, o_m, d_o["mem"], lse_m, sent, hp=1, causal=False, dq_scale=scale_m,
                                    name="mem_bwd", q0=1536, v0=512)
    grads["w_mem_kv"] = _mm(memb, jnp.concatenate([dk_m, dv_m], axis=1), "tn", name="g_w_mem_kv")

    dq_sb, dk_sb, dv_sb = _sb_bwd(qkv, d_o["sb"], sent, q0=0, k0=512, v0=1024, dq_scale=scale_b, name="sb_bwd")

    dq_mla, dk_mla, dv_a = _softmax_bwd(q_mla, k_mla, kv_a, o_a, d_o["mla"], lse_a, sent, hp=2, causal=True, dq_scale=scale_a,
                                        name="mla_bwd", v0=1024)

    def rope_q_bwd(dq, cos, sin):
        return (jnp.concatenate([g * cos + _swap_halves(g, MLA_NOPE) * sin for g in _lane_groups(dq)], axis=1),)

    (dq_a,) = _rowwise(rope_q_bwd, [dq_mla, rope["cos_q"], rope["sin_k64_t"]], [(1024, BF16)], name="rope_q_bwd", rows=s)
    grads["w_q_b"] = _mm(n_q, dq_a, "tn", name="g_w_q_b")
    dn_q = _mm(dq_a, w["w_q_b"], "nt", name="d_n_q")

    def rope_k_bwd(dk, cos, sin):
        groups = _lane_groups(dk)
        g = groups[0]
        for other in groups[1:]:
            g = g + other
        d_rope = pltpu.roll(g * cos + _swap_halves(g, MLA_NOPE) * sin, MLA_NOPE, axis=1)
        nope = _low_half(g.shape)
        return d_rope, jnp.concatenate([jnp.where(nope, grp, 0.0) for grp in groups], axis=1)

    dk_rope, dk_nope = _rowwise(rope_k_bwd, [dk_mla, rope["cos_k64"], rope["sin_k64_t"]], [(128, BF16), (1024, BF16)],
                                name="rope_k_bwd", rows=s)
    grads["w_kv_b"] = jnp.concatenate([_mm(n_kv, dk_nope, "tn", name="g_w_kv_b_k"), _mm(n_kv, dv_a, "tn", name="g_w_kv_b_v")], axis=1)
    dn_kv = _mm(dk_nope, w["w_kv_b"][:, :1024], "nt", name="d_n_kv_k")
    dn_kv = _mm(dv_a, w["w_kv_b"][:, 1024:], "nt", name="d_n_kv_v", add=dn_kv)

    def rms_bwd(c_q, c_kv, dq, dkv, g_q, g_kv):
        res = []
        for c, dn, g in ((c_q, dq, g_q), (c_kv, dkv, g_kv)):
            r = lax.rsqrt(jnp.mean(c * c, axis=1, keepdims=True) + RMS_EPS)
            t = dn * g
            res += [r * t - c * (r * r * r) * jnp.mean(c * t, axis=1, keepdims=True), _colsum(dn * c * r)]
        return res

    dc_q, grads["q_a_gain"], dc_kv, grads["kv_a_gain"] = _rowwise(
        rms_bwd, [(proj, 256, COL_CQ), (proj, 128, COL_CKV), dn_q, dn_kv, w["q_a_gain"], w["kv_a_gain"]],
        [(256, BF16), ("sum", 256), (128, BF16), ("sum", 128)], name="rms_bwd", rows=s)

    sent = emit({n: grads[n] for n in ("w_mem_kv", "w_q_b", "w_kv_b")})

    dproj = jnp.concatenate(
        [dc_q, dc_kv, dk_rope, d_gate["mla"], d_gate["sb"], d_gate["mem"], dq_sb.astype(BF16), dk_sb.astype(BF16),
         dv_sb.astype(BF16), dq_m.astype(BF16)], axis=1)
    grads["w_in"] = _mm(xb, dproj, "tn", name="g_w_in", behind=sent)
    sent = emit({"w_in": grads["w_in"]})
    grad_x = _mm(dproj, w["w_in"], "nt", name="grad_x", add=dx, behind=sent)
    return loss, grad_x, grads


def _shard_shape(shape, axis):
    return tuple(d // N_DEV if a == axis else d for a, d in enumerate(shape))


def _from_blocks(blocks, name):
    shape, axis = SHARDED[name]
    return blocks.reshape(shape) if axis == 0 else blocks.transpose(1, 0, 2).reshape(shape)


def _to_blocks(full, name):
    shape, axis = SHARDED[name]
    shp = _shard_shape(shape, axis)
    return full.reshape(N_DEV, *shp) if axis == 0 else full.reshape(shape[0], N_DEV, shp[1]).transpose(1, 0, 2)


def _pad_heads(a, used):
    rows = a.shape[0]
    a = a.reshape(rows, MLA_HEADS, used)
    return jnp.concatenate([a, jnp.zeros((rows, MLA_HEADS, LANES - used), a.dtype)], axis=2).reshape(rows, MLA_HEADS * LANES)


def _to_kernel_layout(name, full):
    if name == "w_in":
        return jnp.concatenate([jnp.zeros((D_MODEL, IN_PAD), full.dtype) if piece is None else full[:, piece[0]:piece[0] + piece[1]]
                                for piece in IN_PIECES], axis=1)
    if name == "w_q_b":
        return _pad_heads(full, MLA_NOPE + MLA_ROPE)
    if name == "w_kv_b":
        kv = full.reshape(MLA_KV_LORA, MLA_HEADS, MLA_NOPE + MLA_V)
        return jnp.concatenate([_pad_heads(kv[:, :, :MLA_NOPE].reshape(MLA_KV_LORA, -1), MLA_NOPE),
                                kv[:, :, MLA_NOPE:].reshape(MLA_KV_LORA, -1)], axis=1)
    return full


def _from_kernel_layout(name, g):
    if name == "w_in":
        placed, at = [], 0
        for piece in IN_PIECES:
            if piece is not None:
                placed.append((piece[0], g[:, at:at + piece[1]]))
            at += IN_PAD if piece is None else piece[1]
        return jnp.concatenate([cols for _, cols in sorted(placed, key=lambda item: item[0])], axis=1)
    if name == "w_q_b":
        return g.reshape(MLA_Q_LORA, MLA_HEADS, LANES)[:, :, :MLA_NOPE + MLA_ROPE].reshape(MLA_Q_LORA, -1)
    if name == "w_kv_b":
        return jnp.concatenate([g[:, :1024].reshape(MLA_KV_LORA, MLA_HEADS, LANES)[:, :, :MLA_NOPE],
                                g[:, 1024:].reshape(MLA_KV_LORA, MLA_HEADS, MLA_V)], axis=2).reshape(MLA_KV_LORA, -1)
    return g


def _pack_small(vectors, loss=None):
    flat = [v.reshape(-1) for v in vectors]
    flat.append(jnp.zeros((SMALL_ROWS * SMALL_LANES - LOSS_INDEX,), F32) if loss is None else
                jnp.concatenate([loss.reshape(-1)[:1], jnp.zeros((SMALL_ROWS * SMALL_LANES - LOSS_INDEX - 1,), F32)]))
    return jnp.concatenate(flat).reshape(SMALL_ROWS, SMALL_LANES)


def _unpack_small(packed):
    flat, res, off = packed.reshape(-1), [], 0
    for _, n in SMALL:
        res.append(flat[off:off + n].reshape(1, n))
        off += n
    return res


def _me_and_peers():
    x, y, c = lax.axis_index("x"), lax.axis_index("y"), lax.axis_index("c")
    peers = []
    for kk in range(1, N_DEV):
        px, py, pc = (x + (kk >> 2)) % 2, (y + ((kk >> 1) & 1)) % 2, (c + (kk & 1)) % 2
        peers.append(((px, py, pc), 4 * px + 2 * py + pc))
    return 4 * x + 2 * y + c, peers


def _share_small(small, *, name):
    def body(small_ref, all_ref, send_sems, recv_sems, local_sem):
        me, peers = _me_and_peers()
        copies = [pltpu.make_async_remote_copy(src_ref=small_ref, dst_ref=all_ref.at[me], send_sem=send_sems.at[kk], recv_sem=recv_sems.at[kk],
                                               device_id=pos, device_id_type=pl.DeviceIdType.MESH) for kk, (pos, _) in enumerate(peers)]
        copies.append(pltpu.make_async_copy(small_ref, all_ref.at[me], local_sem))
        for cp in copies:
            cp.start()
        for cp in copies:
            cp.wait()

    hbm = pl.BlockSpec(memory_space=pl.ANY)
    return pl.pallas_call(
        body, name=name, in_specs=[hbm], out_specs=hbm, out_shape=jax.ShapeDtypeStruct((N_DEV, *small.shape), small.dtype),
        scratch_shapes=[pltpu.SemaphoreType.DMA((N_DEV - 1,)), pltpu.SemaphoreType.DMA((N_DEV - 1,)), pltpu.SemaphoreType.DMA],
        compiler_params=pltpu.CompilerParams(has_side_effects=True),
    )(small)


_HBM = pl.BlockSpec(memory_space=pltpu.HBM)
_SEM = pl.BlockSpec(memory_space=pltpu.SEMAPHORE)


def _exchange_copies(srcs, zones, send_sems, recv_sems, gather):
    me, peers = _me_and_peers()
    return [pltpu.make_async_remote_copy(
        src_ref=srcs[t] if gather else srcs[t].at[peer], dst_ref=zones[t].at[me], send_sem=send_sems.at[7 * t + kk],
        recv_sem=recv_sems.at[7 * t + kk], device_id=pos, device_id_type=pl.DeviceIdType.MESH)
        for t in range(len(srcs)) for kk, (pos, peer) in enumerate(peers)]


def _exchange_start(tensors, *, gather, name):
    n = len(tensors)
    zones = [lax.empty((N_DEV, *(t.shape if gather else t.shape[1:])), t.dtype) for t in tensors]

    def body(*refs):
        for cp in _exchange_copies(refs[:n], refs[n:2 * n], refs[2 * n], refs[2 * n + 1], gather):
            cp.start()
        refs[-1][...] = jnp.zeros_like(refs[-1])

    buffers = [pltpu.HBM(a.shape, a.dtype) for a in tensors + zones]
    res = pl.pallas_call(
        body, name=name, in_specs=[_HBM] * (2 * n),
        out_shape=(pltpu.SemaphoreType.DMA((7 * n,)), pltpu.SemaphoreType.DMA((7 * n,)), *buffers, jax.ShapeDtypeStruct((8, LANES), F32)),
        out_specs=(_SEM, _SEM, *[_HBM] * (2 * n), pl.BlockSpec(memory_space=pltpu.VMEM)),
        input_output_aliases={i: 2 + i for i in range(2 * n)},
        compiler_params=pltpu.CompilerParams(has_side_effects=pltpu.SideEffectType.DATAFLOW_SIDE_EFFECTING),
    )(*[pltpu.with_memory_space_constraint(a, pltpu.HBM) for a in tensors + zones])
    return dict(sems=res[:2], buffers=res[2:2 + 2 * n], gather=gather, started=res[-1])


def _exchange_wait(started, after, *, name):
    n = len(started["buffers"]) // 2

    def body(*refs):
        for cp in _exchange_copies(refs[:n], refs[n:2 * n], refs[2 * n], refs[2 * n + 1], started["gather"]):
            cp.wait_send()
            cp.wait_recv()

    res = pl.pallas_call(
        body, name=name, in_specs=[_HBM] * (2 * n) + [_SEM, _SEM] + [_ANY] * len(after),
        out_shape=tuple(pltpu.HBM(a.shape, a.dtype) for a in started["buffers"]), out_specs=tuple([_HBM] * (2 * n)),
        input_output_aliases={i: i for i in range(2 * n)},
        compiler_params=pltpu.CompilerParams(has_side_effects=pltpu.SideEffectType.DATAFLOW_SIDE_EFFECTING),
    )(*started["buffers"], *started["sems"], *after)
    return res[:n], res[n:]


def _adamw(contrib, w, m, v, *, name):
    rows, cols = w.shape
    tile = min(rows, ADAM_ROWS)

    def body(c_ref, w_ref, m_ref, v_ref, g_ref, d_ref, nm_ref, nv_ref):
        g = c_ref[0].astype(F32)
        for s in range(1, N_DEV):
            g = g + c_ref[s].astype(F32)
        m_new = ADAM_B1 * m_ref[...] + (1.0 - ADAM_B1) * g
        v_new = ADAM_B2 * v_ref[...] + (1.0 - ADAM_B2) * (g * g)
        m_hat = m_new / (1.0 - ADAM_B1 ** ADAM_STEP)
        v_hat = v_new / (1.0 - ADAM_B2 ** ADAM_STEP)
        g_ref[...] = g
        d_ref[...] = -ADAM_LR * (m_hat / (jnp.sqrt(v_hat) + ADAM_EPS) + ADAM_WD * w_ref[...])
        nm_ref[...] = m_new
        nv_ref[...] = v_new

    spec = pl.BlockSpec((tile, cols), lambda i: (i, 0))
    return pl.pallas_call(
        body, name=name, grid=(rows // tile,),
        in_specs=[pl.BlockSpec((N_DEV, tile, cols), lambda i: (0, i, 0)), spec, spec, spec], out_specs=[spec] * 4,
        out_shape=[jax.ShapeDtypeStruct((rows, cols), F32)] * 4, compiler_params=_cparams("parallel"),
    )(contrib, w, m, v)


class _Weights:
    def __init__(self, gathers, vectors, me):
        self.gathers, self.ready, self.me, self.after = gathers, dict(vectors), me, ()

    def arrive_after(self, *values):
        self.after = values

    def __getitem__(self, name):
        if name not in self.ready:
            gi = next(i for i, group in enumerate(GATHER_GROUPS) if name in group)
            after = [*self.after, *[g["started"] for g in self.gathers]]
            shards, zones = _exchange_wait(self.gathers[gi], after, name=f"gather_wait_{gi}")
            for n, shard, zone in zip(GATHER_GROUPS[gi], shards, zones, strict=True):
                blocks = lax.dynamic_update_slice_in_dim(zone, shard[None], self.me, 0)
                self.ready[n] = _to_kernel_layout(n, _from_blocks(blocks, n))
        return self.ready[name]


def kernel(x, mem, w_in, w_mem_kv, q_a_gain, w_q_b, kv_a_gain, w_kv_b, w_branch_mla, w_branch_sb, w_branch_mem, w_merge_gate, b_merge_gate, w_out, ln_gain, ln_bias, loss_target, m_w_in, m_w_mem_kv, m_q_a_gain, m_w_q_b, m_kv_a_gain, m_w_kv_b, m_w_branch_mla, m_w_branch_sb, m_w_branch_mem, m_w_merge_gate, m_b_merge_gate, m_w_out, m_ln_gain, m_ln_bias, v_w_in, v_w_mem_kv, v_q_a_gain, v_w_q_b, v_kv_a_gain, v_w_kv_b, v_w_branch_mla, v_w_branch_sb, v_w_branch_mem, v_w_merge_gate, v_b_merge_gate, v_w_out, v_ln_gain, v_ln_bias):
    given = dict(locals())
    small_names = [n for n, _ in SMALL]
    smalls = lambda prefix: [given[prefix + n] for n in small_names]
    me = 4 * lax.axis_index("x") + 2 * lax.axis_index("y") + lax.axis_index("c")

    gathers = [_exchange_start([given[n][0].astype(BF16) for n in group], gather=True, name=f"gather_start_{gi}")
               for gi, group in enumerate(GATHER_GROUPS)]
    w = _Weights(gathers, {n: given[n] for n in small_names}, me)
    exchanges = []
    results = [{}, {}, {}, {}]

    def finish(gi, after):
        names, started = exchanges[gi]
        sent, zones = _exchange_wait(started, after, name=f"grads_wait_{gi}")
        done = []
        for n, blocks, zone in zip(names, sent, zones, strict=True):
            own = lax.dynamic_index_in_dim(blocks, me, 0, keepdims=True)
            contrib = lax.dynamic_update_slice_in_dim(zone, own, me, 0)
            outs = _adamw(contrib, given[n][0], given["m_" + n][0], given["v_" + n][0], name=f"adamw_{n}")
            for kind, res in zip(results, outs, strict=True):
                kind[n] = res[None]
            done.append(outs[1])
        return done

    def emit(grads):
        blocks = [_to_blocks(_from_kernel_layout(n, g), n).astype(BF16) for n, g in grads.items()]
        exchanges.append((tuple(grads), _exchange_start(blocks, gather=False, name=f"grads_start_{len(exchanges)}")))
        started = [exchanges[-1][1]["started"]]
        if len(exchanges) == len(GRAD_GROUPS):
            for gi in range(len(GRAD_GROUPS) - 1):
                started += finish(gi, started[:1])
        return started

    loss, grad_x, grads = _local_step(x[0], mem[0], loss_target[0], w, emit)

    contrib_small = _share_small(_pack_small([grads[n] for n in small_names], loss), name="share_small")
    sml = _adamw(contrib_small, _pack_small(smalls("")), _pack_small(smalls("m_")), _pack_small(smalls("v_")), name="adamw_small")
    for kind, packed in zip(results, sml, strict=True):
        kind.update(zip(small_names, _unpack_small(packed), strict=True))
    finish(len(GRAD_GROUPS) - 1, [grad_x])
    order = ["w_in", "w_mem_kv", "q_a_gain", "w_q_b", "kv_a_gain", "w_kv_b", "w_branch_mla", "w_branch_sb", "w_branch_mem",
             "w_merge_gate", "b_merge_gate", "w_out", "ln_gain", "ln_bias"]
    loss_out = sml[0].reshape(-1)[LOSS_INDEX]
    return (loss_out, grad_x[None], *[kind[n] for kind in results for n in order])
```

```python
import math

import jax
import jax.numpy as jnp
from jax import lax
from jax.experimental import pallas as pl
from jax.experimental.pallas import tpu as pltpu

F32, BF16 = jnp.float32, jnp.bfloat16

N_DEV = 8
D_MODEL = 1024
MLA_HEADS, MLA_NOPE, MLA_ROPE, MLA_V = 8, 64, 32, 64
MLA_Q_LORA, MLA_KV_LORA = 256, 128
SB_HEAD_DIM = 64
MEM_HEAD_DIM = 128
ROPE_BASE = 10000.0
RMS_EPS = 1e-6
LN_EPS = 1e-5
DEEPNORM_ALPHA = 2.0 ** 0.25
ADAM_LR, ADAM_B1, ADAM_B2, ADAM_EPS, ADAM_WD, ADAM_STEP = 0.001, 0.9, 0.999, 1e-08, 0.01, 10
LOG2E, LN2 = math.log2(math.e), math.log(2.0)

LANES = 128
GROUPS = 4
PROJ_WIDTH = 4096
COL_CQ, COL_CKV, COL_KROPE, COL_GATE_A, COL_GATE_B, COL_GATE_M = 0, 256, 384, 512, 1024, 1536
QKV_FIRST, QKV_WIDTH = 2048, 2048
COL_QB, COL_KB, COL_VB, COL_QM = 0, 512, 1024, 1536
IN_PIECES = ((0, 416), None, (416, 512), (2464, 512), (3488, 512), (928, 512), (1440, 512), (1952, 512), (2976, 512))
IN_PAD = 96

VMEM_LIMIT_BYTES = 56 * 1024 * 1024
NEG_BIG = -1e30
Q_BLOCK = 512
SB_BWD_Q_BLOCK = 512
TRI_BLOCK = 256
TILE_ROWS = 64
KEY_CHUNK = 512

SHARDED = {
    "w_in": ((1024, 4000), 1), "w_mem_kv": ((1024, 1024), 0), "w_q_b": ((256, 768), 1), "w_kv_b": ((128, 1024), 1),
    "w_branch_mla": ((512, 1024), 1), "w_branch_sb": ((512, 1024), 1), "w_branch_mem": ((512, 1024), 1),
    "w_merge_gate": ((1024, 3072), 1), "w_out": ((1024, 1024), 0),
}
GATHER_GROUPS = (("w_in",), ("w_merge_gate",), ("w_q_b", "w_kv_b", "w_mem_kv", "w_branch_mla", "w_branch_sb", "w_branch_mem", "w_out"))
GRAD_GROUPS = (("w_out", "w_merge_gate", "w_branch_mla", "w_branch_sb", "w_branch_mem"), ("w_mem_kv", "w_q_b", "w_kv_b"), ("w_in",))
SMALL = (("q_a_gain", 256), ("kv_a_gain", 128), ("b_merge_gate", 3072), ("ln_gain", 1024), ("ln_bias", 1024))
SMALL_ROWS, SMALL_LANES = 48, 128
ADAM_ROWS = 256
LOSS_INDEX = 5504


def _cparams(*sem):
    return pltpu.CompilerParams(dimension_semantics=sem or None, vmem_limit_bytes=VMEM_LIMIT_BYTES)


_DIMS = {"nn": (((1,), (0,)), ((), ())), "nt": (((1,), (1,)), ((), ())), "tn": (((0,), (0,)), ((), ()))}


def _dot(a, b, dims):
    return lax.dot_general(a, b, _DIMS[dims], preferred_element_type=F32)


def _tile(dim, want):
    if dim <= want:
        return dim
    t = want - want % LANES
    while dim % t:
        t -= LANES
    assert t > 0, (dim, want)
    return t


_ANY = pl.BlockSpec(memory_space=pl.ANY)


def _mm(a, b, dims, *, name, out_dtype=F32, add=None, add_scale=1.0, col_scale=None, b_cols=None, behind=None,
        tm=1024, tn=1024, tk=1024):
    batch = a.shape[0] if a.ndim == 3 else None
    if dims == "nn":
        (m, k), (k2, n) = a.shape[-2:], b.shape[-2:]
    elif dims == "nt":
        (m, k), (n, k2) = a.shape[-2:], b.shape[-2:]
    else:
        (k, m), (k2, n) = a.shape[-2:], b.shape[-2:]
    assert k == k2 and a.ndim == b.ndim, (a.shape, b.shape, dims)
    assert batch is None or (b.shape[0] == batch and add is None and col_scale is None and b_cols is None)
    b_first = 0
    if b_cols is not None:
        assert dims == "nn"
        b_first, n = b_cols
    tm, tn, tk = _tile(m, tm), _tile(n, tn), _tile(k, tk)
    assert b_first % tn == 0
    jb = b_first // tn
    nk = k // tk

    def spec(block, index):
        if batch is None:
            return pl.BlockSpec(block, lambda bb, i, j, kk: index(i, j, kk))
        return pl.BlockSpec((None, *block), lambda bb, i, j, kk: (bb, *index(i, j, kk)))

    a_spec = spec((tk, tm), lambda i, j, kk: (kk, i)) if dims == "tn" else spec((tm, tk), lambda i, j, kk: (i, kk))
    b_spec = spec((tn, tk), lambda i, j, kk: (j, kk)) if dims == "nt" else spec((tk, tn), lambda i, j, kk: (kk, jb + j))
    o_spec = spec((tm, tn), lambda i, j, kk: (i, j))
    behind = [] if behind is None else behind if isinstance(behind, (list, tuple)) else [behind]
    optional = [(add, o_spec), (col_scale, pl.BlockSpec((1, tn), lambda bb, i, j, kk: (0, j))), *[(v, _ANY) for v in behind]]
    present = [(v, spec) for v, spec in optional if v is not None]

    def body(*refs):
        a_ref, b_ref = refs[:2]
        extra = iter(refs[2:2 + len(present)])
        add_ref = next(extra) if add is not None else None
        scale_ref = next(extra) if col_scale is not None else None
        o_ref = refs[2 + len(present)]
        part = _dot(a_ref[...].astype(BF16), b_ref[...].astype(BF16), dims)

        def finish(r):
            if add is not None:
                r = r + add_scale * add_ref[...]
            if col_scale is not None:
                r = r * scale_ref[...]
            o_ref[...] = r.astype(out_dtype)

        if nk == 1:
            finish(part)
            return
        acc = refs[-1]
        kk = pl.program_id(3)

        @pl.when(kk == 0)
        def _():
            acc[...] = part

        @pl.when(kk > 0)
        def _():
            acc[...] += part

        @pl.when(kk == nk - 1)
        def _():
            finish(acc[...])

    return pl.pallas_call(
        body, name=name, grid=(batch or 1, m // tm, n // tn, nk),
        in_specs=[a_spec, b_spec] + [spec for _, spec in present], out_specs=o_spec,
        out_shape=jax.ShapeDtypeStruct((m, n) if batch is None else (batch, m, n), out_dtype),
        scratch_shapes=[pltpu.VMEM((tm, tn), F32)] if nk > 1 else [],
        compiler_params=_cparams("parallel", "parallel", "parallel", "arbitrary"),
    )(a, b, *[v for v, _ in present])


def _rowwise(fn, ins, outs, *, name, rows, tr=512):
    n_in = len(ins)
    tr = min(tr, rows)
    in_specs, args = [], []
    for it in ins:
        arr, w, off = it if isinstance(it, tuple) else (it, it.shape[-1], 0)
        assert off % w == 0
        cb = off // w
        if arr.ndim == 3:
            in_specs.append(pl.BlockSpec((arr.shape[0], tr, w), lambda i, cb=cb: (0, i, cb)))
        elif arr.shape[0] == 1:
            in_specs.append(pl.BlockSpec((1, w), lambda i, cb=cb: (0, cb)))
        else:
            in_specs.append(pl.BlockSpec((tr, w), lambda i, cb=cb: (i, cb)))
        args.append(arr)
    out_shape, out_specs, is_sum = [], [], []
    for out in outs:
        is_sum.append(out[0] == "sum")
        if out[0] == "sum":
            out_shape.append(jax.ShapeDtypeStruct((1, out[1]), F32))
            out_specs.append(pl.BlockSpec((1, out[1]), lambda i: (0, 0)))
        elif len(out) == 3:
            out_shape.append(jax.ShapeDtypeStruct((out[0], rows, out[1]), out[2]))
            out_specs.append(pl.BlockSpec((out[0], tr, out[1]), lambda i: (0, i, 0)))
        else:
            out_shape.append(jax.ShapeDtypeStruct((rows, out[0]), out[1]))
            out_specs.append(pl.BlockSpec((tr, out[0]), lambda i: (i, 0)))

    def body(*refs):
        res = fn(*[r[...] for r in refs[:n_in]])
        for r, val, s in zip(refs[n_in:], res, is_sum, strict=True):
            if s:
                @pl.when(pl.program_id(0) == 0)
                def _(r=r):
                    r[...] = jnp.zeros_like(r)

                r[...] += val
            elif isinstance(val, (list, tuple)):
                for n, part in enumerate(val):
                    r[n] = part.astype(r.dtype)
            else:
                r[...] = val.astype(r.dtype)

    return pl.pallas_call(
        body, name=name, grid=(rows // tr,), in_specs=in_specs, out_specs=out_specs, out_shape=out_shape,
        compiler_params=_cparams("arbitrary"),
    )(*args)


def _colsum(v):
    return jnp.sum(v, axis=0, keepdims=True)


def _sigmoid(v):
    return 1.0 / (1.0 + jnp.exp(-v))


def _lane_groups(v):
    return [v[:, g * LANES:(g + 1) * LANES] for g in range(v.shape[1] // LANES)]


def _swap_halves(v, first_lane):
    lane = lax.broadcasted_iota(jnp.int32, v.shape, 1)
    return jnp.where(lane < first_lane + 16, pltpu.roll(v, 112, axis=1), pltpu.roll(v, 16, axis=1))


def _lane_sum(acc, v):
    for part in _lane_groups(v):
        acc = acc + part
    return acc


def _low_half(shape):
    return lax.broadcasted_iota(jnp.int32, shape, 1) < LANES // 2


def _select_heads(per_head, pick):
    if len(per_head) == 1:
        return pick(per_head[0], 0)
    return jnp.where(_low_half(per_head[0].shape), pick(per_head[0], 0), pick(per_head[1], 1))


def _attn_specs(s, sk, hp, bq, q0, k0, v0):
    wq = hp * LANES
    assert q0 % wq == 0 and k0 % wq == 0 and v0 % LANES == 0
    qb0, kb0, vb0 = q0 // wq, k0 // wq, v0 // LANES
    q_spec = pl.BlockSpec((bq, wq), lambda g, i: (i, qb0 + g))
    k_spec = pl.BlockSpec((sk, wq), lambda g, i: (0, kb0 + g))
    v_spec = pl.BlockSpec((sk, LANES), lambda g, i: (0, vb0 + g))
    row_out = lambda w: pl.BlockSpec((bq, w), lambda g, i: (i, g))
    key_out = lambda w: pl.BlockSpec((sk, w), lambda g, i: (0, g))
    return q_spec, k_spec, v_spec, row_out, key_out


def _chunks(i, bq, ch, sk, causal):
    return ((i + 1) * bq - 1) // ch if causal else jnp.int32(sk // ch - 1)


def _positions(i, c, bq, ch):
    return (i * bq + lax.broadcasted_iota(jnp.int32, (bq, ch), 0), c * ch + lax.broadcasted_iota(jnp.int32, (bq, ch), 1))


def _softmax_fwd(q, k, v, *, hp, causal, name, q0=0, k0=0, v0=0):
    s, sk = q.shape[0], k.shape[0]
    bq, ch = min(Q_BLOCK, s), min(KEY_CHUNK, sk)
    q_spec, k_spec, v_spec, row_out, _ = _attn_specs(s, sk, hp, bq, q0, k0, v0)

    def body(q_ref, k_ref, v_ref, o_ref, lse_ref, s_scr):
        i = pl.program_id(1)
        qs = _lane_groups(q_ref[...])
        last = _chunks(i, bq, ch, sk, causal)

        def scores(c, ms, masked):
            off = pl.multiple_of(c * ch, ch)
            out = []
            for j in range(hp):
                sc = _dot(qs[j], k_ref[pl.ds(off, ch), j * LANES:(j + 1) * LANES], "nt")
                if masked:
                    qpos, kpos = _positions(i, c, bq, ch)
                    sc = jnp.where(kpos <= qpos, sc, NEG_BIG)
                s_scr[j, c] = sc
                m = ms[j]
                for part in _lane_groups(sc):
                    m = jnp.maximum(m, part)
                out.append(m)
            return tuple(out)

        ms = lax.fori_loop(0, last, lambda c, m: scores(c, m, False), tuple(jnp.full((bq, LANES), NEG_BIG, F32) for _ in range(hp)))
        ms = scores(last, ms, causal)
        row_max = [jnp.max(m, axis=1, keepdims=True) for m in ms]

        def weigh(c, carry):
            off = pl.multiple_of(c * ch, ch)
            vt = v_ref[pl.ds(off, ch), :]
            out = []
            for j in range(hp):
                l, acc = carry[j]
                p = jnp.exp2(s_scr[j, c] - row_max[j])
                out.append((_lane_sum(l, p), acc + _dot(p.astype(BF16), vt, "nn")))
            return tuple(out)

        zero = jnp.zeros((bq, LANES), F32)
        res = lax.fori_loop(0, last + 1, weigh, tuple((zero, zero) for _ in range(hp)))
        row_sum = [jnp.sum(l, axis=1, keepdims=True) for l, _ in res]
        o_ref[...] = _select_heads([acc for _, acc in res], lambda acc, j: acc / row_sum[j])
        lse_ref[...] = _select_heads([jnp.broadcast_to(row_max[j] + jnp.log2(row_sum[j]), (bq, LANES)) for j in range(hp)], lambda a, j: a)

    return pl.pallas_call(
        body, name=name, grid=(GROUPS, s // bq), in_specs=[q_spec, k_spec, v_spec], out_specs=[row_out(LANES), row_out(LANES)],
        out_shape=[jax.ShapeDtypeStruct((s, GROUPS * LANES), F32)] * 2,
        scratch_shapes=[pltpu.VMEM((hp, sk // ch, bq, ch), F32)], compiler_params=_cparams("parallel", "arbitrary"),
    )(q, k, v)


def _head_cotangent(do, j, hp):
    if hp == 1:
        return do
    return jnp.where(_low_half(do.shape) == (j == 0), do, 0.0)


def _softmax_bwd(q, k, v, o, do, lse, behind, *, hp, causal, dq_scale, name, q0=0, k0=0, v0=0):
    s, sk = q.shape[0], k.shape[0]
    bq, ch = min(Q_BLOCK, s), min(KEY_CHUNK, sk)
    wq = hp * LANES
    q_spec, k_spec, v_spec, row_out, key_out = _attn_specs(s, sk, hp, bq, q0, k0, v0)

    def body(q_ref, k_ref, v_ref, o_ref, do_ref, lse_ref, _, dq_ref, dk_ref, dv_ref):
        i = pl.program_id(1)

        @pl.when(i == 0)
        def _():
            dk_ref[...] = jnp.zeros_like(dk_ref)
            dv_ref[...] = jnp.zeros_like(dv_ref)

        qs = _lane_groups(q_ref[...])
        do_all, o_all, lse_all = do_ref[...], o_ref[...], lse_ref[...]
        dos, deltas, lses = [], [], []
        for j in range(hp):
            d = _head_cotangent(do_all, j, hp)
            deltas.append(jnp.sum(d * o_all, axis=1, keepdims=True))
            dos.append(d.astype(BF16))
            lses.append(lse_all[:, j * (LANES // hp):j * (LANES // hp) + 1])
        last = _chunks(i, bq, ch, sk, causal)

        def chunk(c, dqs, masked):
            off = pl.multiple_of(c * ch, ch)
            vt = v_ref[pl.ds(off, ch), :]
            out, dks, dv = [], [], None
            for j in range(hp):
                kt = k_ref[pl.ds(off, ch), j * LANES:(j + 1) * LANES]
                p = jnp.exp2(_dot(qs[j], kt, "nt") - lses[j])
                if masked:
                    qpos, kpos = _positions(i, c, bq, ch)
                    p = jnp.where(kpos <= qpos, p, 0.0)
                ds = (p * (_dot(dos[j], vt, "nt") - deltas[j]) * LN2).astype(BF16)
                out.append(dqs[j] + _dot(ds, kt, "nn"))
                dks.append(_dot(ds, qs[j], "tn"))
                dvj = _dot(p.astype(BF16), dos[j], "tn")
                dv = dvj if dv is None else dv + dvj
            dk_ref[pl.ds(off, ch), :] += dks[0] if hp == 1 else jnp.concatenate(dks, axis=1)
            dv_ref[pl.ds(off, ch), :] += dv
            return tuple(out)

        dqs = lax.fori_loop(0, last, lambda c, d: chunk(c, d, False), tuple(jnp.zeros((bq, LANES), F32) for _ in range(hp)))
        dqs = chunk(last, dqs, causal)
        dq_ref[...] = (dqs[0] if hp == 1 else jnp.concatenate(dqs, axis=1)) * dq_scale

    return pl.pallas_call(
        body, name=name, grid=(GROUPS, s // bq),
        in_specs=[q_spec, k_spec, v_spec, row_out(LANES), row_out(LANES), row_out(LANES), _ANY],
        out_specs=[row_out(wq), key_out(wq), key_out(LANES)],
        out_shape=[jax.ShapeDtypeStruct((s, GROUPS * wq), F32), jax.ShapeDtypeStruct((sk, GROUPS * wq), F32),
                   jax.ShapeDtypeStruct((sk, GROUPS * LANES), F32)],
        compiler_params=_cparams("arbitrary", "arbitrary"),
    )(q, k, v, o, do, lse, behind)


def _log2_sigmoid_pair(z2):
    minus_abs = lax.bitcast_convert_type(lax.bitcast_convert_type(z2, jnp.uint32) | jnp.uint32(0x80000000), F32)
    log_beta = jnp.minimum(z2, 0.0) - jnp.log2(1.0 + jnp.exp2(minus_abs))
    return log_beta, log_beta - z2


def _tilewise(fn, *arrays):
    rows, cols = arrays[0].shape
    step = min(TILE_ROWS, rows)
    grid = [[fn(*[None if a is None else a[r:r + step, c:c + LANES] for a in arrays]) for c in range(0, cols, LANES)]
            for r in range(0, rows, step)]
    return [jnp.concatenate([jnp.concatenate([cell[k] for cell in row], axis=1) for row in grid], axis=0)
            for k in range(len(grid[0][0]))]


def _split(v):
    hi = v.astype(BF16)
    return hi, (v - hi.astype(F32)).astype(BF16)


def _tri(n, after):
    rows, cols = lax.broadcasted_iota(jnp.int32, (n, n), 0), lax.broadcasted_iota(jnp.int32, (n, n), 1)
    return (rows > cols if after else rows < cols).astype(BF16)


def _running_sums(v, terms, start, tri, backwards):
    n = tri.shape[0]
    n_blocks = v.shape[1] // n
    order = range(n_blocks - 1, -1, -1) if backwards else range(n_blocks)
    stacked = tri if len(terms) == 1 else jnp.concatenate([tri] * len(terms), axis=0)
    parts, run = [None] * n_blocks, start
    for t in order:
        cols = slice(t * n, (t + 1) * n)
        lhs = terms[0][:, cols] if len(terms) == 1 else jnp.concatenate([term[:, cols] for term in terms], axis=1)
        parts[t] = _dot(lhs, stacked, "nn") + run
        run = run + jnp.sum(v[:, cols], axis=1, keepdims=True)
    return (parts[0] if n_blocks == 1 else jnp.concatenate(parts, axis=1)), run


def _sb_weights(qm, kt, run, tri, strict):
    def logs(z2, keep):
        log_beta, log_keep = _log2_sigmoid_pair(z2)
        if keep is not None:
            log_keep = jnp.where(keep, log_keep, 0.0)
        return log_beta, log_keep, *_split(log_keep)

    log_beta, log_keep, hi, lo = _tilewise(logs, _dot(qm, kt, "nt"), strict)
    behind, run = _running_sums(log_keep, (hi, lo), run, tri, True)

    def weigh(log_beta, behind, keep):
        a = jnp.exp2(log_beta + behind)
        return (a if keep is None else jnp.where(keep, a, 0.0),)

    (a,) = _tilewise(weigh, log_beta, behind, strict)
    return a, log_beta, run


def _sb_queries(q_all):
    low = _low_half(q_all.shape)
    zero = jnp.zeros_like(q_all)
    return [jnp.where(low, q_all, zero), jnp.where(low, zero, q_all)]


def _sb_fwd(qkv, *, q0, k0, v0, name):
    s = qkv.shape[0]
    bq, ch = min(Q_BLOCK, s), min(KEY_CHUNK, s)
    q_spec, k_spec, v_spec, row_out, _ = _attn_specs(s, s, 1, bq, q0, k0, v0)

    def body(q_ref, k_ref, v_ref, o_ref):
        i = pl.program_id(1)
        qms = _sb_queries(q_ref[...])
        tri = _tri(min(TRI_BLOCK, ch), True)
        last = _chunks(i, bq, ch, s, True)

        def chunk(c, carry, masked):
            off = pl.multiple_of(c * ch, ch)
            kt, vt = k_ref[pl.ds(off, ch), :], v_ref[pl.ds(off, ch), :]
            strict = None
            if masked:
                qpos, kpos = _positions(i, c, bq, ch)
                strict = kpos < qpos
            out = []
            for j in range(2):
                run, acc = carry[j]
                a, _, run = _sb_weights(qms[j], kt, run, tri, strict)
                out.append((run, acc + _dot(a.astype(BF16), vt, "nn")))
            return tuple(out)

        carry = chunk(last, tuple((jnp.zeros((bq, 1), F32), jnp.zeros((bq, LANES), F32)) for _ in range(2)), True)
        res = lax.fori_loop(0, last, lambda n, c: chunk(last - 1 - n, c, False), carry)
        o_ref[...] = _select_heads([acc for _, acc in res], lambda acc, j: acc)

    return pl.pallas_call(
        body, name=name, grid=(GROUPS, s // bq), in_specs=[q_spec, k_spec, v_spec], out_specs=row_out(LANES),
        out_shape=jax.ShapeDtypeStruct((s, GROUPS * LANES), F32), compiler_params=_cparams("parallel", "arbitrary"),
    )(qkv, qkv, qkv)


def _sb_bwd(qkv, do, behind, *, q0, k0, v0, dq_scale, name):
    s = qkv.shape[0]
    bq, ch = min(SB_BWD_Q_BLOCK, s), min(KEY_CHUNK, s)
    q_spec, k_spec, v_spec, row_out, key_out = _attn_specs(s, s, 1, bq, q0, k0, v0)

    def body(q_ref, k_ref, v_ref, do_ref, _, dq_ref, dk_ref, dv_ref, g_s, beta_s):
        i = pl.program_id(1)

        @pl.when(i == 0)
        def _():
            dk_ref[...] = jnp.zeros_like(dk_ref)
            dv_ref[...] = jnp.zeros_like(dv_ref)

        qms = _sb_queries(q_ref[...])
        do_all = do_ref[...]
        dos = [_head_cotangent(do_all, j, 2).astype(BF16) for j in range(2)]
        dos_ln2 = [(_head_cotangent(do_all, j, 2) * LN2).astype(BF16) for j in range(2)]
        tri_after, tri_before = _tri(min(TRI_BLOCK, ch), True), _tri(min(TRI_BLOCK, ch), False)
        last = _chunks(i, bq, ch, s, True)

        def strict_mask(c):
            qpos, kpos = _positions(i, c, bq, ch)
            return kpos < qpos

        def sweep1(c, runs, masked):
            off = pl.multiple_of(c * ch, ch)
            kt, vt = k_ref[pl.ds(off, ch), :], v_ref[pl.ds(off, ch), :]
            strict = strict_mask(c) if masked else None
            out, dv = [], None
            for j in range(2):
                a, log_beta, run = _sb_weights(qms[j], kt, runs[j], tri_after, strict)
                g_s[j, c] = (a * _dot(dos_ln2[j], vt, "nt")).astype(BF16)
                beta_s[j, c] = jnp.exp2(log_beta).astype(BF16)
                dvj = _dot(a.astype(BF16), dos[j], "tn")
                dv = dvj if dv is None else dv + dvj
                out.append(run)
            dv_ref[pl.ds(off, ch), :] += dv
            return tuple(out)

        runs = sweep1(last, tuple(jnp.zeros((bq, 1), F32) for _ in range(2)), True)
        lax.fori_loop(0, last, lambda n, r: sweep1(last - 1 - n, r, False), runs)

        def sweep2(c, carry, masked):
            off = pl.multiple_of(c * ch, ch)
            kt = k_ref[pl.ds(off, ch), :]
            out, dk = [], None
            for j in range(2):
                before, dq = carry[j]
                g16, beta = g_s[j, c], beta_s[j, c].astype(F32)
                g = g16.astype(F32)
                in_front, before = _running_sums(g, (g16,), before, tri_before, False)
                dz = g * (1.0 - beta) - beta * in_front
                if masked:
                    dz = jnp.where(strict_mask(c), dz, 0.0)
                dz = dz.astype(BF16)
                dkj = _dot(dz, qms[j], "tn")
                dk = dkj if dk is None else dk + dkj
                out.append((before, dq + _dot(dz, kt, "nn")))
            dk_ref[pl.ds(off, ch), :] += dk
            return tuple(out)

        carry = lax.fori_loop(0, last, lambda c, cr: sweep2(c, cr, False),
                              tuple((jnp.zeros((bq, 1), F32), jnp.zeros((bq, LANES), F32)) for _ in range(2)))
        res = sweep2(last, carry, True)
        dq_ref[...] = _select_heads([dq for _, dq in res], lambda dq, j: dq) * dq_scale

    n_ch = s // ch
    return pl.pallas_call(
        body, name=name, grid=(GROUPS, s // bq), in_specs=[q_spec, k_spec, v_spec, row_out(LANES), _ANY],
        out_specs=[row_out(LANES), key_out(LANES), key_out(LANES)],
        out_shape=[jax.ShapeDtypeStruct((s, GROUPS * LANES), F32)] * 3,
        scratch_shapes=[pltpu.VMEM((2, n_ch, bq, ch), BF16)] * 2,
        compiler_params=_cparams("arbitrary", "arbitrary"),
    )(qkv, qkv, qkv, do, behind)


def _rope_tables(s):
    half = MLA_ROPE // 2
    freqs = ROPE_BASE ** (-jnp.arange(half, dtype=F32) / half)
    ang = jnp.arange(s, dtype=F32)[:, None] * freqs[None, :]
    cos, sin = jnp.cos(ang), jnp.sin(ang)
    tail = jnp.zeros((s, LANES - MLA_NOPE - MLA_ROPE), F32)
    lead = lambda fill: jnp.full((s, MLA_NOPE), fill, F32)
    return dict(
        cos_k0=jnp.concatenate([cos, cos, lead(0.0), tail], axis=1), sin_k0=jnp.concatenate([-sin, sin, lead(0.0), tail], axis=1),
        cos_k64=jnp.concatenate([lead(0.0), cos, cos, tail], axis=1), sin_k64=jnp.concatenate([lead(0.0), -sin, sin, tail], axis=1),
        cos_q=jnp.concatenate([lead(1.0), cos, cos, tail], axis=1),
        sin_k64_t=jnp.concatenate([lead(0.0), sin, -sin, tail], axis=1),
    )


def _local_step(x, mem, target, w, emit=lambda grads: [jnp.zeros((8, LANES), F32)]):
    s = x.shape[0]
    rope = _rope_tables(s)
    xb = x.astype(BF16)
    inv_d = 1.0 / D_MODEL
    scale_a = LOG2E / math.sqrt(MLA_NOPE + MLA_ROPE)
    scale_b = LOG2E / math.sqrt(SB_HEAD_DIM)
    scale_m = LOG2E / math.sqrt(MEM_HEAD_DIM)
    arrive_after = getattr(w, "arrive_after", lambda *values: None)
    memb = mem.astype(BF16)

    arrive_after(xb, memb, *rope.values())
    proj = _mm(xb, w["w_in"], "nn", name="proj", b_cols=(0, QKV_FIRST))
    one = jnp.ones((1, 512), F32)
    qkv = _mm(xb, w["w_in"], "nn", name="proj_qkv", b_cols=(QKV_FIRST, QKV_WIDTH), out_dtype=BF16,
              col_scale=jnp.concatenate([one * scale_b, one, one, one * scale_m], axis=1))
    arrive_after(qkv)
    pre = _mm(xb, w["w_merge_gate"], "nn", name="merge_pre", out_dtype=BF16)
    arrive_after(pre)

    def rms_pair(c_q, c_kv, g_q, g_kv):
        return (c_q * lax.rsqrt(jnp.mean(c_q * c_q, axis=1, keepdims=True) + RMS_EPS) * g_q,
                c_kv * lax.rsqrt(jnp.mean(c_kv * c_kv, axis=1, keepdims=True) + RMS_EPS) * g_kv)

    n_q, n_kv = _rowwise(rms_pair, [(proj, 256, COL_CQ), (proj, 128, COL_CKV), w["q_a_gain"], w["kv_a_gain"]],
                         [(256, BF16), (128, BF16)], name="rms_pair", rows=s)
    q_a = _mm(n_q, w["w_q_b"], "nn", name="q_up")
    kv_a = _mm(n_kv, w["w_kv_b"], "nn", name="kv_up", out_dtype=BF16)

    def rope_q(qa, cos, sin):
        return (jnp.concatenate([(g * cos + _swap_halves(g, MLA_NOPE) * sin) * scale_a for g in _lane_groups(qa)], axis=1),)

    (q_mla,) = _rowwise(rope_q, [q_a, rope["cos_q"], rope["sin_k64"]], [(1024, BF16)], name="rope_q", rows=s)

    def rope_k(k_nope, k_rope, cos, sin):
        k_pe = pltpu.roll(k_rope * cos + _swap_halves(k_rope, 0) * sin, MLA_NOPE, axis=1).astype(BF16)
        return (jnp.concatenate([g + k_pe for g in _lane_groups(k_nope)], axis=1),)

    (k_mla,) = _rowwise(rope_k, [(kv_a, 1024, 0), (proj, 128, COL_KROPE), rope["cos_k0"], rope["sin_k0"]],
                        [(1024, BF16)], name="rope_k", rows=s)
    o_a, lse_a = _softmax_fwd(q_mla, k_mla, kv_a, hp=2, causal=True, name="mla_fwd", v0=1024)

    o_b = _sb_fwd(qkv, q0=COL_QB, k0=COL_KB, v0=COL_VB, name="sb_fwd")

    mem_kv =_mm(memb, w["w_mem_kv"], "nn", name="mem_kv", out_dtype=BF16)
    o_m, lse_m = _softmax_fwd(qkv, mem_kv, mem_kv, hp=1, causal=False, name="mem_fwd", q0=1536, v0=512)

    o_br = {"mla": o_a, "sb": o_b, "mem": o_m}
    gate_col = {"mla": COL_GATE_A, "sb": COL_GATE_B, "mem": COL_GATE_M}

    branches = ("mla", "sb", "mem")
    branch_ins = [o_br[br] for br in branches] + [(proj, 512, gate_col[br]) for br in branches]
    w_branch = jnp.stack([w[f"w_branch_{br}"] for br in branches])

    def gated(oa, ob, om, ga, gb, gm):
        return ([o * gate * _sigmoid(gate) for o, gate in ((oa, ga), (ob, gb), (om, gm))],)

    (u,) = _rowwise(gated, branch_ins, [(3, 512, BF16)], name="gated", rows=s)
    y = _mm(u, w_branch, "nn", name="branch", out_dtype=BF16)

    def merge(pa, pb, pm, ba, bb, bm, ys):
        return (sum(_sigmoid(p.astype(F32) + b) * ys[n].astype(F32) for n, (p, b) in enumerate(((pa, ba), (pb, bb), (pm, bm)))),)

    bias = w["b_merge_gate"]
    gate_ins = [(pre, 1024, 0), (pre, 1024, 1024), (pre, 1024, 2048), (bias, 1024, 0), (bias, 1024, 1024), (bias, 1024, 2048)]
    (merged,) = _rowwise(merge, gate_ins + [y], [(1024, BF16)], name="merge", rows=s)
    out = _mm(merged, w["w_out"], "nn", name="out_proj")

    def norm_loss(xv, ov, tv, gain, bias_ln):
        z = DEEPNORM_ALPHA * xv + ov
        zc = z - jnp.mean(z, axis=1, keepdims=True)
        rstd = lax.rsqrt(jnp.mean(zc * zc, axis=1, keepdims=True) + LN_EPS)
        xhat = zc * rstd
        err = xhat * gain + bias_ln - tv
        loss = 0.5 * jnp.sum(jnp.mean(err * err, axis=1, keepdims=True), axis=0, keepdims=True)
        dy = err * inv_d
        dxhat = dy * gain
        dz = rstd * (dxhat - jnp.mean(dxhat, axis=1, keepdims=True) - xhat * jnp.mean(dxhat * xhat, axis=1, keepdims=True))
        return dz, dz, _colsum(dy * xhat), _colsum(dy), jnp.broadcast_to(loss, (1, LANES))

    dz, dzb, g_ln_gain, g_ln_bias, loss = _rowwise(
        norm_loss, [x, out, target, w["ln_gain"], w["ln_bias"]],
        [(1024, F32), (1024, BF16), ("sum", 1024), ("sum", 1024), ("sum", LANES)], name="norm_loss", rows=s)

    grads = {"ln_gain": g_ln_gain, "ln_bias": g_ln_bias}
    dmerged = _mm(dzb, w["w_out"], "nt", name="d_merged")
    grads["w_out"] = _mm(merged, dzb, "tn", name="g_w_out")

    def merge_bwd(dm, pa, pb, pm, ba, bb, bm, ys):
        dys, dpre = [], []
        for n, (p, b) in enumerate(((pa, ba), (pb, bb), (pm, bm))):
            g = _sigmoid(p.astype(F32) + b)
            dpre.append(dm * ys[n].astype(F32) * g * (1.0 - g))
            dys.append(dm * g)
        dpre = jnp.concatenate(dpre, axis=1)
        return dpre, _colsum(dpre), dys

    dpre, grads["b_merge_gate"], dy = _rowwise(
        merge_bwd, [dmerged] + gate_ins + [y], [(3072, BF16), ("sum", 3072), (3, 1024, BF16)], name="merge_bwd", rows=s, tr=256)
    grads["w_merge_gate"] = _mm(xb, dpre, "tn", name="g_w_merge")
    dx = _mm(dpre, w["w_merge_gate"], "nt", name="dx_merge", add=dz, add_scale=DEEPNORM_ALPHA)
    g_w_branch = _mm(u, dy, "tn", name="g_w_branch")
    du = _mm(dy, w_branch, "nt", name="d_u")
    for n, br in enumerate(branches):
        grads[f"w_branch_{br}"] = g_w_branch[n]

    def gated_bwd(dus, oa, ob, om, ga, gb, gm):
        d_os, d_gates = [], []
        for n, (o, gate) in enumerate(((oa, ga), (ob, gb), (om, gm))):
            sg = _sigmoid(gate)
            d_os.append(dus[n] * gate * sg)
            d_gates.append(dus[n] * o * sg * (1.0 + gate * (1.0 - sg)))
        return *d_os, *d_gates

    res = _rowwise(gated_bwd, [du] + branch_ins, [(512, F32)] * 3 + [(512, BF16)] * 3, name="gated_bwd", rows=s)
    d_o, d_gate = dict(zip(branches, res[:3], strict=True)), dict(zip(branches, res[3:], strict=True))
    (sent,) = emit({n: grads[n] for n in ("w_out", "w_merge_gate", "w_branch_mla", "w_branch_sb", "w_branch_mem")})

    dq_m, dk_m, dv_m = _softmax_bwd(qkv, mem_kv, mem_kv, o_m, d_o["mem"], lse_m, sent, hp=1, causal=False, dq_scale=scale_m,
                                    name="mem_bwd", q0=1536, v0=512)
    grads["w_mem_kv"] = _mm(memb, jnp.concatenate([dk_m, dv_m], axis=1), "tn", name="g_w_mem_kv")

    dq_sb, dk_sb, dv_sb = _sb_bwd(qkv, d_o["sb"], sent, q0=0, k0=512, v0=1024, dq_scale=scale_b, name="sb_bwd")

    dq_mla, dk_mla, dv_a = _softmax_bwd(q_mla, k_mla, kv_a, o_a, d_o["mla"], lse_a, sent, hp=2, causal=True, dq_scale=scale_a,
                                        name="mla_bwd", v0=1024)

    def rope_q_bwd(dq, cos, sin):
        return (jnp.concatenate([g * cos + _swap_halves(g, MLA_NOPE) * sin for g in _lane_groups(dq)], axis=1),)

    (dq_a,) = _rowwise(rope_q_bwd, [dq_mla, rope["cos_q"], rope["sin_k64_t"]], [(1024, BF16)], name="rope_q_bwd", rows=s)
    grads["w_q_b"] = _mm(n_q, dq_a, "tn", name="g_w_q_b")
    dn_q = _mm(dq_a, w["w_q_b"], "nt", name="d_n_q")

    def rope_k_bwd(dk, cos, sin):
        groups = _lane_groups(dk)
        g = groups[0]
        for other in groups[1:]:
            g = g + other
        d_rope = pltpu.roll(g * cos + _swap_halves(g, MLA_NOPE) * sin, MLA_NOPE, axis=1)
        nope = _low_half(g.shape)
        return d_rope, jnp.concatenate([jnp.where(nope, grp, 0.0) for grp in groups], axis=1)

    dk_rope, dk_nope = _rowwise(rope_k_bwd, [dk_mla, rope["cos_k64"], rope["sin_k64_t"]], [(128, BF16), (1024, BF16)],
                                name="rope_k_bwd", rows=s)
    grads["w_kv_b"] = jnp.concatenate([_mm(n_kv, dk_nope, "tn", name="g_w_kv_b_k"), _mm(n_kv, dv_a, "tn", name="g_w_kv_b_v")], axis=1)
    dn_kv = _mm(dk_nope, w["w_kv_b"][:, :1024], "nt", name="d_n_kv_k")
    dn_kv = _mm(dv_a, w["w_kv_b"][:, 1024:], "nt", name="d_n_kv_v", add=dn_kv)

    def rms_bwd(c_q, c_kv, dq, dkv, g_q, g_kv):
        res = []
        for c, dn, g in ((c_q, dq, g_q), (c_kv, dkv, g_kv)):
            r = lax.rsqrt(jnp.mean(c * c, axis=1, keepdims=True) + RMS_EPS)
            t = dn * g
            res += [r * t - c * (r * r * r) * jnp.mean(c * t, axis=1, keepdims=True), _colsum(dn * c * r)]
        return res

    dc_q, grads["q_a_gain"], dc_kv, grads["kv_a_gain"] = _rowwise(
        rms_bwd, [(proj, 256, COL_CQ), (proj, 128, COL_CKV), dn_q, dn_kv, w["q_a_gain"], w["kv_a_gain"]],
        [(256, BF16), ("sum", 256), (128, BF16), ("sum", 128)], name="rms_bwd", rows=s)

    sent = emit({n: grads[n] for n in ("w_mem_kv", "w_q_b", "w_kv_b")})

    dproj = jnp.concatenate(
        [dc_q, dc_kv, dk_rope, d_gate["mla"], d_gate["sb"], d_gate["mem"], dq_sb.astype(BF16), dk_sb.astype(BF16),
         dv_sb.astype(BF16), dq_m.astype(BF16)], axis=1)
    grads["w_in"] = _mm(xb, dproj, "tn", name="g_w_in", behind=sent)
    sent = emit({"w_in": grads["w_in"]})
    grad_x = _mm(dproj, w["w_in"], "nt", name="grad_x", add=dx, behind=sent)
    return loss, grad_x, grads


def _shard_shape(shape, axis):
    return tuple(d // N_DEV if a == axis else d for a, d in enumerate(shape))


def _from_blocks(blocks, name):
    shape, axis = SHARDED[name]
    return blocks.reshape(shape) if axis == 0 else blocks.transpose(1, 0, 2).reshape(shape)


def _to_blocks(full, name):
    shape, axis = SHARDED[name]
    shp = _shard_shape(shape, axis)
    return full.reshape(N_DEV, *shp) if axis == 0 else full.reshape(shape[0], N_DEV, shp[1]).transpose(1, 0, 2)


def _pad_heads(a, used):
    rows = a.shape[0]
    a = a.reshape(rows, MLA_HEADS, used)
    return jnp.concatenate([a, jnp.zeros((rows, MLA_HEADS, LANES - used), a.dtype)], axis=2).reshape(rows, MLA_HEADS * LANES)


def _to_kernel_layout(name, full):
    if name == "w_in":
        return jnp.concatenate([jnp.zeros((D_MODEL, IN_PAD), full.dtype) if piece is None else full[:, piece[0]:piece[0] + piece[1]]
                                for piece in IN_PIECES], axis=1)
    if name == "w_q_b":
        return _pad_heads(full, MLA_NOPE + MLA_ROPE)
    if name == "w_kv_b":
        kv = full.reshape(MLA_KV_LORA, MLA_HEADS, MLA_NOPE + MLA_V)
        return jnp.concatenate([_pad_heads(kv[:, :, :MLA_NOPE].reshape(MLA_KV_LORA, -1), MLA_NOPE),
                                kv[:, :, MLA_NOPE:].reshape(MLA_KV_LORA, -1)], axis=1)
    return full


def _from_kernel_layout(name, g):
    if name == "w_in":
        placed, at = [], 0
        for piece in IN_PIECES:
            if piece is not None:
                placed.append((piece[0], g[:, at:at + piece[1]]))
            at += IN_PAD if piece is None else piece[1]
        return jnp.concatenate([cols for _, cols in sorted(placed, key=lambda item: item[0])], axis=1)
    if name == "w_q_b":
        return g.reshape(MLA_Q_LORA, MLA_HEADS, LANES)[:, :, :MLA_NOPE + MLA_ROPE].reshape(MLA_Q_LORA, -1)
    if name == "w_kv_b":
        return jnp.concatenate([g[:, :1024].reshape(MLA_KV_LORA, MLA_HEADS, LANES)[:, :, :MLA_NOPE],
                                g[:, 1024:].reshape(MLA_KV_LORA, MLA_HEADS, MLA_V)], axis=2).reshape(MLA_KV_LORA, -1)
    return g


def _pack_small(vectors, loss=None):
    flat = [v.reshape(-1) for v in vectors]
    flat.append(jnp.zeros((SMALL_ROWS * SMALL_LANES - LOSS_INDEX,), F32) if loss is None else
                jnp.concatenate([loss.reshape(-1)[:1], jnp.zeros((SMALL_ROWS * SMALL_LANES - LOSS_INDEX - 1,), F32)]))
    return jnp.concatenate(flat).reshape(SMALL_ROWS, SMALL_LANES)


def _unpack_small(packed):
    flat, res, off = packed.reshape(-1), [], 0
    for _, n in SMALL:
        res.append(flat[off:off + n].reshape(1, n))
        off += n
    return res


def _me_and_peers():
    x, y, c = lax.axis_index("x"), lax.axis_index("y"), lax.axis_index("c")
    peers = []
    for kk in range(1, N_DEV):
        px, py, pc = (x + (kk >> 2)) % 2, (y + ((kk >> 1) & 1)) % 2, (c + (kk & 1)) % 2
        peers.append(((px, py, pc), 4 * px + 2 * py + pc))
    return 4 * x + 2 * y + c, peers


def _share_small(small, *, name):
    def body(small_ref, all_ref, send_sems, recv_sems, local_sem):
        me, peers = _me_and_peers()
        copies = [pltpu.make_async_remote_copy(src_ref=small_ref, dst_ref=all_ref.at[me], send_sem=send_sems.at[kk], recv_sem=recv_sems.at[kk],
                                               device_id=pos, device_id_type=pl.DeviceIdType.MESH) for kk, (pos, _) in enumerate(peers)]
        copies.append(pltpu.make_async_copy(small_ref, all_ref.at[me], local_sem))
        for cp in copies:
            cp.start()
        for cp in copies:
            cp.wait()

    hbm = pl.BlockSpec(memory_space=pl.ANY)
    return pl.pallas_call(
        body, name=name, in_specs=[hbm], out_specs=hbm, out_shape=jax.ShapeDtypeStruct((N_DEV, *small.shape), small.dtype),
        scratch_shapes=[pltpu.SemaphoreType.DMA((N_DEV - 1,)), pltpu.SemaphoreType.DMA((N_DEV - 1,)), pltpu.SemaphoreType.DMA],
        compiler_params=pltpu.CompilerParams(has_side_effects=True),
    )(small)


_HBM = pl.BlockSpec(memory_space=pltpu.HBM)
_SEM = pl.BlockSpec(memory_space=pltpu.SEMAPHORE)


def _exchange_copies(srcs, zones, send_sems, recv_sems, gather):
    me, peers = _me_and_peers()
    return [pltpu.make_async_remote_copy(
        src_ref=srcs[t] if gather else srcs[t].at[peer], dst_ref=zones[t].at[me], send_sem=send_sems.at[7 * t + kk],
        recv_sem=recv_sems.at[7 * t + kk], device_id=pos, device_id_type=pl.DeviceIdType.MESH)
        for t in range(len(srcs)) for kk, (pos, peer) in enumerate(peers)]


def _exchange_start(tensors, *, gather, name):
    n = len(tensors)
    zones = [lax.empty((N_DEV, *(t.shape if gather else t.shape[1:])), t.dtype) for t in tensors]

    def body(*refs):
        for cp in _exchange_copies(refs[:n], refs[n:2 * n], refs[2 * n], refs[2 * n + 1], gather):
            cp.start()
        refs[-1][...] = jnp.zeros_like(refs[-1])

    buffers = [pltpu.HBM(a.shape, a.dtype) for a in tensors + zones]
    res = pl.pallas_call(
        body, name=name, in_specs=[_HBM] * (2 * n),
        out_shape=(pltpu.SemaphoreType.DMA((7 * n,)), pltpu.SemaphoreType.DMA((7 * n,)), *buffers, jax.ShapeDtypeStruct((8, LANES), F32)),
        out_specs=(_SEM, _SEM, *[_HBM] * (2 * n), pl.BlockSpec(memory_space=pltpu.VMEM)),
        input_output_aliases={i: 2 + i for i in range(2 * n)},
        compiler_params=pltpu.CompilerParams(has_side_effects=pltpu.SideEffectType.DATAFLOW_SIDE_EFFECTING),
    )(*[pltpu.with_memory_space_constraint(a, pltpu.HBM) for a in tensors + zones])
    return dict(sems=res[:2], buffers=res[2:2 + 2 * n], gather=gather, started=res[-1])


def _exchange_wait(started, after, *, name):
    n = len(started["buffers"]) // 2

    def body(*refs):
        for cp in _exchange_copies(refs[:n], refs[n:2 * n], refs[2 * n], refs[2 * n + 1], started["gather"]):
            cp.wait_send()
            cp.wait_recv()

    res = pl.pallas_call(
        body, name=name, in_specs=[_HBM] * (2 * n) + [_SEM, _SEM] + [_ANY] * len(after),
        out_shape=tuple(pltpu.HBM(a.shape, a.dtype) for a in started["buffers"]), out_specs=tuple([_HBM] * (2 * n)),
        input_output_aliases={i: i for i in range(2 * n)},
        compiler_params=pltpu.CompilerParams(has_side_effects=pltpu.SideEffectType.DATAFLOW_SIDE_EFFECTING),
    )(*started["buffers"], *started["sems"], *after)
    return res[:n], res[n:]


def _adamw(contrib, w, m, v, *, name):
    rows, cols = w.shape
    tile = min(rows, ADAM_ROWS)

    def body(c_ref, w_ref, m_ref, v_ref, g_ref, d_ref, nm_ref, nv_ref):
        g = c_ref[0].astype(F32)
        for s in range(1, N_DEV):
            g = g + c_ref[s].astype(F32)
        m_new = ADAM_B1 * m_ref[...] + (1.0 - ADAM_B1) * g
        v_new = ADAM_B2 * v_ref[...] + (1.0 - ADAM_B2) * (g * g)
        m_hat = m_new / (1.0 - ADAM_B1 ** ADAM_STEP)
        v_hat = v_new / (1.0 - ADAM_B2 ** ADAM_STEP)
        g_ref[...] = g
        d_ref[...] = -ADAM_LR * (m_hat / (jnp.sqrt(v_hat) + ADAM_EPS) + ADAM_WD * w_ref[...])
        nm_ref[...] = m_new
        nv_ref[...] = v_new

    spec = pl.BlockSpec((tile, cols), lambda i: (i, 0))
    return pl.pallas_call(
        body, name=name, grid=(rows // tile,),
        in_specs=[pl.BlockSpec((N_DEV, tile, cols), lambda i: (0, i, 0)), spec, spec, spec], out_specs=[spec] * 4,
        out_shape=[jax.ShapeDtypeStruct((rows, cols), F32)] * 4, compiler_params=_cparams("parallel"),
    )(contrib, w, m, v)


class _Weights:
    def __init__(self, gathers, vectors, me):
        self.gathers, self.ready, self.me, self.after = gathers, dict(vectors), me, ()

    def arrive_after(self, *values):
        self.after = values

    def __getitem__(self, name):
        if name not in self.ready:
            gi = next(i for i, group in enumerate(GATHER_GROUPS) if name in group)
            after = [*self.after, *[g["started"] for g in self.gathers]]
            shards, zones = _exchange_wait(self.gathers[gi], after, name=f"gather_wait_{gi}")
            for n, shard, zone in zip(GATHER_GROUPS[gi], shards, zones, strict=True):
                blocks = lax.dynamic_update_slice_in_dim(zone, shard[None], self.me, 0)
                self.ready[n] = _to_kernel_layout(n, _from_blocks(blocks, n))
        return self.ready[name]


def kernel(x, mem, w_in, w_mem_kv, q_a_gain, w_q_b, kv_a_gain, w_kv_b, w_branch_mla, w_branch_sb, w_branch_mem, w_merge_gate, b_merge_gate, w_out, ln_gain, ln_bias, loss_target, m_w_in, m_w_mem_kv, m_q_a_gain, m_w_q_b, m_kv_a_gain, m_w_kv_b, m_w_branch_mla, m_w_branch_sb, m_w_branch_mem, m_w_merge_gate, m_b_merge_gate, m_w_out, m_ln_gain, m_ln_bias, v_w_in, v_w_mem_kv, v_q_a_gain, v_w_q_b, v_kv_a_gain, v_w_kv_b, v_w_branch_mla, v_w_branch_sb, v_w_branch_mem, v_w_merge_gate, v_b_merge_gate, v_w_out, v_ln_gain, v_ln_bias):
    given = dict(locals())
    small_names = [n for n, _ in SMALL]
    smalls = lambda prefix: [given[prefix + n] for n in small_names]
    me = 4 * lax.axis_index("x") + 2 * lax.axis_index("y") + lax.axis_index("c")

    gathers = [_exchange_start([given[n][0].astype(BF16) for n in group], gather=True, name=f"gather_start_{gi}")
               for gi, group in enumerate(GATHER_GROUPS)]
    w = _Weights(gathers, {n: given[n] for n in small_names}, me)
    exchanges = []
    results = [{}, {}, {}, {}]

    def finish(gi, after):
        names, started = exchanges[gi]
        sent, zones = _exchange_wait(started, after, name=f"grads_wait_{gi}")
        done = []
        for n, blocks, zone in zip(names, sent, zones, strict=True):
            own = lax.dynamic_index_in_dim(blocks, me, 0, keepdims=True)
            contrib = lax.dynamic_update_slice_in_dim(zone, own, me, 0)
            outs = _adamw(contrib, given[n][0], given["m_" + n][0], given["v_" + n][0], name=f"adamw_{n}")
            for kind, res in zip(results, outs, strict=True):
                kind[n] = res[None]
            done.append(outs[1])
        return done

    def emit(grads):
        blocks = [_to_blocks(_from_kernel_layout(n, g), n).astype(BF16) for n, g in grads.items()]
        exchanges.append((tuple(grads), _exchange_start(blocks, gather=False, name=f"grads_start_{len(exchanges)}")))
        started = [exchanges[-1][1]["started"]]
        if len(exchanges) == len(GRAD_GROUPS):
            for gi in range(len(GRAD_GROUPS) - 1):
                started += finish(gi, started[:1])
        return started

    loss, grad_x, grads = _local_step(x[0], mem[0], loss_target[0], w, emit)

    contrib_small = _share_small(_pack_small([grads[n] for n in small_names], loss), name="share_small")
    sml = _adamw(contrib_small, _pack_small(smalls("")), _pack_small(smalls("m_")), _pack_small(smalls("v_")), name="adamw_small")
    for kind, packed in zip(results, sml, strict=True):
        kind.update(zip(small_names, _unpack_small(packed), strict=True))
    finish(len(GRAD_GROUPS) - 1, [grad_x])
    order = ["w_in", "w_mem_kv", "q_a_gain", "w_q_b", "kv_a_gain", "w_kv_b", "w_branch_mla", "w_branch_sb", "w_branch_mem",
             "w_merge_gate", "b_merge_gate", "w_out", "ln_gain", "ln_bias"]
    loss_out = sml[0].reshape(-1)[LOSS_INDEX]
    return (loss_out, grad_x[None], *[kind[n] for kind in results for n in order])
```

```python
import math

import jax
import jax.numpy as jnp
from jax import lax
from jax.experimental import pallas as pl
from jax.experimental.pallas import tpu as pltpu

F32, BF16 = jnp.float32, jnp.bfloat16

N_DEV = 8
D_MODEL = 1024
MLA_HEADS, MLA_NOPE, MLA_ROPE, MLA_V = 8, 64, 32, 64
MLA_Q_LORA, MLA_KV_LORA = 256, 128
SB_HEAD_DIM = 64
MEM_HEAD_DIM = 128
ROPE_BASE = 10000.0
RMS_EPS = 1e-6
LN_EPS = 1e-5
DEEPNORM_ALPHA = 2.0 ** 0.25
ADAM_LR, ADAM_B1, ADAM_B2, ADAM_EPS, ADAM_WD, ADAM_STEP = 0.001, 0.9, 0.999, 1e-08, 0.01, 10
LOG2E, LN2 = math.log2(math.e), math.log(2.0)

LANES = 128
GROUPS = 4
PROJ_WIDTH = 4096
COL_CQ, COL_CKV, COL_KROPE, COL_GATE_A, COL_GATE_B, COL_GATE_M = 0, 256, 384, 512, 1024, 1536
QKV_FIRST, QKV_WIDTH = 2048, 2048
COL_QB, COL_KB, COL_VB, COL_QM = 0, 512, 1024, 1536
IN_PIECES = ((0, 416), None, (416, 512), (2464, 512), (3488, 512), (928, 512), (1440, 512), (1952, 512), (2976, 512))
IN_PAD = 96

VMEM_LIMIT_BYTES = 56 * 1024 * 1024
NEG_BIG = -1e30
Q_BLOCK = 512
MEM_Q_BLOCK = 2048
SB_BWD_Q_BLOCK = 512
TRI_BLOCK = 256
TILE_ROWS = 64
KEY_CHUNK = 512

SHARDED = {
    "w_in": ((1024, 4000), 1), "w_mem_kv": ((1024, 1024), 0), "w_q_b": ((256, 768), 1), "w_kv_b": ((128, 1024), 1),
    "w_branch_mla": ((512, 1024), 1), "w_branch_sb": ((512, 1024), 1), "w_branch_mem": ((512, 1024), 1),
    "w_merge_gate": ((1024, 3072), 1), "w_out": ((1024, 1024), 0),
}
GATHER_GROUPS = (("w_in",), ("w_merge_gate",), ("w_q_b", "w_kv_b", "w_mem_kv", "w_branch_mla", "w_branch_sb", "w_branch_mem", "w_out"))
GRAD_GROUPS = (("w_out", "w_merge_gate", "w_branch_mla", "w_branch_sb", "w_branch_mem"), ("w_mem_kv", "w_q_b", "w_kv_b"), ("w_in",))
SMALL = (("q_a_gain", 256), ("kv_a_gain", 128), ("b_merge_gate", 3072), ("ln_gain", 1024), ("ln_bias", 1024))
SMALL_ROWS, SMALL_LANES = 48, 128
ADAM_ROWS = 256
LOSS_INDEX = 5504


def _cparams(*sem):
    return pltpu.CompilerParams(dimension_semantics=sem or None, vmem_limit_bytes=VMEM_LIMIT_BYTES)


_DIMS = {"nn": (((1,), (0,)), ((), ())), "nt": (((1,), (1,)), ((), ())), "tn": (((0,), (0,)), ((), ()))}


def _dot(a, b, dims):
    return lax.dot_general(a, b, _DIMS[dims], preferred_element_type=F32)


def _tile(dim, want):
    if dim <= want:
        return dim
    t = want - want % LANES
    while dim % t:
        t -= LANES
    assert t > 0, (dim, want)
    return t


_ANY = pl.BlockSpec(memory_space=pl.ANY)


def _mm(a, b, dims, *, name, out_dtype=F32, add=None, add_scale=1.0, col_scale=None, b_cols=None, behind=None,
        tm=1024, tn=1024, tk=1024):
    batch = a.shape[0] if a.ndim == 3 else None
    if dims == "nn":
        (m, k), (k2, n) = a.shape[-2:], b.shape[-2:]
    elif dims == "nt":
        (m, k), (n, k2) = a.shape[-2:], b.shape[-2:]
    else:
        (k, m), (k2, n) = a.shape[-2:], b.shape[-2:]
    assert k == k2 and a.ndim == b.ndim, (a.shape, b.shape, dims)
    assert batch is None or (b.shape[0] == batch and add is None and col_scale is None and b_cols is None)
    b_first = 0
    if b_cols is not None:
        assert dims == "nn"
        b_first, n = b_cols
    tm, tn, tk = _tile(m, tm), _tile(n, tn), _tile(k, tk)
    assert b_first % tn == 0
    jb = b_first // tn
    nk = k // tk

    def spec(block, index):
        if batch is None:
            return pl.BlockSpec(block, lambda bb, i, j, kk: index(i, j, kk))
        return pl.BlockSpec((None, *block), lambda bb, i, j, kk: (bb, *index(i, j, kk)))

    a_spec = spec((tk, tm), lambda i, j, kk: (kk, i)) if dims == "tn" else spec((tm, tk), lambda i, j, kk: (i, kk))
    b_spec = spec((tn, tk), lambda i, j, kk: (j, kk)) if dims == "nt" else spec((tk, tn), lambda i, j, kk: (kk, jb + j))
    o_spec = spec((tm, tn), lambda i, j, kk: (i, j))
    behind = [] if behind is None else behind if isinstance(behind, (list, tuple)) else [behind]
    optional = [(add, o_spec), (col_scale, pl.BlockSpec((1, tn), lambda bb, i, j, kk: (0, j))), *[(v, _ANY) for v in behind]]
    present = [(v, spec) for v, spec in optional if v is not None]

    def body(*refs):
        a_ref, b_ref = refs[:2]
        extra = iter(refs[2:2 + len(present)])
        add_ref = next(extra) if add is not None else None
        scale_ref = next(extra) if col_scale is not None else None
        o_ref = refs[2 + len(present)]
        part = _dot(a_ref[...].astype(BF16), b_ref[...].astype(BF16), dims)

        def finish(r):
            if add is not None:
                r = r + add_scale * add_ref[...]
            if col_scale is not None:
                r = r * scale_ref[...]
            o_ref[...] = r.astype(out_dtype)

        if nk == 1:
            finish(part)
            return
        acc = refs[-1]
        kk = pl.program_id(3)

        @pl.when(kk == 0)
        def _():
            acc[...] = part

        @pl.when(kk > 0)
        def _():
            acc[...] += part

        @pl.when(kk == nk - 1)
        def _():
            finish(acc[...])

    return pl.pallas_call(
        body, name=name, grid=(batch or 1, m // tm, n // tn, nk),
        in_specs=[a_spec, b_spec] + [spec for _, spec in present], out_specs=o_spec,
        out_shape=jax.ShapeDtypeStruct((m, n) if batch is None else (batch, m, n), out_dtype),
        scratch_shapes=[pltpu.VMEM((tm, tn), F32)] if nk > 1 else [],
        compiler_params=_cparams("parallel", "parallel", "parallel", "arbitrary"),
    )(a, b, *[v for v, _ in present])


def _rowwise(fn, ins, outs, *, name, rows, tr=512):
    n_in = len(ins)
    tr = min(tr, rows)
    in_specs, args = [], []
    for it in ins:
        arr, w, off = it if isinstance(it, tuple) else (it, it.shape[-1], 0)
        assert off % w == 0
        cb = off // w
        if arr.ndim == 3:
            in_specs.append(pl.BlockSpec((arr.shape[0], tr, w), lambda i, cb=cb: (0, i, cb)))
        elif arr.shape[0] == 1:
            in_specs.append(pl.BlockSpec((1, w), lambda i, cb=cb: (0, cb)))
        else:
            in_specs.append(pl.BlockSpec((tr, w), lambda i, cb=cb: (i, cb)))
        args.append(arr)
    out_shape, out_specs, is_sum = [], [], []
    for out in outs:
        is_sum.append(out[0] == "sum")
        if out[0] == "sum":
            out_shape.append(jax.ShapeDtypeStruct((1, out[1]), F32))
            out_specs.append(pl.BlockSpec((1, out[1]), lambda i: (0, 0)))
        elif len(out) == 3:
            out_shape.append(jax.ShapeDtypeStruct((out[0], rows, out[1]), out[2]))
            out_specs.append(pl.BlockSpec((out[0], tr, out[1]), lambda i: (0, i, 0)))
        else:
            out_shape.append(jax.ShapeDtypeStruct((rows, out[0]), out[1]))
            out_specs.append(pl.BlockSpec((tr, out[0]), lambda i: (i, 0)))

    def body(*refs):
        res = fn(*[r[...] for r in refs[:n_in]])
        for r, val, s in zip(refs[n_in:], res, is_sum, strict=True):
            if s:
                @pl.when(pl.program_id(0) == 0)
                def _(r=r):
                    r[...] = jnp.zeros_like(r)

                r[...] += val
            elif isinstance(val, (list, tuple)):
                for n, part in enumerate(val):
                    r[n] = part.astype(r.dtype)
            else:
                r[...] = val.astype(r.dtype)

    return pl.pallas_call(
        body, name=name, grid=(rows // tr,), in_specs=in_specs, out_specs=out_specs, out_shape=out_shape,
        compiler_params=_cparams("arbitrary"),
    )(*args)


def _colsum(v):
    return jnp.sum(v, axis=0, keepdims=True)


def _sigmoid(v):
    return 1.0 / (1.0 + jnp.exp(-v))


def _lane_groups(v):
    return [v[:, g * LANES:(g + 1) * LANES] for g in range(v.shape[1] // LANES)]


def _swap_halves(v, first_lane):
    lane = lax.broadcasted_iota(jnp.int32, v.shape, 1)
    return jnp.where(lane < first_lane + 16, pltpu.roll(v, 112, axis=1), pltpu.roll(v, 16, axis=1))


def _lane_sum(acc, v):
    for part in _lane_groups(v):
        acc = acc + part
    return acc


def _low_half(shape):
    return lax.broadcasted_iota(jnp.int32, shape, 1) < LANES // 2


def _select_heads(per_head, pick):
    if len(per_head) == 1:
        return pick(per_head[0], 0)
    return jnp.where(_low_half(per_head[0].shape), pick(per_head[0], 0), pick(per_head[1], 1))


def _attn_specs(s, sk, hp, bq, q0, k0, v0):
    wq = hp * LANES
    assert q0 % wq == 0 and k0 % wq == 0 and v0 % LANES == 0
    qb0, kb0, vb0 = q0 // wq, k0 // wq, v0 // LANES
    q_spec = pl.BlockSpec((bq, wq), lambda g, i: (i, qb0 + g))
    k_spec = pl.BlockSpec((sk, wq), lambda g, i: (0, kb0 + g))
    v_spec = pl.BlockSpec((sk, LANES), lambda g, i: (0, vb0 + g))
    row_out = lambda w: pl.BlockSpec((bq, w), lambda g, i: (i, g))
    key_out = lambda w: pl.BlockSpec((sk, w), lambda g, i: (0, g))
    return q_spec, k_spec, v_spec, row_out, key_out


def _chunks(i, bq, ch, sk, causal):
    return ((i + 1) * bq - 1) // ch if causal else jnp.int32(sk // ch - 1)


def _positions(i, c, bq, ch):
    return (i * bq + lax.broadcasted_iota(jnp.int32, (bq, ch), 0), c * ch + lax.broadcasted_iota(jnp.int32, (bq, ch), 1))


def _softmax_fwd(q, k, v, *, hp, causal, name, q0=0, k0=0, v0=0, q_rows=Q_BLOCK):
    s, sk = q.shape[0], k.shape[0]
    bq, ch = min(q_rows, s), min(KEY_CHUNK, sk)
    assert not causal or bq <= ch
    q_spec, k_spec, v_spec, row_out, _ = _attn_specs(s, sk, hp, bq, q0, k0, v0)

    def body(q_ref, k_ref, v_ref, o_ref, lse_ref, s_scr):
        i = pl.program_id(1)
        qs = _lane_groups(q_ref[...])
        last = _chunks(i, bq, ch, sk, causal)

        def scores(c, ms, masked):
            off = pl.multiple_of(c * ch, ch)
            out = []
            for j in range(hp):
                sc = _dot(qs[j], k_ref[pl.ds(off, ch), j * LANES:(j + 1) * LANES], "nt")
                if masked:
                    qpos, kpos = _positions(i, c, bq, ch)
                    sc = jnp.where(kpos <= qpos, sc, NEG_BIG)
                s_scr[j, c] = sc
                m = ms[j]
                for part in _lane_groups(sc):
                    m = jnp.maximum(m, part)
                out.append(m)
            return tuple(out)

        ms = lax.fori_loop(0, last, lambda c, m: scores(c, m, False), tuple(jnp.full((bq, LANES), NEG_BIG, F32) for _ in range(hp)))
        ms = scores(last, ms, causal)
        row_max = [jnp.max(m, axis=1, keepdims=True) for m in ms]

        def weigh(c, carry):
            off = pl.multiple_of(c * ch, ch)
            vt = v_ref[pl.ds(off, ch), :]
            out = []
            for j in range(hp):
                l, acc = carry[j]
                p = jnp.exp2(s_scr[j, c] - row_max[j])
                out.append((_lane_sum(l, p), acc + _dot(p.astype(BF16), vt, "nn")))
            return tuple(out)

        zero = jnp.zeros((bq, LANES), F32)
        res = lax.fori_loop(0, last + 1, weigh, tuple((zero, zero) for _ in range(hp)))
        row_sum = [jnp.sum(l, axis=1, keepdims=True) for l, _ in res]
        o_ref[...] = _select_heads([acc for _, acc in res], lambda acc, j: acc / row_sum[j])
        lse_ref[...] = _select_heads([jnp.broadcast_to(row_max[j] + jnp.log2(row_sum[j]), (bq, LANES)) for j in range(hp)], lambda a, j: a)

    return pl.pallas_call(
        body, name=name, grid=(GROUPS, s // bq), in_specs=[q_spec, k_spec, v_spec], out_specs=[row_out(LANES), row_out(LANES)],
        out_shape=[jax.ShapeDtypeStruct((s, GROUPS * LANES), F32)] * 2,
        scratch_shapes=[pltpu.VMEM((hp, sk // ch, bq, ch), F32)], compiler_params=_cparams("parallel", "arbitrary"),
    )(q, k, v)


def _head_cotangent(do, j, hp):
    if hp == 1:
        return do
    return jnp.where(_low_half(do.shape) == (j == 0), do, 0.0)


def _softmax_bwd(q, k, v, o, do, lse, behind, *, hp, causal, dq_scale, name, q0=0, k0=0, v0=0, q_rows=Q_BLOCK):
    s, sk = q.shape[0], k.shape[0]
    bq, ch = min(q_rows, s), min(KEY_CHUNK, sk)
    assert not causal or bq <= ch
    wq = hp * LANES
    q_spec, k_spec, v_spec, row_out, key_out = _attn_specs(s, sk, hp, bq, q0, k0, v0)

    def body(q_ref, k_ref, v_ref, o_ref, do_ref, lse_ref, _, dq_ref, dk_ref, dv_ref):
        i = pl.program_id(1)

        @pl.when(i == 0)
        def _():
            dk_ref[...] = jnp.zeros_like(dk_ref)
            dv_ref[...] = jnp.zeros_like(dv_ref)

        qs = _lane_groups(q_ref[...])
        do_all, o_all, lse_all = do_ref[...], o_ref[...], lse_ref[...]
        dos, deltas, lses = [], [], []
        for j in range(hp):
            d = _head_cotangent(do_all, j, hp)
            deltas.append(jnp.sum(d * o_all, axis=1, keepdims=True))
            dos.append(d.astype(BF16))
            lses.append(lse_all[:, j * (LANES // hp):j * (LANES // hp) + 1])
        last = _chunks(i, bq, ch, sk, causal)

        def chunk(c, dqs, masked):
            off = pl.multiple_of(c * ch, ch)
            vt = v_ref[pl.ds(off, ch), :]
            out, dks, dv = [], [], None
            for j in range(hp):
                kt = k_ref[pl.ds(off, ch), j * LANES:(j + 1) * LANES]
                p = jnp.exp2(_dot(qs[j], kt, "nt") - lses[j])
                if masked:
                    qpos, kpos = _positions(i, c, bq, ch)
                    p = jnp.where(kpos <= qpos, p, 0.0)
                ds = (p * (_dot(dos[j], vt, "nt") - deltas[j]) * LN2).astype(BF16)
                out.append(dqs[j] + _dot(ds, kt, "nn"))
                dks.append(_dot(ds, qs[j], "tn"))
                dvj = _dot(p.astype(BF16), dos[j], "tn")
                dv = dvj if dv is None else dv + dvj
            dk_ref[pl.ds(off, ch), :] += dks[0] if hp == 1 else jnp.concatenate(dks, axis=1)
            dv_ref[pl.ds(off, ch), :] += dv
            return tuple(out)

        dqs = lax.fori_loop(0, last, lambda c, d: chunk(c, d, False), tuple(jnp.zeros((bq, LANES), F32) for _ in range(hp)))
        dqs = chunk(last, dqs, causal)
        dq_ref[...] = (dqs[0] if hp == 1 else jnp.concatenate(dqs, axis=1)) * dq_scale

    return pl.pallas_call(
        body, name=name, grid=(GROUPS, s // bq),
        in_specs=[q_spec, k_spec, v_spec, row_out(LANES), row_out(LANES), row_out(LANES), _ANY],
        out_specs=[row_out(wq), key_out(wq), key_out(LANES)],
        out_shape=[jax.ShapeDtypeStruct((s, GROUPS * wq), F32), jax.ShapeDtypeStruct((sk, GROUPS * wq), F32),
                   jax.ShapeDtypeStruct((sk, GROUPS * LANES), F32)],
        compiler_params=_cparams("arbitrary", "arbitrary"),
    )(q, k, v, o, do, lse, behind)


def _log2_sigmoid_pair(z2):
    minus_abs = lax.bitcast_convert_type(lax.bitcast_convert_type(z2, jnp.uint32) | jnp.uint32(0x80000000), F32)
    log_beta = jnp.minimum(z2, 0.0) - jnp.log2(1.0 + jnp.exp2(minus_abs))
    return log_beta, log_beta - z2


def _tilewise(fn, *arrays):
    rows, cols = arrays[0].shape
    step = min(TILE_ROWS, rows)
    grid = [[fn(*[None if a is None else a[r:r + step, c:c + LANES] for a in arrays]) for c in range(0, cols, LANES)]
            for r in range(0, rows, step)]
    return [jnp.concatenate([jnp.concatenate([cell[k] for cell in row], axis=1) for row in grid], axis=0)
            for k in range(len(grid[0][0]))]


def _split(v):
    hi = v.astype(BF16)
    return hi, (v - hi.astype(F32)).astype(BF16)


def _tri(n, after):
    rows, cols = lax.broadcasted_iota(jnp.int32, (n, n), 0), lax.broadcasted_iota(jnp.int32, (n, n), 1)
    return (rows > cols if after else rows < cols).astype(BF16)


def _running_sums(v, terms, start, tri, backwards):
    n = tri.shape[0]
    n_blocks = v.shape[1] // n
    order = range(n_blocks - 1, -1, -1) if backwards else range(n_blocks)
    stacked = tri if len(terms) == 1 else jnp.concatenate([tri] * len(terms), axis=0)
    parts, run = [None] * n_blocks, start
    for t in order:
        cols = slice(t * n, (t + 1) * n)
        lhs = terms[0][:, cols] if len(terms) == 1 else jnp.concatenate([term[:, cols] for term in terms], axis=1)
        parts[t] = _dot(lhs, stacked, "nn") + run
        run = run + jnp.sum(v[:, cols], axis=1, keepdims=True)
    return (parts[0] if n_blocks == 1 else jnp.concatenate(parts, axis=1)), run


def _sb_weights(qm, kt, run, tri, strict):
    def logs(z2, keep):
        log_beta, log_keep = _log2_sigmoid_pair(z2)
        if keep is not None:
            log_keep = jnp.where(keep, log_keep, 0.0)
        return log_beta, log_keep, *_split(log_keep)

    log_beta, log_keep, hi, lo = _tilewise(logs, _dot(qm, kt, "nt"), strict)
    behind, run = _running_sums(log_keep, (hi, lo), run, tri, True)

    def weigh(log_beta, behind, keep):
        a = jnp.exp2(log_beta + behind)
        return (a if keep is None else jnp.where(keep, a, 0.0),)

    (a,) = _tilewise(weigh, log_beta, behind, strict)
    return a, log_beta, run


def _sb_queries(q_all):
    low = _low_half(q_all.shape)
    zero = jnp.zeros_like(q_all)
    return [jnp.where(low, q_all, zero), jnp.where(low, zero, q_all)]


def _sb_fwd(qkv, *, q0, k0, v0, name):
    s = qkv.shape[0]
    bq, ch = min(Q_BLOCK, s), min(KEY_CHUNK, s)
    q_spec, k_spec, v_spec, row_out, _ = _attn_specs(s, s, 1, bq, q0, k0, v0)

    def body(q_ref, k_ref, v_ref, o_ref):
        i = pl.program_id(1)
        qms = _sb_queries(q_ref[...])
        tri = _tri(min(TRI_BLOCK, ch), True)
        last = _chunks(i, bq, ch, s, True)

        def chunk(c, carry, masked):
            off = pl.multiple_of(c * ch, ch)
            kt, vt = k_ref[pl.ds(off, ch), :], v_ref[pl.ds(off, ch), :]
            strict = None
            if masked:
                qpos, kpos = _positions(i, c, bq, ch)
                strict = kpos < qpos
            out = []
            for j in range(2):
                run, acc = carry[j]
                a, _, run = _sb_weights(qms[j], kt, run, tri, strict)
                out.append((run, acc + _dot(a.astype(BF16), vt, "nn")))
            return tuple(out)

        carry = chunk(last, tuple((jnp.zeros((bq, 1), F32), jnp.zeros((bq, LANES), F32)) for _ in range(2)), True)
        res = lax.fori_loop(0, last, lambda n, c: chunk(last - 1 - n, c, False), carry)
        o_ref[...] = _select_heads([acc for _, acc in res], lambda acc, j: acc)

    return pl.pallas_call(
        body, name=name, grid=(GROUPS, s // bq), in_specs=[q_spec, k_spec, v_spec], out_specs=row_out(LANES),
        out_shape=jax.ShapeDtypeStruct((s, GROUPS * LANES), F32), compiler_params=_cparams("parallel", "arbitrary"),
    )(qkv, qkv, qkv)


def _sb_bwd(qkv, do, behind, *, q0, k0, v0, dq_scale, name):
    s = qkv.shape[0]
    bq, ch = min(SB_BWD_Q_BLOCK, s), min(KEY_CHUNK, s)
    q_spec, k_spec, v_spec, row_out, key_out = _attn_specs(s, s, 1, bq, q0, k0, v0)

    def body(q_ref, k_ref, v_ref, do_ref, _, dq_ref, dk_ref, dv_ref, g_s, beta_s):
        i = pl.program_id(1)

        @pl.when(i == 0)
        def _():
            dk_ref[...] = jnp.zeros_like(dk_ref)
            dv_ref[...] = jnp.zeros_like(dv_ref)

        qms = _sb_queries(q_ref[...])
        do_all = do_ref[...]
        dos = [_head_cotangent(do_all, j, 2).astype(BF16) for j in range(2)]
        dos_ln2 = [(_head_cotangent(do_all, j, 2) * LN2).astype(BF16) for j in range(2)]
        tri_after, tri_before = _tri(min(TRI_BLOCK, ch), True), _tri(min(TRI_BLOCK, ch), False)
        last = _chunks(i, bq, ch, s, True)

        def strict_mask(c):
            qpos, kpos = _positions(i, c, bq, ch)
            return kpos < qpos

        def sweep1(c, runs, masked):
            off = pl.multiple_of(c * ch, ch)
            kt, vt = k_ref[pl.ds(off, ch), :], v_ref[pl.ds(off, ch), :]
            strict = strict_mask(c) if masked else None
            out, dv = [], None
            for j in range(2):
                a, log_beta, run = _sb_weights(qms[j], kt, runs[j], tri_after, strict)
                g_s[j, c] = (a * _dot(dos_ln2[j], vt, "nt")).astype(BF16)
                beta_s[j, c] = jnp.exp2(log_beta).astype(BF16)
                dvj = _dot(a.astype(BF16), dos[j], "tn")
                dv = dvj if dv is None else dv + dvj
                out.append(run)
            dv_ref[pl.ds(off, ch), :] += dv
            return tuple(out)

        runs = sweep1(last, tuple(jnp.zeros((bq, 1), F32) for _ in range(2)), True)
        lax.fori_loop(0, last, lambda n, r: sweep1(last - 1 - n, r, False), runs)

        def sweep2(c, carry, masked):
            off = pl.multiple_of(c * ch, ch)
            kt = k_ref[pl.ds(off, ch), :]
            out, dk = [], None
            for j in range(2):
                before, dq = carry[j]
                g16, beta = g_s[j, c], beta_s[j, c].astype(F32)
                g = g16.astype(F32)
                in_front, before = _running_sums(g, (g16,), before, tri_before, False)
                dz = g * (1.0 - beta) - beta * in_front
                if masked:
                    dz = jnp.where(strict_mask(c), dz, 0.0)
                dz = dz.astype(BF16)
                dkj = _dot(dz, qms[j], "tn")
                dk = dkj if dk is None else dk + dkj
                out.append((before, dq + _dot(dz, kt, "nn")))
            dk_ref[pl.ds(off, ch), :] += dk
            return tuple(out)

        carry = lax.fori_loop(0, last, lambda c, cr: sweep2(c, cr, False),
                              tuple((jnp.zeros((bq, 1), F32), jnp.zeros((bq, LANES), F32)) for _ in range(2)))
        res = sweep2(last, carry, True)
        dq_ref[...] = _select_heads([dq for _, dq in res], lambda dq, j: dq) * dq_scale

    n_ch = s // ch
    return pl.pallas_call(
        body, name=name, grid=(GROUPS, s // bq), in_specs=[q_spec, k_spec, v_spec, row_out(LANES), _ANY],
        out_specs=[row_out(LANES), key_out(LANES), key_out(LANES)],
        out_shape=[jax.ShapeDtypeStruct((s, GROUPS * LANES), F32)] * 3,
        scratch_shapes=[pltpu.VMEM((2, n_ch, bq, ch), BF16)] * 2,
        compiler_params=_cparams("arbitrary", "arbitrary"),
    )(qkv, qkv, qkv, do, behind)


def _rope_tables(s):
    half = MLA_ROPE // 2
    freqs = ROPE_BASE ** (-jnp.arange(half, dtype=F32) / half)
    ang = jnp.arange(s, dtype=F32)[:, None] * freqs[None, :]
    cos, sin = jnp.cos(ang), jnp.sin(ang)
    tail = jnp.zeros((s, LANES - MLA_NOPE - MLA_ROPE), F32)
    lead = lambda fill: jnp.full((s, MLA_NOPE), fill, F32)
    return dict(
        cos_k0=jnp.concatenate([cos, cos, lead(0.0), tail], axis=1), sin_k0=jnp.concatenate([-sin, sin, lead(0.0), tail], axis=1),
        cos_k64=jnp.concatenate([lead(0.0), cos, cos, tail], axis=1), sin_k64=jnp.concatenate([lead(0.0), -sin, sin, tail], axis=1),
        cos_q=jnp.concatenate([lead(1.0), cos, cos, tail], axis=1),
        sin_k64_t=jnp.concatenate([lead(0.0), sin, -sin, tail], axis=1),
    )


def _local_step(x, mem, target, w, emit=lambda grads: [jnp.zeros((8, LANES), F32)]):
    s = x.shape[0]
    rope = _rope_tables(s)
    xb = x.astype(BF16)
    inv_d = 1.0 / D_MODEL
    scale_a = LOG2E / math.sqrt(MLA_NOPE + MLA_ROPE)
    scale_b = LOG2E / math.sqrt(SB_HEAD_DIM)
    scale_m = LOG2E / math.sqrt(MEM_HEAD_DIM)
    arrive_after = getattr(w, "arrive_after", lambda *values: None)
    memb = mem.astype(BF16)

    arrive_after(xb, memb, *rope.values())
    proj = _mm(xb, w["w_in"], "nn", name="proj", b_cols=(0, QKV_FIRST))
    one = jnp.ones((1, 512), F32)
    qkv = _mm(xb, w["w_in"], "nn", name="proj_qkv", b_cols=(QKV_FIRST, QKV_WIDTH), out_dtype=BF16,
              col_scale=jnp.concatenate([one * scale_b, one, one, one * scale_m], axis=1))
    arrive_after(qkv)
    pre = _mm(xb, w["w_merge_gate"], "nn", name="merge_pre", out_dtype=BF16)
    arrive_after(pre)

    def rms_pair(c_q, c_kv, g_q, g_kv):
        return (c_q * lax.rsqrt(jnp.mean(c_q * c_q, axis=1, keepdims=True) + RMS_EPS) * g_q,
                c_kv * lax.rsqrt(jnp.mean(c_kv * c_kv, axis=1, keepdims=True) + RMS_EPS) * g_kv)

    n_q, n_kv = _rowwise(rms_pair, [(proj, 256, COL_CQ), (proj, 128, COL_CKV), w["q_a_gain"], w["kv_a_gain"]],
                         [(256, BF16), (128, BF16)], name="rms_pair", rows=s)
    q_a = _mm(n_q, w["w_q_b"], "nn", name="q_up")
    kv_a = _mm(n_kv, w["w_kv_b"], "nn", name="kv_up", out_dtype=BF16)

    def rope_q(qa, cos, sin):
        return (jnp.concatenate([(g * cos + _swap_halves(g, MLA_NOPE) * sin) * scale_a for g in _lane_groups(qa)], axis=1),)

    (q_mla,) = _rowwise(rope_q, [q_a, rope["cos_q"], rope["sin_k64"]], [(1024, BF16)], name="rope_q", rows=s)

    def rope_k(k_nope, k_rope, cos, sin):
        k_pe = pltpu.roll(k_rope * cos + _swap_halves(k_rope, 0) * sin, MLA_NOPE, axis=1).astype(BF16)
        return (jnp.concatenate([g + k_pe for g in _lane_groups(k_nope)], axis=1),)

    (k_mla,) = _rowwise(rope_k, [(kv_a, 1024, 0), (proj, 128, COL_KROPE), rope["cos_k0"], rope["sin_k0"]],
                        [(1024, BF16)], name="rope_k", rows=s)
    o_a, lse_a = _softmax_fwd(q_mla, k_mla, kv_a, hp=2, causal=True, name="mla_fwd", v0=1024)

    o_b = _sb_fwd(qkv, q0=COL_QB, k0=COL_KB, v0=COL_VB, name="sb_fwd")

    mem_kv =_mm(memb, w["w_mem_kv"], "nn", name="mem_kv", out_dtype=BF16)
    o_m, lse_m = _softmax_fwd(qkv, mem_kv, mem_kv, hp=1, causal=False, name="mem_fwd", q0=1536, v0=512, q_rows=MEM_Q_BLOCK)

    o_br = {"mla": o_a, "sb": o_b, "mem": o_m}
    gate_col = {"mla": COL_GATE_A, "sb": COL_GATE_B, "mem": COL_GATE_M}

    branches = ("mla", "sb", "mem")
    branch_ins = [o_br[br] for br in branches] + [(proj, 512, gate_col[br]) for br in branches]
    w_branch = jnp.stack([w[f"w_branch_{br}"] for br in branches])

    def gated(oa, ob, om, ga, gb, gm):
        return ([o * gate * _sigmoid(gate) for o, gate in ((oa, ga), (ob, gb), (om, gm))],)

    (u,) = _rowwise(gated, branch_ins, [(3, 512, BF16)], name="gated", rows=s)
    y = _mm(u, w_branch, "nn", name="branch", out_dtype=BF16)

    def merge(pa, pb, pm, ba, bb, bm, ys):
        return (sum(_sigmoid(p.astype(F32) + b) * ys[n].astype(F32) for n, (p, b) in enumerate(((pa, ba), (pb, bb), (pm, bm)))),)

    bias = w["b_merge_gate"]
    gate_ins = [(pre, 1024, 0), (pre, 1024, 1024), (pre, 1024, 2048), (bias, 1024, 0), (bias, 1024, 1024), (bias, 1024, 2048)]
    (merged,) = _rowwise(merge, gate_ins + [y], [(1024, BF16)], name="merge", rows=s)
    out = _mm(merged, w["w_out"], "nn", name="out_proj")

    def norm_loss(xv, ov, tv, gain, bias_ln):
        z = DEEPNORM_ALPHA * xv + ov
        zc = z - jnp.mean(z, axis=1, keepdims=True)
        rstd = lax.rsqrt(jnp.mean(zc * zc, axis=1, keepdims=True) + LN_EPS)
        xhat = zc * rstd
        err = xhat * gain + bias_ln - tv
        loss = 0.5 * jnp.sum(jnp.mean(err * err, axis=1, keepdims=True), axis=0, keepdims=True)
        dy = err * inv_d
        dxhat = dy * gain
        dz = rstd * (dxhat - jnp.mean(dxhat, axis=1, keepdims=True) - xhat * jnp.mean(dxhat * xhat, axis=1, keepdims=True))
        return dz, dz, _colsum(dy * xhat), _colsum(dy), jnp.broadcast_to(loss, (1, LANES))

    dz, dzb, g_ln_gain, g_ln_bias, loss = _rowwise(
        norm_loss, [x, out, target, w["ln_gain"], w["ln_bias"]],
        [(1024, F32), (1024, BF16), ("sum", 1024), ("sum", 1024), ("sum", LANES)], name="norm_loss", rows=s)

    grads = {"ln_gain": g_ln_gain, "ln_bias": g_ln_bias}
    dmerged = _mm(dzb, w["w_out"], "nt", name="d_merged")
    grads["w_out"] = _mm(merged, dzb, "tn", name="g_w_out", out_dtype=BF16)

    def merge_bwd(dm, pa, pb, pm, ba, bb, bm, ys):
        dys, dpre = [], []
        for n, (p, b) in enumerate(((pa, ba), (pb, bb), (pm, bm))):
            g = _sigmoid(p.astype(F32) + b)
            dpre.append(dm * ys[n].astype(F32) * g * (1.0 - g))
            dys.append(dm * g)
        dpre = jnp.concatenate(dpre, axis=1)
        return dpre, _colsum(dpre), dys

    dpre, grads["b_merge_gate"], dy = _rowwise(
        merge_bwd, [dmerged] + gate_ins + [y], [(3072, BF16), ("sum", 3072), (3, 1024, BF16)], name="merge_bwd", rows=s, tr=256)
    grads["w_merge_gate"] = _mm(xb, dpre, "tn", name="g_w_merge", out_dtype=BF16)
    dx = _mm(dpre, w["w_merge_gate"], "nt", name="dx_merge", add=dz, add_scale=DEEPNORM_ALPHA)
    g_w_branch = _mm(u, dy, "tn", name="g_w_branch", out_dtype=BF16)
    du = _mm(dy, w_branch, "nt", name="d_u")
    for n, br in enumerate(branches):
        grads[f"w_branch_{br}"] = g_w_branch[n]

    def gated_bwd(dus, oa, ob, om, ga, gb, gm):
        d_os, d_gates = [], []
        for n, (o, gate) in enumerate(((oa, ga), (ob, gb), (om, gm))):
            sg = _sigmoid(gate)
            d_os.append(dus[n] * gate * sg)
            d_gates.append(dus[n] * o * sg * (1.0 + gate * (1.0 - sg)))
        return *d_os, *d_gates

    res = _rowwise(gated_bwd, [du] + branch_ins, [(512, F32)] * 3 + [(512, BF16)] * 3, name="gated_bwd", rows=s)
    d_o, d_gate = dict(zip(branches, res[:3], strict=True)), dict(zip(branches, res[3:], strict=True))
    (sent,) = emit({n: grads[n] for n in ("w_out", "w_merge_gate", "w_branch_mla", "w_branch_sb", "w_branch_mem")})

    dq_m, dk_m, dv_m = _softmax_bwd(qkv, mem_kv, mem_kv, o_m, d_o["mem"], lse_m, sent, hp=1, causal=False, dq_scale=scale_m,
                                    name="mem_bwd", q0=1536, v0=512, q_rows=MEM_Q_BLOCK)
    grads["w_mem_kv"] = _mm(memb, jnp.concatenate([dk_m, dv_m], axis=1), "tn", name="g_w_mem_kv", out_dtype=BF16)

    dq_sb, dk_sb, dv_sb = _sb_bwd(qkv, d_o["sb"], sent, q0=0, k0=512, v0=1024, dq_scale=scale_b, name="sb_bwd")

    dq_mla, dk_mla, dv_a = _softmax_bwd(q_mla, k_mla, kv_a, o_a, d_o["mla"], lse_a, sent, hp=2, causal=True, dq_scale=scale_a,
                                        name="mla_bwd", v0=1024)

    def rope_q_bwd(dq, cos, sin):
        return (jnp.concatenate([g * cos + _swap_halves(g, MLA_NOPE) * sin for g in _lane_groups(dq)], axis=1),)

    (dq_a,) = _rowwise(rope_q_bwd, [dq_mla, rope["cos_q"], rope["sin_k64_t"]], [(1024, BF16)], name="rope_q_bwd", rows=s)
    grads["w_q_b"] = _mm(n_q, dq_a, "tn", name="g_w_q_b", out_dtype=BF16)
    dn_q = _mm(dq_a, w["w_q_b"], "nt", name="d_n_q")

    def rope_k_bwd(dk, cos, sin):
        groups = _lane_groups(dk)
        g = groups[0]
        for other in groups[1:]:
            g = g + other
        d_rope = pltpu.roll(g * cos + _swap_halves(g, MLA_NOPE) * sin, MLA_NOPE, axis=1)
        nope = _low_half(g.shape)
        return d_rope, jnp.concatenate([jnp.where(nope, grp, 0.0) for grp in groups], axis=1)

    dk_rope, dk_nope = _rowwise(rope_k_bwd, [dk_mla, rope["cos_k64"], rope["sin_k64_t"]], [(128, BF16), (1024, BF16)],
                                name="rope_k_bwd", rows=s)
    grads["w_kv_b"] = jnp.concatenate([_mm(n_kv, dk_nope, "tn", name="g_w_kv_b_k", out_dtype=BF16),
                                       _mm(n_kv, dv_a, "tn", name="g_w_kv_b_v", out_dtype=BF16)], axis=1)
    dn_kv = _mm(dk_nope, w["w_kv_b"][:, :1024], "nt", name="d_n_kv_k")
    dn_kv = _mm(dv_a, w["w_kv_b"][:, 1024:], "nt", name="d_n_kv_v", add=dn_kv)

    def rms_bwd(c_q, c_kv, dq, dkv, g_q, g_kv):
        res = []
        for c, dn, g in ((c_q, dq, g_q), (c_kv, dkv, g_kv)):
            r = lax.rsqrt(jnp.mean(c * c, axis=1, keepdims=True) + RMS_EPS)
            t = dn * g
            res += [r * t - c * (r * r * r) * jnp.mean(c * t, axis=1, keepdims=True), _colsum(dn * c * r)]
        return res

    dc_q, grads["q_a_gain"], dc_kv, grads["kv_a_gain"] = _rowwise(
        rms_bwd, [(proj, 256, COL_CQ), (proj, 128, COL_CKV), dn_q, dn_kv, w["q_a_gain"], w["kv_a_gain"]],
        [(256, BF16), ("sum", 256), (128, BF16), ("sum", 128)], name="rms_bwd", rows=s)

    sent = emit({n: grads[n] for n in ("w_mem_kv", "w_q_b", "w_kv_b")})

    dproj = jnp.concatenate(
        [dc_q, dc_kv, dk_rope, d_gate["mla"], d_gate["sb"], d_gate["mem"], dq_sb.astype(BF16), dk_sb.astype(BF16),
         dv_sb.astype(BF16), dq_m.astype(BF16)], axis=1)
    grads["w_in"] = _mm(xb, dproj, "tn", name="g_w_in", out_dtype=BF16, behind=sent)
    sent = emit({"w_in": grads["w_in"]})
    grad_x = _mm(dproj, w["w_in"], "nt", name="grad_x", add=dx, behind=sent)
    return loss, grad_x, grads


def _shard_shape(shape, axis):
    return tuple(d // N_DEV if a == axis else d for a, d in enumerate(shape))


def _from_blocks(blocks, name):
    shape, axis = SHARDED[name]
    return blocks.reshape(shape) if axis == 0 else blocks.transpose(1, 0, 2).reshape(shape)


def _to_blocks(full, name):
    shape, axis = SHARDED[name]
    shp = _shard_shape(shape, axis)
    return full.reshape(N_DEV, *shp) if axis == 0 else full.reshape(shape[0], N_DEV, shp[1]).transpose(1, 0, 2)


def _pad_heads(a, used):
    rows = a.shape[0]
    a = a.reshape(rows, MLA_HEADS, used)
    return jnp.concatenate([a, jnp.zeros((rows, MLA_HEADS, LANES - used), a.dtype)], axis=2).reshape(rows, MLA_HEADS * LANES)


def _to_kernel_layout(name, full):
    if name == "w_in":
        return jnp.concatenate([jnp.zeros((D_MODEL, IN_PAD), full.dtype) if piece is None else full[:, piece[0]:piece[0] + piece[1]]
                                for piece in IN_PIECES], axis=1)
    if name == "w_q_b":
        return _pad_heads(full, MLA_NOPE + MLA_ROPE)
    if name == "w_kv_b":
        kv = full.reshape(MLA_KV_LORA, MLA_HEADS, MLA_NOPE + MLA_V)
        return jnp.concatenate([_pad_heads(kv[:, :, :MLA_NOPE].reshape(MLA_KV_LORA, -1), MLA_NOPE),
                                kv[:, :, MLA_NOPE:].reshape(MLA_KV_LORA, -1)], axis=1)
    return full


def _from_kernel_layout(name, g):
    if name == "w_in":
        placed, at = [], 0
        for piece in IN_PIECES:
            if piece is not None:
                placed.append((piece[0], g[:, at:at + piece[1]]))
            at += IN_PAD if piece is None else piece[1]
        return jnp.concatenate([cols for _, cols in sorted(placed, key=lambda item: item[0])], axis=1)
    if name == "w_q_b":
        return g.reshape(MLA_Q_LORA, MLA_HEADS, LANES)[:, :, :MLA_NOPE + MLA_ROPE].reshape(MLA_Q_LORA, -1)
    if name == "w_kv_b":
        return jnp.concatenate([g[:, :1024].reshape(MLA_KV_LORA, MLA_HEADS, LANES)[:, :, :MLA_NOPE],
                                g[:, 1024:].reshape(MLA_KV_LORA, MLA_HEADS, MLA_V)], axis=2).reshape(MLA_KV_LORA, -1)
    return g


def _pack_small(vectors, loss=None):
    flat = [v.reshape(-1) for v in vectors]
    flat.append(jnp.zeros((SMALL_ROWS * SMALL_LANES - LOSS_INDEX,), F32) if loss is None else
                jnp.concatenate([loss.reshape(-1)[:1], jnp.zeros((SMALL_ROWS * SMALL_LANES - LOSS_INDEX - 1,), F32)]))
    return jnp.concatenate(flat).reshape(SMALL_ROWS, SMALL_LANES)


def _unpack_small(packed):
    flat, res, off = packed.reshape(-1), [], 0
    for _, n in SMALL:
        res.append(flat[off:off + n].reshape(1, n))
        off += n
    return res


def _me_and_peers():
    x, y, c = lax.axis_index("x"), lax.axis_index("y"), lax.axis_index("c")
    peers = []
    for kk in range(1, N_DEV):
        px, py, pc = (x + (kk >> 2)) % 2, (y + ((kk >> 1) & 1)) % 2, (c + (kk & 1)) % 2
        peers.append(((px, py, pc), 4 * px + 2 * py + pc))
    return 4 * x + 2 * y + c, peers


def _share_small(small, *, name):
    def body(small_ref, all_ref, send_sems, recv_sems, local_sem):
        me, peers = _me_and_peers()
        copies = [pltpu.make_async_remote_copy(src_ref=small_ref, dst_ref=all_ref.at[me], send_sem=send_sems.at[kk], recv_sem=recv_sems.at[kk],
                                               device_id=pos, device_id_type=pl.DeviceIdType.MESH) for kk, (pos, _) in enumerate(peers)]
        copies.append(pltpu.make_async_copy(small_ref, all_ref.at[me], local_sem))
        for cp in copies:
            cp.start()
        for cp in copies:
            cp.wait()

    hbm = pl.BlockSpec(memory_space=pl.ANY)
    return pl.pallas_call(
        body, name=name, in_specs=[hbm], out_specs=hbm, out_shape=jax.ShapeDtypeStruct((N_DEV, *small.shape), small.dtype),
        scratch_shapes=[pltpu.SemaphoreType.DMA((N_DEV - 1,)), pltpu.SemaphoreType.DMA((N_DEV - 1,)), pltpu.SemaphoreType.DMA],
        compiler_params=pltpu.CompilerParams(has_side_effects=True),
    )(small)


_HBM = pl.BlockSpec(memory_space=pltpu.HBM)
_SEM = pl.BlockSpec(memory_space=pltpu.SEMAPHORE)


def _exchange_copies(srcs, zones, send_sems, recv_sems, gather):
    me, peers = _me_and_peers()
    return [pltpu.make_async_remote_copy(
        src_ref=srcs[t] if gather else srcs[t].at[peer], dst_ref=zones[t].at[me], send_sem=send_sems.at[7 * t + kk],
        recv_sem=recv_sems.at[7 * t + kk], device_id=pos, device_id_type=pl.DeviceIdType.MESH)
        for t in range(len(srcs)) for kk, (pos, peer) in enumerate(peers)]


def _exchange_start(tensors, *, gather, name):
    n = len(tensors)
    zones = [lax.empty((N_DEV, *(t.shape if gather else t.shape[1:])), t.dtype) for t in tensors]

    def body(*refs):
        for cp in _exchange_copies(refs[:n], refs[n:2 * n], refs[2 * n], refs[2 * n + 1], gather):
            cp.start()
        refs[-1][...] = jnp.zeros_like(refs[-1])

    buffers = [pltpu.HBM(a.shape, a.dtype) for a in tensors + zones]
    res = pl.pallas_call(
        body, name=name, in_specs=[_HBM] * (2 * n),
        out_shape=(pltpu.SemaphoreType.DMA((7 * n,)), pltpu.SemaphoreType.DMA((7 * n,)), *buffers, jax.ShapeDtypeStruct((8, LANES), F32)),
        out_specs=(_SEM, _SEM, *[_HBM] * (2 * n), pl.BlockSpec(memory_space=pltpu.VMEM)),
        input_output_aliases={i: 2 + i for i in range(2 * n)},
        compiler_params=pltpu.CompilerParams(has_side_effects=pltpu.SideEffectType.DATAFLOW_SIDE_EFFECTING),
    )(*[pltpu.with_memory_space_constraint(a, pltpu.HBM) for a in tensors + zones])
    return dict(sems=res[:2], buffers=res[2:2 + 2 * n], gather=gather, started=res[-1])


def _exchange_wait(started, after, *, name):
    n = len(started["buffers"]) // 2

    def body(*refs):
        for cp in _exchange_copies(refs[:n], refs[n:2 * n], refs[2 * n], refs[2 * n + 1], started["gather"]):
            cp.wait_send()
            cp.wait_recv()

    res = pl.pallas_call(
        body, name=name, in_specs=[_HBM] * (2 * n) + [_SEM, _SEM] + [_ANY] * len(after),
        out_shape=tuple(pltpu.HBM(a.shape, a.dtype) for a in started["buffers"]), out_specs=tuple([_HBM] * (2 * n)),
        input_output_aliases={i: i for i in range(2 * n)},
        compiler_params=pltpu.CompilerParams(has_side_effects=pltpu.SideEffectType.DATAFLOW_SIDE_EFFECTING),
    )(*started["buffers"], *started["sems"], *after)
    return res[:n], res[n:]


def _adamw(contrib, w, m, v, *, name):
    rows, cols = w.shape
    tile = min(rows, ADAM_ROWS)

    def body(c_ref, w_ref, m_ref, v_ref, g_ref, d_ref, nm_ref, nv_ref):
        g = c_ref[0].astype(F32)
        for s in range(1, N_DEV):
            g = g + c_ref[s].astype(F32)
        m_new = ADAM_B1 * m_ref[...] + (1.0 - ADAM_B1) * g
        v_new = ADAM_B2 * v_ref[...] + (1.0 - ADAM_B2) * (g * g)
        m_hat = m_new / (1.0 - ADAM_B1 ** ADAM_STEP)
        v_hat = v_new / (1.0 - ADAM_B2 ** ADAM_STEP)
        g_ref[...] = g
        d_ref[...] = -ADAM_LR * (m_hat / (jnp.sqrt(v_hat) + ADAM_EPS) + ADAM_WD * w_ref[...])
        nm_ref[...] = m_new
        nv_ref[...] = v_new

    spec = pl.BlockSpec((tile, cols), lambda i: (i, 0))
    return pl.pallas_call(
        body, name=name, grid=(rows // tile,),
        in_specs=[pl.BlockSpec((N_DEV, tile, cols), lambda i: (0, i, 0)), spec, spec, spec], out_specs=[spec] * 4,
        out_shape=[jax.ShapeDtypeStruct((rows, cols), F32)] * 4, compiler_params=_cparams("parallel"),
    )(contrib, w, m, v)


class _Weights:
    def __init__(self, gathers, vectors, me):
        self.gathers, self.ready, self.me, self.after = gathers, dict(vectors), me, ()

    def arrive_after(self, *values):
        self.after = values

    def __getitem__(self, name):
        if name not in self.ready:
            gi = next(i for i, group in enumerate(GATHER_GROUPS) if name in group)
            after = [*self.after, *[g["started"] for g in self.gathers]]
            shards, zones = _exchange_wait(self.gathers[gi], after, name=f"gather_wait_{gi}")
            for n, shard, zone in zip(GATHER_GROUPS[gi], shards, zones, strict=True):
                blocks = lax.dynamic_update_slice_in_dim(zone, shard[None], self.me, 0)
                self.ready[n] = _to_kernel_layout(n, _from_blocks(blocks, n))
        return self.ready[name]


def kernel(x, mem, w_in, w_mem_kv, q_a_gain, w_q_b, kv_a_gain, w_kv_b, w_branch_mla, w_branch_sb, w_branch_mem, w_merge_gate, b_merge_gate, w_out, ln_gain, ln_bias, loss_target, m_w_in, m_w_mem_kv, m_q_a_gain, m_w_q_b, m_kv_a_gain, m_w_kv_b, m_w_branch_mla, m_w_branch_sb, m_w_branch_mem, m_w_merge_gate, m_b_merge_gate, m_w_out, m_ln_gain, m_ln_bias, v_w_in, v_w_mem_kv, v_q_a_gain, v_w_q_b, v_kv_a_gain, v_w_kv_b, v_w_branch_mla, v_w_branch_sb, v_w_branch_mem, v_w_merge_gate, v_b_merge_gate, v_w_out, v_ln_gain, v_ln_bias):
    given = dict(locals())
    small_names = [n for n, _ in SMALL]
    smalls = lambda prefix: [given[prefix + n] for n in small_names]
    me = 4 * lax.axis_index("x") + 2 * lax.axis_index("y") + lax.axis_index("c")

    gathers = [_exchange_start([given[n][0].astype(BF16) for n in group], gather=True, name=f"gather_start_{gi}")
               for gi, group in enumerate(GATHER_GROUPS)]
    w = _Weights(gathers, {n: given[n] for n in small_names}, me)
    exchanges = []
    results = [{}, {}, {}, {}]

    def finish(gi, after):
        names, started = exchanges[gi]
        sent, zones = _exchange_wait(started, after, name=f"grads_wait_{gi}")
        done = []
        for n, blocks, zone in zip(names, sent, zones, strict=True):
            own = lax.dynamic_index_in_dim(blocks, me, 0, keepdims=True)
            contrib = lax.dynamic_update_slice_in_dim(zone, own, me, 0)
            outs = _adamw(contrib, given[n][0], given["m_" + n][0], given["v_" + n][0], name=f"adamw_{n}")
            for kind, res in zip(results, outs, strict=True):
                kind[n] = res[None]
            done.append(outs[1])
        return done

    def emit(grads):
        blocks = [_to_blocks(_from_kernel_layout(n, g), n).astype(BF16) for n, g in grads.items()]
        exchanges.append((tuple(grads), _exchange_start(blocks, gather=False, name=f"grads_start_{len(exchanges)}")))
        started = [exchanges[-1][1]["started"]]
        if len(exchanges) == len(GRAD_GROUPS):
            for gi in range(len(GRAD_GROUPS) - 1):
                started += finish(gi, started[:1])
        return started

    loss, grad_x, grads = _local_step(x[0], mem[0], loss_target[0], w, emit)

    contrib_small = _share_small(_pack_small([grads[n] for n in small_names], loss), name="share_small")
    sml = _adamw(contrib_small, _pack_small(smalls("")), _pack_small(smalls("m_")), _pack_small(smalls("v_")), name="adamw_small")
    for kind, packed in zip(results, sml, strict=True):
        kind.update(zip(small_names, _unpack_small(packed), strict=True))
    finish(len(GRAD_GROUPS) - 1, [grad_x])
    order = ["w_in", "w_mem_kv", "q_a_gain", "w_q_b", "kv_a_gain", "w_kv_b", "w_branch_mla", "w_branch_sb", "w_branch_mem",
             "w_merge_gate", "b_merge_gate", "w_out", "ln_gain", "ln_bias"]
    loss_out = sml[0].reshape(-1)[LOSS_INDEX]
    return (loss_out, grad_x[None], *[kind[n] for kind in results for n in order])
```

```python
import math

import jax
import jax.numpy as jnp
from jax import lax
from jax.experimental import pallas as pl
from jax.experimental.pallas import tpu as pltpu

F32, BF16 = jnp.float32, jnp.bfloat16

N_DEV = 8
D_MODEL = 1024
MLA_HEADS, MLA_NOPE, MLA_ROPE, MLA_V = 8, 64, 32, 64
MLA_Q_LORA, MLA_KV_LORA = 256, 128
SB_HEAD_DIM = 64
MEM_HEAD_DIM = 128
ROPE_BASE = 10000.0
RMS_EPS = 1e-6
LN_EPS = 1e-5
DEEPNORM_ALPHA = 2.0 ** 0.25
ADAM_LR, ADAM_B1, ADAM_B2, ADAM_EPS, ADAM_WD, ADAM_STEP = 0.001, 0.9, 0.999, 1e-08, 0.01, 10
LOG2E, LN2 = math.log2(math.e), math.log(2.0)

LANES = 128
GROUPS = 4
PROJ_WIDTH = 4096
COL_CQ, COL_CKV, COL_KROPE, COL_GATE_A, COL_GATE_B, COL_GATE_M = 0, 256, 384, 512, 1024, 1536
QKV_FIRST, QKV_WIDTH = 2048, 2048
COL_QB, COL_KB, COL_VB, COL_QM = 0, 512, 1024, 1536
IN_PIECES = ((0, 416), None, (416, 512), (2464, 512), (3488, 512), (928, 512), (1440, 512), (1952, 512), (2976, 512))
IN_PAD = 96

VMEM_LIMIT_BYTES = 56 * 1024 * 1024
NEG_BIG = -1e30
Q_BLOCK = 512
MEM_Q_BLOCK = 2048
SB_BWD_Q_BLOCK = 512
TRI_BLOCK = 256
TILE_ROWS = 64
KEY_CHUNK = 512

SHARDED = {
    "w_in": ((1024, 4000), 1), "w_mem_kv": ((1024, 1024), 0), "w_q_b": ((256, 768), 1), "w_kv_b": ((128, 1024), 1),
    "w_branch_mla": ((512, 1024), 1), "w_branch_sb": ((512, 1024), 1), "w_branch_mem": ((512, 1024), 1),
    "w_merge_gate": ((1024, 3072), 1), "w_out": ((1024, 1024), 0),
}
GATHER_GROUPS = (("w_in",), ("w_merge_gate",), ("w_q_b", "w_kv_b", "w_mem_kv", "w_branch_mla", "w_branch_sb", "w_branch_mem", "w_out"))
GRAD_GROUPS = (("w_out", "w_merge_gate", "w_branch_mla", "w_branch_sb", "w_branch_mem"), ("w_mem_kv", "w_q_b", "w_kv_b"), ("w_in",))
SMALL = (("q_a_gain", 256), ("kv_a_gain", 128), ("b_merge_gate", 3072), ("ln_gain", 1024), ("ln_bias", 1024))
SMALL_ROWS, SMALL_LANES = 48, 128
ADAM_ROWS = 256
LOSS_INDEX = 5504


def _cparams(*sem):
    return pltpu.CompilerParams(dimension_semantics=sem or None, vmem_limit_bytes=VMEM_LIMIT_BYTES)


_DIMS = {"nn": (((1,), (0,)), ((), ())), "nt": (((1,), (1,)), ((), ())), "tn": (((0,), (0,)), ((), ()))}


def _dot(a, b, dims):
    return lax.dot_general(a, b, _DIMS[dims], preferred_element_type=F32)


def _tile(dim, want):
    if dim <= want:
        return dim
    t = want - want % LANES
    while dim % t:
        t -= LANES
    assert t > 0, (dim, want)
    return t


_ANY = pl.BlockSpec(memory_space=pl.ANY)


def _mm(a, b, dims, *, name, out_dtype=F32, add=None, add_scale=1.0, col_scale=None, b_cols=None, behind=None,
        tm=1024, tn=1024, tk=1024):
    batch = a.shape[0] if a.ndim == 3 else None
    if dims == "nn":
        (m, k), (k2, n) = a.shape[-2:], b.shape[-2:]
    elif dims == "nt":
        (m, k), (n, k2) = a.shape[-2:], b.shape[-2:]
    else:
        (k, m), (k2, n) = a.shape[-2:], b.shape[-2:]
    assert k == k2 and a.ndim == b.ndim, (a.shape, b.shape, dims)
    assert batch is None or (b.shape[0] == batch and add is None and col_scale is None and b_cols is None)
    b_first = 0
    if b_cols is not None:
        assert dims == "nn"
        b_first, n = b_cols
    tm, tn, tk = _tile(m, tm), _tile(n, tn), _tile(k, tk)
    assert b_first % tn == 0
    jb = b_first // tn
    nk = k // tk

    def spec(block, index):
        if batch is None:
            return pl.BlockSpec(block, lambda bb, i, j, kk: index(i, j, kk))
        return pl.BlockSpec((None, *block), lambda bb, i, j, kk: (bb, *index(i, j, kk)))

    a_spec = spec((tk, tm), lambda i, j, kk: (kk, i)) if dims == "tn" else spec((tm, tk), lambda i, j, kk: (i, kk))
    b_spec = spec((tn, tk), lambda i, j, kk: (j, kk)) if dims == "nt" else spec((tk, tn), lambda i, j, kk: (kk, jb + j))
    o_spec = spec((tm, tn), lambda i, j, kk: (i, j))
    behind = [] if behind is None else behind if isinstance(behind, (list, tuple)) else [behind]
    optional = [(add, o_spec), (col_scale, pl.BlockSpec((1, tn), lambda bb, i, j, kk: (0, j))), *[(v, _ANY) for v in behind]]
    present = [(v, spec) for v, spec in optional if v is not None]

    def body(*refs):
        a_ref, b_ref = refs[:2]
        extra = iter(refs[2:2 + len(present)])
        add_ref = next(extra) if add is not None else None
        scale_ref = next(extra) if col_scale is not None else None
        o_ref = refs[2 + len(present)]
        part = _dot(a_ref[...].astype(BF16), b_ref[...].astype(BF16), dims)

        def finish(r):
            if add is not None:
                r = r + add_scale * add_ref[...]
            if col_scale is not None:
                r = r * scale_ref[...]
            o_ref[...] = r.astype(out_dtype)

        if nk == 1:
            finish(part)
            return
        acc = refs[-1]
        kk = pl.program_id(3)

        @pl.when(kk == 0)
        def _():
            acc[...] = part

        @pl.when(kk > 0)
        def _():
            acc[...] += part

        @pl.when(kk == nk - 1)
        def _():
            finish(acc[...])

    return pl.pallas_call(
        body, name=name, grid=(batch or 1, m // tm, n // tn, nk),
        in_specs=[a_spec, b_spec] + [spec for _, spec in present], out_specs=o_spec,
        out_shape=jax.ShapeDtypeStruct((m, n) if batch is None else (batch, m, n), out_dtype),
        scratch_shapes=[pltpu.VMEM((tm, tn), F32)] if nk > 1 else [],
        compiler_params=_cparams("parallel", "parallel", "parallel", "arbitrary"),
    )(a, b, *[v for v, _ in present])


def _rowwise(fn, ins, outs, *, name, rows, tr=512):
    n_in = len(ins)
    tr = min(tr, rows)
    in_specs, args = [], []
    for it in ins:
        arr, w, off = it if isinstance(it, tuple) else (it, it.shape[-1], 0)
        assert off % w == 0
        cb = off // w
        if arr.ndim == 3:
            in_specs.append(pl.BlockSpec((arr.shape[0], tr, w), lambda i, cb=cb: (0, i, cb)))
        elif arr.shape[0] == 1:
            in_specs.append(pl.BlockSpec((1, w), lambda i, cb=cb: (0, cb)))
        else:
            in_specs.append(pl.BlockSpec((tr, w), lambda i, cb=cb: (i, cb)))
        args.append(arr)
    out_shape, out_specs, is_sum = [], [], []
    for out in outs:
        is_sum.append(out[0] == "sum")
        if out[0] == "sum":
            out_shape.append(jax.ShapeDtypeStruct((1, out[1]), F32))
            out_specs.append(pl.BlockSpec((1, out[1]), lambda i: (0, 0)))
        elif len(out) == 3:
            out_shape.append(jax.ShapeDtypeStruct((out[0], rows, out[1]), out[2]))
            out_specs.append(pl.BlockSpec((out[0], tr, out[1]), lambda i: (0, i, 0)))
        else:
            out_shape.append(jax.ShapeDtypeStruct((rows, out[0]), out[1]))
            out_specs.append(pl.BlockSpec((tr, out[0]), lambda i: (i, 0)))

    def body(*refs):
        res = fn(*[r[...] for r in refs[:n_in]])
        for r, val, s in zip(refs[n_in:], res, is_sum, strict=True):
            if s:
                @pl.when(pl.program_id(0) == 0)
                def _(r=r):
                    r[...] = jnp.zeros_like(r)

                r[...] += val
            elif isinstance(val, (list, tuple)):
                for n, part in enumerate(val):
                    r[n] = part.astype(r.dtype)
            else:
                r[...] = val.astype(r.dtype)

    return pl.pallas_call(
        body, name=name, grid=(rows // tr,), in_specs=in_specs, out_specs=out_specs, out_shape=out_shape,
        compiler_params=_cparams("arbitrary"),
    )(*args)


def _colsum(v):
    return jnp.sum(v, axis=0, keepdims=True)


def _sigmoid(v):
    return 1.0 / (1.0 + jnp.exp(-v))


def _lane_groups(v):
    return [v[:, g * LANES:(g + 1) * LANES] for g in range(v.shape[1] // LANES)]


def _swap_halves(v, first_lane):
    lane = lax.broadcasted_iota(jnp.int32, v.shape, 1)
    return jnp.where(lane < first_lane + 16, pltpu.roll(v, 112, axis=1), pltpu.roll(v, 16, axis=1))


def _lane_sum(acc, v):
    for part in _lane_groups(v):
        acc = acc + part
    return acc


def _low_half(shape):
    return lax.broadcasted_iota(jnp.int32, shape, 1) < LANES // 2


def _select_heads(per_head, pick):
    if len(per_head) == 1:
        return pick(per_head[0], 0)
    return jnp.where(_low_half(per_head[0].shape), pick(per_head[0], 0), pick(per_head[1], 1))


def _attn_specs(s, sk, hp, bq, q0, k0, v0):
    wq = hp * LANES
    assert q0 % wq == 0 and k0 % wq == 0 and v0 % LANES == 0
    qb0, kb0, vb0 = q0 // wq, k0 // wq, v0 // LANES
    q_spec = pl.BlockSpec((bq, wq), lambda g, i: (i, qb0 + g))
    k_spec = pl.BlockSpec((sk, wq), lambda g, i: (0, kb0 + g))
    v_spec = pl.BlockSpec((sk, LANES), lambda g, i: (0, vb0 + g))
    row_out = lambda w: pl.BlockSpec((bq, w), lambda g, i: (i, g))
    key_out = lambda w: pl.BlockSpec((sk, w), lambda g, i: (0, g))
    return q_spec, k_spec, v_spec, row_out, key_out


def _chunks(i, bq, ch, sk, causal):
    return ((i + 1) * bq - 1) // ch if causal else jnp.int32(sk // ch - 1)


def _positions(i, c, bq, ch):
    return (i * bq + lax.broadcasted_iota(jnp.int32, (bq, ch), 0), c * ch + lax.broadcasted_iota(jnp.int32, (bq, ch), 1))


def _softmax_fwd(q, k, v, *, hp, causal, name, q0=0, k0=0, v0=0, q_rows=Q_BLOCK):
    s, sk = q.shape[0], k.shape[0]
    bq, ch = min(q_rows, s), min(KEY_CHUNK, sk)
    assert not causal or bq <= ch
    q_spec, k_spec, v_spec, row_out, _ = _attn_specs(s, sk, hp, bq, q0, k0, v0)

    def body(q_ref, k_ref, v_ref, o_ref, lse_ref, s_scr):
        i = pl.program_id(1)
        qs = _lane_groups(q_ref[...])
        last = _chunks(i, bq, ch, sk, causal)

        def scores(c, ms, masked):
            off = pl.multiple_of(c * ch, ch)
            out = []
            for j in range(hp):
                sc = _dot(qs[j], k_ref[pl.ds(off, ch), j * LANES:(j + 1) * LANES], "nt")
                if masked:
                    qpos, kpos = _positions(i, c, bq, ch)
                    sc = jnp.where(kpos <= qpos, sc, NEG_BIG)
                s_scr[j, c] = sc
                m = ms[j]
                for part in _lane_groups(sc):
                    m = jnp.maximum(m, part)
                out.append(m)
            return tuple(out)

        ms = lax.fori_loop(0, last, lambda c, m: scores(c, m, False), tuple(jnp.full((bq, LANES), NEG_BIG, F32) for _ in range(hp)))
        ms = scores(last, ms, causal)
        row_max = [jnp.max(m, axis=1, keepdims=True) for m in ms]

        def weigh(c, carry):
            off = pl.multiple_of(c * ch, ch)
            vt = v_ref[pl.ds(off, ch), :]
            out = []
            for j in range(hp):
                l, acc = carry[j]
                p = jnp.exp2(s_scr[j, c] - row_max[j])
                out.append((_lane_sum(l, p), acc + _dot(p.astype(BF16), vt, "nn")))
            return tuple(out)

        zero = jnp.zeros((bq, LANES), F32)
        res = lax.fori_loop(0, last + 1, weigh, tuple((zero, zero) for _ in range(hp)))
        row_sum = [jnp.sum(l, axis=1, keepdims=True) for l, _ in res]
        o_ref[...] = _select_heads([acc for _, acc in res], lambda acc, j: acc / row_sum[j])
        lse_ref[...] = _select_heads([jnp.broadcast_to(row_max[j] + jnp.log2(row_sum[j]), (bq, LANES)) for j in range(hp)], lambda a, j: a)

    return pl.pallas_call(
        body, name=name, grid=(GROUPS, s // bq), in_specs=[q_spec, k_spec, v_spec], out_specs=[row_out(LANES), row_out(LANES)],
        out_shape=[jax.ShapeDtypeStruct((s, GROUPS * LANES), F32)] * 2,
        scratch_shapes=[pltpu.VMEM((hp, sk // ch, bq, ch), F32)], compiler_params=_cparams("parallel", "arbitrary"),
    )(q, k, v)


def _head_cotangent(do, j, hp):
    if hp == 1:
        return do
    return jnp.where(_low_half(do.shape) == (j == 0), do, 0.0)


def _softmax_bwd(q, k, v, o, do, lse, behind, *, hp, causal, dq_scale, name, q0=0, k0=0, v0=0, q_rows=Q_BLOCK):
    s, sk = q.shape[0], k.shape[0]
    bq, ch = min(q_rows, s), min(KEY_CHUNK, sk)
    assert not causal or bq <= ch
    wq = hp * LANES
    q_spec, k_spec, v_spec, row_out, key_out = _attn_specs(s, sk, hp, bq, q0, k0, v0)

    def body(q_ref, k_ref, v_ref, o_ref, do_ref, lse_ref, _, dq_ref, dk_ref, dv_ref, dk_t, dv_t):
        i = pl.program_id(1)

        @pl.when(i == 0)
        def _():
            dk_t[...] = jnp.zeros_like(dk_t)
            dv_t[...] = jnp.zeros_like(dv_t)

        qs = _lane_groups(q_ref[...])
        do_all, o_all, lse_all = do_ref[...], o_ref[...], lse_ref[...]
        dos, deltas, lses = [], [], []
        for j in range(hp):
            d = _head_cotangent(do_all, j, hp)
            deltas.append(jnp.sum(d * o_all, axis=1, keepdims=True))
            dos.append(d.astype(BF16))
            lses.append(lse_all[:, j * (LANES // hp):j * (LANES // hp) + 1])
        last = _chunks(i, bq, ch, sk, causal)

        def chunk(c, dqs, masked):
            off = pl.multiple_of(c * ch, ch)
            vt = v_ref[pl.ds(off, ch), :]
            out, dks, dv = [], [], None
            for j in range(hp):
                kt = k_ref[pl.ds(off, ch), j * LANES:(j + 1) * LANES]
                p = jnp.exp2(_dot(qs[j], kt, "nt") - lses[j])
                if masked:
                    qpos, kpos = _positions(i, c, bq, ch)
                    p = jnp.where(kpos <= qpos, p, 0.0)
                ds = (p * (_dot(dos[j], vt, "nt") - deltas[j]) * LN2).astype(BF16)
                out.append(dqs[j] + _dot(ds, kt, "nn"))
                dks.append(_dot(qs[j], ds, "tn"))
                dvj = _dot(dos[j], p.astype(BF16), "tn")
                dv = dvj if dv is None else dv + dvj
            dk_t[c] += dks[0] if hp == 1 else jnp.concatenate(dks, axis=0)
            dv_t[c] += dv
            return tuple(out)

        dqs = lax.fori_loop(0, last, lambda c, d: chunk(c, d, False), tuple(jnp.zeros((bq, LANES), F32) for _ in range(hp)))
        dqs = chunk(last, dqs, causal)
        dq_ref[...] = (dqs[0] if hp == 1 else jnp.concatenate(dqs, axis=1)) * dq_scale

        @pl.when(i == s // bq - 1)
        def _():
            for c in range(sk // ch):
                dk_ref[c * ch:(c + 1) * ch, :] = dk_t[c].T
                dv_ref[c * ch:(c + 1) * ch, :] = dv_t[c].T

    return pl.pallas_call(
        body, name=name, grid=(GROUPS, s // bq),
        in_specs=[q_spec, k_spec, v_spec, row_out(LANES), row_out(LANES), row_out(LANES), _ANY],
        out_specs=[row_out(wq), key_out(wq), key_out(LANES)],
        out_shape=[jax.ShapeDtypeStruct((s, GROUPS * wq), F32), jax.ShapeDtypeStruct((sk, GROUPS * wq), F32),
                   jax.ShapeDtypeStruct((sk, GROUPS * LANES), F32)],
        scratch_shapes=[pltpu.VMEM((sk // ch, wq, ch), F32), pltpu.VMEM((sk // ch, LANES, ch), F32)],
        compiler_params=_cparams("arbitrary", "arbitrary"),
    )(q, k, v, o, do, lse, behind)


def _log2_sigmoid_pair(z2):
    minus_abs = lax.bitcast_convert_type(lax.bitcast_convert_type(z2, jnp.uint32) | jnp.uint32(0x80000000), F32)
    log_beta = jnp.minimum(z2, 0.0) - jnp.log2(1.0 + jnp.exp2(minus_abs))
    return log_beta, log_beta - z2


def _tilewise(fn, *arrays):
    rows, cols = arrays[0].shape
    step = min(TILE_ROWS, rows)
    grid = [[fn(*[None if a is None else a[r:r + step, c:c + LANES] for a in arrays]) for c in range(0, cols, LANES)]
            for r in range(0, rows, step)]
    return [jnp.concatenate([jnp.concatenate([cell[k] for cell in row], axis=1) for row in grid], axis=0)
            for k in range(len(grid[0][0]))]


def _split(v):
    hi = v.astype(BF16)
    return hi, (v - hi.astype(F32)).astype(BF16)


def _tri(n, after):
    rows, cols = lax.broadcasted_iota(jnp.int32, (n, n), 0), lax.broadcasted_iota(jnp.int32, (n, n), 1)
    return (rows > cols if after else rows < cols).astype(BF16)


def _running_sums(v, terms, start, tri, backwards):
    n = tri.shape[0]
    n_blocks = v.shape[1] // n
    order = range(n_blocks - 1, -1, -1) if backwards else range(n_blocks)
    stacked = tri if len(terms) == 1 else jnp.concatenate([tri] * len(terms), axis=0)
    parts, run = [None] * n_blocks, start
    for t in order:
        cols = slice(t * n, (t + 1) * n)
        lhs = terms[0][:, cols] if len(terms) == 1 else jnp.concatenate([term[:, cols] for term in terms], axis=1)
        parts[t] = _dot(lhs, stacked, "nn") + run
        run = run + jnp.sum(v[:, cols], axis=1, keepdims=True)
    return (parts[0] if n_blocks == 1 else jnp.concatenate(parts, axis=1)), run


def _sb_weights(qm, kt, run, tri, strict):
    def logs(z2, keep):
        log_beta, log_keep = _log2_sigmoid_pair(z2)
        if keep is not None:
            log_keep = jnp.where(keep, log_keep, 0.0)
        return log_beta, log_keep, *_split(log_keep)

    log_beta, log_keep, hi, lo = _tilewise(logs, _dot(qm, kt, "nt"), strict)
    behind, run = _running_sums(log_keep, (hi, lo), run, tri, True)

    def weigh(log_beta, behind, keep):
        a = jnp.exp2(log_beta + behind)
        return (a if keep is None else jnp.where(keep, a, 0.0),)

    (a,) = _tilewise(weigh, log_beta, behind, strict)
    return a, log_beta, run


def _sb_queries(q_all):
    low = _low_half(q_all.shape)
    zero = jnp.zeros_like(q_all)
    return [jnp.where(low, q_all, zero), jnp.where(low, zero, q_all)]


def _sb_fwd(qkv, *, q0, k0, v0, name):
    s = qkv.shape[0]
    bq, ch = min(Q_BLOCK, s), min(KEY_CHUNK, s)
    q_spec, k_spec, v_spec, row_out, _ = _attn_specs(s, s, 1, bq, q0, k0, v0)

    def body(q_ref, k_ref, v_ref, o_ref):
        i = pl.program_id(1)
        qms = _sb_queries(q_ref[...])
        tri = _tri(min(TRI_BLOCK, ch), True)
        last = _chunks(i, bq, ch, s, True)

        def chunk(c, carry, masked):
            off = pl.multiple_of(c * ch, ch)
            kt, vt = k_ref[pl.ds(off, ch), :], v_ref[pl.ds(off, ch), :]
            strict = None
            if masked:
                qpos, kpos = _positions(i, c, bq, ch)
                strict = kpos < qpos
            out = []
            for j in range(2):
                run, acc = carry[j]
                a, _, run = _sb_weights(qms[j], kt, run, tri, strict)
                out.append((run, acc + _dot(a.astype(BF16), vt, "nn")))
            return tuple(out)

        carry = chunk(last, tuple((jnp.zeros((bq, 1), F32), jnp.zeros((bq, LANES), F32)) for _ in range(2)), True)
        res = lax.fori_loop(0, last, lambda n, c: chunk(last - 1 - n, c, False), carry)
        o_ref[...] = _select_heads([acc for _, acc in res], lambda acc, j: acc)

    return pl.pallas_call(
        body, name=name, grid=(GROUPS, s // bq), in_specs=[q_spec, k_spec, v_spec], out_specs=row_out(LANES),
        out_shape=jax.ShapeDtypeStruct((s, GROUPS * LANES), F32), compiler_params=_cparams("parallel", "arbitrary"),
    )(qkv, qkv, qkv)


def _sb_bwd(qkv, do, behind, *, q0, k0, v0, dq_scale, name):
    s = qkv.shape[0]
    bq, ch = min(SB_BWD_Q_BLOCK, s), min(KEY_CHUNK, s)
    q_spec, k_spec, v_spec, row_out, key_out = _attn_specs(s, s, 1, bq, q0, k0, v0)

    def body(q_ref, k_ref, v_ref, do_ref, _, dq_ref, dk_ref, dv_ref, g_s, beta_s):
        i = pl.program_id(1)

        @pl.when(i == 0)
        def _():
            dk_ref[...] = jnp.zeros_like(dk_ref)
            dv_ref[...] = jnp.zeros_like(dv_ref)

        qms = _sb_queries(q_ref[...])
        do_all = do_ref[...]
        dos = [_head_cotangent(do_all, j, 2).astype(BF16) for j in range(2)]
        dos_ln2 = [(_head_cotangent(do_all, j, 2) * LN2).astype(BF16) for j in range(2)]
        tri_after, tri_before = _tri(min(TRI_BLOCK, ch), True), _tri(min(TRI_BLOCK, ch), False)
        last = _chunks(i, bq, ch, s, True)

        def strict_mask(c):
            qpos, kpos = _positions(i, c, bq, ch)
            return kpos < qpos

        def sweep1(c, runs, masked):
            off = pl.multiple_of(c * ch, ch)
            kt, vt = k_ref[pl.ds(off, ch), :], v_ref[pl.ds(off, ch), :]
            strict = strict_mask(c) if masked else None
            out, dv = [], None
            for j in range(2):
                a, log_beta, run = _sb_weights(qms[j], kt, runs[j], tri_after, strict)
                g_s[j, c] = (a * _dot(dos_ln2[j], vt, "nt")).astype(BF16)
                beta_s[j, c] = jnp.exp2(log_beta).astype(BF16)
                dvj = _dot(a.astype(BF16), dos[j], "tn")
                dv = dvj if dv is None else dv + dvj
                out.append(run)
            dv_ref[pl.ds(off, ch), :] += dv
            return tuple(out)

        runs = sweep1(last, tuple(jnp.zeros((bq, 1), F32) for _ in range(2)), True)
        lax.fori_loop(0, last, lambda n, r: sweep1(last - 1 - n, r, False), runs)

        def sweep2(c, carry, masked):
            off = pl.multiple_of(c * ch, ch)
            kt = k_ref[pl.ds(off, ch), :]
            out, dk = [], None
            for j in range(2):
                before, dq = carry[j]
                g16, beta = g_s[j, c], beta_s[j, c].astype(F32)
                g = g16.astype(F32)
                in_front, before = _running_sums(g, (g16,), before, tri_before, False)
                dz = g * (1.0 - beta) - beta * in_front
                if masked:
                    dz = jnp.where(strict_mask(c), dz, 0.0)
                dz = dz.astype(BF16)
                dkj = _dot(dz, qms[j], "tn")
                dk = dkj if dk is None else dk + dkj
                out.append((before, dq + _dot(dz, kt, "nn")))
            dk_ref[pl.ds(off, ch), :] += dk
            return tuple(out)

        carry = lax.fori_loop(0, last, lambda c, cr: sweep2(c, cr, False),
                              tuple((jnp.zeros((bq, 1), F32), jnp.zeros((bq, LANES), F32)) for _ in range(2)))
        res = sweep2(last, carry, True)
        dq_ref[...] = _select_heads([dq for _, dq in res], lambda dq, j: dq) * dq_scale

    n_ch = s // ch
    return pl.pallas_call(
        body, name=name, grid=(GROUPS, s // bq), in_specs=[q_spec, k_spec, v_spec, row_out(LANES), _ANY],
        out_specs=[row_out(LANES), key_out(LANES), key_out(LANES)],
        out_shape=[jax.ShapeDtypeStruct((s, GROUPS * LANES), F32)] * 3,
        scratch_shapes=[pltpu.VMEM((2, n_ch, bq, ch), BF16)] * 2,
        compiler_params=_cparams("arbitrary", "arbitrary"),
    )(qkv, qkv, qkv, do, behind)


def _rope_tables(s):
    half = MLA_ROPE // 2
    freqs = ROPE_BASE ** (-jnp.arange(half, dtype=F32) / half)
    ang = jnp.arange(s, dtype=F32)[:, None] * freqs[None, :]
    cos, sin = jnp.cos(ang), jnp.sin(ang)
    tail = jnp.zeros((s, LANES - MLA_NOPE - MLA_ROPE), F32)
    lead = lambda fill: jnp.full((s, MLA_NOPE), fill, F32)
    return dict(
        cos_k0=jnp.concatenate([cos, cos, lead(0.0), tail], axis=1), sin_k0=jnp.concatenate([-sin, sin, lead(0.0), tail], axis=1),
        cos_k64=jnp.concatenate([lead(0.0), cos, cos, tail], axis=1), sin_k64=jnp.concatenate([lead(0.0), -sin, sin, tail], axis=1),
        cos_q=jnp.concatenate([lead(1.0), cos, cos, tail], axis=1),
        sin_k64_t=jnp.concatenate([lead(0.0), sin, -sin, tail], axis=1),
    )


def _local_step(x, mem, target, w, emit=lambda grads: [jnp.zeros((8, LANES), F32)]):
    s = x.shape[0]
    rope = _rope_tables(s)
    xb = x.astype(BF16)
    inv_d = 1.0 / D_MODEL
    scale_a = LOG2E / math.sqrt(MLA_NOPE + MLA_ROPE)
    scale_b = LOG2E / math.sqrt(SB_HEAD_DIM)
    scale_m = LOG2E / math.sqrt(MEM_HEAD_DIM)
    arrive_after = getattr(w, "arrive_after", lambda *values: None)
    memb = mem.astype(BF16)

    arrive_after(xb, memb, *rope.values())
    proj = _mm(xb, w["w_in"], "nn", name="proj", b_cols=(0, QKV_FIRST))
    one = jnp.ones((1, 512), F32)
    qkv = _mm(xb, w["w_in"], "nn", name="proj_qkv", b_cols=(QKV_FIRST, QKV_WIDTH), out_dtype=BF16,
              col_scale=jnp.concatenate([one * scale_b, one, one, one * scale_m], axis=1))
    arrive_after(qkv)
    pre = _mm(xb, w["w_merge_gate"], "nn", name="merge_pre", out_dtype=BF16)
    arrive_after(pre)

    def rms_pair(c_q, c_kv, g_q, g_kv):
        return (c_q * lax.rsqrt(jnp.mean(c_q * c_q, axis=1, keepdims=True) + RMS_EPS) * g_q,
                c_kv * lax.rsqrt(jnp.mean(c_kv * c_kv, axis=1, keepdims=True) + RMS_EPS) * g_kv)

    n_q, n_kv = _rowwise(rms_pair, [(proj, 256, COL_CQ), (proj, 128, COL_CKV), w["q_a_gain"], w["kv_a_gain"]],
                         [(256, BF16), (128, BF16)], name="rms_pair", rows=s)
    q_a = _mm(n_q, w["w_q_b"], "nn", name="q_up")
    kv_a = _mm(n_kv, w["w_kv_b"], "nn", name="kv_up", out_dtype=BF16)

    def rope_q(qa, cos, sin):
        return (jnp.concatenate([(g * cos + _swap_halves(g, MLA_NOPE) * sin) * scale_a for g in _lane_groups(qa)], axis=1),)

    (q_mla,) = _rowwise(rope_q, [q_a, rope["cos_q"], rope["sin_k64"]], [(1024, BF16)], name="rope_q", rows=s)

    def rope_k(k_nope, k_rope, cos, sin):
        k_pe = pltpu.roll(k_rope * cos + _swap_halves(k_rope, 0) * sin, MLA_NOPE, axis=1).astype(BF16)
        return (jnp.concatenate([g + k_pe for g in _lane_groups(k_nope)], axis=1),)

    (k_mla,) = _rowwise(rope_k, [(kv_a, 1024, 0), (proj, 128, COL_KROPE), rope["cos_k0"], rope["sin_k0"]],
                        [(1024, BF16)], name="rope_k", rows=s)
    o_a, lse_a = _softmax_fwd(q_mla, k_mla, kv_a, hp=2, causal=True, name="mla_fwd", v0=1024)

    o_b = _sb_fwd(qkv, q0=COL_QB, k0=COL_KB, v0=COL_VB, name="sb_fwd")

    mem_kv =_mm(memb, w["w_mem_kv"], "nn", name="mem_kv", out_dtype=BF16)
    o_m, lse_m = _softmax_fwd(qkv, mem_kv, mem_kv, hp=1, causal=False, name="mem_fwd", q0=1536, v0=512, q_rows=MEM_Q_BLOCK)

    o_br = {"mla": o_a, "sb": o_b, "mem": o_m}
    gate_col = {"mla": COL_GATE_A, "sb": COL_GATE_B, "mem": COL_GATE_M}

    branches = ("mla", "sb", "mem")
    branch_ins = [o_br[br] for br in branches] + [(proj, 512, gate_col[br]) for br in branches]
    w_branch = jnp.stack([w[f"w_branch_{br}"] for br in branches])

    def gated(oa, ob, om, ga, gb, gm):
        return ([o * gate * _sigmoid(gate) for o, gate in ((oa, ga), (ob, gb), (om, gm))],)

    (u,) = _rowwise(gated, branch_ins, [(3, 512, BF16)], name="gated", rows=s)
    y = _mm(u, w_branch, "nn", name="branch", out_dtype=BF16)

    def merge(pa, pb, pm, ba, bb, bm, ys):
        return (sum(_sigmoid(p.astype(F32) + b) * ys[n].astype(F32) for n, (p, b) in enumerate(((pa, ba), (pb, bb), (pm, bm)))),)

    bias = w["b_merge_gate"]
    gate_ins = [(pre, 1024, 0), (pre, 1024, 1024), (pre, 1024, 2048), (bias, 1024, 0), (bias, 1024, 1024), (bias, 1024, 2048)]
    (merged,) = _rowwise(merge, gate_ins + [y], [(1024, BF16)], name="merge", rows=s)
    out = _mm(merged, w["w_out"], "nn", name="out_proj")

    def norm_loss(xv, ov, tv, gain, bias_ln):
        z = DEEPNORM_ALPHA * xv + ov
        zc = z - jnp.mean(z, axis=1, keepdims=True)
        rstd = lax.rsqrt(jnp.mean(zc * zc, axis=1, keepdims=True) + LN_EPS)
        xhat = zc * rstd
        err = xhat * gain + bias_ln - tv
        loss = 0.5 * jnp.sum(jnp.mean(err * err, axis=1, keepdims=True), axis=0, keepdims=True)
        dy = err * inv_d
        dxhat = dy * gain
        dz = rstd * (dxhat - jnp.mean(dxhat, axis=1, keepdims=True) - xhat * jnp.mean(dxhat * xhat, axis=1, keepdims=True))
        return dz, dz, _colsum(dy * xhat), _colsum(dy), jnp.broadcast_to(loss, (1, LANES))

    dz, dzb, g_ln_gain, g_ln_bias, loss = _rowwise(
        norm_loss, [x, out, target, w["ln_gain"], w["ln_bias"]],
        [(1024, F32), (1024, BF16), ("sum", 1024), ("sum", 1024), ("sum", LANES)], name="norm_loss", rows=s)

    grads = {"ln_gain": g_ln_gain, "ln_bias": g_ln_bias}
    dmerged = _mm(dzb, w["w_out"], "nt", name="d_merged")
    grads["w_out"] = _mm(merged, dzb, "tn", name="g_w_out", out_dtype=BF16)

    def merge_bwd(dm, pa, pb, pm, ba, bb, bm, ys):
        dys, dpre = [], []
        for n, (p, b) in enumerate(((pa, ba), (pb, bb), (pm, bm))):
            g = _sigmoid(p.astype(F32) + b)
            dpre.append(dm * ys[n].astype(F32) * g * (1.0 - g))
            dys.append(dm * g)
        dpre = jnp.concatenate(dpre, axis=1)
        return dpre, _colsum(dpre), dys

    dpre, grads["b_merge_gate"], dy = _rowwise(
        merge_bwd, [dmerged] + gate_ins + [y], [(3072, BF16), ("sum", 3072), (3, 1024, BF16)], name="merge_bwd", rows=s, tr=256)
    grads["w_merge_gate"] = _mm(xb, dpre, "tn", name="g_w_merge", out_dtype=BF16)
    dx = _mm(dpre, w["w_merge_gate"], "nt", name="dx_merge", add=dz, add_scale=DEEPNORM_ALPHA)
    g_w_branch = _mm(u, dy, "tn", name="g_w_branch", out_dtype=BF16)
    du = _mm(dy, w_branch, "nt", name="d_u")
    for n, br in enumerate(branches):
        grads[f"w_branch_{br}"] = g_w_branch[n]

    def gated_bwd(dus, oa, ob, om, ga, gb, gm):
        d_os, d_gates = [], []
        for n, (o, gate) in enumerate(((oa, ga), (ob, gb), (om, gm))):
            sg = _sigmoid(gate)
            d_os.append(dus[n] * gate * sg)
            d_gates.append(dus[n] * o * sg * (1.0 + gate * (1.0 - sg)))
        return *d_os, *d_gates

    res = _rowwise(gated_bwd, [du] + branch_ins, [(512, F32)] * 3 + [(512, BF16)] * 3, name="gated_bwd", rows=s)
    d_o, d_gate = dict(zip(branches, res[:3], strict=True)), dict(zip(branches, res[3:], strict=True))
    (sent,) = emit({n: grads[n] for n in ("w_out", "w_merge_gate", "w_branch_mla", "w_branch_sb", "w_branch_mem")})

    dq_m, dk_m, dv_m = _softmax_bwd(qkv, mem_kv, mem_kv, o_m, d_o["mem"], lse_m, sent, hp=1, causal=False, dq_scale=scale_m,
                                    name="mem_bwd", q0=1536, v0=512, q_rows=MEM_Q_BLOCK)
    grads["w_mem_kv"] = _mm(memb, jnp.concatenate([dk_m, dv_m], axis=1), "tn", name="g_w_mem_kv", out_dtype=BF16)

    dq_sb, dk_sb, dv_sb = _sb_bwd(qkv, d_o["sb"], sent, q0=0, k0=512, v0=1024, dq_scale=scale_b, name="sb_bwd")

    dq_mla, dk_mla, dv_a = _softmax_bwd(q_mla, k_mla, kv_a, o_a, d_o["mla"], lse_a, sent, hp=2, causal=True, dq_scale=scale_a,
                                        name="mla_bwd", v0=1024)

    def rope_q_bwd(dq, cos, sin):
        return (jnp.concatenate([g * cos + _swap_halves(g, MLA_NOPE) * sin for g in _lane_groups(dq)], axis=1),)

    (dq_a,) = _rowwise(rope_q_bwd, [dq_mla, rope["cos_q"], rope["sin_k64_t"]], [(1024, BF16)], name="rope_q_bwd", rows=s)
    grads["w_q_b"] = _mm(n_q, dq_a, "tn", name="g_w_q_b", out_dtype=BF16)
    dn_q = _mm(dq_a, w["w_q_b"], "nt", name="d_n_q")

    def rope_k_bwd(dk, cos, sin):
        groups = _lane_groups(dk)
        g = groups[0]
        for other in groups[1:]:
            g = g + other
        d_rope = pltpu.roll(g * cos + _swap_halves(g, MLA_NOPE) * sin, MLA_NOPE, axis=1)
        nope = _low_half(g.shape)
        return d_rope, jnp.concatenate([jnp.where(nope, grp, 0.0) for grp in groups], axis=1)

    dk_rope, dk_nope = _rowwise(rope_k_bwd, [dk_mla, rope["cos_k64"], rope["sin_k64_t"]], [(128, BF16), (1024, BF16)],
                                name="rope_k_bwd", rows=s)
    grads["w_kv_b"] = jnp.concatenate([_mm(n_kv, dk_nope, "tn", name="g_w_kv_b_k", out_dtype=BF16),
                                       _mm(n_kv, dv_a, "tn", name="g_w_kv_b_v", out_dtype=BF16)], axis=1)
    dn_kv = _mm(dk_nope, w["w_kv_b"][:, :1024], "nt", name="d_n_kv_k")
    dn_kv = _mm(dv_a, w["w_kv_b"][:, 1024:], "nt", name="d_n_kv_v", add=dn_kv)

    def rms_bwd(c_q, c_kv, dq, dkv, g_q, g_kv):
        res = []
        for c, dn, g in ((c_q, dq, g_q), (c_kv, dkv, g_kv)):
            r = lax.rsqrt(jnp.mean(c * c, axis=1, keepdims=True) + RMS_EPS)
            t = dn * g
            res += [r * t - c * (r * r * r) * jnp.mean(c * t, axis=1, keepdims=True), _colsum(dn * c * r)]
        return res

    dc_q, grads["q_a_gain"], dc_kv, grads["kv_a_gain"] = _rowwise(
        rms_bwd, [(proj, 256, COL_CQ), (proj, 128, COL_CKV), dn_q, dn_kv, w["q_a_gain"], w["kv_a_gain"]],
        [(256, BF16), ("sum", 256), (128, BF16), ("sum", 128)], name="rms_bwd", rows=s)

    sent = emit({n: grads[n] for n in ("w_mem_kv", "w_q_b", "w_kv_b")})

    dproj = jnp.concatenate(
        [dc_q, dc_kv, dk_rope, d_gate["mla"], d_gate["sb"], d_gate["mem"], dq_sb.astype(BF16), dk_sb.astype(BF16),
         dv_sb.astype(BF16), dq_m.astype(BF16)], axis=1)
    grads["w_in"] = _mm(xb, dproj, "tn", name="g_w_in", out_dtype=BF16, behind=sent)
    sent = emit({"w_in": grads["w_in"]})
    grad_x = _mm(dproj, w["w_in"], "nt", name="grad_x", add=dx, behind=sent)
    return loss, grad_x, grads


def _shard_shape(shape, axis):
    return tuple(d // N_DEV if a == axis else d for a, d in enumerate(shape))


def _from_blocks(blocks, name):
    shape, axis = SHARDED[name]
    return blocks.reshape(shape) if axis == 0 else blocks.transpose(1, 0, 2).reshape(shape)


def _to_blocks(full, name):
    shape, axis = SHARDED[name]
    shp = _shard_shape(shape, axis)
    return full.reshape(N_DEV, *shp) if axis == 0 else full.reshape(shape[0], N_DEV, shp[1]).transpose(1, 0, 2)


def _pad_heads(a, used):
    rows = a.shape[0]
    a = a.reshape(rows, MLA_HEADS, used)
    return jnp.concatenate([a, jnp.zeros((rows, MLA_HEADS, LANES - used), a.dtype)], axis=2).reshape(rows, MLA_HEADS * LANES)


def _to_kernel_layout(name, full):
    if name == "w_in":
        return jnp.concatenate([jnp.zeros((D_MODEL, IN_PAD), full.dtype) if piece is None else full[:, piece[0]:piece[0] + piece[1]]
                                for piece in IN_PIECES], axis=1)
    if name == "w_q_b":
        return _pad_heads(full, MLA_NOPE + MLA_ROPE)
    if name == "w_kv_b":
        kv = full.reshape(MLA_KV_LORA, MLA_HEADS, MLA_NOPE + MLA_V)
        return jnp.concatenate([_pad_heads(kv[:, :, :MLA_NOPE].reshape(MLA_KV_LORA, -1), MLA_NOPE),
                                kv[:, :, MLA_NOPE:].reshape(MLA_KV_LORA, -1)], axis=1)
    return full


def _from_kernel_layout(name, g):
    if name == "w_in":
        placed, at = [], 0
        for piece in IN_PIECES:
            if piece is not None:
                placed.append((piece[0], g[:, at:at + piece[1]]))
            at += IN_PAD if piece is None else piece[1]
        return jnp.concatenate([cols for _, cols in sorted(placed, key=lambda item: item[0])], axis=1)
    if name == "w_q_b":
        return g.reshape(MLA_Q_LORA, MLA_HEADS, LANES)[:, :, :MLA_NOPE + MLA_ROPE].reshape(MLA_Q_LORA, -1)
    if name == "w_kv_b":
        return jnp.concatenate([g[:, :1024].reshape(MLA_KV_LORA, MLA_HEADS, LANES)[:, :, :MLA_NOPE],
                                g[:, 1024:].reshape(MLA_KV_LORA, MLA_HEADS, MLA_V)], axis=2).reshape(MLA_KV_LORA, -1)
    return g


def _pack_small(vectors, loss=None):
    flat = [v.reshape(-1) for v in vectors]
    flat.append(jnp.zeros((SMALL_ROWS * SMALL_LANES - LOSS_INDEX,), F32) if loss is None else
                jnp.concatenate([loss.reshape(-1)[:1], jnp.zeros((SMALL_ROWS * SMALL_LANES - LOSS_INDEX - 1,), F32)]))
    return jnp.concatenate(flat).reshape(SMALL_ROWS, SMALL_LANES)


def _unpack_small(packed):
    flat, res, off = packed.reshape(-1), [], 0
    for _, n in SMALL:
        res.append(flat[off:off + n].reshape(1, n))
        off += n
    return res


def _me_and_peers():
    x, y, c = lax.axis_index("x"), lax.axis_index("y"), lax.axis_index("c")
    peers = []
    for kk in range(1, N_DEV):
        px, py, pc = (x + (kk >> 2)) % 2, (y + ((kk >> 1) & 1)) % 2, (c + (kk & 1)) % 2
        peers.append(((px, py, pc), 4 * px + 2 * py + pc))
    return 4 * x + 2 * y + c, peers


def _share_small(small, *, name):
    def body(small_ref, all_ref, send_sems, recv_sems, local_sem):
        me, peers = _me_and_peers()
        copies = [pltpu.make_async_remote_copy(src_ref=small_ref, dst_ref=all_ref.at[me], send_sem=send_sems.at[kk], recv_sem=recv_sems.at[kk],
                                               device_id=pos, device_id_type=pl.DeviceIdType.MESH) for kk, (pos, _) in enumerate(peers)]
        copies.append(pltpu.make_async_copy(small_ref, all_ref.at[me], local_sem))
        for cp in copies:
            cp.start()
        for cp in copies:
            cp.wait()

    hbm = pl.BlockSpec(memory_space=pl.ANY)
    return pl.pallas_call(
        body, name=name, in_specs=[hbm], out_specs=hbm, out_shape=jax.ShapeDtypeStruct((N_DEV, *small.shape), small.dtype),
        scratch_shapes=[pltpu.SemaphoreType.DMA((N_DEV - 1,)), pltpu.SemaphoreType.DMA((N_DEV - 1,)), pltpu.SemaphoreType.DMA],
        compiler_params=pltpu.CompilerParams(has_side_effects=True),
    )(small)


_HBM = pl.BlockSpec(memory_space=pltpu.HBM)
_SEM = pl.BlockSpec(memory_space=pltpu.SEMAPHORE)


def _exchange_copies(srcs, zones, send_sems, recv_sems, gather):
    me, peers = _me_and_peers()
    return [pltpu.make_async_remote_copy(
        src_ref=srcs[t] if gather else srcs[t].at[peer], dst_ref=zones[t].at[me], send_sem=send_sems.at[7 * t + kk],
        recv_sem=recv_sems.at[7 * t + kk], device_id=pos, device_id_type=pl.DeviceIdType.MESH)
        for t in range(len(srcs)) for kk, (pos, peer) in enumerate(peers)]


def _exchange_start(tensors, *, gather, name):
    n = len(tensors)
    zones = [lax.empty((N_DEV, *(t.shape if gather else t.shape[1:])), t.dtype) for t in tensors]

    def body(*refs):
        for cp in _exchange_copies(refs[:n], refs[n:2 * n], refs[2 * n], refs[2 * n + 1], gather):
            cp.start()
        refs[-1][...] = jnp.zeros_like(refs[-1])

    buffers = [pltpu.HBM(a.shape, a.dtype) for a in tensors + zones]
    res = pl.pallas_call(
        body, name=name, in_specs=[_HBM] * (2 * n),
        out_shape=(pltpu.SemaphoreType.DMA((7 * n,)), pltpu.SemaphoreType.DMA((7 * n,)), *buffers, jax.ShapeDtypeStruct((8, LANES), F32)),
        out_specs=(_SEM, _SEM, *[_HBM] * (2 * n), pl.BlockSpec(memory_space=pltpu.VMEM)),
        input_output_aliases={i: 2 + i for i in range(2 * n)},
        compiler_params=pltpu.CompilerParams(has_side_effects=pltpu.SideEffectType.DATAFLOW_SIDE_EFFECTING),
    )(*[pltpu.with_memory_space_constraint(a, pltpu.HBM) for a in tensors + zones])
    return dict(sems=res[:2], buffers=res[2:2 + 2 * n], gather=gather, started=res[-1])


def _exchange_wait(started, after, *, name):
    n = len(started["buffers"]) // 2

    def body(*refs):
        for cp in _exchange_copies(refs[:n], refs[n:2 * n], refs[2 * n], refs[2 * n + 1], started["gather"]):
            cp.wait_send()
            cp.wait_recv()

    res = pl.pallas_call(
        body, name=name, in_specs=[_HBM] * (2 * n) + [_SEM, _SEM] + [_ANY] * len(after),
        out_shape=tuple(pltpu.HBM(a.shape, a.dtype) for a in started["buffers"]), out_specs=tuple([_HBM] * (2 * n)),
        input_output_aliases={i: i for i in range(2 * n)},
        compiler_params=pltpu.CompilerParams(has_side_effects=pltpu.SideEffectType.DATAFLOW_SIDE_EFFECTING),
    )(*started["buffers"], *started["sems"], *after)
    return res[:n], res[n:]


def _adamw(contrib, w, m, v, *, name):
    rows, cols = w.shape
    tile = min(rows, ADAM_ROWS)

    def body(c_ref, w_ref, m_ref, v_ref, g_ref, d_ref, nm_ref, nv_ref):
        g = c_ref[0].astype(F32)
        for s in range(1, N_DEV):
            g = g + c_ref[s].astype(F32)
        m_new = ADAM_B1 * m_ref[...] + (1.0 - ADAM_B1) * g
        v_new = ADAM_B2 * v_ref[...] + (1.0 - ADAM_B2) * (g * g)
        m_hat = m_new / (1.0 - ADAM_B1 ** ADAM_STEP)
        v_hat = v_new / (1.0 - ADAM_B2 ** ADAM_STEP)
        g_ref[...] = g
        d_ref[...] = -ADAM_LR * (m_hat / (jnp.sqrt(v_hat) + ADAM_EPS) + ADAM_WD * w_ref[...])
        nm_ref[...] = m_new
        nv_ref[...] = v_new

    spec = pl.BlockSpec((tile, cols), lambda i: (i, 0))
    return pl.pallas_call(
        body, name=name, grid=(rows // tile,),
        in_specs=[pl.BlockSpec((N_DEV, tile, cols), lambda i: (0, i, 0)), spec, spec, spec], out_specs=[spec] * 4,
        out_shape=[jax.ShapeDtypeStruct((rows, cols), F32)] * 4, compiler_params=_cparams("parallel"),
    )(contrib, w, m, v)


class _Weights:
    def __init__(self, gathers, vectors, me):
        self.gathers, self.ready, self.me, self.after = gathers, dict(vectors), me, ()

    def arrive_after(self, *values):
        self.after = values

    def __getitem__(self, name):
        if name not in self.ready:
            gi = next(i for i, group in enumerate(GATHER_GROUPS) if name in group)
            after = [*self.after, *[g["started"] for g in self.gathers]]
            shards, zones = _exchange_wait(self.gathers[gi], after, name=f"gather_wait_{gi}")
            for n, shard, zone in zip(GATHER_GROUPS[gi], shards, zones, strict=True):
                blocks = lax.dynamic_update_slice_in_dim(zone, shard[None], self.me, 0)
                self.ready[n] = _to_kernel_layout(n, _from_blocks(blocks, n))
        return self.ready[name]


def kernel(x, mem, w_in, w_mem_kv, q_a_gain, w_q_b, kv_a_gain, w_kv_b, w_branch_mla, w_branch_sb, w_branch_mem, w_merge_gate, b_merge_gate, w_out, ln_gain, ln_bias, loss_target, m_w_in, m_w_mem_kv, m_q_a_gain, m_w_q_b, m_kv_a_gain, m_w_kv_b, m_w_branch_mla, m_w_branch_sb, m_w_branch_mem, m_w_merge_gate, m_b_merge_gate, m_w_out, m_ln_gain, m_ln_bias, v_w_in, v_w_mem_kv, v_q_a_gain, v_w_q_b, v_kv_a_gain, v_w_kv_b, v_w_branch_mla, v_w_branch_sb, v_w_branch_mem, v_w_merge_gate, v_b_merge_gate, v_w_out, v_ln_gain, v_ln_bias):
    given = dict(locals())
    small_names = [n for n, _ in SMALL]
    smalls = lambda prefix: [given[prefix + n] for n in small_names]
    me = 4 * lax.axis_index("x") + 2 * lax.axis_index("y") + lax.axis_index("c")

    gathers = [_exchange_start([given[n][0].astype(BF16) for n in group], gather=True, name=f"gather_start_{gi}")
               for gi, group in enumerate(GATHER_GROUPS)]
    w = _Weights(gathers, {n: given[n] for n in small_names}, me)
    exchanges = []
    results = [{}, {}, {}, {}]

    def finish(gi, after):
        names, started = exchanges[gi]
        sent, zones = _exchange_wait(started, after, name=f"grads_wait_{gi}")
        done = []
        for n, blocks, zone in zip(names, sent, zones, strict=True):
            own = lax.dynamic_index_in_dim(blocks, me, 0, keepdims=True)
            contrib = lax.dynamic_update_slice_in_dim(zone, own, me, 0)
            outs = _adamw(contrib, given[n][0], given["m_" + n][0], given["v_" + n][0], name=f"adamw_{n}")
            for kind, res in zip(results, outs, strict=True):
                kind[n] = res[None]
            done.append(outs[1])
        return done

    def emit(grads):
        blocks = [_to_blocks(_from_kernel_layout(n, g), n).astype(BF16) for n, g in grads.items()]
        exchanges.append((tuple(grads), _exchange_start(blocks, gather=False, name=f"grads_start_{len(exchanges)}")))
        started = [exchanges[-1][1]["started"]]
        if len(exchanges) == len(GRAD_GROUPS):
            for gi in range(len(GRAD_GROUPS) - 1):
                started += finish(gi, started[:1])
        return started

    loss, grad_x, grads = _local_step(x[0], mem[0], loss_target[0], w, emit)

    contrib_small = _share_small(_pack_small([grads[n] for n in small_names], loss), name="share_small")
    sml = _adamw(contrib_small, _pack_small(smalls("")), _pack_small(smalls("m_")), _pack_small(smalls("v_")), name="adamw_small")
    for kind, packed in zip(results, sml, strict=True):
        kind.update(zip(small_names, _unpack_small(packed), strict=True))
    finish(len(GRAD_GROUPS) - 1, [grad_x])
    order = ["w_in", "w_mem_kv", "q_a_gain", "w_q_b", "kv_a_gain", "w_kv_b", "w_branch_mla", "w_branch_sb", "w_branch_mem",
             "w_merge_gate", "b_merge_gate", "w_out", "ln_gain", "ln_bias"]
    loss_out = sml[0].reshape(-1)[LOSS_INDEX]
    return (loss_out, grad_x[None], *[kind[n] for kind in results for n in order])
```

```python
import math

import jax
import jax.numpy as jnp
from jax import lax
from jax.experimental import pallas as pl
from jax.experimental.pallas import tpu as pltpu

F32, BF16 = jnp.float32, jnp.bfloat16

N_DEV = 8
D_MODEL = 1024
MLA_HEADS, MLA_NOPE, MLA_ROPE, MLA_V = 8, 64, 32, 64
MLA_Q_LORA, MLA_KV_LORA = 256, 128
SB_HEAD_DIM = 64
MEM_HEAD_DIM = 128
ROPE_BASE = 10000.0
RMS_EPS = 1e-6
LN_EPS = 1e-5
DEEPNORM_ALPHA = 2.0 ** 0.25
ADAM_LR, ADAM_B1, ADAM_B2, ADAM_EPS, ADAM_WD, ADAM_STEP = 0.001, 0.9, 0.999, 1e-08, 0.01, 10
LOG2E, LN2 = math.log2(math.e), math.log(2.0)

LANES = 128
GROUPS = 4
PROJ_WIDTH = 4096
COL_CQ, COL_CKV, COL_KROPE, COL_GATE_A, COL_GATE_B, COL_GATE_M = 0, 256, 384, 512, 1024, 1536
QKV_FIRST, QKV_WIDTH = 2048, 2048
COL_QB, COL_KB, COL_VB, COL_QM = 0, 512, 1024, 1536
IN_PIECES = ((0, 416), None, (416, 512), (2464, 512), (3488, 512), (928, 512), (1440, 512), (1952, 512), (2976, 512))
IN_PAD = 96

VMEM_LIMIT_BYTES = 56 * 1024 * 1024
NEG_BIG = -1e30
Q_BLOCK = 512
MEM_Q_BLOCK = 2048
SB_BWD_Q_BLOCK = 512
TRI_BLOCK = 256
TILE_ROWS = 64
KEY_CHUNK = 512

SHARDED = {
    "w_in": ((1024, 4000), 1), "w_mem_kv": ((1024, 1024), 0), "w_q_b": ((256, 768), 1), "w_kv_b": ((128, 1024), 1),
    "w_branch_mla": ((512, 1024), 1), "w_branch_sb": ((512, 1024), 1), "w_branch_mem": ((512, 1024), 1),
    "w_merge_gate": ((1024, 3072), 1), "w_out": ((1024, 1024), 0),
}
GATHER_GROUPS = (("w_in",), ("w_merge_gate",), ("w_q_b", "w_kv_b", "w_mem_kv", "w_branch_mla", "w_branch_sb", "w_branch_mem", "w_out"))
GRAD_GROUPS = (("w_out", "w_merge_gate", "w_branch_mla", "w_branch_sb", "w_branch_mem"), ("w_mem_kv", "w_q_b", "w_kv_b"), ("w_in",))
SMALL = (("q_a_gain", 256), ("kv_a_gain", 128), ("b_merge_gate", 3072), ("ln_gain", 1024), ("ln_bias", 1024))
SMALL_ROWS, SMALL_LANES = 48, 128
ADAM_ROWS = 256
LOSS_INDEX = 5504


def _cparams(*sem):
    return pltpu.CompilerParams(dimension_semantics=sem or None, vmem_limit_bytes=VMEM_LIMIT_BYTES)


_DIMS = {"nn": (((1,), (0,)), ((), ())), "nt": (((1,), (1,)), ((), ())), "tn": (((0,), (0,)), ((), ()))}


def _dot(a, b, dims):
    return lax.dot_general(a, b, _DIMS[dims], preferred_element_type=F32)


def _tile(dim, want):
    if dim <= want:
        return dim
    t = want - want % LANES
    while dim % t:
        t -= LANES
    assert t > 0, (dim, want)
    return t


_ANY = pl.BlockSpec(memory_space=pl.ANY)


def _mm(a, b, dims, *, name, out_dtype=F32, add=None, add_scale=1.0, col_scale=None, b_cols=None, behind=None,
        tm=1024, tn=1024, tk=1024):
    batch = a.shape[0] if a.ndim == 3 else None
    if dims == "nn":
        (m, k), (k2, n) = a.shape[-2:], b.shape[-2:]
    elif dims == "nt":
        (m, k), (n, k2) = a.shape[-2:], b.shape[-2:]
    else:
        (k, m), (k2, n) = a.shape[-2:], b.shape[-2:]
    assert k == k2 and a.ndim == b.ndim, (a.shape, b.shape, dims)
    assert batch is None or (b.shape[0] == batch and add is None and col_scale is None and b_cols is None)
    b_first = 0
    if b_cols is not None:
        assert dims == "nn"
        b_first, n = b_cols
    tm, tn, tk = _tile(m, tm), _tile(n, tn), _tile(k, tk)
    assert b_first % tn == 0
    jb = b_first // tn
    nk = k // tk

    def spec(block, index):
        if batch is None:
            return pl.BlockSpec(block, lambda bb, i, j, kk: index(i, j, kk))
        return pl.BlockSpec((None, *block), lambda bb, i, j, kk: (bb, *index(i, j, kk)))

    a_spec = spec((tk, tm), lambda i, j, kk: (kk, i)) if dims == "tn" else spec((tm, tk), lambda i, j, kk: (i, kk))
    b_spec = spec((tn, tk), lambda i, j, kk: (j, kk)) if dims == "nt" else spec((tk, tn), lambda i, j, kk: (kk, jb + j))
    o_spec = spec((tm, tn), lambda i, j, kk: (i, j))
    behind = [] if behind is None else behind if isinstance(behind, (list, tuple)) else [behind]
    optional = [(add, o_spec), (col_scale, pl.BlockSpec((1, tn), lambda bb, i, j, kk: (0, j))), *[(v, _ANY) for v in behind]]
    present = [(v, spec) for v, spec in optional if v is not None]

    def body(*refs):
        a_ref, b_ref = refs[:2]
        extra = iter(refs[2:2 + len(present)])
        add_ref = next(extra) if add is not None else None
        scale_ref = next(extra) if col_scale is not None else None
        o_ref = refs[2 + len(present)]
        part = _dot(a_ref[...].astype(BF16), b_ref[...].astype(BF16), dims)

        def finish(r):
            if add is not None:
                r = r + add_scale * add_ref[...]
            if col_scale is not None:
                r = r * scale_ref[...]
            o_ref[...] = r.astype(out_dtype)

        if nk == 1:
            finish(part)
            return
        acc = refs[-1]
        kk = pl.program_id(3)

        @pl.when(kk == 0)
        def _():
            acc[...] = part

        @pl.when(kk > 0)
        def _():
            acc[...] += part

        @pl.when(kk == nk - 1)
        def _():
            finish(acc[...])

    return pl.pallas_call(
        body, name=name, grid=(batch or 1, m // tm, n // tn, nk),
        in_specs=[a_spec, b_spec] + [spec for _, spec in present], out_specs=o_spec,
        out_shape=jax.ShapeDtypeStruct((m, n) if batch is None else (batch, m, n), out_dtype),
        scratch_shapes=[pltpu.VMEM((tm, tn), F32)] if nk > 1 else [],
        compiler_params=_cparams("parallel", "parallel", "parallel", "arbitrary"),
    )(a, b, *[v for v, _ in present])


def _rowwise(fn, ins, outs, *, name, rows, tr=512):
    n_in = len(ins)
    tr = min(tr, rows)
    in_specs, args = [], []
    for it in ins:
        if isinstance(it, tuple) and it[0] == "whole":
            in_specs.append(pl.BlockSpec(it[1].shape, lambda i: (0, 0)))
            args.append(it[1])
            continue
        arr, w, off = it if isinstance(it, tuple) else (it, it.shape[-1], 0)
        assert off % w == 0
        cb = off // w
        if arr.ndim == 3:
            in_specs.append(pl.BlockSpec((arr.shape[0], tr, w), lambda i, cb=cb: (0, i, cb)))
        elif arr.shape[0] == 1:
            in_specs.append(pl.BlockSpec((1, w), lambda i, cb=cb: (0, cb)))
        else:
            in_specs.append(pl.BlockSpec((tr, w), lambda i, cb=cb: (i, cb)))
        args.append(arr)
    out_shape, out_specs, is_sum = [], [], []
    for out in outs:
        is_sum.append(out[0] == "sum")
        if out[0] == "sum":
            shape = out[1] if isinstance(out[1], tuple) else (1, out[1])
            out_shape.append(jax.ShapeDtypeStruct(shape, F32))
            out_specs.append(pl.BlockSpec(shape, lambda i: (0, 0)))
        elif len(out) == 3:
            out_shape.append(jax.ShapeDtypeStruct((out[0], rows, out[1]), out[2]))
            out_specs.append(pl.BlockSpec((out[0], tr, out[1]), lambda i: (0, i, 0)))
        else:
            out_shape.append(jax.ShapeDtypeStruct((rows, out[0]), out[1]))
            out_specs.append(pl.BlockSpec((tr, out[0]), lambda i: (i, 0)))

    def body(*refs):
        res = fn(*[r[...] for r in refs[:n_in]])
        for r, val, s in zip(refs[n_in:], res, is_sum, strict=True):
            if s:
                @pl.when(pl.program_id(0) == 0)
                def _(r=r):
                    r[...] = jnp.zeros_like(r)

                r[...] += val
            elif isinstance(val, (list, tuple)):
                for n, part in enumerate(val):
                    r[n] = part.astype(r.dtype)
            else:
                r[...] = val.astype(r.dtype)

    return pl.pallas_call(
        body, name=name, grid=(rows // tr,), in_specs=in_specs, out_specs=out_specs, out_shape=out_shape,
        compiler_params=_cparams("arbitrary"),
    )(*args)


def _colsum(v):
    return jnp.sum(v, axis=0, keepdims=True)


def _sigmoid(v):
    return 1.0 / (1.0 + jnp.exp(-v))


def _lane_groups(v):
    return [v[:, g * LANES:(g + 1) * LANES] for g in range(v.shape[1] // LANES)]


def _swap_halves(v, first_lane):
    lane = lax.broadcasted_iota(jnp.int32, v.shape, 1)
    return jnp.where(lane < first_lane + 16, pltpu.roll(v, 112, axis=1), pltpu.roll(v, 16, axis=1))


def _lane_sum(acc, v):
    for part in _lane_groups(v):
        acc = acc + part
    return acc


def _low_half(shape):
    return lax.broadcasted_iota(jnp.int32, shape, 1) < LANES // 2


def _select_heads(per_head, pick):
    if len(per_head) == 1:
        return pick(per_head[0], 0)
    return jnp.where(_low_half(per_head[0].shape), pick(per_head[0], 0), pick(per_head[1], 1))


def _attn_specs(s, sk, hp, bq, q0, k0, v0):
    wq = hp * LANES
    assert q0 % wq == 0 and k0 % wq == 0 and v0 % LANES == 0
    qb0, kb0, vb0 = q0 // wq, k0 // wq, v0 // LANES
    q_spec = pl.BlockSpec((bq, wq), lambda g, i: (i, qb0 + g))
    k_spec = pl.BlockSpec((sk, wq), lambda g, i: (0, kb0 + g))
    v_spec = pl.BlockSpec((sk, LANES), lambda g, i: (0, vb0 + g))
    row_out = lambda w: pl.BlockSpec((bq, w), lambda g, i: (i, g))
    key_out = lambda w: pl.BlockSpec((sk, w), lambda g, i: (0, g))
    return q_spec, k_spec, v_spec, row_out, key_out


def _chunks(i, bq, ch, sk, causal):
    return ((i + 1) * bq - 1) // ch if causal else jnp.int32(sk // ch - 1)


def _positions(i, c, bq, ch):
    return (i * bq + lax.broadcasted_iota(jnp.int32, (bq, ch), 0), c * ch + lax.broadcasted_iota(jnp.int32, (bq, ch), 1))


def _softmax_fwd(q, k, v, *, hp, causal, name, q0=0, k0=0, v0=0, q_rows=Q_BLOCK):
    s, sk = q.shape[0], k.shape[0]
    bq, ch = min(q_rows, s), min(KEY_CHUNK, sk)
    assert not causal or bq <= ch
    q_spec, k_spec, v_spec, row_out, _ = _attn_specs(s, sk, hp, bq, q0, k0, v0)

    def body(q_ref, k_ref, v_ref, o_ref, lse_ref, s_scr):
        i = pl.program_id(1)
        qs = _lane_groups(q_ref[...])
        last = _chunks(i, bq, ch, sk, causal)

        def scores(c, ms, masked):
            off = pl.multiple_of(c * ch, ch)
            out = []
            for j in range(hp):
                sc = _dot(qs[j], k_ref[pl.ds(off, ch), j * LANES:(j + 1) * LANES], "nt")
                if masked:
                    qpos, kpos = _positions(i, c, bq, ch)
                    sc = jnp.where(kpos <= qpos, sc, NEG_BIG)
                s_scr[j, c] = sc
                m = ms[j]
                for part in _lane_groups(sc):
                    m = jnp.maximum(m, part)
                out.append(m)
            return tuple(out)

        ms = lax.fori_loop(0, last, lambda c, m: scores(c, m, False), tuple(jnp.full((bq, LANES), NEG_BIG, F32) for _ in range(hp)))
        ms = scores(last, ms, causal)
        row_max = [jnp.max(m, axis=1, keepdims=True) for m in ms]

        def weigh(c, carry):
            off = pl.multiple_of(c * ch, ch)
            vt = v_ref[pl.ds(off, ch), :]
            out = []
            for j in range(hp):
                l, acc = carry[j]
                p = jnp.exp2(s_scr[j, c] - row_max[j])
                out.append((_lane_sum(l, p), acc + _dot(p.astype(BF16), vt, "nn")))
            return tuple(out)

        zero = jnp.zeros((bq, LANES), F32)
        res = lax.fori_loop(0, last + 1, weigh, tuple((zero, zero) for _ in range(hp)))
        row_sum = [jnp.sum(l, axis=1, keepdims=True) for l, _ in res]
        o_ref[...] = _select_heads([acc for _, acc in res], lambda acc, j: acc / row_sum[j])
        lse_ref[...] = _select_heads([jnp.broadcast_to(row_max[j] + jnp.log2(row_sum[j]), (bq, LANES)) for j in range(hp)], lambda a, j: a)

    return pl.pallas_call(
        body, name=name, grid=(GROUPS, s // bq), in_specs=[q_spec, k_spec, v_spec], out_specs=[row_out(LANES), row_out(LANES)],
        out_shape=[jax.ShapeDtypeStruct((s, GROUPS * LANES), F32)] * 2,
        scratch_shapes=[pltpu.VMEM((hp, sk // ch, bq, ch), F32)], compiler_params=_cparams("parallel", "arbitrary"),
    )(q, k, v)


def _head_cotangent(do, j, hp):
    if hp == 1:
        return do
    return jnp.where(_low_half(do.shape) == (j == 0), do, 0.0)


def _softmax_bwd(q, k, v, o, do, lse, behind, *, hp, causal, dq_scale, name, q0=0, k0=0, v0=0, q_rows=Q_BLOCK):
    s, sk = q.shape[0], k.shape[0]
    bq, ch = min(q_rows, s), min(KEY_CHUNK, sk)
    assert not causal or bq <= ch
    wq = hp * LANES
    q_spec, k_spec, v_spec, row_out, key_out = _attn_specs(s, sk, hp, bq, q0, k0, v0)

    def body(q_ref, k_ref, v_ref, o_ref, do_ref, lse_ref, _, dq_ref, dk_ref, dv_ref, dk_t, dv_t):
        i = pl.program_id(1)

        @pl.when(i == 0)
        def _():
            dk_t[...] = jnp.zeros_like(dk_t)
            dv_t[...] = jnp.zeros_like(dv_t)

        qs = _lane_groups(q_ref[...])
        do_all, o_all, lse_all = do_ref[...], o_ref[...], lse_ref[...]
        dos, deltas, lses = [], [], []
        for j in range(hp):
            d = _head_cotangent(do_all, j, hp)
            deltas.append(jnp.sum(d * o_all, axis=1, keepdims=True))
            dos.append(d.astype(BF16))
            lses.append(lse_all[:, j * (LANES // hp):j * (LANES // hp) + 1])
        last = _chunks(i, bq, ch, sk, causal)

        def chunk(c, dqs, masked):
            off = pl.multiple_of(c * ch, ch)
            vt = v_ref[pl.ds(off, ch), :]
            out, dks, dv = [], [], None
            for j in range(hp):
                kt = k_ref[pl.ds(off, ch), j * LANES:(j + 1) * LANES]
                p = jnp.exp2(_dot(qs[j], kt, "nt") - lses[j])
                if masked:
                    qpos, kpos = _positions(i, c, bq, ch)
                    p = jnp.where(kpos <= qpos, p, 0.0)
                ds = (p * (_dot(dos[j], vt, "nt") - deltas[j]) * LN2).astype(BF16)
                out.append(dqs[j] + _dot(ds, kt, "nn"))
                dks.append(_dot(qs[j], ds, "tn"))
                dvj = _dot(dos[j], p.astype(BF16), "tn")
                dv = dvj if dv is None else dv + dvj
            dk_t[c] += dks[0] if hp == 1 else jnp.concatenate(dks, axis=0)
            dv_t[c] += dv
            return tuple(out)

        dqs = lax.fori_loop(0, last, lambda c, d: chunk(c, d, False), tuple(jnp.zeros((bq, LANES), F32) for _ in range(hp)))
        dqs = chunk(last, dqs, causal)
        dq_ref[...] = (dqs[0] if hp == 1 else jnp.concatenate(dqs, axis=1)) * dq_scale

        @pl.when(i == s // bq - 1)
        def _():
            for c in range(sk // ch):
                dk_ref[c * ch:(c + 1) * ch, :] = dk_t[c].T
                dv_ref[c * ch:(c + 1) * ch, :] = dv_t[c].T

    return pl.pallas_call(
        body, name=name, grid=(GROUPS, s // bq),
        in_specs=[q_spec, k_spec, v_spec, row_out(LANES), row_out(LANES), row_out(LANES), _ANY],
        out_specs=[row_out(wq), key_out(wq), key_out(LANES)],
        out_shape=[jax.ShapeDtypeStruct((s, GROUPS * wq), F32), jax.ShapeDtypeStruct((sk, GROUPS * wq), F32),
                   jax.ShapeDtypeStruct((sk, GROUPS * LANES), F32)],
        scratch_shapes=[pltpu.VMEM((sk // ch, wq, ch), F32), pltpu.VMEM((sk // ch, LANES, ch), F32)],
        compiler_params=_cparams("arbitrary", "arbitrary"),
    )(q, k, v, o, do, lse, behind)


def _log2_sigmoid_pair(z2):
    minus_abs = lax.bitcast_convert_type(lax.bitcast_convert_type(z2, jnp.uint32) | jnp.uint32(0x80000000), F32)
    log_beta = jnp.minimum(z2, 0.0) - jnp.log2(1.0 + jnp.exp2(minus_abs))
    return log_beta, log_beta - z2


def _tilewise(fn, *arrays):
    rows, cols = arrays[0].shape
    step = min(TILE_ROWS, rows)
    grid = [[fn(*[None if a is None else a[r:r + step, c:c + LANES] for a in arrays]) for c in range(0, cols, LANES)]
            for r in range(0, rows, step)]
    return [jnp.concatenate([jnp.concatenate([cell[k] for cell in row], axis=1) for row in grid], axis=0)
            for k in range(len(grid[0][0]))]


def _split(v):
    hi = v.astype(BF16)
    return hi, (v - hi.astype(F32)).astype(BF16)


def _tri(n, after):
    rows, cols = lax.broadcasted_iota(jnp.int32, (n, n), 0), lax.broadcasted_iota(jnp.int32, (n, n), 1)
    return (rows > cols if after else rows < cols).astype(BF16)


def _running_sums(v, terms, start, tri, backwards):
    n = tri.shape[0]
    n_blocks = v.shape[1] // n
    order = range(n_blocks - 1, -1, -1) if backwards else range(n_blocks)
    stacked = tri if len(terms) == 1 else jnp.concatenate([tri] * len(terms), axis=0)
    parts, run = [None] * n_blocks, start
    for t in order:
        cols = slice(t * n, (t + 1) * n)
        lhs = terms[0][:, cols] if len(terms) == 1 else jnp.concatenate([term[:, cols] for term in terms], axis=1)
        parts[t] = _dot(lhs, stacked, "nn") + run
        run = run + jnp.sum(v[:, cols], axis=1, keepdims=True)
    return (parts[0] if n_blocks == 1 else jnp.concatenate(parts, axis=1)), run


def _sb_weights(qm, kt, run, tri, strict):
    def logs(z2, keep):
        log_beta, log_keep = _log2_sigmoid_pair(z2)
        if keep is not None:
            log_keep = jnp.where(keep, log_keep, 0.0)
        return log_beta, log_keep, *_split(log_keep)

    log_beta, log_keep, hi, lo = _tilewise(logs, _dot(qm, kt, "nt"), strict)
    behind, run = _running_sums(log_keep, (hi, lo), run, tri, True)

    def weigh(log_beta, behind, keep):
        a = jnp.exp2(log_beta + behind)
        return (a if keep is None else jnp.where(keep, a, 0.0),)

    (a,) = _tilewise(weigh, log_beta, behind, strict)
    return a, log_beta, run


def _sb_queries(q_all):
    low = _low_half(q_all.shape)
    zero = jnp.zeros_like(q_all)
    return [jnp.where(low, q_all, zero), jnp.where(low, zero, q_all)]


def _sb_fwd(qkv, *, q0, k0, v0, name):
    s = qkv.shape[0]
    bq, ch = min(Q_BLOCK, s), min(KEY_CHUNK, s)
    q_spec, k_spec, v_spec, row_out, _ = _attn_specs(s, s, 1, bq, q0, k0, v0)

    def body(q_ref, k_ref, v_ref, o_ref):
        i = pl.program_id(1)
        qms = _sb_queries(q_ref[...])
        tri = _tri(min(TRI_BLOCK, ch), True)
        last = _chunks(i, bq, ch, s, True)

        def chunk(c, carry, masked):
            off = pl.multiple_of(c * ch, ch)
            kt, vt = k_ref[pl.ds(off, ch), :], v_ref[pl.ds(off, ch), :]
            strict = None
            if masked:
                qpos, kpos = _positions(i, c, bq, ch)
                strict = kpos < qpos
            out = []
            for j in range(2):
                run, acc = carry[j]
                a, _, run = _sb_weights(qms[j], kt, run, tri, strict)
                out.append((run, acc + _dot(a.astype(BF16), vt, "nn")))
            return tuple(out)

        carry = chunk(last, tuple((jnp.zeros((bq, 1), F32), jnp.zeros((bq, LANES), F32)) for _ in range(2)), True)
        res = lax.fori_loop(0, last, lambda n, c: chunk(last - 1 - n, c, False), carry)
        o_ref[...] = _select_heads([acc for _, acc in res], lambda acc, j: acc)

    return pl.pallas_call(
        body, name=name, grid=(GROUPS, s // bq), in_specs=[q_spec, k_spec, v_spec], out_specs=row_out(LANES),
        out_shape=jax.ShapeDtypeStruct((s, GROUPS * LANES), F32), compiler_params=_cparams("parallel", "arbitrary"),
    )(qkv, qkv, qkv)


def _sb_bwd(qkv, do, behind, *, q0, k0, v0, dq_scale, name):
    s = qkv.shape[0]
    bq, ch = min(SB_BWD_Q_BLOCK, s), min(KEY_CHUNK, s)
    q_spec, k_spec, v_spec, row_out, key_out = _attn_specs(s, s, 1, bq, q0, k0, v0)

    def body(q_ref, k_ref, v_ref, do_ref, _, dq_ref, dk_ref, dv_ref, g_s, beta_s):
        i = pl.program_id(1)

        @pl.when(i == 0)
        def _():
            dk_ref[...] = jnp.zeros_like(dk_ref)
            dv_ref[...] = jnp.zeros_like(dv_ref)

        qms = _sb_queries(q_ref[...])
        do_all = do_ref[...]
        dos = [_head_cotangent(do_all, j, 2).astype(BF16) for j in range(2)]
        dos_ln2 = [(_head_cotangent(do_all, j, 2) * LN2).astype(BF16) for j in range(2)]
        tri_after, tri_before = _tri(min(TRI_BLOCK, ch), True), _tri(min(TRI_BLOCK, ch), False)
        last = _chunks(i, bq, ch, s, True)

        def strict_mask(c):
            qpos, kpos = _positions(i, c, bq, ch)
            return kpos < qpos

        def sweep1(c, runs, masked):
            off = pl.multiple_of(c * ch, ch)
            kt, vt = k_ref[pl.ds(off, ch), :], v_ref[pl.ds(off, ch), :]
            strict = strict_mask(c) if masked else None
            out, dv = [], None
            for j in range(2):
                a, log_beta, run = _sb_weights(qms[j], kt, runs[j], tri_after, strict)
                g_s[j, c] = (a * _dot(dos_ln2[j], vt, "nt")).astype(BF16)
                beta_s[j, c] = jnp.exp2(log_beta).astype(BF16)
                dvj = _dot(a.astype(BF16), dos[j], "tn")
                dv = dvj if dv is None else dv + dvj
                out.append(run)
            dv_ref[pl.ds(off, ch), :] += dv
            return tuple(out)

        runs = sweep1(last, tuple(jnp.zeros((bq, 1), F32) for _ in range(2)), True)
        lax.fori_loop(0, last, lambda n, r: sweep1(last - 1 - n, r, False), runs)

        def sweep2(c, carry, masked):
            off = pl.multiple_of(c * ch, ch)
            kt = k_ref[pl.ds(off, ch), :]
            out, dk = [], None
            for j in range(2):
                before, dq = carry[j]
                g16, beta = g_s[j, c], beta_s[j, c].astype(F32)
                g = g16.astype(F32)
                in_front, before = _running_sums(g, (g16,), before, tri_before, False)
                dz = g * (1.0 - beta) - beta * in_front
                if masked:
                    dz = jnp.where(strict_mask(c), dz, 0.0)
                dz = dz.astype(BF16)
                dkj = _dot(dz, qms[j], "tn")
                dk = dkj if dk is None else dk + dkj
                out.append((before, dq + _dot(dz, kt, "nn")))
            dk_ref[pl.ds(off, ch), :] += dk
            return tuple(out)

        carry = lax.fori_loop(0, last, lambda c, cr: sweep2(c, cr, False),
                              tuple((jnp.zeros((bq, 1), F32), jnp.zeros((bq, LANES), F32)) for _ in range(2)))
        res = sweep2(last, carry, True)
        dq_ref[...] = _select_heads([dq for _, dq in res], lambda dq, j: dq) * dq_scale

    n_ch = s // ch
    return pl.pallas_call(
        body, name=name, grid=(GROUPS, s // bq), in_specs=[q_spec, k_spec, v_spec, row_out(LANES), _ANY],
        out_specs=[row_out(LANES), key_out(LANES), key_out(LANES)],
        out_shape=[jax.ShapeDtypeStruct((s, GROUPS * LANES), F32)] * 3,
        scratch_shapes=[pltpu.VMEM((2, n_ch, bq, ch), BF16)] * 2,
        compiler_params=_cparams("arbitrary", "arbitrary"),
    )(qkv, qkv, qkv, do, behind)


def _rope_tables(s):
    half = MLA_ROPE // 2
    freqs = ROPE_BASE ** (-jnp.arange(half, dtype=F32) / half)
    ang = jnp.arange(s, dtype=F32)[:, None] * freqs[None, :]
    cos, sin = jnp.cos(ang), jnp.sin(ang)
    tail = jnp.zeros((s, LANES - MLA_NOPE - MLA_ROPE), F32)
    lead = lambda fill: jnp.full((s, MLA_NOPE), fill, F32)
    return dict(
        cos_k0=jnp.concatenate([cos, cos, lead(0.0), tail], axis=1), sin_k0=jnp.concatenate([-sin, sin, lead(0.0), tail], axis=1),
        cos_k64=jnp.concatenate([lead(0.0), cos, cos, tail], axis=1), sin_k64=jnp.concatenate([lead(0.0), -sin, sin, tail], axis=1),
        cos_q=jnp.concatenate([lead(1.0), cos, cos, tail], axis=1),
        sin_k64_t=jnp.concatenate([lead(0.0), sin, -sin, tail], axis=1),
    )


def _local_step(x, mem, target, w, emit=lambda grads: [jnp.zeros((8, LANES), F32)]):
    s = x.shape[0]
    rope = _rope_tables(s)
    xb = x.astype(BF16)
    inv_d = 1.0 / D_MODEL
    scale_a = LOG2E / math.sqrt(MLA_NOPE + MLA_ROPE)
    scale_b = LOG2E / math.sqrt(SB_HEAD_DIM)
    scale_m = LOG2E / math.sqrt(MEM_HEAD_DIM)
    arrive_after = getattr(w, "arrive_after", lambda *values: None)
    memb = mem.astype(BF16)

    arrive_after(xb, memb, *rope.values())
    proj = _mm(xb, w["w_in"], "nn", name="proj", b_cols=(0, QKV_FIRST))
    one = jnp.ones((1, 512), F32)
    qkv = _mm(xb, w["w_in"], "nn", name="proj_qkv", b_cols=(QKV_FIRST, QKV_WIDTH), out_dtype=BF16,
              col_scale=jnp.concatenate([one * scale_b, one, one, one * scale_m], axis=1))
    arrive_after(qkv)
    pre = _mm(xb, w["w_merge_gate"], "nn", name="merge_pre", out_dtype=BF16)
    arrive_after(pre)

    def mla_inputs(c_q, c_kv, k_rope, g_q, g_kv, w_q, w_kv, cos_q, sin_q, cos_k, sin_k):
        n_q = (c_q * lax.rsqrt(jnp.mean(c_q * c_q, axis=1, keepdims=True) + RMS_EPS) * g_q).astype(BF16)
        n_kv = (c_kv * lax.rsqrt(jnp.mean(c_kv * c_kv, axis=1, keepdims=True) + RMS_EPS) * g_kv).astype(BF16)
        q_a = _dot(n_q, w_q, "nn")
        kv_a = _dot(n_kv, w_kv, "nn").astype(BF16)
        q = jnp.concatenate([(g * cos_q + _swap_halves(g, MLA_NOPE) * sin_q) * scale_a for g in _lane_groups(q_a)], axis=1)
        k_pe = pltpu.roll(k_rope * cos_k + _swap_halves(k_rope, 0) * sin_k, MLA_NOPE, axis=1).astype(BF16)
        k = jnp.concatenate([g + k_pe for g in _lane_groups(kv_a[:, :1024])], axis=1)
        return n_q, n_kv, q, k, kv_a[:, 1024:]

    n_q, n_kv, q_mla, k_mla, v_a = _rowwise(
        mla_inputs, [(proj, 256, COL_CQ), (proj, 128, COL_CKV), (proj, 128, COL_KROPE), w["q_a_gain"], w["kv_a_gain"],
                     ("whole", w["w_q_b"]), ("whole", w["w_kv_b"]), rope["cos_q"], rope["sin_k64"], rope["cos_k0"], rope["sin_k0"]],
        [(256, BF16), (128, BF16), (1024, BF16), (1024, BF16), (512, BF16)], name="mla_inputs", rows=s)
    o_a, lse_a = _softmax_fwd(q_mla, k_mla, v_a, hp=2, causal=True, name="mla_fwd")

    o_b = _sb_fwd(qkv, q0=COL_QB, k0=COL_KB, v0=COL_VB, name="sb_fwd")

    mem_kv =_mm(memb, w["w_mem_kv"], "nn", name="mem_kv", out_dtype=BF16)
    o_m, lse_m = _softmax_fwd(qkv, mem_kv, mem_kv, hp=1, causal=False, name="mem_fwd", q0=1536, v0=512, q_rows=MEM_Q_BLOCK)

    o_br = {"mla": o_a, "sb": o_b, "mem": o_m}
    gate_col = {"mla": COL_GATE_A, "sb": COL_GATE_B, "mem": COL_GATE_M}

    branches = ("mla", "sb", "mem")
    branch_ins = [o_br[br] for br in branches] + [(proj, 512, gate_col[br]) for br in branches]
    w_branch = jnp.stack([w[f"w_branch_{br}"] for br in branches])

    def gated(oa, ob, om, ga, gb, gm):
        return ([o * gate * _sigmoid(gate) for o, gate in ((oa, ga), (ob, gb), (om, gm))],)

    (u,) = _rowwise(gated, branch_ins, [(3, 512, BF16)], name="gated", rows=s)
    y = _mm(u, w_branch, "nn", name="branch", out_dtype=BF16)

    def merge(pa, pb, pm, ba, bb, bm, ys):
        return (sum(_sigmoid(p.astype(F32) + b) * ys[n].astype(F32) for n, (p, b) in enumerate(((pa, ba), (pb, bb), (pm, bm)))),)

    bias = w["b_merge_gate"]
    gate_ins = [(pre, 1024, 0), (pre, 1024, 1024), (pre, 1024, 2048), (bias, 1024, 0), (bias, 1024, 1024), (bias, 1024, 2048)]
    (merged,) = _rowwise(merge, gate_ins + [y], [(1024, BF16)], name="merge", rows=s)
    def norm_loss(xv, mv, tv, gain, bias_ln, w_o):
        z = DEEPNORM_ALPHA * xv + _dot(mv, w_o, "nn")
        zc = z - jnp.mean(z, axis=1, keepdims=True)
        rstd = lax.rsqrt(jnp.mean(zc * zc, axis=1, keepdims=True) + LN_EPS)
        xhat = zc * rstd
        err = xhat * gain + bias_ln - tv
        loss = 0.5 * jnp.sum(jnp.mean(err * err, axis=1, keepdims=True), axis=0, keepdims=True)
        dy = err * inv_d
        dxhat = dy * gain
        dz = rstd * (dxhat - jnp.mean(dxhat, axis=1, keepdims=True) - xhat * jnp.mean(dxhat * xhat, axis=1, keepdims=True))
        dz16 = dz.astype(BF16)
        return dz, dz16, _colsum(dy * xhat), _colsum(dy), jnp.broadcast_to(loss, (1, LANES)), _dot(dz16, w_o, "nt")

    dz, dzb, g_ln_gain, g_ln_bias, loss, dmerged = _rowwise(
        norm_loss, [x, merged, target, w["ln_gain"], w["ln_bias"], ("whole", w["w_out"])],
        [(1024, F32), (1024, BF16), ("sum", 1024), ("sum", 1024), ("sum", LANES), (1024, F32)], name="norm_loss", rows=s)

    grads = {"ln_gain": g_ln_gain, "ln_bias": g_ln_bias}
    grads["w_out"] = _mm(merged, dzb, "tn", name="g_w_out", out_dtype=BF16)

    def merge_bwd(dm, pa, pb, pm, ba, bb, bm, ys):
        dys, dpre = [], []
        for n, (p, b) in enumerate(((pa, ba), (pb, bb), (pm, bm))):
            g = _sigmoid(p.astype(F32) + b)
            dpre.append(dm * ys[n].astype(F32) * g * (1.0 - g))
            dys.append(dm * g)
        dpre = jnp.concatenate(dpre, axis=1)
        return dpre, _colsum(dpre), dys

    dpre, grads["b_merge_gate"], dy = _rowwise(
        merge_bwd, [dmerged] + gate_ins + [y], [(3072, BF16), ("sum", 3072), (3, 1024, BF16)], name="merge_bwd", rows=s, tr=256)
    grads["w_merge_gate"] = _mm(xb, dpre, "tn", name="g_w_merge", out_dtype=BF16)
    dx = _mm(dpre, w["w_merge_gate"], "nt", name="dx_merge", add=dz, add_scale=DEEPNORM_ALPHA)
    g_w_branch = _mm(u, dy, "tn", name="g_w_branch", out_dtype=BF16)
    du = _mm(dy, w_branch, "nt", name="d_u")
    for n, br in enumerate(branches):
        grads[f"w_branch_{br}"] = g_w_branch[n]

    def gated_bwd(dus, oa, ob, om, ga, gb, gm):
        d_os, d_gates = [], []
        for n, (o, gate) in enumerate(((oa, ga), (ob, gb), (om, gm))):
            sg = _sigmoid(gate)
            d_os.append(dus[n] * gate * sg)
            d_gates.append(dus[n] * o * sg * (1.0 + gate * (1.0 - sg)))
        return *d_os, *d_gates

    res = _rowwise(gated_bwd, [du] + branch_ins, [(512, F32)] * 3 + [(512, BF16)] * 3, name="gated_bwd", rows=s)
    d_o, d_gate = dict(zip(branches, res[:3], strict=True)), dict(zip(branches, res[3:], strict=True))
    (sent,) = emit({n: grads[n] for n in ("w_out", "w_merge_gate", "w_branch_mla", "w_branch_sb", "w_branch_mem")})

    dq_m, dk_m, dv_m = _softmax_bwd(qkv, mem_kv, mem_kv, o_m, d_o["mem"], lse_m, sent, hp=1, causal=False, dq_scale=scale_m,
                                    name="mem_bwd", q0=1536, v0=512, q_rows=MEM_Q_BLOCK)
    grads["w_mem_kv"] = _mm(memb, jnp.concatenate([dk_m, dv_m], axis=1), "tn", name="g_w_mem_kv", out_dtype=BF16)

    dq_sb, dk_sb, dv_sb = _sb_bwd(qkv, d_o["sb"], sent, q0=0, k0=512, v0=1024, dq_scale=scale_b, name="sb_bwd")

    dq_mla, dk_mla, dv_a = _softmax_bwd(q_mla, k_mla, v_a, o_a, d_o["mla"], lse_a, sent, hp=2, causal=True, dq_scale=scale_a,
                                        name="mla_bwd")

    def mla_inputs_bwd(dq, dk, dv, n_q, n_kv, c_q, c_kv, g_q, g_kv, w_q, w_kv, cos_q, sin_q, cos_k, sin_k):
        dq_a = jnp.concatenate([g * cos_q + _swap_halves(g, MLA_NOPE) * sin_q for g in _lane_groups(dq)], axis=1).astype(BF16)
        groups = _lane_groups(dk)
        g_rope = groups[0]
        for other in groups[1:]:
            g_rope = g_rope + other
        dk_rope = pltpu.roll(g_rope * cos_k + _swap_halves(g_rope, MLA_NOPE) * sin_k, MLA_NOPE, axis=1)
        nope = _low_half(g_rope.shape)
        dkv_a = jnp.concatenate([jnp.where(nope, grp, 0.0) for grp in groups] + [dv], axis=1).astype(BF16)
        res = []
        for c, dn, g in ((c_q, _dot(dq_a, w_q, "nt"), g_q), (c_kv, _dot(dkv_a, w_kv, "nt"), g_kv)):
            r = lax.rsqrt(jnp.mean(c * c, axis=1, keepdims=True) + RMS_EPS)
            t = dn * g
            res += [r * t - c * (r * r * r) * jnp.mean(c * t, axis=1, keepdims=True), _colsum(dn * c * r)]
        return *res, dk_rope, _dot(n_q, dq_a, "tn"), _dot(n_kv, dkv_a, "tn")

    dc_q, grads["q_a_gain"], dc_kv, grads["kv_a_gain"], dk_rope, g_w_q_b, g_w_kv_b = _rowwise(
        mla_inputs_bwd, [dq_mla, dk_mla, dv_a, n_q, n_kv, (proj, 256, COL_CQ), (proj, 128, COL_CKV), w["q_a_gain"], w["kv_a_gain"],
                         ("whole", w["w_q_b"]), ("whole", w["w_kv_b"]), rope["cos_q"], rope["sin_k64_t"], rope["cos_k64"], rope["sin_k64_t"]],
        [(256, BF16), ("sum", 256), (128, BF16), ("sum", 128), (128, BF16), ("sum", (MLA_Q_LORA, 1024)), ("sum", (MLA_KV_LORA, 1536))],
        name="mla_inputs_bwd", rows=s)
    grads["w_q_b"], grads["w_kv_b"] = g_w_q_b.astype(BF16), g_w_kv_b.astype(BF16)

    sent = emit({n: grads[n] for n in ("w_mem_kv", "w_q_b", "w_kv_b")})

    dproj = jnp.concatenate(
        [dc_q, dc_kv, dk_rope, d_gate["mla"], d_gate["sb"], d_gate["mem"], dq_sb.astype(BF16), dk_sb.astype(BF16),
         dv_sb.astype(BF16), dq_m.astype(BF16)], axis=1)
    grads["w_in"] = _mm(xb, dproj, "tn", name="g_w_in", out_dtype=BF16, behind=sent)
    sent = emit({"w_in": grads["w_in"]})
    grad_x = _mm(dproj, w["w_in"], "nt", name="grad_x", add=dx, behind=sent)
    return loss, grad_x, grads


def _shard_shape(shape, axis):
    return tuple(d // N_DEV if a == axis else d for a, d in enumerate(shape))


def _from_blocks(blocks, name):
    shape, axis = SHARDED[name]
    return blocks.reshape(shape) if axis == 0 else blocks.transpose(1, 0, 2).reshape(shape)


def _to_blocks(full, name):
    shape, axis = SHARDED[name]
    shp = _shard_shape(shape, axis)
    return full.reshape(N_DEV, *shp) if axis == 0 else full.reshape(shape[0], N_DEV, shp[1]).transpose(1, 0, 2)


def _pad_heads(a, used):
    rows = a.shape[0]
    a = a.reshape(rows, MLA_HEADS, used)
    return jnp.concatenate([a, jnp.zeros((rows, MLA_HEADS, LANES - used), a.dtype)], axis=2).reshape(rows, MLA_HEADS * LANES)


def _to_kernel_layout(name, full):
    if name == "w_in":
        return jnp.concatenate([jnp.zeros((D_MODEL, IN_PAD), full.dtype) if piece is None else full[:, piece[0]:piece[0] + piece[1]]
                                for piece in IN_PIECES], axis=1)
    if name == "w_q_b":
        return _pad_heads(full, MLA_NOPE + MLA_ROPE)
    if name == "w_kv_b":
        kv = full.reshape(MLA_KV_LORA, MLA_HEADS, MLA_NOPE + MLA_V)
        return jnp.concatenate([_pad_heads(kv[:, :, :MLA_NOPE].reshape(MLA_KV_LORA, -1), MLA_NOPE),
                                kv[:, :, MLA_NOPE:].reshape(MLA_KV_LORA, -1)], axis=1)
    return full


def _from_kernel_layout(name, g):
    if name == "w_in":
        placed, at = [], 0
        for piece in IN_PIECES:
            if piece is not None:
                placed.append((piece[0], g[:, at:at + piece[1]]))
            at += IN_PAD if piece is None else piece[1]
        return jnp.concatenate([cols for _, cols in sorted(placed, key=lambda item: item[0])], axis=1)
    if name == "w_q_b":
        return g.reshape(MLA_Q_LORA, MLA_HEADS, LANES)[:, :, :MLA_NOPE + MLA_ROPE].reshape(MLA_Q_LORA, -1)
    if name == "w_kv_b":
        return jnp.concatenate([g[:, :1024].reshape(MLA_KV_LORA, MLA_HEADS, LANES)[:, :, :MLA_NOPE],
                                g[:, 1024:].reshape(MLA_KV_LORA, MLA_HEADS, MLA_V)], axis=2).reshape(MLA_KV_LORA, -1)
    return g


def _pack_small(vectors, loss=None):
    flat = [v.reshape(-1) for v in vectors]
    flat.append(jnp.zeros((SMALL_ROWS * SMALL_LANES - LOSS_INDEX,), F32) if loss is None else
                jnp.concatenate([loss.reshape(-1)[:1], jnp.zeros((SMALL_ROWS * SMALL_LANES - LOSS_INDEX - 1,), F32)]))
    return jnp.concatenate(flat).reshape(SMALL_ROWS, SMALL_LANES)


def _unpack_small(packed):
    flat, res, off = packed.reshape(-1), [], 0
    for _, n in SMALL:
        res.append(flat[off:off + n].reshape(1, n))
        off += n
    return res


def _me_and_peers():
    x, y, c = lax.axis_index("x"), lax.axis_index("y"), lax.axis_index("c")
    peers = []
    for kk in range(1, N_DEV):
        px, py, pc = (x + (kk >> 2)) % 2, (y + ((kk >> 1) & 1)) % 2, (c + (kk & 1)) % 2
        peers.append(((px, py, pc), 4 * px + 2 * py + pc))
    return 4 * x + 2 * y + c, peers


def _share_small(small, *, name):
    def body(small_ref, all_ref, send_sems, recv_sems, local_sem):
        me, peers = _me_and_peers()
        copies = [pltpu.make_async_remote_copy(src_ref=small_ref, dst_ref=all_ref.at[me], send_sem=send_sems.at[kk], recv_sem=recv_sems.at[kk],
                                               device_id=pos, device_id_type=pl.DeviceIdType.MESH) for kk, (pos, _) in enumerate(peers)]
        copies.append(pltpu.make_async_copy(small_ref, all_ref.at[me], local_sem))
        for cp in copies:
            cp.start()
        for cp in copies:
            cp.wait()

    hbm = pl.BlockSpec(memory_space=pl.ANY)
    return pl.pallas_call(
        body, name=name, in_specs=[hbm], out_specs=hbm, out_shape=jax.ShapeDtypeStruct((N_DEV, *small.shape), small.dtype),
        scratch_shapes=[pltpu.SemaphoreType.DMA((N_DEV - 1,)), pltpu.SemaphoreType.DMA((N_DEV - 1,)), pltpu.SemaphoreType.DMA],
        compiler_params=pltpu.CompilerParams(has_side_effects=True),
    )(small)


_HBM = pl.BlockSpec(memory_space=pltpu.HBM)
_SEM = pl.BlockSpec(memory_space=pltpu.SEMAPHORE)


def _exchange_copies(srcs, zones, send_sems, recv_sems, gather):
    me, peers = _me_and_peers()
    return [pltpu.make_async_remote_copy(
        src_ref=srcs[t] if gather else srcs[t].at[peer], dst_ref=zones[t].at[me], send_sem=send_sems.at[7 * t + kk],
        recv_sem=recv_sems.at[7 * t + kk], device_id=pos, device_id_type=pl.DeviceIdType.MESH)
        for t in range(len(srcs)) for kk, (pos, peer) in enumerate(peers)]


def _exchange_start(tensors, *, gather, name):
    n = len(tensors)
    zones = [lax.empty((N_DEV, *(t.shape if gather else t.shape[1:])), t.dtype) for t in tensors]

    def body(*refs):
        for cp in _exchange_copies(refs[:n], refs[n:2 * n], refs[2 * n], refs[2 * n + 1], gather):
            cp.start()
        refs[-1][...] = jnp.zeros_like(refs[-1])

    buffers = [pltpu.HBM(a.shape, a.dtype) for a in tensors + zones]
    res = pl.pallas_call(
        body, name=name, in_specs=[_HBM] * (2 * n),
        out_shape=(pltpu.SemaphoreType.DMA((7 * n,)), pltpu.SemaphoreType.DMA((7 * n,)), *buffers, jax.ShapeDtypeStruct((8, LANES), F32)),
        out_specs=(_SEM, _SEM, *[_HBM] * (2 * n), pl.BlockSpec(memory_space=pltpu.VMEM)),
        input_output_aliases={i: 2 + i for i in range(2 * n)},
        compiler_params=pltpu.CompilerParams(has_side_effects=pltpu.SideEffectType.DATAFLOW_SIDE_EFFECTING),
    )(*[pltpu.with_memory_space_constraint(a, pltpu.HBM) for a in tensors + zones])
    return dict(sems=res[:2], buffers=res[2:2 + 2 * n], gather=gather, started=res[-1])


def _exchange_wait(started, after, *, name):
    n = len(started["buffers"]) // 2

    def body(*refs):
        for cp in _exchange_copies(refs[:n], refs[n:2 * n], refs[2 * n], refs[2 * n + 1], started["gather"]):
            cp.wait_send()
            cp.wait_recv()

    res = pl.pallas_call(
        body, name=name, in_specs=[_HBM] * (2 * n) + [_SEM, _SEM] + [_ANY] * len(after),
        out_shape=tuple(pltpu.HBM(a.shape, a.dtype) for a in started["buffers"]), out_specs=tuple([_HBM] * (2 * n)),
        input_output_aliases={i: i for i in range(2 * n)},
        compiler_params=pltpu.CompilerParams(has_side_effects=pltpu.SideEffectType.DATAFLOW_SIDE_EFFECTING),
    )(*started["buffers"], *started["sems"], *after)
    return res[:n], res[n:]


def _adamw(contrib, w, m, v, *, name):
    rows, cols = w.shape
    tile = min(rows, ADAM_ROWS)

    def body(c_ref, w_ref, m_ref, v_ref, g_ref, d_ref, nm_ref, nv_ref):
        g = c_ref[0].astype(F32)
        for s in range(1, N_DEV):
            g = g + c_ref[s].astype(F32)
        m_new = ADAM_B1 * m_ref[...] + (1.0 - ADAM_B1) * g
        v_new = ADAM_B2 * v_ref[...] + (1.0 - ADAM_B2) * (g * g)
        m_hat = m_new / (1.0 - ADAM_B1 ** ADAM_STEP)
        v_hat = v_new / (1.0 - ADAM_B2 ** ADAM_STEP)
        g_ref[...] = g
        d_ref[...] = -ADAM_LR * (m_hat / (jnp.sqrt(v_hat) + ADAM_EPS) + ADAM_WD * w_ref[...])
        nm_ref[...] = m_new
        nv_ref[...] = v_new

    spec = pl.BlockSpec((tile, cols), lambda i: (i, 0))
    return pl.pallas_call(
        body, name=name, grid=(rows // tile,),
        in_specs=[pl.BlockSpec((N_DEV, tile, cols), lambda i: (0, i, 0)), spec, spec, spec], out_specs=[spec] * 4,
        out_shape=[jax.ShapeDtypeStruct((rows, cols), F32)] * 4, compiler_params=_cparams("parallel"),
    )(contrib, w, m, v)


class _Weights:
    def __init__(self, gathers, vectors, me):
        self.gathers, self.ready, self.me, self.after = gathers, dict(vectors), me, ()

    def arrive_after(self, *values):
        self.after = values

    def __getitem__(self, name):
        if name not in self.ready:
            gi = next(i for i, group in enumerate(GATHER_GROUPS) if name in group)
            after = [*self.after, *[g["started"] for g in self.gathers]]
            shards, zones = _exchange_wait(self.gathers[gi], after, name=f"gather_wait_{gi}")
            for n, shard, zone in zip(GATHER_GROUPS[gi], shards, zones, strict=True):
                blocks = lax.dynamic_update_slice_in_dim(zone, shard[None], self.me, 0)
                self.ready[n] = _to_kernel_layout(n, _from_blocks(blocks, n))
        return self.ready[name]


def kernel(x, mem, w_in, w_mem_kv, q_a_gain, w_q_b, kv_a_gain, w_kv_b, w_branch_mla, w_branch_sb, w_branch_mem, w_merge_gate, b_merge_gate, w_out, ln_gain, ln_bias, loss_target, m_w_in, m_w_mem_kv, m_q_a_gain, m_w_q_b, m_kv_a_gain, m_w_kv_b, m_w_branch_mla, m_w_branch_sb, m_w_branch_mem, m_w_merge_gate, m_b_merge_gate, m_w_out, m_ln_gain, m_ln_bias, v_w_in, v_w_mem_kv, v_q_a_gain, v_w_q_b, v_kv_a_gain, v_w_kv_b, v_w_branch_mla, v_w_branch_sb, v_w_branch_mem, v_w_merge_gate, v_b_merge_gate, v_w_out, v_ln_gain, v_ln_bias):
    given = dict(locals())
    small_names = [n for n, _ in SMALL]
    smalls = lambda prefix: [given[prefix + n] for n in small_names]
    me = 4 * lax.axis_index("x") + 2 * lax.axis_index("y") + lax.axis_index("c")

    gathers = [_exchange_start([given[n][0].astype(BF16) for n in group], gather=True, name=f"gather_start_{gi}")
               for gi, group in enumerate(GATHER_GROUPS)]
    w = _Weights(gathers, {n: given[n] for n in small_names}, me)
    exchanges = []
    results = [{}, {}, {}, {}]

    def finish(gi, after):
        names, started = exchanges[gi]
        sent, zones = _exchange_wait(started, after, name=f"grads_wait_{gi}")
        done = []
        for n, blocks, zone in zip(names, sent, zones, strict=True):
            own = lax.dynamic_index_in_dim(blocks, me, 0, keepdims=True)
            contrib = lax.dynamic_update_slice_in_dim(zone, own, me, 0)
            outs = _adamw(contrib, given[n][0], given["m_" + n][0], given["v_" + n][0], name=f"adamw_{n}")
            for kind, res in zip(results, outs, strict=True):
                kind[n] = res[None]
            done.append(outs[1])
        return done

    def emit(grads):
        blocks = [_to_blocks(_from_kernel_layout(n, g), n).astype(BF16) for n, g in grads.items()]
        exchanges.append((tuple(grads), _exchange_start(blocks, gather=False, name=f"grads_start_{len(exchanges)}")))
        started = [exchanges[-1][1]["started"]]
        if len(exchanges) == len(GRAD_GROUPS):
            for gi in range(len(GRAD_GROUPS) - 1):
                started += finish(gi, started[:1])
        return started

    loss, grad_x, grads = _local_step(x[0], mem[0], loss_target[0], w, emit)

    contrib_small = _share_small(_pack_small([grads[n] for n in small_names], loss), name="share_small")
    sml = _adamw(contrib_small, _pack_small(smalls("")), _pack_small(smalls("m_")), _pack_small(smalls("v_")), name="adamw_small")
    for kind, packed in zip(results, sml, strict=True):
        kind.update(zip(small_names, _unpack_small(packed), strict=True))
    finish(len(GRAD_GROUPS) - 1, [grad_x])
    order = ["w_in", "w_mem_kv", "q_a_gain", "w_q_b", "kv_a_gain", "w_kv_b", "w_branch_mla", "w_branch_sb", "w_branch_mem",
             "w_merge_gate", "b_merge_gate", "w_out", "ln_gain", "ln_bias"]
    loss_out = sml[0].reshape(-1)[LOSS_INDEX]
    return (loss_out, grad_x[None], *[kind[n] for kind in results for n in order])
```

```python
import math

import jax
import jax.numpy as jnp
from jax import lax
from jax.experimental import pallas as pl
from jax.experimental.pallas import tpu as pltpu

F32, BF16 = jnp.float32, jnp.bfloat16

N_DEV = 8
D_MODEL = 1024
MLA_HEADS, MLA_NOPE, MLA_ROPE, MLA_V = 8, 64, 32, 64
MLA_Q_LORA, MLA_KV_LORA = 256, 128
SB_HEAD_DIM = 64
MEM_HEAD_DIM = 128
ROPE_BASE = 10000.0
RMS_EPS = 1e-6
LN_EPS = 1e-5
DEEPNORM_ALPHA = 2.0 ** 0.25
ADAM_LR, ADAM_B1, ADAM_B2, ADAM_EPS, ADAM_WD, ADAM_STEP = 0.001, 0.9, 0.999, 1e-08, 0.01, 10
LOG2E, LN2 = math.log2(math.e), math.log(2.0)

LANES = 128
GROUPS = 4
PROJ_WIDTH = 4096
COL_CQ, COL_CKV, COL_KROPE, COL_GATE_A, COL_GATE_B, COL_GATE_M = 0, 256, 384, 512, 1024, 1536
QKV_FIRST, QKV_WIDTH = 2048, 2048
COL_QB, COL_KB, COL_VB, COL_QM = 0, 512, 1024, 1536
IN_PIECES = ((0, 416), None, (416, 512), (2464, 512), (3488, 512), (928, 512), (1440, 512), (1952, 512), (2976, 512))
IN_PAD = 96

VMEM_LIMIT_BYTES = 56 * 1024 * 1024
NEG_BIG = -1e30
Q_BLOCK = 512
MEM_Q_BLOCK = 2048
SB_BWD_Q_BLOCK = 512
TRI_BLOCK = 256
TILE_ROWS = 64
KEY_CHUNK = 512

SHARDED = {
    "w_in": ((1024, 4000), 1), "w_mem_kv": ((1024, 1024), 0), "w_q_b": ((256, 768), 1), "w_kv_b": ((128, 1024), 1),
    "w_branch_mla": ((512, 1024), 1), "w_branch_sb": ((512, 1024), 1), "w_branch_mem": ((512, 1024), 1),
    "w_merge_gate": ((1024, 3072), 1), "w_out": ((1024, 1024), 0),
}
GATHER_GROUPS = (("w_in",), ("w_merge_gate",), ("w_q_b", "w_kv_b", "w_mem_kv", "w_branch_mla", "w_branch_sb", "w_branch_mem", "w_out"))
GRAD_GROUPS = (("w_out", "w_merge_gate", "w_branch_mla", "w_branch_sb", "w_branch_mem"), ("w_mem_kv", "w_q_b", "w_kv_b"), ("w_in",))
SMALL = (("q_a_gain", 256), ("kv_a_gain", 128), ("b_merge_gate", 3072), ("ln_gain", 1024), ("ln_bias", 1024))
SMALL_ROWS, SMALL_LANES = 48, 128
ADAM_ROWS = 256
LOSS_INDEX = 5504


def _cparams(*sem):
    return pltpu.CompilerParams(dimension_semantics=sem or None, vmem_limit_bytes=VMEM_LIMIT_BYTES)


_DIMS = {"nn": (((1,), (0,)), ((), ())), "nt": (((1,), (1,)), ((), ())), "tn": (((0,), (0,)), ((), ()))}


def _dot(a, b, dims):
    return lax.dot_general(a, b, _DIMS[dims], preferred_element_type=F32)


def _tile(dim, want):
    if dim <= want:
        return dim
    t = want - want % LANES
    while dim % t:
        t -= LANES
    assert t > 0, (dim, want)
    return t


_ANY = pl.BlockSpec(memory_space=pl.ANY)


def _mm(a, b, dims, *, name, out_dtype=F32, add=None, add_scale=1.0, col_scale=None, b_cols=None, behind=None,
        tm=1024, tn=1024, tk=1024):
    batch = a.shape[0] if a.ndim == 3 else None
    if dims == "nn":
        (m, k), (k2, n) = a.shape[-2:], b.shape[-2:]
    elif dims == "nt":
        (m, k), (n, k2) = a.shape[-2:], b.shape[-2:]
    else:
        (k, m), (k2, n) = a.shape[-2:], b.shape[-2:]
    assert k == k2 and a.ndim == b.ndim, (a.shape, b.shape, dims)
    assert batch is None or (b.shape[0] == batch and add is None and col_scale is None and b_cols is None)
    b_first = 0
    if b_cols is not None:
        assert dims == "nn"
        b_first, n = b_cols
    tm, tn, tk = _tile(m, tm), _tile(n, tn), _tile(k, tk)
    assert b_first % tn == 0
    jb = b_first // tn
    nk = k // tk

    def spec(block, index):
        if batch is None:
            return pl.BlockSpec(block, lambda bb, i, j, kk: index(i, j, kk))
        return pl.BlockSpec((None, *block), lambda bb, i, j, kk: (bb, *index(i, j, kk)))

    a_spec = spec((tk, tm), lambda i, j, kk: (kk, i)) if dims == "tn" else spec((tm, tk), lambda i, j, kk: (i, kk))
    b_spec = spec((tn, tk), lambda i, j, kk: (j, kk)) if dims == "nt" else spec((tk, tn), lambda i, j, kk: (kk, jb + j))
    o_spec = spec((tm, tn), lambda i, j, kk: (i, j))
    behind = [] if behind is None else behind if isinstance(behind, (list, tuple)) else [behind]
    optional = [(add, o_spec), (col_scale, pl.BlockSpec((1, tn), lambda bb, i, j, kk: (0, j))), *[(v, _ANY) for v in behind]]
    present = [(v, spec) for v, spec in optional if v is not None]

    def body(*refs):
        a_ref, b_ref = refs[:2]
        extra = iter(refs[2:2 + len(present)])
        add_ref = next(extra) if add is not None else None
        scale_ref = next(extra) if col_scale is not None else None
        o_ref = refs[2 + len(present)]
        part = _dot(a_ref[...].astype(BF16), b_ref[...].astype(BF16), dims)

        def finish(r):
            if add is not None:
                r = r + add_scale * add_ref[...]
            if col_scale is not None:
                r = r * scale_ref[...]
            o_ref[...] = r.astype(out_dtype)

        if nk == 1:
            finish(part)
            return
        acc = refs[-1]
        kk = pl.program_id(3)

        @pl.when(kk == 0)
        def _():
            acc[...] = part

        @pl.when(kk > 0)
        def _():
            acc[...] += part

        @pl.when(kk == nk - 1)
        def _():
            finish(acc[...])

    return pl.pallas_call(
        body, name=name, grid=(batch or 1, m // tm, n // tn, nk),
        in_specs=[a_spec, b_spec] + [spec for _, spec in present], out_specs=o_spec,
        out_shape=jax.ShapeDtypeStruct((m, n) if batch is None else (batch, m, n), out_dtype),
        scratch_shapes=[pltpu.VMEM((tm, tn), F32)] if nk > 1 else [],
        compiler_params=_cparams("parallel", "parallel", "parallel", "arbitrary"),
    )(a, b, *[v for v, _ in present])


def _rowwise(fn, ins, outs, *, name, rows, tr=512):
    n_in = len(ins)
    tr = min(tr, rows)
    in_specs, args = [], []
    for it in ins:
        if isinstance(it, tuple) and it[0] == "whole":
            in_specs.append(pl.BlockSpec(it[1].shape, lambda i, nd=it[1].ndim: (0,) * nd))
            args.append(it[1])
            continue
        arr, w, off = it if isinstance(it, tuple) else (it, it.shape[-1], 0)
        assert off % w == 0
        cb = off // w
        if arr.ndim == 3:
            in_specs.append(pl.BlockSpec((arr.shape[0], tr, w), lambda i, cb=cb: (0, i, cb)))
        elif arr.shape[0] == 1:
            in_specs.append(pl.BlockSpec((1, w), lambda i, cb=cb: (0, cb)))
        else:
            in_specs.append(pl.BlockSpec((tr, w), lambda i, cb=cb: (i, cb)))
        args.append(arr)
    out_shape, out_specs, is_sum = [], [], []
    for out in outs:
        is_sum.append(out[0] == "sum")
        if out[0] == "sum":
            shape = out[1] if isinstance(out[1], tuple) else (1, out[1])
            out_shape.append(jax.ShapeDtypeStruct(shape, F32))
            out_specs.append(pl.BlockSpec(shape, lambda i: (0, 0)))
        elif len(out) == 3:
            out_shape.append(jax.ShapeDtypeStruct((out[0], rows, out[1]), out[2]))
            out_specs.append(pl.BlockSpec((out[0], tr, out[1]), lambda i: (0, i, 0)))
        else:
            out_shape.append(jax.ShapeDtypeStruct((rows, out[0]), out[1]))
            out_specs.append(pl.BlockSpec((tr, out[0]), lambda i: (i, 0)))

    def body(*refs):
        res = fn(*[r[...] for r in refs[:n_in]])
        for r, val, s in zip(refs[n_in:], res, is_sum, strict=True):
            if s:
                @pl.when(pl.program_id(0) == 0)
                def _(r=r):
                    r[...] = jnp.zeros_like(r)

                r[...] += val
            elif isinstance(val, (list, tuple)):
                for n, part in enumerate(val):
                    r[n] = part.astype(r.dtype)
            else:
                r[...] = val.astype(r.dtype)

    return pl.pallas_call(
        body, name=name, grid=(rows // tr,), in_specs=in_specs, out_specs=out_specs, out_shape=out_shape,
        compiler_params=_cparams("arbitrary"),
    )(*args)


def _colsum(v):
    return jnp.sum(v, axis=0, keepdims=True)


def _sigmoid(v):
    return 1.0 / (1.0 + jnp.exp(-v))


def _lane_groups(v):
    return [v[:, g * LANES:(g + 1) * LANES] for g in range(v.shape[1] // LANES)]


def _swap_halves(v, first_lane):
    lane = lax.broadcasted_iota(jnp.int32, v.shape, 1)
    return jnp.where(lane < first_lane + 16, pltpu.roll(v, 112, axis=1), pltpu.roll(v, 16, axis=1))


def _lane_sum(acc, v):
    for part in _lane_groups(v):
        acc = acc + part
    return acc


def _low_half(shape):
    return lax.broadcasted_iota(jnp.int32, shape, 1) < LANES // 2


def _select_heads(per_head, pick):
    if len(per_head) == 1:
        return pick(per_head[0], 0)
    return jnp.where(_low_half(per_head[0].shape), pick(per_head[0], 0), pick(per_head[1], 1))


def _attn_specs(s, sk, hp, bq, q0, k0, v0):
    wq = hp * LANES
    assert q0 % wq == 0 and k0 % wq == 0 and v0 % LANES == 0
    qb0, kb0, vb0 = q0 // wq, k0 // wq, v0 // LANES
    q_spec = pl.BlockSpec((bq, wq), lambda g, i: (i, qb0 + g))
    k_spec = pl.BlockSpec((sk, wq), lambda g, i: (0, kb0 + g))
    v_spec = pl.BlockSpec((sk, LANES), lambda g, i: (0, vb0 + g))
    row_out = lambda w: pl.BlockSpec((bq, w), lambda g, i: (i, g))
    key_out = lambda w: pl.BlockSpec((sk, w), lambda g, i: (0, g))
    return q_spec, k_spec, v_spec, row_out, key_out


def _chunks(i, bq, ch, sk, causal):
    return ((i + 1) * bq - 1) // ch if causal else jnp.int32(sk // ch - 1)


def _positions(i, c, bq, ch):
    return (i * bq + lax.broadcasted_iota(jnp.int32, (bq, ch), 0), c * ch + lax.broadcasted_iota(jnp.int32, (bq, ch), 1))


def _softmax_fwd(q, k, v, *, hp, causal, name, q0=0, k0=0, v0=0, q_rows=Q_BLOCK):
    s, sk = q.shape[0], k.shape[0]
    bq, ch = min(q_rows, s), min(KEY_CHUNK, sk)
    assert not causal or bq <= ch
    q_spec, k_spec, v_spec, row_out, _ = _attn_specs(s, sk, hp, bq, q0, k0, v0)

    def body(q_ref, k_ref, v_ref, o_ref, lse_ref, s_scr):
        i = pl.program_id(1)
        qs = _lane_groups(q_ref[...])
        last = _chunks(i, bq, ch, sk, causal)

        def scores(c, ms, masked):
            off = pl.multiple_of(c * ch, ch)
            out = []
            for j in range(hp):
                sc = _dot(qs[j], k_ref[pl.ds(off, ch), j * LANES:(j + 1) * LANES], "nt")
                if masked:
                    qpos, kpos = _positions(i, c, bq, ch)
                    sc = jnp.where(kpos <= qpos, sc, NEG_BIG)
                s_scr[j, c] = sc
                m = ms[j]
                for part in _lane_groups(sc):
                    m = jnp.maximum(m, part)
                out.append(m)
            return tuple(out)

        ms = lax.fori_loop(0, last, lambda c, m: scores(c, m, False), tuple(jnp.full((bq, LANES), NEG_BIG, F32) for _ in range(hp)))
        ms = scores(last, ms, causal)
        row_max = [jnp.max(m, axis=1, keepdims=True) for m in ms]

        def weigh(c, carry):
            off = pl.multiple_of(c * ch, ch)
            vt = v_ref[pl.ds(off, ch), :]
            out = []
            for j in range(hp):
                l, acc = carry[j]
                p = jnp.exp2(s_scr[j, c] - row_max[j])
                out.append((_lane_sum(l, p), acc + _dot(p.astype(BF16), vt, "nn")))
            return tuple(out)

        zero = jnp.zeros((bq, LANES), F32)
        res = lax.fori_loop(0, last + 1, weigh, tuple((zero, zero) for _ in range(hp)))
        row_sum = [jnp.sum(l, axis=1, keepdims=True) for l, _ in res]
        o_ref[...] = _select_heads([acc for _, acc in res], lambda acc, j: acc / row_sum[j])
        lse_ref[...] = _select_heads([jnp.broadcast_to(row_max[j] + jnp.log2(row_sum[j]), (bq, LANES)) for j in range(hp)], lambda a, j: a)

    return pl.pallas_call(
        body, name=name, grid=(GROUPS, s // bq), in_specs=[q_spec, k_spec, v_spec], out_specs=[row_out(LANES), row_out(LANES)],
        out_shape=[jax.ShapeDtypeStruct((s, GROUPS * LANES), F32)] * 2,
        scratch_shapes=[pltpu.VMEM((hp, sk // ch, bq, ch), F32)], compiler_params=_cparams("parallel", "arbitrary"),
    )(q, k, v)


def _head_cotangent(do, j, hp):
    if hp == 1:
        return do
    return jnp.where(_low_half(do.shape) == (j == 0), do, 0.0)


def _softmax_bwd(q, k, v, o, do, lse, behind, *, hp, causal, dq_scale, name, q0=0, k0=0, v0=0, q_rows=Q_BLOCK):
    s, sk = q.shape[0], k.shape[0]
    bq, ch = min(q_rows, s), min(KEY_CHUNK, sk)
    assert not causal or bq <= ch
    wq = hp * LANES
    q_spec, k_spec, v_spec, row_out, key_out = _attn_specs(s, sk, hp, bq, q0, k0, v0)

    def body(q_ref, k_ref, v_ref, o_ref, do_ref, lse_ref, _, dq_ref, dk_ref, dv_ref, dk_t, dv_t):
        i = pl.program_id(1)

        @pl.when(i == 0)
        def _():
            dk_t[...] = jnp.zeros_like(dk_t)
            dv_t[...] = jnp.zeros_like(dv_t)

        qs = _lane_groups(q_ref[...])
        do_all, o_all, lse_all = do_ref[...], o_ref[...], lse_ref[...]
        dos, deltas, lses = [], [], []
        for j in range(hp):
            d = _head_cotangent(do_all, j, hp)
            deltas.append(jnp.sum(d * o_all, axis=1, keepdims=True))
            dos.append(d.astype(BF16))
            lses.append(lse_all[:, j * (LANES // hp):j * (LANES // hp) + 1])
        last = _chunks(i, bq, ch, sk, causal)

        def chunk(c, dqs, masked):
            off = pl.multiple_of(c * ch, ch)
            vt = v_ref[pl.ds(off, ch), :]
            out, dks, dv = [], [], None
            for j in range(hp):
                kt = k_ref[pl.ds(off, ch), j * LANES:(j + 1) * LANES]
                p = jnp.exp2(_dot(qs[j], kt, "nt") - lses[j])
                if masked:
                    qpos, kpos = _positions(i, c, bq, ch)
                    p = jnp.where(kpos <= qpos, p, 0.0)
                ds = (p * (_dot(dos[j], vt, "nt") - deltas[j]) * LN2).astype(BF16)
                out.append(dqs[j] + _dot(ds, kt, "nn"))
                dks.append(_dot(qs[j], ds, "tn"))
                dvj = _dot(dos[j], p.astype(BF16), "tn")
                dv = dvj if dv is None else dv + dvj
            dk_t[c] += dks[0] if hp == 1 else jnp.concatenate(dks, axis=0)
            dv_t[c] += dv
            return tuple(out)

        dqs = lax.fori_loop(0, last, lambda c, d: chunk(c, d, False), tuple(jnp.zeros((bq, LANES), F32) for _ in range(hp)))
        dqs = chunk(last, dqs, causal)
        dq_ref[...] = (dqs[0] if hp == 1 else jnp.concatenate(dqs, axis=1)) * dq_scale

        @pl.when(i == s // bq - 1)
        def _():
            for c in range(sk // ch):
                dk_ref[c * ch:(c + 1) * ch, :] = dk_t[c].T
                dv_ref[c * ch:(c + 1) * ch, :] = dv_t[c].T

    return pl.pallas_call(
        body, name=name, grid=(GROUPS, s // bq),
        in_specs=[q_spec, k_spec, v_spec, row_out(LANES), row_out(LANES), row_out(LANES), _ANY],
        out_specs=[row_out(wq), key_out(wq), key_out(LANES)],
        out_shape=[jax.ShapeDtypeStruct((s, GROUPS * wq), F32), jax.ShapeDtypeStruct((sk, GROUPS * wq), F32),
                   jax.ShapeDtypeStruct((sk, GROUPS * LANES), F32)],
        scratch_shapes=[pltpu.VMEM((sk // ch, wq, ch), F32), pltpu.VMEM((sk // ch, LANES, ch), F32)],
        compiler_params=_cparams("arbitrary", "arbitrary"),
    )(q, k, v, o, do, lse, behind)


def _log2_sigmoid_pair(z2):
    minus_abs = lax.bitcast_convert_type(lax.bitcast_convert_type(z2, jnp.uint32) | jnp.uint32(0x80000000), F32)
    log_beta = jnp.minimum(z2, 0.0) - jnp.log2(1.0 + jnp.exp2(minus_abs))
    return log_beta, log_beta - z2


def _tilewise(fn, *arrays):
    rows, cols = arrays[0].shape
    step = min(TILE_ROWS, rows)
    grid = [[fn(*[None if a is None else a[r:r + step, c:c + LANES] for a in arrays]) for c in range(0, cols, LANES)]
            for r in range(0, rows, step)]
    return [jnp.concatenate([jnp.concatenate([cell[k] for cell in row], axis=1) for row in grid], axis=0)
            for k in range(len(grid[0][0]))]


def _split(v):
    hi = v.astype(BF16)
    return hi, (v - hi.astype(F32)).astype(BF16)


def _tri(n, after):
    rows, cols = lax.broadcasted_iota(jnp.int32, (n, n), 0), lax.broadcasted_iota(jnp.int32, (n, n), 1)
    return (rows > cols if after else rows < cols).astype(BF16)


def _running_sums(v, terms, start, tri, backwards):
    n = tri.shape[0]
    n_blocks = v.shape[1] // n
    order = range(n_blocks - 1, -1, -1) if backwards else range(n_blocks)
    stacked = tri if len(terms) == 1 else jnp.concatenate([tri] * len(terms), axis=0)
    parts, run = [None] * n_blocks, start
    for t in order:
        cols = slice(t * n, (t + 1) * n)
        lhs = terms[0][:, cols] if len(terms) == 1 else jnp.concatenate([term[:, cols] for term in terms], axis=1)
        parts[t] = _dot(lhs, stacked, "nn") + run
        run = run + jnp.sum(v[:, cols], axis=1, keepdims=True)
    return (parts[0] if n_blocks == 1 else jnp.concatenate(parts, axis=1)), run


def _sb_weights(qm, kt, run, tri, strict):
    def logs(z2, keep):
        log_beta, log_keep = _log2_sigmoid_pair(z2)
        if keep is not None:
            log_keep = jnp.where(keep, log_keep, 0.0)
        return log_beta, log_keep, *_split(log_keep)

    log_beta, log_keep, hi, lo = _tilewise(logs, _dot(qm, kt, "nt"), strict)
    behind, run = _running_sums(log_keep, (hi, lo), run, tri, True)

    def weigh(log_beta, behind, keep):
        a = jnp.exp2(log_beta + behind)
        return (a if keep is None else jnp.where(keep, a, 0.0),)

    (a,) = _tilewise(weigh, log_beta, behind, strict)
    return a, log_beta, run


def _sb_queries(q_all):
    low = _low_half(q_all.shape)
    zero = jnp.zeros_like(q_all)
    return [jnp.where(low, q_all, zero), jnp.where(low, zero, q_all)]


def _sb_fwd(qkv, *, q0, k0, v0, name):
    s = qkv.shape[0]
    bq, ch = min(Q_BLOCK, s), min(KEY_CHUNK, s)
    q_spec, k_spec, v_spec, row_out, _ = _attn_specs(s, s, 1, bq, q0, k0, v0)

    def body(q_ref, k_ref, v_ref, o_ref):
        i = pl.program_id(1)
        qms = _sb_queries(q_ref[...])
        tri = _tri(min(TRI_BLOCK, ch), True)
        last = _chunks(i, bq, ch, s, True)

        def chunk(c, carry, masked):
            off = pl.multiple_of(c * ch, ch)
            kt, vt = k_ref[pl.ds(off, ch), :], v_ref[pl.ds(off, ch), :]
            strict = None
            if masked:
                qpos, kpos = _positions(i, c, bq, ch)
                strict = kpos < qpos
            out = []
            for j in range(2):
                run, acc = carry[j]
                a, _, run = _sb_weights(qms[j], kt, run, tri, strict)
                out.append((run, acc + _dot(a.astype(BF16), vt, "nn")))
            return tuple(out)

        carry = chunk(last, tuple((jnp.zeros((bq, 1), F32), jnp.zeros((bq, LANES), F32)) for _ in range(2)), True)
        res = lax.fori_loop(0, last, lambda n, c: chunk(last - 1 - n, c, False), carry)
        o_ref[...] = _select_heads([acc for _, acc in res], lambda acc, j: acc)

    return pl.pallas_call(
        body, name=name, grid=(GROUPS, s // bq), in_specs=[q_spec, k_spec, v_spec], out_specs=row_out(LANES),
        out_shape=jax.ShapeDtypeStruct((s, GROUPS * LANES), F32), compiler_params=_cparams("parallel", "arbitrary"),
    )(qkv, qkv, qkv)


def _sb_bwd(qkv, do, behind, *, q0, k0, v0, dq_scale, name):
    s = qkv.shape[0]
    bq, ch = min(SB_BWD_Q_BLOCK, s), min(KEY_CHUNK, s)
    q_spec, k_spec, v_spec, row_out, key_out = _attn_specs(s, s, 1, bq, q0, k0, v0)

    def body(q_ref, k_ref, v_ref, do_ref, _, dq_ref, dk_ref, dv_ref, g_s, beta_s):
        i = pl.program_id(1)

        @pl.when(i == 0)
        def _():
            dk_ref[...] = jnp.zeros_like(dk_ref)
            dv_ref[...] = jnp.zeros_like(dv_ref)

        qms = _sb_queries(q_ref[...])
        do_all = do_ref[...]
        dos = [_head_cotangent(do_all, j, 2).astype(BF16) for j in range(2)]
        dos_ln2 = [(_head_cotangent(do_all, j, 2) * LN2).astype(BF16) for j in range(2)]
        tri_after, tri_before = _tri(min(TRI_BLOCK, ch), True), _tri(min(TRI_BLOCK, ch), False)
        last = _chunks(i, bq, ch, s, True)

        def strict_mask(c):
            qpos, kpos = _positions(i, c, bq, ch)
            return kpos < qpos

        def sweep1(c, runs, masked):
            off = pl.multiple_of(c * ch, ch)
            kt, vt = k_ref[pl.ds(off, ch), :], v_ref[pl.ds(off, ch), :]
            strict = strict_mask(c) if masked else None
            out, dv = [], None
            for j in range(2):
                a, log_beta, run = _sb_weights(qms[j], kt, runs[j], tri_after, strict)
                g_s[j, c] = (a * _dot(dos_ln2[j], vt, "nt")).astype(BF16)
                beta_s[j, c] = jnp.exp2(log_beta).astype(BF16)
                dvj = _dot(a.astype(BF16), dos[j], "tn")
                dv = dvj if dv is None else dv + dvj
                out.append(run)
            dv_ref[pl.ds(off, ch), :] += dv
            return tuple(out)

        runs = sweep1(last, tuple(jnp.zeros((bq, 1), F32) for _ in range(2)), True)
        lax.fori_loop(0, last, lambda n, r: sweep1(last - 1 - n, r, False), runs)

        def sweep2(c, carry, masked):
            off = pl.multiple_of(c * ch, ch)
            kt = k_ref[pl.ds(off, ch), :]
            out, dk = [], None
            for j in range(2):
                before, dq = carry[j]
                g16, beta = g_s[j, c], beta_s[j, c].astype(F32)
                g = g16.astype(F32)
                in_front, before = _running_sums(g, (g16,), before, tri_before, False)
                dz = g * (1.0 - beta) - beta * in_front
                if masked:
                    dz = jnp.where(strict_mask(c), dz, 0.0)
                dz = dz.astype(BF16)
                dkj = _dot(dz, qms[j], "tn")
                dk = dkj if dk is None else dk + dkj
                out.append((before, dq + _dot(dz, kt, "nn")))
            dk_ref[pl.ds(off, ch), :] += dk
            return tuple(out)

        carry = lax.fori_loop(0, last, lambda c, cr: sweep2(c, cr, False),
                              tuple((jnp.zeros((bq, 1), F32), jnp.zeros((bq, LANES), F32)) for _ in range(2)))
        res = sweep2(last, carry, True)
        dq_ref[...] = _select_heads([dq for _, dq in res], lambda dq, j: dq) * dq_scale

    n_ch = s // ch
    return pl.pallas_call(
        body, name=name, grid=(GROUPS, s // bq), in_specs=[q_spec, k_spec, v_spec, row_out(LANES), _ANY],
        out_specs=[row_out(LANES), key_out(LANES), key_out(LANES)],
        out_shape=[jax.ShapeDtypeStruct((s, GROUPS * LANES), F32)] * 3,
        scratch_shapes=[pltpu.VMEM((2, n_ch, bq, ch), BF16)] * 2,
        compiler_params=_cparams("arbitrary", "arbitrary"),
    )(qkv, qkv, qkv, do, behind)


def _rope_tables(s):
    half = MLA_ROPE // 2
    freqs = ROPE_BASE ** (-jnp.arange(half, dtype=F32) / half)
    ang = jnp.arange(s, dtype=F32)[:, None] * freqs[None, :]
    cos, sin = jnp.cos(ang), jnp.sin(ang)
    tail = jnp.zeros((s, LANES - MLA_NOPE - MLA_ROPE), F32)
    lead = lambda fill: jnp.full((s, MLA_NOPE), fill, F32)
    return dict(
        cos_k0=jnp.concatenate([cos, cos, lead(0.0), tail], axis=1), sin_k0=jnp.concatenate([-sin, sin, lead(0.0), tail], axis=1),
        cos_k64=jnp.concatenate([lead(0.0), cos, cos, tail], axis=1), sin_k64=jnp.concatenate([lead(0.0), -sin, sin, tail], axis=1),
        cos_q=jnp.concatenate([lead(1.0), cos, cos, tail], axis=1),
        sin_k64_t=jnp.concatenate([lead(0.0), sin, -sin, tail], axis=1),
    )


def _local_step(x, mem, target, w, emit=lambda grads: [jnp.zeros((8, LANES), F32)]):
    s = x.shape[0]
    rope = _rope_tables(s)
    xb = x.astype(BF16)
    inv_d = 1.0 / D_MODEL
    scale_a = LOG2E / math.sqrt(MLA_NOPE + MLA_ROPE)
    scale_b = LOG2E / math.sqrt(SB_HEAD_DIM)
    scale_m = LOG2E / math.sqrt(MEM_HEAD_DIM)
    arrive_after = getattr(w, "arrive_after", lambda *values: None)
    memb = mem.astype(BF16)

    arrive_after(xb, memb, *rope.values())
    proj = _mm(xb, w["w_in"], "nn", name="proj", b_cols=(0, QKV_FIRST))
    one = jnp.ones((1, 512), F32)
    qkv = _mm(xb, w["w_in"], "nn", name="proj_qkv", b_cols=(QKV_FIRST, QKV_WIDTH), out_dtype=BF16,
              col_scale=jnp.concatenate([one * scale_b, one, one, one * scale_m], axis=1))
    arrive_after(qkv)
    pre = _mm(xb, w["w_merge_gate"], "nn", name="merge_pre", out_dtype=BF16)
    arrive_after(pre)

    def mla_inputs(c_q, c_kv, k_rope, g_q, g_kv, w_q, w_kv, cos_q, sin_q, cos_k, sin_k):
        n_q = (c_q * lax.rsqrt(jnp.mean(c_q * c_q, axis=1, keepdims=True) + RMS_EPS) * g_q).astype(BF16)
        n_kv = (c_kv * lax.rsqrt(jnp.mean(c_kv * c_kv, axis=1, keepdims=True) + RMS_EPS) * g_kv).astype(BF16)
        q_a = _dot(n_q, w_q, "nn")
        kv_a = _dot(n_kv, w_kv, "nn").astype(BF16)
        q = jnp.concatenate([(g * cos_q + _swap_halves(g, MLA_NOPE) * sin_q) * scale_a for g in _lane_groups(q_a)], axis=1)
        k_pe = pltpu.roll(k_rope * cos_k + _swap_halves(k_rope, 0) * sin_k, MLA_NOPE, axis=1).astype(BF16)
        k = jnp.concatenate([g + k_pe for g in _lane_groups(kv_a[:, :1024])], axis=1)
        return n_q, n_kv, q, k, kv_a[:, 1024:]

    n_q, n_kv, q_mla, k_mla, v_a = _rowwise(
        mla_inputs, [(proj, 256, COL_CQ), (proj, 128, COL_CKV), (proj, 128, COL_KROPE), w["q_a_gain"], w["kv_a_gain"],
                     ("whole", w["w_q_b"]), ("whole", w["w_kv_b"]), rope["cos_q"], rope["sin_k64"], rope["cos_k0"], rope["sin_k0"]],
        [(256, BF16), (128, BF16), (1024, BF16), (1024, BF16), (512, BF16)], name="mla_inputs", rows=s)
    o_a, lse_a = _softmax_fwd(q_mla, k_mla, v_a, hp=2, causal=True, name="mla_fwd")

    o_b = _sb_fwd(qkv, q0=COL_QB, k0=COL_KB, v0=COL_VB, name="sb_fwd")

    mem_kv =_mm(memb, w["w_mem_kv"], "nn", name="mem_kv", out_dtype=BF16)
    o_m, lse_m = _softmax_fwd(qkv, mem_kv, mem_kv, hp=1, causal=False, name="mem_fwd", q0=1536, v0=512, q_rows=MEM_Q_BLOCK)

    branches = ("mla", "sb", "mem")
    w_branch = jnp.stack([w[f"w_branch_{br}"] for br in branches])
    bias = w["b_merge_gate"]

    def head(oa, ob, om, ga, gb, gm, pa, pb, pm, ba, bb, bm, xv, tv, gain, bias_ln, w_b, w_o):
        us, ys, gs = [], [], []
        for n, (o, gate, p, b) in enumerate(((oa, ga, pa, ba), (ob, gb, pb, bb), (om, gm, pm, bm))):
            us.append((o * gate * _sigmoid(gate)).astype(BF16))
            ys.append(_dot(us[n], w_b[n], "nn"))
            gs.append(_sigmoid(p.astype(F32) + b))
        merged = (gs[0] * ys[0] + gs[1] * ys[1] + gs[2] * ys[2]).astype(BF16)
        z = DEEPNORM_ALPHA * xv + _dot(merged, w_o, "nn")
        zc = z - jnp.mean(z, axis=1, keepdims=True)
        rstd = lax.rsqrt(jnp.mean(zc * zc, axis=1, keepdims=True) + LN_EPS)
        xhat = zc * rstd
        err = xhat * gain + bias_ln - tv
        loss = 0.5 * jnp.sum(jnp.mean(err * err, axis=1, keepdims=True), axis=0, keepdims=True)
        dy = err * inv_d
        dxhat = dy * gain
        dz = rstd * (dxhat - jnp.mean(dxhat, axis=1, keepdims=True) - xhat * jnp.mean(dxhat * xhat, axis=1, keepdims=True))
        dz16 = dz.astype(BF16)
        dm = _dot(dz16, w_o, "nt")
        dpre = jnp.concatenate([dm * ys[n] * gs[n] * (1.0 - gs[n]) for n in range(3)], axis=1)
        dys = [(dm * gs[n]).astype(BF16) for n in range(3)]
        d_os, d_gates = [], []
        for n, (o, gate) in enumerate(((oa, ga), (ob, gb), (om, gm))):
            du, sg = _dot(dys[n], w_b[n], "nt"), _sigmoid(gate)
            d_os.append(du * gate * sg)
            d_gates.append(du * o * sg * (1.0 + gate * (1.0 - sg)))
        return (us, merged, dz, dz16, _colsum(dy * xhat), _colsum(dy), jnp.broadcast_to(loss, (1, LANES)), dpre, _colsum(dpre),
                dys, *d_os, *d_gates)

    grads = {}
    (u, merged, dz, dzb, grads["ln_gain"], grads["ln_bias"], loss, dpre, grads["b_merge_gate"], dy, *rest) = _rowwise(
        head, [o_a, o_b, o_m, (proj, 512, COL_GATE_A), (proj, 512, COL_GATE_B), (proj, 512, COL_GATE_M),
               (pre, 1024, 0), (pre, 1024, 1024), (pre, 1024, 2048), (bias, 1024, 0), (bias, 1024, 1024), (bias, 1024, 2048),
               x, target, w["ln_gain"], w["ln_bias"], ("whole", w_branch), ("whole", w["w_out"])],
        [(3, 512, BF16), (1024, BF16), (1024, F32), (1024, BF16), ("sum", 1024), ("sum", 1024), ("sum", LANES),
         (3072, BF16), ("sum", 3072), (3, 1024, BF16)] + [(512, F32)] * 3 + [(512, BF16)] * 3, name="head", rows=s, tr=256)
    d_o, d_gate = dict(zip(branches, rest[:3], strict=True)), dict(zip(branches, rest[3:], strict=True))

    grads["w_out"] = _mm(merged, dzb, "tn", name="g_w_out", out_dtype=BF16)
    grads["w_merge_gate"] = _mm(xb, dpre, "tn", name="g_w_merge", out_dtype=BF16)
    dx = _mm(dpre, w["w_merge_gate"], "nt", name="dx_merge", add=dz, add_scale=DEEPNORM_ALPHA)
    g_w_branch = _mm(u, dy, "tn", name="g_w_branch", out_dtype=BF16)
    for n, br in enumerate(branches):
        grads[f"w_branch_{br}"] = g_w_branch[n]
    (sent,) = emit({n: grads[n] for n in ("w_out", "w_merge_gate", "w_branch_mla", "w_branch_sb", "w_branch_mem")})

    dq_m, dk_m, dv_m = _softmax_bwd(qkv, mem_kv, mem_kv, o_m, d_o["mem"], lse_m, sent, hp=1, causal=False, dq_scale=scale_m,
                                    name="mem_bwd", q0=1536, v0=512, q_rows=MEM_Q_BLOCK)
    grads["w_mem_kv"] = _mm(memb, jnp.concatenate([dk_m, dv_m], axis=1), "tn", name="g_w_mem_kv", out_dtype=BF16)

    dq_sb, dk_sb, dv_sb = _sb_bwd(qkv, d_o["sb"], sent, q0=0, k0=512, v0=1024, dq_scale=scale_b, name="sb_bwd")

    dq_mla, dk_mla, dv_a = _softmax_bwd(q_mla, k_mla, v_a, o_a, d_o["mla"], lse_a, sent, hp=2, causal=True, dq_scale=scale_a,
                                        name="mla_bwd")

    def mla_inputs_bwd(dq, dk, dv, n_q, n_kv, c_q, c_kv, g_q, g_kv, w_q, w_kv, cos_q, sin_q, cos_k, sin_k):
        dq_a = jnp.concatenate([g * cos_q + _swap_halves(g, MLA_NOPE) * sin_q for g in _lane_groups(dq)], axis=1).astype(BF16)
        groups = _lane_groups(dk)
        g_rope = groups[0]
        for other in groups[1:]:
            g_rope = g_rope + other
        dk_rope = pltpu.roll(g_rope * cos_k + _swap_halves(g_rope, MLA_NOPE) * sin_k, MLA_NOPE, axis=1)
        nope = _low_half(g_rope.shape)
        dkv_a = jnp.concatenate([jnp.where(nope, grp, 0.0) for grp in groups] + [dv], axis=1).astype(BF16)
        res = []
        for c, dn, g in ((c_q, _dot(dq_a, w_q, "nt"), g_q), (c_kv, _dot(dkv_a, w_kv, "nt"), g_kv)):
            r = lax.rsqrt(jnp.mean(c * c, axis=1, keepdims=True) + RMS_EPS)
            t = dn * g
            res += [r * t - c * (r * r * r) * jnp.mean(c * t, axis=1, keepdims=True), _colsum(dn * c * r)]
        return *res, dk_rope, _dot(n_q, dq_a, "tn"), _dot(n_kv, dkv_a, "tn")

    dc_q, grads["q_a_gain"], dc_kv, grads["kv_a_gain"], dk_rope, g_w_q_b, g_w_kv_b = _rowwise(
        mla_inputs_bwd, [dq_mla, dk_mla, dv_a, n_q, n_kv, (proj, 256, COL_CQ), (proj, 128, COL_CKV), w["q_a_gain"], w["kv_a_gain"],
                         ("whole", w["w_q_b"]), ("whole", w["w_kv_b"]), rope["cos_q"], rope["sin_k64_t"], rope["cos_k64"], rope["sin_k64_t"]],
        [(256, BF16), ("sum", 256), (128, BF16), ("sum", 128), (128, BF16), ("sum", (MLA_Q_LORA, 1024)), ("sum", (MLA_KV_LORA, 1536))],
        name="mla_inputs_bwd", rows=s)
    grads["w_q_b"], grads["w_kv_b"] = g_w_q_b.astype(BF16), g_w_kv_b.astype(BF16)

    sent = emit({n: grads[n] for n in ("w_mem_kv", "w_q_b", "w_kv_b")})

    dproj = jnp.concatenate(
        [dc_q, dc_kv, dk_rope, d_gate["mla"], d_gate["sb"], d_gate["mem"], dq_sb.astype(BF16), dk_sb.astype(BF16),
         dv_sb.astype(BF16), dq_m.astype(BF16)], axis=1)
    grads["w_in"] = _mm(xb, dproj, "tn", name="g_w_in", out_dtype=BF16, behind=sent)
    sent = emit({"w_in": grads["w_in"]})
    grad_x = _mm(dproj, w["w_in"], "nt", name="grad_x", add=dx, behind=sent)
    return loss, grad_x, grads


def _shard_shape(shape, axis):
    return tuple(d // N_DEV if a == axis else d for a, d in enumerate(shape))


def _from_blocks(blocks, name):
    shape, axis = SHARDED[name]
    return blocks.reshape(shape) if axis == 0 else blocks.transpose(1, 0, 2).reshape(shape)


def _to_blocks(full, name):
    shape, axis = SHARDED[name]
    shp = _shard_shape(shape, axis)
    return full.reshape(N_DEV, *shp) if axis == 0 else full.reshape(shape[0], N_DEV, shp[1]).transpose(1, 0, 2)


def _pad_heads(a, used):
    rows = a.shape[0]
    a = a.reshape(rows, MLA_HEADS, used)
    return jnp.concatenate([a, jnp.zeros((rows, MLA_HEADS, LANES - used), a.dtype)], axis=2).reshape(rows, MLA_HEADS * LANES)


def _to_kernel_layout(name, full):
    if name == "w_in":
        return jnp.concatenate([jnp.zeros((D_MODEL, IN_PAD), full.dtype) if piece is None else full[:, piece[0]:piece[0] + piece[1]]
                                for piece in IN_PIECES], axis=1)
    if name == "w_q_b":
        return _pad_heads(full, MLA_NOPE + MLA_ROPE)
    if name == "w_kv_b":
        kv = full.reshape(MLA_KV_LORA, MLA_HEADS, MLA_NOPE + MLA_V)
        return jnp.concatenate([_pad_heads(kv[:, :, :MLA_NOPE].reshape(MLA_KV_LORA, -1), MLA_NOPE),
                                kv[:, :, MLA_NOPE:].reshape(MLA_KV_LORA, -1)], axis=1)
    return full


def _from_kernel_layout(name, g):
    if name == "w_in":
        placed, at = [], 0
        for piece in IN_PIECES:
            if piece is not None:
                placed.append((piece[0], g[:, at:at + piece[1]]))
            at += IN_PAD if piece is None else piece[1]
        return jnp.concatenate([cols for _, cols in sorted(placed, key=lambda item: item[0])], axis=1)
    if name == "w_q_b":
        return g.reshape(MLA_Q_LORA, MLA_HEADS, LANES)[:, :, :MLA_NOPE + MLA_ROPE].reshape(MLA_Q_LORA, -1)
    if name == "w_kv_b":
        return jnp.concatenate([g[:, :1024].reshape(MLA_KV_LORA, MLA_HEADS, LANES)[:, :, :MLA_NOPE],
                                g[:, 1024:].reshape(MLA_KV_LORA, MLA_HEADS, MLA_V)], axis=2).reshape(MLA_KV_LORA, -1)
    return g


def _pack_small(vectors, loss=None):
    flat = [v.reshape(-1) for v in vectors]
    flat.append(jnp.zeros((SMALL_ROWS * SMALL_LANES - LOSS_INDEX,), F32) if loss is None else
                jnp.concatenate([loss.reshape(-1)[:1], jnp.zeros((SMALL_ROWS * SMALL_LANES - LOSS_INDEX - 1,), F32)]))
    return jnp.concatenate(flat).reshape(SMALL_ROWS, SMALL_LANES)


def _unpack_small(packed):
    flat, res, off = packed.reshape(-1), [], 0
    for _, n in SMALL:
        res.append(flat[off:off + n].reshape(1, n))
        off += n
    return res


def _me_and_peers():
    x, y, c = lax.axis_index("x"), lax.axis_index("y"), lax.axis_index("c")
    peers = []
    for kk in range(1, N_DEV):
        px, py, pc = (x + (kk >> 2)) % 2, (y + ((kk >> 1) & 1)) % 2, (c + (kk & 1)) % 2
        peers.append(((px, py, pc), 4 * px + 2 * py + pc))
    return 4 * x + 2 * y + c, peers


def _share_small(small, *, name):
    def body(small_ref, all_ref, send_sems, recv_sems, local_sem):
        me, peers = _me_and_peers()
        copies = [pltpu.make_async_remote_copy(src_ref=small_ref, dst_ref=all_ref.at[me], send_sem=send_sems.at[kk], recv_sem=recv_sems.at[kk],
                                               device_id=pos, device_id_type=pl.DeviceIdType.MESH) for kk, (pos, _) in enumerate(peers)]
        copies.append(pltpu.make_async_copy(small_ref, all_ref.at[me], local_sem))
        for cp in copies:
            cp.start()
        for cp in copies:
            cp.wait()

    hbm = pl.BlockSpec(memory_space=pl.ANY)
    return pl.pallas_call(
        body, name=name, in_specs=[hbm], out_specs=hbm, out_shape=jax.ShapeDtypeStruct((N_DEV, *small.shape), small.dtype),
        scratch_shapes=[pltpu.SemaphoreType.DMA((N_DEV - 1,)), pltpu.SemaphoreType.DMA((N_DEV - 1,)), pltpu.SemaphoreType.DMA],
        compiler_params=pltpu.CompilerParams(has_side_effects=True),
    )(small)


_HBM = pl.BlockSpec(memory_space=pltpu.HBM)
_SEM = pl.BlockSpec(memory_space=pltpu.SEMAPHORE)


def _exchange_copies(srcs, zones, send_sems, recv_sems, gather):
    me, peers = _me_and_peers()
    return [pltpu.make_async_remote_copy(
        src_ref=srcs[t] if gather else srcs[t].at[peer], dst_ref=zones[t].at[me], send_sem=send_sems.at[7 * t + kk],
        recv_sem=recv_sems.at[7 * t + kk], device_id=pos, device_id_type=pl.DeviceIdType.MESH)
        for t in range(len(srcs)) for kk, (pos, peer) in enumerate(peers)]


def _exchange_start(tensors, *, gather, name):
    n = len(tensors)
    zones = [lax.empty((N_DEV, *(t.shape if gather else t.shape[1:])), t.dtype) for t in tensors]

    def body(*refs):
        for cp in _exchange_copies(refs[:n], refs[n:2 * n], refs[2 * n], refs[2 * n + 1], gather):
            cp.start()
        refs[-1][...] = jnp.zeros_like(refs[-1])

    buffers = [pltpu.HBM(a.shape, a.dtype) for a in tensors + zones]
    res = pl.pallas_call(
        body, name=name, in_specs=[_HBM] * (2 * n),
        out_shape=(pltpu.SemaphoreType.DMA((7 * n,)), pltpu.SemaphoreType.DMA((7 * n,)), *buffers, jax.ShapeDtypeStruct((8, LANES), F32)),
        out_specs=(_SEM, _SEM, *[_HBM] * (2 * n), pl.BlockSpec(memory_space=pltpu.VMEM)),
        input_output_aliases={i: 2 + i for i in range(2 * n)},
        compiler_params=pltpu.CompilerParams(has_side_effects=pltpu.SideEffectType.DATAFLOW_SIDE_EFFECTING),
    )(*[pltpu.with_memory_space_constraint(a, pltpu.HBM) for a in tensors + zones])
    return dict(sems=res[:2], buffers=res[2:2 + 2 * n], gather=gather, started=res[-1])


def _exchange_wait(started, after, *, name):
    n = len(started["buffers"]) // 2

    def body(*refs):
        for cp in _exchange_copies(refs[:n], refs[n:2 * n], refs[2 * n], refs[2 * n + 1], started["gather"]):
            cp.wait_send()
            cp.wait_recv()

    res = pl.pallas_call(
        body, name=name, in_specs=[_HBM] * (2 * n) + [_SEM, _SEM] + [_ANY] * len(after),
        out_shape=tuple(pltpu.HBM(a.shape, a.dtype) for a in started["buffers"]), out_specs=tuple([_HBM] * (2 * n)),
        input_output_aliases={i: i for i in range(2 * n)},
        compiler_params=pltpu.CompilerParams(has_side_effects=pltpu.SideEffectType.DATAFLOW_SIDE_EFFECTING),
    )(*started["buffers"], *started["sems"], *after)
    return res[:n], res[n:]


def _adamw(contrib, w, m, v, *, name):
    rows, cols = w.shape
    tile = min(rows, ADAM_ROWS)

    def body(c_ref, w_ref, m_ref, v_ref, g_ref, d_ref, nm_ref, nv_ref):
        g = c_ref[0].astype(F32)
        for s in range(1, N_DEV):
            g = g + c_ref[s].astype(F32)
        m_new = ADAM_B1 * m_ref[...] + (1.0 - ADAM_B1) * g
        v_new = ADAM_B2 * v_ref[...] + (1.0 - ADAM_B2) * (g * g)
        m_hat = m_new / (1.0 - ADAM_B1 ** ADAM_STEP)
        v_hat = v_new / (1.0 - ADAM_B2 ** ADAM_STEP)
        g_ref[...] = g
        d_ref[...] = -ADAM_LR * (m_hat / (jnp.sqrt(v_hat) + ADAM_EPS) + ADAM_WD * w_ref[...])
        nm_ref[...] = m_new
        nv_ref[...] = v_new

    spec = pl.BlockSpec((tile, cols), lambda i: (i, 0))
    return pl.pallas_call(
        body, name=name, grid=(rows // tile,),
        in_specs=[pl.BlockSpec((N_DEV, tile, cols), lambda i: (0, i, 0)), spec, spec, spec], out_specs=[spec] * 4,
        out_shape=[jax.ShapeDtypeStruct((rows, cols), F32)] * 4, compiler_params=_cparams("parallel"),
    )(contrib, w, m, v)


class _Weights:
    def __init__(self, gathers, vectors, me):
        self.gathers, self.ready, self.me, self.after = gathers, dict(vectors), me, ()

    def arrive_after(self, *values):
        self.after = values

    def __getitem__(self, name):
        if name not in self.ready:
            gi = next(i for i, group in enumerate(GATHER_GROUPS) if name in group)
            after = [*self.after, *[g["started"] for g in self.gathers]]
            shards, zones = _exchange_wait(self.gathers[gi], after, name=f"gather_wait_{gi}")
            for n, shard, zone in zip(GATHER_GROUPS[gi], shards, zones, strict=True):
                blocks = lax.dynamic_update_slice_in_dim(zone, shard[None], self.me, 0)
                self.ready[n] = _to_kernel_layout(n, _from_blocks(blocks, n))
        return self.ready[name]


def kernel(x, mem, w_in, w_mem_kv, q_a_gain, w_q_b, kv_a_gain, w_kv_b, w_branch_mla, w_branch_sb, w_branch_mem, w_merge_gate, b_merge_gate, w_out, ln_gain, ln_bias, loss_target, m_w_in, m_w_mem_kv, m_q_a_gain, m_w_q_b, m_kv_a_gain, m_w_kv_b, m_w_branch_mla, m_w_branch_sb, m_w_branch_mem, m_w_merge_gate, m_b_merge_gate, m_w_out, m_ln_gain, m_ln_bias, v_w_in, v_w_mem_kv, v_q_a_gain, v_w_q_b, v_kv_a_gain, v_w_kv_b, v_w_branch_mla, v_w_branch_sb, v_w_branch_mem, v_w_merge_gate, v_b_merge_gate, v_w_out, v_ln_gain, v_ln_bias):
    given = dict(locals())
    small_names = [n for n, _ in SMALL]
    smalls = lambda prefix: [given[prefix + n] for n in small_names]
    me = 4 * lax.axis_index("x") + 2 * lax.axis_index("y") + lax.axis_index("c")

    gathers = [_exchange_start([given[n][0].astype(BF16) for n in group], gather=True, name=f"gather_start_{gi}")
               for gi, group in enumerate(GATHER_GROUPS)]
    w = _Weights(gathers, {n: given[n] for n in small_names}, me)
    exchanges = []
    results = [{}, {}, {}, {}]

    def finish(gi, after):
        names, started = exchanges[gi]
        sent, zones = _exchange_wait(started, after, name=f"grads_wait_{gi}")
        done = []
        for n, blocks, zone in zip(names, sent, zones, strict=True):
            own = lax.dynamic_index_in_dim(blocks, me, 0, keepdims=True)
            contrib = lax.dynamic_update_slice_in_dim(zone, own, me, 0)
            outs = _adamw(contrib, given[n][0], given["m_" + n][0], given["v_" + n][0], name=f"adamw_{n}")
            for kind, res in zip(results, outs, strict=True):
                kind[n] = res[None]
            done.append(outs[1])
        return done

    def emit(grads):
        blocks = [_to_blocks(_from_kernel_layout(n, g), n).astype(BF16) for n, g in grads.items()]
        exchanges.append((tuple(grads), _exchange_start(blocks, gather=False, name=f"grads_start_{len(exchanges)}")))
        started = [exchanges[-1][1]["started"]]
        if len(exchanges) == len(GRAD_GROUPS):
            for gi in range(len(GRAD_GROUPS) - 1):
                started += finish(gi, started[:1])
        return started

    loss, grad_x, grads = _local_step(x[0], mem[0], loss_target[0], w, emit)

    contrib_small = _share_small(_pack_small([grads[n] for n in small_names], loss), name="share_small")
    sml = _adamw(contrib_small, _pack_small(smalls("")), _pack_small(smalls("m_")), _pack_small(smalls("v_")), name="adamw_small")
    for kind, packed in zip(results, sml, strict=True):
        kind.update(zip(small_names, _unpack_small(packed), strict=True))
    finish(len(GRAD_GROUPS) - 1, [grad_x])
    order = ["w_in", "w_mem_kv", "q_a_gain", "w_q_b", "kv_a_gain", "w_kv_b", "w_branch_mla", "w_branch_sb", "w_branch_mem",
             "w_merge_gate", "b_merge_gate", "w_out", "ln_gain", "ln_bias"]
    loss_out = sml[0].reshape(-1)[LOSS_INDEX]
    return (loss_out, grad_x[None], *[kind[n] for kind in results for n in order])
```

```python
import math

import jax
import jax.numpy as jnp
from jax import lax
from jax.experimental import pallas as pl
from jax.experimental.pallas import tpu as pltpu

F32, BF16 = jnp.float32, jnp.bfloat16

N_DEV = 8
D_MODEL = 1024
MLA_HEADS, MLA_NOPE, MLA_ROPE, MLA_V = 8, 64, 32, 64
MLA_Q_LORA, MLA_KV_LORA = 256, 128
SB_HEAD_DIM = 64
MEM_HEAD_DIM = 128
ROPE_BASE = 10000.0
RMS_EPS = 1e-6
LN_EPS = 1e-5
DEEPNORM_ALPHA = 2.0 ** 0.25
ADAM_LR, ADAM_B1, ADAM_B2, ADAM_EPS, ADAM_WD, ADAM_STEP = 0.001, 0.9, 0.999, 1e-08, 0.01, 10
LOG2E, LN2 = math.log2(math.e), math.log(2.0)

LANES = 128
GROUPS = 4
PROJ_WIDTH = 4096
COL_CQ, COL_CKV, COL_KROPE, COL_GATE_A, COL_GATE_B, COL_GATE_M = 0, 256, 384, 512, 1024, 1536
QKV_FIRST, QKV_WIDTH = 2048, 2048
COL_QB, COL_KB, COL_VB, COL_QM = 0, 512, 1024, 1536
IN_PIECES = ((0, 416), None, (416, 512), (2464, 512), (3488, 512), (928, 512), (1440, 512), (1952, 512), (2976, 512))
IN_PAD = 96

VMEM_LIMIT_BYTES = 56 * 1024 * 1024
NEG_BIG = -1e30
Q_BLOCK = 512
MEM_Q_BLOCK = 2048
SB_BWD_Q_BLOCK = 512
TRI_BLOCK = 256
TILE_ROWS = 64
KEY_CHUNK = 512

SHARDED = {
    "w_in": ((1024, 4000), 1), "w_mem_kv": ((1024, 1024), 0), "w_q_b": ((256, 768), 1), "w_kv_b": ((128, 1024), 1),
    "w_branch_mla": ((512, 1024), 1), "w_branch_sb": ((512, 1024), 1), "w_branch_mem": ((512, 1024), 1),
    "w_merge_gate": ((1024, 3072), 1), "w_out": ((1024, 1024), 0),
}
GATHER_GROUPS = (("w_in",), ("w_merge_gate",), ("w_q_b", "w_kv_b", "w_mem_kv", "w_branch_mla", "w_branch_sb", "w_branch_mem", "w_out"))
GRAD_GROUPS = (("w_out", "w_merge_gate", "w_branch_mla", "w_branch_sb", "w_branch_mem"), ("w_mem_kv", "w_q_b", "w_kv_b"), ("w_in",))
SMALL = (("q_a_gain", 256), ("kv_a_gain", 128), ("b_merge_gate", 3072), ("ln_gain", 1024), ("ln_bias", 1024))
SMALL_ROWS, SMALL_LANES = 48, 128
ADAM_ROWS = 256
LOSS_INDEX = 5504


def _cparams(*sem):
    return pltpu.CompilerParams(dimension_semantics=sem or None, vmem_limit_bytes=VMEM_LIMIT_BYTES)


_DIMS = {"nn": (((1,), (0,)), ((), ())), "nt": (((1,), (1,)), ((), ())), "tn": (((0,), (0,)), ((), ()))}


def _dot(a, b, dims):
    return lax.dot_general(a, b, _DIMS[dims], preferred_element_type=F32)


def _tile(dim, want):
    if dim <= want:
        return dim
    t = want - want % LANES
    while dim % t:
        t -= LANES
    assert t > 0, (dim, want)
    return t


_ANY = pl.BlockSpec(memory_space=pl.ANY)


def _mm(a, b, dims, *, name, out_dtype=F32, add=None, add_scale=1.0, col_scale=None, b_cols=None, behind=None,
        tm=1024, tn=1024, tk=1024):
    batch = a.shape[0] if a.ndim == 3 else None
    if dims == "nn":
        (m, k), (k2, n) = a.shape[-2:], b.shape[-2:]
    elif dims == "nt":
        (m, k), (n, k2) = a.shape[-2:], b.shape[-2:]
    else:
        (k, m), (k2, n) = a.shape[-2:], b.shape[-2:]
    assert k == k2 and a.ndim == b.ndim, (a.shape, b.shape, dims)
    assert batch is None or (b.shape[0] == batch and add is None and col_scale is None and b_cols is None)
    b_first = 0
    if b_cols is not None:
        assert dims == "nn"
        b_first, n = b_cols
    tm, tn, tk = _tile(m, tm), _tile(n, tn), _tile(k, tk)
    assert b_first % tn == 0
    jb = b_first // tn
    nk = k // tk

    def spec(block, index):
        if batch is None:
            return pl.BlockSpec(block, lambda bb, i, j, kk: index(i, j, kk))
        return pl.BlockSpec((None, *block), lambda bb, i, j, kk: (bb, *index(i, j, kk)))

    a_spec = spec((tk, tm), lambda i, j, kk: (kk, i)) if dims == "tn" else spec((tm, tk), lambda i, j, kk: (i, kk))
    b_spec = spec((tn, tk), lambda i, j, kk: (j, kk)) if dims == "nt" else spec((tk, tn), lambda i, j, kk: (kk, jb + j))
    o_spec = spec((tm, tn), lambda i, j, kk: (i, j))
    behind = [] if behind is None else behind if isinstance(behind, (list, tuple)) else [behind]
    optional = [(add, o_spec), (col_scale, pl.BlockSpec((1, tn), lambda bb, i, j, kk: (0, j))), *[(v, _ANY) for v in behind]]
    present = [(v, spec) for v, spec in optional if v is not None]

    def body(*refs):
        a_ref, b_ref = refs[:2]
        extra = iter(refs[2:2 + len(present)])
        add_ref = next(extra) if add is not None else None
        scale_ref = next(extra) if col_scale is not None else None
        o_ref = refs[2 + len(present)]
        part = _dot(a_ref[...].astype(BF16), b_ref[...].astype(BF16), dims)

        def finish(r):
            if add is not None:
                r = r + add_scale * add_ref[...]
            if col_scale is not None:
                r = r * scale_ref[...]
            o_ref[...] = r.astype(out_dtype)

        if nk == 1:
            finish(part)
            return
        acc = refs[-1]
        kk = pl.program_id(3)

        @pl.when(kk == 0)
        def _():
            acc[...] = part

        @pl.when(kk > 0)
        def _():
            acc[...] += part

        @pl.when(kk == nk - 1)
        def _():
            finish(acc[...])

    return pl.pallas_call(
        body, name=name, grid=(batch or 1, m // tm, n // tn, nk),
        in_specs=[a_spec, b_spec] + [spec for _, spec in present], out_specs=o_spec,
        out_shape=jax.ShapeDtypeStruct((m, n) if batch is None else (batch, m, n), out_dtype),
        scratch_shapes=[pltpu.VMEM((tm, tn), F32)] if nk > 1 else [],
        compiler_params=_cparams("parallel", "parallel", "parallel", "arbitrary"),
    )(a, b, *[v for v, _ in present])


def _rowwise(fn, ins, outs, *, name, rows, tr=512):
    n_in = len(ins)
    tr = min(tr, rows)
    in_specs, args = [], []
    for it in ins:
        if isinstance(it, tuple) and it[0] == "whole":
            in_specs.append(pl.BlockSpec(it[1].shape, lambda i, nd=it[1].ndim: (0,) * nd))
            args.append(it[1])
            continue
        arr, w, off = it if isinstance(it, tuple) else (it, it.shape[-1], 0)
        assert off % w == 0
        cb = off // w
        if arr.ndim == 3:
            in_specs.append(pl.BlockSpec((arr.shape[0], tr, w), lambda i, cb=cb: (0, i, cb)))
        elif arr.shape[0] == 1:
            in_specs.append(pl.BlockSpec((1, w), lambda i, cb=cb: (0, cb)))
        else:
            in_specs.append(pl.BlockSpec((tr, w), lambda i, cb=cb: (i, cb)))
        args.append(arr)
    out_shape, out_specs, is_sum = [], [], []
    for out in outs:
        is_sum.append(out[0] == "sum")
        if out[0] == "sum":
            shape = out[1] if isinstance(out[1], tuple) else (1, out[1])
            out_shape.append(jax.ShapeDtypeStruct(shape, F32))
            out_specs.append(pl.BlockSpec(shape, lambda i: (0, 0)))
        elif len(out) == 3:
            out_shape.append(jax.ShapeDtypeStruct((out[0], rows, out[1]), out[2]))
            out_specs.append(pl.BlockSpec((out[0], tr, out[1]), lambda i: (0, i, 0)))
        else:
            out_shape.append(jax.ShapeDtypeStruct((rows, out[0]), out[1]))
            out_specs.append(pl.BlockSpec((tr, out[0]), lambda i: (i, 0)))

    def body(*refs):
        res = fn(*[r[...] for r in refs[:n_in]])
        for r, val, s in zip(refs[n_in:], res, is_sum, strict=True):
            if s:
                @pl.when(pl.program_id(0) == 0)
                def _(r=r):
                    r[...] = jnp.zeros_like(r)

                r[...] += val
            elif isinstance(val, (list, tuple)):
                for n, part in enumerate(val):
                    r[n] = part.astype(r.dtype)
            else:
                r[...] = val.astype(r.dtype)

    return pl.pallas_call(
        body, name=name, grid=(rows // tr,), in_specs=in_specs, out_specs=out_specs, out_shape=out_shape,
        compiler_params=_cparams("arbitrary"),
    )(*args)


def _colsum(v):
    return jnp.sum(v, axis=0, keepdims=True)


def _sigmoid(v):
    return 1.0 / (1.0 + jnp.exp(-v))


def _lane_groups(v):
    return [v[:, g * LANES:(g + 1) * LANES] for g in range(v.shape[1] // LANES)]


def _swap_halves(v, first_lane):
    lane = lax.broadcasted_iota(jnp.int32, v.shape, 1)
    return jnp.where(lane < first_lane + 16, pltpu.roll(v, 112, axis=1), pltpu.roll(v, 16, axis=1))


def _lane_sum(acc, v):
    for part in _lane_groups(v):
        acc = acc + part
    return acc


def _low_half(shape):
    return lax.broadcasted_iota(jnp.int32, shape, 1) < LANES // 2


def _select_heads(per_head, pick):
    if len(per_head) == 1:
        return pick(per_head[0], 0)
    return jnp.where(_low_half(per_head[0].shape), pick(per_head[0], 0), pick(per_head[1], 1))


def _attn_specs(s, sk, hp, bq, q0, k0, v0):
    wq = hp * LANES
    assert q0 % wq == 0 and k0 % wq == 0 and v0 % LANES == 0
    qb0, kb0, vb0 = q0 // wq, k0 // wq, v0 // LANES
    q_spec = pl.BlockSpec((bq, wq), lambda g, i: (i, qb0 + g))
    k_spec = pl.BlockSpec((sk, wq), lambda g, i: (0, kb0 + g))
    v_spec = pl.BlockSpec((sk, LANES), lambda g, i: (0, vb0 + g))
    row_out = lambda w: pl.BlockSpec((bq, w), lambda g, i: (i, g))
    key_out = lambda w: pl.BlockSpec((sk, w), lambda g, i: (0, g))
    return q_spec, k_spec, v_spec, row_out, key_out


def _chunks(i, bq, ch, sk, causal):
    return ((i + 1) * bq - 1) // ch if causal else jnp.int32(sk // ch - 1)


def _positions(i, c, bq, ch):
    return (i * bq + lax.broadcasted_iota(jnp.int32, (bq, ch), 0), c * ch + lax.broadcasted_iota(jnp.int32, (bq, ch), 1))


def _softmax_fwd(q, k, v, *, hp, causal, name, q0=0, k0=0, v0=0, q_rows=Q_BLOCK):
    s, sk = q.shape[0], k.shape[0]
    bq, ch = min(q_rows, s), min(KEY_CHUNK, sk)
    assert not causal or bq <= ch
    q_spec, k_spec, v_spec, row_out, _ = _attn_specs(s, sk, hp, bq, q0, k0, v0)

    def body(q_ref, k_ref, v_ref, o_ref, lse_ref, s_scr):
        i = pl.program_id(1)
        qs = _lane_groups(q_ref[...])
        last = _chunks(i, bq, ch, sk, causal)

        def scores(c, ms, masked):
            off = pl.multiple_of(c * ch, ch)
            out = []
            for j in range(hp):
                sc = _dot(qs[j], k_ref[pl.ds(off, ch), j * LANES:(j + 1) * LANES], "nt")
                if masked:
                    qpos, kpos = _positions(i, c, bq, ch)
                    sc = jnp.where(kpos <= qpos, sc, NEG_BIG)
                s_scr[j, c] = sc
                m = ms[j]
                for part in _lane_groups(sc):
                    m = jnp.maximum(m, part)
                out.append(m)
            return tuple(out)

        ms = lax.fori_loop(0, last, lambda c, m: scores(c, m, False), tuple(jnp.full((bq, LANES), NEG_BIG, F32) for _ in range(hp)))
        ms = scores(last, ms, causal)
        row_max = [jnp.max(m, axis=1, keepdims=True) for m in ms]

        def weigh(c, carry):
            off = pl.multiple_of(c * ch, ch)
            vt = v_ref[pl.ds(off, ch), :]
            out = []
            for j in range(hp):
                l, acc = carry[j]
                p = jnp.exp2(s_scr[j, c] - row_max[j])
                out.append((_lane_sum(l, p), acc + _dot(p.astype(BF16), vt, "nn")))
            return tuple(out)

        zero = jnp.zeros((bq, LANES), F32)
        res = lax.fori_loop(0, last + 1, weigh, tuple((zero, zero) for _ in range(hp)))
        row_sum = [jnp.sum(l, axis=1, keepdims=True) for l, _ in res]
        o_ref[...] = _select_heads([acc for _, acc in res], lambda acc, j: acc / row_sum[j])
        lse_ref[...] = _select_heads([jnp.broadcast_to(row_max[j] + jnp.log2(row_sum[j]), (bq, LANES)) for j in range(hp)], lambda a, j: a)

    return pl.pallas_call(
        body, name=name, grid=(GROUPS, s // bq), in_specs=[q_spec, k_spec, v_spec], out_specs=[row_out(LANES), row_out(LANES)],
        out_shape=[jax.ShapeDtypeStruct((s, GROUPS * LANES), F32)] * 2,
        scratch_shapes=[pltpu.VMEM((hp, sk // ch, bq, ch), F32)], compiler_params=_cparams("parallel", "arbitrary"),
    )(q, k, v)


def _head_cotangent(do, j, hp):
    if hp == 1:
        return do
    return jnp.where(_low_half(do.shape) == (j == 0), do, 0.0)


def _softmax_bwd(q, k, v, o, do, lse, behind, *, hp, causal, dq_scale, name, q0=0, k0=0, v0=0, q_rows=Q_BLOCK):
    s, sk = q.shape[0], k.shape[0]
    bq, ch = min(q_rows, s), min(KEY_CHUNK, sk)
    assert not causal or bq <= ch
    wq = hp * LANES
    q_spec, k_spec, v_spec, row_out, key_out = _attn_specs(s, sk, hp, bq, q0, k0, v0)

    def body(q_ref, k_ref, v_ref, o_ref, do_ref, lse_ref, _, dq_ref, dk_ref, dv_ref, dk_t, dv_t):
        i = pl.program_id(1)

        @pl.when(i == 0)
        def _():
            dk_t[...] = jnp.zeros_like(dk_t)
            dv_t[...] = jnp.zeros_like(dv_t)

        qs = _lane_groups(q_ref[...])
        do_all, o_all, lse_all = do_ref[...], o_ref[...], lse_ref[...]
        dos, deltas, lses = [], [], []
        for j in range(hp):
            d = _head_cotangent(do_all, j, hp)
            deltas.append(jnp.sum(d * o_all, axis=1, keepdims=True))
            dos.append(d.astype(BF16))
            lses.append(lse_all[:, j * (LANES // hp):j * (LANES // hp) + 1])
        last = _chunks(i, bq, ch, sk, causal)

        def chunk(c, dqs, masked):
            off = pl.multiple_of(c * ch, ch)
            vt = v_ref[pl.ds(off, ch), :]
            out, dks, dv = [], [], None
            for j in range(hp):
                kt = k_ref[pl.ds(off, ch), j * LANES:(j + 1) * LANES]
                p = jnp.exp2(_dot(qs[j], kt, "nt") - lses[j])
                if masked:
                    qpos, kpos = _positions(i, c, bq, ch)
                    p = jnp.where(kpos <= qpos, p, 0.0)
                ds = (p * (_dot(dos[j], vt, "nt") - deltas[j]) * LN2).astype(BF16)
                out.append(dqs[j] + _dot(ds, kt, "nn"))
                dks.append(_dot(qs[j], ds, "tn"))
                dvj = _dot(dos[j], p.astype(BF16), "tn")
                dv = dvj if dv is None else dv + dvj
            dk_t[c] += dks[0] if hp == 1 else jnp.concatenate(dks, axis=0)
            dv_t[c] += dv
            return tuple(out)

        dqs = lax.fori_loop(0, last, lambda c, d: chunk(c, d, False), tuple(jnp.zeros((bq, LANES), F32) for _ in range(hp)))
        dqs = chunk(last, dqs, causal)
        dq_ref[...] = (dqs[0] if hp == 1 else jnp.concatenate(dqs, axis=1)) * dq_scale

        @pl.when(i == s // bq - 1)
        def _():
            for c in range(sk // ch):
                dk_ref[c * ch:(c + 1) * ch, :] = dk_t[c].T
                dv_ref[c * ch:(c + 1) * ch, :] = dv_t[c].T

    return pl.pallas_call(
        body, name=name, grid=(GROUPS, s // bq),
        in_specs=[q_spec, k_spec, v_spec, row_out(LANES), row_out(LANES), row_out(LANES), _ANY],
        out_specs=[row_out(wq), key_out(wq), key_out(LANES)],
        out_shape=[jax.ShapeDtypeStruct((s, GROUPS * wq), F32), jax.ShapeDtypeStruct((sk, GROUPS * wq), F32),
                   jax.ShapeDtypeStruct((sk, GROUPS * LANES), F32)],
        scratch_shapes=[pltpu.VMEM((sk // ch, wq, ch), F32), pltpu.VMEM((sk // ch, LANES, ch), F32)],
        compiler_params=_cparams("arbitrary", "arbitrary"),
    )(q, k, v, o, do, lse, behind)


def _log2_sigmoid_pair(z2):
    minus_abs = lax.bitcast_convert_type(lax.bitcast_convert_type(z2, jnp.uint32) | jnp.uint32(0x80000000), F32)
    log_beta = jnp.minimum(z2, 0.0) - jnp.log2(1.0 + jnp.exp2(minus_abs))
    return log_beta, log_beta - z2


def _tilewise(fn, *arrays):
    rows, cols = arrays[0].shape
    step = min(TILE_ROWS, rows)
    grid = [[fn(*[None if a is None else a[r:r + step, c:c + LANES] for a in arrays]) for c in range(0, cols, LANES)]
            for r in range(0, rows, step)]
    return [jnp.concatenate([jnp.concatenate([cell[k] for cell in row], axis=1) for row in grid], axis=0)
            for k in range(len(grid[0][0]))]


def _split(v):
    hi = v.astype(BF16)
    return hi, (v - hi.astype(F32)).astype(BF16)


def _tri(n, after):
    rows, cols = lax.broadcasted_iota(jnp.int32, (n, n), 0), lax.broadcasted_iota(jnp.int32, (n, n), 1)
    return (rows > cols if after else rows < cols).astype(BF16)


def _running_sums(v, terms, start, tri, backwards):
    n = tri.shape[0]
    n_blocks = v.shape[1] // n
    order = range(n_blocks - 1, -1, -1) if backwards else range(n_blocks)
    stacked = tri if len(terms) == 1 else jnp.concatenate([tri] * len(terms), axis=0)
    parts, run = [None] * n_blocks, start
    for t in order:
        cols = slice(t * n, (t + 1) * n)
        lhs = terms[0][:, cols] if len(terms) == 1 else jnp.concatenate([term[:, cols] for term in terms], axis=1)
        parts[t] = _dot(lhs, stacked, "nn") + run
        run = run + jnp.sum(v[:, cols], axis=1, keepdims=True)
    return (parts[0] if n_blocks == 1 else jnp.concatenate(parts, axis=1)), run


def _sb_weights(qm, kt, run, tri, strict):
    def logs(z2, keep):
        log_beta, log_keep = _log2_sigmoid_pair(z2)
        if keep is not None:
            log_keep = jnp.where(keep, log_keep, 0.0)
        return log_beta, log_keep, *_split(log_keep)

    log_beta, log_keep, hi, lo = _tilewise(logs, _dot(qm, kt, "nt"), strict)
    behind, run = _running_sums(log_keep, (hi, lo), run, tri, True)

    def weigh(log_beta, behind, keep):
        a = jnp.exp2(log_beta + behind)
        return (a if keep is None else jnp.where(keep, a, 0.0),)

    (a,) = _tilewise(weigh, log_beta, behind, strict)
    return a, log_beta, run


def _sb_queries(q_all):
    low = _low_half(q_all.shape)
    zero = jnp.zeros_like(q_all)
    return [jnp.where(low, q_all, zero), jnp.where(low, zero, q_all)]


def _sb_fwd(qkv, *, q0, k0, v0, name):
    s = qkv.shape[0]
    bq, ch = min(Q_BLOCK, s), min(KEY_CHUNK, s)
    q_spec, k_spec, v_spec, row_out, _ = _attn_specs(s, s, 1, bq, q0, k0, v0)

    def body(q_ref, k_ref, v_ref, o_ref):
        i = pl.program_id(1)
        qms = _sb_queries(q_ref[...])
        tri = _tri(min(TRI_BLOCK, ch), True)
        last = _chunks(i, bq, ch, s, True)

        def chunk(c, carry, masked):
            off = pl.multiple_of(c * ch, ch)
            kt, vt = k_ref[pl.ds(off, ch), :], v_ref[pl.ds(off, ch), :]
            strict = None
            if masked:
                qpos, kpos = _positions(i, c, bq, ch)
                strict = kpos < qpos
            out = []
            for j in range(2):
                run, acc = carry[j]
                a, _, run = _sb_weights(qms[j], kt, run, tri, strict)
                out.append((run, acc + _dot(a.astype(BF16), vt, "nn")))
            return tuple(out)

        carry = chunk(last, tuple((jnp.zeros((bq, 1), F32), jnp.zeros((bq, LANES), F32)) for _ in range(2)), True)
        res = lax.fori_loop(0, last, lambda n, c: chunk(last - 1 - n, c, False), carry)
        o_ref[...] = _select_heads([acc for _, acc in res], lambda acc, j: acc)

    return pl.pallas_call(
        body, name=name, grid=(GROUPS, s // bq), in_specs=[q_spec, k_spec, v_spec], out_specs=row_out(LANES),
        out_shape=jax.ShapeDtypeStruct((s, GROUPS * LANES), F32), compiler_params=_cparams("parallel", "arbitrary"),
    )(qkv, qkv, qkv)


def _sb_bwd(qkv, do, behind, *, q0, k0, v0, dq_scale, name):
    s = qkv.shape[0]
    bq, ch = min(SB_BWD_Q_BLOCK, s), min(KEY_CHUNK, s)
    q_spec, k_spec, v_spec, row_out, key_out = _attn_specs(s, s, 1, bq, q0, k0, v0)

    def body(q_ref, k_ref, v_ref, do_ref, _, dq_ref, dk_ref, dv_ref, g_s, beta_s):
        i = pl.program_id(1)

        @pl.when(i == 0)
        def _():
            dk_ref[...] = jnp.zeros_like(dk_ref)
            dv_ref[...] = jnp.zeros_like(dv_ref)

        qms = _sb_queries(q_ref[...])
        do_all = do_ref[...]
        dos = [_head_cotangent(do_all, j, 2).astype(BF16) for j in range(2)]
        dos_ln2 = [(_head_cotangent(do_all, j, 2) * LN2).astype(BF16) for j in range(2)]
        tri_after, tri_before = _tri(min(TRI_BLOCK, ch), True), _tri(min(TRI_BLOCK, ch), False)
        last = _chunks(i, bq, ch, s, True)

        def strict_mask(c):
            qpos, kpos = _positions(i, c, bq, ch)
            return kpos < qpos

        def sweep1(c, runs, masked):
            off = pl.multiple_of(c * ch, ch)
            kt, vt = k_ref[pl.ds(off, ch), :], v_ref[pl.ds(off, ch), :]
            strict = strict_mask(c) if masked else None
            out, dv = [], None
            for j in range(2):
                a, log_beta, run = _sb_weights(qms[j], kt, runs[j], tri_after, strict)
                g_s[j, c] = (a * _dot(dos_ln2[j], vt, "nt")).astype(BF16)
                beta_s[j, c] = jnp.exp2(log_beta).astype(BF16)
                dvj = _dot(a.astype(BF16), dos[j], "tn")
                dv = dvj if dv is None else dv + dvj
                out.append(run)
            dv_ref[pl.ds(off, ch), :] += dv
            return tuple(out)

        runs = sweep1(last, tuple(jnp.zeros((bq, 1), F32) for _ in range(2)), True)
        lax.fori_loop(0, last, lambda n, r: sweep1(last - 1 - n, r, False), runs)

        def sweep2(c, carry, masked):
            off = pl.multiple_of(c * ch, ch)
            kt = k_ref[pl.ds(off, ch), :]
            out, dk = [], None
            for j in range(2):
                before, dq = carry[j]
                g16, beta = g_s[j, c], beta_s[j, c].astype(F32)
                g = g16.astype(F32)
                in_front, before = _running_sums(g, (g16,), before, tri_before, False)
                dz = g * (1.0 - beta) - beta * in_front
                if masked:
                    dz = jnp.where(strict_mask(c), dz, 0.0)
                dz = dz.astype(BF16)
                dkj = _dot(dz, qms[j], "tn")
                dk = dkj if dk is None else dk + dkj
                out.append((before, dq + _dot(dz, kt, "nn")))
            dk_ref[pl.ds(off, ch), :] += dk
            return tuple(out)

        carry = lax.fori_loop(0, last, lambda c, cr: sweep2(c, cr, False),
                              tuple((jnp.zeros((bq, 1), F32), jnp.zeros((bq, LANES), F32)) for _ in range(2)))
        res = sweep2(last, carry, True)
        dq_ref[...] = _select_heads([dq for _, dq in res], lambda dq, j: dq) * dq_scale

    n_ch = s // ch
    return pl.pallas_call(
        body, name=name, grid=(GROUPS, s // bq), in_specs=[q_spec, k_spec, v_spec, row_out(LANES), _ANY],
        out_specs=[row_out(LANES), key_out(LANES), key_out(LANES)],
        out_shape=[jax.ShapeDtypeStruct((s, GROUPS * LANES), F32)] * 3,
        scratch_shapes=[pltpu.VMEM((2, n_ch, bq, ch), BF16)] * 2,
        compiler_params=_cparams("arbitrary", "arbitrary"),
    )(qkv, qkv, qkv, do, behind)


def _rope_tables(s):
    half = MLA_ROPE // 2
    freqs = ROPE_BASE ** (-jnp.arange(half, dtype=F32) / half)
    ang = jnp.arange(s, dtype=F32)[:, None] * freqs[None, :]
    cos, sin = jnp.cos(ang), jnp.sin(ang)
    tail = jnp.zeros((s, LANES - MLA_NOPE - MLA_ROPE), F32)
    lead = lambda fill: jnp.full((s, MLA_NOPE), fill, F32)
    return dict(
        cos_k0=jnp.concatenate([cos, cos, lead(0.0), tail], axis=1), sin_k0=jnp.concatenate([-sin, sin, lead(0.0), tail], axis=1),
        cos_k64=jnp.concatenate([lead(0.0), cos, cos, tail], axis=1), sin_k64=jnp.concatenate([lead(0.0), -sin, sin, tail], axis=1),
        cos_q=jnp.concatenate([lead(1.0), cos, cos, tail], axis=1),
        sin_k64_t=jnp.concatenate([lead(0.0), sin, -sin, tail], axis=1),
    )


def _local_step(x, mem, target, w, emit=lambda grads: [jnp.zeros((8, LANES), F32)]):
    s = x.shape[0]
    rope = _rope_tables(s)
    xb = x.astype(BF16)
    inv_d = 1.0 / D_MODEL
    scale_a = LOG2E / math.sqrt(MLA_NOPE + MLA_ROPE)
    scale_b = LOG2E / math.sqrt(SB_HEAD_DIM)
    scale_m = LOG2E / math.sqrt(MEM_HEAD_DIM)
    arrive_after = getattr(w, "arrive_after", lambda *values: None)
    memb = mem.astype(BF16)

    arrive_after(xb, memb, *rope.values())
    proj = _mm(xb, w["w_in"], "nn", name="proj", b_cols=(0, QKV_FIRST))
    one = jnp.ones((1, 512), F32)
    qkv = _mm(xb, w["w_in"], "nn", name="proj_qkv", b_cols=(QKV_FIRST, QKV_WIDTH), out_dtype=BF16,
              col_scale=jnp.concatenate([one * scale_b, one, one, one * scale_m], axis=1))
    arrive_after(qkv)
    pre = _mm(xb, w["w_merge_gate"], "nn", name="merge_pre", out_dtype=BF16)
    arrive_after(pre)

    def mla_inputs(c_q, c_kv, k_rope, g_q, g_kv, w_q, w_kv, cos_q, sin_q, cos_k, sin_k):
        n_q = (c_q * lax.rsqrt(jnp.mean(c_q * c_q, axis=1, keepdims=True) + RMS_EPS) * g_q).astype(BF16)
        n_kv = (c_kv * lax.rsqrt(jnp.mean(c_kv * c_kv, axis=1, keepdims=True) + RMS_EPS) * g_kv).astype(BF16)
        q_a = _dot(n_q, w_q, "nn")
        kv_a = _dot(n_kv, w_kv, "nn").astype(BF16)
        q = jnp.concatenate([(g * cos_q + _swap_halves(g, MLA_NOPE) * sin_q) * scale_a for g in _lane_groups(q_a)], axis=1)
        k_pe = pltpu.roll(k_rope * cos_k + _swap_halves(k_rope, 0) * sin_k, MLA_NOPE, axis=1).astype(BF16)
        k = jnp.concatenate([g + k_pe for g in _lane_groups(kv_a[:, :1024])], axis=1)
        return n_q, n_kv, q, k, kv_a[:, 1024:]

    n_q, n_kv, q_mla, k_mla, v_a = _rowwise(
        mla_inputs, [(proj, 256, COL_CQ), (proj, 128, COL_CKV), (proj, 128, COL_KROPE), w["q_a_gain"], w["kv_a_gain"],
                     ("whole", w["w_q_b"]), ("whole", w["w_kv_b"]), rope["cos_q"], rope["sin_k64"], rope["cos_k0"], rope["sin_k0"]],
        [(256, BF16), (128, BF16), (1024, BF16), (1024, BF16), (512, BF16)], name="mla_inputs", rows=s)
    o_a, lse_a = _softmax_fwd(q_mla, k_mla, v_a, hp=2, causal=True, name="mla_fwd")

    o_b = _sb_fwd(qkv, q0=COL_QB, k0=COL_KB, v0=COL_VB, name="sb_fwd")

    mem_kv =_mm(memb, w["w_mem_kv"], "nn", name="mem_kv", out_dtype=BF16)
    o_m, lse_m = _softmax_fwd(qkv, mem_kv, mem_kv, hp=1, causal=False, name="mem_fwd", q0=1536, v0=512, q_rows=MEM_Q_BLOCK)

    branches = ("mla", "sb", "mem")
    w_branch = jnp.stack([w[f"w_branch_{br}"] for br in branches])
    bias = w["b_merge_gate"]

    def head(oa, ob, om, ga, gb, gm, pa, pb, pm, ba, bb, bm, xv, tv, gain, bias_ln, w_b, w_o, w_g):
        us, ys, gs = [], [], []
        for n, (o, gate, p, b) in enumerate(((oa, ga, pa, ba), (ob, gb, pb, bb), (om, gm, pm, bm))):
            us.append((o * gate * _sigmoid(gate)).astype(BF16))
            ys.append(_dot(us[n], w_b[n], "nn"))
            gs.append(_sigmoid(p.astype(F32) + b))
        merged = (gs[0] * ys[0] + gs[1] * ys[1] + gs[2] * ys[2]).astype(BF16)
        z = DEEPNORM_ALPHA * xv + _dot(merged, w_o, "nn")
        zc = z - jnp.mean(z, axis=1, keepdims=True)
        rstd = lax.rsqrt(jnp.mean(zc * zc, axis=1, keepdims=True) + LN_EPS)
        xhat = zc * rstd
        err = xhat * gain + bias_ln - tv
        loss = 0.5 * jnp.sum(jnp.mean(err * err, axis=1, keepdims=True), axis=0, keepdims=True)
        dy = err * inv_d
        dxhat = dy * gain
        dz = rstd * (dxhat - jnp.mean(dxhat, axis=1, keepdims=True) - xhat * jnp.mean(dxhat * xhat, axis=1, keepdims=True))
        dz16 = dz.astype(BF16)
        dm = _dot(dz16, w_o, "nt")
        dpre = jnp.concatenate([dm * ys[n] * gs[n] * (1.0 - gs[n]) for n in range(3)], axis=1)
        dx = DEEPNORM_ALPHA * dz + _dot(dpre.astype(BF16), w_g, "nt")
        dys = [(dm * gs[n]).astype(BF16) for n in range(3)]
        d_os, d_gates = [], []
        for n, (o, gate) in enumerate(((oa, ga), (ob, gb), (om, gm))):
            du, sg = _dot(dys[n], w_b[n], "nt"), _sigmoid(gate)
            d_os.append(du * gate * sg)
            d_gates.append(du * o * sg * (1.0 + gate * (1.0 - sg)))
        return (us, merged, dx, dz16, _colsum(dy * xhat), _colsum(dy), jnp.broadcast_to(loss, (1, LANES)), dpre, _colsum(dpre),
                dys, *d_os, *d_gates)

    grads = {}
    (u, merged, dx, dzb, grads["ln_gain"], grads["ln_bias"], loss, dpre, grads["b_merge_gate"], dy, *rest) = _rowwise(
        head, [o_a, o_b, o_m, (proj, 512, COL_GATE_A), (proj, 512, COL_GATE_B), (proj, 512, COL_GATE_M),
               (pre, 1024, 0), (pre, 1024, 1024), (pre, 1024, 2048), (bias, 1024, 0), (bias, 1024, 1024), (bias, 1024, 2048),
               x, target, w["ln_gain"], w["ln_bias"], ("whole", w_branch), ("whole", w["w_out"]), ("whole", w["w_merge_gate"])],
        [(3, 512, BF16), (1024, BF16), (1024, F32), (1024, BF16), ("sum", 1024), ("sum", 1024), ("sum", LANES),
         (3072, BF16), ("sum", 3072), (3, 1024, BF16)] + [(512, F32)] * 3 + [(512, BF16)] * 3, name="head", rows=s, tr=256)
    d_o, d_gate = dict(zip(branches, rest[:3], strict=True)), dict(zip(branches, rest[3:], strict=True))

    grads["w_out"] = _mm(merged, dzb, "tn", name="g_w_out", out_dtype=BF16)
    grads["w_merge_gate"] = _mm(xb, dpre, "tn", name="g_w_merge", out_dtype=BF16)
    g_w_branch = _mm(u, dy, "tn", name="g_w_branch", out_dtype=BF16)
    for n, br in enumerate(branches):
        grads[f"w_branch_{br}"] = g_w_branch[n]
    (sent,) = emit({n: grads[n] for n in ("w_out", "w_merge_gate", "w_branch_mla", "w_branch_sb", "w_branch_mem")})

    dq_m, dk_m, dv_m = _softmax_bwd(qkv, mem_kv, mem_kv, o_m, d_o["mem"], lse_m, sent, hp=1, causal=False, dq_scale=scale_m,
                                    name="mem_bwd", q0=1536, v0=512, q_rows=MEM_Q_BLOCK)
    grads["w_mem_kv"] = _mm(memb, jnp.concatenate([dk_m, dv_m], axis=1), "tn", name="g_w_mem_kv", out_dtype=BF16)

    dq_sb, dk_sb, dv_sb = _sb_bwd(qkv, d_o["sb"], sent, q0=0, k0=512, v0=1024, dq_scale=scale_b, name="sb_bwd")

    dq_mla, dk_mla, dv_a = _softmax_bwd(q_mla, k_mla, v_a, o_a, d_o["mla"], lse_a, sent, hp=2, causal=True, dq_scale=scale_a,
                                        name="mla_bwd")

    def mla_inputs_bwd(dq, dk, dv, n_q, n_kv, c_q, c_kv, g_q, g_kv, w_q, w_kv, cos_q, sin_q, cos_k, sin_k):
        dq_a = jnp.concatenate([g * cos_q + _swap_halves(g, MLA_NOPE) * sin_q for g in _lane_groups(dq)], axis=1).astype(BF16)
        groups = _lane_groups(dk)
        g_rope = groups[0]
        for other in groups[1:]:
            g_rope = g_rope + other
        dk_rope = pltpu.roll(g_rope * cos_k + _swap_halves(g_rope, MLA_NOPE) * sin_k, MLA_NOPE, axis=1)
        nope = _low_half(g_rope.shape)
        dkv_a = jnp.concatenate([jnp.where(nope, grp, 0.0) for grp in groups] + [dv], axis=1).astype(BF16)
        res = []
        for c, dn, g in ((c_q, _dot(dq_a, w_q, "nt"), g_q), (c_kv, _dot(dkv_a, w_kv, "nt"), g_kv)):
            r = lax.rsqrt(jnp.mean(c * c, axis=1, keepdims=True) + RMS_EPS)
            t = dn * g
            res += [r * t - c * (r * r * r) * jnp.mean(c * t, axis=1, keepdims=True), _colsum(dn * c * r)]
        return *res, dk_rope, _dot(n_q, dq_a, "tn"), _dot(n_kv, dkv_a, "tn")

    dc_q, grads["q_a_gain"], dc_kv, grads["kv_a_gain"], dk_rope, g_w_q_b, g_w_kv_b = _rowwise(
        mla_inputs_bwd, [dq_mla, dk_mla, dv_a, n_q, n_kv, (proj, 256, COL_CQ), (proj, 128, COL_CKV), w["q_a_gain"], w["kv_a_gain"],
                         ("whole", w["w_q_b"]), ("whole", w["w_kv_b"]), rope["cos_q"], rope["sin_k64_t"], rope["cos_k64"], rope["sin_k64_t"]],
        [(256, BF16), ("sum", 256), (128, BF16), ("sum", 128), (128, BF16), ("sum", (MLA_Q_LORA, 1024)), ("sum", (MLA_KV_LORA, 1536))],
        name="mla_inputs_bwd", rows=s)
    grads["w_q_b"], grads["w_kv_b"] = g_w_q_b.astype(BF16), g_w_kv_b.astype(BF16)

    sent = emit({n: grads[n] for n in ("w_mem_kv", "w_q_b", "w_kv_b")})

    dproj = jnp.concatenate(
        [dc_q, dc_kv, dk_rope, d_gate["mla"], d_gate["sb"], d_gate["mem"], dq_sb.astype(BF16), dk_sb.astype(BF16),
         dv_sb.astype(BF16), dq_m.astype(BF16)], axis=1)
    grads["w_in"] = _mm(xb, dproj, "tn", name="g_w_in", out_dtype=BF16, behind=sent)
    sent = emit({"w_in": grads["w_in"]})
    grad_x = _mm(dproj, w["w_in"], "nt", name="grad_x", add=dx, behind=sent)
    return loss, grad_x, grads


def _shard_shape(shape, axis):
    return tuple(d // N_DEV if a == axis else d for a, d in enumerate(shape))


def _from_blocks(blocks, name):
    shape, axis = SHARDED[name]
    return blocks.reshape(shape) if axis == 0 else blocks.transpose(1, 0, 2).reshape(shape)


def _to_blocks(full, name):
    shape, axis = SHARDED[name]
    shp = _shard_shape(shape, axis)
    return full.reshape(N_DEV, *shp) if axis == 0 else full.reshape(shape[0], N_DEV, shp[1]).transpose(1, 0, 2)


def _pad_heads(a, used):
    rows = a.shape[0]
    a = a.reshape(rows, MLA_HEADS, used)
    return jnp.concatenate([a, jnp.zeros((rows, MLA_HEADS, LANES - used), a.dtype)], axis=2).reshape(rows, MLA_HEADS * LANES)


def _to_kernel_layout(name, full):
    if name == "w_in":
        return jnp.concatenate([jnp.zeros((D_MODEL, IN_PAD), full.dtype) if piece is None else full[:, piece[0]:piece[0] + piece[1]]
                                for piece in IN_PIECES], axis=1)
    if name == "w_q_b":
        return _pad_heads(full, MLA_NOPE + MLA_ROPE)
    if name == "w_kv_b":
        kv = full.reshape(MLA_KV_LORA, MLA_HEADS, MLA_NOPE + MLA_V)
        return jnp.concatenate([_pad_heads(kv[:, :, :MLA_NOPE].reshape(MLA_KV_LORA, -1), MLA_NOPE),
                                kv[:, :, MLA_NOPE:].reshape(MLA_KV_LORA, -1)], axis=1)
    return full


def _from_kernel_layout(name, g):
    if name == "w_in":
        placed, at = [], 0
        for piece in IN_PIECES:
            if piece is not None:
                placed.append((piece[0], g[:, at:at + piece[1]]))
            at += IN_PAD if piece is None else piece[1]
        return jnp.concatenate([cols for _, cols in sorted(placed, key=lambda item: item[0])], axis=1)
    if name == "w_q_b":
        return g.reshape(MLA_Q_LORA, MLA_HEADS, LANES)[:, :, :MLA_NOPE + MLA_ROPE].reshape(MLA_Q_LORA, -1)
    if name == "w_kv_b":
        return jnp.concatenate([g[:, :1024].reshape(MLA_KV_LORA, MLA_HEADS, LANES)[:, :, :MLA_NOPE],
                                g[:, 1024:].reshape(MLA_KV_LORA, MLA_HEADS, MLA_V)], axis=2).reshape(MLA_KV_LORA, -1)
    return g


def _pack_small(vectors, loss=None):
    flat = [v.reshape(-1) for v in vectors]
    flat.append(jnp.zeros((SMALL_ROWS * SMALL_LANES - LOSS_INDEX,), F32) if loss is None else
                jnp.concatenate([loss.reshape(-1)[:1], jnp.zeros((SMALL_ROWS * SMALL_LANES - LOSS_INDEX - 1,), F32)]))
    return jnp.concatenate(flat).reshape(SMALL_ROWS, SMALL_LANES)


def _unpack_small(packed):
    flat, res, off = packed.reshape(-1), [], 0
    for _, n in SMALL:
        res.append(flat[off:off + n].reshape(1, n))
        off += n
    return res


def _me_and_peers():
    x, y, c = lax.axis_index("x"), lax.axis_index("y"), lax.axis_index("c")
    peers = []
    for kk in range(1, N_DEV):
        px, py, pc = (x + (kk >> 2)) % 2, (y + ((kk >> 1) & 1)) % 2, (c + (kk & 1)) % 2
        peers.append(((px, py, pc), 4 * px + 2 * py + pc))
    return 4 * x + 2 * y + c, peers


def _share_small(small, *, name):
    def body(small_ref, all_ref, send_sems, recv_sems, local_sem):
        me, peers = _me_and_peers()
        copies = [pltpu.make_async_remote_copy(src_ref=small_ref, dst_ref=all_ref.at[me], send_sem=send_sems.at[kk], recv_sem=recv_sems.at[kk],
                                               device_id=pos, device_id_type=pl.DeviceIdType.MESH) for kk, (pos, _) in enumerate(peers)]
        copies.append(pltpu.make_async_copy(small_ref, all_ref.at[me], local_sem))
        for cp in copies:
            cp.start()
        for cp in copies:
            cp.wait()

    hbm = pl.BlockSpec(memory_space=pl.ANY)
    return pl.pallas_call(
        body, name=name, in_specs=[hbm], out_specs=hbm, out_shape=jax.ShapeDtypeStruct((N_DEV, *small.shape), small.dtype),
        scratch_shapes=[pltpu.SemaphoreType.DMA((N_DEV - 1,)), pltpu.SemaphoreType.DMA((N_DEV - 1,)), pltpu.SemaphoreType.DMA],
        compiler_params=pltpu.CompilerParams(has_side_effects=True),
    )(small)


_HBM = pl.BlockSpec(memory_space=pltpu.HBM)
_SEM = pl.BlockSpec(memory_space=pltpu.SEMAPHORE)


def _exchange_copies(srcs, zones, send_sems, recv_sems, gather):
    me, peers = _me_and_peers()
    return [pltpu.make_async_remote_copy(
        src_ref=srcs[t] if gather else srcs[t].at[peer], dst_ref=zones[t].at[me], send_sem=send_sems.at[7 * t + kk],
        recv_sem=recv_sems.at[7 * t + kk], device_id=pos, device_id_type=pl.DeviceIdType.MESH)
        for t in range(len(srcs)) for kk, (pos, peer) in enumerate(peers)]


def _exchange_start(tensors, *, gather, name):
    n = len(tensors)
    zones = [lax.empty((N_DEV, *(t.shape if gather else t.shape[1:])), t.dtype) for t in tensors]

    def body(*refs):
        for cp in _exchange_copies(refs[:n], refs[n:2 * n], refs[2 * n], refs[2 * n + 1], gather):
            cp.start()
        refs[-1][...] = jnp.zeros_like(refs[-1])

    buffers = [pltpu.HBM(a.shape, a.dtype) for a in tensors + zones]
    res = pl.pallas_call(
        body, name=name, in_specs=[_HBM] * (2 * n),
        out_shape=(pltpu.SemaphoreType.DMA((7 * n,)), pltpu.SemaphoreType.DMA((7 * n,)), *buffers, jax.ShapeDtypeStruct((8, LANES), F32)),
        out_specs=(_SEM, _SEM, *[_HBM] * (2 * n), pl.BlockSpec(memory_space=pltpu.VMEM)),
        input_output_aliases={i: 2 + i for i in range(2 * n)},
        compiler_params=pltpu.CompilerParams(has_side_effects=pltpu.SideEffectType.DATAFLOW_SIDE_EFFECTING),
    )(*[pltpu.with_memory_space_constraint(a, pltpu.HBM) for a in tensors + zones])
    return dict(sems=res[:2], buffers=res[2:2 + 2 * n], gather=gather, started=res[-1])


def _exchange_wait(started, after, *, name):
    n = len(started["buffers"]) // 2

    def body(*refs):
        for cp in _exchange_copies(refs[:n], refs[n:2 * n], refs[2 * n], refs[2 * n + 1], started["gather"]):
            cp.wait_send()
            cp.wait_recv()

    res = pl.pallas_call(
        body, name=name, in_specs=[_HBM] * (2 * n) + [_SEM, _SEM] + [_ANY] * len(after),
        out_shape=tuple(pltpu.HBM(a.shape, a.dtype) for a in started["buffers"]), out_specs=tuple([_HBM] * (2 * n)),
        input_output_aliases={i: i for i in range(2 * n)},
        compiler_params=pltpu.CompilerParams(has_side_effects=pltpu.SideEffectType.DATAFLOW_SIDE_EFFECTING),
    )(*started["buffers"], *started["sems"], *after)
    return res[:n], res[n:]


def _adamw(contrib, w, m, v, *, name):
    rows, cols = w.shape
    tile = min(rows, ADAM_ROWS)

    def body(c_ref, w_ref, m_ref, v_ref, g_ref, d_ref, nm_ref, nv_ref):
        g = c_ref[0].astype(F32)
        for s in range(1, N_DEV):
            g = g + c_ref[s].astype(F32)
        m_new = ADAM_B1 * m_ref[...] + (1.0 - ADAM_B1) * g
        v_new = ADAM_B2 * v_ref[...] + (1.0 - ADAM_B2) * (g * g)
        m_hat = m_new / (1.0 - ADAM_B1 ** ADAM_STEP)
        v_hat = v_new / (1.0 - ADAM_B2 ** ADAM_STEP)
        g_ref[...] = g
        d_ref[...] = -ADAM_LR * (m_hat / (jnp.sqrt(v_hat) + ADAM_EPS) + ADAM_WD * w_ref[...])
        nm_ref[...] = m_new
        nv_ref[...] = v_new

    spec = pl.BlockSpec((tile, cols), lambda i: (i, 0))
    return pl.pallas_call(
        body, name=name, grid=(rows // tile,),
        in_specs=[pl.BlockSpec((N_DEV, tile, cols), lambda i: (0, i, 0)), spec, spec, spec], out_specs=[spec] * 4,
        out_shape=[jax.ShapeDtypeStruct((rows, cols), F32)] * 4, compiler_params=_cparams("parallel"),
    )(contrib, w, m, v)


class _Weights:
    def __init__(self, gathers, vectors, me):
        self.gathers, self.ready, self.me, self.after = gathers, dict(vectors), me, ()

    def arrive_after(self, *values):
        self.after = values

    def __getitem__(self, name):
        if name not in self.ready:
            gi = next(i for i, group in enumerate(GATHER_GROUPS) if name in group)
            after = [*self.after, *[g["started"] for g in self.gathers]]
            shards, zones = _exchange_wait(self.gathers[gi], after, name=f"gather_wait_{gi}")
            for n, shard, zone in zip(GATHER_GROUPS[gi], shards, zones, strict=True):
                blocks = lax.dynamic_update_slice_in_dim(zone, shard[None], self.me, 0)
                self.ready[n] = _to_kernel_layout(n, _from_blocks(blocks, n))
        return self.ready[name]


def kernel(x, mem, w_in, w_mem_kv, q_a_gain, w_q_b, kv_a_gain, w_kv_b, w_branch_mla, w_branch_sb, w_branch_mem, w_merge_gate, b_merge_gate, w_out, ln_gain, ln_bias, loss_target, m_w_in, m_w_mem_kv, m_q_a_gain, m_w_q_b, m_kv_a_gain, m_w_kv_b, m_w_branch_mla, m_w_branch_sb, m_w_branch_mem, m_w_merge_gate, m_b_merge_gate, m_w_out, m_ln_gain, m_ln_bias, v_w_in, v_w_mem_kv, v_q_a_gain, v_w_q_b, v_kv_a_gain, v_w_kv_b, v_w_branch_mla, v_w_branch_sb, v_w_branch_mem, v_w_merge_gate, v_b_merge_gate, v_w_out, v_ln_gain, v_ln_bias):
    given = dict(locals())
    small_names = [n for n, _ in SMALL]
    smalls = lambda prefix: [given[prefix + n] for n in small_names]
    me = 4 * lax.axis_index("x") + 2 * lax.axis_index("y") + lax.axis_index("c")

    gathers = [_exchange_start([given[n][0].astype(BF16) for n in group], gather=True, name=f"gather_start_{gi}")
               for gi, group in enumerate(GATHER_GROUPS)]
    w = _Weights(gathers, {n: given[n] for n in small_names}, me)
    exchanges = []
    results = [{}, {}, {}, {}]

    def finish(gi, after):
        names, started = exchanges[gi]
        sent, zones = _exchange_wait(started, after, name=f"grads_wait_{gi}")
        done = []
        for n, blocks, zone in zip(names, sent, zones, strict=True):
            own = lax.dynamic_index_in_dim(blocks, me, 0, keepdims=True)
            contrib = lax.dynamic_update_slice_in_dim(zone, own, me, 0)
            outs = _adamw(contrib, given[n][0], given["m_" + n][0], given["v_" + n][0], name=f"adamw_{n}")
            for kind, res in zip(results, outs, strict=True):
                kind[n] = res[None]
            done.append(outs[1])
        return done

    def emit(grads):
        blocks = [_to_blocks(_from_kernel_layout(n, g), n).astype(BF16) for n, g in grads.items()]
        exchanges.append((tuple(grads), _exchange_start(blocks, gather=False, name=f"grads_start_{len(exchanges)}")))
        started = [exchanges[-1][1]["started"]]
        if len(exchanges) == len(GRAD_GROUPS):
            for gi in range(len(GRAD_GROUPS) - 1):
                started += finish(gi, started[:1])
        return started

    loss, grad_x, grads = _local_step(x[0], mem[0], loss_target[0], w, emit)

    contrib_small = _share_small(_pack_small([grads[n] for n in small_names], loss), name="share_small")
    sml = _adamw(contrib_small, _pack_small(smalls("")), _pack_small(smalls("m_")), _pack_small(smalls("v_")), name="adamw_small")
    for kind, packed in zip(results, sml, strict=True):
        kind.update(zip(small_names, _unpack_small(packed), strict=True))
    finish(len(GRAD_GROUPS) - 1, [grad_x])
    order = ["w_in", "w_mem_kv", "q_a_gain", "w_q_b", "kv_a_gain", "w_kv_b", "w_branch_mla", "w_branch_sb", "w_branch_mem",
             "w_merge_gate", "b_merge_gate", "w_out", "ln_gain", "ln_bias"]
    loss_out = sml[0].reshape(-1)[LOSS_INDEX]
    return (loss_out, grad_x[None], *[kind[n] for kind in results for n in order])
```

```python
import math

import jax
import jax.numpy as jnp
from jax import lax
from jax.experimental import pallas as pl
from jax.experimental.pallas import tpu as pltpu

F32, BF16 = jnp.float32, jnp.bfloat16

N_DEV = 8
D_MODEL = 1024
MLA_HEADS, MLA_NOPE, MLA_ROPE, MLA_V = 8, 64, 32, 64
MLA_Q_LORA, MLA_KV_LORA = 256, 128
SB_HEAD_DIM = 64
MEM_HEAD_DIM = 128
ROPE_BASE = 10000.0
RMS_EPS = 1e-6
LN_EPS = 1e-5
DEEPNORM_ALPHA = 2.0 ** 0.25
ADAM_LR, ADAM_B1, ADAM_B2, ADAM_EPS, ADAM_WD, ADAM_STEP = 0.001, 0.9, 0.999, 1e-08, 0.01, 10
LOG2E, LN2 = math.log2(math.e), math.log(2.0)

LANES = 128
GROUPS = 4
PROJ_WIDTH = 4096
COL_CQ, COL_CKV, COL_KROPE, COL_GATE_A, COL_GATE_B, COL_GATE_M = 0, 256, 384, 512, 1024, 1536
QKV_FIRST, QKV_WIDTH = 2048, 2048
COL_QB, COL_KB, COL_VB, COL_QM = 0, 512, 1024, 1536
IN_PIECES = ((0, 416), None, (416, 512), (2464, 512), (3488, 512), (928, 512), (1440, 512), (1952, 512), (2976, 512))
IN_PAD = 96

VMEM_LIMIT_BYTES = 56 * 1024 * 1024
NEG_BIG = -1e30
Q_BLOCK = 512
MEM_Q_BLOCK = 2048
SB_BWD_Q_BLOCK = 512
TRI_BLOCK = 256
TILE_ROWS = 64
KEY_CHUNK = 512

SHARDED = {
    "w_in": ((1024, 4000), 1), "w_mem_kv": ((1024, 1024), 0), "w_q_b": ((256, 768), 1), "w_kv_b": ((128, 1024), 1),
    "w_branch_mla": ((512, 1024), 1), "w_branch_sb": ((512, 1024), 1), "w_branch_mem": ((512, 1024), 1),
    "w_merge_gate": ((1024, 3072), 1), "w_out": ((1024, 1024), 0),
}
GATHER_GROUPS = (("w_in",), ("w_merge_gate",), ("w_q_b", "w_kv_b", "w_mem_kv", "w_branch_mla", "w_branch_sb", "w_branch_mem", "w_out"))
GRAD_GROUPS = (("w_out", "w_merge_gate", "w_branch_mla", "w_branch_sb", "w_branch_mem"), ("w_mem_kv", "w_q_b", "w_kv_b"), ("w_in",))
SMALL = (("q_a_gain", 256), ("kv_a_gain", 128), ("b_merge_gate", 3072), ("ln_gain", 1024), ("ln_bias", 1024))
SMALL_ROWS, SMALL_LANES = 48, 128
ADAM_ROWS = 256
LOSS_INDEX = 5504


def _cparams(*sem):
    return pltpu.CompilerParams(dimension_semantics=sem or None, vmem_limit_bytes=VMEM_LIMIT_BYTES)


_DIMS = {"nn": (((1,), (0,)), ((), ())), "nt": (((1,), (1,)), ((), ())), "tn": (((0,), (0,)), ((), ()))}


def _dot(a, b, dims):
    return lax.dot_general(a, b, _DIMS[dims], preferred_element_type=F32)


def _tile(dim, want):
    if dim <= want:
        return dim
    t = want - want % LANES
    while dim % t:
        t -= LANES
    assert t > 0, (dim, want)
    return t


_ANY = pl.BlockSpec(memory_space=pl.ANY)


def _mm(a, b, dims, *, name, out_dtype=F32, add=None, add_scale=1.0, col_scale=None, b_cols=None, behind=None,
        tm=1024, tn=1024, tk=1024):
    batch = a.shape[0] if a.ndim == 3 else None
    if dims == "nn":
        (m, k), (k2, n) = a.shape[-2:], b.shape[-2:]
    elif dims == "nt":
        (m, k), (n, k2) = a.shape[-2:], b.shape[-2:]
    else:
        (k, m), (k2, n) = a.shape[-2:], b.shape[-2:]
    assert k == k2 and a.ndim == b.ndim, (a.shape, b.shape, dims)
    assert batch is None or (b.shape[0] == batch and add is None and col_scale is None and b_cols is None)
    b_first = 0
    if b_cols is not None:
        assert dims == "nn"
        b_first, n = b_cols
    tm, tn, tk = _tile(m, tm), _tile(n, tn), _tile(k, tk)
    assert b_first % tn == 0
    jb = b_first // tn
    nk = k // tk

    def spec(block, index):
        if batch is None:
            return pl.BlockSpec(block, lambda bb, i, j, kk: index(i, j, kk))
        return pl.BlockSpec((None, *block), lambda bb, i, j, kk: (bb, *index(i, j, kk)))

    a_spec = spec((tk, tm), lambda i, j, kk: (kk, i)) if dims == "tn" else spec((tm, tk), lambda i, j, kk: (i, kk))
    b_spec = spec((tn, tk), lambda i, j, kk: (j, kk)) if dims == "nt" else spec((tk, tn), lambda i, j, kk: (kk, jb + j))
    o_spec = spec((tm, tn), lambda i, j, kk: (i, j))
    behind = [] if behind is None else behind if isinstance(behind, (list, tuple)) else [behind]
    optional = [(add, o_spec), (col_scale, pl.BlockSpec((1, tn), lambda bb, i, j, kk: (0, j))), *[(v, _ANY) for v in behind]]
    present = [(v, spec) for v, spec in optional if v is not None]

    def body(*refs):
        a_ref, b_ref = refs[:2]
        extra = iter(refs[2:2 + len(present)])
        add_ref = next(extra) if add is not None else None
        scale_ref = next(extra) if col_scale is not None else None
        o_ref = refs[2 + len(present)]
        part = _dot(a_ref[...].astype(BF16), b_ref[...].astype(BF16), dims)

        def finish(r):
            if add is not None:
                r = r + add_scale * add_ref[...]
            if col_scale is not None:
                r = r * scale_ref[...]
            o_ref[...] = r.astype(out_dtype)

        if nk == 1:
            finish(part)
            return
        acc = refs[-1]
        kk = pl.program_id(3)

        @pl.when(kk == 0)
        def _():
            acc[...] = part

        @pl.when(kk > 0)
        def _():
            acc[...] += part

        @pl.when(kk == nk - 1)
        def _():
            finish(acc[...])

    return pl.pallas_call(
        body, name=name, grid=(batch or 1, m // tm, n // tn, nk),
        in_specs=[a_spec, b_spec] + [spec for _, spec in present], out_specs=o_spec,
        out_shape=jax.ShapeDtypeStruct((m, n) if batch is None else (batch, m, n), out_dtype),
        scratch_shapes=[pltpu.VMEM((tm, tn), F32)] if nk > 1 else [],
        compiler_params=_cparams("parallel", "parallel", "parallel", "arbitrary"),
    )(a, b, *[v for v, _ in present])


def _rowwise(fn, ins, outs, *, name, rows, tr=512):
    n_in = len(ins)
    tr = min(tr, rows)
    in_specs, args = [], []
    for it in ins:
        if isinstance(it, tuple) and it[0] == "whole":
            in_specs.append(pl.BlockSpec(it[1].shape, lambda i, nd=it[1].ndim: (0,) * nd))
            args.append(it[1])
            continue
        arr, w, off = it if isinstance(it, tuple) else (it, it.shape[-1], 0)
        assert off % w == 0
        cb = off // w
        if arr.ndim == 3:
            in_specs.append(pl.BlockSpec((arr.shape[0], tr, w), lambda i, cb=cb: (0, i, cb)))
        elif arr.shape[0] == 1:
            in_specs.append(pl.BlockSpec((1, w), lambda i, cb=cb: (0, cb)))
        else:
            in_specs.append(pl.BlockSpec((tr, w), lambda i, cb=cb: (i, cb)))
        args.append(arr)
    out_shape, out_specs, is_sum = [], [], []
    for out in outs:
        is_sum.append(out[0] == "sum")
        if out[0] == "sum":
            shape = out[1] if isinstance(out[1], tuple) else (1, out[1])
            out_shape.append(jax.ShapeDtypeStruct(shape, F32))
            out_specs.append(pl.BlockSpec(shape, lambda i: (0, 0)))
        elif len(out) == 3:
            out_shape.append(jax.ShapeDtypeStruct((out[0], rows, out[1]), out[2]))
            out_specs.append(pl.BlockSpec((out[0], tr, out[1]), lambda i: (0, i, 0)))
        else:
            out_shape.append(jax.ShapeDtypeStruct((rows, out[0]), out[1]))
            out_specs.append(pl.BlockSpec((tr, out[0]), lambda i: (i, 0)))

    def body(*refs):
        res = fn(*[r[...] for r in refs[:n_in]])
        for r, val, s in zip(refs[n_in:], res, is_sum, strict=True):
            if s:
                @pl.when(pl.program_id(0) == 0)
                def _(r=r):
                    r[...] = jnp.zeros_like(r)

                r[...] += val
            elif isinstance(val, (list, tuple)):
                for n, part in enumerate(val):
                    r[n] = part.astype(r.dtype)
            else:
                r[...] = val.astype(r.dtype)

    return pl.pallas_call(
        body, name=name, grid=(rows // tr,), in_specs=in_specs, out_specs=out_specs, out_shape=out_shape,
        compiler_params=_cparams("arbitrary"),
    )(*args)


def _colsum(v):
    return jnp.sum(v, axis=0, keepdims=True)


def _sigmoid(v):
    return 1.0 / (1.0 + jnp.exp(-v))


def _lane_groups(v):
    return [v[:, g * LANES:(g + 1) * LANES] for g in range(v.shape[1] // LANES)]


def _swap_halves(v, first_lane):
    lane = lax.broadcasted_iota(jnp.int32, v.shape, 1)
    return jnp.where(lane < first_lane + 16, pltpu.roll(v, 112, axis=1), pltpu.roll(v, 16, axis=1))


def _lane_sum(acc, v):
    for part in _lane_groups(v):
        acc = acc + part
    return acc


def _low_half(shape):
    return lax.broadcasted_iota(jnp.int32, shape, 1) < LANES // 2


def _select_heads(per_head, pick):
    if len(per_head) == 1:
        return pick(per_head[0], 0)
    return jnp.where(_low_half(per_head[0].shape), pick(per_head[0], 0), pick(per_head[1], 1))


def _attn_specs(s, sk, hp, bq, q0, k0, v0):
    wq = hp * LANES
    assert q0 % wq == 0 and k0 % wq == 0 and v0 % LANES == 0
    qb0, kb0, vb0 = q0 // wq, k0 // wq, v0 // LANES
    q_spec = pl.BlockSpec((bq, wq), lambda g, i: (i, qb0 + g))
    k_spec = pl.BlockSpec((sk, wq), lambda g, i: (0, kb0 + g))
    v_spec = pl.BlockSpec((sk, LANES), lambda g, i: (0, vb0 + g))
    row_out = lambda w: pl.BlockSpec((bq, w), lambda g, i: (i, g))
    key_out = lambda w: pl.BlockSpec((sk, w), lambda g, i: (0, g))
    return q_spec, k_spec, v_spec, row_out, key_out


def _chunks(i, bq, ch, sk, causal):
    return ((i + 1) * bq - 1) // ch if causal else jnp.int32(sk // ch - 1)


def _positions(i, c, bq, ch):
    return (i * bq + lax.broadcasted_iota(jnp.int32, (bq, ch), 0), c * ch + lax.broadcasted_iota(jnp.int32, (bq, ch), 1))


def _softmax_fwd(q, k, v, *, hp, causal, name, q0=0, k0=0, v0=0, q_rows=Q_BLOCK):
    s, sk = q.shape[0], k.shape[0]
    bq, ch = min(q_rows, s), min(KEY_CHUNK, sk)
    assert not causal or bq <= ch
    q_spec, k_spec, v_spec, row_out, _ = _attn_specs(s, sk, hp, bq, q0, k0, v0)

    def body(q_ref, k_ref, v_ref, o_ref, lse_ref, s_scr):
        i = pl.program_id(1)
        qs = _lane_groups(q_ref[...])
        last = _chunks(i, bq, ch, sk, causal)

        def scores(c, ms, masked):
            off = pl.multiple_of(c * ch, ch)
            out = []
            for j in range(hp):
                sc = _dot(qs[j], k_ref[pl.ds(off, ch), j * LANES:(j + 1) * LANES], "nt")
                if masked:
                    qpos, kpos = _positions(i, c, bq, ch)
                    sc = jnp.where(kpos <= qpos, sc, NEG_BIG)
                s_scr[j, c] = sc
                m = ms[j]
                for part in _lane_groups(sc):
                    m = jnp.maximum(m, part)
                out.append(m)
            return tuple(out)

        ms = lax.fori_loop(0, last, lambda c, m: scores(c, m, False), tuple(jnp.full((bq, LANES), NEG_BIG, F32) for _ in range(hp)))
        ms = scores(last, ms, causal)
        row_max = [jnp.max(m, axis=1, keepdims=True) for m in ms]

        def weigh(c, carry):
            off = pl.multiple_of(c * ch, ch)
            vt = v_ref[pl.ds(off, ch), :]
            out = []
            for j in range(hp):
                l, acc = carry[j]
                p = jnp.exp2(s_scr[j, c] - row_max[j])
                out.append((_lane_sum(l, p), acc + _dot(p.astype(BF16), vt, "nn")))
            return tuple(out)

        zero = jnp.zeros((bq, LANES), F32)
        res = lax.fori_loop(0, last + 1, weigh, tuple((zero, zero) for _ in range(hp)))
        row_sum = [jnp.sum(l, axis=1, keepdims=True) for l, _ in res]
        o_ref[...] = _select_heads([acc for _, acc in res], lambda acc, j: acc / row_sum[j])
        lse_ref[...] = _select_heads([jnp.broadcast_to(row_max[j] + jnp.log2(row_sum[j]), (bq, LANES)) for j in range(hp)], lambda a, j: a)

    return pl.pallas_call(
        body, name=name, grid=(GROUPS, s // bq), in_specs=[q_spec, k_spec, v_spec], out_specs=[row_out(LANES), row_out(LANES)],
        out_shape=[jax.ShapeDtypeStruct((s, GROUPS * LANES), F32)] * 2,
        scratch_shapes=[pltpu.VMEM((hp, sk // ch, bq, ch), F32)], compiler_params=_cparams("parallel", "arbitrary"),
    )(q, k, v)


def _head_cotangent(do, j, hp):
    if hp == 1:
        return do
    return jnp.where(_low_half(do.shape) == (j == 0), do, 0.0)


def _softmax_bwd(q, k, v, o, do, lse, behind, *, hp, causal, dq_scale, name, q0=0, k0=0, v0=0, q_rows=Q_BLOCK):
    s, sk = q.shape[0], k.shape[0]
    bq, ch = min(q_rows, s), min(KEY_CHUNK, sk)
    assert not causal or bq <= ch
    wq = hp * LANES
    q_spec, k_spec, v_spec, row_out, key_out = _attn_specs(s, sk, hp, bq, q0, k0, v0)

    def body(q_ref, k_ref, v_ref, o_ref, do_ref, lse_ref, _, dq_ref, dk_ref, dv_ref, dk_t, dv_t):
        i = pl.program_id(1)

        @pl.when(i == 0)
        def _():
            dk_t[...] = jnp.zeros_like(dk_t)
            dv_t[...] = jnp.zeros_like(dv_t)

        qs = _lane_groups(q_ref[...])
        do_all, o_all, lse_all = do_ref[...], o_ref[...], lse_ref[...]
        dos, deltas, lses = [], [], []
        for j in range(hp):
            d = _head_cotangent(do_all, j, hp)
            deltas.append(jnp.sum(d * o_all, axis=1, keepdims=True))
            dos.append(d.astype(BF16))
            lses.append(lse_all[:, j * (LANES // hp):j * (LANES // hp) + 1])
        last = _chunks(i, bq, ch, sk, causal)

        def chunk(c, dqs, masked):
            off = pl.multiple_of(c * ch, ch)
            vt = v_ref[pl.ds(off, ch), :]
            out, dks, dv = [], [], None
            for j in range(hp):
                kt = k_ref[pl.ds(off, ch), j * LANES:(j + 1) * LANES]
                p = jnp.exp2(_dot(qs[j], kt, "nt") - lses[j])
                if masked:
                    qpos, kpos = _positions(i, c, bq, ch)
                    p = jnp.where(kpos <= qpos, p, 0.0)
                ds = (p * (_dot(dos[j], vt, "nt") - deltas[j]) * LN2).astype(BF16)
                out.append(dqs[j] + _dot(ds, kt, "nn"))
                dks.append(_dot(qs[j], ds, "tn"))
                dvj = _dot(dos[j], p.astype(BF16), "tn")
                dv = dvj if dv is None else dv + dvj
            dk_t[c] += dks[0] if hp == 1 else jnp.concatenate(dks, axis=0)
            dv_t[c] += dv
            return tuple(out)

        dqs = lax.fori_loop(0, last, lambda c, d: chunk(c, d, False), tuple(jnp.zeros((bq, LANES), F32) for _ in range(hp)))
        dqs = chunk(last, dqs, causal)
        dq_ref[...] = (dqs[0] if hp == 1 else jnp.concatenate(dqs, axis=1)) * dq_scale

        @pl.when(i == s // bq - 1)
        def _():
            for c in range(sk // ch):
                dk_ref[c * ch:(c + 1) * ch, :] = dk_t[c].T
                dv_ref[c * ch:(c + 1) * ch, :] = dv_t[c].T

    return pl.pallas_call(
        body, name=name, grid=(GROUPS, s // bq),
        in_specs=[q_spec, k_spec, v_spec, row_out(LANES), row_out(LANES), row_out(LANES), _ANY],
        out_specs=[row_out(wq), key_out(wq), key_out(LANES)],
        out_shape=[jax.ShapeDtypeStruct((s, GROUPS * wq), F32), jax.ShapeDtypeStruct((sk, GROUPS * wq), F32),
                   jax.ShapeDtypeStruct((sk, GROUPS * LANES), F32)],
        scratch_shapes=[pltpu.VMEM((sk // ch, wq, ch), F32), pltpu.VMEM((sk // ch, LANES, ch), F32)],
        compiler_params=_cparams("arbitrary", "arbitrary"),
    )(q, k, v, o, do, lse, behind)


def _log2_sigmoid_pair(z2):
    minus_abs = lax.bitcast_convert_type(lax.bitcast_convert_type(z2, jnp.uint32) | jnp.uint32(0x80000000), F32)
    log_beta = jnp.minimum(z2, 0.0) - jnp.log2(1.0 + jnp.exp2(minus_abs))
    return log_beta, log_beta - z2


def _tilewise(fn, *arrays):
    rows, cols = arrays[0].shape
    step = min(TILE_ROWS, rows)
    grid = [[fn(*[None if a is None else a[r:r + step, c:c + LANES] for a in arrays]) for c in range(0, cols, LANES)]
            for r in range(0, rows, step)]
    return [jnp.concatenate([jnp.concatenate([cell[k] for cell in row], axis=1) for row in grid], axis=0)
            for k in range(len(grid[0][0]))]


def _split(v):
    hi = v.astype(BF16)
    return hi, (v - hi.astype(F32)).astype(BF16)


def _tri(n, after):
    rows, cols = lax.broadcasted_iota(jnp.int32, (n, n), 0), lax.broadcasted_iota(jnp.int32, (n, n), 1)
    return (rows > cols if after else rows < cols).astype(BF16)


def _running_sums(v, terms, start, tri, backwards):
    n = tri.shape[0]
    n_blocks = v.shape[1] // n
    order = range(n_blocks - 1, -1, -1) if backwards else range(n_blocks)
    stacked = tri if len(terms) == 1 else jnp.concatenate([tri] * len(terms), axis=0)
    parts, run = [None] * n_blocks, start
    for t in order:
        cols = slice(t * n, (t + 1) * n)
        lhs = terms[0][:, cols] if len(terms) == 1 else jnp.concatenate([term[:, cols] for term in terms], axis=1)
        parts[t] = _dot(lhs, stacked, "nn") + run
        run = run + jnp.sum(v[:, cols], axis=1, keepdims=True)
    return (parts[0] if n_blocks == 1 else jnp.concatenate(parts, axis=1)), run


def _sb_weights(qm, kt, run, tri, strict):
    def logs(z2, keep):
        log_beta, log_keep = _log2_sigmoid_pair(z2)
        if keep is not None:
            log_keep = jnp.where(keep, log_keep, 0.0)
        return log_beta, log_keep, *_split(log_keep)

    log_beta, log_keep, hi, lo = _tilewise(logs, _dot(qm, kt, "nt"), strict)
    behind, run = _running_sums(log_keep, (hi, lo), run, tri, True)

    def weigh(log_beta, behind, keep):
        a = jnp.exp2(log_beta + behind)
        return (a if keep is None else jnp.where(keep, a, 0.0),)

    (a,) = _tilewise(weigh, log_beta, behind, strict)
    return a, log_beta, run


def _sb_queries(q_all):
    low = _low_half(q_all.shape)
    zero = jnp.zeros_like(q_all)
    return [jnp.where(low, q_all, zero), jnp.where(low, zero, q_all)]


def _sb_fwd(qkv, *, q0, k0, v0, name):
    s = qkv.shape[0]
    bq, ch = min(Q_BLOCK, s), min(KEY_CHUNK, s)
    assert bq == ch
    n_q = s // bq
    q_spec, k_spec, v_spec, row_out, _ = _attn_specs(s, s, 1, bq, q0, k0, v0)

    def body(q_ref, k_ref, v_ref, o_ref, saved_ref, stage, sems):
        g, i = pl.program_id(0), pl.program_id(1)
        qms = _sb_queries(q_ref[...])
        tri = _tri(min(TRI_BLOCK, ch), True)
        last = _chunks(i, bq, ch, s, True)
        first_tile = i * (i + 1) // 2

        def save(slot, c):
            return pltpu.make_async_copy(stage.at[slot], saved_ref.at[g, first_tile + c], sems.at[slot])

        def chunk(c, step, carry, masked):
            off = pl.multiple_of(c * ch, ch)
            kt, vt = k_ref[pl.ds(off, ch), :], v_ref[pl.ds(off, ch), :]
            strict = None
            if masked:
                qpos, kpos = _positions(i, c, bq, ch)
                strict = kpos < qpos
            slot = step % 2
            if not masked:
                @pl.when(step >= 2)
                def _():
                    save(slot, c).wait()
            out = []
            for j in range(2):
                run, acc = carry[j]
                a, log_beta, run = _sb_weights(qms[j], kt, run, tri, strict)
                a = a.astype(BF16)
                stage[slot, j] = a
                stage[slot, 2 + j] = jnp.exp2(log_beta).astype(BF16)
                out.append((run, acc + _dot(a, vt, "nn")))
            save(slot, c).start()
            return tuple(out)

        carry = chunk(last, 0, tuple((jnp.zeros((bq, 1), F32), jnp.zeros((bq, LANES), F32)) for _ in range(2)), True)
        res = lax.fori_loop(0, last, lambda n, c: chunk(last - 1 - n, n + 1, c, False), carry)
        save(last % 2, 0).wait()

        @pl.when(last >= 1)
        def _():
            save((last + 1) % 2, 0).wait()

        o_ref[...] = _select_heads([acc for _, acc in res], lambda acc, j: acc)

    return pl.pallas_call(
        body, name=name, grid=(GROUPS, n_q), in_specs=[q_spec, k_spec, v_spec], out_specs=[row_out(LANES), _ANY],
        out_shape=[jax.ShapeDtypeStruct((s, GROUPS * LANES), F32),
                   jax.ShapeDtypeStruct((GROUPS, n_q * (n_q + 1) // 2, 4, bq, ch), BF16)],
        scratch_shapes=[pltpu.VMEM((2, 4, bq, ch), BF16), pltpu.SemaphoreType.DMA((2,))],
        compiler_params=_cparams("parallel", "arbitrary"),
    )(qkv, qkv, qkv)


def _sb_bwd(qkv, do, saved, behind, *, q0, k0, v0, dq_scale, name):
    s = qkv.shape[0]
    bq, ch = min(SB_BWD_Q_BLOCK, s), min(KEY_CHUNK, s)
    assert bq == ch and saved.shape[2:] == (4, bq, ch)
    q_spec, k_spec, v_spec, row_out, key_out = _attn_specs(s, s, 1, bq, q0, k0, v0)

    def body(q_ref, k_ref, v_ref, do_ref, saved_ref, _, dq_ref, dk_ref, dv_ref, g_s, beta_s, stage, sems):
        g_index, i = pl.program_id(0), pl.program_id(1)

        @pl.when(i == 0)
        def _():
            dk_ref[...] = jnp.zeros_like(dk_ref)
            dv_ref[...] = jnp.zeros_like(dv_ref)

        qms = _sb_queries(q_ref[...])
        do_all = do_ref[...]
        dos = [_head_cotangent(do_all, j, 2).astype(BF16) for j in range(2)]
        dos_ln2 = [(_head_cotangent(do_all, j, 2) * LN2).astype(BF16) for j in range(2)]
        tri_before = _tri(min(TRI_BLOCK, ch), False)
        last = _chunks(i, bq, ch, s, True)
        first_tile = i * (i + 1) // 2

        def strict_mask(c):
            qpos, kpos = _positions(i, c, bq, ch)
            return kpos < qpos

        def fetch(slot, c):
            return pltpu.make_async_copy(saved_ref.at[g_index, first_tile + c], stage.at[slot], sems.at[slot])

        def sweep1(n, unused):
            c, slot = last - n, n % 2

            @pl.when(c >= 1)
            def _():
                fetch(1 - slot, c - 1).start()

            fetch(slot, c).wait()
            off = pl.multiple_of(c * ch, ch)
            vt = v_ref[pl.ds(off, ch), :]
            dv = None
            for j in range(2):
                a = stage[slot, j]
                g_s[j, c] = (a.astype(F32) * _dot(dos_ln2[j], vt, "nt")).astype(BF16)
                beta_s[j, c] = stage[slot, 2 + j]
                dvj = _dot(a, dos[j], "tn")
                dv = dvj if dv is None else dv + dvj
            dv_ref[pl.ds(off, ch), :] += dv
            return unused

        fetch(0, last).start()
        lax.fori_loop(0, last + 1, sweep1, 0)

        def sweep2(c, carry, masked):
            off = pl.multiple_of(c * ch, ch)
            kt = k_ref[pl.ds(off, ch), :]
            out, dk = [], None
            for j in range(2):
                before, dq = carry[j]
                g16, beta = g_s[j, c], beta_s[j, c].astype(F32)
                g = g16.astype(F32)
                in_front, before = _running_sums(g, (g16,), before, tri_before, False)
                dz = g * (1.0 - beta) - beta * in_front
                if masked:
                    dz = jnp.where(strict_mask(c), dz, 0.0)
                dz = dz.astype(BF16)
                dkj = _dot(dz, qms[j], "tn")
                dk = dkj if dk is None else dk + dkj
                out.append((before, dq + _dot(dz, kt, "nn")))
            dk_ref[pl.ds(off, ch), :] += dk
            return tuple(out)

        carry = lax.fori_loop(0, last, lambda c, cr: sweep2(c, cr, False),
                              tuple((jnp.zeros((bq, 1), F32), jnp.zeros((bq, LANES), F32)) for _ in range(2)))
        res = sweep2(last, carry, True)
        dq_ref[...] = _select_heads([dq for _, dq in res], lambda dq, j: dq) * dq_scale

    n_ch = s // ch
    return pl.pallas_call(
        body, name=name, grid=(GROUPS, s // bq), in_specs=[q_spec, k_spec, v_spec, row_out(LANES), _ANY, _ANY],
        out_specs=[row_out(LANES), key_out(LANES), key_out(LANES)],
        out_shape=[jax.ShapeDtypeStruct((s, GROUPS * LANES), F32)] * 3,
        scratch_shapes=[pltpu.VMEM((2, n_ch, bq, ch), BF16)] * 2 + [pltpu.VMEM((2, 4, bq, ch), BF16), pltpu.SemaphoreType.DMA((2,))],
        compiler_params=_cparams("arbitrary", "arbitrary"),
    )(qkv, qkv, qkv, do, saved, behind)


def _rope_tables(s):
    half = MLA_ROPE // 2
    freqs = ROPE_BASE ** (-jnp.arange(half, dtype=F32) / half)
    ang = jnp.arange(s, dtype=F32)[:, None] * freqs[None, :]
    cos, sin = jnp.cos(ang), jnp.sin(ang)
    tail = jnp.zeros((s, LANES - MLA_NOPE - MLA_ROPE), F32)
    lead = lambda fill: jnp.full((s, MLA_NOPE), fill, F32)
    return dict(
        cos_k0=jnp.concatenate([cos, cos, lead(0.0), tail], axis=1), sin_k0=jnp.concatenate([-sin, sin, lead(0.0), tail], axis=1),
        cos_k64=jnp.concatenate([lead(0.0), cos, cos, tail], axis=1), sin_k64=jnp.concatenate([lead(0.0), -sin, sin, tail], axis=1),
        cos_q=jnp.concatenate([lead(1.0), cos, cos, tail], axis=1),
        sin_k64_t=jnp.concatenate([lead(0.0), sin, -sin, tail], axis=1),
    )


def _local_step(x, mem, target, w, emit=lambda grads: [jnp.zeros((8, LANES), F32)]):
    s = x.shape[0]
    rope = _rope_tables(s)
    xb = x.astype(BF16)
    inv_d = 1.0 / D_MODEL
    scale_a = LOG2E / math.sqrt(MLA_NOPE + MLA_ROPE)
    scale_b = LOG2E / math.sqrt(SB_HEAD_DIM)
    scale_m = LOG2E / math.sqrt(MEM_HEAD_DIM)
    arrive_after = getattr(w, "arrive_after", lambda *values: None)
    memb = mem.astype(BF16)

    arrive_after(xb, memb, *rope.values())
    proj = _mm(xb, w["w_in"], "nn", name="proj", b_cols=(0, QKV_FIRST))
    one = jnp.ones((1, 512), F32)
    qkv = _mm(xb, w["w_in"], "nn", name="proj_qkv", b_cols=(QKV_FIRST, QKV_WIDTH), out_dtype=BF16,
              col_scale=jnp.concatenate([one * scale_b, one, one, one * scale_m], axis=1))
    arrive_after(qkv)
    pre = _mm(xb, w["w_merge_gate"], "nn", name="merge_pre", out_dtype=BF16)
    arrive_after(pre)

    def mla_inputs(c_q, c_kv, k_rope, g_q, g_kv, w_q, w_kv, cos_q, sin_q, cos_k, sin_k):
        n_q = (c_q * lax.rsqrt(jnp.mean(c_q * c_q, axis=1, keepdims=True) + RMS_EPS) * g_q).astype(BF16)
        n_kv = (c_kv * lax.rsqrt(jnp.mean(c_kv * c_kv, axis=1, keepdims=True) + RMS_EPS) * g_kv).astype(BF16)
        q_a = _dot(n_q, w_q, "nn")
        kv_a = _dot(n_kv, w_kv, "nn").astype(BF16)
        q = jnp.concatenate([(g * cos_q + _swap_halves(g, MLA_NOPE) * sin_q) * scale_a for g in _lane_groups(q_a)], axis=1)
        k_pe = pltpu.roll(k_rope * cos_k + _swap_halves(k_rope, 0) * sin_k, MLA_NOPE, axis=1).astype(BF16)
        k = jnp.concatenate([g + k_pe for g in _lane_groups(kv_a[:, :1024])], axis=1)
        return n_q, n_kv, q, k, kv_a[:, 1024:]

    n_q, n_kv, q_mla, k_mla, v_a = _rowwise(
        mla_inputs, [(proj, 256, COL_CQ), (proj, 128, COL_CKV), (proj, 128, COL_KROPE), w["q_a_gain"], w["kv_a_gain"],
                     ("whole", w["w_q_b"]), ("whole", w["w_kv_b"]), rope["cos_q"], rope["sin_k64"], rope["cos_k0"], rope["sin_k0"]],
        [(256, BF16), (128, BF16), (1024, BF16), (1024, BF16), (512, BF16)], name="mla_inputs", rows=s)
    o_a, lse_a = _softmax_fwd(q_mla, k_mla, v_a, hp=2, causal=True, name="mla_fwd")

    o_b, sb_saved = _sb_fwd(qkv, q0=COL_QB, k0=COL_KB, v0=COL_VB, name="sb_fwd")

    mem_kv =_mm(memb, w["w_mem_kv"], "nn", name="mem_kv", out_dtype=BF16)
    o_m, lse_m = _softmax_fwd(qkv, mem_kv, mem_kv, hp=1, causal=False, name="mem_fwd", q0=1536, v0=512, q_rows=MEM_Q_BLOCK)

    branches = ("mla", "sb", "mem")
    w_branch = jnp.stack([w[f"w_branch_{br}"] for br in branches])
    bias = w["b_merge_gate"]

    def head(oa, ob, om, ga, gb, gm, pa, pb, pm, ba, bb, bm, xv, tv, gain, bias_ln, w_b, w_o, w_g):
        us, ys, gs = [], [], []
        for n, (o, gate, p, b) in enumerate(((oa, ga, pa, ba), (ob, gb, pb, bb), (om, gm, pm, bm))):
            us.append((o * gate * _sigmoid(gate)).astype(BF16))
            ys.append(_dot(us[n], w_b[n], "nn"))
            gs.append(_sigmoid(p.astype(F32) + b))
        merged = (gs[0] * ys[0] + gs[1] * ys[1] + gs[2] * ys[2]).astype(BF16)
        z = DEEPNORM_ALPHA * xv + _dot(merged, w_o, "nn")
        zc = z - jnp.mean(z, axis=1, keepdims=True)
        rstd = lax.rsqrt(jnp.mean(zc * zc, axis=1, keepdims=True) + LN_EPS)
        xhat = zc * rstd
        err = xhat * gain + bias_ln - tv
        loss = 0.5 * jnp.sum(jnp.mean(err * err, axis=1, keepdims=True), axis=0, keepdims=True)
        dy = err * inv_d
        dxhat = dy * gain
        dz = rstd * (dxhat - jnp.mean(dxhat, axis=1, keepdims=True) - xhat * jnp.mean(dxhat * xhat, axis=1, keepdims=True))
        dz16 = dz.astype(BF16)
        dm = _dot(dz16, w_o, "nt")
        dpre = jnp.concatenate([dm * ys[n] * gs[n] * (1.0 - gs[n]) for n in range(3)], axis=1)
        dx = DEEPNORM_ALPHA * dz + _dot(dpre.astype(BF16), w_g, "nt")
        dys = [(dm * gs[n]).astype(BF16) for n in range(3)]
        d_os, d_gates = [], []
        for n, (o, gate) in enumerate(((oa, ga), (ob, gb), (om, gm))):
            du, sg = _dot(dys[n], w_b[n], "nt"), _sigmoid(gate)
            d_os.append(du * gate * sg)
            d_gates.append(du * o * sg * (1.0 + gate * (1.0 - sg)))
        return (us, merged, dx, dz16, _colsum(dy * xhat), _colsum(dy), jnp.broadcast_to(loss, (1, LANES)), dpre, _colsum(dpre),
                dys, *d_os, *d_gates)

    grads = {}
    (u, merged, dx, dzb, grads["ln_gain"], grads["ln_bias"], loss, dpre, grads["b_merge_gate"], dy, *rest) = _rowwise(
        head, [o_a, o_b, o_m, (proj, 512, COL_GATE_A), (proj, 512, COL_GATE_B), (proj, 512, COL_GATE_M),
               (pre, 1024, 0), (pre, 1024, 1024), (pre, 1024, 2048), (bias, 1024, 0), (bias, 1024, 1024), (bias, 1024, 2048),
               x, target, w["ln_gain"], w["ln_bias"], ("whole", w_branch), ("whole", w["w_out"]), ("whole", w["w_merge_gate"])],
        [(3, 512, BF16), (1024, BF16), (1024, F32), (1024, BF16), ("sum", 1024), ("sum", 1024), ("sum", LANES),
         (3072, BF16), ("sum", 3072), (3, 1024, BF16)] + [(512, F32)] * 3 + [(512, BF16)] * 3, name="head", rows=s, tr=256)
    d_o, d_gate = dict(zip(branches, rest[:3], strict=True)), dict(zip(branches, rest[3:], strict=True))

    grads["w_out"] = _mm(merged, dzb, "tn", name="g_w_out", out_dtype=BF16)
    grads["w_merge_gate"] = _mm(xb, dpre, "tn", name="g_w_merge", out_dtype=BF16)
    g_w_branch = _mm(u, dy, "tn", name="g_w_branch", out_dtype=BF16)
    for n, br in enumerate(branches):
        grads[f"w_branch_{br}"] = g_w_branch[n]
    (sent,) = emit({n: grads[n] for n in ("w_out", "w_merge_gate", "w_branch_mla", "w_branch_sb", "w_branch_mem")})

    dq_m, dk_m, dv_m = _softmax_bwd(qkv, mem_kv, mem_kv, o_m, d_o["mem"], lse_m, sent, hp=1, causal=False, dq_scale=scale_m,
                                    name="mem_bwd", q0=1536, v0=512, q_rows=MEM_Q_BLOCK)
    grads["w_mem_kv"] = _mm(memb, jnp.concatenate([dk_m, dv_m], axis=1), "tn", name="g_w_mem_kv", out_dtype=BF16)

    dq_sb, dk_sb, dv_sb = _sb_bwd(qkv, d_o["sb"], sb_saved, sent, q0=COL_QB, k0=COL_KB, v0=COL_VB, dq_scale=scale_b, name="sb_bwd")

    dq_mla, dk_mla, dv_a = _softmax_bwd(q_mla, k_mla, v_a, o_a, d_o["mla"], lse_a, sent, hp=2, causal=True, dq_scale=scale_a,
                                        name="mla_bwd")

    def mla_inputs_bwd(dq, dk, dv, n_q, n_kv, c_q, c_kv, g_q, g_kv, w_q, w_kv, cos_q, sin_q, cos_k, sin_k):
        dq_a = jnp.concatenate([g * cos_q + _swap_halves(g, MLA_NOPE) * sin_q for g in _lane_groups(dq)], axis=1).astype(BF16)
        groups = _lane_groups(dk)
        g_rope = groups[0]
        for other in groups[1:]:
            g_rope = g_rope + other
        dk_rope = pltpu.roll(g_rope * cos_k + _swap_halves(g_rope, MLA_NOPE) * sin_k, MLA_NOPE, axis=1)
        nope = _low_half(g_rope.shape)
        dkv_a = jnp.concatenate([jnp.where(nope, grp, 0.0) for grp in groups] + [dv], axis=1).astype(BF16)
        res = []
        for c, dn, g in ((c_q, _dot(dq_a, w_q, "nt"), g_q), (c_kv, _dot(dkv_a, w_kv, "nt"), g_kv)):
            r = lax.rsqrt(jnp.mean(c * c, axis=1, keepdims=True) + RMS_EPS)
            t = dn * g
            res += [r * t - c * (r * r * r) * jnp.mean(c * t, axis=1, keepdims=True), _colsum(dn * c * r)]
        return *res, dk_rope, _dot(n_q, dq_a, "tn"), _dot(n_kv, dkv_a, "tn")

    dc_q, grads["q_a_gain"], dc_kv, grads["kv_a_gain"], dk_rope, g_w_q_b, g_w_kv_b = _rowwise(
        mla_inputs_bwd, [dq_mla, dk_mla, dv_a, n_q, n_kv, (proj, 256, COL_CQ), (proj, 128, COL_CKV), w["q_a_gain"], w["kv_a_gain"],
                         ("whole", w["w_q_b"]), ("whole", w["w_kv_b"]), rope["cos_q"], rope["sin_k64_t"], rope["cos_k64"], rope["sin_k64_t"]],
        [(256, BF16), ("sum", 256), (128, BF16), ("sum", 128), (128, BF16), ("sum", (MLA_Q_LORA, 1024)), ("sum", (MLA_KV_LORA, 1536))],
        name="mla_inputs_bwd", rows=s)
    grads["w_q_b"], grads["w_kv_b"] = g_w_q_b.astype(BF16), g_w_kv_b.astype(BF16)

    sent = emit({n: grads[n] for n in ("w_mem_kv", "w_q_b", "w_kv_b")})

    dproj = jnp.concatenate(
        [dc_q, dc_kv, dk_rope, d_gate["mla"], d_gate["sb"], d_gate["mem"], dq_sb.astype(BF16), dk_sb.astype(BF16),
         dv_sb.astype(BF16), dq_m.astype(BF16)], axis=1)
    grads["w_in"] = _mm(xb, dproj, "tn", name="g_w_in", out_dtype=BF16, behind=sent)
    sent = emit({"w_in": grads["w_in"]})
    grad_x = _mm(dproj, w["w_in"], "nt", name="grad_x", add=dx, behind=sent)
    return loss, grad_x, grads


def _shard_shape(shape, axis):
    return tuple(d // N_DEV if a == axis else d for a, d in enumerate(shape))


def _from_blocks(blocks, name):
    shape, axis = SHARDED[name]
    return blocks.reshape(shape) if axis == 0 else blocks.transpose(1, 0, 2).reshape(shape)


def _to_blocks(full, name):
    shape, axis = SHARDED[name]
    shp = _shard_shape(shape, axis)
    return full.reshape(N_DEV, *shp) if axis == 0 else full.reshape(shape[0], N_DEV, shp[1]).transpose(1, 0, 2)


def _pad_heads(a, used):
    rows = a.shape[0]
    a = a.reshape(rows, MLA_HEADS, used)
    return jnp.concatenate([a, jnp.zeros((rows, MLA_HEADS, LANES - used), a.dtype)], axis=2).reshape(rows, MLA_HEADS * LANES)


def _to_kernel_layout(name, full):
    if name == "w_in":
        return jnp.concatenate([jnp.zeros((D_MODEL, IN_PAD), full.dtype) if piece is None else full[:, piece[0]:piece[0] + piece[1]]
                                for piece in IN_PIECES], axis=1)
    if name == "w_q_b":
        return _pad_heads(full, MLA_NOPE + MLA_ROPE)
    if name == "w_kv_b":
        kv = full.reshape(MLA_KV_LORA, MLA_HEADS, MLA_NOPE + MLA_V)
        return jnp.concatenate([_pad_heads(kv[:, :, :MLA_NOPE].reshape(MLA_KV_LORA, -1), MLA_NOPE),
                                kv[:, :, MLA_NOPE:].reshape(MLA_KV_LORA, -1)], axis=1)
    return full


def _from_kernel_layout(name, g):
    if name == "w_in":
        placed, at = [], 0
        for piece in IN_PIECES:
            if piece is not None:
                placed.append((piece[0], g[:, at:at + piece[1]]))
            at += IN_PAD if piece is None else piece[1]
        return jnp.concatenate([cols for _, cols in sorted(placed, key=lambda item: item[0])], axis=1)
    if name == "w_q_b":
        return g.reshape(MLA_Q_LORA, MLA_HEADS, LANES)[:, :, :MLA_NOPE + MLA_ROPE].reshape(MLA_Q_LORA, -1)
    if name == "w_kv_b":
        return jnp.concatenate([g[:, :1024].reshape(MLA_KV_LORA, MLA_HEADS, LANES)[:, :, :MLA_NOPE],
                                g[:, 1024:].reshape(MLA_KV_LORA, MLA_HEADS, MLA_V)], axis=2).reshape(MLA_KV_LORA, -1)
    return g


def _pack_small(vectors, loss=None):
    flat = [v.reshape(-1) for v in vectors]
    flat.append(jnp.zeros((SMALL_ROWS * SMALL_LANES - LOSS_INDEX,), F32) if loss is None else
                jnp.concatenate([loss.reshape(-1)[:1], jnp.zeros((SMALL_ROWS * SMALL_LANES - LOSS_INDEX - 1,), F32)]))
    return jnp.concatenate(flat).reshape(SMALL_ROWS, SMALL_LANES)


def _unpack_small(packed):
    flat, res, off = packed.reshape(-1), [], 0
    for _, n in SMALL:
        res.append(flat[off:off + n].reshape(1, n))
        off += n
    return res


def _me_and_peers():
    x, y, c = lax.axis_index("x"), lax.axis_index("y"), lax.axis_index("c")
    peers = []
    for kk in range(1, N_DEV):
        px, py, pc = (x + (kk >> 2)) % 2, (y + ((kk >> 1) & 1)) % 2, (c + (kk & 1)) % 2
        peers.append(((px, py, pc), 4 * px + 2 * py + pc))
    return 4 * x + 2 * y + c, peers


def _share_small(small, *, name):
    def body(small_ref, all_ref, send_sems, recv_sems, local_sem):
        me, peers = _me_and_peers()
        copies = [pltpu.make_async_remote_copy(src_ref=small_ref, dst_ref=all_ref.at[me], send_sem=send_sems.at[kk], recv_sem=recv_sems.at[kk],
                                               device_id=pos, device_id_type=pl.DeviceIdType.MESH) for kk, (pos, _) in enumerate(peers)]
        copies.append(pltpu.make_async_copy(small_ref, all_ref.at[me], local_sem))
        for cp in copies:
            cp.start()
        for cp in copies:
            cp.wait()

    hbm = pl.BlockSpec(memory_space=pl.ANY)
    return pl.pallas_call(
        body, name=name, in_specs=[hbm], out_specs=hbm, out_shape=jax.ShapeDtypeStruct((N_DEV, *small.shape), small.dtype),
        scratch_shapes=[pltpu.SemaphoreType.DMA((N_DEV - 1,)), pltpu.SemaphoreType.DMA((N_DEV - 1,)), pltpu.SemaphoreType.DMA],
        compiler_params=pltpu.CompilerParams(has_side_effects=True),
    )(small)


_HBM = pl.BlockSpec(memory_space=pltpu.HBM)
_SEM = pl.BlockSpec(memory_space=pltpu.SEMAPHORE)


def _exchange_copies(srcs, zones, send_sems, recv_sems, gather):
    me, peers = _me_and_peers()
    return [pltpu.make_async_remote_copy(
        src_ref=srcs[t] if gather else srcs[t].at[peer], dst_ref=zones[t].at[me], send_sem=send_sems.at[7 * t + kk],
        recv_sem=recv_sems.at[7 * t + kk], device_id=pos, device_id_type=pl.DeviceIdType.MESH)
        for t in range(len(srcs)) for kk, (pos, peer) in enumerate(peers)]


def _exchange_start(tensors, *, gather, name):
    n = len(tensors)
    zones = [lax.empty((N_DEV, *(t.shape if gather else t.shape[1:])), t.dtype) for t in tensors]

    def body(*refs):
        for cp in _exchange_copies(refs[:n], refs[n:2 * n], refs[2 * n], refs[2 * n + 1], gather):
            cp.start()
        refs[-1][...] = jnp.zeros_like(refs[-1])

    buffers = [pltpu.HBM(a.shape, a.dtype) for a in tensors + zones]
    res = pl.pallas_call(
        body, name=name, in_specs=[_HBM] * (2 * n),
        out_shape=(pltpu.SemaphoreType.DMA((7 * n,)), pltpu.SemaphoreType.DMA((7 * n,)), *buffers, jax.ShapeDtypeStruct((8, LANES), F32)),
        out_specs=(_SEM, _SEM, *[_HBM] * (2 * n), pl.BlockSpec(memory_space=pltpu.VMEM)),
        input_output_aliases={i: 2 + i for i in range(2 * n)},
        compiler_params=pltpu.CompilerParams(has_side_effects=pltpu.SideEffectType.DATAFLOW_SIDE_EFFECTING),
    )(*[pltpu.with_memory_space_constraint(a, pltpu.HBM) for a in tensors + zones])
    return dict(sems=res[:2], buffers=res[2:2 + 2 * n], gather=gather, started=res[-1])


def _exchange_wait(started, after, *, name):
    n = len(started["buffers"]) // 2

    def body(*refs):
        for cp in _exchange_copies(refs[:n], refs[n:2 * n], refs[2 * n], refs[2 * n + 1], started["gather"]):
            cp.wait_send()
            cp.wait_recv()

    res = pl.pallas_call(
        body, name=name, in_specs=[_HBM] * (2 * n) + [_SEM, _SEM] + [_ANY] * len(after),
        out_shape=tuple(pltpu.HBM(a.shape, a.dtype) for a in started["buffers"]), out_specs=tuple([_HBM] * (2 * n)),
        input_output_aliases={i: i for i in range(2 * n)},
        compiler_params=pltpu.CompilerParams(has_side_effects=pltpu.SideEffectType.DATAFLOW_SIDE_EFFECTING),
    )(*started["buffers"], *started["sems"], *after)
    return res[:n], res[n:]


def _adamw(contrib, w, m, v, *, name):
    rows, cols = w.shape
    tile = min(rows, ADAM_ROWS)

    def body(c_ref, w_ref, m_ref, v_ref, g_ref, d_ref, nm_ref, nv_ref):
        g = c_ref[0].astype(F32)
        for s in range(1, N_DEV):
            g = g + c_ref[s].astype(F32)
        m_new = ADAM_B1 * m_ref[...] + (1.0 - ADAM_B1) * g
        v_new = ADAM_B2 * v_ref[...] + (1.0 - ADAM_B2) * (g * g)
        m_hat = m_new / (1.0 - ADAM_B1 ** ADAM_STEP)
        v_hat = v_new / (1.0 - ADAM_B2 ** ADAM_STEP)
        g_ref[...] = g
        d_ref[...] = -ADAM_LR * (m_hat / (jnp.sqrt(v_hat) + ADAM_EPS) + ADAM_WD * w_ref[...])
        nm_ref[...] = m_new
        nv_ref[...] = v_new

    spec = pl.BlockSpec((tile, cols), lambda i: (i, 0))
    return pl.pallas_call(
        body, name=name, grid=(rows // tile,),
        in_specs=[pl.BlockSpec((N_DEV, tile, cols), lambda i: (0, i, 0)), spec, spec, spec], out_specs=[spec] * 4,
        out_shape=[jax.ShapeDtypeStruct((rows, cols), F32)] * 4, compiler_params=_cparams("parallel"),
    )(contrib, w, m, v)


class _Weights:
    def __init__(self, gathers, vectors, me):
        self.gathers, self.ready, self.me, self.after = gathers, dict(vectors), me, ()

    def arrive_after(self, *values):
        self.after = values

    def __getitem__(self, name):
        if name not in self.ready:
            gi = next(i for i, group in enumerate(GATHER_GROUPS) if name in group)
            after = [*self.after, *[g["started"] for g in self.gathers]]
            shards, zones = _exchange_wait(self.gathers[gi], after, name=f"gather_wait_{gi}")
            for n, shard, zone in zip(GATHER_GROUPS[gi], shards, zones, strict=True):
                blocks = lax.dynamic_update_slice_in_dim(zone, shard[None], self.me, 0)
                self.ready[n] = _to_kernel_layout(n, _from_blocks(blocks, n))
        return self.ready[name]


def kernel(x, mem, w_in, w_mem_kv, q_a_gain, w_q_b, kv_a_gain, w_kv_b, w_branch_mla, w_branch_sb, w_branch_mem, w_merge_gate, b_merge_gate, w_out, ln_gain, ln_bias, loss_target, m_w_in, m_w_mem_kv, m_q_a_gain, m_w_q_b, m_kv_a_gain, m_w_kv_b, m_w_branch_mla, m_w_branch_sb, m_w_branch_mem, m_w_merge_gate, m_b_merge_gate, m_w_out, m_ln_gain, m_ln_bias, v_w_in, v_w_mem_kv, v_q_a_gain, v_w_q_b, v_kv_a_gain, v_w_kv_b, v_w_branch_mla, v_w_branch_sb, v_w_branch_mem, v_w_merge_gate, v_b_merge_gate, v_w_out, v_ln_gain, v_ln_bias):
    given = dict(locals())
    small_names = [n for n, _ in SMALL]
    smalls = lambda prefix: [given[prefix + n] for n in small_names]
    me = 4 * lax.axis_index("x") + 2 * lax.axis_index("y") + lax.axis_index("c")

    gathers = [_exchange_start([given[n][0].astype(BF16) for n in group], gather=True, name=f"gather_start_{gi}")
               for gi, group in enumerate(GATHER_GROUPS)]
    w = _Weights(gathers, {n: given[n] for n in small_names}, me)
    exchanges = []
    results = [{}, {}, {}, {}]

    def finish(gi, after):
        names, started = exchanges[gi]
        sent, zones = _exchange_wait(started, after, name=f"grads_wait_{gi}")
        done = []
        for n, blocks, zone in zip(names, sent, zones, strict=True):
            own = lax.dynamic_index_in_dim(blocks, me, 0, keepdims=True)
            contrib = lax.dynamic_update_slice_in_dim(zone, own, me, 0)
            outs = _adamw(contrib, given[n][0], given["m_" + n][0], given["v_" + n][0], name=f"adamw_{n}")
            for kind, res in zip(results, outs, strict=True):
                kind[n] = res[None]
            done.append(outs[1])
        return done

    def emit(grads):
        blocks = [_to_blocks(_from_kernel_layout(n, g), n).astype(BF16) for n, g in grads.items()]
        exchanges.append((tuple(grads), _exchange_start(blocks, gather=False, name=f"grads_start_{len(exchanges)}")))
        started = [exchanges[-1][1]["started"]]
        if len(exchanges) == len(GRAD_GROUPS):
            for gi in range(len(GRAD_GROUPS) - 1):
                started += finish(gi, started[:1])
        return started

    loss, grad_x, grads = _local_step(x[0], mem[0], loss_target[0], w, emit)

    contrib_small = _share_small(_pack_small([grads[n] for n in small_names], loss), name="share_small")
    sml = _adamw(contrib_small, _pack_small(smalls("")), _pack_small(smalls("m_")), _pack_small(smalls("v_")), name="adamw_small")
    for kind, packed in zip(results, sml, strict=True):
        kind.update(zip(small_names, _unpack_small(packed), strict=True))
    finish(len(GRAD_GROUPS) - 1, [grad_x])
    order = ["w_in", "w_mem_kv", "q_a_gain", "w_q_b", "kv_a_gain", "w_kv_b", "w_branch_mla", "w_branch_sb", "w_branch_mem",
             "w_merge_gate", "b_merge_gate", "w_out", "ln_gain", "ln_bias"]
    loss_out = sml[0].reshape(-1)[LOSS_INDEX]
    return (loss_out, grad_x[None], *[kind[n] for kind in results for n in order])
```

```python
import math

import jax
import jax.numpy as jnp
from jax import lax
from jax.experimental import pallas as pl
from jax.experimental.pallas import tpu as pltpu

F32, BF16 = jnp.float32, jnp.bfloat16

N_DEV = 8
D_MODEL = 1024
MLA_HEADS, MLA_NOPE, MLA_ROPE, MLA_V = 8, 64, 32, 64
MLA_Q_LORA, MLA_KV_LORA = 256, 128
SB_HEAD_DIM = 64
MEM_HEAD_DIM = 128
ROPE_BASE = 10000.0
RMS_EPS = 1e-6
LN_EPS = 1e-5
DEEPNORM_ALPHA = 2.0 ** 0.25
ADAM_LR, ADAM_B1, ADAM_B2, ADAM_EPS, ADAM_WD, ADAM_STEP = 0.001, 0.9, 0.999, 1e-08, 0.01, 10
LOG2E, LN2 = math.log2(math.e), math.log(2.0)

LANES = 128
GROUPS = 4
PROJ_WIDTH = 4096
COL_CQ, COL_CKV, COL_KROPE, COL_GATE_A, COL_GATE_B, COL_GATE_M = 0, 256, 384, 512, 1024, 1536
QKV_FIRST, QKV_WIDTH = 2048, 2048
COL_QB, COL_KB, COL_VB, COL_QM = 0, 512, 1024, 1536
IN_PIECES = ((0, 416), None, (416, 512), (2464, 512), (3488, 512), (928, 512), (1440, 512), (1952, 512), (2976, 512))
IN_PAD = 96

VMEM_LIMIT_BYTES = 56 * 1024 * 1024
NEG_BIG = -1e30
Q_BLOCK = 512
MEM_Q_BLOCK = 2048
SB_BWD_Q_BLOCK = 512
TRI_BLOCK = 256
TILE_ROWS = 64
KEY_CHUNK = 512

SHARDED = {
    "w_in": ((1024, 4000), 1), "w_mem_kv": ((1024, 1024), 0), "w_q_b": ((256, 768), 1), "w_kv_b": ((128, 1024), 1),
    "w_branch_mla": ((512, 1024), 1), "w_branch_sb": ((512, 1024), 1), "w_branch_mem": ((512, 1024), 1),
    "w_merge_gate": ((1024, 3072), 1), "w_out": ((1024, 1024), 0),
}
GATHER_GROUPS = (("w_in",), ("w_merge_gate",), ("w_q_b", "w_kv_b", "w_mem_kv", "w_branch_mla", "w_branch_sb", "w_branch_mem", "w_out"))
GRAD_GROUPS = (("w_out", "w_merge_gate", "w_branch_mla", "w_branch_sb", "w_branch_mem"), ("w_mem_kv", "w_q_b", "w_kv_b"), ("w_in",))
SMALL = (("q_a_gain", 256), ("kv_a_gain", 128), ("b_merge_gate", 3072), ("ln_gain", 1024), ("ln_bias", 1024))
SMALL_ROWS, SMALL_LANES = 48, 128
ADAM_ROWS = 256
LOSS_INDEX = 5504


def _cparams(*sem):
    return pltpu.CompilerParams(dimension_semantics=sem or None, vmem_limit_bytes=VMEM_LIMIT_BYTES)


_DIMS = {"nn": (((1,), (0,)), ((), ())), "nt": (((1,), (1,)), ((), ())), "tn": (((0,), (0,)), ((), ()))}


def _dot(a, b, dims):
    return lax.dot_general(a, b, _DIMS[dims], preferred_element_type=F32)


def _tile(dim, want):
    if dim <= want:
        return dim
    t = want - want % LANES
    while dim % t:
        t -= LANES
    assert t > 0, (dim, want)
    return t


_ANY = pl.BlockSpec(memory_space=pl.ANY)


def _mm(a, b, dims, *, name, out_dtype=F32, add=None, add_scale=1.0, col_scale=None, b_cols=None, behind=None,
        tm=1024, tn=1024, tk=1024):
    batch = a.shape[0] if a.ndim == 3 else None
    if dims == "nn":
        (m, k), (k2, n) = a.shape[-2:], b.shape[-2:]
    elif dims == "nt":
        (m, k), (n, k2) = a.shape[-2:], b.shape[-2:]
    else:
        (k, m), (k2, n) = a.shape[-2:], b.shape[-2:]
    assert k == k2 and a.ndim == b.ndim, (a.shape, b.shape, dims)
    assert batch is None or (b.shape[0] == batch and add is None and col_scale is None and b_cols is None)
    b_first = 0
    if b_cols is not None:
        assert dims == "nn"
        b_first, n = b_cols
    tm, tn, tk = _tile(m, tm), _tile(n, tn), _tile(k, tk)
    assert b_first % tn == 0
    jb = b_first // tn
    nk = k // tk

    def spec(block, index):
        if batch is None:
            return pl.BlockSpec(block, lambda bb, i, j, kk: index(i, j, kk))
        return pl.BlockSpec((None, *block), lambda bb, i, j, kk: (bb, *index(i, j, kk)))

    a_spec = spec((tk, tm), lambda i, j, kk: (kk, i)) if dims == "tn" else spec((tm, tk), lambda i, j, kk: (i, kk))
    b_spec = spec((tn, tk), lambda i, j, kk: (j, kk)) if dims == "nt" else spec((tk, tn), lambda i, j, kk: (kk, jb + j))
    o_spec = spec((tm, tn), lambda i, j, kk: (i, j))
    behind = [] if behind is None else behind if isinstance(behind, (list, tuple)) else [behind]
    optional = [(add, o_spec), (col_scale, pl.BlockSpec((1, tn), lambda bb, i, j, kk: (0, j))), *[(v, _ANY) for v in behind]]
    present = [(v, spec) for v, spec in optional if v is not None]

    def body(*refs):
        a_ref, b_ref = refs[:2]
        extra = iter(refs[2:2 + len(present)])
        add_ref = next(extra) if add is not None else None
        scale_ref = next(extra) if col_scale is not None else None
        o_ref = refs[2 + len(present)]
        part = _dot(a_ref[...].astype(BF16), b_ref[...].astype(BF16), dims)

        def finish(r):
            if add is not None:
                r = r + add_scale * add_ref[...]
            if col_scale is not None:
                r = r * scale_ref[...]
            o_ref[...] = r.astype(out_dtype)

        if nk == 1:
            finish(part)
            return
        acc = refs[-1]
        kk = pl.program_id(3)

        @pl.when(kk == 0)
        def _():
            acc[...] = part

        @pl.when(kk > 0)
        def _():
            acc[...] += part

        @pl.when(kk == nk - 1)
        def _():
            finish(acc[...])

    return pl.pallas_call(
        body, name=name, grid=(batch or 1, m // tm, n // tn, nk),
        in_specs=[a_spec, b_spec] + [spec for _, spec in present], out_specs=o_spec,
        out_shape=jax.ShapeDtypeStruct((m, n) if batch is None else (batch, m, n), out_dtype),
        scratch_shapes=[pltpu.VMEM((tm, tn), F32)] if nk > 1 else [],
        compiler_params=_cparams("parallel", "parallel", "parallel", "arbitrary"),
    )(a, b, *[v for v, _ in present])


def _rowwise(fn, ins, outs, *, name, rows, tr=512):
    n_in = len(ins)
    tr = min(tr, rows)
    in_specs, args = [], []
    for it in ins:
        if isinstance(it, tuple) and it[0] == "whole":
            in_specs.append(pl.BlockSpec(it[1].shape, lambda i, nd=it[1].ndim: (0,) * nd))
            args.append(it[1])
            continue
        arr, w, off = it if isinstance(it, tuple) else (it, it.shape[-1], 0)
        assert off % w == 0
        cb = off // w
        if arr.ndim == 3:
            in_specs.append(pl.BlockSpec((arr.shape[0], tr, w), lambda i, cb=cb: (0, i, cb)))
        elif arr.shape[0] == 1:
            in_specs.append(pl.BlockSpec((1, w), lambda i, cb=cb: (0, cb)))
        else:
            in_specs.append(pl.BlockSpec((tr, w), lambda i, cb=cb: (i, cb)))
        args.append(arr)
    out_shape, out_specs, is_sum = [], [], []
    for out in outs:
        is_sum.append(out[0] == "sum")
        if out[0] == "sum":
            shape = out[1] if isinstance(out[1], tuple) else (1, out[1])
            out_shape.append(jax.ShapeDtypeStruct(shape, F32))
            out_specs.append(pl.BlockSpec(shape, lambda i: (0, 0)))
        elif len(out) == 3:
            out_shape.append(jax.ShapeDtypeStruct((out[0], rows, out[1]), out[2]))
            out_specs.append(pl.BlockSpec((out[0], tr, out[1]), lambda i: (0, i, 0)))
        else:
            out_shape.append(jax.ShapeDtypeStruct((rows, out[0]), out[1]))
            out_specs.append(pl.BlockSpec((tr, out[0]), lambda i: (i, 0)))

    def body(*refs):
        res = fn(*[r[...] for r in refs[:n_in]])
        for r, val, s in zip(refs[n_in:], res, is_sum, strict=True):
            if s:
                @pl.when(pl.program_id(0) == 0)
                def _(r=r):
                    r[...] = jnp.zeros_like(r)

                r[...] += val
            elif isinstance(val, (list, tuple)):
                for n, part in enumerate(val):
                    r[n] = part.astype(r.dtype)
            else:
                r[...] = val.astype(r.dtype)

    return pl.pallas_call(
        body, name=name, grid=(rows // tr,), in_specs=in_specs, out_specs=out_specs, out_shape=out_shape,
        compiler_params=_cparams("arbitrary"),
    )(*args)


def _colsum(v):
    return jnp.sum(v, axis=0, keepdims=True)


def _sigmoid(v):
    return 1.0 / (1.0 + jnp.exp(-v))


def _lane_groups(v):
    return [v[:, g * LANES:(g + 1) * LANES] for g in range(v.shape[1] // LANES)]


def _swap_halves(v, first_lane):
    lane = lax.broadcasted_iota(jnp.int32, v.shape, 1)
    return jnp.where(lane < first_lane + 16, pltpu.roll(v, 112, axis=1), pltpu.roll(v, 16, axis=1))


def _lane_sum(acc, v):
    for part in _lane_groups(v):
        acc = acc + part
    return acc


def _low_half(shape):
    return lax.broadcasted_iota(jnp.int32, shape, 1) < LANES // 2


def _select_heads(per_head, pick):
    if len(per_head) == 1:
        return pick(per_head[0], 0)
    return jnp.where(_low_half(per_head[0].shape), pick(per_head[0], 0), pick(per_head[1], 1))


def _attn_specs(s, sk, hp, bq, q0, k0, v0):
    wq = hp * LANES
    assert q0 % wq == 0 and k0 % wq == 0 and v0 % LANES == 0
    qb0, kb0, vb0 = q0 // wq, k0 // wq, v0 // LANES
    q_spec = pl.BlockSpec((bq, wq), lambda g, i: (i, qb0 + g))
    k_spec = pl.BlockSpec((sk, wq), lambda g, i: (0, kb0 + g))
    v_spec = pl.BlockSpec((sk, LANES), lambda g, i: (0, vb0 + g))
    row_out = lambda w: pl.BlockSpec((bq, w), lambda g, i: (i, g))
    key_out = lambda w: pl.BlockSpec((sk, w), lambda g, i: (0, g))
    return q_spec, k_spec, v_spec, row_out, key_out


def _chunks(i, bq, ch, sk, causal):
    return ((i + 1) * bq - 1) // ch if causal else jnp.int32(sk // ch - 1)


def _positions(i, c, bq, ch):
    return (i * bq + lax.broadcasted_iota(jnp.int32, (bq, ch), 0), c * ch + lax.broadcasted_iota(jnp.int32, (bq, ch), 1))


def _softmax_fwd(q, k, v, *, hp, causal, name, q0=0, k0=0, v0=0, q_rows=Q_BLOCK):
    s, sk = q.shape[0], k.shape[0]
    bq, ch = min(q_rows, s), min(KEY_CHUNK, sk)
    assert not causal or bq <= ch
    q_spec, k_spec, v_spec, row_out, _ = _attn_specs(s, sk, hp, bq, q0, k0, v0)

    def body(q_ref, k_ref, v_ref, o_ref, lse_ref, s_scr):
        i = pl.program_id(1)
        qs = _lane_groups(q_ref[...])
        last = _chunks(i, bq, ch, sk, causal)

        def scores(c, ms, masked):
            off = pl.multiple_of(c * ch, ch)
            out = []
            for j in range(hp):
                sc = _dot(qs[j], k_ref[pl.ds(off, ch), j * LANES:(j + 1) * LANES], "nt")
                if masked:
                    qpos, kpos = _positions(i, c, bq, ch)
                    sc = jnp.where(kpos <= qpos, sc, NEG_BIG)
                s_scr[j, c] = sc
                m = ms[j]
                for part in _lane_groups(sc):
                    m = jnp.maximum(m, part)
                out.append(m)
            return tuple(out)

        ms = lax.fori_loop(0, last, lambda c, m: scores(c, m, False), tuple(jnp.full((bq, LANES), NEG_BIG, F32) for _ in range(hp)))
        ms = scores(last, ms, causal)
        row_max = [jnp.max(m, axis=1, keepdims=True) for m in ms]

        def weigh(c, carry):
            off = pl.multiple_of(c * ch, ch)
            vt = v_ref[pl.ds(off, ch), :]
            out = []
            for j in range(hp):
                l, acc = carry[j]
                p = jnp.exp2(s_scr[j, c] - row_max[j])
                out.append((_lane_sum(l, p), acc + _dot(p.astype(BF16), vt, "nn")))
            return tuple(out)

        zero = jnp.zeros((bq, LANES), F32)
        res = lax.fori_loop(0, last + 1, weigh, tuple((zero, zero) for _ in range(hp)))
        row_sum = [jnp.sum(l, axis=1, keepdims=True) for l, _ in res]
        o_ref[...] = _select_heads([acc for _, acc in res], lambda acc, j: acc / row_sum[j])
        lse_ref[...] = _select_heads([jnp.broadcast_to(row_max[j] + jnp.log2(row_sum[j]), (bq, LANES)) for j in range(hp)], lambda a, j: a)

    return pl.pallas_call(
        body, name=name, grid=(GROUPS, s // bq), in_specs=[q_spec, k_spec, v_spec], out_specs=[row_out(LANES), row_out(LANES)],
        out_shape=[jax.ShapeDtypeStruct((s, GROUPS * LANES), F32)] * 2,
        scratch_shapes=[pltpu.VMEM((hp, sk // ch, bq, ch), F32)], compiler_params=_cparams("parallel", "arbitrary"),
    )(q, k, v)


def _head_cotangent(do, j, hp):
    if hp == 1:
        return do
    return jnp.where(_low_half(do.shape) == (j == 0), do, 0.0)


def _softmax_bwd(q, k, v, o, do, lse, behind, *, hp, causal, dq_scale, name, q0=0, k0=0, v0=0, q_rows=Q_BLOCK):
    s, sk = q.shape[0], k.shape[0]
    bq, ch = min(q_rows, s), min(KEY_CHUNK, sk)
    assert not causal or bq <= ch
    wq = hp * LANES
    q_spec, k_spec, v_spec, row_out, key_out = _attn_specs(s, sk, hp, bq, q0, k0, v0)

    def body(q_ref, k_ref, v_ref, o_ref, do_ref, lse_ref, _, dq_ref, dk_ref, dv_ref, dk_t, dv_t):
        i = pl.program_id(1)

        @pl.when(i == 0)
        def _():
            dk_t[...] = jnp.zeros_like(dk_t)
            dv_t[...] = jnp.zeros_like(dv_t)

        qs = _lane_groups(q_ref[...])
        do_all, o_all, lse_all = do_ref[...], o_ref[...], lse_ref[...]
        dos, deltas, lses = [], [], []
        for j in range(hp):
            d = _head_cotangent(do_all, j, hp)
            deltas.append(jnp.sum(d * o_all, axis=1, keepdims=True))
            dos.append(d.astype(BF16))
            lses.append(lse_all[:, j * (LANES // hp):j * (LANES // hp) + 1])
        last = _chunks(i, bq, ch, sk, causal)

        def chunk(c, dqs, masked):
            off = pl.multiple_of(c * ch, ch)
            vt = v_ref[pl.ds(off, ch), :]
            out, dks, dv = [], [], None
            for j in range(hp):
                kt = k_ref[pl.ds(off, ch), j * LANES:(j + 1) * LANES]
                p = jnp.exp2(_dot(qs[j], kt, "nt") - lses[j])
                if masked:
                    qpos, kpos = _positions(i, c, bq, ch)
                    p = jnp.where(kpos <= qpos, p, 0.0)
                ds = (p * (_dot(dos[j], vt, "nt") - deltas[j]) * LN2).astype(BF16)
                out.append(dqs[j] + _dot(ds, kt, "nn"))
                dks.append(_dot(qs[j], ds, "tn"))
                dvj = _dot(dos[j], p.astype(BF16), "tn")
                dv = dvj if dv is None else dv + dvj
            dk_t[c] += dks[0] if hp == 1 else jnp.concatenate(dks, axis=0)
            dv_t[c] += dv
            return tuple(out)

        dqs = lax.fori_loop(0, last, lambda c, d: chunk(c, d, False), tuple(jnp.zeros((bq, LANES), F32) for _ in range(hp)))
        dqs = chunk(last, dqs, causal)
        dq_ref[...] = (dqs[0] if hp == 1 else jnp.concatenate(dqs, axis=1)) * dq_scale

        @pl.when(i == s // bq - 1)
        def _():
            for c in range(sk // ch):
                dk_ref[c * ch:(c + 1) * ch, :] = dk_t[c].T
                dv_ref[c * ch:(c + 1) * ch, :] = dv_t[c].T

    return pl.pallas_call(
        body, name=name, grid=(GROUPS, s // bq),
        in_specs=[q_spec, k_spec, v_spec, row_out(LANES), row_out(LANES), row_out(LANES), _ANY],
        out_specs=[row_out(wq), key_out(wq), key_out(LANES)],
        out_shape=[jax.ShapeDtypeStruct((s, GROUPS * wq), F32), jax.ShapeDtypeStruct((sk, GROUPS * wq), F32),
                   jax.ShapeDtypeStruct((sk, GROUPS * LANES), F32)],
        scratch_shapes=[pltpu.VMEM((sk // ch, wq, ch), F32), pltpu.VMEM((sk // ch, LANES, ch), F32)],
        compiler_params=_cparams("arbitrary", "arbitrary"),
    )(q, k, v, o, do, lse, behind)


def _log2_sigmoid_pair(z2):
    minus_abs = lax.bitcast_convert_type(lax.bitcast_convert_type(z2, jnp.uint32) | jnp.uint32(0x80000000), F32)
    log_beta = jnp.minimum(z2, 0.0) - jnp.log2(1.0 + jnp.exp2(minus_abs))
    return log_beta, log_beta - z2


def _tilewise(fn, *arrays):
    rows, cols = arrays[0].shape
    step = min(TILE_ROWS, rows)
    grid = [[fn(*[None if a is None else a[r:r + step, c:c + LANES] for a in arrays]) for c in range(0, cols, LANES)]
            for r in range(0, rows, step)]
    return [jnp.concatenate([jnp.concatenate([cell[k] for cell in row], axis=1) for row in grid], axis=0)
            for k in range(len(grid[0][0]))]


def _split(v):
    hi = v.astype(BF16)
    return hi, (v - hi.astype(F32)).astype(BF16)


def _tri(n, after):
    rows, cols = lax.broadcasted_iota(jnp.int32, (n, n), 0), lax.broadcasted_iota(jnp.int32, (n, n), 1)
    return (rows > cols if after else rows < cols).astype(BF16)


def _running_sums(v, terms, start, tri, backwards):
    n = tri.shape[0]
    n_blocks = v.shape[1] // n
    order = range(n_blocks - 1, -1, -1) if backwards else range(n_blocks)
    stacked = tri if len(terms) == 1 else jnp.concatenate([tri] * len(terms), axis=0)
    parts, run = [None] * n_blocks, start
    for t in order:
        cols = slice(t * n, (t + 1) * n)
        lhs = terms[0][:, cols] if len(terms) == 1 else jnp.concatenate([term[:, cols] for term in terms], axis=1)
        parts[t] = _dot(lhs, stacked, "nn") + run
        run = run + jnp.sum(v[:, cols], axis=1, keepdims=True)
    return (parts[0] if n_blocks == 1 else jnp.concatenate(parts, axis=1)), run


def _sb_weights(qm, kt, run, tri, strict):
    def logs(z2, keep):
        log_beta, log_keep = _log2_sigmoid_pair(z2)
        if keep is not None:
            log_keep = jnp.where(keep, log_keep, 0.0)
        return log_beta, log_keep, *_split(log_keep)

    log_beta, log_keep, hi, lo = _tilewise(logs, _dot(qm, kt, "nt"), strict)
    behind, run = _running_sums(log_keep, (hi, lo), run, tri, True)

    def weigh(log_beta, behind, keep):
        a = jnp.exp2(log_beta + behind)
        return (a if keep is None else jnp.where(keep, a, 0.0),)

    (a,) = _tilewise(weigh, log_beta, behind, strict)
    return a, log_beta, run


class _Copies:
    def __init__(self, copies):
        self.copies = copies

    def start(self):
        for cp in self.copies:
            cp.start()

    def wait(self):
        for cp in self.copies:
            cp.wait()


def _sb_queries(q_all):
    low = _low_half(q_all.shape)
    zero = jnp.zeros_like(q_all)
    return [jnp.where(low, q_all, zero), jnp.where(low, zero, q_all)]


def _sb_fwd(qkv, *, q0, k0, v0, name):
    s = qkv.shape[0]
    bq, ch = min(Q_BLOCK, s), min(KEY_CHUNK, s)
    assert bq == ch
    n_q = s // bq
    q_spec, k_spec, v_spec, row_out, _ = _attn_specs(s, s, 1, bq, q0, k0, v0)

    def body(q_ref, k_ref, v_ref, o_ref, saved_ref, stage, sems):
        g, i = pl.program_id(0), pl.program_id(1)
        qms = _sb_queries(q_ref[...])
        tri = _tri(min(TRI_BLOCK, ch), True)
        last = _chunks(i, bq, ch, s, True)
        first_tile = i * (i + 1) // 2

        def save(slot, c):
            return _Copies([pltpu.make_async_copy(stage.at[slot, p], saved_ref.at[g, first_tile + c, p], sems.at[slot, p])
                            for p in range(4)])

        def chunk(c, step, carry, masked):
            off = pl.multiple_of(c * ch, ch)
            kt, vt = k_ref[pl.ds(off, ch), :], v_ref[pl.ds(off, ch), :]
            strict = None
            if masked:
                qpos, kpos = _positions(i, c, bq, ch)
                strict = kpos < qpos
            slot = step % 3
            if not masked:
                @pl.when(step >= 3)
                def _():
                    save(slot, c).wait()
            out = []
            for j in range(2):
                run, acc = carry[j]
                a, log_beta, run = _sb_weights(qms[j], kt, run, tri, strict)
                a = a.astype(BF16)
                stage[slot, j] = a
                stage[slot, 2 + j] = jnp.exp2(log_beta).astype(BF16)
                out.append((run, acc + _dot(a, vt, "nn")))
            save(slot, c).start()
            return tuple(out)

        carry = chunk(last, 0, tuple((jnp.zeros((bq, 1), F32), jnp.zeros((bq, LANES), F32)) for _ in range(2)), True)
        res = lax.fori_loop(0, last, lambda n, c: chunk(last - 1 - n, n + 1, c, False), carry)
        for back in range(3):
            @pl.when(last >= back)
            def _(back=back):
                save((last - back) % 3, 0).wait()

        o_ref[...] = _select_heads([acc for _, acc in res], lambda acc, j: acc)

    return pl.pallas_call(
        body, name=name, grid=(GROUPS, n_q), in_specs=[q_spec, k_spec, v_spec], out_specs=[row_out(LANES), _ANY],
        out_shape=[jax.ShapeDtypeStruct((s, GROUPS * LANES), F32),
                   jax.ShapeDtypeStruct((GROUPS, n_q * (n_q + 1) // 2, 4, bq, ch), BF16)],
        scratch_shapes=[pltpu.VMEM((3, 4, bq, ch), BF16), pltpu.SemaphoreType.DMA((3, 4))],
        compiler_params=_cparams("parallel", "arbitrary"),
    )(qkv, qkv, qkv)


def _sb_bwd(qkv, do, saved, behind, *, q0, k0, v0, dq_scale, name):
    s = qkv.shape[0]
    bq, ch = min(SB_BWD_Q_BLOCK, s), min(KEY_CHUNK, s)
    assert bq == ch and saved.shape[2:] == (4, bq, ch)
    q_spec, k_spec, v_spec, row_out, key_out = _attn_specs(s, s, 1, bq, q0, k0, v0)

    def body(q_ref, k_ref, v_ref, do_ref, saved_ref, _, dq_ref, dk_ref, dv_ref, g_s, beta_s, stage, sems):
        g_index, i = pl.program_id(0), pl.program_id(1)

        @pl.when(i == 0)
        def _():
            dk_ref[...] = jnp.zeros_like(dk_ref)
            dv_ref[...] = jnp.zeros_like(dv_ref)

        qms = _sb_queries(q_ref[...])
        do_all = do_ref[...]
        dos = [_head_cotangent(do_all, j, 2).astype(BF16) for j in range(2)]
        dos_ln2 = [(_head_cotangent(do_all, j, 2) * LN2).astype(BF16) for j in range(2)]
        tri_before = _tri(min(TRI_BLOCK, ch), False)
        last = _chunks(i, bq, ch, s, True)
        first_tile = i * (i + 1) // 2

        def strict_mask(c):
            qpos, kpos = _positions(i, c, bq, ch)
            return kpos < qpos

        def fetch(slot, c):
            return _Copies([pltpu.make_async_copy(saved_ref.at[g_index, first_tile + c, p], stage.at[slot, p], sems.at[slot, p])
                            for p in range(4)])

        def sweep1(n, unused):
            c, slot = last - n, n % 3

            @pl.when(c >= 2)
            def _():
                fetch((n + 2) % 3, c - 2).start()

            fetch(slot, c).wait()
            off = pl.multiple_of(c * ch, ch)
            vt = v_ref[pl.ds(off, ch), :]
            dv = None
            for j in range(2):
                a = stage[slot, j]
                g_s[j, c] = (a.astype(F32) * _dot(dos_ln2[j], vt, "nt")).astype(BF16)
                beta_s[j, c] = stage[slot, 2 + j]
                dvj = _dot(a, dos[j], "tn")
                dv = dvj if dv is None else dv + dvj
            dv_ref[pl.ds(off, ch), :] += dv
            return unused

        fetch(0, last).start()

        @pl.when(last >= 1)
        def _():
            fetch(1, last - 1).start()

        lax.fori_loop(0, last + 1, sweep1, 0)

        def sweep2(c, carry, masked):
            off = pl.multiple_of(c * ch, ch)
            kt = k_ref[pl.ds(off, ch), :]
            out, dk = [], None
            for j in range(2):
                before, dq = carry[j]
                g16, beta = g_s[j, c], beta_s[j, c].astype(F32)
                g = g16.astype(F32)
                in_front, before = _running_sums(g, (g16,), before, tri_before, False)
                dz = g * (1.0 - beta) - beta * in_front
                if masked:
                    dz = jnp.where(strict_mask(c), dz, 0.0)
                dz = dz.astype(BF16)
                dkj = _dot(dz, qms[j], "tn")
                dk = dkj if dk is None else dk + dkj
                out.append((before, dq + _dot(dz, kt, "nn")))
            dk_ref[pl.ds(off, ch), :] += dk
            return tuple(out)

        carry = lax.fori_loop(0, last, lambda c, cr: sweep2(c, cr, False),
                              tuple((jnp.zeros((bq, 1), F32), jnp.zeros((bq, LANES), F32)) for _ in range(2)))
        res = sweep2(last, carry, True)
        dq_ref[...] = _select_heads([dq for _, dq in res], lambda dq, j: dq) * dq_scale

    n_ch = s // ch
    return pl.pallas_call(
        body, name=name, grid=(GROUPS, s // bq), in_specs=[q_spec, k_spec, v_spec, row_out(LANES), _ANY, _ANY],
        out_specs=[row_out(LANES), key_out(LANES), key_out(LANES)],
        out_shape=[jax.ShapeDtypeStruct((s, GROUPS * LANES), F32)] * 3,
        scratch_shapes=[pltpu.VMEM((2, n_ch, bq, ch), BF16)] * 2 + [pltpu.VMEM((3, 4, bq, ch), BF16), pltpu.SemaphoreType.DMA((3, 4))],
        compiler_params=_cparams("arbitrary", "arbitrary"),
    )(qkv, qkv, qkv, do, saved, behind)


def _rope_tables(s):
    half = MLA_ROPE // 2
    freqs = ROPE_BASE ** (-jnp.arange(half, dtype=F32) / half)
    ang = jnp.arange(s, dtype=F32)[:, None] * freqs[None, :]
    cos, sin = jnp.cos(ang), jnp.sin(ang)
    tail = jnp.zeros((s, LANES - MLA_NOPE - MLA_ROPE), F32)
    lead = lambda fill: jnp.full((s, MLA_NOPE), fill, F32)
    return dict(
        cos_k0=jnp.concatenate([cos, cos, lead(0.0), tail], axis=1), sin_k0=jnp.concatenate([-sin, sin, lead(0.0), tail], axis=1),
        cos_k64=jnp.concatenate([lead(0.0), cos, cos, tail], axis=1), sin_k64=jnp.concatenate([lead(0.0), -sin, sin, tail], axis=1),
        cos_q=jnp.concatenate([lead(1.0), cos, cos, tail], axis=1),
        sin_k64_t=jnp.concatenate([lead(0.0), sin, -sin, tail], axis=1),
    )


def _local_step(x, mem, target, w, emit=lambda grads: [jnp.zeros((8, LANES), F32)]):
    s = x.shape[0]
    rope = _rope_tables(s)
    xb = x.astype(BF16)
    inv_d = 1.0 / D_MODEL
    scale_a = LOG2E / math.sqrt(MLA_NOPE + MLA_ROPE)
    scale_b = LOG2E / math.sqrt(SB_HEAD_DIM)
    scale_m = LOG2E / math.sqrt(MEM_HEAD_DIM)
    arrive_after = getattr(w, "arrive_after", lambda *values: None)
    memb = mem.astype(BF16)

    arrive_after(xb, memb, *rope.values())
    proj = _mm(xb, w["w_in"], "nn", name="proj", b_cols=(0, QKV_FIRST))
    one = jnp.ones((1, 512), F32)
    qkv = _mm(xb, w["w_in"], "nn", name="proj_qkv", b_cols=(QKV_FIRST, QKV_WIDTH), out_dtype=BF16,
              col_scale=jnp.concatenate([one * scale_b, one, one, one * scale_m], axis=1))
    arrive_after(qkv)
    pre = _mm(xb, w["w_merge_gate"], "nn", name="merge_pre", out_dtype=BF16)
    arrive_after(pre)

    def mla_inputs(c_q, c_kv, k_rope, g_q, g_kv, w_q, w_kv, cos_q, sin_q, cos_k, sin_k):
        n_q = (c_q * lax.rsqrt(jnp.mean(c_q * c_q, axis=1, keepdims=True) + RMS_EPS) * g_q).astype(BF16)
        n_kv = (c_kv * lax.rsqrt(jnp.mean(c_kv * c_kv, axis=1, keepdims=True) + RMS_EPS) * g_kv).astype(BF16)
        q_a = _dot(n_q, w_q, "nn")
        kv_a = _dot(n_kv, w_kv, "nn").astype(BF16)
        q = jnp.concatenate([(g * cos_q + _swap_halves(g, MLA_NOPE) * sin_q) * scale_a for g in _lane_groups(q_a)], axis=1)
        k_pe = pltpu.roll(k_rope * cos_k + _swap_halves(k_rope, 0) * sin_k, MLA_NOPE, axis=1).astype(BF16)
        k = jnp.concatenate([g + k_pe for g in _lane_groups(kv_a[:, :1024])], axis=1)
        return n_q, n_kv, q, k, kv_a[:, 1024:]

    n_q, n_kv, q_mla, k_mla, v_a = _rowwise(
        mla_inputs, [(proj, 256, COL_CQ), (proj, 128, COL_CKV), (proj, 128, COL_KROPE), w["q_a_gain"], w["kv_a_gain"],
                     ("whole", w["w_q_b"]), ("whole", w["w_kv_b"]), rope["cos_q"], rope["sin_k64"], rope["cos_k0"], rope["sin_k0"]],
        [(256, BF16), (128, BF16), (1024, BF16), (1024, BF16), (512, BF16)], name="mla_inputs", rows=s)
    o_a, lse_a = _softmax_fwd(q_mla, k_mla, v_a, hp=2, causal=True, name="mla_fwd")

    o_b, sb_saved = _sb_fwd(qkv, q0=COL_QB, k0=COL_KB, v0=COL_VB, name="sb_fwd")

    mem_kv =_mm(memb, w["w_mem_kv"], "nn", name="mem_kv", out_dtype=BF16)
    o_m, lse_m = _softmax_fwd(qkv, mem_kv, mem_kv, hp=1, causal=False, name="mem_fwd", q0=1536, v0=512, q_rows=MEM_Q_BLOCK)

    branches = ("mla", "sb", "mem")
    w_branch = jnp.stack([w[f"w_branch_{br}"] for br in branches])
    bias = w["b_merge_gate"]

    def head(oa, ob, om, ga, gb, gm, pa, pb, pm, ba, bb, bm, xv, tv, gain, bias_ln, w_b, w_o, w_g):
        us, ys, gs = [], [], []
        for n, (o, gate, p, b) in enumerate(((oa, ga, pa, ba), (ob, gb, pb, bb), (om, gm, pm, bm))):
            us.append((o * gate * _sigmoid(gate)).astype(BF16))
            ys.append(_dot(us[n], w_b[n], "nn"))
            gs.append(_sigmoid(p.astype(F32) + b))
        merged = (gs[0] * ys[0] + gs[1] * ys[1] + gs[2] * ys[2]).astype(BF16)
        z = DEEPNORM_ALPHA * xv + _dot(merged, w_o, "nn")
        zc = z - jnp.mean(z, axis=1, keepdims=True)
        rstd = lax.rsqrt(jnp.mean(zc * zc, axis=1, keepdims=True) + LN_EPS)
        xhat = zc * rstd
        err = xhat * gain + bias_ln - tv
        loss = 0.5 * jnp.sum(jnp.mean(err * err, axis=1, keepdims=True), axis=0, keepdims=True)
        dy = err * inv_d
        dxhat = dy * gain
        dz = rstd * (dxhat - jnp.mean(dxhat, axis=1, keepdims=True) - xhat * jnp.mean(dxhat * xhat, axis=1, keepdims=True))
        dz16 = dz.astype(BF16)
        dm = _dot(dz16, w_o, "nt")
        dpre = jnp.concatenate([dm * ys[n] * gs[n] * (1.0 - gs[n]) for n in range(3)], axis=1)
        dx = DEEPNORM_ALPHA * dz + _dot(dpre.astype(BF16), w_g, "nt")
        dys = [(dm * gs[n]).astype(BF16) for n in range(3)]
        d_os, d_gates = [], []
        for n, (o, gate) in enumerate(((oa, ga), (ob, gb), (om, gm))):
            du, sg = _dot(dys[n], w_b[n], "nt"), _sigmoid(gate)
            d_os.append(du * gate * sg)
            d_gates.append(du * o * sg * (1.0 + gate * (1.0 - sg)))
        return (us, merged, dx, dz16, _colsum(dy * xhat), _colsum(dy), jnp.broadcast_to(loss, (1, LANES)), dpre, _colsum(dpre),
                dys, *d_os, *d_gates)

    grads = {}
    (u, merged, dx, dzb, grads["ln_gain"], grads["ln_bias"], loss, dpre, grads["b_merge_gate"], dy, *rest) = _rowwise(
        head, [o_a, o_b, o_m, (proj, 512, COL_GATE_A), (proj, 512, COL_GATE_B), (proj, 512, COL_GATE_M),
               (pre, 1024, 0), (pre, 1024, 1024), (pre, 1024, 2048), (bias, 1024, 0), (bias, 1024, 1024), (bias, 1024, 2048),
               x, target, w["ln_gain"], w["ln_bias"], ("whole", w_branch), ("whole", w["w_out"]), ("whole", w["w_merge_gate"])],
        [(3, 512, BF16), (1024, BF16), (1024, F32), (1024, BF16), ("sum", 1024), ("sum", 1024), ("sum", LANES),
         (3072, BF16), ("sum", 3072), (3, 1024, BF16)] + [(512, F32)] * 3 + [(512, BF16)] * 3, name="head", rows=s, tr=256)
    d_o, d_gate = dict(zip(branches, rest[:3], strict=True)), dict(zip(branches, rest[3:], strict=True))

    grads["w_out"] = _mm(merged, dzb, "tn", name="g_w_out", out_dtype=BF16)
    grads["w_merge_gate"] = _mm(xb, dpre, "tn", name="g_w_merge", out_dtype=BF16)
    g_w_branch = _mm(u, dy, "tn", name="g_w_branch", out_dtype=BF16)
    for n, br in enumerate(branches):
        grads[f"w_branch_{br}"] = g_w_branch[n]
    (sent,) = emit({n: grads[n] for n in ("w_out", "w_merge_gate", "w_branch_mla", "w_branch_sb", "w_branch_mem")})

    dq_m, dk_m, dv_m = _softmax_bwd(qkv, mem_kv, mem_kv, o_m, d_o["mem"], lse_m, sent, hp=1, causal=False, dq_scale=scale_m,
                                    name="mem_bwd", q0=1536, v0=512, q_rows=MEM_Q_BLOCK)
    grads["w_mem_kv"] = _mm(memb, jnp.concatenate([dk_m, dv_m], axis=1), "tn", name="g_w_mem_kv", out_dtype=BF16)

    dq_sb, dk_sb, dv_sb = _sb_bwd(qkv, d_o["sb"], sb_saved, sent, q0=COL_QB, k0=COL_KB, v0=COL_VB, dq_scale=scale_b, name="sb_bwd")

    dq_mla, dk_mla, dv_a = _softmax_bwd(q_mla, k_mla, v_a, o_a, d_o["mla"], lse_a, sent, hp=2, causal=True, dq_scale=scale_a,
                                        name="mla_bwd")

    def mla_inputs_bwd(dq, dk, dv, n_q, n_kv, c_q, c_kv, g_q, g_kv, w_q, w_kv, cos_q, sin_q, cos_k, sin_k):
        dq_a = jnp.concatenate([g * cos_q + _swap_halves(g, MLA_NOPE) * sin_q for g in _lane_groups(dq)], axis=1).astype(BF16)
        groups = _lane_groups(dk)
        g_rope = groups[0]
        for other in groups[1:]:
            g_rope = g_rope + other
        dk_rope = pltpu.roll(g_rope * cos_k + _swap_halves(g_rope, MLA_NOPE) * sin_k, MLA_NOPE, axis=1)
        nope = _low_half(g_rope.shape)
        dkv_a = jnp.concatenate([jnp.where(nope, grp, 0.0) for grp in groups] + [dv], axis=1).astype(BF16)
        res = []
        for c, dn, g in ((c_q, _dot(dq_a, w_q, "nt"), g_q), (c_kv, _dot(dkv_a, w_kv, "nt"), g_kv)):
            r = lax.rsqrt(jnp.mean(c * c, axis=1, keepdims=True) + RMS_EPS)
            t = dn * g
            res += [r * t - c * (r * r * r) * jnp.mean(c * t, axis=1, keepdims=True), _colsum(dn * c * r)]
        return *res, dk_rope, _dot(n_q, dq_a, "tn"), _dot(n_kv, dkv_a, "tn")

    dc_q, grads["q_a_gain"], dc_kv, grads["kv_a_gain"], dk_rope, g_w_q_b, g_w_kv_b = _rowwise(
        mla_inputs_bwd, [dq_mla, dk_mla, dv_a, n_q, n_kv, (proj, 256, COL_CQ), (proj, 128, COL_CKV), w["q_a_gain"], w["kv_a_gain"],
                         ("whole", w["w_q_b"]), ("whole", w["w_kv_b"]), rope["cos_q"], rope["sin_k64_t"], rope["cos_k64"], rope["sin_k64_t"]],
        [(256, BF16), ("sum", 256), (128, BF16), ("sum", 128), (128, BF16), ("sum", (MLA_Q_LORA, 1024)), ("sum", (MLA_KV_LORA, 1536))],
        name="mla_inputs_bwd", rows=s)
    grads["w_q_b"], grads["w_kv_b"] = g_w_q_b.astype(BF16), g_w_kv_b.astype(BF16)

    sent = emit({n: grads[n] for n in ("w_mem_kv", "w_q_b", "w_kv_b")})

    dproj = jnp.concatenate(
        [dc_q, dc_kv, dk_rope, d_gate["mla"], d_gate["sb"], d_gate["mem"], dq_sb.astype(BF16), dk_sb.astype(BF16),
         dv_sb.astype(BF16), dq_m.astype(BF16)], axis=1)
    grads["w_in"] = _mm(xb, dproj, "tn", name="g_w_in", out_dtype=BF16, behind=sent)
    sent = emit({"w_in": grads["w_in"]})
    grad_x = _mm(dproj, w["w_in"], "nt", name="grad_x", add=dx, behind=sent)
    return loss, grad_x, grads


def _shard_shape(shape, axis):
    return tuple(d // N_DEV if a == axis else d for a, d in enumerate(shape))


def _from_blocks(blocks, name):
    shape, axis = SHARDED[name]
    return blocks.reshape(shape) if axis == 0 else blocks.transpose(1, 0, 2).reshape(shape)


def _to_blocks(full, name):
    shape, axis = SHARDED[name]
    shp = _shard_shape(shape, axis)
    return full.reshape(N_DEV, *shp) if axis == 0 else full.reshape(shape[0], N_DEV, shp[1]).transpose(1, 0, 2)


def _pad_heads(a, used):
    rows = a.shape[0]
    a = a.reshape(rows, MLA_HEADS, used)
    return jnp.concatenate([a, jnp.zeros((rows, MLA_HEADS, LANES - used), a.dtype)], axis=2).reshape(rows, MLA_HEADS * LANES)


def _to_kernel_layout(name, full):
    if name == "w_in":
        return jnp.concatenate([jnp.zeros((D_MODEL, IN_PAD), full.dtype) if piece is None else full[:, piece[0]:piece[0] + piece[1]]
                                for piece in IN_PIECES], axis=1)
    if name == "w_q_b":
        return _pad_heads(full, MLA_NOPE + MLA_ROPE)
    if name == "w_kv_b":
        kv = full.reshape(MLA_KV_LORA, MLA_HEADS, MLA_NOPE + MLA_V)
        return jnp.concatenate([_pad_heads(kv[:, :, :MLA_NOPE].reshape(MLA_KV_LORA, -1), MLA_NOPE),
                                kv[:, :, MLA_NOPE:].reshape(MLA_KV_LORA, -1)], axis=1)
    return full


def _from_kernel_layout(name, g):
    if name == "w_in":
        placed, at = [], 0
        for piece in IN_PIECES:
            if piece is not None:
                placed.append((piece[0], g[:, at:at + piece[1]]))
            at += IN_PAD if piece is None else piece[1]
        return jnp.concatenate([cols for _, cols in sorted(placed, key=lambda item: item[0])], axis=1)
    if name == "w_q_b":
        return g.reshape(MLA_Q_LORA, MLA_HEADS, LANES)[:, :, :MLA_NOPE + MLA_ROPE].reshape(MLA_Q_LORA, -1)
    if name == "w_kv_b":
        return jnp.concatenate([g[:, :1024].reshape(MLA_KV_LORA, MLA_HEADS, LANES)[:, :, :MLA_NOPE],
                                g[:, 1024:].reshape(MLA_KV_LORA, MLA_HEADS, MLA_V)], axis=2).reshape(MLA_KV_LORA, -1)
    return g


def _pack_small(vectors, loss=None):
    flat = [v.reshape(-1) for v in vectors]
    flat.append(jnp.zeros((SMALL_ROWS * SMALL_LANES - LOSS_INDEX,), F32) if loss is None else
                jnp.concatenate([loss.reshape(-1)[:1], jnp.zeros((SMALL_ROWS * SMALL_LANES - LOSS_INDEX - 1,), F32)]))
    return jnp.concatenate(flat).reshape(SMALL_ROWS, SMALL_LANES)


def _unpack_small(packed):
    flat, res, off = packed.reshape(-1), [], 0
    for _, n in SMALL:
        res.append(flat[off:off + n].reshape(1, n))
        off += n
    return res


def _me_and_peers():
    x, y, c = lax.axis_index("x"), lax.axis_index("y"), lax.axis_index("c")
    peers = []
    for kk in range(1, N_DEV):
        px, py, pc = (x + (kk >> 2)) % 2, (y + ((kk >> 1) & 1)) % 2, (c + (kk & 1)) % 2
        peers.append(((px, py, pc), 4 * px + 2 * py + pc))
    return 4 * x + 2 * y + c, peers


def _share_small(small, *, name):
    def body(small_ref, all_ref, send_sems, recv_sems, local_sem):
        me, peers = _me_and_peers()
        copies = [pltpu.make_async_remote_copy(src_ref=small_ref, dst_ref=all_ref.at[me], send_sem=send_sems.at[kk], recv_sem=recv_sems.at[kk],
                                               device_id=pos, device_id_type=pl.DeviceIdType.MESH) for kk, (pos, _) in enumerate(peers)]
        copies.append(pltpu.make_async_copy(small_ref, all_ref.at[me], local_sem))
        for cp in copies:
            cp.start()
        for cp in copies:
            cp.wait()

    hbm = pl.BlockSpec(memory_space=pl.ANY)
    return pl.pallas_call(
        body, name=name, in_specs=[hbm], out_specs=hbm, out_shape=jax.ShapeDtypeStruct((N_DEV, *small.shape), small.dtype),
        scratch_shapes=[pltpu.SemaphoreType.DMA((N_DEV - 1,)), pltpu.SemaphoreType.DMA((N_DEV - 1,)), pltpu.SemaphoreType.DMA],
        compiler_params=pltpu.CompilerParams(has_side_effects=True),
    )(small)


_HBM = pl.BlockSpec(memory_space=pltpu.HBM)
_SEM = pl.BlockSpec(memory_space=pltpu.SEMAPHORE)


def _exchange_copies(srcs, zones, send_sems, recv_sems, gather):
    me, peers = _me_and_peers()
    return [pltpu.make_async_remote_copy(
        src_ref=srcs[t] if gather else srcs[t].at[peer], dst_ref=zones[t].at[me], send_sem=send_sems.at[7 * t + kk],
        recv_sem=recv_sems.at[7 * t + kk], device_id=pos, device_id_type=pl.DeviceIdType.MESH)
        for t in range(len(srcs)) for kk, (pos, peer) in enumerate(peers)]


def _exchange_start(tensors, *, gather, name):
    n = len(tensors)
    zones = [lax.empty((N_DEV, *(t.shape if gather else t.shape[1:])), t.dtype) for t in tensors]

    def body(*refs):
        for cp in _exchange_copies(refs[:n], refs[n:2 * n], refs[2 * n], refs[2 * n + 1], gather):
            cp.start()
        refs[-1][...] = jnp.zeros_like(refs[-1])

    buffers = [pltpu.HBM(a.shape, a.dtype) for a in tensors + zones]
    res = pl.pallas_call(
        body, name=name, in_specs=[_HBM] * (2 * n),
        out_shape=(pltpu.SemaphoreType.DMA((7 * n,)), pltpu.SemaphoreType.DMA((7 * n,)), *buffers, jax.ShapeDtypeStruct((8, LANES), F32)),
        out_specs=(_SEM, _SEM, *[_HBM] * (2 * n), pl.BlockSpec(memory_space=pltpu.VMEM)),
        input_output_aliases={i: 2 + i for i in range(2 * n)},
        compiler_params=pltpu.CompilerParams(has_side_effects=pltpu.SideEffectType.DATAFLOW_SIDE_EFFECTING),
    )(*[pltpu.with_memory_space_constraint(a, pltpu.HBM) for a in tensors + zones])
    return dict(sems=res[:2], buffers=res[2:2 + 2 * n], gather=gather, started=res[-1])


def _exchange_wait(started, after, *, name):
    n = len(started["buffers"]) // 2

    def body(*refs):
        for cp in _exchange_copies(refs[:n], refs[n:2 * n], refs[2 * n], refs[2 * n + 1], started["gather"]):
            cp.wait_send()
            cp.wait_recv()

    res = pl.pallas_call(
        body, name=name, in_specs=[_HBM] * (2 * n) + [_SEM, _SEM] + [_ANY] * len(after),
        out_shape=tuple(pltpu.HBM(a.shape, a.dtype) for a in started["buffers"]), out_specs=tuple([_HBM] * (2 * n)),
        input_output_aliases={i: i for i in range(2 * n)},
        compiler_params=pltpu.CompilerParams(has_side_effects=pltpu.SideEffectType.DATAFLOW_SIDE_EFFECTING),
    )(*started["buffers"], *started["sems"], *after)
    return res[:n], res[n:]


def _adamw(contrib, w, m, v, *, name):
    rows, cols = w.shape
    tile = min(rows, ADAM_ROWS)

    def body(c_ref, w_ref, m_ref, v_ref, g_ref, d_ref, nm_ref, nv_ref):
        g = c_ref[0].astype(F32)
        for s in range(1, N_DEV):
            g = g + c_ref[s].astype(F32)
        m_new = ADAM_B1 * m_ref[...] + (1.0 - ADAM_B1) * g
        v_new = ADAM_B2 * v_ref[...] + (1.0 - ADAM_B2) * (g * g)
        m_hat = m_new / (1.0 - ADAM_B1 ** ADAM_STEP)
        v_hat = v_new / (1.0 - ADAM_B2 ** ADAM_STEP)
        g_ref[...] = g
        d_ref[...] = -ADAM_LR * (m_hat / (jnp.sqrt(v_hat) + ADAM_EPS) + ADAM_WD * w_ref[...])
        nm_ref[...] = m_new
        nv_ref[...] = v_new

    spec = pl.BlockSpec((tile, cols), lambda i: (i, 0))
    return pl.pallas_call(
        body, name=name, grid=(rows // tile,),
        in_specs=[pl.BlockSpec((N_DEV, tile, cols), lambda i: (0, i, 0)), spec, spec, spec], out_specs=[spec] * 4,
        out_shape=[jax.ShapeDtypeStruct((rows, cols), F32)] * 4, compiler_params=_cparams("parallel"),
    )(contrib, w, m, v)


class _Weights:
    def __init__(self, gathers, vectors, me):
        self.gathers, self.ready, self.me, self.after = gathers, dict(vectors), me, ()

    def arrive_after(self, *values):
        self.after = values

    def __getitem__(self, name):
        if name not in self.ready:
            gi = next(i for i, group in enumerate(GATHER_GROUPS) if name in group)
            after = [*self.after, *[g["started"] for g in self.gathers]]
            shards, zones = _exchange_wait(self.gathers[gi], after, name=f"gather_wait_{gi}")
            for n, shard, zone in zip(GATHER_GROUPS[gi], shards, zones, strict=True):
                blocks = lax.dynamic_update_slice_in_dim(zone, shard[None], self.me, 0)
                self.ready[n] = _to_kernel_layout(n, _from_blocks(blocks, n))
        return self.ready[name]


def kernel(x, mem, w_in, w_mem_kv, q_a_gain, w_q_b, kv_a_gain, w_kv_b, w_branch_mla, w_branch_sb, w_branch_mem, w_merge_gate, b_merge_gate, w_out, ln_gain, ln_bias, loss_target, m_w_in, m_w_mem_kv, m_q_a_gain, m_w_q_b, m_kv_a_gain, m_w_kv_b, m_w_branch_mla, m_w_branch_sb, m_w_branch_mem, m_w_merge_gate, m_b_merge_gate, m_w_out, m_ln_gain, m_ln_bias, v_w_in, v_w_mem_kv, v_q_a_gain, v_w_q_b, v_kv_a_gain, v_w_kv_b, v_w_branch_mla, v_w_branch_sb, v_w_branch_mem, v_w_merge_gate, v_b_merge_gate, v_w_out, v_ln_gain, v_ln_bias):
    given = dict(locals())
    small_names = [n for n, _ in SMALL]
    smalls = lambda prefix: [given[prefix + n] for n in small_names]
    me = 4 * lax.axis_index("x") + 2 * lax.axis_index("y") + lax.axis_index("c")

    gathers = [_exchange_start([given[n][0].astype(BF16) for n in group], gather=True, name=f"gather_start_{gi}")
               for gi, group in enumerate(GATHER_GROUPS)]
    w = _Weights(gathers, {n: given[n] for n in small_names}, me)
    exchanges = []
    results = [{}, {}, {}, {}]

    def finish(gi, after):
        names, started = exchanges[gi]
        sent, zones = _exchange_wait(started, after, name=f"grads_wait_{gi}")
        done = []
        for n, blocks, zone in zip(names, sent, zones, strict=True):
            own = lax.dynamic_index_in_dim(blocks, me, 0, keepdims=True)
            contrib = lax.dynamic_update_slice_in_dim(zone, own, me, 0)
            outs = _adamw(contrib, given[n][0], given["m_" + n][0], given["v_" + n][0], name=f"adamw_{n}")
            for kind, res in zip(results, outs, strict=True):
                kind[n] = res[None]
            done.append(outs[1])
        return done

    def emit(grads):
        blocks = [_to_blocks(_from_kernel_layout(n, g), n).astype(BF16) for n, g in grads.items()]
        exchanges.append((tuple(grads), _exchange_start(blocks, gather=False, name=f"grads_start_{len(exchanges)}")))
        started = [exchanges[-1][1]["started"]]
        if len(exchanges) == len(GRAD_GROUPS):
            for gi in range(len(GRAD_GROUPS) - 1):
                started += finish(gi, started[:1])
        return started

    loss, grad_x, grads = _local_step(x[0], mem[0], loss_target[0], w, emit)

    contrib_small = _share_small(_pack_small([grads[n] for n in small_names], loss), name="share_small")
    sml = _adamw(contrib_small, _pack_small(smalls("")), _pack_small(smalls("m_")), _pack_small(smalls("v_")), name="adamw_small")
    for kind, packed in zip(results, sml, strict=True):
        kind.update(zip(small_names, _unpack_small(packed), strict=True))
    finish(len(GRAD_GROUPS) - 1, [grad_x])
    order = ["w_in", "w_mem_kv", "q_a_gain", "w_q_b", "kv_a_gain", "w_kv_b", "w_branch_mla", "w_branch_sb", "w_branch_mem",
             "w_merge_gate", "b_merge_gate", "w_out", "ln_gain", "ln_bias"]
    loss_out = sml[0].reshape(-1)[LOSS_INDEX]
    return (loss_out, grad_x[None], *[kind[n] for kind in results for n in order])
```

```python
import math

import jax
import jax.numpy as jnp
from jax import lax
from jax.experimental import pallas as pl
from jax.experimental.pallas import tpu as pltpu

F32, BF16 = jnp.float32, jnp.bfloat16

N_DEV = 8
D_MODEL = 1024
MLA_HEADS, MLA_NOPE, MLA_ROPE, MLA_V = 8, 64, 32, 64
MLA_Q_LORA, MLA_KV_LORA = 256, 128
SB_HEAD_DIM = 64
MEM_HEAD_DIM = 128
ROPE_BASE = 10000.0
RMS_EPS = 1e-6
LN_EPS = 1e-5
DEEPNORM_ALPHA = 2.0 ** 0.25
ADAM_LR, ADAM_B1, ADAM_B2, ADAM_EPS, ADAM_WD, ADAM_STEP = 0.001, 0.9, 0.999, 1e-08, 0.01, 10
LOG2E, LN2 = math.log2(math.e), math.log(2.0)

LANES = 128
GROUPS = 4
PROJ_WIDTH = 4096
COL_CQ, COL_CKV, COL_KROPE, COL_GATE_A, COL_GATE_B, COL_GATE_M = 0, 256, 384, 512, 1024, 1536
QKV_FIRST, QKV_WIDTH = 2048, 2048
COL_QB, COL_KB, COL_VB, COL_QM = 0, 512, 1024, 1536
IN_PIECES = ((0, 416), None, (416, 512), (2464, 512), (3488, 512), (928, 512), (1440, 512), (1952, 512), (2976, 512))
IN_PAD = 96

VMEM_LIMIT_BYTES = 56 * 1024 * 1024
NEG_BIG = -1e30
Q_BLOCK = 512
MEM_Q_BLOCK = 2048
SB_BWD_Q_BLOCK = 512
TRI_BLOCK = 256
TILE_ROWS = 64
KEY_CHUNK = 512

SHARDED = {
    "w_in": ((1024, 4000), 1), "w_mem_kv": ((1024, 1024), 0), "w_q_b": ((256, 768), 1), "w_kv_b": ((128, 1024), 1),
    "w_branch_mla": ((512, 1024), 1), "w_branch_sb": ((512, 1024), 1), "w_branch_mem": ((512, 1024), 1),
    "w_merge_gate": ((1024, 3072), 1), "w_out": ((1024, 1024), 0),
}
GATHER_GROUPS = (("w_in",), ("w_merge_gate",), ("w_q_b", "w_kv_b", "w_mem_kv", "w_branch_mla", "w_branch_sb", "w_branch_mem", "w_out"))
GRAD_GROUPS = (("w_out", "w_merge_gate", "w_branch_mla", "w_branch_sb", "w_branch_mem"), ("w_mem_kv", "w_q_b", "w_kv_b"), ("w_in",))
SMALL = (("q_a_gain", 256), ("kv_a_gain", 128), ("b_merge_gate", 3072), ("ln_gain", 1024), ("ln_bias", 1024))
SMALL_ROWS, SMALL_LANES = 48, 128
ADAM_ROWS = 256
LOSS_INDEX = 5504


def _cparams(*sem):
    return pltpu.CompilerParams(dimension_semantics=sem or None, vmem_limit_bytes=VMEM_LIMIT_BYTES)


_DIMS = {"nn": (((1,), (0,)), ((), ())), "nt": (((1,), (1,)), ((), ())), "tn": (((0,), (0,)), ((), ()))}


def _dot(a, b, dims):
    return lax.dot_general(a, b, _DIMS[dims], preferred_element_type=F32)


def _tile(dim, want):
    if dim <= want:
        return dim
    t = want - want % LANES
    while dim % t:
        t -= LANES
    assert t > 0, (dim, want)
    return t


_ANY = pl.BlockSpec(memory_space=pl.ANY)


def _mm(a, b, dims, *, name, out_dtype=F32, add=None, add_scale=1.0, col_scale=None, b_cols=None, behind=None,
        tm=1024, tn=1024, tk=1024):
    batch = a.shape[0] if a.ndim == 3 else None
    if dims == "nn":
        (m, k), (k2, n) = a.shape[-2:], b.shape[-2:]
    elif dims == "nt":
        (m, k), (n, k2) = a.shape[-2:], b.shape[-2:]
    else:
        (k, m), (k2, n) = a.shape[-2:], b.shape[-2:]
    assert k == k2 and a.ndim == b.ndim, (a.shape, b.shape, dims)
    assert batch is None or (b.shape[0] == batch and add is None and col_scale is None and b_cols is None)
    b_first = 0
    if b_cols is not None:
        assert dims == "nn"
        b_first, n = b_cols
    tm, tn, tk = _tile(m, tm), _tile(n, tn), _tile(k, tk)
    assert b_first % tn == 0
    jb = b_first // tn
    nk = k // tk

    def spec(block, index):
        if batch is None:
            return pl.BlockSpec(block, lambda bb, i, j, kk: index(i, j, kk))
        return pl.BlockSpec((None, *block), lambda bb, i, j, kk: (bb, *index(i, j, kk)))

    a_spec = spec((tk, tm), lambda i, j, kk: (kk, i)) if dims == "tn" else spec((tm, tk), lambda i, j, kk: (i, kk))
    b_spec = spec((tn, tk), lambda i, j, kk: (j, kk)) if dims == "nt" else spec((tk, tn), lambda i, j, kk: (kk, jb + j))
    o_spec = spec((tm, tn), lambda i, j, kk: (i, j))
    behind = [] if behind is None else behind if isinstance(behind, (list, tuple)) else [behind]
    optional = [(add, o_spec), (col_scale, pl.BlockSpec((1, tn), lambda bb, i, j, kk: (0, j))), *[(v, _ANY) for v in behind]]
    present = [(v, spec) for v, spec in optional if v is not None]

    def body(*refs):
        a_ref, b_ref = refs[:2]
        extra = iter(refs[2:2 + len(present)])
        add_ref = next(extra) if add is not None else None
        scale_ref = next(extra) if col_scale is not None else None
        o_ref = refs[2 + len(present)]
        part = _dot(a_ref[...].astype(BF16), b_ref[...].astype(BF16), dims)

        def finish(r):
            if add is not None:
                r = r + add_scale * add_ref[...]
            if col_scale is not None:
                r = r * scale_ref[...]
            o_ref[...] = r.astype(out_dtype)

        if nk == 1:
            finish(part)
            return
        acc = refs[-1]
        kk = pl.program_id(3)

        @pl.when(kk == 0)
        def _():
            acc[...] = part

        @pl.when(kk > 0)
        def _():
            acc[...] += part

        @pl.when(kk == nk - 1)
        def _():
            finish(acc[...])

    return pl.pallas_call(
        body, name=name, grid=(batch or 1, m // tm, n // tn, nk),
        in_specs=[a_spec, b_spec] + [spec for _, spec in present], out_specs=o_spec,
        out_shape=jax.ShapeDtypeStruct((m, n) if batch is None else (batch, m, n), out_dtype),
        scratch_shapes=[pltpu.VMEM((tm, tn), F32)] if nk > 1 else [],
        compiler_params=_cparams("parallel", "parallel", "parallel", "arbitrary"),
    )(a, b, *[v for v, _ in present])


def _rowwise(fn, ins, outs, *, name, rows, tr=512):
    n_in = len(ins)
    tr = min(tr, rows)
    in_specs, args = [], []
    for it in ins:
        if isinstance(it, tuple) and it[0] == "whole":
            in_specs.append(pl.BlockSpec(it[1].shape, lambda i, nd=it[1].ndim: (0,) * nd))
            args.append(it[1])
            continue
        arr, w, off = it if isinstance(it, tuple) else (it, it.shape[-1], 0)
        assert off % w == 0
        cb = off // w
        if arr.ndim == 3:
            in_specs.append(pl.BlockSpec((arr.shape[0], tr, w), lambda i, cb=cb: (0, i, cb)))
        elif arr.shape[0] == 1:
            in_specs.append(pl.BlockSpec((1, w), lambda i, cb=cb: (0, cb)))
        else:
            in_specs.append(pl.BlockSpec((tr, w), lambda i, cb=cb: (i, cb)))
        args.append(arr)
    out_shape, out_specs, is_sum = [], [], []
    for out in outs:
        is_sum.append(out[0] == "sum")
        if out[0] == "sum":
            shape = out[1] if isinstance(out[1], tuple) else (1, out[1])
            out_shape.append(jax.ShapeDtypeStruct(shape, F32))
            out_specs.append(pl.BlockSpec(shape, lambda i: (0, 0)))
        elif len(out) == 3:
            out_shape.append(jax.ShapeDtypeStruct((out[0], rows, out[1]), out[2]))
            out_specs.append(pl.BlockSpec((out[0], tr, out[1]), lambda i: (0, i, 0)))
        else:
            out_shape.append(jax.ShapeDtypeStruct((rows, out[0]), out[1]))
            out_specs.append(pl.BlockSpec((tr, out[0]), lambda i: (i, 0)))

    def body(*refs):
        res = fn(*[r[...] for r in refs[:n_in]])
        for r, val, s in zip(refs[n_in:], res, is_sum, strict=True):
            if s:
                @pl.when(pl.program_id(0) == 0)
                def _(r=r):
                    r[...] = jnp.zeros_like(r)

                r[...] += val
            elif isinstance(val, (list, tuple)):
                for n, part in enumerate(val):
                    r[n] = part.astype(r.dtype)
            else:
                r[...] = val.astype(r.dtype)

    return pl.pallas_call(
        body, name=name, grid=(rows // tr,), in_specs=in_specs, out_specs=out_specs, out_shape=out_shape,
        compiler_params=_cparams("arbitrary"),
    )(*args)


def _colsum(v):
    return jnp.sum(v, axis=0, keepdims=True)


def _sigmoid(v):
    return 1.0 / (1.0 + jnp.exp(-v))


def _lane_groups(v):
    return [v[:, g * LANES:(g + 1) * LANES] for g in range(v.shape[1] // LANES)]


def _swap_halves(v, first_lane):
    lane = lax.broadcasted_iota(jnp.int32, v.shape, 1)
    return jnp.where(lane < first_lane + 16, pltpu.roll(v, 112, axis=1), pltpu.roll(v, 16, axis=1))


def _lane_sum(acc, v):
    for part in _lane_groups(v):
        acc = acc + part
    return acc


def _low_half(shape):
    return lax.broadcasted_iota(jnp.int32, shape, 1) < LANES // 2


def _select_heads(per_head, pick):
    if len(per_head) == 1:
        return pick(per_head[0], 0)
    return jnp.where(_low_half(per_head[0].shape), pick(per_head[0], 0), pick(per_head[1], 1))


def _attn_specs(s, sk, hp, bq, q0, k0, v0):
    wq = hp * LANES
    assert q0 % wq == 0 and k0 % wq == 0 and v0 % LANES == 0
    qb0, kb0, vb0 = q0 // wq, k0 // wq, v0 // LANES
    q_spec = pl.BlockSpec((bq, wq), lambda g, i: (i, qb0 + g))
    k_spec = pl.BlockSpec((sk, wq), lambda g, i: (0, kb0 + g))
    v_spec = pl.BlockSpec((sk, LANES), lambda g, i: (0, vb0 + g))
    row_out = lambda w: pl.BlockSpec((bq, w), lambda g, i: (i, g))
    key_out = lambda w: pl.BlockSpec((sk, w), lambda g, i: (0, g))
    return q_spec, k_spec, v_spec, row_out, key_out


def _chunks(i, bq, ch, sk, causal):
    return ((i + 1) * bq - 1) // ch if causal else jnp.int32(sk // ch - 1)


def _positions(i, c, bq, ch):
    return (i * bq + lax.broadcasted_iota(jnp.int32, (bq, ch), 0), c * ch + lax.broadcasted_iota(jnp.int32, (bq, ch), 1))


def _softmax_fwd(q, k, v, *, hp, causal, name, q0=0, k0=0, v0=0, q_rows=Q_BLOCK):
    s, sk = q.shape[0], k.shape[0]
    bq, ch = min(q_rows, s), min(KEY_CHUNK, sk)
    assert not causal or bq <= ch
    q_spec, k_spec, v_spec, row_out, _ = _attn_specs(s, sk, hp, bq, q0, k0, v0)

    def body(q_ref, k_ref, v_ref, o_ref, lse_ref, s_scr):
        i = pl.program_id(1)
        qs = _lane_groups(q_ref[...])
        last = _chunks(i, bq, ch, sk, causal)

        def scores(c, ms, masked):
            off = pl.multiple_of(c * ch, ch)
            out = []
            for j in range(hp):
                sc = _dot(qs[j], k_ref[pl.ds(off, ch), j * LANES:(j + 1) * LANES], "nt")
                if masked:
                    qpos, kpos = _positions(i, c, bq, ch)
                    sc = jnp.where(kpos <= qpos, sc, NEG_BIG)
                s_scr[j, c] = sc
                m = ms[j]
                for part in _lane_groups(sc):
                    m = jnp.maximum(m, part)
                out.append(m)
            return tuple(out)

        ms = lax.fori_loop(0, last, lambda c, m: scores(c, m, False), tuple(jnp.full((bq, LANES), NEG_BIG, F32) for _ in range(hp)))
        ms = scores(last, ms, causal)
        row_max = [jnp.max(m, axis=1, keepdims=True) for m in ms]

        def weigh(c, carry):
            off = pl.multiple_of(c * ch, ch)
            vt = v_ref[pl.ds(off, ch), :]
            out = []
            for j in range(hp):
                l, acc = carry[j]
                p = jnp.exp2(s_scr[j, c] - row_max[j])
                out.append((_lane_sum(l, p), acc + _dot(p.astype(BF16), vt, "nn")))
            return tuple(out)

        zero = jnp.zeros((bq, LANES), F32)
        res = lax.fori_loop(0, last + 1, weigh, tuple((zero, zero) for _ in range(hp)))
        row_sum = [jnp.sum(l, axis=1, keepdims=True) for l, _ in res]
        o_ref[...] = _select_heads([acc for _, acc in res], lambda acc, j: acc / row_sum[j])
        lse_ref[...] = _select_heads([jnp.broadcast_to(row_max[j] + jnp.log2(row_sum[j]), (bq, LANES)) for j in range(hp)], lambda a, j: a)

    return pl.pallas_call(
        body, name=name, grid=(GROUPS, s // bq), in_specs=[q_spec, k_spec, v_spec], out_specs=[row_out(LANES), row_out(LANES)],
        out_shape=[jax.ShapeDtypeStruct((s, GROUPS * LANES), F32)] * 2,
        scratch_shapes=[pltpu.VMEM((hp, sk // ch, bq, ch), F32)], compiler_params=_cparams("parallel", "arbitrary"),
    )(q, k, v)


def _head_cotangent(do, j, hp):
    if hp == 1:
        return do
    return jnp.where(_low_half(do.shape) == (j == 0), do, 0.0)


def _softmax_bwd(q, k, v, o, do, lse, behind, *, hp, causal, dq_scale, name, q0=0, k0=0, v0=0, q_rows=Q_BLOCK):
    s, sk = q.shape[0], k.shape[0]
    bq, ch = min(q_rows, s), min(KEY_CHUNK, sk)
    assert not causal or bq <= ch
    wq = hp * LANES
    q_spec, k_spec, v_spec, row_out, key_out = _attn_specs(s, sk, hp, bq, q0, k0, v0)

    def body(q_ref, k_ref, v_ref, o_ref, do_ref, lse_ref, _, dq_ref, dk_ref, dv_ref, dk_t, dv_t):
        i = pl.program_id(1)

        @pl.when(i == 0)
        def _():
            dk_t[...] = jnp.zeros_like(dk_t)
            dv_t[...] = jnp.zeros_like(dv_t)

        qs = _lane_groups(q_ref[...])
        do_all, o_all, lse_all = do_ref[...], o_ref[...], lse_ref[...]
        dos, deltas, lses = [], [], []
        for j in range(hp):
            d = _head_cotangent(do_all, j, hp)
            deltas.append(jnp.sum(d * o_all, axis=1, keepdims=True))
            dos.append(d.astype(BF16))
            lses.append(lse_all[:, j * (LANES // hp):j * (LANES // hp) + 1])
        last = _chunks(i, bq, ch, sk, causal)

        def chunk(c, dqs, masked):
            off = pl.multiple_of(c * ch, ch)
            vt = v_ref[pl.ds(off, ch), :]
            out, dks, dv = [], [], None
            for j in range(hp):
                kt = k_ref[pl.ds(off, ch), j * LANES:(j + 1) * LANES]
                p = jnp.exp2(_dot(qs[j], kt, "nt") - lses[j])
                if masked:
                    qpos, kpos = _positions(i, c, bq, ch)
                    p = jnp.where(kpos <= qpos, p, 0.0)
                ds = (p * (_dot(dos[j], vt, "nt") - deltas[j]) * LN2).astype(BF16)
                out.append(dqs[j] + _dot(ds, kt, "nn"))
                dks.append(_dot(qs[j], ds, "tn"))
                dvj = _dot(dos[j], p.astype(BF16), "tn")
                dv = dvj if dv is None else dv + dvj
            dk_t[c] += dks[0] if hp == 1 else jnp.concatenate(dks, axis=0)
            dv_t[c] += dv
            return tuple(out)

        dqs = lax.fori_loop(0, last, lambda c, d: chunk(c, d, False), tuple(jnp.zeros((bq, LANES), F32) for _ in range(hp)))
        dqs = chunk(last, dqs, causal)
        dq_ref[...] = (dqs[0] if hp == 1 else jnp.concatenate(dqs, axis=1)) * dq_scale

        @pl.when(i == s // bq - 1)
        def _():
            for c in range(sk // ch):
                dk_ref[c * ch:(c + 1) * ch, :] = dk_t[c].T
                dv_ref[c * ch:(c + 1) * ch, :] = dv_t[c].T

    return pl.pallas_call(
        body, name=name, grid=(GROUPS, s // bq),
        in_specs=[q_spec, k_spec, v_spec, row_out(LANES), row_out(LANES), row_out(LANES), _ANY],
        out_specs=[row_out(wq), key_out(wq), key_out(LANES)],
        out_shape=[jax.ShapeDtypeStruct((s, GROUPS * wq), F32), jax.ShapeDtypeStruct((sk, GROUPS * wq), F32),
                   jax.ShapeDtypeStruct((sk, GROUPS * LANES), F32)],
        scratch_shapes=[pltpu.VMEM((sk // ch, wq, ch), F32), pltpu.VMEM((sk // ch, LANES, ch), F32)],
        compiler_params=_cparams("arbitrary", "arbitrary"),
    )(q, k, v, o, do, lse, behind)


def _log2_sigmoid_pair(z2):
    minus_abs = lax.bitcast_convert_type(lax.bitcast_convert_type(z2, jnp.uint32) | jnp.uint32(0x80000000), F32)
    log_beta = jnp.minimum(z2, 0.0) - jnp.log2(1.0 + jnp.exp2(minus_abs))
    return log_beta, log_beta - z2


def _tilewise(fn, *arrays):
    rows, cols = arrays[0].shape
    step = min(TILE_ROWS, rows)
    grid = [[fn(*[None if a is None else a[r:r + step, c:c + LANES] for a in arrays]) for c in range(0, cols, LANES)]
            for r in range(0, rows, step)]
    return [jnp.concatenate([jnp.concatenate([cell[k] for cell in row], axis=1) for row in grid], axis=0)
            for k in range(len(grid[0][0]))]


def _split(v):
    hi = v.astype(BF16)
    return hi, (v - hi.astype(F32)).astype(BF16)


def _tri(n, after):
    rows, cols = lax.broadcasted_iota(jnp.int32, (n, n), 0), lax.broadcasted_iota(jnp.int32, (n, n), 1)
    return (rows > cols if after else rows < cols).astype(BF16)


def _running_sums(v, terms, start, tri, backwards):
    n = tri.shape[0]
    n_blocks = v.shape[1] // n
    order = range(n_blocks - 1, -1, -1) if backwards else range(n_blocks)
    stacked = tri if len(terms) == 1 else jnp.concatenate([tri] * len(terms), axis=0)
    parts, run = [None] * n_blocks, start
    for t in order:
        cols = slice(t * n, (t + 1) * n)
        lhs = terms[0][:, cols] if len(terms) == 1 else jnp.concatenate([term[:, cols] for term in terms], axis=1)
        parts[t] = _dot(lhs, stacked, "nn") + run
        run = run + jnp.sum(v[:, cols], axis=1, keepdims=True)
    return (parts[0] if n_blocks == 1 else jnp.concatenate(parts, axis=1)), run


def _sb_weights(qm, kt, run, tri, strict):
    def logs(z2, keep):
        log_beta, log_keep = _log2_sigmoid_pair(z2)
        if keep is not None:
            log_keep = jnp.where(keep, log_keep, 0.0)
        return log_beta, log_keep, *_split(log_keep)

    log_beta, log_keep, hi, lo = _tilewise(logs, _dot(qm, kt, "nt"), strict)
    behind, run = _running_sums(log_keep, (hi, lo), run, tri, True)

    def weigh(log_beta, behind, keep):
        a = jnp.exp2(log_beta + behind)
        return (a if keep is None else jnp.where(keep, a, 0.0),)

    (a,) = _tilewise(weigh, log_beta, behind, strict)
    return a, log_beta, run


class _Copies:
    def __init__(self, copies):
        self.copies = copies

    def start(self):
        for cp in self.copies:
            cp.start()

    def wait(self):
        for cp in self.copies:
            cp.wait()


def _sb_queries(q_all):
    low = _low_half(q_all.shape)
    zero = jnp.zeros_like(q_all)
    return [jnp.where(low, q_all, zero), jnp.where(low, zero, q_all)]


def _sb_fwd(qkv, *, q0, k0, v0, name):
    s = qkv.shape[0]
    bq, ch = min(Q_BLOCK, s), min(KEY_CHUNK, s)
    assert bq == ch
    n_q = s // bq
    q_spec, k_spec, v_spec, row_out, _ = _attn_specs(s, s, 1, bq, q0, k0, v0)

    def body(q_ref, k_ref, v_ref, o_ref, saved_ref, stage, sems):
        g, i = pl.program_id(0), pl.program_id(1)
        qms = _sb_queries(q_ref[...])
        tri = _tri(min(TRI_BLOCK, ch), True)
        last = _chunks(i, bq, ch, s, True)
        first_tile = i * (i + 1) // 2

        def save(slot, c):
            return _Copies([pltpu.make_async_copy(stage.at[slot, p], saved_ref.at[g, first_tile + c, p], sems.at[slot, p])
                            for p in range(2)])

        def chunk(c, step, carry, masked):
            off = pl.multiple_of(c * ch, ch)
            kt, vt = k_ref[pl.ds(off, ch), :], v_ref[pl.ds(off, ch), :]
            strict = None
            if masked:
                qpos, kpos = _positions(i, c, bq, ch)
                strict = kpos < qpos
            slot = step % 3
            if not masked:
                @pl.when(step >= 3)
                def _():
                    save(slot, c).wait()
            out = []
            for j in range(2):
                run, acc = carry[j]
                a, _, run = _sb_weights(qms[j], kt, run, tri, strict)
                a = a.astype(BF16)
                stage[slot, j] = a
                out.append((run, acc + _dot(a, vt, "nn")))
            save(slot, c).start()
            return tuple(out)

        carry = chunk(last, 0, tuple((jnp.zeros((bq, 1), F32), jnp.zeros((bq, LANES), F32)) for _ in range(2)), True)
        res = lax.fori_loop(0, last, lambda n, c: chunk(last - 1 - n, n + 1, c, False), carry)
        for back in range(3):
            @pl.when(last >= back)
            def _(back=back):
                save((last - back) % 3, 0).wait()

        o_ref[...] = _select_heads([acc for _, acc in res], lambda acc, j: acc)

    return pl.pallas_call(
        body, name=name, grid=(GROUPS, n_q), in_specs=[q_spec, k_spec, v_spec], out_specs=[row_out(LANES), _ANY],
        out_shape=[jax.ShapeDtypeStruct((s, GROUPS * LANES), F32),
                   jax.ShapeDtypeStruct((GROUPS, n_q * (n_q + 1) // 2, 2, bq, ch), BF16)],
        scratch_shapes=[pltpu.VMEM((3, 2, bq, ch), BF16), pltpu.SemaphoreType.DMA((3, 2))],
        compiler_params=_cparams("parallel", "arbitrary"),
    )(qkv, qkv, qkv)


def _sb_bwd(qkv, do, saved, behind, *, q0, k0, v0, dq_scale, name):
    s = qkv.shape[0]
    bq, ch = min(SB_BWD_Q_BLOCK, s), min(KEY_CHUNK, s)
    assert bq == ch and saved.shape[2:] == (2, bq, ch)
    q_spec, k_spec, v_spec, row_out, key_out = _attn_specs(s, s, 1, bq, q0, k0, v0)

    def body(q_ref, k_ref, v_ref, do_ref, saved_ref, _, dq_ref, dk_ref, dv_ref, g_s, beta_s, stage, sems):
        g_index, i = pl.program_id(0), pl.program_id(1)

        @pl.when(i == 0)
        def _():
            dk_ref[...] = jnp.zeros_like(dk_ref)
            dv_ref[...] = jnp.zeros_like(dv_ref)

        qms = _sb_queries(q_ref[...])
        do_all = do_ref[...]
        dos = [_head_cotangent(do_all, j, 2).astype(BF16) for j in range(2)]
        dos_ln2 = [(_head_cotangent(do_all, j, 2) * LN2).astype(BF16) for j in range(2)]
        tri_before = _tri(min(TRI_BLOCK, ch), False)
        last = _chunks(i, bq, ch, s, True)
        first_tile = i * (i + 1) // 2

        def strict_mask(c):
            qpos, kpos = _positions(i, c, bq, ch)
            return kpos < qpos

        def fetch(slot, c):
            return _Copies([pltpu.make_async_copy(saved_ref.at[g_index, first_tile + c, p], stage.at[slot, p], sems.at[slot, p])
                            for p in range(2)])

        def sweep1(n, unused):
            c, slot = last - n, n % 3

            @pl.when(c >= 2)
            def _():
                fetch((n + 2) % 3, c - 2).start()

            fetch(slot, c).wait()
            off = pl.multiple_of(c * ch, ch)
            kt, vt = k_ref[pl.ds(off, ch), :], v_ref[pl.ds(off, ch), :]
            dv = None
            for j in range(2):
                a = stage[slot, j]
                g_s[j, c] = (a.astype(F32) * _dot(dos_ln2[j], vt, "nt")).astype(BF16)
                beta_s[j, c] = (1.0 / (1.0 + jnp.exp2(-_dot(qms[j], kt, "nt")))).astype(BF16)
                dvj = _dot(a, dos[j], "tn")
                dv = dvj if dv is None else dv + dvj
            dv_ref[pl.ds(off, ch), :] += dv
            return unused

        fetch(0, last).start()

        @pl.when(last >= 1)
        def _():
            fetch(1, last - 1).start()

        lax.fori_loop(0, last + 1, sweep1, 0)

        def sweep2(c, carry, masked):
            off = pl.multiple_of(c * ch, ch)
            kt = k_ref[pl.ds(off, ch), :]
            out, dk = [], None
            for j in range(2):
                before, dq = carry[j]
                g16, beta = g_s[j, c], beta_s[j, c].astype(F32)
                g = g16.astype(F32)
                in_front, before = _running_sums(g, (g16,), before, tri_before, False)
                dz = g * (1.0 - beta) - beta * in_front
                if masked:
                    dz = jnp.where(strict_mask(c), dz, 0.0)
                dz = dz.astype(BF16)
                dkj = _dot(dz, qms[j], "tn")
                dk = dkj if dk is None else dk + dkj
                out.append((before, dq + _dot(dz, kt, "nn")))
            dk_ref[pl.ds(off, ch), :] += dk
            return tuple(out)

        carry = lax.fori_loop(0, last, lambda c, cr: sweep2(c, cr, False),
                              tuple((jnp.zeros((bq, 1), F32), jnp.zeros((bq, LANES), F32)) for _ in range(2)))
        res = sweep2(last, carry, True)
        dq_ref[...] = _select_heads([dq for _, dq in res], lambda dq, j: dq) * dq_scale

    n_ch = s // ch
    return pl.pallas_call(
        body, name=name, grid=(GROUPS, s // bq), in_specs=[q_spec, k_spec, v_spec, row_out(LANES), _ANY, _ANY],
        out_specs=[row_out(LANES), key_out(LANES), key_out(LANES)],
        out_shape=[jax.ShapeDtypeStruct((s, GROUPS * LANES), F32)] * 3,
        scratch_shapes=[pltpu.VMEM((2, n_ch, bq, ch), BF16)] * 2 + [pltpu.VMEM((3, 2, bq, ch), BF16), pltpu.SemaphoreType.DMA((3, 2))],
        compiler_params=_cparams("arbitrary", "arbitrary"),
    )(qkv, qkv, qkv, do, saved, behind)


def _rope_tables(s):
    half = MLA_ROPE // 2
    freqs = ROPE_BASE ** (-jnp.arange(half, dtype=F32) / half)
    ang = jnp.arange(s, dtype=F32)[:, None] * freqs[None, :]
    cos, sin = jnp.cos(ang), jnp.sin(ang)
    tail = jnp.zeros((s, LANES - MLA_NOPE - MLA_ROPE), F32)
    lead = lambda fill: jnp.full((s, MLA_NOPE), fill, F32)
    return dict(
        cos_k0=jnp.concatenate([cos, cos, lead(0.0), tail], axis=1), sin_k0=jnp.concatenate([-sin, sin, lead(0.0), tail], axis=1),
        cos_k64=jnp.concatenate([lead(0.0), cos, cos, tail], axis=1), sin_k64=jnp.concatenate([lead(0.0), -sin, sin, tail], axis=1),
        cos_q=jnp.concatenate([lead(1.0), cos, cos, tail], axis=1),
        sin_k64_t=jnp.concatenate([lead(0.0), sin, -sin, tail], axis=1),
    )


def _local_step(x, mem, target, w, emit=lambda grads: [jnp.zeros((8, LANES), F32)]):
    s = x.shape[0]
    rope = _rope_tables(s)
    xb = x.astype(BF16)
    inv_d = 1.0 / D_MODEL
    scale_a = LOG2E / math.sqrt(MLA_NOPE + MLA_ROPE)
    scale_b = LOG2E / math.sqrt(SB_HEAD_DIM)
    scale_m = LOG2E / math.sqrt(MEM_HEAD_DIM)
    arrive_after = getattr(w, "arrive_after", lambda *values: None)
    memb = mem.astype(BF16)

    arrive_after(xb, memb, *rope.values())
    proj = _mm(xb, w["w_in"], "nn", name="proj", b_cols=(0, QKV_FIRST))
    one = jnp.ones((1, 512), F32)
    qkv = _mm(xb, w["w_in"], "nn", name="proj_qkv", b_cols=(QKV_FIRST, QKV_WIDTH), out_dtype=BF16,
              col_scale=jnp.concatenate([one * scale_b, one, one, one * scale_m], axis=1))
    arrive_after(qkv)
    pre = _mm(xb, w["w_merge_gate"], "nn", name="merge_pre", out_dtype=BF16)
    arrive_after(pre)

    def mla_inputs(c_q, c_kv, k_rope, g_q, g_kv, w_q, w_kv, cos_q, sin_q, cos_k, sin_k):
        n_q = (c_q * lax.rsqrt(jnp.mean(c_q * c_q, axis=1, keepdims=True) + RMS_EPS) * g_q).astype(BF16)
        n_kv = (c_kv * lax.rsqrt(jnp.mean(c_kv * c_kv, axis=1, keepdims=True) + RMS_EPS) * g_kv).astype(BF16)
        q_a = _dot(n_q, w_q, "nn")
        kv_a = _dot(n_kv, w_kv, "nn").astype(BF16)
        q = jnp.concatenate([(g * cos_q + _swap_halves(g, MLA_NOPE) * sin_q) * scale_a for g in _lane_groups(q_a)], axis=1)
        k_pe = pltpu.roll(k_rope * cos_k + _swap_halves(k_rope, 0) * sin_k, MLA_NOPE, axis=1).astype(BF16)
        k = jnp.concatenate([g + k_pe for g in _lane_groups(kv_a[:, :1024])], axis=1)
        return n_q, n_kv, q, k, kv_a[:, 1024:]

    n_q, n_kv, q_mla, k_mla, v_a = _rowwise(
        mla_inputs, [(proj, 256, COL_CQ), (proj, 128, COL_CKV), (proj, 128, COL_KROPE), w["q_a_gain"], w["kv_a_gain"],
                     ("whole", w["w_q_b"]), ("whole", w["w_kv_b"]), rope["cos_q"], rope["sin_k64"], rope["cos_k0"], rope["sin_k0"]],
        [(256, BF16), (128, BF16), (1024, BF16), (1024, BF16), (512, BF16)], name="mla_inputs", rows=s)
    o_a, lse_a = _softmax_fwd(q_mla, k_mla, v_a, hp=2, causal=True, name="mla_fwd")

    o_b, sb_saved = _sb_fwd(qkv, q0=COL_QB, k0=COL_KB, v0=COL_VB, name="sb_fwd")

    mem_kv =_mm(memb, w["w_mem_kv"], "nn", name="mem_kv", out_dtype=BF16)
    o_m, lse_m = _softmax_fwd(qkv, mem_kv, mem_kv, hp=1, causal=False, name="mem_fwd", q0=1536, v0=512, q_rows=MEM_Q_BLOCK)

    branches = ("mla", "sb", "mem")
    w_branch = jnp.stack([w[f"w_branch_{br}"] for br in branches])
    bias = w["b_merge_gate"]

    def head(oa, ob, om, ga, gb, gm, pa, pb, pm, ba, bb, bm, xv, tv, gain, bias_ln, w_b, w_o, w_g):
        us, ys, gs = [], [], []
        for n, (o, gate, p, b) in enumerate(((oa, ga, pa, ba), (ob, gb, pb, bb), (om, gm, pm, bm))):
            us.append((o * gate * _sigmoid(gate)).astype(BF16))
            ys.append(_dot(us[n], w_b[n], "nn"))
            gs.append(_sigmoid(p.astype(F32) + b))
        merged = (gs[0] * ys[0] + gs[1] * ys[1] + gs[2] * ys[2]).astype(BF16)
        z = DEEPNORM_ALPHA * xv + _dot(merged, w_o, "nn")
        zc = z - jnp.mean(z, axis=1, keepdims=True)
        rstd = lax.rsqrt(jnp.mean(zc * zc, axis=1, keepdims=True) + LN_EPS)
        xhat = zc * rstd
        err = xhat * gain + bias_ln - tv
        loss = 0.5 * jnp.sum(jnp.mean(err * err, axis=1, keepdims=True), axis=0, keepdims=True)
        dy = err * inv_d
        dxhat = dy * gain
        dz = rstd * (dxhat - jnp.mean(dxhat, axis=1, keepdims=True) - xhat * jnp.mean(dxhat * xhat, axis=1, keepdims=True))
        dz16 = dz.astype(BF16)
        dm = _dot(dz16, w_o, "nt")
        dpre = jnp.concatenate([dm * ys[n] * gs[n] * (1.0 - gs[n]) for n in range(3)], axis=1)
        dx = DEEPNORM_ALPHA * dz + _dot(dpre.astype(BF16), w_g, "nt")
        dys = [(dm * gs[n]).astype(BF16) for n in range(3)]
        d_os, d_gates = [], []
        for n, (o, gate) in enumerate(((oa, ga), (ob, gb), (om, gm))):
            du, sg = _dot(dys[n], w_b[n], "nt"), _sigmoid(gate)
            d_os.append(du * gate * sg)
            d_gates.append(du * o * sg * (1.0 + gate * (1.0 - sg)))
        return (us, merged, dx, dz16, _colsum(dy * xhat), _colsum(dy), jnp.broadcast_to(loss, (1, LANES)), dpre, _colsum(dpre),
                dys, *d_os, *d_gates)

    grads = {}
    (u, merged, dx, dzb, grads["ln_gain"], grads["ln_bias"], loss, dpre, grads["b_merge_gate"], dy, *rest) = _rowwise(
        head, [o_a, o_b, o_m, (proj, 512, COL_GATE_A), (proj, 512, COL_GATE_B), (proj, 512, COL_GATE_M),
               (pre, 1024, 0), (pre, 1024, 1024), (pre, 1024, 2048), (bias, 1024, 0), (bias, 1024, 1024), (bias, 1024, 2048),
               x, target, w["ln_gain"], w["ln_bias"], ("whole", w_branch), ("whole", w["w_out"]), ("whole", w["w_merge_gate"])],
        [(3, 512, BF16), (1024, BF16), (1024, F32), (1024, BF16), ("sum", 1024), ("sum", 1024), ("sum", LANES),
         (3072, BF16), ("sum", 3072), (3, 1024, BF16)] + [(512, F32)] * 3 + [(512, BF16)] * 3, name="head", rows=s, tr=256)
    d_o, d_gate = dict(zip(branches, rest[:3], strict=True)), dict(zip(branches, rest[3:], strict=True))

    grads["w_out"] = _mm(merged, dzb, "tn", name="g_w_out", out_dtype=BF16)
    grads["w_merge_gate"] = _mm(xb, dpre, "tn", name="g_w_merge", out_dtype=BF16)
    g_w_branch = _mm(u, dy, "tn", name="g_w_branch", out_dtype=BF16)
    for n, br in enumerate(branches):
        grads[f"w_branch_{br}"] = g_w_branch[n]
    (sent,) = emit({n: grads[n] for n in ("w_out", "w_merge_gate", "w_branch_mla", "w_branch_sb", "w_branch_mem")})

    dq_m, dk_m, dv_m = _softmax_bwd(qkv, mem_kv, mem_kv, o_m, d_o["mem"], lse_m, sent, hp=1, causal=False, dq_scale=scale_m,
                                    name="mem_bwd", q0=1536, v0=512, q_rows=MEM_Q_BLOCK)
    grads["w_mem_kv"] = _mm(memb, jnp.concatenate([dk_m, dv_m], axis=1), "tn", name="g_w_mem_kv", out_dtype=BF16)

    dq_sb, dk_sb, dv_sb = _sb_bwd(qkv, d_o["sb"], sb_saved, sent, q0=COL_QB, k0=COL_KB, v0=COL_VB, dq_scale=scale_b, name="sb_bwd")

    dq_mla, dk_mla, dv_a = _softmax_bwd(q_mla, k_mla, v_a, o_a, d_o["mla"], lse_a, sent, hp=2, causal=True, dq_scale=scale_a,
                                        name="mla_bwd")

    def mla_inputs_bwd(dq, dk, dv, n_q, n_kv, c_q, c_kv, g_q, g_kv, w_q, w_kv, cos_q, sin_q, cos_k, sin_k):
        dq_a = jnp.concatenate([g * cos_q + _swap_halves(g, MLA_NOPE) * sin_q for g in _lane_groups(dq)], axis=1).astype(BF16)
        groups = _lane_groups(dk)
        g_rope = groups[0]
        for other in groups[1:]:
            g_rope = g_rope + other
        dk_rope = pltpu.roll(g_rope * cos_k + _swap_halves(g_rope, MLA_NOPE) * sin_k, MLA_NOPE, axis=1)
        nope = _low_half(g_rope.shape)
        dkv_a = jnp.concatenate([jnp.where(nope, grp, 0.0) for grp in groups] + [dv], axis=1).astype(BF16)
        res = []
        for c, dn, g in ((c_q, _dot(dq_a, w_q, "nt"), g_q), (c_kv, _dot(dkv_a, w_kv, "nt"), g_kv)):
            r = lax.rsqrt(jnp.mean(c * c, axis=1, keepdims=True) + RMS_EPS)
            t = dn * g
            res += [r * t - c * (r * r * r) * jnp.mean(c * t, axis=1, keepdims=True), _colsum(dn * c * r)]
        return *res, dk_rope, _dot(n_q, dq_a, "tn"), _dot(n_kv, dkv_a, "tn")

    dc_q, grads["q_a_gain"], dc_kv, grads["kv_a_gain"], dk_rope, g_w_q_b, g_w_kv_b = _rowwise(
        mla_inputs_bwd, [dq_mla, dk_mla, dv_a, n_q, n_kv, (proj, 256, COL_CQ), (proj, 128, COL_CKV), w["q_a_gain"], w["kv_a_gain"],
                         ("whole", w["w_q_b"]), ("whole", w["w_kv_b"]), rope["cos_q"], rope["sin_k64_t"], rope["cos_k64"], rope["sin_k64_t"]],
        [(256, BF16), ("sum", 256), (128, BF16), ("sum", 128), (128, BF16), ("sum", (MLA_Q_LORA, 1024)), ("sum", (MLA_KV_LORA, 1536))],
        name="mla_inputs_bwd", rows=s)
    grads["w_q_b"], grads["w_kv_b"] = g_w_q_b.astype(BF16), g_w_kv_b.astype(BF16)

    sent = emit({n: grads[n] for n in ("w_mem_kv", "w_q_b", "w_kv_b")})

    dproj = jnp.concatenate(
        [dc_q, dc_kv, dk_rope, d_gate["mla"], d_gate["sb"], d_gate["mem"], dq_sb.astype(BF16), dk_sb.astype(BF16),
         dv_sb.astype(BF16), dq_m.astype(BF16)], axis=1)
    grads["w_in"] = _mm(xb, dproj, "tn", name="g_w_in", out_dtype=BF16, behind=sent)
    sent = emit({"w_in": grads["w_in"]})
    grad_x = _mm(dproj, w["w_in"], "nt", name="grad_x", add=dx, behind=sent)
    return loss, grad_x, grads


def _shard_shape(shape, axis):
    return tuple(d // N_DEV if a == axis else d for a, d in enumerate(shape))


def _from_blocks(blocks, name):
    shape, axis = SHARDED[name]
    return blocks.reshape(shape) if axis == 0 else blocks.transpose(1, 0, 2).reshape(shape)


def _to_blocks(full, name):
    shape, axis = SHARDED[name]
    shp = _shard_shape(shape, axis)
    return full.reshape(N_DEV, *shp) if axis == 0 else full.reshape(shape[0], N_DEV, shp[1]).transpose(1, 0, 2)


def _pad_heads(a, used):
    rows = a.shape[0]
    a = a.reshape(rows, MLA_HEADS, used)
    return jnp.concatenate([a, jnp.zeros((rows, MLA_HEADS, LANES - used), a.dtype)], axis=2).reshape(rows, MLA_HEADS * LANES)


def _to_kernel_layout(name, full):
    if name == "w_in":
        return jnp.concatenate([jnp.zeros((D_MODEL, IN_PAD), full.dtype) if piece is None else full[:, piece[0]:piece[0] + piece[1]]
                                for piece in IN_PIECES], axis=1)
    if name == "w_q_b":
        return _pad_heads(full, MLA_NOPE + MLA_ROPE)
    if name == "w_kv_b":
        kv = full.reshape(MLA_KV_LORA, MLA_HEADS, MLA_NOPE + MLA_V)
        return jnp.concatenate([_pad_heads(kv[:, :, :MLA_NOPE].reshape(MLA_KV_LORA, -1), MLA_NOPE),
                                kv[:, :, MLA_NOPE:].reshape(MLA_KV_LORA, -1)], axis=1)
    return full


def _from_kernel_layout(name, g):
    if name == "w_in":
        placed, at = [], 0
        for piece in IN_PIECES:
            if piece is not None:
                placed.append((piece[0], g[:, at:at + piece[1]]))
            at += IN_PAD if piece is None else piece[1]
        return jnp.concatenate([cols for _, cols in sorted(placed, key=lambda item: item[0])], axis=1)
    if name == "w_q_b":
        return g.reshape(MLA_Q_LORA, MLA_HEADS, LANES)[:, :, :MLA_NOPE + MLA_ROPE].reshape(MLA_Q_LORA, -1)
    if name == "w_kv_b":
        return jnp.concatenate([g[:, :1024].reshape(MLA_KV_LORA, MLA_HEADS, LANES)[:, :, :MLA_NOPE],
                                g[:, 1024:].reshape(MLA_KV_LORA, MLA_HEADS, MLA_V)], axis=2).reshape(MLA_KV_LORA, -1)
    return g


def _pack_small(vectors, loss=None):
    flat = [v.reshape(-1) for v in vectors]
    flat.append(jnp.zeros((SMALL_ROWS * SMALL_LANES - LOSS_INDEX,), F32) if loss is None else
                jnp.concatenate([loss.reshape(-1)[:1], jnp.zeros((SMALL_ROWS * SMALL_LANES - LOSS_INDEX - 1,), F32)]))
    return jnp.concatenate(flat).reshape(SMALL_ROWS, SMALL_LANES)


def _unpack_small(packed):
    flat, res, off = packed.reshape(-1), [], 0
    for _, n in SMALL:
        res.append(flat[off:off + n].reshape(1, n))
        off += n
    return res


def _me_and_peers():
    x, y, c = lax.axis_index("x"), lax.axis_index("y"), lax.axis_index("c")
    peers = []
    for kk in range(1, N_DEV):
        px, py, pc = (x + (kk >> 2)) % 2, (y + ((kk >> 1) & 1)) % 2, (c + (kk & 1)) % 2
        peers.append(((px, py, pc), 4 * px + 2 * py + pc))
    return 4 * x + 2 * y + c, peers


def _share_small(small, *, name):
    def body(small_ref, all_ref, send_sems, recv_sems, local_sem):
        me, peers = _me_and_peers()
        copies = [pltpu.make_async_remote_copy(src_ref=small_ref, dst_ref=all_ref.at[me], send_sem=send_sems.at[kk], recv_sem=recv_sems.at[kk],
                                               device_id=pos, device_id_type=pl.DeviceIdType.MESH) for kk, (pos, _) in enumerate(peers)]
        copies.append(pltpu.make_async_copy(small_ref, all_ref.at[me], local_sem))
        for cp in copies:
            cp.start()
        for cp in copies:
            cp.wait()

    hbm = pl.BlockSpec(memory_space=pl.ANY)
    return pl.pallas_call(
        body, name=name, in_specs=[hbm], out_specs=hbm, out_shape=jax.ShapeDtypeStruct((N_DEV, *small.shape), small.dtype),
        scratch_shapes=[pltpu.SemaphoreType.DMA((N_DEV - 1,)), pltpu.SemaphoreType.DMA((N_DEV - 1,)), pltpu.SemaphoreType.DMA],
        compiler_params=pltpu.CompilerParams(has_side_effects=True),
    )(small)


_HBM = pl.BlockSpec(memory_space=pltpu.HBM)
_SEM = pl.BlockSpec(memory_space=pltpu.SEMAPHORE)


def _exchange_copies(srcs, zones, send_sems, recv_sems, gather):
    me, peers = _me_and_peers()
    return [pltpu.make_async_remote_copy(
        src_ref=srcs[t] if gather else srcs[t].at[peer], dst_ref=zones[t].at[me], send_sem=send_sems.at[7 * t + kk],
        recv_sem=recv_sems.at[7 * t + kk], device_id=pos, device_id_type=pl.DeviceIdType.MESH)
        for t in range(len(srcs)) for kk, (pos, peer) in enumerate(peers)]


def _exchange_start(tensors, *, gather, name):
    n = len(tensors)
    zones = [lax.empty((N_DEV, *(t.shape if gather else t.shape[1:])), t.dtype) for t in tensors]

    def body(*refs):
        for cp in _exchange_copies(refs[:n], refs[n:2 * n], refs[2 * n], refs[2 * n + 1], gather):
            cp.start()
        refs[-1][...] = jnp.zeros_like(refs[-1])

    buffers = [pltpu.HBM(a.shape, a.dtype) for a in tensors + zones]
    res = pl.pallas_call(
        body, name=name, in_specs=[_HBM] * (2 * n),
        out_shape=(pltpu.SemaphoreType.DMA((7 * n,)), pltpu.SemaphoreType.DMA((7 * n,)), *buffers, jax.ShapeDtypeStruct((8, LANES), F32)),
        out_specs=(_SEM, _SEM, *[_HBM] * (2 * n), pl.BlockSpec(memory_space=pltpu.VMEM)),
        input_output_aliases={i: 2 + i for i in range(2 * n)},
        compiler_params=pltpu.CompilerParams(has_side_effects=pltpu.SideEffectType.DATAFLOW_SIDE_EFFECTING),
    )(*[pltpu.with_memory_space_constraint(a, pltpu.HBM) for a in tensors + zones])
    return dict(sems=res[:2], buffers=res[2:2 + 2 * n], gather=gather, started=res[-1])


def _exchange_wait(started, after, *, name):
    n = len(started["buffers"]) // 2

    def body(*refs):
        for cp in _exchange_copies(refs[:n], refs[n:2 * n], refs[2 * n], refs[2 * n + 1], started["gather"]):
            cp.wait_send()
            cp.wait_recv()

    res = pl.pallas_call(
        body, name=name, in_specs=[_HBM] * (2 * n) + [_SEM, _SEM] + [_ANY] * len(after),
        out_shape=tuple(pltpu.HBM(a.shape, a.dtype) for a in started["buffers"]), out_specs=tuple([_HBM] * (2 * n)),
        input_output_aliases={i: i for i in range(2 * n)},
        compiler_params=pltpu.CompilerParams(has_side_effects=pltpu.SideEffectType.DATAFLOW_SIDE_EFFECTING),
    )(*started["buffers"], *started["sems"], *after)
    return res[:n], res[n:]


def _adamw(contrib, w, m, v, *, name):
    rows, cols = w.shape
    tile = min(rows, ADAM_ROWS)

    def body(c_ref, w_ref, m_ref, v_ref, g_ref, d_ref, nm_ref, nv_ref):
        g = c_ref[0].astype(F32)
        for s in range(1, N_DEV):
            g = g + c_ref[s].astype(F32)
        m_new = ADAM_B1 * m_ref[...] + (1.0 - ADAM_B1) * g
        v_new = ADAM_B2 * v_ref[...] + (1.0 - ADAM_B2) * (g * g)
        m_hat = m_new / (1.0 - ADAM_B1 ** ADAM_STEP)
        v_hat = v_new / (1.0 - ADAM_B2 ** ADAM_STEP)
        g_ref[...] = g
        d_ref[...] = -ADAM_LR * (m_hat / (jnp.sqrt(v_hat) + ADAM_EPS) + ADAM_WD * w_ref[...])
        nm_ref[...] = m_new
        nv_ref[...] = v_new

    spec = pl.BlockSpec((tile, cols), lambda i: (i, 0))
    return pl.pallas_call(
        body, name=name, grid=(rows // tile,),
        in_specs=[pl.BlockSpec((N_DEV, tile, cols), lambda i: (0, i, 0)), spec, spec, spec], out_specs=[spec] * 4,
        out_shape=[jax.ShapeDtypeStruct((rows, cols), F32)] * 4, compiler_params=_cparams("parallel"),
    )(contrib, w, m, v)


class _Weights:
    def __init__(self, gathers, vectors, me):
        self.gathers, self.ready, self.me, self.after = gathers, dict(vectors), me, ()

    def arrive_after(self, *values):
        self.after = values

    def __getitem__(self, name):
        if name not in self.ready:
            gi = next(i for i, group in enumerate(GATHER_GROUPS) if name in group)
            after = [*self.after, *[g["started"] for g in self.gathers]]
            shards, zones = _exchange_wait(self.gathers[gi], after, name=f"gather_wait_{gi}")
            for n, shard, zone in zip(GATHER_GROUPS[gi], shards, zones, strict=True):
                blocks = lax.dynamic_update_slice_in_dim(zone, shard[None], self.me, 0)
                self.ready[n] = _to_kernel_layout(n, _from_blocks(blocks, n))
        return self.ready[name]


def kernel(x, mem, w_in, w_mem_kv, q_a_gain, w_q_b, kv_a_gain, w_kv_b, w_branch_mla, w_branch_sb, w_branch_mem, w_merge_gate, b_merge_gate, w_out, ln_gain, ln_bias, loss_target, m_w_in, m_w_mem_kv, m_q_a_gain, m_w_q_b, m_kv_a_gain, m_w_kv_b, m_w_branch_mla, m_w_branch_sb, m_w_branch_mem, m_w_merge_gate, m_b_merge_gate, m_w_out, m_ln_gain, m_ln_bias, v_w_in, v_w_mem_kv, v_q_a_gain, v_w_q_b, v_kv_a_gain, v_w_kv_b, v_w_branch_mla, v_w_branch_sb, v_w_branch_mem, v_w_merge_gate, v_b_merge_gate, v_w_out, v_ln_gain, v_ln_bias):
    given = dict(locals())
    small_names = [n for n, _ in SMALL]
    smalls = lambda prefix: [given[prefix + n] for n in small_names]
    me = 4 * lax.axis_index("x") + 2 * lax.axis_index("y") + lax.axis_index("c")

    gathers = [_exchange_start([given[n][0].astype(BF16) for n in group], gather=True, name=f"gather_start_{gi}")
               for gi, group in enumerate(GATHER_GROUPS)]
    w = _Weights(gathers, {n: given[n] for n in small_names}, me)
    exchanges = []
    results = [{}, {}, {}, {}]

    def finish(gi, after):
        names, started = exchanges[gi]
        sent, zones = _exchange_wait(started, after, name=f"grads_wait_{gi}")
        done = []
        for n, blocks, zone in zip(names, sent, zones, strict=True):
            own = lax.dynamic_index_in_dim(blocks, me, 0, keepdims=True)
            contrib = lax.dynamic_update_slice_in_dim(zone, own, me, 0)
            outs = _adamw(contrib, given[n][0], given["m_" + n][0], given["v_" + n][0], name=f"adamw_{n}")
            for kind, res in zip(results, outs, strict=True):
                kind[n] = res[None]
            done.append(outs[1])
        return done

    def emit(grads):
        blocks = [_to_blocks(_from_kernel_layout(n, g), n).astype(BF16) for n, g in grads.items()]
        exchanges.append((tuple(grads), _exchange_start(blocks, gather=False, name=f"grads_start_{len(exchanges)}")))
        started = [exchanges[-1][1]["started"]]
        if len(exchanges) == len(GRAD_GROUPS):
            for gi in range(len(GRAD_GROUPS) - 1):
                started += finish(gi, started[:1])
        return started

    loss, grad_x, grads = _local_step(x[0], mem[0], loss_target[0], w, emit)

    contrib_small = _share_small(_pack_small([grads[n] for n in small_names], loss), name="share_small")
    sml = _adamw(contrib_small, _pack_small(smalls("")), _pack_small(smalls("m_")), _pack_small(smalls("v_")), name="adamw_small")
    for kind, packed in zip(results, sml, strict=True):
        kind.update(zip(small_names, _unpack_small(packed), strict=True))
    finish(len(GRAD_GROUPS) - 1, [grad_x])
    order = ["w_in", "w_mem_kv", "q_a_gain", "w_q_b", "kv_a_gain", "w_kv_b", "w_branch_mla", "w_branch_sb", "w_branch_mem",
             "w_merge_gate", "b_merge_gate", "w_out", "ln_gain", "ln_bias"]
    loss_out = sml[0].reshape(-1)[LOSS_INDEX]
    return (loss_out, grad_x[None], *[kind[n] for kind in results for n in order])
```

```python
import math

import jax
import jax.numpy as jnp
from jax import lax
from jax.experimental import pallas as pl
from jax.experimental.pallas import tpu as pltpu

F32, BF16 = jnp.float32, jnp.bfloat16

N_DEV = 8
D_MODEL = 1024
MLA_HEADS, MLA_NOPE, MLA_ROPE, MLA_V = 8, 64, 32, 64
MLA_Q_LORA, MLA_KV_LORA = 256, 128
SB_HEAD_DIM = 64
MEM_HEAD_DIM = 128
ROPE_BASE = 10000.0
RMS_EPS = 1e-6
LN_EPS = 1e-5
DEEPNORM_ALPHA = 2.0 ** 0.25
ADAM_LR, ADAM_B1, ADAM_B2, ADAM_EPS, ADAM_WD, ADAM_STEP = 0.001, 0.9, 0.999, 1e-08, 0.01, 10
LOG2E, LN2 = math.log2(math.e), math.log(2.0)

LANES = 128
GROUPS = 4
PROJ_WIDTH = 4096
COL_CQ, COL_CKV, COL_KROPE, COL_GATE_A, COL_GATE_B, COL_GATE_M = 0, 256, 384, 512, 1024, 1536
QKV_FIRST, QKV_WIDTH = 2048, 2048
COL_QB, COL_KB, COL_VB, COL_QM = 0, 512, 1024, 1536
IN_PIECES = ((0, 416), None, (416, 512), (2464, 512), (3488, 512), (928, 512), (1440, 512), (1952, 512), (2976, 512))
IN_PAD = 96

VMEM_LIMIT_BYTES = 56 * 1024 * 1024
NEG_BIG = -1e30
Q_BLOCK = 512
MEM_Q_BLOCK = 2048
SB_BWD_Q_BLOCK = 512
TRI_BLOCK = 256
TILE_ROWS = 64
KEY_CHUNK = 512

SHARDED = {
    "w_in": ((1024, 4000), 1), "w_mem_kv": ((1024, 1024), 0), "w_q_b": ((256, 768), 1), "w_kv_b": ((128, 1024), 1),
    "w_branch_mla": ((512, 1024), 1), "w_branch_sb": ((512, 1024), 1), "w_branch_mem": ((512, 1024), 1),
    "w_merge_gate": ((1024, 3072), 1), "w_out": ((1024, 1024), 0),
}
GATHER_GROUPS = (("w_in",), ("w_merge_gate",), ("w_q_b", "w_kv_b", "w_mem_kv", "w_branch_mla", "w_branch_sb", "w_branch_mem", "w_out"))
GRAD_GROUPS = (("w_out", "w_merge_gate", "w_branch_mla", "w_branch_sb", "w_branch_mem"), ("w_mem_kv", "w_q_b", "w_kv_b"), ("w_in",))
SMALL = (("q_a_gain", 256), ("kv_a_gain", 128), ("b_merge_gate", 3072), ("ln_gain", 1024), ("ln_bias", 1024))
SMALL_ROWS, SMALL_LANES = 48, 128
ADAM_ROWS = 256
LOSS_INDEX = 5504


def _cparams(*sem):
    return pltpu.CompilerParams(dimension_semantics=sem or None, vmem_limit_bytes=VMEM_LIMIT_BYTES)


_DIMS = {"nn": (((1,), (0,)), ((), ())), "nt": (((1,), (1,)), ((), ())), "tn": (((0,), (0,)), ((), ()))}


def _dot(a, b, dims):
    return lax.dot_general(a, b, _DIMS[dims], preferred_element_type=F32)


def _tile(dim, want):
    if dim <= want:
        return dim
    t = want - want % LANES
    while dim % t:
        t -= LANES
    assert t > 0, (dim, want)
    return t


_ANY = pl.BlockSpec(memory_space=pl.ANY)


def _mm(a, b, dims, *, name, out_dtype=F32, add=None, add_scale=1.0, col_scale=None, b_cols=None, behind=None,
        tm=1024, tn=1024, tk=1024):
    batch = a.shape[0] if a.ndim == 3 else None
    if dims == "nn":
        (m, k), (k2, n) = a.shape[-2:], b.shape[-2:]
    elif dims == "nt":
        (m, k), (n, k2) = a.shape[-2:], b.shape[-2:]
    else:
        (k, m), (k2, n) = a.shape[-2:], b.shape[-2:]
    assert k == k2 and a.ndim == b.ndim, (a.shape, b.shape, dims)
    assert batch is None or (b.shape[0] == batch and add is None and col_scale is None and b_cols is None)
    b_first = 0
    if b_cols is not None:
        assert dims == "nn"
        b_first, n = b_cols
    tm, tn, tk = _tile(m, tm), _tile(n, tn), _tile(k, tk)
    assert b_first % tn == 0
    jb = b_first // tn
    nk = k // tk

    def spec(block, index):
        if batch is None:
            return pl.BlockSpec(block, lambda bb, i, j, kk: index(i, j, kk))
        return pl.BlockSpec((None, *block), lambda bb, i, j, kk: (bb, *index(i, j, kk)))

    a_spec = spec((tk, tm), lambda i, j, kk: (kk, i)) if dims == "tn" else spec((tm, tk), lambda i, j, kk: (i, kk))
    b_spec = spec((tn, tk), lambda i, j, kk: (j, kk)) if dims == "nt" else spec((tk, tn), lambda i, j, kk: (kk, jb + j))
    o_spec = spec((tm, tn), lambda i, j, kk: (i, j))
    behind = [] if behind is None else behind if isinstance(behind, (list, tuple)) else [behind]
    optional = [(add, o_spec), (col_scale, pl.BlockSpec((1, tn), lambda bb, i, j, kk: (0, j))), *[(v, _ANY) for v in behind]]
    present = [(v, spec) for v, spec in optional if v is not None]

    def body(*refs):
        a_ref, b_ref = refs[:2]
        extra = iter(refs[2:2 + len(present)])
        add_ref = next(extra) if add is not None else None
        scale_ref = next(extra) if col_scale is not None else None
        o_ref = refs[2 + len(present)]
        part = _dot(a_ref[...].astype(BF16), b_ref[...].astype(BF16), dims)

        def finish(r):
            if add is not None:
                r = r + add_scale * add_ref[...]
            if col_scale is not None:
                r = r * scale_ref[...]
            o_ref[...] = r.astype(out_dtype)

        if nk == 1:
            finish(part)
            return
        acc = refs[-1]
        kk = pl.program_id(3)

        @pl.when(kk == 0)
        def _():
            acc[...] = part

        @pl.when(kk > 0)
        def _():
            acc[...] += part

        @pl.when(kk == nk - 1)
        def _():
            finish(acc[...])

    return pl.pallas_call(
        body, name=name, grid=(batch or 1, m // tm, n // tn, nk),
        in_specs=[a_spec, b_spec] + [spec for _, spec in present], out_specs=o_spec,
        out_shape=jax.ShapeDtypeStruct((m, n) if batch is None else (batch, m, n), out_dtype),
        scratch_shapes=[pltpu.VMEM((tm, tn), F32)] if nk > 1 else [],
        compiler_params=_cparams("parallel", "parallel", "parallel", "arbitrary"),
    )(a, b, *[v for v, _ in present])


def _rowwise(fn, ins, outs, *, name, rows, tr=512):
    n_in = len(ins)
    tr = min(tr, rows)
    in_specs, args = [], []
    for it in ins:
        if isinstance(it, tuple) and it[0] == "whole":
            in_specs.append(pl.BlockSpec(it[1].shape, lambda i, nd=it[1].ndim: (0,) * nd))
            args.append(it[1])
            continue
        arr, w, off = it if isinstance(it, tuple) else (it, it.shape[-1], 0)
        assert off % w == 0
        cb = off // w
        if arr.ndim == 3:
            in_specs.append(pl.BlockSpec((arr.shape[0], tr, w), lambda i, cb=cb: (0, i, cb)))
        elif arr.shape[0] == 1:
            in_specs.append(pl.BlockSpec((1, w), lambda i, cb=cb: (0, cb)))
        else:
            in_specs.append(pl.BlockSpec((tr, w), lambda i, cb=cb: (i, cb)))
        args.append(arr)
    out_shape, out_specs, is_sum = [], [], []
    for out in outs:
        is_sum.append(out[0] == "sum")
        if out[0] == "sum":
            shape = out[1] if isinstance(out[1], tuple) else (1, out[1])
            out_shape.append(jax.ShapeDtypeStruct(shape, F32))
            out_specs.append(pl.BlockSpec(shape, lambda i: (0, 0)))
        elif len(out) == 3:
            out_shape.append(jax.ShapeDtypeStruct((out[0], rows, out[1]), out[2]))
            out_specs.append(pl.BlockSpec((out[0], tr, out[1]), lambda i: (0, i, 0)))
        else:
            out_shape.append(jax.ShapeDtypeStruct((rows, out[0]), out[1]))
            out_specs.append(pl.BlockSpec((tr, out[0]), lambda i: (i, 0)))

    def body(*refs):
        res = fn(*[r[...] for r in refs[:n_in]])
        for r, val, s in zip(refs[n_in:], res, is_sum, strict=True):
            if s:
                @pl.when(pl.program_id(0) == 0)
                def _(r=r):
                    r[...] = jnp.zeros_like(r)

                r[...] += val
            elif isinstance(val, (list, tuple)):
                for n, part in enumerate(val):
                    r[n] = part.astype(r.dtype)
            else:
                r[...] = val.astype(r.dtype)

    return pl.pallas_call(
        body, name=name, grid=(rows // tr,), in_specs=in_specs, out_specs=out_specs, out_shape=out_shape,
        compiler_params=_cparams("arbitrary"),
    )(*args)


def _colsum(v):
    return jnp.sum(v, axis=0, keepdims=True)


def _sigmoid(v):
    return 1.0 / (1.0 + jnp.exp(-v))


def _lane_groups(v):
    return [v[:, g * LANES:(g + 1) * LANES] for g in range(v.shape[1] // LANES)]


def _swap_halves(v, first_lane):
    lane = lax.broadcasted_iota(jnp.int32, v.shape, 1)
    return jnp.where(lane < first_lane + 16, pltpu.roll(v, 112, axis=1), pltpu.roll(v, 16, axis=1))


def _lane_sum(acc, v):
    for part in _lane_groups(v):
        acc = acc + part
    return acc


def _low_half(shape):
    return lax.broadcasted_iota(jnp.int32, shape, 1) < LANES // 2


def _select_heads(per_head, pick):
    if len(per_head) == 1:
        return pick(per_head[0], 0)
    return jnp.where(_low_half(per_head[0].shape), pick(per_head[0], 0), pick(per_head[1], 1))


def _attn_specs(s, sk, hp, bq, q0, k0, v0):
    wq = hp * LANES
    assert q0 % wq == 0 and k0 % wq == 0 and v0 % LANES == 0
    qb0, kb0, vb0 = q0 // wq, k0 // wq, v0 // LANES
    q_spec = pl.BlockSpec((bq, wq), lambda g, i: (i, qb0 + g))
    k_spec = pl.BlockSpec((sk, wq), lambda g, i: (0, kb0 + g))
    v_spec = pl.BlockSpec((sk, LANES), lambda g, i: (0, vb0 + g))
    row_out = lambda w: pl.BlockSpec((bq, w), lambda g, i: (i, g))
    key_out = lambda w: pl.BlockSpec((sk, w), lambda g, i: (0, g))
    return q_spec, k_spec, v_spec, row_out, key_out


def _chunks(i, bq, ch, sk, causal):
    return ((i + 1) * bq - 1) // ch if causal else jnp.int32(sk // ch - 1)


def _positions(i, c, bq, ch):
    return (i * bq + lax.broadcasted_iota(jnp.int32, (bq, ch), 0), c * ch + lax.broadcasted_iota(jnp.int32, (bq, ch), 1))


def _softmax_fwd(q, k, v, *, hp, causal, name, q0=0, k0=0, v0=0, q_rows=Q_BLOCK):
    s, sk = q.shape[0], k.shape[0]
    bq, ch = min(q_rows, s), min(KEY_CHUNK, sk)
    assert not causal or bq <= ch
    q_spec, k_spec, v_spec, row_out, _ = _attn_specs(s, sk, hp, bq, q0, k0, v0)

    def body(q_ref, k_ref, v_ref, o_ref, lse_ref, s_scr):
        i = pl.program_id(1)
        qs = _lane_groups(q_ref[...])
        last = _chunks(i, bq, ch, sk, causal)

        def scores(c, ms, masked):
            off = pl.multiple_of(c * ch, ch)
            out = []
            for j in range(hp):
                sc = _dot(qs[j], k_ref[pl.ds(off, ch), j * LANES:(j + 1) * LANES], "nt")
                if masked:
                    qpos, kpos = _positions(i, c, bq, ch)
                    sc = jnp.where(kpos <= qpos, sc, NEG_BIG)
                s_scr[j, c] = sc
                m = ms[j]
                for part in _lane_groups(sc):
                    m = jnp.maximum(m, part)
                out.append(m)
            return tuple(out)

        ms = lax.fori_loop(0, last, lambda c, m: scores(c, m, False), tuple(jnp.full((bq, LANES), NEG_BIG, F32) for _ in range(hp)))
        ms = scores(last, ms, causal)
        row_max = [jnp.max(m, axis=1, keepdims=True) for m in ms]

        def weigh(c, carry):
            off = pl.multiple_of(c * ch, ch)
            vt = v_ref[pl.ds(off, ch), :]
            out = []
            for j in range(hp):
                l, acc = carry[j]
                p = jnp.exp2(s_scr[j, c] - row_max[j])
                out.append((_lane_sum(l, p), acc + _dot(p.astype(BF16), vt, "nn")))
            return tuple(out)

        zero = jnp.zeros((bq, LANES), F32)
        res = lax.fori_loop(0, last + 1, weigh, tuple((zero, zero) for _ in range(hp)))
        row_sum = [jnp.sum(l, axis=1, keepdims=True) for l, _ in res]
        o_ref[...] = _select_heads([acc for _, acc in res], lambda acc, j: acc / row_sum[j])
        lse_ref[...] = _select_heads([jnp.broadcast_to(row_max[j] + jnp.log2(row_sum[j]), (bq, LANES)) for j in range(hp)], lambda a, j: a)

    return pl.pallas_call(
        body, name=name, grid=(GROUPS, s // bq), in_specs=[q_spec, k_spec, v_spec], out_specs=[row_out(LANES), row_out(LANES)],
        out_shape=[jax.ShapeDtypeStruct((s, GROUPS * LANES), F32)] * 2,
        scratch_shapes=[pltpu.VMEM((hp, sk // ch, bq, ch), F32)], compiler_params=_cparams("parallel", "arbitrary"),
    )(q, k, v)


def _head_cotangent(do, j, hp):
    if hp == 1:
        return do
    return jnp.where(_low_half(do.shape) == (j == 0), do, 0.0)


def _softmax_bwd(q, k, v, o, do, lse, behind, *, hp, causal, dq_scale, name, q0=0, k0=0, v0=0, q_rows=Q_BLOCK):
    s, sk = q.shape[0], k.shape[0]
    bq, ch = min(q_rows, s), min(KEY_CHUNK, sk)
    assert not causal or bq <= ch
    wq = hp * LANES
    q_spec, k_spec, v_spec, row_out, key_out = _attn_specs(s, sk, hp, bq, q0, k0, v0)

    def body(q_ref, k_ref, v_ref, o_ref, do_ref, lse_ref, _, dq_ref, dk_ref, dv_ref, dk_t, dv_t):
        i = pl.program_id(1)

        @pl.when(i == 0)
        def _():
            dk_t[...] = jnp.zeros_like(dk_t)
            dv_t[...] = jnp.zeros_like(dv_t)

        qs = _lane_groups(q_ref[...])
        do_all, o_all, lse_all = do_ref[...], o_ref[...], lse_ref[...]
        dos, deltas, lses = [], [], []
        for j in range(hp):
            d = _head_cotangent(do_all, j, hp)
            deltas.append(jnp.sum(d * o_all, axis=1, keepdims=True))
            dos.append(d.astype(BF16))
            lses.append(lse_all[:, j * (LANES // hp):j * (LANES // hp) + 1])
        last = _chunks(i, bq, ch, sk, causal)

        def chunk(c, dqs, masked):
            off = pl.multiple_of(c * ch, ch)
            vt = v_ref[pl.ds(off, ch), :]
            out, dks, dv = [], [], None
            for j in range(hp):
                kt = k_ref[pl.ds(off, ch), j * LANES:(j + 1) * LANES]
                p = jnp.exp2(_dot(qs[j], kt, "nt") - lses[j])
                if masked:
                    qpos, kpos = _positions(i, c, bq, ch)
                    p = jnp.where(kpos <= qpos, p, 0.0)
                ds = (p * (_dot(dos[j], vt, "nt") - deltas[j]) * LN2).astype(BF16)
                out.append(dqs[j] + _dot(ds, kt, "nn"))
                dks.append(_dot(qs[j], ds, "tn"))
                dvj = _dot(dos[j], p.astype(BF16), "tn")
                dv = dvj if dv is None else dv + dvj
            dk_t[c] += dks[0] if hp == 1 else jnp.concatenate(dks, axis=0)
            dv_t[c] += dv
            return tuple(out)

        dqs = lax.fori_loop(0, last, lambda c, d: chunk(c, d, False), tuple(jnp.zeros((bq, LANES), F32) for _ in range(hp)))
        dqs = chunk(last, dqs, causal)
        dq_ref[...] = (dqs[0] if hp == 1 else jnp.concatenate(dqs, axis=1)) * dq_scale

        @pl.when(i == s // bq - 1)
        def _():
            for c in range(sk // ch):
                dk_ref[c * ch:(c + 1) * ch, :] = dk_t[c].T
                dv_ref[c * ch:(c + 1) * ch, :] = dv_t[c].T

    return pl.pallas_call(
        body, name=name, grid=(GROUPS, s // bq),
        in_specs=[q_spec, k_spec, v_spec, row_out(LANES), row_out(LANES), row_out(LANES), _ANY],
        out_specs=[row_out(wq), key_out(wq), key_out(LANES)],
        out_shape=[jax.ShapeDtypeStruct((s, GROUPS * wq), F32), jax.ShapeDtypeStruct((sk, GROUPS * wq), F32),
                   jax.ShapeDtypeStruct((sk, GROUPS * LANES), F32)],
        scratch_shapes=[pltpu.VMEM((sk // ch, wq, ch), F32), pltpu.VMEM((sk // ch, LANES, ch), F32)],
        compiler_params=_cparams("arbitrary", "arbitrary"),
    )(q, k, v, o, do, lse, behind)


def _log2_sigmoid_pair(z2):
    minus_abs = lax.bitcast_convert_type(lax.bitcast_convert_type(z2, jnp.uint32) | jnp.uint32(0x80000000), F32)
    log_beta = jnp.minimum(z2, 0.0) - jnp.log2(1.0 + jnp.exp2(minus_abs))
    return log_beta, log_beta - z2


def _tilewise(fn, *arrays):
    rows, cols = arrays[0].shape
    step = min(TILE_ROWS, rows)
    grid = [[fn(*[None if a is None else a[r:r + step, c:c + LANES] for a in arrays]) for c in range(0, cols, LANES)]
            for r in range(0, rows, step)]
    return [jnp.concatenate([jnp.concatenate([cell[k] for cell in row], axis=1) for row in grid], axis=0)
            for k in range(len(grid[0][0]))]


def _split(v):
    hi = v.astype(BF16)
    return hi, (v - hi.astype(F32)).astype(BF16)


def _tri(n, after):
    rows, cols = lax.broadcasted_iota(jnp.int32, (n, n), 0), lax.broadcasted_iota(jnp.int32, (n, n), 1)
    return (rows > cols if after else rows < cols).astype(BF16)


def _running_sums(v, terms, start, tri, backwards):
    n = tri.shape[0]
    n_blocks = v.shape[1] // n
    order = range(n_blocks - 1, -1, -1) if backwards else range(n_blocks)
    stacked = tri if len(terms) == 1 else jnp.concatenate([tri] * len(terms), axis=0)
    parts, run = [None] * n_blocks, start
    for t in order:
        cols = slice(t * n, (t + 1) * n)
        lhs = terms[0][:, cols] if len(terms) == 1 else jnp.concatenate([term[:, cols] for term in terms], axis=1)
        parts[t] = _dot(lhs, stacked, "nn") + run
        run = run + jnp.sum(v[:, cols], axis=1, keepdims=True)
    return (parts[0] if n_blocks == 1 else jnp.concatenate(parts, axis=1)), run


def _sb_weights(qm, kt, run, tri, strict):
    def logs(z2, keep):
        log_beta, log_keep = _log2_sigmoid_pair(z2)
        if keep is not None:
            log_keep = jnp.where(keep, log_keep, 0.0)
        return log_beta, log_keep, *_split(log_keep)

    log_beta, log_keep, hi, lo = _tilewise(logs, _dot(qm, kt, "nt"), strict)
    behind, run = _running_sums(log_keep, (hi, lo), run, tri, True)

    def weigh(log_beta, behind, keep):
        a = jnp.exp2(log_beta + behind)
        return (a if keep is None else jnp.where(keep, a, 0.0),)

    (a,) = _tilewise(weigh, log_beta, behind, strict)
    return a, log_beta, run


COPY_PARTS = 4


def _row_parts(rows):
    part = rows // COPY_PARTS
    return [pl.ds(h * part, part) for h in range(COPY_PARTS)]


class _Copies:
    def __init__(self, copies):
        self.copies = copies

    def start(self):
        for cp in self.copies:
            cp.start()

    def wait(self):
        for cp in self.copies:
            cp.wait()


def _sb_queries(q_all):
    low = _low_half(q_all.shape)
    zero = jnp.zeros_like(q_all)
    return [jnp.where(low, q_all, zero), jnp.where(low, zero, q_all)]


def _sb_fwd(qkv, *, q0, k0, v0, name):
    s = qkv.shape[0]
    bq, ch = min(Q_BLOCK, s), min(KEY_CHUNK, s)
    assert bq == ch
    n_q = s // bq
    q_spec, k_spec, v_spec, row_out, _ = _attn_specs(s, s, 1, bq, q0, k0, v0)

    def body(q_ref, k_ref, v_ref, o_ref, saved_ref, stage, sems):
        g, i = pl.program_id(0), pl.program_id(1)
        qms = _sb_queries(q_ref[...])
        tri = _tri(min(TRI_BLOCK, ch), True)
        last = _chunks(i, bq, ch, s, True)
        first_tile = i * (i + 1) // 2

        def save(slot, c):
            return _Copies([pltpu.make_async_copy(stage.at[slot, p, rows], saved_ref.at[g, first_tile + c, p, rows], sems.at[slot, p, h])
                            for p in range(4) for h, rows in enumerate(_row_parts(bq))])

        def chunk(c, step, carry, masked):
            off = pl.multiple_of(c * ch, ch)
            kt, vt = k_ref[pl.ds(off, ch), :], v_ref[pl.ds(off, ch), :]
            strict = None
            if masked:
                qpos, kpos = _positions(i, c, bq, ch)
                strict = kpos < qpos
            slot = step % 3
            if not masked:
                @pl.when(step >= 3)
                def _():
                    save(slot, c).wait()
            out = []
            for j in range(2):
                run, acc = carry[j]
                a, log_beta, run = _sb_weights(qms[j], kt, run, tri, strict)
                a = a.astype(BF16)
                stage[slot, j] = a
                stage[slot, 2 + j] = jnp.exp2(log_beta).astype(BF16)
                out.append((run, acc + _dot(a, vt, "nn")))
            save(slot, c).start()
            return tuple(out)

        carry = chunk(last, 0, tuple((jnp.zeros((bq, 1), F32), jnp.zeros((bq, LANES), F32)) for _ in range(2)), True)
        res = lax.fori_loop(0, last, lambda n, c: chunk(last - 1 - n, n + 1, c, False), carry)
        for back in range(3):
            @pl.when(last >= back)
            def _(back=back):
                save((last - back) % 3, 0).wait()

        o_ref[...] = _select_heads([acc for _, acc in res], lambda acc, j: acc)

    return pl.pallas_call(
        body, name=name, grid=(GROUPS, n_q), in_specs=[q_spec, k_spec, v_spec], out_specs=[row_out(LANES), _ANY],
        out_shape=[jax.ShapeDtypeStruct((s, GROUPS * LANES), F32),
                   jax.ShapeDtypeStruct((GROUPS, n_q * (n_q + 1) // 2, 4, bq, ch), BF16)],
        scratch_shapes=[pltpu.VMEM((3, 4, bq, ch), BF16), pltpu.SemaphoreType.DMA((3, 4, COPY_PARTS))],
        compiler_params=_cparams("parallel", "arbitrary"),
    )(qkv, qkv, qkv)


def _sb_bwd(qkv, do, saved, behind, *, q0, k0, v0, dq_scale, name):
    s = qkv.shape[0]
    bq, ch = min(SB_BWD_Q_BLOCK, s), min(KEY_CHUNK, s)
    assert bq == ch and saved.shape[2:] == (4, bq, ch)
    q_spec, k_spec, v_spec, row_out, key_out = _attn_specs(s, s, 1, bq, q0, k0, v0)

    def body(q_ref, k_ref, v_ref, do_ref, saved_ref, _, dq_ref, dk_ref, dv_ref, g_s, beta_s, stage, sems):
        g_index, i = pl.program_id(0), pl.program_id(1)

        @pl.when(i == 0)
        def _():
            dk_ref[...] = jnp.zeros_like(dk_ref)
            dv_ref[...] = jnp.zeros_like(dv_ref)

        qms = _sb_queries(q_ref[...])
        do_all = do_ref[...]
        dos = [_head_cotangent(do_all, j, 2).astype(BF16) for j in range(2)]
        dos_ln2 = [(_head_cotangent(do_all, j, 2) * LN2).astype(BF16) for j in range(2)]
        tri_before = _tri(min(TRI_BLOCK, ch), False)
        last = _chunks(i, bq, ch, s, True)
        first_tile = i * (i + 1) // 2

        def strict_mask(c):
            qpos, kpos = _positions(i, c, bq, ch)
            return kpos < qpos

        def fetch(slot, c):
            return _Copies([pltpu.make_async_copy(saved_ref.at[g_index, first_tile + c, p, rows], stage.at[slot, p, rows], sems.at[slot, p, h])
                            for p in range(4) for h, rows in enumerate(_row_parts(bq))])

        def sweep1(n, unused):
            c, slot = last - n, n % 3

            @pl.when(c >= 2)
            def _():
                fetch((n + 2) % 3, c - 2).start()

            fetch(slot, c).wait()
            off = pl.multiple_of(c * ch, ch)
            vt = v_ref[pl.ds(off, ch), :]
            dv = None
            for j in range(2):
                a = stage[slot, j]
                g_s[j, c] = (a.astype(F32) * _dot(dos_ln2[j], vt, "nt")).astype(BF16)
                beta_s[j, c] = stage[slot, 2 + j]
                dvj = _dot(a, dos[j], "tn")
                dv = dvj if dv is None else dv + dvj
            dv_ref[pl.ds(off, ch), :] += dv
            return unused

        fetch(0, last).start()

        @pl.when(last >= 1)
        def _():
            fetch(1, last - 1).start()

        lax.fori_loop(0, last + 1, sweep1, 0)

        def sweep2(c, carry, masked):
            off = pl.multiple_of(c * ch, ch)
            kt = k_ref[pl.ds(off, ch), :]
            out, dk = [], None
            for j in range(2):
                before, dq = carry[j]
                g16, beta = g_s[j, c], beta_s[j, c].astype(F32)
                g = g16.astype(F32)
                in_front, before = _running_sums(g, (g16,), before, tri_before, False)
                dz = g * (1.0 - beta) - beta * in_front
                if masked:
                    dz = jnp.where(strict_mask(c), dz, 0.0)
                dz = dz.astype(BF16)
                dkj = _dot(dz, qms[j], "tn")
                dk = dkj if dk is None else dk + dkj
                out.append((before, dq + _dot(dz, kt, "nn")))
            dk_ref[pl.ds(off, ch), :] += dk
            return tuple(out)

        carry = lax.fori_loop(0, last, lambda c, cr: sweep2(c, cr, False),
                              tuple((jnp.zeros((bq, 1), F32), jnp.zeros((bq, LANES), F32)) for _ in range(2)))
        res = sweep2(last, carry, True)
        dq_ref[...] = _select_heads([dq for _, dq in res], lambda dq, j: dq) * dq_scale

    n_ch = s // ch
    return pl.pallas_call(
        body, name=name, grid=(GROUPS, s // bq), in_specs=[q_spec, k_spec, v_spec, row_out(LANES), _ANY, _ANY],
        out_specs=[row_out(LANES), key_out(LANES), key_out(LANES)],
        out_shape=[jax.ShapeDtypeStruct((s, GROUPS * LANES), F32)] * 3,
        scratch_shapes=[pltpu.VMEM((2, n_ch, bq, ch), BF16)] * 2 + [pltpu.VMEM((3, 4, bq, ch), BF16), pltpu.SemaphoreType.DMA((3, 4, COPY_PARTS))],
        compiler_params=_cparams("arbitrary", "arbitrary"),
    )(qkv, qkv, qkv, do, saved, behind)


def _rope_tables(s):
    half = MLA_ROPE // 2
    freqs = ROPE_BASE ** (-jnp.arange(half, dtype=F32) / half)
    ang = jnp.arange(s, dtype=F32)[:, None] * freqs[None, :]
    cos, sin = jnp.cos(ang), jnp.sin(ang)
    tail = jnp.zeros((s, LANES - MLA_NOPE - MLA_ROPE), F32)
    lead = lambda fill: jnp.full((s, MLA_NOPE), fill, F32)
    return dict(
        cos_k0=jnp.concatenate([cos, cos, lead(0.0), tail], axis=1), sin_k0=jnp.concatenate([-sin, sin, lead(0.0), tail], axis=1),
        cos_k64=jnp.concatenate([lead(0.0), cos, cos, tail], axis=1), sin_k64=jnp.concatenate([lead(0.0), -sin, sin, tail], axis=1),
        cos_q=jnp.concatenate([lead(1.0), cos, cos, tail], axis=1),
        sin_k64_t=jnp.concatenate([lead(0.0), sin, -sin, tail], axis=1),
    )


def _local_step(x, mem, target, w, emit=lambda grads: [jnp.zeros((8, LANES), F32)]):
    s = x.shape[0]
    rope = _rope_tables(s)
    xb = x.astype(BF16)
    inv_d = 1.0 / D_MODEL
    scale_a = LOG2E / math.sqrt(MLA_NOPE + MLA_ROPE)
    scale_b = LOG2E / math.sqrt(SB_HEAD_DIM)
    scale_m = LOG2E / math.sqrt(MEM_HEAD_DIM)
    arrive_after = getattr(w, "arrive_after", lambda *values: None)
    memb = mem.astype(BF16)

    arrive_after(xb, memb, *rope.values())
    proj = _mm(xb, w["w_in"], "nn", name="proj", b_cols=(0, QKV_FIRST))
    one = jnp.ones((1, 512), F32)
    qkv = _mm(xb, w["w_in"], "nn", name="proj_qkv", b_cols=(QKV_FIRST, QKV_WIDTH), out_dtype=BF16,
              col_scale=jnp.concatenate([one * scale_b, one, one, one * scale_m], axis=1))
    arrive_after(qkv)
    pre = _mm(xb, w["w_merge_gate"], "nn", name="merge_pre", out_dtype=BF16)
    arrive_after(pre)

    def mla_inputs(c_q, c_kv, k_rope, g_q, g_kv, w_q, w_kv, cos_q, sin_q, cos_k, sin_k):
        n_q = (c_q * lax.rsqrt(jnp.mean(c_q * c_q, axis=1, keepdims=True) + RMS_EPS) * g_q).astype(BF16)
        n_kv = (c_kv * lax.rsqrt(jnp.mean(c_kv * c_kv, axis=1, keepdims=True) + RMS_EPS) * g_kv).astype(BF16)
        q_a = _dot(n_q, w_q, "nn")
        kv_a = _dot(n_kv, w_kv, "nn").astype(BF16)
        q = jnp.concatenate([(g * cos_q + _swap_halves(g, MLA_NOPE) * sin_q) * scale_a for g in _lane_groups(q_a)], axis=1)
        k_pe = pltpu.roll(k_rope * cos_k + _swap_halves(k_rope, 0) * sin_k, MLA_NOPE, axis=1).astype(BF16)
        k = jnp.concatenate([g + k_pe for g in _lane_groups(kv_a[:, :1024])], axis=1)
        return n_q, n_kv, q, k, kv_a[:, 1024:]

    n_q, n_kv, q_mla, k_mla, v_a = _rowwise(
        mla_inputs, [(proj, 256, COL_CQ), (proj, 128, COL_CKV), (proj, 128, COL_KROPE), w["q_a_gain"], w["kv_a_gain"],
                     ("whole", w["w_q_b"]), ("whole", w["w_kv_b"]), rope["cos_q"], rope["sin_k64"], rope["cos_k0"], rope["sin_k0"]],
        [(256, BF16), (128, BF16), (1024, BF16), (1024, BF16), (512, BF16)], name="mla_inputs", rows=s)
    o_a, lse_a = _softmax_fwd(q_mla, k_mla, v_a, hp=2, causal=True, name="mla_fwd")

    o_b, sb_saved = _sb_fwd(qkv, q0=COL_QB, k0=COL_KB, v0=COL_VB, name="sb_fwd")

    mem_kv =_mm(memb, w["w_mem_kv"], "nn", name="mem_kv", out_dtype=BF16)
    o_m, lse_m = _softmax_fwd(qkv, mem_kv, mem_kv, hp=1, causal=False, name="mem_fwd", q0=1536, v0=512, q_rows=MEM_Q_BLOCK)

    branches = ("mla", "sb", "mem")
    w_branch = jnp.stack([w[f"w_branch_{br}"] for br in branches])
    bias = w["b_merge_gate"]

    def head(oa, ob, om, ga, gb, gm, pa, pb, pm, ba, bb, bm, xv, tv, gain, bias_ln, w_b, w_o, w_g):
        us, ys, gs = [], [], []
        for n, (o, gate, p, b) in enumerate(((oa, ga, pa, ba), (ob, gb, pb, bb), (om, gm, pm, bm))):
            us.append((o * gate * _sigmoid(gate)).astype(BF16))
            ys.append(_dot(us[n], w_b[n], "nn"))
            gs.append(_sigmoid(p.astype(F32) + b))
        merged = (gs[0] * ys[0] + gs[1] * ys[1] + gs[2] * ys[2]).astype(BF16)
        z = DEEPNORM_ALPHA * xv + _dot(merged, w_o, "nn")
        zc = z - jnp.mean(z, axis=1, keepdims=True)
        rstd = lax.rsqrt(jnp.mean(zc * zc, axis=1, keepdims=True) + LN_EPS)
        xhat = zc * rstd
        err = xhat * gain + bias_ln - tv
        loss = 0.5 * jnp.sum(jnp.mean(err * err, axis=1, keepdims=True), axis=0, keepdims=True)
        dy = err * inv_d
        dxhat = dy * gain
        dz = rstd * (dxhat - jnp.mean(dxhat, axis=1, keepdims=True) - xhat * jnp.mean(dxhat * xhat, axis=1, keepdims=True))
        dz16 = dz.astype(BF16)
        dm = _dot(dz16, w_o, "nt")
        dpre = jnp.concatenate([dm * ys[n] * gs[n] * (1.0 - gs[n]) for n in range(3)], axis=1)
        dx = DEEPNORM_ALPHA * dz + _dot(dpre.astype(BF16), w_g, "nt")
        dys = [(dm * gs[n]).astype(BF16) for n in range(3)]
        d_os, d_gates = [], []
        for n, (o, gate) in enumerate(((oa, ga), (ob, gb), (om, gm))):
            du, sg = _dot(dys[n], w_b[n], "nt"), _sigmoid(gate)
            d_os.append(du * gate * sg)
            d_gates.append(du * o * sg * (1.0 + gate * (1.0 - sg)))
        return (us, merged, dx, dz16, _colsum(dy * xhat), _colsum(dy), jnp.broadcast_to(loss, (1, LANES)), dpre, _colsum(dpre),
                dys, *d_os, *d_gates)

    grads = {}
    (u, merged, dx, dzb, grads["ln_gain"], grads["ln_bias"], loss, dpre, grads["b_merge_gate"], dy, *rest) = _rowwise(
        head, [o_a, o_b, o_m, (proj, 512, COL_GATE_A), (proj, 512, COL_GATE_B), (proj, 512, COL_GATE_M),
               (pre, 1024, 0), (pre, 1024, 1024), (pre, 1024, 2048), (bias, 1024, 0), (bias, 1024, 1024), (bias, 1024, 2048),
               x, target, w["ln_gain"], w["ln_bias"], ("whole", w_branch), ("whole", w["w_out"]), ("whole", w["w_merge_gate"])],
        [(3, 512, BF16), (1024, BF16), (1024, F32), (1024, BF16), ("sum", 1024), ("sum", 1024), ("sum", LANES),
         (3072, BF16), ("sum", 3072), (3, 1024, BF16)] + [(512, F32)] * 3 + [(512, BF16)] * 3, name="head", rows=s, tr=256)
    d_o, d_gate = dict(zip(branches, rest[:3], strict=True)), dict(zip(branches, rest[3:], strict=True))

    grads["w_out"] = _mm(merged, dzb, "tn", name="g_w_out", out_dtype=BF16)
    grads["w_merge_gate"] = _mm(xb, dpre, "tn", name="g_w_merge", out_dtype=BF16)
    g_w_branch = _mm(u, dy, "tn", name="g_w_branch", out_dtype=BF16)
    for n, br in enumerate(branches):
        grads[f"w_branch_{br}"] = g_w_branch[n]
    (sent,) = emit({n: grads[n] for n in ("w_out", "w_merge_gate", "w_branch_mla", "w_branch_sb", "w_branch_mem")})

    dq_m, dk_m, dv_m = _softmax_bwd(qkv, mem_kv, mem_kv, o_m, d_o["mem"], lse_m, sent, hp=1, causal=False, dq_scale=scale_m,
                                    name="mem_bwd", q0=1536, v0=512, q_rows=MEM_Q_BLOCK)
    grads["w_mem_kv"] = _mm(memb, jnp.concatenate([dk_m, dv_m], axis=1), "tn", name="g_w_mem_kv", out_dtype=BF16)

    dq_sb, dk_sb, dv_sb = _sb_bwd(qkv, d_o["sb"], sb_saved, sent, q0=COL_QB, k0=COL_KB, v0=COL_VB, dq_scale=scale_b, name="sb_bwd")

    dq_mla, dk_mla, dv_a = _softmax_bwd(q_mla, k_mla, v_a, o_a, d_o["mla"], lse_a, sent, hp=2, causal=True, dq_scale=scale_a,
                                        name="mla_bwd")

    def mla_inputs_bwd(dq, dk, dv, n_q, n_kv, c_q, c_kv, g_q, g_kv, w_q, w_kv, cos_q, sin_q, cos_k, sin_k):
        dq_a = jnp.concatenate([g * cos_q + _swap_halves(g, MLA_NOPE) * sin_q for g in _lane_groups(dq)], axis=1).astype(BF16)
        groups = _lane_groups(dk)
        g_rope = groups[0]
        for other in groups[1:]:
            g_rope = g_rope + other
        dk_rope = pltpu.roll(g_rope * cos_k + _swap_halves(g_rope, MLA_NOPE) * sin_k, MLA_NOPE, axis=1)
        nope = _low_half(g_rope.shape)
        dkv_a = jnp.concatenate([jnp.where(nope, grp, 0.0) for grp in groups] + [dv], axis=1).astype(BF16)
        res = []
        for c, dn, g in ((c_q, _dot(dq_a, w_q, "nt"), g_q), (c_kv, _dot(dkv_a, w_kv, "nt"), g_kv)):
            r = lax.rsqrt(jnp.mean(c * c, axis=1, keepdims=True) + RMS_EPS)
            t = dn * g
            res += [r * t - c * (r * r * r) * jnp.mean(c * t, axis=1, keepdims=True), _colsum(dn * c * r)]
        return *res, dk_rope, _dot(n_q, dq_a, "tn"), _dot(n_kv, dkv_a, "tn")

    dc_q, grads["q_a_gain"], dc_kv, grads["kv_a_gain"], dk_rope, g_w_q_b, g_w_kv_b = _rowwise(
        mla_inputs_bwd, [dq_mla, dk_mla, dv_a, n_q, n_kv, (proj, 256, COL_CQ), (proj, 128, COL_CKV), w["q_a_gain"], w["kv_a_gain"],
                         ("whole", w["w_q_b"]), ("whole", w["w_kv_b"]), rope["cos_q"], rope["sin_k64_t"], rope["cos_k64"], rope["sin_k64_t"]],
        [(256, BF16), ("sum", 256), (128, BF16), ("sum", 128), (128, BF16), ("sum", (MLA_Q_LORA, 1024)), ("sum", (MLA_KV_LORA, 1536))],
        name="mla_inputs_bwd", rows=s)
    grads["w_q_b"], grads["w_kv_b"] = g_w_q_b.astype(BF16), g_w_kv_b.astype(BF16)

    sent = emit({n: grads[n] for n in ("w_mem_kv", "w_q_b", "w_kv_b")})

    dproj = jnp.concatenate(
        [dc_q, dc_kv, dk_rope, d_gate["mla"], d_gate["sb"], d_gate["mem"], dq_sb.astype(BF16), dk_sb.astype(BF16),
         dv_sb.astype(BF16), dq_m.astype(BF16)], axis=1)
    grads["w_in"] = _mm(xb, dproj, "tn", name="g_w_in", out_dtype=BF16, behind=sent)
    sent = emit({"w_in": grads["w_in"]})
    grad_x = _mm(dproj, w["w_in"], "nt", name="grad_x", add=dx, behind=sent)
    return loss, grad_x, grads


def _shard_shape(shape, axis):
    return tuple(d // N_DEV if a == axis else d for a, d in enumerate(shape))


def _from_blocks(blocks, name):
    shape, axis = SHARDED[name]
    return blocks.reshape(shape) if axis == 0 else blocks.transpose(1, 0, 2).reshape(shape)


def _to_blocks(full, name):
    shape, axis = SHARDED[name]
    shp = _shard_shape(shape, axis)
    return full.reshape(N_DEV, *shp) if axis == 0 else full.reshape(shape[0], N_DEV, shp[1]).transpose(1, 0, 2)


def _pad_heads(a, used):
    rows = a.shape[0]
    a = a.reshape(rows, MLA_HEADS, used)
    return jnp.concatenate([a, jnp.zeros((rows, MLA_HEADS, LANES - used), a.dtype)], axis=2).reshape(rows, MLA_HEADS * LANES)


def _to_kernel_layout(name, full):
    if name == "w_in":
        return jnp.concatenate([jnp.zeros((D_MODEL, IN_PAD), full.dtype) if piece is None else full[:, piece[0]:piece[0] + piece[1]]
                                for piece in IN_PIECES], axis=1)
    if name == "w_q_b":
        return _pad_heads(full, MLA_NOPE + MLA_ROPE)
    if name == "w_kv_b":
        kv = full.reshape(MLA_KV_LORA, MLA_HEADS, MLA_NOPE + MLA_V)
        return jnp.concatenate([_pad_heads(kv[:, :, :MLA_NOPE].reshape(MLA_KV_LORA, -1), MLA_NOPE),
                                kv[:, :, MLA_NOPE:].reshape(MLA_KV_LORA, -1)], axis=1)
    return full


def _from_kernel_layout(name, g):
    if name == "w_in":
        placed, at = [], 0
        for piece in IN_PIECES:
            if piece is not None:
                placed.append((piece[0], g[:, at:at + piece[1]]))
            at += IN_PAD if piece is None else piece[1]
        return jnp.concatenate([cols for _, cols in sorted(placed, key=lambda item: item[0])], axis=1)
    if name == "w_q_b":
        return g.reshape(MLA_Q_LORA, MLA_HEADS, LANES)[:, :, :MLA_NOPE + MLA_ROPE].reshape(MLA_Q_LORA, -1)
    if name == "w_kv_b":
        return jnp.concatenate([g[:, :1024].reshape(MLA_KV_LORA, MLA_HEADS, LANES)[:, :, :MLA_NOPE],
                                g[:, 1024:].reshape(MLA_KV_LORA, MLA_HEADS, MLA_V)], axis=2).reshape(MLA_KV_LORA, -1)
    return g


def _pack_small(vectors, loss=None):
    flat = [v.reshape(-1) for v in vectors]
    flat.append(jnp.zeros((SMALL_ROWS * SMALL_LANES - LOSS_INDEX,), F32) if loss is None else
                jnp.concatenate([loss.reshape(-1)[:1], jnp.zeros((SMALL_ROWS * SMALL_LANES - LOSS_INDEX - 1,), F32)]))
    return jnp.concatenate(flat).reshape(SMALL_ROWS, SMALL_LANES)


def _unpack_small(packed):
    flat, res, off = packed.reshape(-1), [], 0
    for _, n in SMALL:
        res.append(flat[off:off + n].reshape(1, n))
        off += n
    return res


def _me_and_peers():
    x, y, c = lax.axis_index("x"), lax.axis_index("y"), lax.axis_index("c")
    peers = []
    for kk in range(1, N_DEV):
        px, py, pc = (x + (kk >> 2)) % 2, (y + ((kk >> 1) & 1)) % 2, (c + (kk & 1)) % 2
        peers.append(((px, py, pc), 4 * px + 2 * py + pc))
    return 4 * x + 2 * y + c, peers


def _share_small(small, *, name):
    def body(small_ref, all_ref, send_sems, recv_sems, local_sem):
        me, peers = _me_and_peers()
        copies = [pltpu.make_async_remote_copy(src_ref=small_ref, dst_ref=all_ref.at[me], send_sem=send_sems.at[kk], recv_sem=recv_sems.at[kk],
                                               device_id=pos, device_id_type=pl.DeviceIdType.MESH) for kk, (pos, _) in enumerate(peers)]
        copies.append(pltpu.make_async_copy(small_ref, all_ref.at[me], local_sem))
        for cp in copies:
            cp.start()
        for cp in copies:
            cp.wait()

    hbm = pl.BlockSpec(memory_space=pl.ANY)
    return pl.pallas_call(
        body, name=name, in_specs=[hbm], out_specs=hbm, out_shape=jax.ShapeDtypeStruct((N_DEV, *small.shape), small.dtype),
        scratch_shapes=[pltpu.SemaphoreType.DMA((N_DEV - 1,)), pltpu.SemaphoreType.DMA((N_DEV - 1,)), pltpu.SemaphoreType.DMA],
        compiler_params=pltpu.CompilerParams(has_side_effects=True),
    )(small)


_HBM = pl.BlockSpec(memory_space=pltpu.HBM)
_SEM = pl.BlockSpec(memory_space=pltpu.SEMAPHORE)


def _exchange_copies(srcs, zones, send_sems, recv_sems, gather):
    me, peers = _me_and_peers()
    return [pltpu.make_async_remote_copy(
        src_ref=srcs[t] if gather else srcs[t].at[peer], dst_ref=zones[t].at[me], send_sem=send_sems.at[7 * t + kk],
        recv_sem=recv_sems.at[7 * t + kk], device_id=pos, device_id_type=pl.DeviceIdType.MESH)
        for t in range(len(srcs)) for kk, (pos, peer) in enumerate(peers)]


def _exchange_start(tensors, *, gather, name):
    n = len(tensors)
    zones = [lax.empty((N_DEV, *(t.shape if gather else t.shape[1:])), t.dtype) for t in tensors]

    def body(*refs):
        for cp in _exchange_copies(refs[:n], refs[n:2 * n], refs[2 * n], refs[2 * n + 1], gather):
            cp.start()
        refs[-1][...] = jnp.zeros_like(refs[-1])

    buffers = [pltpu.HBM(a.shape, a.dtype) for a in tensors + zones]
    res = pl.pallas_call(
        body, name=name, in_specs=[_HBM] * (2 * n),
        out_shape=(pltpu.SemaphoreType.DMA((7 * n,)), pltpu.SemaphoreType.DMA((7 * n,)), *buffers, jax.ShapeDtypeStruct((8, LANES), F32)),
        out_specs=(_SEM, _SEM, *[_HBM] * (2 * n), pl.BlockSpec(memory_space=pltpu.VMEM)),
        input_output_aliases={i: 2 + i for i in range(2 * n)},
        compiler_params=pltpu.CompilerParams(has_side_effects=pltpu.SideEffectType.DATAFLOW_SIDE_EFFECTING),
    )(*[pltpu.with_memory_space_constraint(a, pltpu.HBM) for a in tensors + zones])
    return dict(sems=res[:2], buffers=res[2:2 + 2 * n], gather=gather, started=res[-1])


def _exchange_wait(started, after, *, name):
    n = len(started["buffers"]) // 2

    def body(*refs):
        for cp in _exchange_copies(refs[:n], refs[n:2 * n], refs[2 * n], refs[2 * n + 1], started["gather"]):
            cp.wait_send()
            cp.wait_recv()

    res = pl.pallas_call(
        body, name=name, in_specs=[_HBM] * (2 * n) + [_SEM, _SEM] + [_ANY] * len(after),
        out_shape=tuple(pltpu.HBM(a.shape, a.dtype) for a in started["buffers"]), out_specs=tuple([_HBM] * (2 * n)),
        input_output_aliases={i: i for i in range(2 * n)},
        compiler_params=pltpu.CompilerParams(has_side_effects=pltpu.SideEffectType.DATAFLOW_SIDE_EFFECTING),
    )(*started["buffers"], *started["sems"], *after)
    return res[:n], res[n:]


def _adamw(contrib, w, m, v, *, name):
    rows, cols = w.shape
    tile = min(rows, ADAM_ROWS)

    def body(c_ref, w_ref, m_ref, v_ref, g_ref, d_ref, nm_ref, nv_ref):
        g = c_ref[0].astype(F32)
        for s in range(1, N_DEV):
            g = g + c_ref[s].astype(F32)
        m_new = ADAM_B1 * m_ref[...] + (1.0 - ADAM_B1) * g
        v_new = ADAM_B2 * v_ref[...] + (1.0 - ADAM_B2) * (g * g)
        m_hat = m_new / (1.0 - ADAM_B1 ** ADAM_STEP)
        v_hat = v_new / (1.0 - ADAM_B2 ** ADAM_STEP)
        g_ref[...] = g
        d_ref[...] = -ADAM_LR * (m_hat / (jnp.sqrt(v_hat) + ADAM_EPS) + ADAM_WD * w_ref[...])
        nm_ref[...] = m_new
        nv_ref[...] = v_new

    spec = pl.BlockSpec((tile, cols), lambda i: (i, 0))
    return pl.pallas_call(
        body, name=name, grid=(rows // tile,),
        in_specs=[pl.BlockSpec((N_DEV, tile, cols), lambda i: (0, i, 0)), spec, spec, spec], out_specs=[spec] * 4,
        out_shape=[jax.ShapeDtypeStruct((rows, cols), F32)] * 4, compiler_params=_cparams("parallel"),
    )(contrib, w, m, v)


class _Weights:
    def __init__(self, gathers, vectors, me):
        self.gathers, self.ready, self.me, self.after = gathers, dict(vectors), me, ()

    def arrive_after(self, *values):
        self.after = values

    def __getitem__(self, name):
        if name not in self.ready:
            gi = next(i for i, group in enumerate(GATHER_GROUPS) if name in group)
            after = [*self.after, *[g["started"] for g in self.gathers]]
            shards, zones = _exchange_wait(self.gathers[gi], after, name=f"gather_wait_{gi}")
            for n, shard, zone in zip(GATHER_GROUPS[gi], shards, zones, strict=True):
                blocks = lax.dynamic_update_slice_in_dim(zone, shard[None], self.me, 0)
                self.ready[n] = _to_kernel_layout(n, _from_blocks(blocks, n))
        return self.ready[name]


def kernel(x, mem, w_in, w_mem_kv, q_a_gain, w_q_b, kv_a_gain, w_kv_b, w_branch_mla, w_branch_sb, w_branch_mem, w_merge_gate, b_merge_gate, w_out, ln_gain, ln_bias, loss_target, m_w_in, m_w_mem_kv, m_q_a_gain, m_w_q_b, m_kv_a_gain, m_w_kv_b, m_w_branch_mla, m_w_branch_sb, m_w_branch_mem, m_w_merge_gate, m_b_merge_gate, m_w_out, m_ln_gain, m_ln_bias, v_w_in, v_w_mem_kv, v_q_a_gain, v_w_q_b, v_kv_a_gain, v_w_kv_b, v_w_branch_mla, v_w_branch_sb, v_w_branch_mem, v_w_merge_gate, v_b_merge_gate, v_w_out, v_ln_gain, v_ln_bias):
    given = dict(locals())
    small_names = [n for n, _ in SMALL]
    smalls = lambda prefix: [given[prefix + n] for n in small_names]
    me = 4 * lax.axis_index("x") + 2 * lax.axis_index("y") + lax.axis_index("c")

    gathers = [_exchange_start([given[n][0].astype(BF16) for n in group], gather=True, name=f"gather_start_{gi}")
               for gi, group in enumerate(GATHER_GROUPS)]
    w = _Weights(gathers, {n: given[n] for n in small_names}, me)
    exchanges = []
    results = [{}, {}, {}, {}]

    def finish(gi, after):
        names, started = exchanges[gi]
        sent, zones = _exchange_wait(started, after, name=f"grads_wait_{gi}")
        done = []
        for n, blocks, zone in zip(names, sent, zones, strict=True):
            own = lax.dynamic_index_in_dim(blocks, me, 0, keepdims=True)
            contrib = lax.dynamic_update_slice_in_dim(zone, own, me, 0)
            outs = _adamw(contrib, given[n][0], given["m_" + n][0], given["v_" + n][0], name=f"adamw_{n}")
            for kind, res in zip(results, outs, strict=True):
                kind[n] = res[None]
            done.append(outs[1])
        return done

    def emit(grads):
        blocks = [_to_blocks(_from_kernel_layout(n, g), n).astype(BF16) for n, g in grads.items()]
        exchanges.append((tuple(grads), _exchange_start(blocks, gather=False, name=f"grads_start_{len(exchanges)}")))
        started = [exchanges[-1][1]["started"]]
        if len(exchanges) == len(GRAD_GROUPS):
            for gi in range(len(GRAD_GROUPS) - 1):
                started += finish(gi, started[:1])
        return started

    loss, grad_x, grads = _local_step(x[0], mem[0], loss_target[0], w, emit)

    contrib_small = _share_small(_pack_small([grads[n] for n in small_names], loss), name="share_small")
    sml = _adamw(contrib_small, _pack_small(smalls("")), _pack_small(smalls("m_")), _pack_small(smalls("v_")), name="adamw_small")
    for kind, packed in zip(results, sml, strict=True):
        kind.update(zip(small_names, _unpack_small(packed), strict=True))
    finish(len(GRAD_GROUPS) - 1, [grad_x])
    order = ["w_in", "w_mem_kv", "q_a_gain", "w_q_b", "kv_a_gain", "w_kv_b", "w_branch_mla", "w_branch_sb", "w_branch_mem",
             "w_merge_gate", "b_merge_gate", "w_out", "ln_gain", "ln_bias"]
    loss_out = sml[0].reshape(-1)[LOSS_INDEX]
    return (loss_out, grad_x[None], *[kind[n] for kind in results for n in order])
```

```python
import math

import jax
import jax.numpy as jnp
from jax import lax
from jax.experimental import pallas as pl
from jax.experimental.pallas import tpu as pltpu

F32, BF16 = jnp.float32, jnp.bfloat16

N_DEV = 8
D_MODEL = 1024
MLA_HEADS, MLA_NOPE, MLA_ROPE, MLA_V = 8, 64, 32, 64
MLA_Q_LORA, MLA_KV_LORA = 256, 128
SB_HEAD_DIM = 64
MEM_HEAD_DIM = 128
ROPE_BASE = 10000.0
RMS_EPS = 1e-6
LN_EPS = 1e-5
DEEPNORM_ALPHA = 2.0 ** 0.25
ADAM_LR, ADAM_B1, ADAM_B2, ADAM_EPS, ADAM_WD, ADAM_STEP = 0.001, 0.9, 0.999, 1e-08, 0.01, 10
LOG2E, LN2 = math.log2(math.e), math.log(2.0)

LANES = 128
GROUPS = 4
PROJ_WIDTH = 4096
COL_CQ, COL_CKV, COL_KROPE, COL_GATE_A, COL_GATE_B, COL_GATE_M = 0, 256, 384, 512, 1024, 1536
QKV_FIRST, QKV_WIDTH = 2048, 2048
COL_QB, COL_KB, COL_VB, COL_QM = 0, 512, 1024, 1536
IN_PIECES = ((0, 416), None, (416, 512), (2464, 512), (3488, 512), (928, 512), (1440, 512), (1952, 512), (2976, 512))
IN_PAD = 96

VMEM_LIMIT_BYTES = 56 * 1024 * 1024
NEG_BIG = -1e30
Q_BLOCK = 512
MEM_Q_BLOCK = 2048
SB_BWD_Q_BLOCK = 512
TRI_BLOCK = 256
TILE_ROWS = 64
KEY_CHUNK = 512

SHARDED = {
    "w_in": ((1024, 4000), 1), "w_mem_kv": ((1024, 1024), 0), "w_q_b": ((256, 768), 1), "w_kv_b": ((128, 1024), 1),
    "w_branch_mla": ((512, 1024), 1), "w_branch_sb": ((512, 1024), 1), "w_branch_mem": ((512, 1024), 1),
    "w_merge_gate": ((1024, 3072), 1), "w_out": ((1024, 1024), 0),
}
GATHER_GROUPS = (("w_in",), ("w_merge_gate",), ("w_q_b", "w_kv_b", "w_mem_kv", "w_branch_mla", "w_branch_sb", "w_branch_mem", "w_out"))
GRAD_GROUPS = (("w_out", "w_merge_gate", "w_branch_mla", "w_branch_sb", "w_branch_mem"), ("w_mem_kv", "w_q_b", "w_kv_b"), ("w_in",))
SMALL = (("q_a_gain", 256), ("kv_a_gain", 128), ("b_merge_gate", 3072), ("ln_gain", 1024), ("ln_bias", 1024))
SMALL_ROWS, SMALL_LANES = 48, 128
ADAM_ROWS = 256
LOSS_INDEX = 5504


def _cparams(*sem):
    return pltpu.CompilerParams(dimension_semantics=sem or None, vmem_limit_bytes=VMEM_LIMIT_BYTES)


_DIMS = {"nn": (((1,), (0,)), ((), ())), "nt": (((1,), (1,)), ((), ())), "tn": (((0,), (0,)), ((), ()))}


def _dot(a, b, dims):
    return lax.dot_general(a, b, _DIMS[dims], preferred_element_type=F32)


def _tile(dim, want):
    if dim <= want:
        return dim
    t = want - want % LANES
    while dim % t:
        t -= LANES
    assert t > 0, (dim, want)
    return t


_ANY = pl.BlockSpec(memory_space=pl.ANY)


def _mm(a, b, dims, *, name, out_dtype=F32, add=None, add_scale=1.0, col_scale=None, b_cols=None, behind=None,
        tm=1024, tn=1024, tk=1024):
    batch = a.shape[0] if a.ndim == 3 else None
    if dims == "nn":
        (m, k), (k2, n) = a.shape[-2:], b.shape[-2:]
    elif dims == "nt":
        (m, k), (n, k2) = a.shape[-2:], b.shape[-2:]
    else:
        (k, m), (k2, n) = a.shape[-2:], b.shape[-2:]
    assert k == k2 and a.ndim == b.ndim, (a.shape, b.shape, dims)
    assert batch is None or (b.shape[0] == batch and add is None and col_scale is None and b_cols is None)
    b_first = 0
    if b_cols is not None:
        assert dims == "nn"
        b_first, n = b_cols
    tm, tn, tk = _tile(m, tm), _tile(n, tn), _tile(k, tk)
    assert b_first % tn == 0
    jb = b_first // tn
    nk = k // tk

    def spec(block, index):
        if batch is None:
            return pl.BlockSpec(block, lambda bb, i, j, kk: index(i, j, kk))
        return pl.BlockSpec((None, *block), lambda bb, i, j, kk: (bb, *index(i, j, kk)))

    a_spec = spec((tk, tm), lambda i, j, kk: (kk, i)) if dims == "tn" else spec((tm, tk), lambda i, j, kk: (i, kk))
    b_spec = spec((tn, tk), lambda i, j, kk: (j, kk)) if dims == "nt" else spec((tk, tn), lambda i, j, kk: (kk, jb + j))
    o_spec = spec((tm, tn), lambda i, j, kk: (i, j))
    behind = [] if behind is None else behind if isinstance(behind, (list, tuple)) else [behind]
    optional = [(add, o_spec), (col_scale, pl.BlockSpec((1, tn), lambda bb, i, j, kk: (0, j))), *[(v, _ANY) for v in behind]]
    present = [(v, spec) for v, spec in optional if v is not None]

    def body(*refs):
        a_ref, b_ref = refs[:2]
        extra = iter(refs[2:2 + len(present)])
        add_ref = next(extra) if add is not None else None
        scale_ref = next(extra) if col_scale is not None else None
        o_ref = refs[2 + len(present)]
        part = _dot(a_ref[...].astype(BF16), b_ref[...].astype(BF16), dims)

        def finish(r):
            if add is not None:
                r = r + add_scale * add_ref[...]
            if col_scale is not None:
                r = r * scale_ref[...]
            o_ref[...] = r.astype(out_dtype)

        if nk == 1:
            finish(part)
            return
        acc = refs[-1]
        kk = pl.program_id(3)

        @pl.when(kk == 0)
        def _():
            acc[...] = part

        @pl.when(kk > 0)
        def _():
            acc[...] += part

        @pl.when(kk == nk - 1)
        def _():
            finish(acc[...])

    return pl.pallas_call(
        body, name=name, grid=(batch or 1, m // tm, n // tn, nk),
        in_specs=[a_spec, b_spec] + [spec for _, spec in present], out_specs=o_spec,
        out_shape=jax.ShapeDtypeStruct((m, n) if batch is None else (batch, m, n), out_dtype),
        scratch_shapes=[pltpu.VMEM((tm, tn), F32)] if nk > 1 else [],
        compiler_params=_cparams("parallel", "parallel", "parallel", "arbitrary"),
    )(a, b, *[v for v, _ in present])


def _rowwise(fn, ins, outs, *, name, rows, tr=512):
    n_in = len(ins)
    tr = min(tr, rows)
    in_specs, args = [], []
    for it in ins:
        if isinstance(it, tuple) and it[0] == "whole":
            in_specs.append(pl.BlockSpec(it[1].shape, lambda i, nd=it[1].ndim: (0,) * nd))
            args.append(it[1])
            continue
        arr, w, off = it if isinstance(it, tuple) else (it, it.shape[-1], 0)
        assert off % w == 0
        cb = off // w
        if arr.ndim == 3:
            in_specs.append(pl.BlockSpec((arr.shape[0], tr, w), lambda i, cb=cb: (0, i, cb)))
        elif arr.shape[0] == 1:
            in_specs.append(pl.BlockSpec((1, w), lambda i, cb=cb: (0, cb)))
        else:
            in_specs.append(pl.BlockSpec((tr, w), lambda i, cb=cb: (i, cb)))
        args.append(arr)
    out_shape, out_specs, is_sum = [], [], []
    for out in outs:
        is_sum.append(out[0] == "sum")
        if out[0] == "sum":
            shape = out[1] if isinstance(out[1], tuple) else (1, out[1])
            out_shape.append(jax.ShapeDtypeStruct(shape, F32))
            out_specs.append(pl.BlockSpec(shape, lambda i: (0, 0)))
        elif len(out) == 3:
            out_shape.append(jax.ShapeDtypeStruct((out[0], rows, out[1]), out[2]))
            out_specs.append(pl.BlockSpec((out[0], tr, out[1]), lambda i: (0, i, 0)))
        else:
            out_shape.append(jax.ShapeDtypeStruct((rows, out[0]), out[1]))
            out_specs.append(pl.BlockSpec((tr, out[0]), lambda i: (i, 0)))

    def body(*refs):
        res = fn(*[r[...] for r in refs[:n_in]])
        for r, val, s in zip(refs[n_in:], res, is_sum, strict=True):
            if s:
                @pl.when(pl.program_id(0) == 0)
                def _(r=r):
                    r[...] = jnp.zeros_like(r)

                r[...] += val
            elif isinstance(val, (list, tuple)):
                for n, part in enumerate(val):
                    r[n] = part.astype(r.dtype)
            else:
                r[...] = val.astype(r.dtype)

    return pl.pallas_call(
        body, name=name, grid=(rows // tr,), in_specs=in_specs, out_specs=out_specs, out_shape=out_shape,
        compiler_params=_cparams("arbitrary"),
    )(*args)


def _colsum(v):
    return jnp.sum(v, axis=0, keepdims=True)


def _sigmoid(v):
    return 1.0 / (1.0 + jnp.exp(-v))


def _lane_groups(v):
    return [v[:, g * LANES:(g + 1) * LANES] for g in range(v.shape[1] // LANES)]


def _swap_halves(v, first_lane):
    lane = lax.broadcasted_iota(jnp.int32, v.shape, 1)
    return jnp.where(lane < first_lane + 16, pltpu.roll(v, 112, axis=1), pltpu.roll(v, 16, axis=1))


def _lane_sum(acc, v):
    for part in _lane_groups(v):
        acc = acc + part
    return acc


def _low_half(shape):
    return lax.broadcasted_iota(jnp.int32, shape, 1) < LANES // 2


def _select_heads(per_head, pick):
    if len(per_head) == 1:
        return pick(per_head[0], 0)
    return jnp.where(_low_half(per_head[0].shape), pick(per_head[0], 0), pick(per_head[1], 1))


def _attn_specs(s, sk, hp, bq, q0, k0, v0):
    wq = hp * LANES
    assert q0 % wq == 0 and k0 % wq == 0 and v0 % LANES == 0
    qb0, kb0, vb0 = q0 // wq, k0 // wq, v0 // LANES
    q_spec = pl.BlockSpec((bq, wq), lambda g, i: (i, qb0 + g))
    k_spec = pl.BlockSpec((sk, wq), lambda g, i: (0, kb0 + g))
    v_spec = pl.BlockSpec((sk, LANES), lambda g, i: (0, vb0 + g))
    row_out = lambda w: pl.BlockSpec((bq, w), lambda g, i: (i, g))
    key_out = lambda w: pl.BlockSpec((sk, w), lambda g, i: (0, g))
    return q_spec, k_spec, v_spec, row_out, key_out


def _chunks(i, bq, ch, sk, causal):
    return ((i + 1) * bq - 1) // ch if causal else jnp.int32(sk // ch - 1)


def _positions(i, c, bq, ch):
    return (i * bq + lax.broadcasted_iota(jnp.int32, (bq, ch), 0), c * ch + lax.broadcasted_iota(jnp.int32, (bq, ch), 1))


def _softmax_fwd(q, k, v, *, hp, causal, name, q0=0, k0=0, v0=0, q_rows=Q_BLOCK):
    s, sk = q.shape[0], k.shape[0]
    bq, ch = min(q_rows, s), min(KEY_CHUNK, sk)
    assert not causal or bq <= ch
    q_spec, k_spec, v_spec, row_out, _ = _attn_specs(s, sk, hp, bq, q0, k0, v0)

    def body(q_ref, k_ref, v_ref, o_ref, lse_ref, s_scr):
        i = pl.program_id(1)
        qs = _lane_groups(q_ref[...])
        last = _chunks(i, bq, ch, sk, causal)

        def scores(c, ms, masked):
            off = pl.multiple_of(c * ch, ch)
            out = []
            for j in range(hp):
                sc = _dot(qs[j], k_ref[pl.ds(off, ch), j * LANES:(j + 1) * LANES], "nt")
                if masked:
                    qpos, kpos = _positions(i, c, bq, ch)
                    sc = jnp.where(kpos <= qpos, sc, NEG_BIG)
                s_scr[j, c] = sc
                m = ms[j]
                for part in _lane_groups(sc):
                    m = jnp.maximum(m, part)
                out.append(m)
            return tuple(out)

        ms = lax.fori_loop(0, last, lambda c, m: scores(c, m, False), tuple(jnp.full((bq, LANES), NEG_BIG, F32) for _ in range(hp)))
        ms = scores(last, ms, causal)
        row_max = [jnp.max(m, axis=1, keepdims=True) for m in ms]

        def weigh(c, carry):
            off = pl.multiple_of(c * ch, ch)
            vt = v_ref[pl.ds(off, ch), :]
            out = []
            for j in range(hp):
                l, acc = carry[j]
                p = jnp.exp2(s_scr[j, c] - row_max[j])
                out.append((_lane_sum(l, p), acc + _dot(p.astype(BF16), vt, "nn")))
            return tuple(out)

        zero = jnp.zeros((bq, LANES), F32)
        res = lax.fori_loop(0, last + 1, weigh, tuple((zero, zero) for _ in range(hp)))
        row_sum = [jnp.sum(l, axis=1, keepdims=True) for l, _ in res]
        o_ref[...] = _select_heads([acc for _, acc in res], lambda acc, j: acc / row_sum[j])
        lse_ref[...] = _select_heads([jnp.broadcast_to(row_max[j] + jnp.log2(row_sum[j]), (bq, LANES)) for j in range(hp)], lambda a, j: a)

    return pl.pallas_call(
        body, name=name, grid=(GROUPS, s // bq), in_specs=[q_spec, k_spec, v_spec], out_specs=[row_out(LANES), row_out(LANES)],
        out_shape=[jax.ShapeDtypeStruct((s, GROUPS * LANES), F32)] * 2,
        scratch_shapes=[pltpu.VMEM((hp, sk // ch, bq, ch), F32)], compiler_params=_cparams("parallel", "arbitrary"),
    )(q, k, v)


def _head_cotangent(do, j, hp):
    if hp == 1:
        return do
    return jnp.where(_low_half(do.shape) == (j == 0), do, 0.0)


def _softmax_bwd(q, k, v, o, do, lse, behind, *, hp, causal, dq_scale, name, q0=0, k0=0, v0=0, q_rows=Q_BLOCK):
    s, sk = q.shape[0], k.shape[0]
    bq, ch = min(q_rows, s), min(KEY_CHUNK, sk)
    assert not causal or bq <= ch
    wq = hp * LANES
    q_spec, k_spec, v_spec, row_out, key_out = _attn_specs(s, sk, hp, bq, q0, k0, v0)

    def body(q_ref, k_ref, v_ref, o_ref, do_ref, lse_ref, _, dq_ref, dk_ref, dv_ref, dk_t, dv_t):
        i = pl.program_id(1)

        @pl.when(i == 0)
        def _():
            dk_t[...] = jnp.zeros_like(dk_t)
            dv_t[...] = jnp.zeros_like(dv_t)

        qs = _lane_groups(q_ref[...])
        do_all, o_all, lse_all = do_ref[...], o_ref[...], lse_ref[...]
        dos, deltas, lses = [], [], []
        for j in range(hp):
            d = _head_cotangent(do_all, j, hp)
            deltas.append(jnp.sum(d * o_all, axis=1, keepdims=True))
            dos.append(d.astype(BF16))
            lses.append(lse_all[:, j * (LANES // hp):j * (LANES // hp) + 1])
        last = _chunks(i, bq, ch, sk, causal)

        def chunk(c, dqs, masked):
            off = pl.multiple_of(c * ch, ch)
            vt = v_ref[pl.ds(off, ch), :]
            out, dks, dv = [], [], None
            for j in range(hp):
                kt = k_ref[pl.ds(off, ch), j * LANES:(j + 1) * LANES]
                p = jnp.exp2(_dot(qs[j], kt, "nt") - lses[j])
                if masked:
                    qpos, kpos = _positions(i, c, bq, ch)
                    p = jnp.where(kpos <= qpos, p, 0.0)
                ds = (p * (_dot(dos[j], vt, "nt") - deltas[j]) * LN2).astype(BF16)
                out.append(dqs[j] + _dot(ds, kt, "nn"))
                dks.append(_dot(qs[j], ds, "tn"))
                dvj = _dot(dos[j], p.astype(BF16), "tn")
                dv = dvj if dv is None else dv + dvj
            dk_t[c] += dks[0] if hp == 1 else jnp.concatenate(dks, axis=0)
            dv_t[c] += dv
            return tuple(out)

        dqs = lax.fori_loop(0, last, lambda c, d: chunk(c, d, False), tuple(jnp.zeros((bq, LANES), F32) for _ in range(hp)))
        dqs = chunk(last, dqs, causal)
        dq_ref[...] = (dqs[0] if hp == 1 else jnp.concatenate(dqs, axis=1)) * dq_scale

        @pl.when(i == s // bq - 1)
        def _():
            for c in range(sk // ch):
                dk_ref[c * ch:(c + 1) * ch, :] = dk_t[c].T
                dv_ref[c * ch:(c + 1) * ch, :] = dv_t[c].T

    return pl.pallas_call(
        body, name=name, grid=(GROUPS, s // bq),
        in_specs=[q_spec, k_spec, v_spec, row_out(LANES), row_out(LANES), row_out(LANES), _ANY],
        out_specs=[row_out(wq), key_out(wq), key_out(LANES)],
        out_shape=[jax.ShapeDtypeStruct((s, GROUPS * wq), F32), jax.ShapeDtypeStruct((sk, GROUPS * wq), F32),
                   jax.ShapeDtypeStruct((sk, GROUPS * LANES), F32)],
        scratch_shapes=[pltpu.VMEM((sk // ch, wq, ch), F32), pltpu.VMEM((sk // ch, LANES, ch), F32)],
        compiler_params=_cparams("arbitrary", "arbitrary"),
    )(q, k, v, o, do, lse, behind)


def _log2_sigmoid_pair(z2):
    minus_abs = lax.bitcast_convert_type(lax.bitcast_convert_type(z2, jnp.uint32) | jnp.uint32(0x80000000), F32)
    log_beta = jnp.minimum(z2, 0.0) - jnp.log2(1.0 + jnp.exp2(minus_abs))
    return log_beta, log_beta - z2


def _tilewise(fn, *arrays):
    rows, cols = arrays[0].shape
    step = min(TILE_ROWS, rows)
    grid = [[fn(*[None if a is None else a[r:r + step, c:c + LANES] for a in arrays]) for c in range(0, cols, LANES)]
            for r in range(0, rows, step)]
    return [jnp.concatenate([jnp.concatenate([cell[k] for cell in row], axis=1) for row in grid], axis=0)
            for k in range(len(grid[0][0]))]


def _split(v):
    hi = v.astype(BF16)
    return hi, (v - hi.astype(F32)).astype(BF16)


def _tri(n, after):
    rows, cols = lax.broadcasted_iota(jnp.int32, (n, n), 0), lax.broadcasted_iota(jnp.int32, (n, n), 1)
    return (rows > cols if after else rows < cols).astype(BF16)


def _running_sums(v, terms, start, tri, backwards):
    n = tri.shape[0]
    n_blocks = v.shape[1] // n
    order = range(n_blocks - 1, -1, -1) if backwards else range(n_blocks)
    stacked = tri if len(terms) == 1 else jnp.concatenate([tri] * len(terms), axis=0)
    parts, run = [None] * n_blocks, start
    for t in order:
        cols = slice(t * n, (t + 1) * n)
        lhs = terms[0][:, cols] if len(terms) == 1 else jnp.concatenate([term[:, cols] for term in terms], axis=1)
        parts[t] = _dot(lhs, stacked, "nn") + run
        run = run + jnp.sum(v[:, cols], axis=1, keepdims=True)
    return (parts[0] if n_blocks == 1 else jnp.concatenate(parts, axis=1)), run


def _sb_weights(qm, kt, run, tri, strict):
    def logs(z2, keep):
        log_beta, log_keep = _log2_sigmoid_pair(z2)
        if keep is not None:
            log_keep = jnp.where(keep, log_keep, 0.0)
        return log_beta, log_keep, *_split(log_keep)

    log_beta, log_keep, hi, lo = _tilewise(logs, _dot(qm, kt, "nt"), strict)
    behind, run = _running_sums(log_keep, (hi, lo), run, tri, True)

    def weigh(log_beta, behind, keep):
        a = jnp.exp2(log_beta + behind)
        return (a if keep is None else jnp.where(keep, a, 0.0),)

    (a,) = _tilewise(weigh, log_beta, behind, strict)
    return a, log_beta, run


class _Copies:
    def __init__(self, copies):
        self.copies = copies

    def start(self):
        for cp in self.copies:
            cp.start()

    def wait(self):
        for cp in self.copies:
            cp.wait()


def _sb_queries(q_all):
    low = _low_half(q_all.shape)
    zero = jnp.zeros_like(q_all)
    return [jnp.where(low, q_all, zero), jnp.where(low, zero, q_all)]


def _sb_fwd(qkv, *, q0, k0, v0, name):
    s = qkv.shape[0]
    bq, ch = min(Q_BLOCK, s), min(KEY_CHUNK, s)
    assert bq == ch
    n_q = s // bq
    q_spec, k_spec, v_spec, row_out, _ = _attn_specs(s, s, 1, bq, q0, k0, v0)

    def body(q_ref, k_ref, v_ref, o_ref, saved_ref, stage, sems):
        g, i = pl.program_id(0), pl.program_id(1)
        qms = _sb_queries(q_ref[...])
        tri = _tri(min(TRI_BLOCK, ch), True)
        last = _chunks(i, bq, ch, s, True)
        first_tile = i * (i + 1) // 2

        def save(slot, c):
            return _Copies([pltpu.make_async_copy(stage.at[slot, p], saved_ref.at[g, first_tile + c, p], sems.at[slot, p])
                            for p in range(4)])

        def chunk(c, step, carry, masked):
            off = pl.multiple_of(c * ch, ch)
            kt, vt = k_ref[pl.ds(off, ch), :], v_ref[pl.ds(off, ch), :]
            strict = None
            if masked:
                qpos, kpos = _positions(i, c, bq, ch)
                strict = kpos < qpos
            slot = step % 3
            if not masked:
                @pl.when(step >= 3)
                def _():
                    save(slot, c).wait()
            out = []
            for j in range(2):
                run, acc = carry[j]
                a, log_beta, run = _sb_weights(qms[j], kt, run, tri, strict)
                a = a.astype(BF16)
                stage[slot, j] = a
                stage[slot, 2 + j] = jnp.exp2(log_beta).astype(BF16)
                out.append((run, acc + _dot(a, vt, "nn")))
            save(slot, c).start()
            return tuple(out)

        carry = chunk(last, 0, tuple((jnp.zeros((bq, 1), F32), jnp.zeros((bq, LANES), F32)) for _ in range(2)), True)
        res = lax.fori_loop(0, last, lambda n, c: chunk(last - 1 - n, n + 1, c, False), carry)
        for back in range(3):
            @pl.when(last >= back)
            def _(back=back):
                save((last - back) % 3, 0).wait()

        o_ref[...] = _select_heads([acc for _, acc in res], lambda acc, j: acc)

    return pl.pallas_call(
        body, name=name, grid=(GROUPS, n_q), in_specs=[q_spec, k_spec, v_spec], out_specs=[row_out(LANES), _ANY],
        out_shape=[jax.ShapeDtypeStruct((s, GROUPS * LANES), F32),
                   jax.ShapeDtypeStruct((GROUPS, n_q * (n_q + 1) // 2, 4, bq, ch), BF16)],
        scratch_shapes=[pltpu.VMEM((3, 4, bq, ch), BF16), pltpu.SemaphoreType.DMA((3, 4))],
        compiler_params=_cparams("parallel", "arbitrary"),
    )(qkv, qkv, qkv)


def _sb_bwd(qkv, do, saved, behind, *, q0, k0, v0, dq_scale, name):
    s = qkv.shape[0]
    bq, ch = min(SB_BWD_Q_BLOCK, s), min(KEY_CHUNK, s)
    assert bq == ch and saved.shape[2:] == (4, bq, ch)
    q_spec, k_spec, v_spec, row_out, key_out = _attn_specs(s, s, 1, bq, q0, k0, v0)

    def body(q_ref, k_ref, v_ref, do_ref, saved_ref, _, dq_ref, dk_ref, dv_ref, g_s, beta_s, stage, sems):
        g_index, i = pl.program_id(0), pl.program_id(1)

        @pl.when(i == 0)
        def _():
            dk_ref[...] = jnp.zeros_like(dk_ref)
            dv_ref[...] = jnp.zeros_like(dv_ref)

        qms = _sb_queries(q_ref[...])
        do_all = do_ref[...]
        dos = [_head_cotangent(do_all, j, 2).astype(BF16) for j in range(2)]
        dos_ln2 = [(_head_cotangent(do_all, j, 2) * LN2).astype(BF16) for j in range(2)]
        tri_before = _tri(min(TRI_BLOCK, ch), False)
        last = _chunks(i, bq, ch, s, True)
        first_tile = i * (i + 1) // 2

        def strict_mask(c):
            qpos, kpos = _positions(i, c, bq, ch)
            return kpos < qpos

        def fetch_tile(slot, group, tile):
            return _Copies([pltpu.make_async_copy(saved_ref.at[group, tile, p], stage.at[slot, p], sems.at[slot, p])
                            for p in range(4)])

        def fetch(slot, c):
            return fetch_tile(slot, g_index, first_tile + c)

        def fetch_first_two(group, block):
            tile = block * (block + 1) // 2 + block
            fetch_tile(0, group, tile).start()

            @pl.when(block >= 1)
            def _():
                fetch_tile(1, group, tile - 1).start()

        def sweep1(n, unused):
            c, slot = last - n, n % 3

            @pl.when(c >= 2)
            def _():
                fetch((n + 2) % 3, c - 2).start()

            fetch(slot, c).wait()
            off = pl.multiple_of(c * ch, ch)
            vt = v_ref[pl.ds(off, ch), :]
            dv = None
            for j in range(2):
                a = stage[slot, j]
                g_s[j, c] = (a.astype(F32) * _dot(dos_ln2[j], vt, "nt")).astype(BF16)
                beta_s[j, c] = stage[slot, 2 + j]
                dvj = _dot(a, dos[j], "tn")
                dv = dvj if dv is None else dv + dvj
            dv_ref[pl.ds(off, ch), :] += dv
            return unused

        @pl.when((g_index == 0) & (i == 0))
        def _():
            fetch_first_two(g_index, i)

        lax.fori_loop(0, last + 1, sweep1, 0)

        n_q = s // bq
        next_block = jnp.where(i + 1 < n_q, i + 1, 0)
        next_group = jnp.where(i + 1 < n_q, g_index, g_index + 1)

        @pl.when(next_group < GROUPS)
        def _():
            fetch_first_two(next_group, next_block)

        def sweep2(c, carry, masked):
            off = pl.multiple_of(c * ch, ch)
            kt = k_ref[pl.ds(off, ch), :]
            out, dk = [], None
            for j in range(2):
                before, dq = carry[j]
                g16, beta = g_s[j, c], beta_s[j, c].astype(F32)
                g = g16.astype(F32)
                in_front, before = _running_sums(g, (g16,), before, tri_before, False)
                dz = g * (1.0 - beta) - beta * in_front
                if masked:
                    dz = jnp.where(strict_mask(c), dz, 0.0)
                dz = dz.astype(BF16)
                dkj = _dot(dz, qms[j], "tn")
                dk = dkj if dk is None else dk + dkj
                out.append((before, dq + _dot(dz, kt, "nn")))
            dk_ref[pl.ds(off, ch), :] += dk
            return tuple(out)

        carry = lax.fori_loop(0, last, lambda c, cr: sweep2(c, cr, False),
                              tuple((jnp.zeros((bq, 1), F32), jnp.zeros((bq, LANES), F32)) for _ in range(2)))
        res = sweep2(last, carry, True)
        dq_ref[...] = _select_heads([dq for _, dq in res], lambda dq, j: dq) * dq_scale

    n_ch = s // ch
    return pl.pallas_call(
        body, name=name, grid=(GROUPS, s // bq), in_specs=[q_spec, k_spec, v_spec, row_out(LANES), _ANY, _ANY],
        out_specs=[row_out(LANES), key_out(LANES), key_out(LANES)],
        out_shape=[jax.ShapeDtypeStruct((s, GROUPS * LANES), F32)] * 3,
        scratch_shapes=[pltpu.VMEM((2, n_ch, bq, ch), BF16)] * 2 + [pltpu.VMEM((3, 4, bq, ch), BF16), pltpu.SemaphoreType.DMA((3, 4))],
        compiler_params=_cparams("arbitrary", "arbitrary"),
    )(qkv, qkv, qkv, do, saved, behind)


def _rope_tables(s):
    half = MLA_ROPE // 2
    freqs = ROPE_BASE ** (-jnp.arange(half, dtype=F32) / half)
    ang = jnp.arange(s, dtype=F32)[:, None] * freqs[None, :]
    cos, sin = jnp.cos(ang), jnp.sin(ang)
    tail = jnp.zeros((s, LANES - MLA_NOPE - MLA_ROPE), F32)
    lead = lambda fill: jnp.full((s, MLA_NOPE), fill, F32)
    return dict(
        cos_k0=jnp.concatenate([cos, cos, lead(0.0), tail], axis=1), sin_k0=jnp.concatenate([-sin, sin, lead(0.0), tail], axis=1),
        cos_k64=jnp.concatenate([lead(0.0), cos, cos, tail], axis=1), sin_k64=jnp.concatenate([lead(0.0), -sin, sin, tail], axis=1),
        cos_q=jnp.concatenate([lead(1.0), cos, cos, tail], axis=1),
        sin_k64_t=jnp.concatenate([lead(0.0), sin, -sin, tail], axis=1),
    )


def _local_step(x, mem, target, w, emit=lambda grads: [jnp.zeros((8, LANES), F32)]):
    s = x.shape[0]
    rope = _rope_tables(s)
    xb = x.astype(BF16)
    inv_d = 1.0 / D_MODEL
    scale_a = LOG2E / math.sqrt(MLA_NOPE + MLA_ROPE)
    scale_b = LOG2E / math.sqrt(SB_HEAD_DIM)
    scale_m = LOG2E / math.sqrt(MEM_HEAD_DIM)
    arrive_after = getattr(w, "arrive_after", lambda *values: None)
    memb = mem.astype(BF16)

    arrive_after(xb, memb, *rope.values())
    proj = _mm(xb, w["w_in"], "nn", name="proj", b_cols=(0, QKV_FIRST))
    one = jnp.ones((1, 512), F32)
    qkv = _mm(xb, w["w_in"], "nn", name="proj_qkv", b_cols=(QKV_FIRST, QKV_WIDTH), out_dtype=BF16,
              col_scale=jnp.concatenate([one * scale_b, one, one, one * scale_m], axis=1))
    arrive_after(qkv)
    pre = _mm(xb, w["w_merge_gate"], "nn", name="merge_pre", out_dtype=BF16)
    arrive_after(pre)

    def mla_inputs(c_q, c_kv, k_rope, g_q, g_kv, w_q, w_kv, cos_q, sin_q, cos_k, sin_k):
        n_q = (c_q * lax.rsqrt(jnp.mean(c_q * c_q, axis=1, keepdims=True) + RMS_EPS) * g_q).astype(BF16)
        n_kv = (c_kv * lax.rsqrt(jnp.mean(c_kv * c_kv, axis=1, keepdims=True) + RMS_EPS) * g_kv).astype(BF16)
        q_a = _dot(n_q, w_q, "nn")
        kv_a = _dot(n_kv, w_kv, "nn").astype(BF16)
        q = jnp.concatenate([(g * cos_q + _swap_halves(g, MLA_NOPE) * sin_q) * scale_a for g in _lane_groups(q_a)], axis=1)
        k_pe = pltpu.roll(k_rope * cos_k + _swap_halves(k_rope, 0) * sin_k, MLA_NOPE, axis=1).astype(BF16)
        k = jnp.concatenate([g + k_pe for g in _lane_groups(kv_a[:, :1024])], axis=1)
        return n_q, n_kv, q, k, kv_a[:, 1024:]

    n_q, n_kv, q_mla, k_mla, v_a = _rowwise(
        mla_inputs, [(proj, 256, COL_CQ), (proj, 128, COL_CKV), (proj, 128, COL_KROPE), w["q_a_gain"], w["kv_a_gain"],
                     ("whole", w["w_q_b"]), ("whole", w["w_kv_b"]), rope["cos_q"], rope["sin_k64"], rope["cos_k0"], rope["sin_k0"]],
        [(256, BF16), (128, BF16), (1024, BF16), (1024, BF16), (512, BF16)], name="mla_inputs", rows=s)
    o_a, lse_a = _softmax_fwd(q_mla, k_mla, v_a, hp=2, causal=True, name="mla_fwd")

    o_b, sb_saved = _sb_fwd(qkv, q0=COL_QB, k0=COL_KB, v0=COL_VB, name="sb_fwd")

    mem_kv =_mm(memb, w["w_mem_kv"], "nn", name="mem_kv", out_dtype=BF16)
    o_m, lse_m = _softmax_fwd(qkv, mem_kv, mem_kv, hp=1, causal=False, name="mem_fwd", q0=1536, v0=512, q_rows=MEM_Q_BLOCK)

    branches = ("mla", "sb", "mem")
    w_branch = jnp.stack([w[f"w_branch_{br}"] for br in branches])
    bias = w["b_merge_gate"]

    def head(oa, ob, om, ga, gb, gm, pa, pb, pm, ba, bb, bm, xv, tv, gain, bias_ln, w_b, w_o, w_g):
        us, ys, gs = [], [], []
        for n, (o, gate, p, b) in enumerate(((oa, ga, pa, ba), (ob, gb, pb, bb), (om, gm, pm, bm))):
            us.append((o * gate * _sigmoid(gate)).astype(BF16))
            ys.append(_dot(us[n], w_b[n], "nn"))
            gs.append(_sigmoid(p.astype(F32) + b))
        merged = (gs[0] * ys[0] + gs[1] * ys[1] + gs[2] * ys[2]).astype(BF16)
        z = DEEPNORM_ALPHA * xv + _dot(merged, w_o, "nn")
        zc = z - jnp.mean(z, axis=1, keepdims=True)
        rstd = lax.rsqrt(jnp.mean(zc * zc, axis=1, keepdims=True) + LN_EPS)
        xhat = zc * rstd
        err = xhat * gain + bias_ln - tv
        loss = 0.5 * jnp.sum(jnp.mean(err * err, axis=1, keepdims=True), axis=0, keepdims=True)
        dy = err * inv_d
        dxhat = dy * gain
        dz = rstd * (dxhat - jnp.mean(dxhat, axis=1, keepdims=True) - xhat * jnp.mean(dxhat * xhat, axis=1, keepdims=True))
        dz16 = dz.astype(BF16)
        dm = _dot(dz16, w_o, "nt")
        dpre = jnp.concatenate([dm * ys[n] * gs[n] * (1.0 - gs[n]) for n in range(3)], axis=1)
        dx = DEEPNORM_ALPHA * dz + _dot(dpre.astype(BF16), w_g, "nt")
        dys = [(dm * gs[n]).astype(BF16) for n in range(3)]
        d_os, d_gates = [], []
        for n, (o, gate) in enumerate(((oa, ga), (ob, gb), (om, gm))):
            du, sg = _dot(dys[n], w_b[n], "nt"), _sigmoid(gate)
            d_os.append(du * gate * sg)
            d_gates.append(du * o * sg * (1.0 + gate * (1.0 - sg)))
        return (us, merged, dx, dz16, _colsum(dy * xhat), _colsum(dy), jnp.broadcast_to(loss, (1, LANES)), dpre, _colsum(dpre),
                dys, *d_os, *d_gates)

    grads = {}
    (u, merged, dx, dzb, grads["ln_gain"], grads["ln_bias"], loss, dpre, grads["b_merge_gate"], dy, *rest) = _rowwise(
        head, [o_a, o_b, o_m, (proj, 512, COL_GATE_A), (proj, 512, COL_GATE_B), (proj, 512, COL_GATE_M),
               (pre, 1024, 0), (pre, 1024, 1024), (pre, 1024, 2048), (bias, 1024, 0), (bias, 1024, 1024), (bias, 1024, 2048),
               x, target, w["ln_gain"], w["ln_bias"], ("whole", w_branch), ("whole", w["w_out"]), ("whole", w["w_merge_gate"])],
        [(3, 512, BF16), (1024, BF16), (1024, F32), (1024, BF16), ("sum", 1024), ("sum", 1024), ("sum", LANES),
         (3072, BF16), ("sum", 3072), (3, 1024, BF16)] + [(512, F32)] * 3 + [(512, BF16)] * 3, name="head", rows=s, tr=256)
    d_o, d_gate = dict(zip(branches, rest[:3], strict=True)), dict(zip(branches, rest[3:], strict=True))

    grads["w_out"] = _mm(merged, dzb, "tn", name="g_w_out", out_dtype=BF16)
    grads["w_merge_gate"] = _mm(xb, dpre, "tn", name="g_w_merge", out_dtype=BF16)
    g_w_branch = _mm(u, dy, "tn", name="g_w_branch", out_dtype=BF16)
    for n, br in enumerate(branches):
        grads[f"w_branch_{br}"] = g_w_branch[n]
    (sent,) = emit({n: grads[n] for n in ("w_out", "w_merge_gate", "w_branch_mla", "w_branch_sb", "w_branch_mem")})

    dq_m, dk_m, dv_m = _softmax_bwd(qkv, mem_kv, mem_kv, o_m, d_o["mem"], lse_m, sent, hp=1, causal=False, dq_scale=scale_m,
                                    name="mem_bwd", q0=1536, v0=512, q_rows=MEM_Q_BLOCK)
    grads["w_mem_kv"] = _mm(memb, jnp.concatenate([dk_m, dv_m], axis=1), "tn", name="g_w_mem_kv", out_dtype=BF16)

    dq_sb, dk_sb, dv_sb = _sb_bwd(qkv, d_o["sb"], sb_saved, sent, q0=COL_QB, k0=COL_KB, v0=COL_VB, dq_scale=scale_b, name="sb_bwd")

    dq_mla, dk_mla, dv_a = _softmax_bwd(q_mla, k_mla, v_a, o_a, d_o["mla"], lse_a, sent, hp=2, causal=True, dq_scale=scale_a,
                                        name="mla_bwd")

    def mla_inputs_bwd(dq, dk, dv, n_q, n_kv, c_q, c_kv, g_q, g_kv, w_q, w_kv, cos_q, sin_q, cos_k, sin_k):
        dq_a = jnp.concatenate([g * cos_q + _swap_halves(g, MLA_NOPE) * sin_q for g in _lane_groups(dq)], axis=1).astype(BF16)
        groups = _lane_groups(dk)
        g_rope = groups[0]
        for other in groups[1:]:
            g_rope = g_rope + other
        dk_rope = pltpu.roll(g_rope * cos_k + _swap_halves(g_rope, MLA_NOPE) * sin_k, MLA_NOPE, axis=1)
        nope = _low_half(g_rope.shape)
        dkv_a = jnp.concatenate([jnp.where(nope, grp, 0.0) for grp in groups] + [dv], axis=1).astype(BF16)
        res = []
        for c, dn, g in ((c_q, _dot(dq_a, w_q, "nt"), g_q), (c_kv, _dot(dkv_a, w_kv, "nt"), g_kv)):
            r = lax.rsqrt(jnp.mean(c * c, axis=1, keepdims=True) + RMS_EPS)
            t = dn * g
            res += [r * t - c * (r * r * r) * jnp.mean(c * t, axis=1, keepdims=True), _colsum(dn * c * r)]
        return *res, dk_rope, _dot(n_q, dq_a, "tn"), _dot(n_kv, dkv_a, "tn")

    dc_q, grads["q_a_gain"], dc_kv, grads["kv_a_gain"], dk_rope, g_w_q_b, g_w_kv_b = _rowwise(
        mla_inputs_bwd, [dq_mla, dk_mla, dv_a, n_q, n_kv, (proj, 256, COL_CQ), (proj, 128, COL_CKV), w["q_a_gain"], w["kv_a_gain"],
                         ("whole", w["w_q_b"]), ("whole", w["w_kv_b"]), rope["cos_q"], rope["sin_k64_t"], rope["cos_k64"], rope["sin_k64_t"]],
        [(256, BF16), ("sum", 256), (128, BF16), ("sum", 128), (128, BF16), ("sum", (MLA_Q_LORA, 1024)), ("sum", (MLA_KV_LORA, 1536))],
        name="mla_inputs_bwd", rows=s)
    grads["w_q_b"], grads["w_kv_b"] = g_w_q_b.astype(BF16), g_w_kv_b.astype(BF16)

    sent = emit({n: grads[n] for n in ("w_mem_kv", "w_q_b", "w_kv_b")})

    dproj = jnp.concatenate(
        [dc_q, dc_kv, dk_rope, d_gate["mla"], d_gate["sb"], d_gate["mem"], dq_sb.astype(BF16), dk_sb.astype(BF16),
         dv_sb.astype(BF16), dq_m.astype(BF16)], axis=1)
    grads["w_in"] = _mm(xb, dproj, "tn", name="g_w_in", out_dtype=BF16, behind=sent)
    sent = emit({"w_in": grads["w_in"]})
    grad_x = _mm(dproj, w["w_in"], "nt", name="grad_x", add=dx, behind=sent)
    return loss, grad_x, grads


def _shard_shape(shape, axis):
    return tuple(d // N_DEV if a == axis else d for a, d in enumerate(shape))


def _from_blocks(blocks, name):
    shape, axis = SHARDED[name]
    return blocks.reshape(shape) if axis == 0 else blocks.transpose(1, 0, 2).reshape(shape)


def _to_blocks(full, name):
    shape, axis = SHARDED[name]
    shp = _shard_shape(shape, axis)
    return full.reshape(N_DEV, *shp) if axis == 0 else full.reshape(shape[0], N_DEV, shp[1]).transpose(1, 0, 2)


def _pad_heads(a, used):
    rows = a.shape[0]
    a = a.reshape(rows, MLA_HEADS, used)
    return jnp.concatenate([a, jnp.zeros((rows, MLA_HEADS, LANES - used), a.dtype)], axis=2).reshape(rows, MLA_HEADS * LANES)


def _to_kernel_layout(name, full):
    if name == "w_in":
        return jnp.concatenate([jnp.zeros((D_MODEL, IN_PAD), full.dtype) if piece is None else full[:, piece[0]:piece[0] + piece[1]]
                                for piece in IN_PIECES], axis=1)
    if name == "w_q_b":
        return _pad_heads(full, MLA_NOPE + MLA_ROPE)
    if name == "w_kv_b":
        kv = full.reshape(MLA_KV_LORA, MLA_HEADS, MLA_NOPE + MLA_V)
        return jnp.concatenate([_pad_heads(kv[:, :, :MLA_NOPE].reshape(MLA_KV_LORA, -1), MLA_NOPE),
                                kv[:, :, MLA_NOPE:].reshape(MLA_KV_LORA, -1)], axis=1)
    return full


def _from_kernel_layout(name, g):
    if name == "w_in":
        placed, at = [], 0
        for piece in IN_PIECES:
            if piece is not None:
                placed.append((piece[0], g[:, at:at + piece[1]]))
            at += IN_PAD if piece is None else piece[1]
        return jnp.concatenate([cols for _, cols in sorted(placed, key=lambda item: item[0])], axis=1)
    if name == "w_q_b":
        return g.reshape(MLA_Q_LORA, MLA_HEADS, LANES)[:, :, :MLA_NOPE + MLA_ROPE].reshape(MLA_Q_LORA, -1)
    if name == "w_kv_b":
        return jnp.concatenate([g[:, :1024].reshape(MLA_KV_LORA, MLA_HEADS, LANES)[:, :, :MLA_NOPE],
                                g[:, 1024:].reshape(MLA_KV_LORA, MLA_HEADS, MLA_V)], axis=2).reshape(MLA_KV_LORA, -1)
    return g


def _pack_small(vectors, loss=None):
    flat = [v.reshape(-1) for v in vectors]
    flat.append(jnp.zeros((SMALL_ROWS * SMALL_LANES - LOSS_INDEX,), F32) if loss is None else
                jnp.concatenate([loss.reshape(-1)[:1], jnp.zeros((SMALL_ROWS * SMALL_LANES - LOSS_INDEX - 1,), F32)]))
    return jnp.concatenate(flat).reshape(SMALL_ROWS, SMALL_LANES)


def _unpack_small(packed):
    flat, res, off = packed.reshape(-1), [], 0
    for _, n in SMALL:
        res.append(flat[off:off + n].reshape(1, n))
        off += n
    return res


def _me_and_peers():
    x, y, c = lax.axis_index("x"), lax.axis_index("y"), lax.axis_index("c")
    peers = []
    for kk in range(1, N_DEV):
        px, py, pc = (x + (kk >> 2)) % 2, (y + ((kk >> 1) & 1)) % 2, (c + (kk & 1)) % 2
        peers.append(((px, py, pc), 4 * px + 2 * py + pc))
    return 4 * x + 2 * y + c, peers


def _share_small(small, *, name):
    def body(small_ref, all_ref, send_sems, recv_sems, local_sem):
        me, peers = _me_and_peers()
        copies = [pltpu.make_async_remote_copy(src_ref=small_ref, dst_ref=all_ref.at[me], send_sem=send_sems.at[kk], recv_sem=recv_sems.at[kk],
                                               device_id=pos, device_id_type=pl.DeviceIdType.MESH) for kk, (pos, _) in enumerate(peers)]
        copies.append(pltpu.make_async_copy(small_ref, all_ref.at[me], local_sem))
        for cp in copies:
            cp.start()
        for cp in copies:
            cp.wait()

    hbm = pl.BlockSpec(memory_space=pl.ANY)
    return pl.pallas_call(
        body, name=name, in_specs=[hbm], out_specs=hbm, out_shape=jax.ShapeDtypeStruct((N_DEV, *small.shape), small.dtype),
        scratch_shapes=[pltpu.SemaphoreType.DMA((N_DEV - 1,)), pltpu.SemaphoreType.DMA((N_DEV - 1,)), pltpu.SemaphoreType.DMA],
        compiler_params=pltpu.CompilerParams(has_side_effects=True),
    )(small)


_HBM = pl.BlockSpec(memory_space=pltpu.HBM)
_SEM = pl.BlockSpec(memory_space=pltpu.SEMAPHORE)


def _exchange_copies(srcs, zones, send_sems, recv_sems, gather):
    me, peers = _me_and_peers()
    return [pltpu.make_async_remote_copy(
        src_ref=srcs[t] if gather else srcs[t].at[peer], dst_ref=zones[t].at[me], send_sem=send_sems.at[7 * t + kk],
        recv_sem=recv_sems.at[7 * t + kk], device_id=pos, device_id_type=pl.DeviceIdType.MESH)
        for t in range(len(srcs)) for kk, (pos, peer) in enumerate(peers)]


def _exchange_start(tensors, *, gather, name):
    n = len(tensors)
    zones = [lax.empty((N_DEV, *(t.shape if gather else t.shape[1:])), t.dtype) for t in tensors]

    def body(*refs):
        for cp in _exchange_copies(refs[:n], refs[n:2 * n], refs[2 * n], refs[2 * n + 1], gather):
            cp.start()
        refs[-1][...] = jnp.zeros_like(refs[-1])

    buffers = [pltpu.HBM(a.shape, a.dtype) for a in tensors + zones]
    res = pl.pallas_call(
        body, name=name, in_specs=[_HBM] * (2 * n),
        out_shape=(pltpu.SemaphoreType.DMA((7 * n,)), pltpu.SemaphoreType.DMA((7 * n,)), *buffers, jax.ShapeDtypeStruct((8, LANES), F32)),
        out_specs=(_SEM, _SEM, *[_HBM] * (2 * n), pl.BlockSpec(memory_space=pltpu.VMEM)),
        input_output_aliases={i: 2 + i for i in range(2 * n)},
        compiler_params=pltpu.CompilerParams(has_side_effects=pltpu.SideEffectType.DATAFLOW_SIDE_EFFECTING),
    )(*[pltpu.with_memory_space_constraint(a, pltpu.HBM) for a in tensors + zones])
    return dict(sems=res[:2], buffers=res[2:2 + 2 * n], gather=gather, started=res[-1])


def _exchange_wait(started, after, *, name):
    n = len(started["buffers"]) // 2

    def body(*refs):
        for cp in _exchange_copies(refs[:n], refs[n:2 * n], refs[2 * n], refs[2 * n + 1], started["gather"]):
            cp.wait_send()
            cp.wait_recv()

    res = pl.pallas_call(
        body, name=name, in_specs=[_HBM] * (2 * n) + [_SEM, _SEM] + [_ANY] * len(after),
        out_shape=tuple(pltpu.HBM(a.shape, a.dtype) for a in started["buffers"]), out_specs=tuple([_HBM] * (2 * n)),
        input_output_aliases={i: i for i in range(2 * n)},
        compiler_params=pltpu.CompilerParams(has_side_effects=pltpu.SideEffectType.DATAFLOW_SIDE_EFFECTING),
    )(*started["buffers"], *started["sems"], *after)
    return res[:n], res[n:]


def _adamw(contrib, w, m, v, *, name):
    rows, cols = w.shape
    tile = min(rows, ADAM_ROWS)

    def body(c_ref, w_ref, m_ref, v_ref, g_ref, d_ref, nm_ref, nv_ref):
        g = c_ref[0].astype(F32)
        for s in range(1, N_DEV):
            g = g + c_ref[s].astype(F32)
        m_new = ADAM_B1 * m_ref[...] + (1.0 - ADAM_B1) * g
        v_new = ADAM_B2 * v_ref[...] + (1.0 - ADAM_B2) * (g * g)
        m_hat = m_new / (1.0 - ADAM_B1 ** ADAM_STEP)
        v_hat = v_new / (1.0 - ADAM_B2 ** ADAM_STEP)
        g_ref[...] = g
        d_ref[...] = -ADAM_LR * (m_hat / (jnp.sqrt(v_hat) + ADAM_EPS) + ADAM_WD * w_ref[...])
        nm_ref[...] = m_new
        nv_ref[...] = v_new

    spec = pl.BlockSpec((tile, cols), lambda i: (i, 0))
    return pl.pallas_call(
        body, name=name, grid=(rows // tile,),
        in_specs=[pl.BlockSpec((N_DEV, tile, cols), lambda i: (0, i, 0)), spec, spec, spec], out_specs=[spec] * 4,
        out_shape=[jax.ShapeDtypeStruct((rows, cols), F32)] * 4, compiler_params=_cparams("parallel"),
    )(contrib, w, m, v)


class _Weights:
    def __init__(self, gathers, vectors, me):
        self.gathers, self.ready, self.me, self.after = gathers, dict(vectors), me, ()

    def arrive_after(self, *values):
        self.after = values

    def __getitem__(self, name):
        if name not in self.ready:
            gi = next(i for i, group in enumerate(GATHER_GROUPS) if name in group)
            after = [*self.after, *[g["started"] for g in self.gathers]]
            shards, zones = _exchange_wait(self.gathers[gi], after, name=f"gather_wait_{gi}")
            for n, shard, zone in zip(GATHER_GROUPS[gi], shards, zones, strict=True):
                blocks = lax.dynamic_update_slice_in_dim(zone, shard[None], self.me, 0)
                self.ready[n] = _to_kernel_layout(n, _from_blocks(blocks, n))
        return self.ready[name]


def kernel(x, mem, w_in, w_mem_kv, q_a_gain, w_q_b, kv_a_gain, w_kv_b, w_branch_mla, w_branch_sb, w_branch_mem, w_merge_gate, b_merge_gate, w_out, ln_gain, ln_bias, loss_target, m_w_in, m_w_mem_kv, m_q_a_gain, m_w_q_b, m_kv_a_gain, m_w_kv_b, m_w_branch_mla, m_w_branch_sb, m_w_branch_mem, m_w_merge_gate, m_b_merge_gate, m_w_out, m_ln_gain, m_ln_bias, v_w_in, v_w_mem_kv, v_q_a_gain, v_w_q_b, v_kv_a_gain, v_w_kv_b, v_w_branch_mla, v_w_branch_sb, v_w_branch_mem, v_w_merge_gate, v_b_merge_gate, v_w_out, v_ln_gain, v_ln_bias):
    given = dict(locals())
    small_names = [n for n, _ in SMALL]
    smalls = lambda prefix: [given[prefix + n] for n in small_names]
    me = 4 * lax.axis_index("x") + 2 * lax.axis_index("y") + lax.axis_index("c")

    gathers = [_exchange_start([given[n][0].astype(BF16) for n in group], gather=True, name=f"gather_start_{gi}")
               for gi, group in enumerate(GATHER_GROUPS)]
    w = _Weights(gathers, {n: given[n] for n in small_names}, me)
    exchanges = []
    results = [{}, {}, {}, {}]

    def finish(gi, after):
        names, started = exchanges[gi]
        sent, zones = _exchange_wait(started, after, name=f"grads_wait_{gi}")
        done = []
        for n, blocks, zone in zip(names, sent, zones, strict=True):
            own = lax.dynamic_index_in_dim(blocks, me, 0, keepdims=True)
            contrib = lax.dynamic_update_slice_in_dim(zone, own, me, 0)
            outs = _adamw(contrib, given[n][0], given["m_" + n][0], given["v_" + n][0], name=f"adamw_{n}")
            for kind, res in zip(results, outs, strict=True):
                kind[n] = res[None]
            done.append(outs[1])
        return done

    def emit(grads):
        blocks = [_to_blocks(_from_kernel_layout(n, g), n).astype(BF16) for n, g in grads.items()]
        exchanges.append((tuple(grads), _exchange_start(blocks, gather=False, name=f"grads_start_{len(exchanges)}")))
        started = [exchanges[-1][1]["started"]]
        if len(exchanges) == len(GRAD_GROUPS):
            for gi in range(len(GRAD_GROUPS) - 1):
                started += finish(gi, started[:1])
        return started

    loss, grad_x, grads = _local_step(x[0], mem[0], loss_target[0], w, emit)

    contrib_small = _share_small(_pack_small([grads[n] for n in small_names], loss), name="share_small")
    sml = _adamw(contrib_small, _pack_small(smalls("")), _pack_small(smalls("m_")), _pack_small(smalls("v_")), name="adamw_small")
    for kind, packed in zip(results, sml, strict=True):
        kind.update(zip(small_names, _unpack_small(packed), strict=True))
    finish(len(GRAD_GROUPS) - 1, [grad_x])
    order = ["w_in", "w_mem_kv", "q_a_gain", "w_q_b", "kv_a_gain", "w_kv_b", "w_branch_mla", "w_branch_sb", "w_branch_mem",
             "w_merge_gate", "b_merge_gate", "w_out", "ln_gain", "ln_bias"]
    loss_out = sml[0].reshape(-1)[LOSS_INDEX]
    return (loss_out, grad_x[None], *[kind[n] for kind in results for n in order])
```

```python
import math

import jax
import jax.numpy as jnp
from jax import lax
from jax.experimental import pallas as pl
from jax.experimental.pallas import tpu as pltpu

F32, BF16 = jnp.float32, jnp.bfloat16

N_DEV = 8
D_MODEL = 1024
MLA_HEADS, MLA_NOPE, MLA_ROPE, MLA_V = 8, 64, 32, 64
MLA_Q_LORA, MLA_KV_LORA = 256, 128
SB_HEAD_DIM = 64
MEM_HEAD_DIM = 128
ROPE_BASE = 10000.0
RMS_EPS = 1e-6
LN_EPS = 1e-5
DEEPNORM_ALPHA = 2.0 ** 0.25
ADAM_LR, ADAM_B1, ADAM_B2, ADAM_EPS, ADAM_WD, ADAM_STEP = 0.001, 0.9, 0.999, 1e-08, 0.01, 10
LOG2E, LN2 = math.log2(math.e), math.log(2.0)

LANES = 128
GROUPS = 4
PROJ_WIDTH = 4096
COL_CQ, COL_CKV, COL_KROPE, COL_GATE_A, COL_GATE_B, COL_GATE_M = 0, 256, 384, 512, 1024, 1536
QKV_FIRST, QKV_WIDTH = 2048, 2048
COL_QB, COL_KB, COL_VB, COL_QM = 0, 512, 1024, 1536
IN_PIECES = ((0, 416), None, (416, 512), (2464, 512), (3488, 512), (928, 512), (1440, 512), (1952, 512), (2976, 512))
IN_PAD = 96

VMEM_LIMIT_BYTES = 56 * 1024 * 1024
NEG_BIG = -1e30
Q_BLOCK = 512
MEM_Q_BLOCK = 2048
SB_BWD_Q_BLOCK = 512
TRI_BLOCK = 256
TILE_ROWS = 64
KEY_CHUNK = 512

SHARDED = {
    "w_in": ((1024, 4000), 1), "w_mem_kv": ((1024, 1024), 0), "w_q_b": ((256, 768), 1), "w_kv_b": ((128, 1024), 1),
    "w_branch_mla": ((512, 1024), 1), "w_branch_sb": ((512, 1024), 1), "w_branch_mem": ((512, 1024), 1),
    "w_merge_gate": ((1024, 3072), 1), "w_out": ((1024, 1024), 0),
}
GATHER_GROUPS = (("w_in",), ("w_merge_gate",), ("w_q_b", "w_kv_b", "w_mem_kv", "w_branch_mla", "w_branch_sb", "w_branch_mem", "w_out"))
GRAD_GROUPS = (("w_out", "w_merge_gate", "w_branch_mla", "w_branch_sb", "w_branch_mem"), ("w_mem_kv", "w_q_b", "w_kv_b"), ("w_in",))
SMALL = (("q_a_gain", 256), ("kv_a_gain", 128), ("b_merge_gate", 3072), ("ln_gain", 1024), ("ln_bias", 1024))
SMALL_ROWS, SMALL_LANES = 48, 128
ADAM_ROWS = 256
LOSS_INDEX = 5504


def _cparams(*sem):
    return pltpu.CompilerParams(dimension_semantics=sem or None, vmem_limit_bytes=VMEM_LIMIT_BYTES)


_DIMS = {"nn": (((1,), (0,)), ((), ())), "nt": (((1,), (1,)), ((), ())), "tn": (((0,), (0,)), ((), ()))}


def _dot(a, b, dims):
    return lax.dot_general(a, b, _DIMS[dims], preferred_element_type=F32)


def _tile(dim, want):
    if dim <= want:
        return dim
    t = want - want % LANES
    while dim % t:
        t -= LANES
    assert t > 0, (dim, want)
    return t


_ANY = pl.BlockSpec(memory_space=pl.ANY)


def _mm(a, b, dims, *, name, out_dtype=F32, add=None, add_scale=1.0, col_scale=None, b_cols=None, behind=None,
        tm=1024, tn=1024, tk=1024):
    batch = a.shape[0] if a.ndim == 3 else None
    if dims == "nn":
        (m, k), (k2, n) = a.shape[-2:], b.shape[-2:]
    elif dims == "nt":
        (m, k), (n, k2) = a.shape[-2:], b.shape[-2:]
    else:
        (k, m), (k2, n) = a.shape[-2:], b.shape[-2:]
    assert k == k2 and a.ndim == b.ndim, (a.shape, b.shape, dims)
    assert batch is None or (b.shape[0] == batch and add is None and col_scale is None and b_cols is None)
    b_first = 0
    if b_cols is not None:
        assert dims == "nn"
        b_first, n = b_cols
    tm, tn, tk = _tile(m, tm), _tile(n, tn), _tile(k, tk)
    assert b_first % tn == 0
    jb = b_first // tn
    nk = k // tk

    def spec(block, index):
        if batch is None:
            return pl.BlockSpec(block, lambda bb, i, j, kk: index(i, j, kk))
        return pl.BlockSpec((None, *block), lambda bb, i, j, kk: (bb, *index(i, j, kk)))

    a_spec = spec((tk, tm), lambda i, j, kk: (kk, i)) if dims == "tn" else spec((tm, tk), lambda i, j, kk: (i, kk))
    b_spec = spec((tn, tk), lambda i, j, kk: (j, kk)) if dims == "nt" else spec((tk, tn), lambda i, j, kk: (kk, jb + j))
    o_spec = spec((tm, tn), lambda i, j, kk: (i, j))
    behind = [] if behind is None else behind if isinstance(behind, (list, tuple)) else [behind]
    optional = [(add, o_spec), (col_scale, pl.BlockSpec((1, tn), lambda bb, i, j, kk: (0, j))), *[(v, _ANY) for v in behind]]
    present = [(v, spec) for v, spec in optional if v is not None]

    def body(*refs):
        a_ref, b_ref = refs[:2]
        extra = iter(refs[2:2 + len(present)])
        add_ref = next(extra) if add is not None else None
        scale_ref = next(extra) if col_scale is not None else None
        o_ref = refs[2 + len(present)]
        part = _dot(a_ref[...].astype(BF16), b_ref[...].astype(BF16), dims)

        def finish(r):
            if add is not None:
                r = r + add_scale * add_ref[...]
            if col_scale is not None:
                r = r * scale_ref[...]
            o_ref[...] = r.astype(out_dtype)

        if nk == 1:
            finish(part)
            return
        acc = refs[-1]
        kk = pl.program_id(3)

        @pl.when(kk == 0)
        def _():
            acc[...] = part

        @pl.when(kk > 0)
        def _():
            acc[...] += part

        @pl.when(kk == nk - 1)
        def _():
            finish(acc[...])

    return pl.pallas_call(
        body, name=name, grid=(batch or 1, m // tm, n // tn, nk),
        in_specs=[a_spec, b_spec] + [spec for _, spec in present], out_specs=o_spec,
        out_shape=jax.ShapeDtypeStruct((m, n) if batch is None else (batch, m, n), out_dtype),
        scratch_shapes=[pltpu.VMEM((tm, tn), F32)] if nk > 1 else [],
        compiler_params=_cparams("parallel", "parallel", "parallel", "arbitrary"),
    )(a, b, *[v for v, _ in present])


def _rowwise(fn, ins, outs, *, name, rows, tr=512):
    n_in = len(ins)
    tr = min(tr, rows)
    in_specs, args = [], []
    for it in ins:
        if isinstance(it, tuple) and it[0] == "whole":
            in_specs.append(pl.BlockSpec(it[1].shape, lambda i, nd=it[1].ndim: (0,) * nd))
            args.append(it[1])
            continue
        arr, w, off = it if isinstance(it, tuple) else (it, it.shape[-1], 0)
        assert off % w == 0
        cb = off // w
        if arr.ndim == 3:
            in_specs.append(pl.BlockSpec((arr.shape[0], tr, w), lambda i, cb=cb: (0, i, cb)))
        elif arr.shape[0] == 1:
            in_specs.append(pl.BlockSpec((1, w), lambda i, cb=cb: (0, cb)))
        else:
            in_specs.append(pl.BlockSpec((tr, w), lambda i, cb=cb: (i, cb)))
        args.append(arr)
    out_shape, out_specs, is_sum = [], [], []
    for out in outs:
        is_sum.append(out[0] == "sum")
        if out[0] == "sum":
            shape = out[1] if isinstance(out[1], tuple) else (1, out[1])
            out_shape.append(jax.ShapeDtypeStruct(shape, F32))
            out_specs.append(pl.BlockSpec(shape, lambda i: (0, 0)))
        elif len(out) == 3:
            out_shape.append(jax.ShapeDtypeStruct((out[0], rows, out[1]), out[2]))
            out_specs.append(pl.BlockSpec((out[0], tr, out[1]), lambda i: (0, i, 0)))
        else:
            out_shape.append(jax.ShapeDtypeStruct((rows, out[0]), out[1]))
            out_specs.append(pl.BlockSpec((tr, out[0]), lambda i: (i, 0)))

    def body(*refs):
        res = fn(*[r[...] for r in refs[:n_in]])
        for r, val, s in zip(refs[n_in:], res, is_sum, strict=True):
            if s:
                @pl.when(pl.program_id(0) == 0)
                def _(r=r):
                    r[...] = jnp.zeros_like(r)

                r[...] += val
            elif isinstance(val, (list, tuple)):
                for n, part in enumerate(val):
                    r[n] = part.astype(r.dtype)
            else:
                r[...] = val.astype(r.dtype)

    return pl.pallas_call(
        body, name=name, grid=(rows // tr,), in_specs=in_specs, out_specs=out_specs, out_shape=out_shape,
        compiler_params=_cparams("arbitrary"),
    )(*args)


def _colsum(v):
    return jnp.sum(v, axis=0, keepdims=True)


def _sigmoid(v):
    return 1.0 / (1.0 + jnp.exp(-v))


def _lane_groups(v):
    return [v[:, g * LANES:(g + 1) * LANES] for g in range(v.shape[1] // LANES)]


def _swap_halves(v, first_lane):
    lane = lax.broadcasted_iota(jnp.int32, v.shape, 1)
    return jnp.where(lane < first_lane + 16, pltpu.roll(v, 112, axis=1), pltpu.roll(v, 16, axis=1))


def _lane_sum(acc, v):
    for part in _lane_groups(v):
        acc = acc + part
    return acc


def _low_half(shape):
    return lax.broadcasted_iota(jnp.int32, shape, 1) < LANES // 2


def _select_heads(per_head, pick):
    if len(per_head) == 1:
        return pick(per_head[0], 0)
    return jnp.where(_low_half(per_head[0].shape), pick(per_head[0], 0), pick(per_head[1], 1))


def _attn_specs(s, sk, hp, bq, q0, k0, v0):
    wq = hp * LANES
    assert q0 % wq == 0 and k0 % wq == 0 and v0 % LANES == 0
    qb0, kb0, vb0 = q0 // wq, k0 // wq, v0 // LANES
    q_spec = pl.BlockSpec((bq, wq), lambda g, i: (i, qb0 + g))
    k_spec = pl.BlockSpec((sk, wq), lambda g, i: (0, kb0 + g))
    v_spec = pl.BlockSpec((sk, LANES), lambda g, i: (0, vb0 + g))
    row_out = lambda w: pl.BlockSpec((bq, w), lambda g, i: (i, g))
    key_out = lambda w: pl.BlockSpec((sk, w), lambda g, i: (0, g))
    return q_spec, k_spec, v_spec, row_out, key_out


def _chunks(i, bq, ch, sk, causal):
    return ((i + 1) * bq - 1) // ch if causal else jnp.int32(sk // ch - 1)


def _positions(i, c, bq, ch):
    return (i * bq + lax.broadcasted_iota(jnp.int32, (bq, ch), 0), c * ch + lax.broadcasted_iota(jnp.int32, (bq, ch), 1))


def _softmax_fwd(q, k, v, *, hp, causal, name, q0=0, k0=0, v0=0, q_rows=Q_BLOCK):
    s, sk = q.shape[0], k.shape[0]
    bq, ch = min(q_rows, s), min(KEY_CHUNK, sk)
    assert not causal or bq <= ch
    q_spec, k_spec, v_spec, row_out, _ = _attn_specs(s, sk, hp, bq, q0, k0, v0)

    def body(q_ref, k_ref, v_ref, o_ref, lse_ref, s_scr):
        i = pl.program_id(1)
        qs = _lane_groups(q_ref[...])
        last = _chunks(i, bq, ch, sk, causal)

        def scores(c, ms, masked):
            off = pl.multiple_of(c * ch, ch)
            out = []
            for j in range(hp):
                sc = _dot(qs[j], k_ref[pl.ds(off, ch), j * LANES:(j + 1) * LANES], "nt")
                if masked:
                    qpos, kpos = _positions(i, c, bq, ch)
                    sc = jnp.where(kpos <= qpos, sc, NEG_BIG)
                s_scr[j, c] = sc
                m = ms[j]
                for part in _lane_groups(sc):
                    m = jnp.maximum(m, part)
                out.append(m)
            return tuple(out)

        ms = lax.fori_loop(0, last, lambda c, m: scores(c, m, False), tuple(jnp.full((bq, LANES), NEG_BIG, F32) for _ in range(hp)))
        ms = scores(last, ms, causal)
        row_max = [jnp.max(m, axis=1, keepdims=True) for m in ms]

        def weigh(c, carry):
            off = pl.multiple_of(c * ch, ch)
            vt = v_ref[pl.ds(off, ch), :]
            out = []
            for j in range(hp):
                l, acc = carry[j]
                p = jnp.exp2(s_scr[j, c] - row_max[j])
                out.append((_lane_sum(l, p), acc + _dot(p.astype(BF16), vt, "nn")))
            return tuple(out)

        zero = jnp.zeros((bq, LANES), F32)
        res = lax.fori_loop(0, last + 1, weigh, tuple((zero, zero) for _ in range(hp)))
        row_sum = [jnp.sum(l, axis=1, keepdims=True) for l, _ in res]
        o_ref[...] = _select_heads([acc for _, acc in res], lambda acc, j: acc / row_sum[j])
        lse_ref[...] = _select_heads([jnp.broadcast_to(row_max[j] + jnp.log2(row_sum[j]), (bq, LANES)) for j in range(hp)], lambda a, j: a)

    return pl.pallas_call(
        body, name=name, grid=(GROUPS, s // bq), in_specs=[q_spec, k_spec, v_spec], out_specs=[row_out(LANES), row_out(LANES)],
        out_shape=[jax.ShapeDtypeStruct((s, GROUPS * LANES), F32)] * 2,
        scratch_shapes=[pltpu.VMEM((hp, sk // ch, bq, ch), F32)], compiler_params=_cparams("parallel", "arbitrary"),
    )(q, k, v)


def _head_cotangent(do, j, hp):
    if hp == 1:
        return do
    return jnp.where(_low_half(do.shape) == (j == 0), do, 0.0)


def _softmax_bwd(q, k, v, o, do, lse, behind, *, hp, causal, dq_scale, name, q0=0, k0=0, v0=0, q_rows=Q_BLOCK):
    s, sk = q.shape[0], k.shape[0]
    bq, ch = min(q_rows, s), min(KEY_CHUNK, sk)
    assert not causal or bq <= ch
    wq = hp * LANES
    q_spec, k_spec, v_spec, row_out, key_out = _attn_specs(s, sk, hp, bq, q0, k0, v0)

    def body(q_ref, k_ref, v_ref, o_ref, do_ref, lse_ref, _, dq_ref, dk_ref, dv_ref, dk_t, dv_t):
        i = pl.program_id(1)

        @pl.when(i == 0)
        def _():
            dk_t[...] = jnp.zeros_like(dk_t)
            dv_t[...] = jnp.zeros_like(dv_t)

        qs = _lane_groups(q_ref[...])
        do_all, o_all, lse_all = do_ref[...], o_ref[...], lse_ref[...]
        dos, deltas, lses = [], [], []
        for j in range(hp):
            d = _head_cotangent(do_all, j, hp)
            deltas.append(jnp.sum(d * o_all, axis=1, keepdims=True))
            dos.append(d.astype(BF16))
            lses.append(lse_all[:, j * (LANES // hp):j * (LANES // hp) + 1])
        last = _chunks(i, bq, ch, sk, causal)

        def chunk(c, dqs, masked):
            off = pl.multiple_of(c * ch, ch)
            vt = v_ref[pl.ds(off, ch), :]
            out, dks, dv = [], [], None
            for j in range(hp):
                kt = k_ref[pl.ds(off, ch), j * LANES:(j + 1) * LANES]
                p = jnp.exp2(_dot(qs[j], kt, "nt") - lses[j])
                if masked:
                    qpos, kpos = _positions(i, c, bq, ch)
                    p = jnp.where(kpos <= qpos, p, 0.0)
                ds = (p * (_dot(dos[j], vt, "nt") - deltas[j]) * LN2).astype(BF16)
                out.append(dqs[j] + _dot(ds, kt, "nn"))
                dks.append(_dot(qs[j], ds, "tn"))
                dvj = _dot(dos[j], p.astype(BF16), "tn")
                dv = dvj if dv is None else dv + dvj
            dk_t[c] += dks[0] if hp == 1 else jnp.concatenate(dks, axis=0)
            dv_t[c] += dv
            return tuple(out)

        dqs = lax.fori_loop(0, last, lambda c, d: chunk(c, d, False), tuple(jnp.zeros((bq, LANES), F32) for _ in range(hp)))
        dqs = chunk(last, dqs, causal)
        dq_ref[...] = (dqs[0] if hp == 1 else jnp.concatenate(dqs, axis=1)) * dq_scale

        @pl.when(i == s // bq - 1)
        def _():
            for c in range(sk // ch):
                dk_ref[c * ch:(c + 1) * ch, :] = dk_t[c].T
                dv_ref[c * ch:(c + 1) * ch, :] = dv_t[c].T

    return pl.pallas_call(
        body, name=name, grid=(GROUPS, s // bq),
        in_specs=[q_spec, k_spec, v_spec, row_out(LANES), row_out(LANES), row_out(LANES), _ANY],
        out_specs=[row_out(wq), key_out(wq), key_out(LANES)],
        out_shape=[jax.ShapeDtypeStruct((s, GROUPS * wq), F32), jax.ShapeDtypeStruct((sk, GROUPS * wq), F32),
                   jax.ShapeDtypeStruct((sk, GROUPS * LANES), F32)],
        scratch_shapes=[pltpu.VMEM((sk // ch, wq, ch), F32), pltpu.VMEM((sk // ch, LANES, ch), F32)],
        compiler_params=_cparams("arbitrary", "arbitrary"),
    )(q, k, v, o, do, lse, behind)


def _log2_sigmoid_pair(z2):
    minus_abs = lax.bitcast_convert_type(lax.bitcast_convert_type(z2, jnp.uint32) | jnp.uint32(0x80000000), F32)
    log_beta = jnp.minimum(z2, 0.0) - jnp.log2(1.0 + jnp.exp2(minus_abs))
    return log_beta, log_beta - z2


def _tilewise(fn, *arrays):
    rows, cols = arrays[0].shape
    step = min(TILE_ROWS, rows)
    grid = [[fn(*[None if a is None else a[r:r + step, c:c + LANES] for a in arrays]) for c in range(0, cols, LANES)]
            for r in range(0, rows, step)]
    return [jnp.concatenate([jnp.concatenate([cell[k] for cell in row], axis=1) for row in grid], axis=0)
            for k in range(len(grid[0][0]))]


def _split(v):
    hi = v.astype(BF16)
    return hi, (v - hi.astype(F32)).astype(BF16)


def _tri(n, after):
    rows, cols = lax.broadcasted_iota(jnp.int32, (n, n), 0), lax.broadcasted_iota(jnp.int32, (n, n), 1)
    return (rows > cols if after else rows < cols).astype(BF16)


def _running_sums(v, terms, start, tri, backwards):
    n = tri.shape[0]
    n_blocks = v.shape[1] // n
    order = range(n_blocks - 1, -1, -1) if backwards else range(n_blocks)
    stacked = tri if len(terms) == 1 else jnp.concatenate([tri] * len(terms), axis=0)
    parts, run = [None] * n_blocks, start
    for t in order:
        cols = slice(t * n, (t + 1) * n)
        lhs = terms[0][:, cols] if len(terms) == 1 else jnp.concatenate([term[:, cols] for term in terms], axis=1)
        parts[t] = _dot(lhs, stacked, "nn") + run
        run = run + jnp.sum(v[:, cols], axis=1, keepdims=True)
    return (parts[0] if n_blocks == 1 else jnp.concatenate(parts, axis=1)), run


def _sb_weights(qm, kt, run, tri, strict):
    def logs(z2, keep):
        log_beta, log_keep = _log2_sigmoid_pair(z2)
        if keep is not None:
            log_keep = jnp.where(keep, log_keep, 0.0)
        return log_beta, log_keep, *_split(log_keep)

    log_beta, log_keep, hi, lo = _tilewise(logs, _dot(qm, kt, "nt"), strict)
    behind, run = _running_sums(log_keep, (hi, lo), run, tri, True)

    def weigh(log_beta, behind, keep):
        a = jnp.exp2(log_beta + behind)
        return (a if keep is None else jnp.where(keep, a, 0.0),)

    (a,) = _tilewise(weigh, log_beta, behind, strict)
    return a, log_beta, run


class _Copies:
    def __init__(self, copies):
        self.copies = copies

    def start(self):
        for cp in self.copies:
            cp.start()

    def wait(self):
        for cp in self.copies:
            cp.wait()


def _sb_queries(q_all):
    low = _low_half(q_all.shape)
    zero = jnp.zeros_like(q_all)
    return [jnp.where(low, q_all, zero), jnp.where(low, zero, q_all)]


def _sb_fwd(qkv, *, q0, k0, v0, name):
    s = qkv.shape[0]
    bq, ch = min(Q_BLOCK, s), min(KEY_CHUNK, s)
    assert bq == ch
    n_q = s // bq
    q_spec, k_spec, v_spec, row_out, _ = _attn_specs(s, s, 1, bq, q0, k0, v0)

    def body(q_ref, k_ref, v_ref, o_ref, saved_ref, stage, sems):
        g, i = pl.program_id(0), pl.program_id(1)
        qms = _sb_queries(q_ref[...])
        tri = _tri(min(TRI_BLOCK, ch), True)
        last = _chunks(i, bq, ch, s, True)
        first_tile = i * (i + 1) // 2
        steps_before = g * (n_q * (n_q + 1) // 2) + first_tile

        def save(slot, c):
            return _Copies([pltpu.make_async_copy(stage.at[slot, p], saved_ref.at[g, first_tile + c, p], sems.at[slot, p])
                            for p in range(4)])

        def chunk(c, step, carry, masked):
            off = pl.multiple_of(c * ch, ch)
            kt, vt = k_ref[pl.ds(off, ch), :], v_ref[pl.ds(off, ch), :]
            strict = None
            if masked:
                qpos, kpos = _positions(i, c, bq, ch)
                strict = kpos < qpos
            slot = (steps_before + step) % 3

            @pl.when(steps_before + step >= 3)
            def _():
                save(slot, c).wait()

            out = []
            for j in range(2):
                run, acc = carry[j]
                a, log_beta, run = _sb_weights(qms[j], kt, run, tri, strict)
                a = a.astype(BF16)
                stage[slot, j] = a
                stage[slot, 2 + j] = jnp.exp2(log_beta).astype(BF16)
                out.append((run, acc + _dot(a, vt, "nn")))
            save(slot, c).start()
            return tuple(out)

        carry = chunk(last, 0, tuple((jnp.zeros((bq, 1), F32), jnp.zeros((bq, LANES), F32)) for _ in range(2)), True)
        res = lax.fori_loop(0, last, lambda n, c: chunk(last - 1 - n, n + 1, c, False), carry)
        @pl.when((g == GROUPS - 1) & (i == n_q - 1))
        def _():
            for back in range(3):
                save((steps_before + last - back) % 3, 0).wait()

        o_ref[...] = _select_heads([acc for _, acc in res], lambda acc, j: acc)

    return pl.pallas_call(
        body, name=name, grid=(GROUPS, n_q), in_specs=[q_spec, k_spec, v_spec], out_specs=[row_out(LANES), _ANY],
        out_shape=[jax.ShapeDtypeStruct((s, GROUPS * LANES), F32),
                   jax.ShapeDtypeStruct((GROUPS, n_q * (n_q + 1) // 2, 4, bq, ch), BF16)],
        scratch_shapes=[pltpu.VMEM((3, 4, bq, ch), BF16), pltpu.SemaphoreType.DMA((3, 4))],
        compiler_params=_cparams("arbitrary", "arbitrary"),
    )(qkv, qkv, qkv)


def _sb_bwd(qkv, do, saved, behind, *, q0, k0, v0, dq_scale, name):
    s = qkv.shape[0]
    bq, ch = min(SB_BWD_Q_BLOCK, s), min(KEY_CHUNK, s)
    assert bq == ch and saved.shape[2:] == (4, bq, ch)
    q_spec, k_spec, v_spec, row_out, key_out = _attn_specs(s, s, 1, bq, q0, k0, v0)

    def body(q_ref, k_ref, v_ref, do_ref, saved_ref, _, dq_ref, dk_ref, dv_ref, g_s, beta_s, stage, sems):
        g_index, i = pl.program_id(0), pl.program_id(1)

        @pl.when(i == 0)
        def _():
            dk_ref[...] = jnp.zeros_like(dk_ref)
            dv_ref[...] = jnp.zeros_like(dv_ref)

        qms = _sb_queries(q_ref[...])
        do_all = do_ref[...]
        dos = [_head_cotangent(do_all, j, 2).astype(BF16) for j in range(2)]
        dos_ln2 = [(_head_cotangent(do_all, j, 2) * LN2).astype(BF16) for j in range(2)]
        tri_before = _tri(min(TRI_BLOCK, ch), False)
        last = _chunks(i, bq, ch, s, True)
        first_tile = i * (i + 1) // 2

        def strict_mask(c):
            qpos, kpos = _positions(i, c, bq, ch)
            return kpos < qpos

        def fetch_tile(slot, group, tile):
            return _Copies([pltpu.make_async_copy(saved_ref.at[group, tile, p], stage.at[slot, p], sems.at[slot, p])
                            for p in range(4)])

        def fetch(slot, c):
            return fetch_tile(slot, g_index, first_tile + c)

        def fetch_first_two(group, block):
            tile = block * (block + 1) // 2 + block
            fetch_tile(0, group, tile).start()

            @pl.when(block >= 1)
            def _():
                fetch_tile(1, group, tile - 1).start()

        def sweep1(n, unused):
            c, slot = last - n, n % 3

            @pl.when(c >= 2)
            def _():
                fetch((n + 2) % 3, c - 2).start()

            fetch(slot, c).wait()
            off = pl.multiple_of(c * ch, ch)
            vt = v_ref[pl.ds(off, ch), :]
            dv = None
            for j in range(2):
                a = stage[slot, j]
                g_s[j, c] = (a.astype(F32) * _dot(dos_ln2[j], vt, "nt")).astype(BF16)
                beta_s[j, c] = stage[slot, 2 + j]
                dvj = _dot(a, dos[j], "tn")
                dv = dvj if dv is None else dv + dvj
            dv_ref[pl.ds(off, ch), :] += dv
            return unused

        @pl.when((g_index == 0) & (i == 0))
        def _():
            fetch_first_two(g_index, i)

        lax.fori_loop(0, last + 1, sweep1, 0)

        n_q = s // bq
        next_block = jnp.where(i + 1 < n_q, i + 1, 0)
        next_group = jnp.where(i + 1 < n_q, g_index, g_index + 1)

        @pl.when(next_group < GROUPS)
        def _():
            fetch_first_two(next_group, next_block)

        def sweep2(c, carry, masked):
            off = pl.multiple_of(c * ch, ch)
            kt = k_ref[pl.ds(off, ch), :]
            out, dk = [], None
            for j in range(2):
                before, dq = carry[j]
                g16, beta = g_s[j, c], beta_s[j, c].astype(F32)
                g = g16.astype(F32)
                in_front, before = _running_sums(g, (g16,), before, tri_before, False)
                dz = g * (1.0 - beta) - beta * in_front
                if masked:
                    dz = jnp.where(strict_mask(c), dz, 0.0)
                dz = dz.astype(BF16)
                dkj = _dot(dz, qms[j], "tn")
                dk = dkj if dk is None else dk + dkj
                out.append((before, dq + _dot(dz, kt, "nn")))
            dk_ref[pl.ds(off, ch), :] += dk
            return tuple(out)

        carry = lax.fori_loop(0, last, lambda c, cr: sweep2(c, cr, False),
                              tuple((jnp.zeros((bq, 1), F32), jnp.zeros((bq, LANES), F32)) for _ in range(2)))
        res = sweep2(last, carry, True)
        dq_ref[...] = _select_heads([dq for _, dq in res], lambda dq, j: dq) * dq_scale

    n_ch = s // ch
    return pl.pallas_call(
        body, name=name, grid=(GROUPS, s // bq), in_specs=[q_spec, k_spec, v_spec, row_out(LANES), _ANY, _ANY],
        out_specs=[row_out(LANES), key_out(LANES), key_out(LANES)],
        out_shape=[jax.ShapeDtypeStruct((s, GROUPS * LANES), F32)] * 3,
        scratch_shapes=[pltpu.VMEM((2, n_ch, bq, ch), BF16)] * 2 + [pltpu.VMEM((3, 4, bq, ch), BF16), pltpu.SemaphoreType.DMA((3, 4))],
        compiler_params=_cparams("arbitrary", "arbitrary"),
    )(qkv, qkv, qkv, do, saved, behind)


def _rope_tables(s):
    half = MLA_ROPE // 2
    freqs = ROPE_BASE ** (-jnp.arange(half, dtype=F32) / half)
    ang = jnp.arange(s, dtype=F32)[:, None] * freqs[None, :]
    cos, sin = jnp.cos(ang), jnp.sin(ang)
    tail = jnp.zeros((s, LANES - MLA_NOPE - MLA_ROPE), F32)
    lead = lambda fill: jnp.full((s, MLA_NOPE), fill, F32)
    return dict(
        cos_k0=jnp.concatenate([cos, cos, lead(0.0), tail], axis=1), sin_k0=jnp.concatenate([-sin, sin, lead(0.0), tail], axis=1),
        cos_k64=jnp.concatenate([lead(0.0), cos, cos, tail], axis=1), sin_k64=jnp.concatenate([lead(0.0), -sin, sin, tail], axis=1),
        cos_q=jnp.concatenate([lead(1.0), cos, cos, tail], axis=1),
        sin_k64_t=jnp.concatenate([lead(0.0), sin, -sin, tail], axis=1),
    )


def _local_step(x, mem, target, w, emit=lambda grads: [jnp.zeros((8, LANES), F32)]):
    s = x.shape[0]
    rope = _rope_tables(s)
    xb = x.astype(BF16)
    inv_d = 1.0 / D_MODEL
    scale_a = LOG2E / math.sqrt(MLA_NOPE + MLA_ROPE)
    scale_b = LOG2E / math.sqrt(SB_HEAD_DIM)
    scale_m = LOG2E / math.sqrt(MEM_HEAD_DIM)
    arrive_after = getattr(w, "arrive_after", lambda *values: None)
    memb = mem.astype(BF16)

    arrive_after(xb, memb, *rope.values())
    proj = _mm(xb, w["w_in"], "nn", name="proj", b_cols=(0, QKV_FIRST))
    one = jnp.ones((1, 512), F32)
    qkv = _mm(xb, w["w_in"], "nn", name="proj_qkv", b_cols=(QKV_FIRST, QKV_WIDTH), out_dtype=BF16,
              col_scale=jnp.concatenate([one * scale_b, one, one, one * scale_m], axis=1))
    arrive_after(qkv)
    pre = _mm(xb, w["w_merge_gate"], "nn", name="merge_pre", out_dtype=BF16)
    arrive_after(pre)

    def mla_inputs(c_q, c_kv, k_rope, g_q, g_kv, w_q, w_kv, cos_q, sin_q, cos_k, sin_k):
        n_q = (c_q * lax.rsqrt(jnp.mean(c_q * c_q, axis=1, keepdims=True) + RMS_EPS) * g_q).astype(BF16)
        n_kv = (c_kv * lax.rsqrt(jnp.mean(c_kv * c_kv, axis=1, keepdims=True) + RMS_EPS) * g_kv).astype(BF16)
        q_a = _dot(n_q, w_q, "nn")
        kv_a = _dot(n_kv, w_kv, "nn").astype(BF16)
        q = jnp.concatenate([(g * cos_q + _swap_halves(g, MLA_NOPE) * sin_q) * scale_a for g in _lane_groups(q_a)], axis=1)
        k_pe = pltpu.roll(k_rope * cos_k + _swap_halves(k_rope, 0) * sin_k, MLA_NOPE, axis=1).astype(BF16)
        k = jnp.concatenate([g + k_pe for g in _lane_groups(kv_a[:, :1024])], axis=1)
        return n_q, n_kv, q, k, kv_a[:, 1024:]

    n_q, n_kv, q_mla, k_mla, v_a = _rowwise(
        mla_inputs, [(proj, 256, COL_CQ), (proj, 128, COL_CKV), (proj, 128, COL_KROPE), w["q_a_gain"], w["kv_a_gain"],
                     ("whole", w["w_q_b"]), ("whole", w["w_kv_b"]), rope["cos_q"], rope["sin_k64"], rope["cos_k0"], rope["sin_k0"]],
        [(256, BF16), (128, BF16), (1024, BF16), (1024, BF16), (512, BF16)], name="mla_inputs", rows=s)
    o_a, lse_a = _softmax_fwd(q_mla, k_mla, v_a, hp=2, causal=True, name="mla_fwd")

    o_b, sb_saved = _sb_fwd(qkv, q0=COL_QB, k0=COL_KB, v0=COL_VB, name="sb_fwd")

    mem_kv =_mm(memb, w["w_mem_kv"], "nn", name="mem_kv", out_dtype=BF16)
    o_m, lse_m = _softmax_fwd(qkv, mem_kv, mem_kv, hp=1, causal=False, name="mem_fwd", q0=1536, v0=512, q_rows=MEM_Q_BLOCK)

    branches = ("mla", "sb", "mem")
    w_branch = jnp.stack([w[f"w_branch_{br}"] for br in branches])
    bias = w["b_merge_gate"]

    def head(oa, ob, om, ga, gb, gm, pa, pb, pm, ba, bb, bm, xv, tv, gain, bias_ln, w_b, w_o, w_g):
        us, ys, gs = [], [], []
        for n, (o, gate, p, b) in enumerate(((oa, ga, pa, ba), (ob, gb, pb, bb), (om, gm, pm, bm))):
            us.append((o * gate * _sigmoid(gate)).astype(BF16))
            ys.append(_dot(us[n], w_b[n], "nn"))
            gs.append(_sigmoid(p.astype(F32) + b))
        merged = (gs[0] * ys[0] + gs[1] * ys[1] + gs[2] * ys[2]).astype(BF16)
        z = DEEPNORM_ALPHA * xv + _dot(merged, w_o, "nn")
        zc = z - jnp.mean(z, axis=1, keepdims=True)
        rstd = lax.rsqrt(jnp.mean(zc * zc, axis=1, keepdims=True) + LN_EPS)
        xhat = zc * rstd
        err = xhat * gain + bias_ln - tv
        loss = 0.5 * jnp.sum(jnp.mean(err * err, axis=1, keepdims=True), axis=0, keepdims=True)
        dy = err * inv_d
        dxhat = dy * gain
        dz = rstd * (dxhat - jnp.mean(dxhat, axis=1, keepdims=True) - xhat * jnp.mean(dxhat * xhat, axis=1, keepdims=True))
        dz16 = dz.astype(BF16)
        dm = _dot(dz16, w_o, "nt")
        dpre = jnp.concatenate([dm * ys[n] * gs[n] * (1.0 - gs[n]) for n in range(3)], axis=1)
        dx = DEEPNORM_ALPHA * dz + _dot(dpre.astype(BF16), w_g, "nt")
        dys = [(dm * gs[n]).astype(BF16) for n in range(3)]
        d_os, d_gates = [], []
        for n, (o, gate) in enumerate(((oa, ga), (ob, gb), (om, gm))):
            du, sg = _dot(dys[n], w_b[n], "nt"), _sigmoid(gate)
            d_os.append(du * gate * sg)
            d_gates.append(du * o * sg * (1.0 + gate * (1.0 - sg)))
        return (us, merged, dx, dz16, _colsum(dy * xhat), _colsum(dy), jnp.broadcast_to(loss, (1, LANES)), dpre, _colsum(dpre),
                dys, *d_os, *d_gates)

    grads = {}
    (u, merged, dx, dzb, grads["ln_gain"], grads["ln_bias"], loss, dpre, grads["b_merge_gate"], dy, *rest) = _rowwise(
        head, [o_a, o_b, o_m, (proj, 512, COL_GATE_A), (proj, 512, COL_GATE_B), (proj, 512, COL_GATE_M),
               (pre, 1024, 0), (pre, 1024, 1024), (pre, 1024, 2048), (bias, 1024, 0), (bias, 1024, 1024), (bias, 1024, 2048),
               x, target, w["ln_gain"], w["ln_bias"], ("whole", w_branch), ("whole", w["w_out"]), ("whole", w["w_merge_gate"])],
        [(3, 512, BF16), (1024, BF16), (1024, F32), (1024, BF16), ("sum", 1024), ("sum", 1024), ("sum", LANES),
         (3072, BF16), ("sum", 3072), (3, 1024, BF16)] + [(512, F32)] * 3 + [(512, BF16)] * 3, name="head", rows=s, tr=256)
    d_o, d_gate = dict(zip(branches, rest[:3], strict=True)), dict(zip(branches, rest[3:], strict=True))

    grads["w_out"] = _mm(merged, dzb, "tn", name="g_w_out", out_dtype=BF16)
    grads["w_merge_gate"] = _mm(xb, dpre, "tn", name="g_w_merge", out_dtype=BF16)
    g_w_branch = _mm(u, dy, "tn", name="g_w_branch", out_dtype=BF16)
    for n, br in enumerate(branches):
        grads[f"w_branch_{br}"] = g_w_branch[n]
    (sent,) = emit({n: grads[n] for n in ("w_out", "w_merge_gate", "w_branch_mla", "w_branch_sb", "w_branch_mem")})

    dq_m, dk_m, dv_m = _softmax_bwd(qkv, mem_kv, mem_kv, o_m, d_o["mem"], lse_m, sent, hp=1, causal=False, dq_scale=scale_m,
                                    name="mem_bwd", q0=1536, v0=512, q_rows=MEM_Q_BLOCK)
    grads["w_mem_kv"] = _mm(memb, jnp.concatenate([dk_m, dv_m], axis=1), "tn", name="g_w_mem_kv", out_dtype=BF16)

    dq_sb, dk_sb, dv_sb = _sb_bwd(qkv, d_o["sb"], sb_saved, sent, q0=COL_QB, k0=COL_KB, v0=COL_VB, dq_scale=scale_b, name="sb_bwd")

    dq_mla, dk_mla, dv_a = _softmax_bwd(q_mla, k_mla, v_a, o_a, d_o["mla"], lse_a, sent, hp=2, causal=True, dq_scale=scale_a,
                                        name="mla_bwd")

    def mla_inputs_bwd(dq, dk, dv, n_q, n_kv, c_q, c_kv, g_q, g_kv, w_q, w_kv, cos_q, sin_q, cos_k, sin_k):
        dq_a = jnp.concatenate([g * cos_q + _swap_halves(g, MLA_NOPE) * sin_q for g in _lane_groups(dq)], axis=1).astype(BF16)
        groups = _lane_groups(dk)
        g_rope = groups[0]
        for other in groups[1:]:
            g_rope = g_rope + other
        dk_rope = pltpu.roll(g_rope * cos_k + _swap_halves(g_rope, MLA_NOPE) * sin_k, MLA_NOPE, axis=1)
        nope = _low_half(g_rope.shape)
        dkv_a = jnp.concatenate([jnp.where(nope, grp, 0.0) for grp in groups] + [dv], axis=1).astype(BF16)
        res = []
        for c, dn, g in ((c_q, _dot(dq_a, w_q, "nt"), g_q), (c_kv, _dot(dkv_a, w_kv, "nt"), g_kv)):
            r = lax.rsqrt(jnp.mean(c * c, axis=1, keepdims=True) + RMS_EPS)
            t = dn * g
            res += [r * t - c * (r * r * r) * jnp.mean(c * t, axis=1, keepdims=True), _colsum(dn * c * r)]
        return *res, dk_rope, _dot(n_q, dq_a, "tn"), _dot(n_kv, dkv_a, "tn")

    dc_q, grads["q_a_gain"], dc_kv, grads["kv_a_gain"], dk_rope, g_w_q_b, g_w_kv_b = _rowwise(
        mla_inputs_bwd, [dq_mla, dk_mla, dv_a, n_q, n_kv, (proj, 256, COL_CQ), (proj, 128, COL_CKV), w["q_a_gain"], w["kv_a_gain"],
                         ("whole", w["w_q_b"]), ("whole", w["w_kv_b"]), rope["cos_q"], rope["sin_k64_t"], rope["cos_k64"], rope["sin_k64_t"]],
        [(256, BF16), ("sum", 256), (128, BF16), ("sum", 128), (128, BF16), ("sum", (MLA_Q_LORA, 1024)), ("sum", (MLA_KV_LORA, 1536))],
        name="mla_inputs_bwd", rows=s)
    grads["w_q_b"], grads["w_kv_b"] = g_w_q_b.astype(BF16), g_w_kv_b.astype(BF16)

    sent = emit({n: grads[n] for n in ("w_mem_kv", "w_q_b", "w_kv_b")})

    dproj = jnp.concatenate(
        [dc_q, dc_kv, dk_rope, d_gate["mla"], d_gate["sb"], d_gate["mem"], dq_sb.astype(BF16), dk_sb.astype(BF16),
         dv_sb.astype(BF16), dq_m.astype(BF16)], axis=1)
    grads["w_in"] = _mm(xb, dproj, "tn", name="g_w_in", out_dtype=BF16, behind=sent)
    sent = emit({"w_in": grads["w_in"]})
    grad_x = _mm(dproj, w["w_in"], "nt", name="grad_x", add=dx, behind=sent)
    return loss, grad_x, grads


def _shard_shape(shape, axis):
    return tuple(d // N_DEV if a == axis else d for a, d in enumerate(shape))


def _from_blocks(blocks, name):
    shape, axis = SHARDED[name]
    return blocks.reshape(shape) if axis == 0 else blocks.transpose(1, 0, 2).reshape(shape)


def _to_blocks(full, name):
    shape, axis = SHARDED[name]
    shp = _shard_shape(shape, axis)
    return full.reshape(N_DEV, *shp) if axis == 0 else full.reshape(shape[0], N_DEV, shp[1]).transpose(1, 0, 2)


def _pad_heads(a, used):
    rows = a.shape[0]
    a = a.reshape(rows, MLA_HEADS, used)
    return jnp.concatenate([a, jnp.zeros((rows, MLA_HEADS, LANES - used), a.dtype)], axis=2).reshape(rows, MLA_HEADS * LANES)


def _to_kernel_layout(name, full):
    if name == "w_in":
        return jnp.concatenate([jnp.zeros((D_MODEL, IN_PAD), full.dtype) if piece is None else full[:, piece[0]:piece[0] + piece[1]]
                                for piece in IN_PIECES], axis=1)
    if name == "w_q_b":
        return _pad_heads(full, MLA_NOPE + MLA_ROPE)
    if name == "w_kv_b":
        kv = full.reshape(MLA_KV_LORA, MLA_HEADS, MLA_NOPE + MLA_V)
        return jnp.concatenate([_pad_heads(kv[:, :, :MLA_NOPE].reshape(MLA_KV_LORA, -1), MLA_NOPE),
                                kv[:, :, MLA_NOPE:].reshape(MLA_KV_LORA, -1)], axis=1)
    return full


def _from_kernel_layout(name, g):
    if name == "w_in":
        placed, at = [], 0
        for piece in IN_PIECES:
            if piece is not None:
                placed.append((piece[0], g[:, at:at + piece[1]]))
            at += IN_PAD if piece is None else piece[1]
        return jnp.concatenate([cols for _, cols in sorted(placed, key=lambda item: item[0])], axis=1)
    if name == "w_q_b":
        return g.reshape(MLA_Q_LORA, MLA_HEADS, LANES)[:, :, :MLA_NOPE + MLA_ROPE].reshape(MLA_Q_LORA, -1)
    if name == "w_kv_b":
        return jnp.concatenate([g[:, :1024].reshape(MLA_KV_LORA, MLA_HEADS, LANES)[:, :, :MLA_NOPE],
                                g[:, 1024:].reshape(MLA_KV_LORA, MLA_HEADS, MLA_V)], axis=2).reshape(MLA_KV_LORA, -1)
    return g


def _pack_small(vectors, loss=None):
    flat = [v.reshape(-1) for v in vectors]
    flat.append(jnp.zeros((SMALL_ROWS * SMALL_LANES - LOSS_INDEX,), F32) if loss is None else
                jnp.concatenate([loss.reshape(-1)[:1], jnp.zeros((SMALL_ROWS * SMALL_LANES - LOSS_INDEX - 1,), F32)]))
    return jnp.concatenate(flat).reshape(SMALL_ROWS, SMALL_LANES)


def _unpack_small(packed):
    flat, res, off = packed.reshape(-1), [], 0
    for _, n in SMALL:
        res.append(flat[off:off + n].reshape(1, n))
        off += n
    return res


def _me_and_peers():
    x, y, c = lax.axis_index("x"), lax.axis_index("y"), lax.axis_index("c")
    peers = []
    for kk in range(1, N_DEV):
        px, py, pc = (x + (kk >> 2)) % 2, (y + ((kk >> 1) & 1)) % 2, (c + (kk & 1)) % 2
        peers.append(((px, py, pc), 4 * px + 2 * py + pc))
    return 4 * x + 2 * y + c, peers


def _share_small(small, *, name):
    def body(small_ref, all_ref, send_sems, recv_sems, local_sem):
        me, peers = _me_and_peers()
        copies = [pltpu.make_async_remote_copy(src_ref=small_ref, dst_ref=all_ref.at[me], send_sem=send_sems.at[kk], recv_sem=recv_sems.at[kk],
                                               device_id=pos, device_id_type=pl.DeviceIdType.MESH) for kk, (pos, _) in enumerate(peers)]
        copies.append(pltpu.make_async_copy(small_ref, all_ref.at[me], local_sem))
        for cp in copies:
            cp.start()
        for cp in copies:
            cp.wait()

    hbm = pl.BlockSpec(memory_space=pl.ANY)
    return pl.pallas_call(
        body, name=name, in_specs=[hbm], out_specs=hbm, out_shape=jax.ShapeDtypeStruct((N_DEV, *small.shape), small.dtype),
        scratch_shapes=[pltpu.SemaphoreType.DMA((N_DEV - 1,)), pltpu.SemaphoreType.DMA((N_DEV - 1,)), pltpu.SemaphoreType.DMA],
        compiler_params=pltpu.CompilerParams(has_side_effects=True),
    )(small)


_HBM = pl.BlockSpec(memory_space=pltpu.HBM)
_SEM = pl.BlockSpec(memory_space=pltpu.SEMAPHORE)


def _exchange_copies(srcs, zones, send_sems, recv_sems, gather):
    me, peers = _me_and_peers()
    return [pltpu.make_async_remote_copy(
        src_ref=srcs[t] if gather else srcs[t].at[peer], dst_ref=zones[t].at[me], send_sem=send_sems.at[7 * t + kk],
        recv_sem=recv_sems.at[7 * t + kk], device_id=pos, device_id_type=pl.DeviceIdType.MESH)
        for t in range(len(srcs)) for kk, (pos, peer) in enumerate(peers)]


def _exchange_start(tensors, *, gather, name):
    n = len(tensors)
    zones = [lax.empty((N_DEV, *(t.shape if gather else t.shape[1:])), t.dtype) for t in tensors]

    def body(*refs):
        for cp in _exchange_copies(refs[:n], refs[n:2 * n], refs[2 * n], refs[2 * n + 1], gather):
            cp.start()
        refs[-1][...] = jnp.zeros_like(refs[-1])

    buffers = [pltpu.HBM(a.shape, a.dtype) for a in tensors + zones]
    res = pl.pallas_call(
        body, name=name, in_specs=[_HBM] * (2 * n),
        out_shape=(pltpu.SemaphoreType.DMA((7 * n,)), pltpu.SemaphoreType.DMA((7 * n,)), *buffers, jax.ShapeDtypeStruct((8, LANES), F32)),
        out_specs=(_SEM, _SEM, *[_HBM] * (2 * n), pl.BlockSpec(memory_space=pltpu.VMEM)),
        input_output_aliases={i: 2 + i for i in range(2 * n)},
        compiler_params=pltpu.CompilerParams(has_side_effects=pltpu.SideEffectType.DATAFLOW_SIDE_EFFECTING),
    )(*[pltpu.with_memory_space_constraint(a, pltpu.HBM) for a in tensors + zones])
    return dict(sems=res[:2], buffers=res[2:2 + 2 * n], gather=gather, started=res[-1])


def _exchange_wait(started, after, *, name):
    n = len(started["buffers"]) // 2

    def body(*refs):
        for cp in _exchange_copies(refs[:n], refs[n:2 * n], refs[2 * n], refs[2 * n + 1], started["gather"]):
            cp.wait_send()
            cp.wait_recv()

    res = pl.pallas_call(
        body, name=name, in_specs=[_HBM] * (2 * n) + [_SEM, _SEM] + [_ANY] * len(after),
        out_shape=tuple(pltpu.HBM(a.shape, a.dtype) for a in started["buffers"]), out_specs=tuple([_HBM] * (2 * n)),
        input_output_aliases={i: i for i in range(2 * n)},
        compiler_params=pltpu.CompilerParams(has_side_effects=pltpu.SideEffectType.DATAFLOW_SIDE_EFFECTING),
    )(*started["buffers"], *started["sems"], *after)
    return res[:n], res[n:]


def _adamw(contrib, w, m, v, *, name):
    rows, cols = w.shape
    tile = min(rows, ADAM_ROWS)

    def body(c_ref, w_ref, m_ref, v_ref, g_ref, d_ref, nm_ref, nv_ref):
        g = c_ref[0].astype(F32)
        for s in range(1, N_DEV):
            g = g + c_ref[s].astype(F32)
        m_new = ADAM_B1 * m_ref[...] + (1.0 - ADAM_B1) * g
        v_new = ADAM_B2 * v_ref[...] + (1.0 - ADAM_B2) * (g * g)
        m_hat = m_new / (1.0 - ADAM_B1 ** ADAM_STEP)
        v_hat = v_new / (1.0 - ADAM_B2 ** ADAM_STEP)
        g_ref[...] = g
        d_ref[...] = -ADAM_LR * (m_hat / (jnp.sqrt(v_hat) + ADAM_EPS) + ADAM_WD * w_ref[...])
        nm_ref[...] = m_new
        nv_ref[...] = v_new

    spec = pl.BlockSpec((tile, cols), lambda i: (i, 0))
    return pl.pallas_call(
        body, name=name, grid=(rows // tile,),
        in_specs=[pl.BlockSpec((N_DEV, tile, cols), lambda i: (0, i, 0)), spec, spec, spec], out_specs=[spec] * 4,
        out_shape=[jax.ShapeDtypeStruct((rows, cols), F32)] * 4, compiler_params=_cparams("parallel"),
    )(contrib, w, m, v)


class _Weights:
    def __init__(self, gathers, vectors, me):
        self.gathers, self.ready, self.me, self.after = gathers, dict(vectors), me, ()

    def arrive_after(self, *values):
        self.after = values

    def __getitem__(self, name):
        if name not in self.ready:
            gi = next(i for i, group in enumerate(GATHER_GROUPS) if name in group)
            after = [*self.after, *[g["started"] for g in self.gathers]]
            shards, zones = _exchange_wait(self.gathers[gi], after, name=f"gather_wait_{gi}")
            for n, shard, zone in zip(GATHER_GROUPS[gi], shards, zones, strict=True):
                blocks = lax.dynamic_update_slice_in_dim(zone, shard[None], self.me, 0)
                self.ready[n] = _to_kernel_layout(n, _from_blocks(blocks, n))
        return self.ready[name]


def kernel(x, mem, w_in, w_mem_kv, q_a_gain, w_q_b, kv_a_gain, w_kv_b, w_branch_mla, w_branch_sb, w_branch_mem, w_merge_gate, b_merge_gate, w_out, ln_gain, ln_bias, loss_target, m_w_in, m_w_mem_kv, m_q_a_gain, m_w_q_b, m_kv_a_gain, m_w_kv_b, m_w_branch_mla, m_w_branch_sb, m_w_branch_mem, m_w_merge_gate, m_b_merge_gate, m_w_out, m_ln_gain, m_ln_bias, v_w_in, v_w_mem_kv, v_q_a_gain, v_w_q_b, v_kv_a_gain, v_w_kv_b, v_w_branch_mla, v_w_branch_sb, v_w_branch_mem, v_w_merge_gate, v_b_merge_gate, v_w_out, v_ln_gain, v_ln_bias):
    given = dict(locals())
    small_names = [n for n, _ in SMALL]
    smalls = lambda prefix: [given[prefix + n] for n in small_names]
    me = 4 * lax.axis_index("x") + 2 * lax.axis_index("y") + lax.axis_index("c")

    gathers = [_exchange_start([given[n][0].astype(BF16) for n in group], gather=True, name=f"gather_start_{gi}")
               for gi, group in enumerate(GATHER_GROUPS)]
    w = _Weights(gathers, {n: given[n] for n in small_names}, me)
    exchanges = []
    results = [{}, {}, {}, {}]

    def finish(gi, after):
        names, started = exchanges[gi]
        sent, zones = _exchange_wait(started, after, name=f"grads_wait_{gi}")
        done = []
        for n, blocks, zone in zip(names, sent, zones, strict=True):
            own = lax.dynamic_index_in_dim(blocks, me, 0, keepdims=True)
            contrib = lax.dynamic_update_slice_in_dim(zone, own, me, 0)
            outs = _adamw(contrib, given[n][0], given["m_" + n][0], given["v_" + n][0], name=f"adamw_{n}")
            for kind, res in zip(results, outs, strict=True):
                kind[n] = res[None]
            done.append(outs[1])
        return done

    def emit(grads):
        blocks = [_to_blocks(_from_kernel_layout(n, g), n).astype(BF16) for n, g in grads.items()]
        exchanges.append((tuple(grads), _exchange_start(blocks, gather=False, name=f"grads_start_{len(exchanges)}")))
        started = [exchanges[-1][1]["started"]]
        if len(exchanges) == len(GRAD_GROUPS):
            for gi in range(len(GRAD_GROUPS) - 1):
                started += finish(gi, started[:1])
        return started

    loss, grad_x, grads = _local_step(x[0], mem[0], loss_target[0], w, emit)

    contrib_small = _share_small(_pack_small([grads[n] for n in small_names], loss), name="share_small")
    sml = _adamw(contrib_small, _pack_small(smalls("")), _pack_small(smalls("m_")), _pack_small(smalls("v_")), name="adamw_small")
    for kind, packed in zip(results, sml, strict=True):
        kind.update(zip(small_names, _unpack_small(packed), strict=True))
    finish(len(GRAD_GROUPS) - 1, [grad_x])
    order = ["w_in", "w_mem_kv", "q_a_gain", "w_q_b", "kv_a_gain", "w_kv_b", "w_branch_mla", "w_branch_sb", "w_branch_mem",
             "w_merge_gate", "b_merge_gate", "w_out", "ln_gain", "ln_bias"]
    loss_out = sml[0].reshape(-1)[LOSS_INDEX]
    return (loss_out, grad_x[None], *[kind[n] for kind in results for n in order])
```

```python
import math

import jax
import jax.numpy as jnp
from jax import lax
from jax.experimental import pallas as pl
from jax.experimental.pallas import tpu as pltpu

F32, BF16 = jnp.float32, jnp.bfloat16

N_DEV = 8
D_MODEL = 1024
MLA_HEADS, MLA_NOPE, MLA_ROPE, MLA_V = 8, 64, 32, 64
MLA_Q_LORA, MLA_KV_LORA = 256, 128
SB_HEAD_DIM = 64
MEM_HEAD_DIM = 128
ROPE_BASE = 10000.0
RMS_EPS = 1e-6
LN_EPS = 1e-5
DEEPNORM_ALPHA = 2.0 ** 0.25
ADAM_LR, ADAM_B1, ADAM_B2, ADAM_EPS, ADAM_WD, ADAM_STEP = 0.001, 0.9, 0.999, 1e-08, 0.01, 10
LOG2E, LN2 = math.log2(math.e), math.log(2.0)

LANES = 128
GROUPS = 4
PROJ_WIDTH = 4096
COL_CQ, COL_CKV, COL_KROPE, COL_GATE_A, COL_GATE_B, COL_GATE_M = 0, 256, 384, 512, 1024, 1536
QKV_FIRST, QKV_WIDTH = 2048, 2048
COL_QB, COL_KB, COL_VB, COL_QM = 0, 512, 1024, 1536
IN_PIECES = ((0, 416), None, (416, 512), (2464, 512), (3488, 512), (928, 512), (1440, 512), (1952, 512), (2976, 512))
IN_PAD = 96

VMEM_LIMIT_BYTES = 56 * 1024 * 1024
NEG_BIG = -1e30
Q_BLOCK = 512
MEM_Q_BLOCK = 2048
SB_BWD_Q_BLOCK = 512
TRI_BLOCK = 256
TILE_ROWS = 64
KEY_CHUNK = 512

SHARDED = {
    "w_in": ((1024, 4000), 1), "w_mem_kv": ((1024, 1024), 0), "w_q_b": ((256, 768), 1), "w_kv_b": ((128, 1024), 1),
    "w_branch_mla": ((512, 1024), 1), "w_branch_sb": ((512, 1024), 1), "w_branch_mem": ((512, 1024), 1),
    "w_merge_gate": ((1024, 3072), 1), "w_out": ((1024, 1024), 0),
}
GATHER_GROUPS = (("w_in",), ("w_merge_gate",), ("w_q_b", "w_kv_b", "w_mem_kv", "w_branch_mla", "w_branch_sb", "w_branch_mem", "w_out"))
GRAD_GROUPS = (("w_out", "w_merge_gate", "w_branch_mla", "w_branch_sb", "w_branch_mem"), ("w_mem_kv", "w_q_b", "w_kv_b"), ("w_in",))
SMALL = (("q_a_gain", 256), ("kv_a_gain", 128), ("b_merge_gate", 3072), ("ln_gain", 1024), ("ln_bias", 1024))
SMALL_ROWS, SMALL_LANES = 48, 128
ADAM_ROWS = 256
LOSS_INDEX = 5504


def _cparams(*sem):
    return pltpu.CompilerParams(dimension_semantics=sem or None, vmem_limit_bytes=VMEM_LIMIT_BYTES)


_DIMS = {"nn": (((1,), (0,)), ((), ())), "nt": (((1,), (1,)), ((), ())), "tn": (((0,), (0,)), ((), ()))}


def _dot(a, b, dims):
    return lax.dot_general(a, b, _DIMS[dims], preferred_element_type=F32)


def _tile(dim, want):
    if dim <= want:
        return dim
    t = want - want % LANES
    while dim % t:
        t -= LANES
    assert t > 0, (dim, want)
    return t


_ANY = pl.BlockSpec(memory_space=pl.ANY)


def _mm(a, b, dims, *, name, out_dtype=F32, add=None, add_scale=1.0, col_scale=None, b_cols=None, behind=None,
        tm=1024, tn=1024, tk=1024):
    batch = a.shape[0] if a.ndim == 3 else None
    if dims == "nn":
        (m, k), (k2, n) = a.shape[-2:], b.shape[-2:]
    elif dims == "nt":
        (m, k), (n, k2) = a.shape[-2:], b.shape[-2:]
    else:
        (k, m), (k2, n) = a.shape[-2:], b.shape[-2:]
    assert k == k2 and a.ndim == b.ndim, (a.shape, b.shape, dims)
    assert batch is None or (b.shape[0] == batch and add is None and col_scale is None and b_cols is None)
    b_first = 0
    if b_cols is not None:
        assert dims == "nn"
        b_first, n = b_cols
    tm, tn, tk = _tile(m, tm), _tile(n, tn), _tile(k, tk)
    assert b_first % tn == 0
    jb = b_first // tn
    nk = k // tk

    def spec(block, index):
        if batch is None:
            return pl.BlockSpec(block, lambda bb, i, j, kk: index(i, j, kk))
        return pl.BlockSpec((None, *block), lambda bb, i, j, kk: (bb, *index(i, j, kk)))

    a_spec = spec((tk, tm), lambda i, j, kk: (kk, i)) if dims == "tn" else spec((tm, tk), lambda i, j, kk: (i, kk))
    b_spec = spec((tn, tk), lambda i, j, kk: (j, kk)) if dims == "nt" else spec((tk, tn), lambda i, j, kk: (kk, jb + j))
    o_spec = spec((tm, tn), lambda i, j, kk: (i, j))
    behind = [] if behind is None else behind if isinstance(behind, (list, tuple)) else [behind]
    optional = [(add, o_spec), (col_scale, pl.BlockSpec((1, tn), lambda bb, i, j, kk: (0, j))), *[(v, _ANY) for v in behind]]
    present = [(v, spec) for v, spec in optional if v is not None]

    def body(*refs):
        a_ref, b_ref = refs[:2]
        extra = iter(refs[2:2 + len(present)])
        add_ref = next(extra) if add is not None else None
        scale_ref = next(extra) if col_scale is not None else None
        o_ref = refs[2 + len(present)]
        part = _dot(a_ref[...].astype(BF16), b_ref[...].astype(BF16), dims)

        def finish(r):
            if add is not None:
                r = r + add_scale * add_ref[...]
            if col_scale is not None:
                r = r * scale_ref[...]
            o_ref[...] = r.astype(out_dtype)

        if nk == 1:
            finish(part)
            return
        acc = refs[-1]
        kk = pl.program_id(3)

        @pl.when(kk == 0)
        def _():
            acc[...] = part

        @pl.when(kk > 0)
        def _():
            acc[...] += part

        @pl.when(kk == nk - 1)
        def _():
            finish(acc[...])

    return pl.pallas_call(
        body, name=name, grid=(batch or 1, m // tm, n // tn, nk),
        in_specs=[a_spec, b_spec] + [spec for _, spec in present], out_specs=o_spec,
        out_shape=jax.ShapeDtypeStruct((m, n) if batch is None else (batch, m, n), out_dtype),
        scratch_shapes=[pltpu.VMEM((tm, tn), F32)] if nk > 1 else [],
        compiler_params=_cparams("parallel", "parallel", "parallel", "arbitrary"),
    )(a, b, *[v for v, _ in present])


def _rowwise(fn, ins, outs, *, name, rows, tr=512):
    n_in = len(ins)
    tr = min(tr, rows)
    in_specs, args = [], []
    for it in ins:
        if isinstance(it, tuple) and it[0] == "whole":
            in_specs.append(pl.BlockSpec(it[1].shape, lambda i, nd=it[1].ndim: (0,) * nd))
            args.append(it[1])
            continue
        arr, w, off = it if isinstance(it, tuple) else (it, it.shape[-1], 0)
        assert off % w == 0
        cb = off // w
        if arr.ndim == 3:
            in_specs.append(pl.BlockSpec((arr.shape[0], tr, w), lambda i, cb=cb: (0, i, cb)))
        elif arr.shape[0] == 1:
            in_specs.append(pl.BlockSpec((1, w), lambda i, cb=cb: (0, cb)))
        else:
            in_specs.append(pl.BlockSpec((tr, w), lambda i, cb=cb: (i, cb)))
        args.append(arr)
    out_shape, out_specs, is_sum = [], [], []
    for out in outs:
        is_sum.append(out[0] == "sum")
        if out[0] == "sum":
            shape = out[1] if isinstance(out[1], tuple) else (1, out[1])
            out_shape.append(jax.ShapeDtypeStruct(shape, F32))
            out_specs.append(pl.BlockSpec(shape, lambda i: (0, 0)))
        elif len(out) == 3:
            out_shape.append(jax.ShapeDtypeStruct((out[0], rows, out[1]), out[2]))
            out_specs.append(pl.BlockSpec((out[0], tr, out[1]), lambda i: (0, i, 0)))
        else:
            out_shape.append(jax.ShapeDtypeStruct((rows, out[0]), out[1]))
            out_specs.append(pl.BlockSpec((tr, out[0]), lambda i: (i, 0)))

    def body(*refs):
        res = fn(*[r[...] for r in refs[:n_in]])
        for r, val, s in zip(refs[n_in:], res, is_sum, strict=True):
            if s:
                @pl.when(pl.program_id(0) == 0)
                def _(r=r):
                    r[...] = jnp.zeros_like(r)

                r[...] += val
            elif isinstance(val, (list, tuple)):
                for n, part in enumerate(val):
                    r[n] = part.astype(r.dtype)
            else:
                r[...] = val.astype(r.dtype)

    return pl.pallas_call(
        body, name=name, grid=(rows // tr,), in_specs=in_specs, out_specs=out_specs, out_shape=out_shape,
        compiler_params=_cparams("arbitrary"),
    )(*args)


def _colsum(v):
    return jnp.sum(v, axis=0, keepdims=True)


def _sigmoid(v):
    return 1.0 / (1.0 + jnp.exp(-v))


def _lane_groups(v):
    return [v[:, g * LANES:(g + 1) * LANES] for g in range(v.shape[1] // LANES)]


def _swap_halves(v, first_lane):
    lane = lax.broadcasted_iota(jnp.int32, v.shape, 1)
    return jnp.where(lane < first_lane + 16, pltpu.roll(v, 112, axis=1), pltpu.roll(v, 16, axis=1))


def _lane_sum(acc, v):
    for part in _lane_groups(v):
        acc = acc + part
    return acc


def _low_half(shape):
    return lax.broadcasted_iota(jnp.int32, shape, 1) < LANES // 2


def _select_heads(per_head, pick):
    if len(per_head) == 1:
        return pick(per_head[0], 0)
    return jnp.where(_low_half(per_head[0].shape), pick(per_head[0], 0), pick(per_head[1], 1))


def _attn_specs(s, sk, hp, bq, q0, k0, v0):
    wq = hp * LANES
    assert q0 % wq == 0 and k0 % wq == 0 and v0 % LANES == 0
    qb0, kb0, vb0 = q0 // wq, k0 // wq, v0 // LANES
    q_spec = pl.BlockSpec((bq, wq), lambda g, i: (i, qb0 + g))
    k_spec = pl.BlockSpec((sk, wq), lambda g, i: (0, kb0 + g))
    v_spec = pl.BlockSpec((sk, LANES), lambda g, i: (0, vb0 + g))
    row_out = lambda w: pl.BlockSpec((bq, w), lambda g, i: (i, g))
    key_out = lambda w: pl.BlockSpec((sk, w), lambda g, i: (0, g))
    return q_spec, k_spec, v_spec, row_out, key_out


def _chunks(i, bq, ch, sk, causal):
    return ((i + 1) * bq - 1) // ch if causal else jnp.int32(sk // ch - 1)


def _positions(i, c, bq, ch):
    return (i * bq + lax.broadcasted_iota(jnp.int32, (bq, ch), 0), c * ch + lax.broadcasted_iota(jnp.int32, (bq, ch), 1))


def _softmax_fwd(q, k, v, *, hp, causal, name, q0=0, k0=0, v0=0, q_rows=Q_BLOCK):
    s, sk = q.shape[0], k.shape[0]
    bq, ch = min(q_rows, s), min(KEY_CHUNK, sk)
    assert not causal or bq <= ch
    q_spec, k_spec, v_spec, row_out, _ = _attn_specs(s, sk, hp, bq, q0, k0, v0)

    def body(q_ref, k_ref, v_ref, o_ref, lse_ref, s_scr):
        i = pl.program_id(1)
        qs = _lane_groups(q_ref[...])
        last = _chunks(i, bq, ch, sk, causal)

        def scores(c, ms, masked):
            off = pl.multiple_of(c * ch, ch)
            out = []
            for j in range(hp):
                sc = _dot(qs[j], k_ref[pl.ds(off, ch), j * LANES:(j + 1) * LANES], "nt")
                if masked:
                    qpos, kpos = _positions(i, c, bq, ch)
                    sc = jnp.where(kpos <= qpos, sc, NEG_BIG)
                s_scr[j, c] = sc
                m = ms[j]
                for part in _lane_groups(sc):
                    m = jnp.maximum(m, part)
                out.append(m)
            return tuple(out)

        ms = lax.fori_loop(0, last, lambda c, m: scores(c, m, False), tuple(jnp.full((bq, LANES), NEG_BIG, F32) for _ in range(hp)))
        ms = scores(last, ms, causal)
        row_max = [jnp.max(m, axis=1, keepdims=True) for m in ms]

        def weigh(c, carry):
            off = pl.multiple_of(c * ch, ch)
            vt = v_ref[pl.ds(off, ch), :]
            out = []
            for j in range(hp):
                l, acc = carry[j]
                p = jnp.exp2(s_scr[j, c] - row_max[j])
                out.append((_lane_sum(l, p), acc + _dot(p.astype(BF16), vt, "nn")))
            return tuple(out)

        zero = jnp.zeros((bq, LANES), F32)
        res = lax.fori_loop(0, last + 1, weigh, tuple((zero, zero) for _ in range(hp)))
        row_sum = [jnp.sum(l, axis=1, keepdims=True) for l, _ in res]
        o_ref[...] = _select_heads([acc for _, acc in res], lambda acc, j: acc / row_sum[j])
        lse_ref[...] = _select_heads([jnp.broadcast_to(row_max[j] + jnp.log2(row_sum[j]), (bq, LANES)) for j in range(hp)], lambda a, j: a)

    return pl.pallas_call(
        body, name=name, grid=(GROUPS, s // bq), in_specs=[q_spec, k_spec, v_spec], out_specs=[row_out(LANES), row_out(LANES)],
        out_shape=[jax.ShapeDtypeStruct((s, GROUPS * LANES), F32)] * 2,
        scratch_shapes=[pltpu.VMEM((hp, sk // ch, bq, ch), F32)], compiler_params=_cparams("parallel", "arbitrary"),
    )(q, k, v)


def _head_cotangent(do, j, hp):
    if hp == 1:
        return do
    return jnp.where(_low_half(do.shape) == (j == 0), do, 0.0)


def _softmax_bwd(q, k, v, o, do, lse, behind, *, hp, causal, dq_scale, name, q0=0, k0=0, v0=0, q_rows=Q_BLOCK):
    s, sk = q.shape[0], k.shape[0]
    bq, ch = min(q_rows, s), min(KEY_CHUNK, sk)
    assert not causal or bq <= ch
    wq = hp * LANES
    q_spec, k_spec, v_spec, row_out, key_out = _attn_specs(s, sk, hp, bq, q0, k0, v0)

    def body(q_ref, k_ref, v_ref, o_ref, do_ref, lse_ref, _, dq_ref, dk_ref, dv_ref, dk_t, dv_t):
        i = pl.program_id(1)

        @pl.when(i == 0)
        def _():
            dk_t[...] = jnp.zeros_like(dk_t)
            dv_t[...] = jnp.zeros_like(dv_t)

        qs = _lane_groups(q_ref[...])
        do_all, o_all, lse_all = do_ref[...], o_ref[...], lse_ref[...]
        dos, deltas, lses = [], [], []
        for j in range(hp):
            d = _head_cotangent(do_all, j, hp)
            deltas.append(jnp.sum(d * o_all, axis=1, keepdims=True))
            dos.append(d.astype(BF16))
            lses.append(lse_all[:, j * (LANES // hp):j * (LANES // hp) + 1])
        last = _chunks(i, bq, ch, sk, causal)

        def chunk(c, dqs, masked):
            off = pl.multiple_of(c * ch, ch)
            vt = v_ref[pl.ds(off, ch), :]
            out, dks, dv = [], [], None
            for j in range(hp):
                kt = k_ref[pl.ds(off, ch), j * LANES:(j + 1) * LANES]
                p = jnp.exp2(_dot(qs[j], kt, "nt") - lses[j])
                if masked:
                    qpos, kpos = _positions(i, c, bq, ch)
                    p = jnp.where(kpos <= qpos, p, 0.0)
                ds = (p * (_dot(dos[j], vt, "nt") - deltas[j]) * LN2).astype(BF16)
                out.append(dqs[j] + _dot(ds, kt, "nn"))
                dks.append(_dot(qs[j], ds, "tn"))
                dvj = _dot(dos[j], p.astype(BF16), "tn")
                dv = dvj if dv is None else dv + dvj
            dk_t[c] += dks[0] if hp == 1 else jnp.concatenate(dks, axis=0)
            dv_t[c] += dv
            return tuple(out)

        dqs = lax.fori_loop(0, last, lambda c, d: chunk(c, d, False), tuple(jnp.zeros((bq, LANES), F32) for _ in range(hp)))
        dqs = chunk(last, dqs, causal)
        dq_ref[...] = (dqs[0] if hp == 1 else jnp.concatenate(dqs, axis=1)) * dq_scale

        @pl.when(i == s // bq - 1)
        def _():
            for c in range(sk // ch):
                dk_ref[c * ch:(c + 1) * ch, :] = dk_t[c].T
                dv_ref[c * ch:(c + 1) * ch, :] = dv_t[c].T

    return pl.pallas_call(
        body, name=name, grid=(GROUPS, s // bq),
        in_specs=[q_spec, k_spec, v_spec, row_out(LANES), row_out(LANES), row_out(LANES), _ANY],
        out_specs=[row_out(wq), key_out(wq), key_out(LANES)],
        out_shape=[jax.ShapeDtypeStruct((s, GROUPS * wq), F32), jax.ShapeDtypeStruct((sk, GROUPS * wq), F32),
                   jax.ShapeDtypeStruct((sk, GROUPS * LANES), F32)],
        scratch_shapes=[pltpu.VMEM((sk // ch, wq, ch), F32), pltpu.VMEM((sk // ch, LANES, ch), F32)],
        compiler_params=_cparams("arbitrary", "arbitrary"),
    )(q, k, v, o, do, lse, behind)


def _log2_sigmoid_pair(z2):
    minus_abs = lax.bitcast_convert_type(lax.bitcast_convert_type(z2, jnp.uint32) | jnp.uint32(0x80000000), F32)
    log_beta = jnp.minimum(z2, 0.0) - jnp.log2(1.0 + jnp.exp2(minus_abs))
    return log_beta, log_beta - z2


def _tilewise(fn, *arrays):
    rows, cols = arrays[0].shape
    step = min(TILE_ROWS, rows)
    grid = [[fn(*[None if a is None else a[r:r + step, c:c + LANES] for a in arrays]) for c in range(0, cols, LANES)]
            for r in range(0, rows, step)]
    return [jnp.concatenate([jnp.concatenate([cell[k] for cell in row], axis=1) for row in grid], axis=0)
            for k in range(len(grid[0][0]))]


def _split(v):
    hi = v.astype(BF16)
    return hi, (v - hi.astype(F32)).astype(BF16)


def _tri(n, after):
    rows, cols = lax.broadcasted_iota(jnp.int32, (n, n), 0), lax.broadcasted_iota(jnp.int32, (n, n), 1)
    return (rows > cols if after else rows < cols).astype(BF16)


def _running_sums(v, terms, start, tri, backwards):
    n = tri.shape[0]
    n_blocks = v.shape[1] // n
    order = range(n_blocks - 1, -1, -1) if backwards else range(n_blocks)
    stacked = tri if len(terms) == 1 else jnp.concatenate([tri] * len(terms), axis=0)
    parts, run = [None] * n_blocks, start
    for t in order:
        cols = slice(t * n, (t + 1) * n)
        lhs = terms[0][:, cols] if len(terms) == 1 else jnp.concatenate([term[:, cols] for term in terms], axis=1)
        parts[t] = _dot(lhs, stacked, "nn") + run
        run = run + jnp.sum(v[:, cols], axis=1, keepdims=True)
    return (parts[0] if n_blocks == 1 else jnp.concatenate(parts, axis=1)), run


def _sb_weights(qm, kt, run, tri, strict):
    def logs(z2, keep):
        log_beta, log_keep = _log2_sigmoid_pair(z2)
        if keep is not None:
            log_keep = jnp.where(keep, log_keep, 0.0)
        return log_beta, log_keep, *_split(log_keep)

    log_beta, log_keep, hi, lo = _tilewise(logs, _dot(qm, kt, "nt"), strict)
    behind, run = _running_sums(log_keep, (hi, lo), run, tri, True)

    def weigh(log_beta, behind, keep):
        a = jnp.exp2(log_beta + behind)
        return (a if keep is None else jnp.where(keep, a, 0.0),)

    (a,) = _tilewise(weigh, log_beta, behind, strict)
    return a, log_beta, run


class _Copies:
    def __init__(self, copies):
        self.copies = copies

    def start(self):
        for cp in self.copies:
            cp.start()

    def wait(self):
        for cp in self.copies:
            cp.wait()


def _sb_queries(q_all):
    low = _low_half(q_all.shape)
    zero = jnp.zeros_like(q_all)
    return [jnp.where(low, q_all, zero), jnp.where(low, zero, q_all)]


def _sb_fwd(qkv, *, q0, k0, v0, name):
    s = qkv.shape[0]
    bq, ch = min(Q_BLOCK, s), min(KEY_CHUNK, s)
    assert bq == ch
    n_q = s // bq
    q_spec, k_spec, v_spec, row_out, _ = _attn_specs(s, s, 1, bq, q0, k0, v0)

    def body(q_ref, k_ref, v_ref, o_ref, saved_ref, stage, sems):
        g, i = pl.program_id(0), pl.program_id(1)
        qms = _sb_queries(q_ref[...])
        tri = _tri(min(TRI_BLOCK, ch), True)
        last = _chunks(i, bq, ch, s, True)
        first_tile = i * (i + 1) // 2
        steps_before = g * (n_q * (n_q + 1) // 2) + first_tile

        def save(slot, c):
            return _Copies([pltpu.make_async_copy(stage.at[slot, p], saved_ref.at[g, first_tile + c, p], sems.at[slot, p])
                            for p in range(4)])

        def chunk(c, step, carry, masked):
            off = pl.multiple_of(c * ch, ch)
            kt, vt = k_ref[pl.ds(off, ch), :], v_ref[pl.ds(off, ch), :]
            strict = None
            if masked:
                qpos, kpos = _positions(i, c, bq, ch)
                strict = kpos < qpos
            slot = (steps_before + step) % 3

            @pl.when(steps_before + step >= 3)
            def _():
                save(slot, c).wait()

            out = []
            for j in range(2):
                run, acc = carry[j]
                a, log_beta, run = _sb_weights(qms[j], kt, run, tri, strict)
                a = a.astype(BF16)
                stage[slot, j] = a
                stage[slot, 2 + j] = jnp.exp2(log_beta).astype(BF16)
                out.append((run, acc + _dot(a, vt, "nn")))
            save(slot, c).start()
            return tuple(out)

        carry = chunk(last, 0, tuple((jnp.zeros((bq, 1), F32), jnp.zeros((bq, LANES), F32)) for _ in range(2)), True)
        res = lax.fori_loop(0, last, lambda n, c: chunk(last - 1 - n, n + 1, c, False), carry)
        @pl.when((g == GROUPS - 1) & (i == n_q - 1))
        def _():
            for back in range(3):
                save((steps_before + last - back) % 3, 0).wait()

        o_ref[...] = _select_heads([acc for _, acc in res], lambda acc, j: acc)

    return pl.pallas_call(
        body, name=name, grid=(GROUPS, n_q), in_specs=[q_spec, k_spec, v_spec], out_specs=[row_out(LANES), _ANY],
        out_shape=[jax.ShapeDtypeStruct((s, GROUPS * LANES), F32),
                   jax.ShapeDtypeStruct((GROUPS, n_q * (n_q + 1) // 2, 4, bq, ch), BF16)],
        scratch_shapes=[pltpu.VMEM((3, 4, bq, ch), BF16), pltpu.SemaphoreType.DMA((3, 4))],
        compiler_params=_cparams("arbitrary", "arbitrary"),
    )(qkv, qkv, qkv)


def _sb_bwd(qkv, do, saved, behind, *, q0, k0, v0, dq_scale, name):
    s = qkv.shape[0]
    bq, ch = min(SB_BWD_Q_BLOCK, s), min(KEY_CHUNK, s)
    assert bq == ch and saved.shape[2:] == (4, bq, ch)
    q_spec, k_spec, v_spec, row_out, key_out = _attn_specs(s, s, 1, bq, q0, k0, v0)

    def body(q_ref, k_ref, v_ref, do_ref, saved_ref, _, dq_ref, dk_ref, dv_ref, g_s, beta_s, stage, sems, dv_t):
        g_index, i = pl.program_id(0), pl.program_id(1)

        @pl.when(i == 0)
        def _():
            dk_ref[...] = jnp.zeros_like(dk_ref)
            dv_t[...] = jnp.zeros_like(dv_t)

        qms = _sb_queries(q_ref[...])
        do_all = do_ref[...]
        dos = [_head_cotangent(do_all, j, 2).astype(BF16) for j in range(2)]
        dos_ln2 = [(_head_cotangent(do_all, j, 2) * LN2).astype(BF16) for j in range(2)]
        tri_before = _tri(min(TRI_BLOCK, ch), False)
        last = _chunks(i, bq, ch, s, True)
        first_tile = i * (i + 1) // 2

        def strict_mask(c):
            qpos, kpos = _positions(i, c, bq, ch)
            return kpos < qpos

        def fetch_tile(slot, group, tile):
            return _Copies([pltpu.make_async_copy(saved_ref.at[group, tile, p], stage.at[slot, p], sems.at[slot, p])
                            for p in range(4)])

        def fetch(slot, c):
            return fetch_tile(slot, g_index, first_tile + c)

        def fetch_first_two(group, block):
            tile = block * (block + 1) // 2 + block
            fetch_tile(0, group, tile).start()

            @pl.when(block >= 1)
            def _():
                fetch_tile(1, group, tile - 1).start()

        def sweep1(n, unused):
            c, slot = last - n, n % 3

            @pl.when(c >= 2)
            def _():
                fetch((n + 2) % 3, c - 2).start()

            fetch(slot, c).wait()
            off = pl.multiple_of(c * ch, ch)
            vt = v_ref[pl.ds(off, ch), :]
            dv = None
            for j in range(2):
                a = stage[slot, j]
                g_s[j, c] = (a.astype(F32) * _dot(dos_ln2[j], vt, "nt")).astype(BF16)
                beta_s[j, c] = stage[slot, 2 + j]
                dvj = _dot(dos[j], a, "tn")
                dv = dvj if dv is None else dv + dvj
            dv_t[c] += dv
            return unused

        @pl.when((g_index == 0) & (i == 0))
        def _():
            fetch_first_two(g_index, i)

        lax.fori_loop(0, last + 1, sweep1, 0)

        n_q = s // bq
        next_block = jnp.where(i + 1 < n_q, i + 1, 0)
        next_group = jnp.where(i + 1 < n_q, g_index, g_index + 1)

        @pl.when(next_group < GROUPS)
        def _():
            fetch_first_two(next_group, next_block)

        def sweep2(c, carry, masked):
            off = pl.multiple_of(c * ch, ch)
            kt = k_ref[pl.ds(off, ch), :]
            out, dk = [], None
            for j in range(2):
                before, dq = carry[j]
                g16, beta = g_s[j, c], beta_s[j, c].astype(F32)
                g = g16.astype(F32)
                in_front, before = _running_sums(g, (g16,), before, tri_before, False)
                dz = g * (1.0 - beta) - beta * in_front
                if masked:
                    dz = jnp.where(strict_mask(c), dz, 0.0)
                dz = dz.astype(BF16)
                dkj = _dot(dz, qms[j], "tn")
                dk = dkj if dk is None else dk + dkj
                out.append((before, dq + _dot(dz, kt, "nn")))
            dk_ref[pl.ds(off, ch), :] += dk
            return tuple(out)

        carry = lax.fori_loop(0, last, lambda c, cr: sweep2(c, cr, False),
                              tuple((jnp.zeros((bq, 1), F32), jnp.zeros((bq, LANES), F32)) for _ in range(2)))
        res = sweep2(last, carry, True)
        dq_ref[...] = _select_heads([dq for _, dq in res], lambda dq, j: dq) * dq_scale

        @pl.when(i == s // bq - 1)
        def _():
            for c in range(n_ch):
                dv_ref[c * ch:(c + 1) * ch, :] = dv_t[c].T

    n_ch = s // ch
    return pl.pallas_call(
        body, name=name, grid=(GROUPS, s // bq), in_specs=[q_spec, k_spec, v_spec, row_out(LANES), _ANY, _ANY],
        out_specs=[row_out(LANES), key_out(LANES), key_out(LANES)],
        out_shape=[jax.ShapeDtypeStruct((s, GROUPS * LANES), F32)] * 3,
        scratch_shapes=[pltpu.VMEM((2, n_ch, bq, ch), BF16)] * 2 + [pltpu.VMEM((3, 4, bq, ch), BF16), pltpu.SemaphoreType.DMA((3, 4)),
                        pltpu.VMEM((n_ch, LANES, ch), F32)],
        compiler_params=_cparams("arbitrary", "arbitrary"),
    )(qkv, qkv, qkv, do, saved, behind)


def _rope_tables(s):
    half = MLA_ROPE // 2
    freqs = ROPE_BASE ** (-jnp.arange(half, dtype=F32) / half)
    ang = jnp.arange(s, dtype=F32)[:, None] * freqs[None, :]
    cos, sin = jnp.cos(ang), jnp.sin(ang)
    tail = jnp.zeros((s, LANES - MLA_NOPE - MLA_ROPE), F32)
    lead = lambda fill: jnp.full((s, MLA_NOPE), fill, F32)
    return dict(
        cos_k0=jnp.concatenate([cos, cos, lead(0.0), tail], axis=1), sin_k0=jnp.concatenate([-sin, sin, lead(0.0), tail], axis=1),
        cos_k64=jnp.concatenate([lead(0.0), cos, cos, tail], axis=1), sin_k64=jnp.concatenate([lead(0.0), -sin, sin, tail], axis=1),
        cos_q=jnp.concatenate([lead(1.0), cos, cos, tail], axis=1),
        sin_k64_t=jnp.concatenate([lead(0.0), sin, -sin, tail], axis=1),
    )


def _local_step(x, mem, target, w, emit=lambda grads: [jnp.zeros((8, LANES), F32)]):
    s = x.shape[0]
    rope = _rope_tables(s)
    xb = x.astype(BF16)
    inv_d = 1.0 / D_MODEL
    scale_a = LOG2E / math.sqrt(MLA_NOPE + MLA_ROPE)
    scale_b = LOG2E / math.sqrt(SB_HEAD_DIM)
    scale_m = LOG2E / math.sqrt(MEM_HEAD_DIM)
    arrive_after = getattr(w, "arrive_after", lambda *values: None)
    memb = mem.astype(BF16)

    arrive_after(xb, memb, *rope.values())
    proj = _mm(xb, w["w_in"], "nn", name="proj", b_cols=(0, QKV_FIRST))
    one = jnp.ones((1, 512), F32)
    qkv = _mm(xb, w["w_in"], "nn", name="proj_qkv", b_cols=(QKV_FIRST, QKV_WIDTH), out_dtype=BF16,
              col_scale=jnp.concatenate([one * scale_b, one, one, one * scale_m], axis=1))
    arrive_after(qkv)
    pre = _mm(xb, w["w_merge_gate"], "nn", name="merge_pre", out_dtype=BF16)
    arrive_after(pre)

    def mla_inputs(c_q, c_kv, k_rope, g_q, g_kv, w_q, w_kv, cos_q, sin_q, cos_k, sin_k):
        n_q = (c_q * lax.rsqrt(jnp.mean(c_q * c_q, axis=1, keepdims=True) + RMS_EPS) * g_q).astype(BF16)
        n_kv = (c_kv * lax.rsqrt(jnp.mean(c_kv * c_kv, axis=1, keepdims=True) + RMS_EPS) * g_kv).astype(BF16)
        q_a = _dot(n_q, w_q, "nn")
        kv_a = _dot(n_kv, w_kv, "nn").astype(BF16)
        q = jnp.concatenate([(g * cos_q + _swap_halves(g, MLA_NOPE) * sin_q) * scale_a for g in _lane_groups(q_a)], axis=1)
        k_pe = pltpu.roll(k_rope * cos_k + _swap_halves(k_rope, 0) * sin_k, MLA_NOPE, axis=1).astype(BF16)
        k = jnp.concatenate([g + k_pe for g in _lane_groups(kv_a[:, :1024])], axis=1)
        return n_q, n_kv, q, k, kv_a[:, 1024:]

    n_q, n_kv, q_mla, k_mla, v_a = _rowwise(
        mla_inputs, [(proj, 256, COL_CQ), (proj, 128, COL_CKV), (proj, 128, COL_KROPE), w["q_a_gain"], w["kv_a_gain"],
                     ("whole", w["w_q_b"]), ("whole", w["w_kv_b"]), rope["cos_q"], rope["sin_k64"], rope["cos_k0"], rope["sin_k0"]],
        [(256, BF16), (128, BF16), (1024, BF16), (1024, BF16), (512, BF16)], name="mla_inputs", rows=s)
    o_a, lse_a = _softmax_fwd(q_mla, k_mla, v_a, hp=2, causal=True, name="mla_fwd")

    o_b, sb_saved = _sb_fwd(qkv, q0=COL_QB, k0=COL_KB, v0=COL_VB, name="sb_fwd")

    mem_kv =_mm(memb, w["w_mem_kv"], "nn", name="mem_kv", out_dtype=BF16)
    o_m, lse_m = _softmax_fwd(qkv, mem_kv, mem_kv, hp=1, causal=False, name="mem_fwd", q0=1536, v0=512, q_rows=MEM_Q_BLOCK)

    branches = ("mla", "sb", "mem")
    w_branch = jnp.stack([w[f"w_branch_{br}"] for br in branches])
    bias = w["b_merge_gate"]

    def head(oa, ob, om, ga, gb, gm, pa, pb, pm, ba, bb, bm, xv, tv, gain, bias_ln, w_b, w_o, w_g):
        us, ys, gs = [], [], []
        for n, (o, gate, p, b) in enumerate(((oa, ga, pa, ba), (ob, gb, pb, bb), (om, gm, pm, bm))):
            us.append((o * gate * _sigmoid(gate)).astype(BF16))
            ys.append(_dot(us[n], w_b[n], "nn"))
            gs.append(_sigmoid(p.astype(F32) + b))
        merged = (gs[0] * ys[0] + gs[1] * ys[1] + gs[2] * ys[2]).astype(BF16)
        z = DEEPNORM_ALPHA * xv + _dot(merged, w_o, "nn")
        zc = z - jnp.mean(z, axis=1, keepdims=True)
        rstd = lax.rsqrt(jnp.mean(zc * zc, axis=1, keepdims=True) + LN_EPS)
        xhat = zc * rstd
        err = xhat * gain + bias_ln - tv
        loss = 0.5 * jnp.sum(jnp.mean(err * err, axis=1, keepdims=True), axis=0, keepdims=True)
        dy = err * inv_d
        dxhat = dy * gain
        dz = rstd * (dxhat - jnp.mean(dxhat, axis=1, keepdims=True) - xhat * jnp.mean(dxhat * xhat, axis=1, keepdims=True))
        dz16 = dz.astype(BF16)
        dm = _dot(dz16, w_o, "nt")
        dpre = jnp.concatenate([dm * ys[n] * gs[n] * (1.0 - gs[n]) for n in range(3)], axis=1)
        dx = DEEPNORM_ALPHA * dz + _dot(dpre.astype(BF16), w_g, "nt")
        dys = [(dm * gs[n]).astype(BF16) for n in range(3)]
        d_os, d_gates = [], []
        for n, (o, gate) in enumerate(((oa, ga), (ob, gb), (om, gm))):
            du, sg = _dot(dys[n], w_b[n], "nt"), _sigmoid(gate)
            d_os.append(du * gate * sg)
            d_gates.append(du * o * sg * (1.0 + gate * (1.0 - sg)))
        return (us, merged, dx, dz16, _colsum(dy * xhat), _colsum(dy), jnp.broadcast_to(loss, (1, LANES)), dpre, _colsum(dpre),
                dys, *d_os, *d_gates)

    grads = {}
    (u, merged, dx, dzb, grads["ln_gain"], grads["ln_bias"], loss, dpre, grads["b_merge_gate"], dy, *rest) = _rowwise(
        head, [o_a, o_b, o_m, (proj, 512, COL_GATE_A), (proj, 512, COL_GATE_B), (proj, 512, COL_GATE_M),
               (pre, 1024, 0), (pre, 1024, 1024), (pre, 1024, 2048), (bias, 1024, 0), (bias, 1024, 1024), (bias, 1024, 2048),
               x, target, w["ln_gain"], w["ln_bias"], ("whole", w_branch), ("whole", w["w_out"]), ("whole", w["w_merge_gate"])],
        [(3, 512, BF16), (1024, BF16), (1024, F32), (1024, BF16), ("sum", 1024), ("sum", 1024), ("sum", LANES),
         (3072, BF16), ("sum", 3072), (3, 1024, BF16)] + [(512, F32)] * 3 + [(512, BF16)] * 3, name="head", rows=s, tr=256)
    d_o, d_gate = dict(zip(branches, rest[:3], strict=True)), dict(zip(branches, rest[3:], strict=True))

    grads["w_out"] = _mm(merged, dzb, "tn", name="g_w_out", out_dtype=BF16)
    grads["w_merge_gate"] = _mm(xb, dpre, "tn", name="g_w_merge", out_dtype=BF16)
    g_w_branch = _mm(u, dy, "tn", name="g_w_branch", out_dtype=BF16)
    for n, br in enumerate(branches):
        grads[f"w_branch_{br}"] = g_w_branch[n]
    (sent,) = emit({n: grads[n] for n in ("w_out", "w_merge_gate", "w_branch_mla", "w_branch_sb", "w_branch_mem")})

    dq_m, dk_m, dv_m = _softmax_bwd(qkv, mem_kv, mem_kv, o_m, d_o["mem"], lse_m, sent, hp=1, causal=False, dq_scale=scale_m,
                                    name="mem_bwd", q0=1536, v0=512, q_rows=MEM_Q_BLOCK)
    grads["w_mem_kv"] = _mm(memb, jnp.concatenate([dk_m, dv_m], axis=1), "tn", name="g_w_mem_kv", out_dtype=BF16)

    dq_sb, dk_sb, dv_sb = _sb_bwd(qkv, d_o["sb"], sb_saved, sent, q0=COL_QB, k0=COL_KB, v0=COL_VB, dq_scale=scale_b, name="sb_bwd")

    dq_mla, dk_mla, dv_a = _softmax_bwd(q_mla, k_mla, v_a, o_a, d_o["mla"], lse_a, sent, hp=2, causal=True, dq_scale=scale_a,
                                        name="mla_bwd")

    def mla_inputs_bwd(dq, dk, dv, n_q, n_kv, c_q, c_kv, g_q, g_kv, w_q, w_kv, cos_q, sin_q, cos_k, sin_k):
        dq_a = jnp.concatenate([g * cos_q + _swap_halves(g, MLA_NOPE) * sin_q for g in _lane_groups(dq)], axis=1).astype(BF16)
        groups = _lane_groups(dk)
        g_rope = groups[0]
        for other in groups[1:]:
            g_rope = g_rope + other
        dk_rope = pltpu.roll(g_rope * cos_k + _swap_halves(g_rope, MLA_NOPE) * sin_k, MLA_NOPE, axis=1)
        nope = _low_half(g_rope.shape)
        dkv_a = jnp.concatenate([jnp.where(nope, grp, 0.0) for grp in groups] + [dv], axis=1).astype(BF16)
        res = []
        for c, dn, g in ((c_q, _dot(dq_a, w_q, "nt"), g_q), (c_kv, _dot(dkv_a, w_kv, "nt"), g_kv)):
            r = lax.rsqrt(jnp.mean(c * c, axis=1, keepdims=True) + RMS_EPS)
            t = dn * g
            res += [r * t - c * (r * r * r) * jnp.mean(c * t, axis=1, keepdims=True), _colsum(dn * c * r)]
        return *res, dk_rope, _dot(n_q, dq_a, "tn"), _dot(n_kv, dkv_a, "tn")

    dc_q, grads["q_a_gain"], dc_kv, grads["kv_a_gain"], dk_rope, g_w_q_b, g_w_kv_b = _rowwise(
        mla_inputs_bwd, [dq_mla, dk_mla, dv_a, n_q, n_kv, (proj, 256, COL_CQ), (proj, 128, COL_CKV), w["q_a_gain"], w["kv_a_gain"],
                         ("whole", w["w_q_b"]), ("whole", w["w_kv_b"]), rope["cos_q"], rope["sin_k64_t"], rope["cos_k64"], rope["sin_k64_t"]],
        [(256, BF16), ("sum", 256), (128, BF16), ("sum", 128), (128, BF16), ("sum", (MLA_Q_LORA, 1024)), ("sum", (MLA_KV_LORA, 1536))],
        name="mla_inputs_bwd", rows=s)
    grads["w_q_b"], grads["w_kv_b"] = g_w_q_b.astype(BF16), g_w_kv_b.astype(BF16)

    sent = emit({n: grads[n] for n in ("w_mem_kv", "w_q_b", "w_kv_b")})

    dproj = jnp.concatenate(
        [dc_q, dc_kv, dk_rope, d_gate["mla"], d_gate["sb"], d_gate["mem"], dq_sb.astype(BF16), dk_sb.astype(BF16),
         dv_sb.astype(BF16), dq_m.astype(BF16)], axis=1)
    grads["w_in"] = _mm(xb, dproj, "tn", name="g_w_in", out_dtype=BF16, behind=sent)
    sent = emit({"w_in": grads["w_in"]})
    grad_x = _mm(dproj, w["w_in"], "nt", name="grad_x", add=dx, behind=sent)
    return loss, grad_x, grads


def _shard_shape(shape, axis):
    return tuple(d // N_DEV if a == axis else d for a, d in enumerate(shape))


def _from_blocks(blocks, name):
    shape, axis = SHARDED[name]
    return blocks.reshape(shape) if axis == 0 else blocks.transpose(1, 0, 2).reshape(shape)


def _to_blocks(full, name):
    shape, axis = SHARDED[name]
    shp = _shard_shape(shape, axis)
    return full.reshape(N_DEV, *shp) if axis == 0 else full.reshape(shape[0], N_DEV, shp[1]).transpose(1, 0, 2)


def _pad_heads(a, used):
    rows = a.shape[0]
    a = a.reshape(rows, MLA_HEADS, used)
    return jnp.concatenate([a, jnp.zeros((rows, MLA_HEADS, LANES - used), a.dtype)], axis=2).reshape(rows, MLA_HEADS * LANES)


def _to_kernel_layout(name, full):
    if name == "w_in":
        return jnp.concatenate([jnp.zeros((D_MODEL, IN_PAD), full.dtype) if piece is None else full[:, piece[0]:piece[0] + piece[1]]
                                for piece in IN_PIECES], axis=1)
    if name == "w_q_b":
        return _pad_heads(full, MLA_NOPE + MLA_ROPE)
    if name == "w_kv_b":
        kv = full.reshape(MLA_KV_LORA, MLA_HEADS, MLA_NOPE + MLA_V)
        return jnp.concatenate([_pad_heads(kv[:, :, :MLA_NOPE].reshape(MLA_KV_LORA, -1), MLA_NOPE),
                                kv[:, :, MLA_NOPE:].reshape(MLA_KV_LORA, -1)], axis=1)
    return full


def _from_kernel_layout(name, g):
    if name == "w_in":
        placed, at = [], 0
        for piece in IN_PIECES:
            if piece is not None:
                placed.append((piece[0], g[:, at:at + piece[1]]))
            at += IN_PAD if piece is None else piece[1]
        return jnp.concatenate([cols for _, cols in sorted(placed, key=lambda item: item[0])], axis=1)
    if name == "w_q_b":
        return g.reshape(MLA_Q_LORA, MLA_HEADS, LANES)[:, :, :MLA_NOPE + MLA_ROPE].reshape(MLA_Q_LORA, -1)
    if name == "w_kv_b":
        return jnp.concatenate([g[:, :1024].reshape(MLA_KV_LORA, MLA_HEADS, LANES)[:, :, :MLA_NOPE],
                                g[:, 1024:].reshape(MLA_KV_LORA, MLA_HEADS, MLA_V)], axis=2).reshape(MLA_KV_LORA, -1)
    return g


def _pack_small(vectors, loss=None):
    flat = [v.reshape(-1) for v in vectors]
    flat.append(jnp.zeros((SMALL_ROWS * SMALL_LANES - LOSS_INDEX,), F32) if loss is None else
                jnp.concatenate([loss.reshape(-1)[:1], jnp.zeros((SMALL_ROWS * SMALL_LANES - LOSS_INDEX - 1,), F32)]))
    return jnp.concatenate(flat).reshape(SMALL_ROWS, SMALL_LANES)


def _unpack_small(packed):
    flat, res, off = packed.reshape(-1), [], 0
    for _, n in SMALL:
        res.append(flat[off:off + n].reshape(1, n))
        off += n
    return res


def _me_and_peers():
    x, y, c = lax.axis_index("x"), lax.axis_index("y"), lax.axis_index("c")
    peers = []
    for kk in range(1, N_DEV):
        px, py, pc = (x + (kk >> 2)) % 2, (y + ((kk >> 1) & 1)) % 2, (c + (kk & 1)) % 2
        peers.append(((px, py, pc), 4 * px + 2 * py + pc))
    return 4 * x + 2 * y + c, peers


def _share_small(small, *, name):
    def body(small_ref, all_ref, send_sems, recv_sems, local_sem):
        me, peers = _me_and_peers()
        copies = [pltpu.make_async_remote_copy(src_ref=small_ref, dst_ref=all_ref.at[me], send_sem=send_sems.at[kk], recv_sem=recv_sems.at[kk],
                                               device_id=pos, device_id_type=pl.DeviceIdType.MESH) for kk, (pos, _) in enumerate(peers)]
        copies.append(pltpu.make_async_copy(small_ref, all_ref.at[me], local_sem))
        for cp in copies:
            cp.start()
        for cp in copies:
            cp.wait()

    hbm = pl.BlockSpec(memory_space=pl.ANY)
    return pl.pallas_call(
        body, name=name, in_specs=[hbm], out_specs=hbm, out_shape=jax.ShapeDtypeStruct((N_DEV, *small.shape), small.dtype),
        scratch_shapes=[pltpu.SemaphoreType.DMA((N_DEV - 1,)), pltpu.SemaphoreType.DMA((N_DEV - 1,)), pltpu.SemaphoreType.DMA],
        compiler_params=pltpu.CompilerParams(has_side_effects=True),
    )(small)


_HBM = pl.BlockSpec(memory_space=pltpu.HBM)
_SEM = pl.BlockSpec(memory_space=pltpu.SEMAPHORE)


def _exchange_copies(srcs, zones, send_sems, recv_sems, gather):
    me, peers = _me_and_peers()
    return [pltpu.make_async_remote_copy(
        src_ref=srcs[t] if gather else srcs[t].at[peer], dst_ref=zones[t].at[me], send_sem=send_sems.at[7 * t + kk],
        recv_sem=recv_sems.at[7 * t + kk], device_id=pos, device_id_type=pl.DeviceIdType.MESH)
        for t in range(len(srcs)) for kk, (pos, peer) in enumerate(peers)]


def _exchange_start(tensors, *, gather, name):
    n = len(tensors)
    zones = [lax.empty((N_DEV, *(t.shape if gather else t.shape[1:])), t.dtype) for t in tensors]

    def body(*refs):
        for cp in _exchange_copies(refs[:n], refs[n:2 * n], refs[2 * n], refs[2 * n + 1], gather):
            cp.start()
        refs[-1][...] = jnp.zeros_like(refs[-1])

    buffers = [pltpu.HBM(a.shape, a.dtype) for a in tensors + zones]
    res = pl.pallas_call(
        body, name=name, in_specs=[_HBM] * (2 * n),
        out_shape=(pltpu.SemaphoreType.DMA((7 * n,)), pltpu.SemaphoreType.DMA((7 * n,)), *buffers, jax.ShapeDtypeStruct((8, LANES), F32)),
        out_specs=(_SEM, _SEM, *[_HBM] * (2 * n), pl.BlockSpec(memory_space=pltpu.VMEM)),
        input_output_aliases={i: 2 + i for i in range(2 * n)},
        compiler_params=pltpu.CompilerParams(has_side_effects=pltpu.SideEffectType.DATAFLOW_SIDE_EFFECTING),
    )(*[pltpu.with_memory_space_constraint(a, pltpu.HBM) for a in tensors + zones])
    return dict(sems=res[:2], buffers=res[2:2 + 2 * n], gather=gather, started=res[-1])


def _exchange_wait(started, after, *, name):
    n = len(started["buffers"]) // 2

    def body(*refs):
        for cp in _exchange_copies(refs[:n], refs[n:2 * n], refs[2 * n], refs[2 * n + 1], started["gather"]):
            cp.wait_send()
            cp.wait_recv()

    res = pl.pallas_call(
        body, name=name, in_specs=[_HBM] * (2 * n) + [_SEM, _SEM] + [_ANY] * len(after),
        out_shape=tuple(pltpu.HBM(a.shape, a.dtype) for a in started["buffers"]), out_specs=tuple([_HBM] * (2 * n)),
        input_output_aliases={i: i for i in range(2 * n)},
        compiler_params=pltpu.CompilerParams(has_side_effects=pltpu.SideEffectType.DATAFLOW_SIDE_EFFECTING),
    )(*started["buffers"], *started["sems"], *after)
    return res[:n], res[n:]


def _adamw(contrib, w, m, v, *, name):
    rows, cols = w.shape
    tile = min(rows, ADAM_ROWS)

    def body(c_ref, w_ref, m_ref, v_ref, g_ref, d_ref, nm_ref, nv_ref):
        g = c_ref[0].astype(F32)
        for s in range(1, N_DEV):
            g = g + c_ref[s].astype(F32)
        m_new = ADAM_B1 * m_ref[...] + (1.0 - ADAM_B1) * g
        v_new = ADAM_B2 * v_ref[...] + (1.0 - ADAM_B2) * (g * g)
        m_hat = m_new / (1.0 - ADAM_B1 ** ADAM_STEP)
        v_hat = v_new / (1.0 - ADAM_B2 ** ADAM_STEP)
        g_ref[...] = g
        d_ref[...] = -ADAM_LR * (m_hat / (jnp.sqrt(v_hat) + ADAM_EPS) + ADAM_WD * w_ref[...])
        nm_ref[...] = m_new
        nv_ref[...] = v_new

    spec = pl.BlockSpec((tile, cols), lambda i: (i, 0))
    return pl.pallas_call(
        body, name=name, grid=(rows // tile,),
        in_specs=[pl.BlockSpec((N_DEV, tile, cols), lambda i: (0, i, 0)), spec, spec, spec], out_specs=[spec] * 4,
        out_shape=[jax.ShapeDtypeStruct((rows, cols), F32)] * 4, compiler_params=_cparams("parallel"),
    )(contrib, w, m, v)


class _Weights:
    def __init__(self, gathers, vectors, me):
        self.gathers, self.ready, self.me, self.after = gathers, dict(vectors), me, ()

    def arrive_after(self, *values):
        self.after = values

    def __getitem__(self, name):
        if name not in self.ready:
            gi = next(i for i, group in enumerate(GATHER_GROUPS) if name in group)
            after = [*self.after, *[g["started"] for g in self.gathers]]
            shards, zones = _exchange_wait(self.gathers[gi], after, name=f"gather_wait_{gi}")
            for n, shard, zone in zip(GATHER_GROUPS[gi], shards, zones, strict=True):
                blocks = lax.dynamic_update_slice_in_dim(zone, shard[None], self.me, 0)
                self.ready[n] = _to_kernel_layout(n, _from_blocks(blocks, n))
        return self.ready[name]


def kernel(x, mem, w_in, w_mem_kv, q_a_gain, w_q_b, kv_a_gain, w_kv_b, w_branch_mla, w_branch_sb, w_branch_mem, w_merge_gate, b_merge_gate, w_out, ln_gain, ln_bias, loss_target, m_w_in, m_w_mem_kv, m_q_a_gain, m_w_q_b, m_kv_a_gain, m_w_kv_b, m_w_branch_mla, m_w_branch_sb, m_w_branch_mem, m_w_merge_gate, m_b_merge_gate, m_w_out, m_ln_gain, m_ln_bias, v_w_in, v_w_mem_kv, v_q_a_gain, v_w_q_b, v_kv_a_gain, v_w_kv_b, v_w_branch_mla, v_w_branch_sb, v_w_branch_mem, v_w_merge_gate, v_b_merge_gate, v_w_out, v_ln_gain, v_ln_bias):
    given = dict(locals())
    small_names = [n for n, _ in SMALL]
    smalls = lambda prefix: [given[prefix + n] for n in small_names]
    me = 4 * lax.axis_index("x") + 2 * lax.axis_index("y") + lax.axis_index("c")

    gathers = [_exchange_start([given[n][0].astype(BF16) for n in group], gather=True, name=f"gather_start_{gi}")
               for gi, group in enumerate(GATHER_GROUPS)]
    w = _Weights(gathers, {n: given[n] for n in small_names}, me)
    exchanges = []
    results = [{}, {}, {}, {}]

    def finish(gi, after):
        names, started = exchanges[gi]
        sent, zones = _exchange_wait(started, after, name=f"grads_wait_{gi}")
        done = []
        for n, blocks, zone in zip(names, sent, zones, strict=True):
            own = lax.dynamic_index_in_dim(blocks, me, 0, keepdims=True)
            contrib = lax.dynamic_update_slice_in_dim(zone, own, me, 0)
            outs = _adamw(contrib, given[n][0], given["m_" + n][0], given["v_" + n][0], name=f"adamw_{n}")
            for kind, res in zip(results, outs, strict=True):
                kind[n] = res[None]
            done.append(outs[1])
        return done

    def emit(grads):
        blocks = [_to_blocks(_from_kernel_layout(n, g), n).astype(BF16) for n, g in grads.items()]
        exchanges.append((tuple(grads), _exchange_start(blocks, gather=False, name=f"grads_start_{len(exchanges)}")))
        started = [exchanges[-1][1]["started"]]
        if len(exchanges) == len(GRAD_GROUPS):
            for gi in range(len(GRAD_GROUPS) - 1):
                started += finish(gi, started[:1])
        return started

    loss, grad_x, grads = _local_step(x[0], mem[0], loss_target[0], w, emit)

    contrib_small = _share_small(_pack_small([grads[n] for n in small_names], loss), name="share_small")
    sml = _adamw(contrib_small, _pack_small(smalls("")), _pack_small(smalls("m_")), _pack_small(smalls("v_")), name="adamw_small")
    for kind, packed in zip(results, sml, strict=True):
        kind.update(zip(small_names, _unpack_small(packed), strict=True))
    finish(len(GRAD_GROUPS) - 1, [grad_x])
    order = ["w_in", "w_mem_kv", "q_a_gain", "w_q_b", "kv_a_gain", "w_kv_b", "w_branch_mla", "w_branch_sb", "w_branch_mem",
             "w_merge_gate", "b_merge_gate", "w_out", "ln_gain", "ln_bias"]
    loss_out = sml[0].reshape(-1)[LOSS_INDEX]
    return (loss_out, grad_x[None], *[kind[n] for kind in results for n in order])
```

```python
import math

import jax
import jax.numpy as jnp
from jax import lax
from jax.experimental import pallas as pl
from jax.experimental.pallas import tpu as pltpu

F32, BF16 = jnp.float32, jnp.bfloat16

N_DEV = 8
D_MODEL = 1024
MLA_HEADS, MLA_NOPE, MLA_ROPE, MLA_V = 8, 64, 32, 64
MLA_Q_LORA, MLA_KV_LORA = 256, 128
SB_HEAD_DIM = 64
MEM_HEAD_DIM = 128
ROPE_BASE = 10000.0
RMS_EPS = 1e-6
LN_EPS = 1e-5
DEEPNORM_ALPHA = 2.0 ** 0.25
ADAM_LR, ADAM_B1, ADAM_B2, ADAM_EPS, ADAM_WD, ADAM_STEP = 0.001, 0.9, 0.999, 1e-08, 0.01, 10
LOG2E, LN2 = math.log2(math.e), math.log(2.0)

LANES = 128
GROUPS = 4
PROJ_WIDTH = 4096
COL_CQ, COL_CKV, COL_KROPE, COL_GATE_A, COL_GATE_B, COL_GATE_M = 0, 256, 384, 512, 1024, 1536
QKV_FIRST, QKV_WIDTH = 2048, 2048
COL_QB, COL_KB, COL_VB, COL_QM = 0, 512, 1024, 1536
IN_PIECES = ((0, 416), None, (416, 512), (2464, 512), (3488, 512), (928, 512), (1440, 512), (1952, 512), (2976, 512))
IN_PAD = 96

VMEM_LIMIT_BYTES = 56 * 1024 * 1024
NEG_BIG = -1e30
Q_BLOCK = 512
MEM_Q_BLOCK = 2048
SB_BWD_Q_BLOCK = 512
TRI_BLOCK = 256
TILE_ROWS = 64
KEY_CHUNK = 512

SHARDED = {
    "w_in": ((1024, 4000), 1), "w_mem_kv": ((1024, 1024), 0), "w_q_b": ((256, 768), 1), "w_kv_b": ((128, 1024), 1),
    "w_branch_mla": ((512, 1024), 1), "w_branch_sb": ((512, 1024), 1), "w_branch_mem": ((512, 1024), 1),
    "w_merge_gate": ((1024, 3072), 1), "w_out": ((1024, 1024), 0),
}
GATHER_GROUPS = (("w_in",), ("w_merge_gate",), ("w_q_b", "w_kv_b", "w_mem_kv", "w_branch_mla", "w_branch_sb", "w_branch_mem", "w_out"))
GRAD_GROUPS = (("w_out", "w_merge_gate", "w_branch_mla", "w_branch_sb", "w_branch_mem"), ("w_mem_kv", "w_q_b", "w_kv_b"), ("w_in",))
SMALL = (("q_a_gain", 256), ("kv_a_gain", 128), ("b_merge_gate", 3072), ("ln_gain", 1024), ("ln_bias", 1024))
SMALL_ROWS, SMALL_LANES = 48, 128
ADAM_ROWS = 256
LOSS_INDEX = 5504


def _cparams(*sem):
    return pltpu.CompilerParams(dimension_semantics=sem or None, vmem_limit_bytes=VMEM_LIMIT_BYTES)


_DIMS = {"nn": (((1,), (0,)), ((), ())), "nt": (((1,), (1,)), ((), ())), "tn": (((0,), (0,)), ((), ()))}


def _dot(a, b, dims):
    return lax.dot_general(a, b, _DIMS[dims], preferred_element_type=F32)


def _tile(dim, want):
    if dim <= want:
        return dim
    t = want - want % LANES
    while dim % t:
        t -= LANES
    assert t > 0, (dim, want)
    return t


_ANY = pl.BlockSpec(memory_space=pl.ANY)


def _mm(a, b, dims, *, name, out_dtype=F32, add=None, add_scale=1.0, col_scale=None, b_cols=None, behind=None,
        tm=1024, tn=1024, tk=1024):
    batch = a.shape[0] if a.ndim == 3 else None
    if dims == "nn":
        (m, k), (k2, n) = a.shape[-2:], b.shape[-2:]
    elif dims == "nt":
        (m, k), (n, k2) = a.shape[-2:], b.shape[-2:]
    else:
        (k, m), (k2, n) = a.shape[-2:], b.shape[-2:]
    assert k == k2 and a.ndim == b.ndim, (a.shape, b.shape, dims)
    assert batch is None or (b.shape[0] == batch and add is None and col_scale is None and b_cols is None)
    b_first = 0
    if b_cols is not None:
        assert dims == "nn"
        b_first, n = b_cols
    tm, tn, tk = _tile(m, tm), _tile(n, tn), _tile(k, tk)
    assert b_first % tn == 0
    jb = b_first // tn
    nk = k // tk

    def spec(block, index):
        if batch is None:
            return pl.BlockSpec(block, lambda bb, i, j, kk: index(i, j, kk))
        return pl.BlockSpec((None, *block), lambda bb, i, j, kk: (bb, *index(i, j, kk)))

    a_spec = spec((tk, tm), lambda i, j, kk: (kk, i)) if dims == "tn" else spec((tm, tk), lambda i, j, kk: (i, kk))
    b_spec = spec((tn, tk), lambda i, j, kk: (j, kk)) if dims == "nt" else spec((tk, tn), lambda i, j, kk: (kk, jb + j))
    o_spec = spec((tm, tn), lambda i, j, kk: (i, j))
    behind = [] if behind is None else behind if isinstance(behind, (list, tuple)) else [behind]
    optional = [(add, o_spec), (col_scale, pl.BlockSpec((1, tn), lambda bb, i, j, kk: (0, j))), *[(v, _ANY) for v in behind]]
    present = [(v, spec) for v, spec in optional if v is not None]

    def body(*refs):
        a_ref, b_ref = refs[:2]
        extra = iter(refs[2:2 + len(present)])
        add_ref = next(extra) if add is not None else None
        scale_ref = next(extra) if col_scale is not None else None
        o_ref = refs[2 + len(present)]
        part = _dot(a_ref[...].astype(BF16), b_ref[...].astype(BF16), dims)

        def finish(r):
            if add is not None:
                r = r + add_scale * add_ref[...]
            if col_scale is not None:
                r = r * scale_ref[...]
            o_ref[...] = r.astype(out_dtype)

        if nk == 1:
            finish(part)
            return
        acc = refs[-1]
        kk = pl.program_id(3)

        @pl.when(kk == 0)
        def _():
            acc[...] = part

        @pl.when(kk > 0)
        def _():
            acc[...] += part

        @pl.when(kk == nk - 1)
        def _():
            finish(acc[...])

    return pl.pallas_call(
        body, name=name, grid=(batch or 1, m // tm, n // tn, nk),
        in_specs=[a_spec, b_spec] + [spec for _, spec in present], out_specs=o_spec,
        out_shape=jax.ShapeDtypeStruct((m, n) if batch is None else (batch, m, n), out_dtype),
        scratch_shapes=[pltpu.VMEM((tm, tn), F32)] if nk > 1 else [],
        compiler_params=_cparams("parallel", "parallel", "parallel", "arbitrary"),
    )(a, b, *[v for v, _ in present])


def _rowwise(fn, ins, outs, *, name, rows, tr=512):
    n_in = len(ins)
    tr = min(tr, rows)
    in_specs, args = [], []
    for it in ins:
        if isinstance(it, tuple) and it[0] == "whole":
            in_specs.append(pl.BlockSpec(it[1].shape, lambda i, nd=it[1].ndim: (0,) * nd))
            args.append(it[1])
            continue
        arr, w, off = it if isinstance(it, tuple) else (it, it.shape[-1], 0)
        assert off % w == 0
        cb = off // w
        if arr.ndim == 3:
            in_specs.append(pl.BlockSpec((arr.shape[0], tr, w), lambda i, cb=cb: (0, i, cb)))
        elif arr.shape[0] == 1:
            in_specs.append(pl.BlockSpec((1, w), lambda i, cb=cb: (0, cb)))
        else:
            in_specs.append(pl.BlockSpec((tr, w), lambda i, cb=cb: (i, cb)))
        args.append(arr)
    out_shape, out_specs, is_sum = [], [], []
    for out in outs:
        is_sum.append(out[0] == "sum")
        if out[0] == "sum":
            shape = out[1] if isinstance(out[1], tuple) else (1, out[1])
            out_shape.append(jax.ShapeDtypeStruct(shape, F32))
            out_specs.append(pl.BlockSpec(shape, lambda i: (0, 0)))
        elif len(out) == 3:
            out_shape.append(jax.ShapeDtypeStruct((out[0], rows, out[1]), out[2]))
            out_specs.append(pl.BlockSpec((out[0], tr, out[1]), lambda i: (0, i, 0)))
        else:
            out_shape.append(jax.ShapeDtypeStruct((rows, out[0]), out[1]))
            out_specs.append(pl.BlockSpec((tr, out[0]), lambda i: (i, 0)))

    def body(*refs):
        res = fn(*[r[...] for r in refs[:n_in]])
        for r, val, s in zip(refs[n_in:], res, is_sum, strict=True):
            if s:
                @pl.when(pl.program_id(0) == 0)
                def _(r=r):
                    r[...] = jnp.zeros_like(r)

                r[...] += val
            elif isinstance(val, (list, tuple)):
                for n, part in enumerate(val):
                    r[n] = part.astype(r.dtype)
            else:
                r[...] = val.astype(r.dtype)

    return pl.pallas_call(
        body, name=name, grid=(rows // tr,), in_specs=in_specs, out_specs=out_specs, out_shape=out_shape,
        compiler_params=_cparams("arbitrary"),
    )(*args)


def _colsum(v):
    return jnp.sum(v, axis=0, keepdims=True)


def _sigmoid(v):
    return 1.0 / (1.0 + jnp.exp(-v))


def _lane_groups(v):
    return [v[:, g * LANES:(g + 1) * LANES] for g in range(v.shape[1] // LANES)]


def _swap_halves(v, first_lane):
    lane = lax.broadcasted_iota(jnp.int32, v.shape, 1)
    return jnp.where(lane < first_lane + 16, pltpu.roll(v, 112, axis=1), pltpu.roll(v, 16, axis=1))


def _lane_sum(acc, v):
    for part in _lane_groups(v):
        acc = acc + part
    return acc


def _low_half(shape):
    return lax.broadcasted_iota(jnp.int32, shape, 1) < LANES // 2


def _select_heads(per_head, pick):
    if len(per_head) == 1:
        return pick(per_head[0], 0)
    return jnp.where(_low_half(per_head[0].shape), pick(per_head[0], 0), pick(per_head[1], 1))


def _attn_specs(s, sk, hp, bq, q0, k0, v0):
    wq = hp * LANES
    assert q0 % wq == 0 and k0 % wq == 0 and v0 % LANES == 0
    qb0, kb0, vb0 = q0 // wq, k0 // wq, v0 // LANES
    q_spec = pl.BlockSpec((bq, wq), lambda g, i: (i, qb0 + g))
    k_spec = pl.BlockSpec((sk, wq), lambda g, i: (0, kb0 + g))
    v_spec = pl.BlockSpec((sk, LANES), lambda g, i: (0, vb0 + g))
    row_out = lambda w: pl.BlockSpec((bq, w), lambda g, i: (i, g))
    key_out = lambda w: pl.BlockSpec((sk, w), lambda g, i: (0, g))
    return q_spec, k_spec, v_spec, row_out, key_out


def _chunks(i, bq, ch, sk, causal):
    return ((i + 1) * bq - 1) // ch if causal else jnp.int32(sk // ch - 1)


def _positions(i, c, bq, ch):
    return (i * bq + lax.broadcasted_iota(jnp.int32, (bq, ch), 0), c * ch + lax.broadcasted_iota(jnp.int32, (bq, ch), 1))


def _softmax_fwd(q, k, v, *, hp, causal, name, q0=0, k0=0, v0=0, q_rows=Q_BLOCK):
    s, sk = q.shape[0], k.shape[0]
    bq, ch = min(q_rows, s), min(KEY_CHUNK, sk)
    assert not causal or bq <= ch
    q_spec, k_spec, v_spec, row_out, _ = _attn_specs(s, sk, hp, bq, q0, k0, v0)

    def body(q_ref, k_ref, v_ref, o_ref, lse_ref, s_scr):
        i = pl.program_id(1)
        qs = _lane_groups(q_ref[...])
        last = _chunks(i, bq, ch, sk, causal)

        def scores(c, ms, masked):
            off = pl.multiple_of(c * ch, ch)
            out = []
            for j in range(hp):
                sc = _dot(qs[j], k_ref[pl.ds(off, ch), j * LANES:(j + 1) * LANES], "nt")
                if masked:
                    qpos, kpos = _positions(i, c, bq, ch)
                    sc = jnp.where(kpos <= qpos, sc, NEG_BIG)
                s_scr[j, c] = sc
                m = ms[j]
                for part in _lane_groups(sc):
                    m = jnp.maximum(m, part)
                out.append(m)
            return tuple(out)

        ms = lax.fori_loop(0, last, lambda c, m: scores(c, m, False), tuple(jnp.full((bq, LANES), NEG_BIG, F32) for _ in range(hp)))
        ms = scores(last, ms, causal)
        row_max = [jnp.max(m, axis=1, keepdims=True) for m in ms]

        def weigh(c, carry):
            off = pl.multiple_of(c * ch, ch)
            vt = v_ref[pl.ds(off, ch), :]
            out = []
            for j in range(hp):
                l, acc = carry[j]
                p = jnp.exp2(s_scr[j, c] - row_max[j])
                out.append((_lane_sum(l, p), acc + _dot(p.astype(BF16), vt, "nn")))
            return tuple(out)

        zero = jnp.zeros((bq, LANES), F32)
        res = lax.fori_loop(0, last + 1, weigh, tuple((zero, zero) for _ in range(hp)))
        row_sum = [jnp.sum(l, axis=1, keepdims=True) for l, _ in res]
        o_ref[...] = _select_heads([acc for _, acc in res], lambda acc, j: acc / row_sum[j])
        lse_ref[...] = _select_heads([jnp.broadcast_to(row_max[j] + jnp.log2(row_sum[j]), (bq, LANES)) for j in range(hp)], lambda a, j: a)

    return pl.pallas_call(
        body, name=name, grid=(GROUPS, s // bq), in_specs=[q_spec, k_spec, v_spec], out_specs=[row_out(LANES), row_out(LANES)],
        out_shape=[jax.ShapeDtypeStruct((s, GROUPS * LANES), F32)] * 2,
        scratch_shapes=[pltpu.VMEM((hp, sk // ch, bq, ch), F32)], compiler_params=_cparams("parallel", "arbitrary"),
    )(q, k, v)


def _head_cotangent(do, j, hp):
    if hp == 1:
        return do
    return jnp.where(_low_half(do.shape) == (j == 0), do, 0.0)


def _softmax_bwd(q, k, v, o, do, lse, behind, *, hp, causal, dq_scale, name, q0=0, k0=0, v0=0, q_rows=Q_BLOCK):
    s, sk = q.shape[0], k.shape[0]
    bq, ch = min(q_rows, s), min(KEY_CHUNK, sk)
    assert not causal or bq <= ch
    wq = hp * LANES
    q_spec, k_spec, v_spec, row_out, key_out = _attn_specs(s, sk, hp, bq, q0, k0, v0)

    def body(q_ref, k_ref, v_ref, o_ref, do_ref, lse_ref, _, dq_ref, dk_ref, dv_ref, dk_t, dv_t):
        i = pl.program_id(1)

        @pl.when(i == 0)
        def _():
            dk_t[...] = jnp.zeros_like(dk_t)
            dv_t[...] = jnp.zeros_like(dv_t)

        qs = _lane_groups(q_ref[...])
        do_all, o_all, lse_all = do_ref[...], o_ref[...], lse_ref[...]
        dos, deltas, lses = [], [], []
        for j in range(hp):
            d = _head_cotangent(do_all, j, hp)
            deltas.append(jnp.sum(d * o_all, axis=1, keepdims=True))
            dos.append(d.astype(BF16))
            lses.append(lse_all[:, j * (LANES // hp):j * (LANES // hp) + 1])
        last = _chunks(i, bq, ch, sk, causal)

        def chunk(c, dqs, masked):
            off = pl.multiple_of(c * ch, ch)
            vt = v_ref[pl.ds(off, ch), :]
            out, dks, dv = [], [], None
            for j in range(hp):
                kt = k_ref[pl.ds(off, ch), j * LANES:(j + 1) * LANES]
                p = jnp.exp2(_dot(qs[j], kt, "nt") - lses[j])
                if masked:
                    qpos, kpos = _positions(i, c, bq, ch)
                    p = jnp.where(kpos <= qpos, p, 0.0)
                ds = (p * (_dot(dos[j], vt, "nt") - deltas[j]) * LN2).astype(BF16)
                out.append(dqs[j] + _dot(ds, kt, "nn"))
                dks.append(_dot(qs[j], ds, "tn"))
                dvj = _dot(dos[j], p.astype(BF16), "tn")
                dv = dvj if dv is None else dv + dvj
            dk_t[c] += dks[0] if hp == 1 else jnp.concatenate(dks, axis=0)
            dv_t[c] += dv
            return tuple(out)

        dqs = lax.fori_loop(0, last, lambda c, d: chunk(c, d, False), tuple(jnp.zeros((bq, LANES), F32) for _ in range(hp)))
        dqs = chunk(last, dqs, causal)
        dq_ref[...] = (dqs[0] if hp == 1 else jnp.concatenate(dqs, axis=1)) * dq_scale

        @pl.when(i == s // bq - 1)
        def _():
            for c in range(sk // ch):
                dk_ref[c * ch:(c + 1) * ch, :] = dk_t[c].T
                dv_ref[c * ch:(c + 1) * ch, :] = dv_t[c].T

    return pl.pallas_call(
        body, name=name, grid=(GROUPS, s // bq),
        in_specs=[q_spec, k_spec, v_spec, row_out(LANES), row_out(LANES), row_out(LANES), _ANY],
        out_specs=[row_out(wq), key_out(wq), key_out(LANES)],
        out_shape=[jax.ShapeDtypeStruct((s, GROUPS * wq), F32), jax.ShapeDtypeStruct((sk, GROUPS * wq), F32),
                   jax.ShapeDtypeStruct((sk, GROUPS * LANES), F32)],
        scratch_shapes=[pltpu.VMEM((sk // ch, wq, ch), F32), pltpu.VMEM((sk // ch, LANES, ch), F32)],
        compiler_params=_cparams("arbitrary", "arbitrary"),
    )(q, k, v, o, do, lse, behind)


def _log2_sigmoid_pair(z2):
    minus_abs = lax.bitcast_convert_type(lax.bitcast_convert_type(z2, jnp.uint32) | jnp.uint32(0x80000000), F32)
    log_beta = jnp.minimum(z2, 0.0) - jnp.log2(1.0 + jnp.exp2(minus_abs))
    return log_beta, log_beta - z2


def _tilewise(fn, *arrays):
    rows, cols = arrays[0].shape
    step = min(TILE_ROWS, rows)
    grid = [[fn(*[None if a is None else a[r:r + step, c:c + LANES] for a in arrays]) for c in range(0, cols, LANES)]
            for r in range(0, rows, step)]
    return [jnp.concatenate([jnp.concatenate([cell[k] for cell in row], axis=1) for row in grid], axis=0)
            for k in range(len(grid[0][0]))]


def _split(v):
    hi = v.astype(BF16)
    return hi, (v - hi.astype(F32)).astype(BF16)


def _tri(n, after):
    rows, cols = lax.broadcasted_iota(jnp.int32, (n, n), 0), lax.broadcasted_iota(jnp.int32, (n, n), 1)
    return (rows > cols if after else rows < cols).astype(BF16)


def _running_sums(v, terms, start, tri, backwards):
    n = tri.shape[0]
    n_blocks = v.shape[1] // n
    order = range(n_blocks - 1, -1, -1) if backwards else range(n_blocks)
    stacked = tri if len(terms) == 1 else jnp.concatenate([tri] * len(terms), axis=0)
    parts, run = [None] * n_blocks, start
    for t in order:
        cols = slice(t * n, (t + 1) * n)
        lhs = terms[0][:, cols] if len(terms) == 1 else jnp.concatenate([term[:, cols] for term in terms], axis=1)
        parts[t] = _dot(lhs, stacked, "nn") + run
        run = run + jnp.sum(v[:, cols], axis=1, keepdims=True)
    return (parts[0] if n_blocks == 1 else jnp.concatenate(parts, axis=1)), run


def _sb_weights(qm, kt, run, tri, strict):
    def logs(z2, keep):
        log_beta, log_keep = _log2_sigmoid_pair(z2)
        if keep is not None:
            log_keep = jnp.where(keep, log_keep, 0.0)
        return log_beta, log_keep, *_split(log_keep)

    log_beta, log_keep, hi, lo = _tilewise(logs, _dot(qm, kt, "nt"), strict)
    behind, run = _running_sums(log_keep, (hi, lo), run, tri, True)

    def weigh(log_beta, behind, keep):
        a = jnp.exp2(log_beta + behind)
        return (a if keep is None else jnp.where(keep, a, 0.0),)

    (a,) = _tilewise(weigh, log_beta, behind, strict)
    return a, log_beta, run


class _Copies:
    def __init__(self, copies):
        self.copies = copies

    def start(self):
        for cp in self.copies:
            cp.start()

    def wait(self):
        for cp in self.copies:
            cp.wait()


def _sb_queries(q_all):
    low = _low_half(q_all.shape)
    zero = jnp.zeros_like(q_all)
    return [jnp.where(low, q_all, zero), jnp.where(low, zero, q_all)]


def _sb_fwd(qkv, *, q0, k0, v0, name):
    s = qkv.shape[0]
    bq, ch = min(Q_BLOCK, s), min(KEY_CHUNK, s)
    assert bq == ch
    n_q = s // bq
    q_spec, k_spec, v_spec, row_out, _ = _attn_specs(s, s, 1, bq, q0, k0, v0)

    def body(q_ref, k_ref, v_ref, o_ref, saved_ref, stage, sems):
        g, i = pl.program_id(0), pl.program_id(1)
        qms = _sb_queries(q_ref[...])
        tri = _tri(min(TRI_BLOCK, ch), True)
        last = _chunks(i, bq, ch, s, True)
        first_tile = i * (i + 1) // 2
        steps_before = g * (n_q * (n_q + 1) // 2) + first_tile

        def save(slot, c):
            return _Copies([pltpu.make_async_copy(stage.at[slot, p], saved_ref.at[g, first_tile + c, p], sems.at[slot, p])
                            for p in range(4)])

        def chunk(c, step, carry, masked):
            off = pl.multiple_of(c * ch, ch)
            kt, vt = k_ref[pl.ds(off, ch), :], v_ref[pl.ds(off, ch), :]
            strict = None
            if masked:
                qpos, kpos = _positions(i, c, bq, ch)
                strict = kpos < qpos
            slot = (steps_before + step) % 3

            @pl.when(steps_before + step >= 3)
            def _():
                save(slot, c).wait()

            out = []
            for j in range(2):
                run, acc = carry[j]
                a, log_beta, run = _sb_weights(qms[j], kt, run, tri, strict)
                a = a.astype(BF16)
                stage[slot, j] = a
                stage[slot, 2 + j] = jnp.exp2(log_beta).astype(BF16)
                out.append((run, acc + _dot(a, vt, "nn")))
            save(slot, c).start()
            return tuple(out)

        carry = chunk(last, 0, tuple((jnp.zeros((bq, 1), F32), jnp.zeros((bq, LANES), F32)) for _ in range(2)), True)
        res = lax.fori_loop(0, last, lambda n, c: chunk(last - 1 - n, n + 1, c, False), carry)
        @pl.when((g == GROUPS - 1) & (i == n_q - 1))
        def _():
            for back in range(3):
                save((steps_before + last - back) % 3, 0).wait()

        o_ref[...] = _select_heads([acc for _, acc in res], lambda acc, j: acc)

    return pl.pallas_call(
        body, name=name, grid=(GROUPS, n_q), in_specs=[q_spec, k_spec, v_spec], out_specs=[row_out(LANES), _ANY],
        out_shape=[jax.ShapeDtypeStruct((s, GROUPS * LANES), F32),
                   jax.ShapeDtypeStruct((GROUPS, n_q * (n_q + 1) // 2, 4, bq, ch), BF16)],
        scratch_shapes=[pltpu.VMEM((3, 4, bq, ch), BF16), pltpu.SemaphoreType.DMA((3, 4))],
        compiler_params=_cparams("arbitrary", "arbitrary"),
    )(qkv, qkv, qkv)


def _sb_bwd(qkv, do, saved, behind, *, q0, k0, v0, dq_scale, name):
    s = qkv.shape[0]
    bq, ch = min(SB_BWD_Q_BLOCK, s), min(KEY_CHUNK, s)
    assert bq == ch and saved.shape[2:] == (4, bq, ch)
    q_spec, k_spec, v_spec, row_out, key_out = _attn_specs(s, s, 1, bq, q0, k0, v0)

    def body(q_ref, k_ref, v_ref, do_ref, saved_ref, _, dq_ref, dk_ref, dv_ref, g_s, beta_s, stage, sems, dv_t):
        g_index, i = pl.program_id(0), pl.program_id(1)

        @pl.when(i == 0)
        def _():
            dk_ref[...] = jnp.zeros_like(dk_ref)
            dv_t[...] = jnp.zeros_like(dv_t)

        qms = _sb_queries(q_ref[...])
        do_all = do_ref[...]
        dos = [_head_cotangent(do_all, j, 2).astype(BF16) for j in range(2)]
        dos_ln2 = [(_head_cotangent(do_all, j, 2) * LN2).astype(BF16) for j in range(2)]
        tri_before = _tri(min(TRI_BLOCK, ch), False)
        last = _chunks(i, bq, ch, s, True)
        first_tile = i * (i + 1) // 2

        def strict_mask(c):
            qpos, kpos = _positions(i, c, bq, ch)
            return kpos < qpos

        def fetch_tile(slot, group, tile):
            return _Copies([pltpu.make_async_copy(saved_ref.at[group, tile, p], stage.at[slot, p], sems.at[slot, p])
                            for p in range(4)])

        def fetch(slot, c):
            return fetch_tile(slot, g_index, first_tile + c)

        def fetch_first_two(group, block):
            tile = block * (block + 1) // 2 + block
            fetch_tile(0, group, tile).start()

            @pl.when(block >= 1)
            def _():
                fetch_tile(1, group, tile - 1).start()

            @pl.when(block >= 2)
            def _():
                fetch_tile(2, group, tile - 2).start()

        def sweep1(n, unused):
            c, slot = last - n, n % 3

            @pl.when((c >= 2) & (n >= 1))
            def _():
                fetch((n + 2) % 3, c - 2).start()

            fetch(slot, c).wait()
            off = pl.multiple_of(c * ch, ch)
            vt = v_ref[pl.ds(off, ch), :]
            dv = None
            for j in range(2):
                a = stage[slot, j]
                g_s[j, c] = (a.astype(F32) * _dot(dos_ln2[j], vt, "nt")).astype(BF16)
                beta_s[j, c] = stage[slot, 2 + j]
                dvj = _dot(dos[j], a, "tn")
                dv = dvj if dv is None else dv + dvj
            dv_t[c] += dv
            return unused

        @pl.when((g_index == 0) & (i == 0))
        def _():
            fetch_first_two(g_index, i)

        lax.fori_loop(0, last + 1, sweep1, 0)

        n_q = s // bq
        next_block = jnp.where(i + 1 < n_q, i + 1, 0)
        next_group = jnp.where(i + 1 < n_q, g_index, g_index + 1)

        @pl.when(next_group < GROUPS)
        def _():
            fetch_first_two(next_group, next_block)

        def sweep2(c, carry, masked):
            off = pl.multiple_of(c * ch, ch)
            kt = k_ref[pl.ds(off, ch), :]
            out, dk = [], None
            for j in range(2):
                before, dq = carry[j]
                g16, beta = g_s[j, c], beta_s[j, c].astype(F32)
                g = g16.astype(F32)
                in_front, before = _running_sums(g, (g16,), before, tri_before, False)
                dz = g * (1.0 - beta) - beta * in_front
                if masked:
                    dz = jnp.where(strict_mask(c), dz, 0.0)
                dz = dz.astype(BF16)
                dkj = _dot(dz, qms[j], "tn")
                dk = dkj if dk is None else dk + dkj
                out.append((before, dq + _dot(dz, kt, "nn")))
            dk_ref[pl.ds(off, ch), :] += dk
            return tuple(out)

        carry = lax.fori_loop(0, last, lambda c, cr: sweep2(c, cr, False),
                              tuple((jnp.zeros((bq, 1), F32), jnp.zeros((bq, LANES), F32)) for _ in range(2)))
        res = sweep2(last, carry, True)
        dq_ref[...] = _select_heads([dq for _, dq in res], lambda dq, j: dq) * dq_scale

        @pl.when(i == s // bq - 1)
        def _():
            for c in range(n_ch):
                dv_ref[c * ch:(c + 1) * ch, :] = dv_t[c].T

    n_ch = s // ch
    return pl.pallas_call(
        body, name=name, grid=(GROUPS, s // bq), in_specs=[q_spec, k_spec, v_spec, row_out(LANES), _ANY, _ANY],
        out_specs=[row_out(LANES), key_out(LANES), key_out(LANES)],
        out_shape=[jax.ShapeDtypeStruct((s, GROUPS * LANES), F32)] * 3,
        scratch_shapes=[pltpu.VMEM((2, n_ch, bq, ch), BF16)] * 2 + [pltpu.VMEM((3, 4, bq, ch), BF16), pltpu.SemaphoreType.DMA((3, 4)),
                        pltpu.VMEM((n_ch, LANES, ch), F32)],
        compiler_params=_cparams("arbitrary", "arbitrary"),
    )(qkv, qkv, qkv, do, saved, behind)


def _rope_tables(s):
    half = MLA_ROPE // 2
    freqs = ROPE_BASE ** (-jnp.arange(half, dtype=F32) / half)
    ang = jnp.arange(s, dtype=F32)[:, None] * freqs[None, :]
    cos, sin = jnp.cos(ang), jnp.sin(ang)
    tail = jnp.zeros((s, LANES - MLA_NOPE - MLA_ROPE), F32)
    lead = lambda fill: jnp.full((s, MLA_NOPE), fill, F32)
    return dict(
        cos_k0=jnp.concatenate([cos, cos, lead(0.0), tail], axis=1), sin_k0=jnp.concatenate([-sin, sin, lead(0.0), tail], axis=1),
        cos_k64=jnp.concatenate([lead(0.0), cos, cos, tail], axis=1), sin_k64=jnp.concatenate([lead(0.0), -sin, sin, tail], axis=1),
        cos_q=jnp.concatenate([lead(1.0), cos, cos, tail], axis=1),
        sin_k64_t=jnp.concatenate([lead(0.0), sin, -sin, tail], axis=1),
    )


def _local_step(x, mem, target, w, emit=lambda grads: [jnp.zeros((8, LANES), F32)]):
    s = x.shape[0]
    rope = _rope_tables(s)
    xb = x.astype(BF16)
    inv_d = 1.0 / D_MODEL
    scale_a = LOG2E / math.sqrt(MLA_NOPE + MLA_ROPE)
    scale_b = LOG2E / math.sqrt(SB_HEAD_DIM)
    scale_m = LOG2E / math.sqrt(MEM_HEAD_DIM)
    arrive_after = getattr(w, "arrive_after", lambda *values: None)
    memb = mem.astype(BF16)

    arrive_after(xb, memb, *rope.values())
    proj = _mm(xb, w["w_in"], "nn", name="proj", b_cols=(0, QKV_FIRST))
    one = jnp.ones((1, 512), F32)
    qkv = _mm(xb, w["w_in"], "nn", name="proj_qkv", b_cols=(QKV_FIRST, QKV_WIDTH), out_dtype=BF16,
              col_scale=jnp.concatenate([one * scale_b, one, one, one * scale_m], axis=1))
    arrive_after(qkv)
    pre = _mm(xb, w["w_merge_gate"], "nn", name="merge_pre", out_dtype=BF16)
    arrive_after(pre)

    def mla_inputs(c_q, c_kv, k_rope, g_q, g_kv, w_q, w_kv, cos_q, sin_q, cos_k, sin_k):
        n_q = (c_q * lax.rsqrt(jnp.mean(c_q * c_q, axis=1, keepdims=True) + RMS_EPS) * g_q).astype(BF16)
        n_kv = (c_kv * lax.rsqrt(jnp.mean(c_kv * c_kv, axis=1, keepdims=True) + RMS_EPS) * g_kv).astype(BF16)
        q_a = _dot(n_q, w_q, "nn")
        kv_a = _dot(n_kv, w_kv, "nn").astype(BF16)
        q = jnp.concatenate([(g * cos_q + _swap_halves(g, MLA_NOPE) * sin_q) * scale_a for g in _lane_groups(q_a)], axis=1)
        k_pe = pltpu.roll(k_rope * cos_k + _swap_halves(k_rope, 0) * sin_k, MLA_NOPE, axis=1).astype(BF16)
        k = jnp.concatenate([g + k_pe for g in _lane_groups(kv_a[:, :1024])], axis=1)
        return n_q, n_kv, q, k, kv_a[:, 1024:]

    n_q, n_kv, q_mla, k_mla, v_a = _rowwise(
        mla_inputs, [(proj, 256, COL_CQ), (proj, 128, COL_CKV), (proj, 128, COL_KROPE), w["q_a_gain"], w["kv_a_gain"],
                     ("whole", w["w_q_b"]), ("whole", w["w_kv_b"]), rope["cos_q"], rope["sin_k64"], rope["cos_k0"], rope["sin_k0"]],
        [(256, BF16), (128, BF16), (1024, BF16), (1024, BF16), (512, BF16)], name="mla_inputs", rows=s)
    o_a, lse_a = _softmax_fwd(q_mla, k_mla, v_a, hp=2, causal=True, name="mla_fwd")

    o_b, sb_saved = _sb_fwd(qkv, q0=COL_QB, k0=COL_KB, v0=COL_VB, name="sb_fwd")

    mem_kv =_mm(memb, w["w_mem_kv"], "nn", name="mem_kv", out_dtype=BF16)
    o_m, lse_m = _softmax_fwd(qkv, mem_kv, mem_kv, hp=1, causal=False, name="mem_fwd", q0=1536, v0=512, q_rows=MEM_Q_BLOCK)

    branches = ("mla", "sb", "mem")
    w_branch = jnp.stack([w[f"w_branch_{br}"] for br in branches])
    bias = w["b_merge_gate"]

    def head(oa, ob, om, ga, gb, gm, pa, pb, pm, ba, bb, bm, xv, tv, gain, bias_ln, w_b, w_o, w_g):
        us, ys, gs = [], [], []
        for n, (o, gate, p, b) in enumerate(((oa, ga, pa, ba), (ob, gb, pb, bb), (om, gm, pm, bm))):
            us.append((o * gate * _sigmoid(gate)).astype(BF16))
            ys.append(_dot(us[n], w_b[n], "nn"))
            gs.append(_sigmoid(p.astype(F32) + b))
        merged = (gs[0] * ys[0] + gs[1] * ys[1] + gs[2] * ys[2]).astype(BF16)
        z = DEEPNORM_ALPHA * xv + _dot(merged, w_o, "nn")
        zc = z - jnp.mean(z, axis=1, keepdims=True)
        rstd = lax.rsqrt(jnp.mean(zc * zc, axis=1, keepdims=True) + LN_EPS)
        xhat = zc * rstd
        err = xhat * gain + bias_ln - tv
        loss = 0.5 * jnp.sum(jnp.mean(err * err, axis=1, keepdims=True), axis=0, keepdims=True)
        dy = err * inv_d
        dxhat = dy * gain
        dz = rstd * (dxhat - jnp.mean(dxhat, axis=1, keepdims=True) - xhat * jnp.mean(dxhat * xhat, axis=1, keepdims=True))
        dz16 = dz.astype(BF16)
        dm = _dot(dz16, w_o, "nt")
        dpre = jnp.concatenate([dm * ys[n] * gs[n] * (1.0 - gs[n]) for n in range(3)], axis=1)
        dx = DEEPNORM_ALPHA * dz + _dot(dpre.astype(BF16), w_g, "nt")
        dys = [(dm * gs[n]).astype(BF16) for n in range(3)]
        d_os, d_gates = [], []
        for n, (o, gate) in enumerate(((oa, ga), (ob, gb), (om, gm))):
            du, sg = _dot(dys[n], w_b[n], "nt"), _sigmoid(gate)
            d_os.append(du * gate * sg)
            d_gates.append(du * o * sg * (1.0 + gate * (1.0 - sg)))
        return (us, merged, dx, dz16, _colsum(dy * xhat), _colsum(dy), jnp.broadcast_to(loss, (1, LANES)), dpre, _colsum(dpre),
                dys, *d_os, *d_gates)

    grads = {}
    (u, merged, dx, dzb, grads["ln_gain"], grads["ln_bias"], loss, dpre, grads["b_merge_gate"], dy, *rest) = _rowwise(
        head, [o_a, o_b, o_m, (proj, 512, COL_GATE_A), (proj, 512, COL_GATE_B), (proj, 512, COL_GATE_M),
               (pre, 1024, 0), (pre, 1024, 1024), (pre, 1024, 2048), (bias, 1024, 0), (bias, 1024, 1024), (bias, 1024, 2048),
               x, target, w["ln_gain"], w["ln_bias"], ("whole", w_branch), ("whole", w["w_out"]), ("whole", w["w_merge_gate"])],
        [(3, 512, BF16), (1024, BF16), (1024, F32), (1024, BF16), ("sum", 1024), ("sum", 1024), ("sum", LANES),
         (3072, BF16), ("sum", 3072), (3, 1024, BF16)] + [(512, F32)] * 3 + [(512, BF16)] * 3, name="head", rows=s, tr=256)
    d_o, d_gate = dict(zip(branches, rest[:3], strict=True)), dict(zip(branches, rest[3:], strict=True))

    grads["w_out"] = _mm(merged, dzb, "tn", name="g_w_out", out_dtype=BF16)
    grads["w_merge_gate"] = _mm(xb, dpre, "tn", name="g_w_merge", out_dtype=BF16)
    g_w_branch = _mm(u, dy, "tn", name="g_w_branch", out_dtype=BF16)
    for n, br in enumerate(branches):
        grads[f"w_branch_{br}"] = g_w_branch[n]
    (sent,) = emit({n: grads[n] for n in ("w_out", "w_merge_gate", "w_branch_mla", "w_branch_sb", "w_branch_mem")})

    dq_m, dk_m, dv_m = _softmax_bwd(qkv, mem_kv, mem_kv, o_m, d_o["mem"], lse_m, sent, hp=1, causal=False, dq_scale=scale_m,
                                    name="mem_bwd", q0=1536, v0=512, q_rows=MEM_Q_BLOCK)
    grads["w_mem_kv"] = _mm(memb, jnp.concatenate([dk_m, dv_m], axis=1), "tn", name="g_w_mem_kv", out_dtype=BF16)

    dq_sb, dk_sb, dv_sb = _sb_bwd(qkv, d_o["sb"], sb_saved, sent, q0=COL_QB, k0=COL_KB, v0=COL_VB, dq_scale=scale_b, name="sb_bwd")

    dq_mla, dk_mla, dv_a = _softmax_bwd(q_mla, k_mla, v_a, o_a, d_o["mla"], lse_a, sent, hp=2, causal=True, dq_scale=scale_a,
                                        name="mla_bwd")

    def mla_inputs_bwd(dq, dk, dv, n_q, n_kv, c_q, c_kv, g_q, g_kv, w_q, w_kv, cos_q, sin_q, cos_k, sin_k):
        dq_a = jnp.concatenate([g * cos_q + _swap_halves(g, MLA_NOPE) * sin_q for g in _lane_groups(dq)], axis=1).astype(BF16)
        groups = _lane_groups(dk)
        g_rope = groups[0]
        for other in groups[1:]:
            g_rope = g_rope + other
        dk_rope = pltpu.roll(g_rope * cos_k + _swap_halves(g_rope, MLA_NOPE) * sin_k, MLA_NOPE, axis=1)
        nope = _low_half(g_rope.shape)
        dkv_a = jnp.concatenate([jnp.where(nope, grp, 0.0) for grp in groups] + [dv], axis=1).astype(BF16)
        res = []
        for c, dn, g in ((c_q, _dot(dq_a, w_q, "nt"), g_q), (c_kv, _dot(dkv_a, w_kv, "nt"), g_kv)):
            r = lax.rsqrt(jnp.mean(c * c, axis=1, keepdims=True) + RMS_EPS)
            t = dn * g
            res += [r * t - c * (r * r * r) * jnp.mean(c * t, axis=1, keepdims=True), _colsum(dn * c * r)]
        return *res, dk_rope, _dot(n_q, dq_a, "tn"), _dot(n_kv, dkv_a, "tn")

    dc_q, grads["q_a_gain"], dc_kv, grads["kv_a_gain"], dk_rope, g_w_q_b, g_w_kv_b = _rowwise(
        mla_inputs_bwd, [dq_mla, dk_mla, dv_a, n_q, n_kv, (proj, 256, COL_CQ), (proj, 128, COL_CKV), w["q_a_gain"], w["kv_a_gain"],
                         ("whole", w["w_q_b"]), ("whole", w["w_kv_b"]), rope["cos_q"], rope["sin_k64_t"], rope["cos_k64"], rope["sin_k64_t"]],
        [(256, BF16), ("sum", 256), (128, BF16), ("sum", 128), (128, BF16), ("sum", (MLA_Q_LORA, 1024)), ("sum", (MLA_KV_LORA, 1536))],
        name="mla_inputs_bwd", rows=s)
    grads["w_q_b"], grads["w_kv_b"] = g_w_q_b.astype(BF16), g_w_kv_b.astype(BF16)

    sent = emit({n: grads[n] for n in ("w_mem_kv", "w_q_b", "w_kv_b")})

    dproj = jnp.concatenate(
        [dc_q, dc_kv, dk_rope, d_gate["mla"], d_gate["sb"], d_gate["mem"], dq_sb.astype(BF16), dk_sb.astype(BF16),
         dv_sb.astype(BF16), dq_m.astype(BF16)], axis=1)
    grads["w_in"] = _mm(xb, dproj, "tn", name="g_w_in", out_dtype=BF16, behind=sent)
    sent = emit({"w_in": grads["w_in"]})
    grad_x = _mm(dproj, w["w_in"], "nt", name="grad_x", add=dx, behind=sent)
    return loss, grad_x, grads


def _shard_shape(shape, axis):
    return tuple(d // N_DEV if a == axis else d for a, d in enumerate(shape))


def _from_blocks(blocks, name):
    shape, axis = SHARDED[name]
    return blocks.reshape(shape) if axis == 0 else blocks.transpose(1, 0, 2).reshape(shape)


def _to_blocks(full, name):
    shape, axis = SHARDED[name]
    shp = _shard_shape(shape, axis)
    return full.reshape(N_DEV, *shp) if axis == 0 else full.reshape(shape[0], N_DEV, shp[1]).transpose(1, 0, 2)


def _pad_heads(a, used):
    rows = a.shape[0]
    a = a.reshape(rows, MLA_HEADS, used)
    return jnp.concatenate([a, jnp.zeros((rows, MLA_HEADS, LANES - used), a.dtype)], axis=2).reshape(rows, MLA_HEADS * LANES)


def _to_kernel_layout(name, full):
    if name == "w_in":
        return jnp.concatenate([jnp.zeros((D_MODEL, IN_PAD), full.dtype) if piece is None else full[:, piece[0]:piece[0] + piece[1]]
                                for piece in IN_PIECES], axis=1)
    if name == "w_q_b":
        return _pad_heads(full, MLA_NOPE + MLA_ROPE)
    if name == "w_kv_b":
        kv = full.reshape(MLA_KV_LORA, MLA_HEADS, MLA_NOPE + MLA_V)
        return jnp.concatenate([_pad_heads(kv[:, :, :MLA_NOPE].reshape(MLA_KV_LORA, -1), MLA_NOPE),
                                kv[:, :, MLA_NOPE:].reshape(MLA_KV_LORA, -1)], axis=1)
    return full


def _from_kernel_layout(name, g):
    if name == "w_in":
        placed, at = [], 0
        for piece in IN_PIECES:
            if piece is not None:
                placed.append((piece[0], g[:, at:at + piece[1]]))
            at += IN_PAD if piece is None else piece[1]
        return jnp.concatenate([cols for _, cols in sorted(placed, key=lambda item: item[0])], axis=1)
    if name == "w_q_b":
        return g.reshape(MLA_Q_LORA, MLA_HEADS, LANES)[:, :, :MLA_NOPE + MLA_ROPE].reshape(MLA_Q_LORA, -1)
    if name == "w_kv_b":
        return jnp.concatenate([g[:, :1024].reshape(MLA_KV_LORA, MLA_HEADS, LANES)[:, :, :MLA_NOPE],
                                g[:, 1024:].reshape(MLA_KV_LORA, MLA_HEADS, MLA_V)], axis=2).reshape(MLA_KV_LORA, -1)
    return g


def _pack_small(vectors, loss=None):
    flat = [v.reshape(-1) for v in vectors]
    flat.append(jnp.zeros((SMALL_ROWS * SMALL_LANES - LOSS_INDEX,), F32) if loss is None else
                jnp.concatenate([loss.reshape(-1)[:1], jnp.zeros((SMALL_ROWS * SMALL_LANES - LOSS_INDEX - 1,), F32)]))
    return jnp.concatenate(flat).reshape(SMALL_ROWS, SMALL_LANES)


def _unpack_small(packed):
    flat, res, off = packed.reshape(-1), [], 0
    for _, n in SMALL:
        res.append(flat[off:off + n].reshape(1, n))
        off += n
    return res


def _me_and_peers():
    x, y, c = lax.axis_index("x"), lax.axis_index("y"), lax.axis_index("c")
    peers = []
    for kk in range(1, N_DEV):
        px, py, pc = (x + (kk >> 2)) % 2, (y + ((kk >> 1) & 1)) % 2, (c + (kk & 1)) % 2
        peers.append(((px, py, pc), 4 * px + 2 * py + pc))
    return 4 * x + 2 * y + c, peers


def _share_small(small, *, name):
    def body(small_ref, all_ref, send_sems, recv_sems, local_sem):
        me, peers = _me_and_peers()
        copies = [pltpu.make_async_remote_copy(src_ref=small_ref, dst_ref=all_ref.at[me], send_sem=send_sems.at[kk], recv_sem=recv_sems.at[kk],
                                               device_id=pos, device_id_type=pl.DeviceIdType.MESH) for kk, (pos, _) in enumerate(peers)]
        copies.append(pltpu.make_async_copy(small_ref, all_ref.at[me], local_sem))
        for cp in copies:
            cp.start()
        for cp in copies:
            cp.wait()

    hbm = pl.BlockSpec(memory_space=pl.ANY)
    return pl.pallas_call(
        body, name=name, in_specs=[hbm], out_specs=hbm, out_shape=jax.ShapeDtypeStruct((N_DEV, *small.shape), small.dtype),
        scratch_shapes=[pltpu.SemaphoreType.DMA((N_DEV - 1,)), pltpu.SemaphoreType.DMA((N_DEV - 1,)), pltpu.SemaphoreType.DMA],
        compiler_params=pltpu.CompilerParams(has_side_effects=True),
    )(small)


_HBM = pl.BlockSpec(memory_space=pltpu.HBM)
_SEM = pl.BlockSpec(memory_space=pltpu.SEMAPHORE)


def _exchange_copies(srcs, zones, send_sems, recv_sems, gather):
    me, peers = _me_and_peers()
    return [pltpu.make_async_remote_copy(
        src_ref=srcs[t] if gather else srcs[t].at[peer], dst_ref=zones[t].at[me], send_sem=send_sems.at[7 * t + kk],
        recv_sem=recv_sems.at[7 * t + kk], device_id=pos, device_id_type=pl.DeviceIdType.MESH)
        for t in range(len(srcs)) for kk, (pos, peer) in enumerate(peers)]


def _exchange_start(tensors, *, gather, name):
    n = len(tensors)
    zones = [lax.empty((N_DEV, *(t.shape if gather else t.shape[1:])), t.dtype) for t in tensors]

    def body(*refs):
        for cp in _exchange_copies(refs[:n], refs[n:2 * n], refs[2 * n], refs[2 * n + 1], gather):
            cp.start()
        refs[-1][...] = jnp.zeros_like(refs[-1])

    buffers = [pltpu.HBM(a.shape, a.dtype) for a in tensors + zones]
    res = pl.pallas_call(
        body, name=name, in_specs=[_HBM] * (2 * n),
        out_shape=(pltpu.SemaphoreType.DMA((7 * n,)), pltpu.SemaphoreType.DMA((7 * n,)), *buffers, jax.ShapeDtypeStruct((8, LANES), F32)),
        out_specs=(_SEM, _SEM, *[_HBM] * (2 * n), pl.BlockSpec(memory_space=pltpu.VMEM)),
        input_output_aliases={i: 2 + i for i in range(2 * n)},
        compiler_params=pltpu.CompilerParams(has_side_effects=pltpu.SideEffectType.DATAFLOW_SIDE_EFFECTING),
    )(*[pltpu.with_memory_space_constraint(a, pltpu.HBM) for a in tensors + zones])
    return dict(sems=res[:2], buffers=res[2:2 + 2 * n], gather=gather, started=res[-1])


def _exchange_wait(started, after, *, name):
    n = len(started["buffers"]) // 2

    def body(*refs):
        for cp in _exchange_copies(refs[:n], refs[n:2 * n], refs[2 * n], refs[2 * n + 1], started["gather"]):
            cp.wait_send()
            cp.wait_recv()

    res = pl.pallas_call(
        body, name=name, in_specs=[_HBM] * (2 * n) + [_SEM, _SEM] + [_ANY] * len(after),
        out_shape=tuple(pltpu.HBM(a.shape, a.dtype) for a in started["buffers"]), out_specs=tuple([_HBM] * (2 * n)),
        input_output_aliases={i: i for i in range(2 * n)},
        compiler_params=pltpu.CompilerParams(has_side_effects=pltpu.SideEffectType.DATAFLOW_SIDE_EFFECTING),
    )(*started["buffers"], *started["sems"], *after)
    return res[:n], res[n:]


def _adamw(contrib, w, m, v, *, name):
    rows, cols = w.shape
    tile = min(rows, ADAM_ROWS)

    def body(c_ref, w_ref, m_ref, v_ref, g_ref, d_ref, nm_ref, nv_ref):
        g = c_ref[0].astype(F32)
        for s in range(1, N_DEV):
            g = g + c_ref[s].astype(F32)
        m_new = ADAM_B1 * m_ref[...] + (1.0 - ADAM_B1) * g
        v_new = ADAM_B2 * v_ref[...] + (1.0 - ADAM_B2) * (g * g)
        m_hat = m_new / (1.0 - ADAM_B1 ** ADAM_STEP)
        v_hat = v_new / (1.0 - ADAM_B2 ** ADAM_STEP)
        g_ref[...] = g
        d_ref[...] = -ADAM_LR * (m_hat / (jnp.sqrt(v_hat) + ADAM_EPS) + ADAM_WD * w_ref[...])
        nm_ref[...] = m_new
        nv_ref[...] = v_new

    spec = pl.BlockSpec((tile, cols), lambda i: (i, 0))
    return pl.pallas_call(
        body, name=name, grid=(rows // tile,),
        in_specs=[pl.BlockSpec((N_DEV, tile, cols), lambda i: (0, i, 0)), spec, spec, spec], out_specs=[spec] * 4,
        out_shape=[jax.ShapeDtypeStruct((rows, cols), F32)] * 4, compiler_params=_cparams("parallel"),
    )(contrib, w, m, v)


class _Weights:
    def __init__(self, gathers, vectors, me):
        self.gathers, self.ready, self.me, self.after = gathers, dict(vectors), me, ()

    def arrive_after(self, *values):
        self.after = values

    def __getitem__(self, name):
        if name not in self.ready:
            gi = next(i for i, group in enumerate(GATHER_GROUPS) if name in group)
            after = [*self.after, *[g["started"] for g in self.gathers]]
            shards, zones = _exchange_wait(self.gathers[gi], after, name=f"gather_wait_{gi}")
            for n, shard, zone in zip(GATHER_GROUPS[gi], shards, zones, strict=True):
                blocks = lax.dynamic_update_slice_in_dim(zone, shard[None], self.me, 0)
                self.ready[n] = _to_kernel_layout(n, _from_blocks(blocks, n))
        return self.ready[name]


def kernel(x, mem, w_in, w_mem_kv, q_a_gain, w_q_b, kv_a_gain, w_kv_b, w_branch_mla, w_branch_sb, w_branch_mem, w_merge_gate, b_merge_gate, w_out, ln_gain, ln_bias, loss_target, m_w_in, m_w_mem_kv, m_q_a_gain, m_w_q_b, m_kv_a_gain, m_w_kv_b, m_w_branch_mla, m_w_branch_sb, m_w_branch_mem, m_w_merge_gate, m_b_merge_gate, m_w_out, m_ln_gain, m_ln_bias, v_w_in, v_w_mem_kv, v_q_a_gain, v_w_q_b, v_kv_a_gain, v_w_kv_b, v_w_branch_mla, v_w_branch_sb, v_w_branch_mem, v_w_merge_gate, v_b_merge_gate, v_w_out, v_ln_gain, v_ln_bias):
    given = dict(locals())
    small_names = [n for n, _ in SMALL]
    smalls = lambda prefix: [given[prefix + n] for n in small_names]
    me = 4 * lax.axis_index("x") + 2 * lax.axis_index("y") + lax.axis_index("c")

    gathers = [_exchange_start([given[n][0].astype(BF16) for n in group], gather=True, name=f"gather_start_{gi}")
               for gi, group in enumerate(GATHER_GROUPS)]
    w = _Weights(gathers, {n: given[n] for n in small_names}, me)
    exchanges = []
    results = [{}, {}, {}, {}]

    def finish(gi, after):
        names, started = exchanges[gi]
        sent, zones = _exchange_wait(started, after, name=f"grads_wait_{gi}")
        done = []
        for n, blocks, zone in zip(names, sent, zones, strict=True):
            own = lax.dynamic_index_in_dim(blocks, me, 0, keepdims=True)
            contrib = lax.dynamic_update_slice_in_dim(zone, own, me, 0)
            outs = _adamw(contrib, given[n][0], given["m_" + n][0], given["v_" + n][0], name=f"adamw_{n}")
            for kind, res in zip(results, outs, strict=True):
                kind[n] = res[None]
            done.append(outs[1])
        return done

    def emit(grads):
        blocks = [_to_blocks(_from_kernel_layout(n, g), n).astype(BF16) for n, g in grads.items()]
        exchanges.append((tuple(grads), _exchange_start(blocks, gather=False, name=f"grads_start_{len(exchanges)}")))
        started = [exchanges[-1][1]["started"]]
        if len(exchanges) == len(GRAD_GROUPS):
            for gi in range(len(GRAD_GROUPS) - 1):
                started += finish(gi, started[:1])
        return started

    loss, grad_x, grads = _local_step(x[0], mem[0], loss_target[0], w, emit)

    contrib_small = _share_small(_pack_small([grads[n] for n in small_names], loss), name="share_small")
    sml = _adamw(contrib_small, _pack_small(smalls("")), _pack_small(smalls("m_")), _pack_small(smalls("v_")), name="adamw_small")
    for kind, packed in zip(results, sml, strict=True):
        kind.update(zip(small_names, _unpack_small(packed), strict=True))
    finish(len(GRAD_GROUPS) - 1, [grad_x])
    order = ["w_in", "w_mem_kv", "q_a_gain", "w_q_b", "kv_a_gain", "w_kv_b", "w_branch_mla", "w_branch_sb", "w_branch_mem",
             "w_merge_gate", "b_merge_gate", "w_out", "ln_gain", "ln_bias"]
    loss_out = sml[0].reshape(-1)[LOSS_INDEX]
    return (loss_out, grad_x[None], *[kind[n] for kind in results for n in order])
```
